```python
import math
import jax, jax.numpy as jnp
from jax import lax
import numpy as np

D_MODEL = 1024
BATCH = 32
SEQ = 2048
DEPTH = 1

MEM_LEN = 256
EPS = 1e-6
D_FF = 2816
FFN_RES_WEIGHT = 0.5
CONV_A_WIDTH = D_MODEL
CONV_A_K = 3
SSM_D_INNER = 2 * D_MODEL
SSM_HEAD_DIM = 64
SSM_HEADS = SSM_D_INNER // SSM_HEAD_DIM
SSM_GROUPS = 4
SSM_STATE = 128
SSM_CONV_K = 4
SSM_CHUNK = 128
SSM_CONV_CH = SSM_D_INNER + 2 * SSM_GROUPS * SSM_STATE
XATTN_HEADS = 4
XATTN_HEAD_DIM = D_MODEL // XATTN_HEADS
XATTN_SCALE = 1.0 / math.sqrt(XATTN_HEAD_DIM)
IN_SIZES = (CONV_A_WIDTH, CONV_A_WIDTH, CONV_A_WIDTH,
            SSM_D_INNER, SSM_CONV_CH, SSM_HEADS,
            D_MODEL, D_MODEL)
D_IN_PROJ = sum(IN_SIZES)
IN_SPLITS = tuple(int(v) for v in np.cumsum(IN_SIZES)[:-1])

kernel_name = "hybrid_shortconv_ssd_gated_macaron"


def rmsnorm(x, g):
    xf = x.astype(jnp.float32)
    y = xf * lax.rsqrt(jnp.mean(xf * xf, axis=-1, keepdims=True) + EPS)
    return (y * g.astype(jnp.float32)).astype(x.dtype)


def swiglu(u, w_gate_up, w_down):
    gate, up = jnp.split(u @ w_gate_up, 2, axis=-1)
    return (jax.nn.silu(gate) * up) @ w_down


def causal_dwconv(x, w):
    k, c = w.shape
    return lax.conv_general_dilated(
        x, w[:, None, :].astype(x.dtype), window_strides=(1,),
        padding=[(k - 1, 0)], dimension_numbers=('NWC', 'WIO', 'NWC'),
        feature_group_count=c)


def short_conv_branch(b_gate, c_gate, v, conv_w, w_out):
    return (b_gate * causal_dwconv(c_gate * v, conv_w)) @ w_out


def ssd_chunked(xh, dt, a, bm, cm):
    b, s, h, p = xh.shape
    g, n = bm.shape[-2:]
    k = h // g
    l = SSM_CHUNK
    c = s // l
    x = (xh.astype(jnp.float32) * dt[..., None]).reshape(b, c, l, g, k, p)
    la = (dt * a).reshape(b, c, l, g, k).transpose(0, 1, 3, 4, 2)
    acs = jnp.cumsum(la, axis=-1)
    bc = bm.astype(jnp.float32).reshape(b, c, l, g, n)
    cc = cm.astype(jnp.float32).reshape(b, c, l, g, n)
    seg = acs[..., :, None] - acs[..., None, :]
    causal = jnp.tril(jnp.ones((l, l), dtype=bool))
    decay = jnp.exp(jnp.where(causal, seg, -jnp.inf))
    cb = jnp.einsum('bclgn,bcsgn->bcgls', cc, bc)
    y_diag = jnp.einsum('bcgls,bcgkls,bcsgkp->bclgkp', cb, decay, x)
    decay_to_end = jnp.exp(acs[..., -1:] - acs)
    states = jnp.einsum('bclgn,bcgkl,bclgkp->bcgkpn', bc, decay_to_end, x)
    chunk_decay = jnp.exp(acs[..., -1])

    def step(carry, inp):
        st, dec = inp
        return carry * dec[..., None, None] + st, carry

    init = jnp.zeros((b, g, k, p, n), jnp.float32)
    _, prev = lax.scan(step, init, (jnp.moveaxis(states, 1, 0), jnp.moveaxis(chunk_decay, 1, 0)))
    prev = jnp.moveaxis(prev, 0, 1)
    y_off = jnp.einsum('bclgn,bcgkpn,bcgkl->bclgkp', cc, prev, jnp.exp(acs))
    return (y_diag + y_off).reshape(b, s, h, p)


def mamba2_branch(z, xbc, dt_raw, conv_w, conv_b, dt_bias, a_log, d_skip, norm_g, w_out):
    xbc = jax.nn.silu(causal_dwconv(xbc, conv_w) + conv_b.astype(xbc.dtype))
    xs, bm, cm = jnp.split(xbc, (SSM_D_INNER, SSM_D_INNER + SSM_GROUPS * SSM_STATE), axis=-1)
    b, s, _ = xs.shape
    xh = xs.reshape(b, s, SSM_HEADS, SSM_HEAD_DIM)
    dt = jax.nn.softplus(dt_raw.astype(jnp.float32) + dt_bias.astype(jnp.float32))
    a = -jnp.exp(a_log.astype(jnp.float32))
    y = ssd_chunked(xh, dt, a,
                    bm.reshape(b, s, SSM_GROUPS, SSM_STATE),
                    cm.reshape(b, s, SSM_GROUPS, SSM_STATE))
    y = y + d_skip.astype(jnp.float32)[:, None] * xh.astype(jnp.float32)
    yg = (y.reshape(b, s, SSM_D_INNER) * jax.nn.silu(z.astype(jnp.float32)))
    yg = yg.reshape(b, s, SSM_GROUPS, SSM_D_INNER // SSM_GROUPS)
    yg = yg * lax.rsqrt(jnp.mean(yg * yg, axis=-1, keepdims=True) + EPS)
    y = (yg.reshape(b, s, SSM_D_INNER) * norm_g.astype(jnp.float32)).astype(z.dtype)
    return y @ w_out


def memory_cross_attention(u, mem_n, w_q, w_kv, w_o):
    b, s, _ = u.shape
    m = mem_n.shape[1]
    q = (u @ w_q).reshape(b, s, XATTN_HEADS, XATTN_HEAD_DIM)
    k, v = jnp.split(mem_n @ w_kv, 2, axis=-1)
    k = k.reshape(b, m, XATTN_HEADS, XATTN_HEAD_DIM)
    v = v.reshape(b, m, XATTN_HEADS, XATTN_HEAD_DIM)
    scores = jnp.einsum('bshd,bmhd->bhsm', q, k).astype(jnp.float32) * XATTN_SCALE
    probs = jax.nn.softmax(scores, axis=-1).astype(v.dtype)
    o = jnp.einsum('bhsm,bmhd->bshd', probs, v).reshape(b, s, D_MODEL)
    return o @ w_o


def _fwd_setup_inputs(seed: int = 0) -> dict:
    key = jax.random.key(seed)
    ks = iter(jax.random.split(key, 40))

    def w(shape, fan_in):
        return jax.random.normal(next(ks), shape, jnp.float32) * (fan_in ** -0.5)

    def gain(shape):
        return 1.0 + 0.02 * jax.random.normal(next(ks), shape, jnp.float32)

    L = DEPTH
    x = jax.random.normal(next(ks), (BATCH, SEQ, D_MODEL), jnp.float32)
    mem = jax.random.normal(next(ks), (BATCH, MEM_LEN, D_MODEL), jnp.float32)
    dt0 = jnp.exp(jax.random.uniform(next(ks), (L, SSM_HEADS), jnp.float32,
                                     math.log(1e-3), math.log(1e-1)))
    dt_bias = dt0 + jnp.log(-jnp.expm1(-dt0))
    a_log = jnp.log(jax.random.uniform(next(ks), (L, SSM_HEADS), jnp.float32, 1.0, 16.0))
    return {
        "x": x,
        "mem": mem,
        "ffn1_norm": gain((L, D_MODEL)),
        "ffn1_w_gate_up": w((L, D_MODEL, 2 * D_FF), D_MODEL),
        "ffn1_w_down": w((L, D_FF, D_MODEL), D_FF),
        "mix_norm": gain((L, D_MODEL)),
        "w_in": w((L, D_MODEL, D_IN_PROJ), D_MODEL),
        "conv_a_w": w((L, CONV_A_K, CONV_A_WIDTH), CONV_A_K),
        "w_out_a": w((L, CONV_A_WIDTH, D_MODEL), CONV_A_WIDTH),
        "ssm_conv_w": w((L, SSM_CONV_K, SSM_CONV_CH), SSM_CONV_K),
        "ssm_conv_b": 0.02 * jax.random.normal(next(ks), (L, SSM_CONV_CH), jnp.float32),
        "ssm_dt_bias": dt_bias,
        "ssm_a_log": a_log,
        "ssm_d": gain((L, SSM_HEADS)),
        "ssm_norm": gain((L, SSM_D_INNER)),
        "w_out_ssm": w((L, SSM_D_INNER, D_MODEL), SSM_D_INNER),
        "w_mix_out": w((L, D_MODEL, D_MODEL), D_MODEL),
        "xattn_norm": gain((L, D_MODEL)),
        "mem_norm": gain((L, D_MODEL)),
        "w_q": w((L, D_MODEL, D_MODEL), D_MODEL),
        "w_kv": w((L, D_MODEL, 2 * D_MODEL), D_MODEL),
        "w_o_x": w((L, D_MODEL, D_MODEL), D_MODEL),
        "ffn2_norm": gain((L, D_MODEL)),
        "ffn2_w_gate_up": w((L, D_MODEL, 2 * D_FF), D_MODEL),
        "ffn2_w_down": w((L, D_FF, D_MODEL), D_FF),
        "final_norm": gain((D_MODEL,)),
    }


def _fwd_reference(x, mem, ffn1_norm, ffn1_w_gate_up, ffn1_w_down, mix_norm, w_in, conv_a_w,
              w_out_a, ssm_conv_w, ssm_conv_b, ssm_dt_bias, ssm_a_log, ssm_d, ssm_norm,
              w_out_ssm, w_mix_out, xattn_norm, mem_norm, w_q, w_kv, w_o_x,
              ffn2_norm, ffn2_w_gate_up, ffn2_w_down, final_norm):
    h = x
    for i in range(DEPTH):
        h = h + FFN_RES_WEIGHT * swiglu(rmsnorm(h, ffn1_norm[i]), ffn1_w_gate_up[i], ffn1_w_down[i])
        u = rmsnorm(h, mix_norm[i])
        proj = u @ w_in[i]
        a_b, a_c, a_v, z, xbc, dt_raw, g_a, g_b = jnp.split(proj, IN_SPLITS, axis=-1)
        y_a = short_conv_branch(a_b, a_c, a_v, conv_a_w[i], w_out_a[i])
        y_b = mamba2_branch(z, xbc, dt_raw, ssm_conv_w[i], ssm_conv_b[i], ssm_dt_bias[i],
                            ssm_a_log[i], ssm_d[i], ssm_norm[i], w_out_ssm[i])
        merged = jax.nn.sigmoid(g_a) * y_a + jax.nn.sigmoid(g_b) * y_b
        h = h + merged @ w_mix_out[i]
        h = h + memory_cross_attention(rmsnorm(h, xattn_norm[i]), rmsnorm(mem, mem_norm[i]),
                                       w_q[i], w_kv[i], w_o_x[i])
        h = h + FFN_RES_WEIGHT * swiglu(rmsnorm(h, ffn2_norm[i]), ffn2_w_gate_up[i], ffn2_w_down[i])
    return rmsnorm(h, final_norm)


import jax as _jax
import jax.numpy as _jnp

TWIN_FORMAT = 'train_step'
FWD_PARAMS = ['x', 'mem', 'ffn1_norm', 'ffn1_w_gate_up', 'ffn1_w_down', 'mix_norm', 'w_in', 'conv_a_w', 'w_out_a', 'ssm_conv_w', 'ssm_conv_b', 'ssm_dt_bias', 'ssm_a_log', 'ssm_d', 'ssm_norm', 'w_out_ssm', 'w_mix_out', 'xattn_norm', 'mem_norm', 'w_q', 'w_kv', 'w_o_x', 'ffn2_norm', 'ffn2_w_gate_up', 'ffn2_w_down', 'final_norm']
TWIN_WEIGHTS = ['ffn1_norm', 'ffn1_w_gate_up', 'ffn1_w_down', 'mix_norm', 'w_in', 'conv_a_w', 'w_out_a', 'ssm_conv_w', 'ssm_conv_b', 'ssm_dt_bias', 'ssm_a_log', 'ssm_d', 'ssm_norm', 'w_out_ssm', 'w_mix_out', 'xattn_norm', 'mem_norm', 'w_q', 'w_kv', 'w_o_x', 'ffn2_norm', 'ffn2_w_gate_up', 'ffn2_w_down', 'final_norm']
TWIN_DIFF_INPUT = 'x'
TWIN_INPUTS = ['x', 'mem', 'ffn1_norm', 'ffn1_w_gate_up', 'ffn1_w_down', 'mix_norm', 'w_in', 'conv_a_w', 'w_out_a', 'ssm_conv_w', 'ssm_conv_b', 'ssm_dt_bias', 'ssm_a_log', 'ssm_d', 'ssm_norm', 'w_out_ssm', 'w_mix_out', 'xattn_norm', 'mem_norm', 'w_q', 'w_kv', 'w_o_x', 'ffn2_norm', 'ffn2_w_gate_up', 'ffn2_w_down', 'final_norm', 'loss_target', 'm_ffn1_norm', 'm_ffn1_w_gate_up', 'm_ffn1_w_down', 'm_mix_norm', 'm_w_in', 'm_conv_a_w', 'm_w_out_a', 'm_ssm_conv_w', 'm_ssm_conv_b', 'm_ssm_dt_bias', 'm_ssm_a_log', 'm_ssm_d', 'm_ssm_norm', 'm_w_out_ssm', 'm_w_mix_out', 'm_xattn_norm', 'm_mem_norm', 'm_w_q', 'm_w_kv', 'm_w_o_x', 'm_ffn2_norm', 'm_ffn2_w_gate_up', 'm_ffn2_w_down', 'm_final_norm', 'v_ffn1_norm', 'v_ffn1_w_gate_up', 'v_ffn1_w_down', 'v_mix_norm', 'v_w_in', 'v_conv_a_w', 'v_w_out_a', 'v_ssm_conv_w', 'v_ssm_conv_b', 'v_ssm_dt_bias', 'v_ssm_a_log', 'v_ssm_d', 'v_ssm_norm', 'v_w_out_ssm', 'v_w_mix_out', 'v_xattn_norm', 'v_mem_norm', 'v_w_q', 'v_w_kv', 'v_w_o_x', 'v_ffn2_norm', 'v_ffn2_w_gate_up', 'v_ffn2_w_down', 'v_final_norm']
TWIN_OUTPUTS = ['loss', 'grad_x', 'grad_ffn1_norm', 'grad_ffn1_w_gate_up', 'grad_ffn1_w_down', 'grad_mix_norm', 'grad_w_in', 'grad_conv_a_w', 'grad_w_out_a', 'grad_ssm_conv_w', 'grad_ssm_conv_b', 'grad_ssm_dt_bias', 'grad_ssm_a_log', 'grad_ssm_d', 'grad_ssm_norm', 'grad_w_out_ssm', 'grad_w_mix_out', 'grad_xattn_norm', 'grad_mem_norm', 'grad_w_q', 'grad_w_kv', 'grad_w_o_x', 'grad_ffn2_norm', 'grad_ffn2_w_gate_up', 'grad_ffn2_w_down', 'grad_final_norm', 'delta_ffn1_norm', 'delta_ffn1_w_gate_up', 'delta_ffn1_w_down', 'delta_mix_norm', 'delta_w_in', 'delta_conv_a_w', 'delta_w_out_a', 'delta_ssm_conv_w', 'delta_ssm_conv_b', 'delta_ssm_dt_bias', 'delta_ssm_a_log', 'delta_ssm_d', 'delta_ssm_norm', 'delta_w_out_ssm', 'delta_w_mix_out', 'delta_xattn_norm', 'delta_mem_norm', 'delta_w_q', 'delta_w_kv', 'delta_w_o_x', 'delta_ffn2_norm', 'delta_ffn2_w_gate_up', 'delta_ffn2_w_down', 'delta_final_norm', 'new_m_ffn1_norm', 'new_m_ffn1_w_gate_up', 'new_m_ffn1_w_down', 'new_m_mix_norm', 'new_m_w_in', 'new_m_conv_a_w', 'new_m_w_out_a', 'new_m_ssm_conv_w', 'new_m_ssm_conv_b', 'new_m_ssm_dt_bias', 'new_m_ssm_a_log', 'new_m_ssm_d', 'new_m_ssm_norm', 'new_m_w_out_ssm', 'new_m_w_mix_out', 'new_m_xattn_norm', 'new_m_mem_norm', 'new_m_w_q', 'new_m_w_kv', 'new_m_w_o_x', 'new_m_ffn2_norm', 'new_m_ffn2_w_gate_up', 'new_m_ffn2_w_down', 'new_m_final_norm', 'new_v_ffn1_norm', 'new_v_ffn1_w_gate_up', 'new_v_ffn1_w_down', 'new_v_mix_norm', 'new_v_w_in', 'new_v_conv_a_w', 'new_v_w_out_a', 'new_v_ssm_conv_w', 'new_v_ssm_conv_b', 'new_v_ssm_dt_bias', 'new_v_ssm_a_log', 'new_v_ssm_d', 'new_v_ssm_norm', 'new_v_w_out_ssm', 'new_v_w_mix_out', 'new_v_xattn_norm', 'new_v_mem_norm', 'new_v_w_q', 'new_v_w_kv', 'new_v_w_o_x', 'new_v_ffn2_norm', 'new_v_ffn2_w_gate_up', 'new_v_ffn2_w_down', 'new_v_final_norm']
TWIN_LEAF_KINDS = {'loss': 'loss', 'grad_x': 'grad_x', 'grad_ffn1_norm': 'grad_w', 'grad_ffn1_w_gate_up': 'grad_w', 'grad_ffn1_w_down': 'grad_w', 'grad_mix_norm': 'grad_w', 'grad_w_in': 'grad_w', 'grad_conv_a_w': 'grad_w', 'grad_w_out_a': 'grad_w', 'grad_ssm_conv_w': 'grad_w', 'grad_ssm_conv_b': 'grad_w', 'grad_ssm_dt_bias': 'grad_w', 'grad_ssm_a_log': 'grad_w', 'grad_ssm_d': 'grad_w', 'grad_ssm_norm': 'grad_w', 'grad_w_out_ssm': 'grad_w', 'grad_w_mix_out': 'grad_w', 'grad_xattn_norm': 'grad_w', 'grad_mem_norm': 'grad_w', 'grad_w_q': 'grad_w', 'grad_w_kv': 'grad_w', 'grad_w_o_x': 'grad_w', 'grad_ffn2_norm': 'grad_w', 'grad_ffn2_w_gate_up': 'grad_w', 'grad_ffn2_w_down': 'grad_w', 'grad_final_norm': 'grad_w', 'delta_ffn1_norm': 'delta_w', 'delta_ffn1_w_gate_up': 'delta_w', 'delta_ffn1_w_down': 'delta_w', 'delta_mix_norm': 'delta_w', 'delta_w_in': 'delta_w', 'delta_conv_a_w': 'delta_w', 'delta_w_out_a': 'delta_w', 'delta_ssm_conv_w': 'delta_w', 'delta_ssm_conv_b': 'delta_w', 'delta_ssm_dt_bias': 'delta_w', 'delta_ssm_a_log': 'delta_w', 'delta_ssm_d': 'delta_w', 'delta_ssm_norm': 'delta_w', 'delta_w_out_ssm': 'delta_w', 'delta_w_mix_out': 'delta_w', 'delta_xattn_norm': 'delta_w', 'delta_mem_norm': 'delta_w', 'delta_w_q': 'delta_w', 'delta_w_kv': 'delta_w', 'delta_w_o_x': 'delta_w', 'delta_ffn2_norm': 'delta_w', 'delta_ffn2_w_gate_up': 'delta_w', 'delta_ffn2_w_down': 'delta_w', 'delta_final_norm': 'delta_w', 'new_m_ffn1_norm': 'new_m', 'new_m_ffn1_w_gate_up': 'new_m', 'new_m_ffn1_w_down': 'new_m', 'new_m_mix_norm': 'new_m', 'new_m_w_in': 'new_m', 'new_m_conv_a_w': 'new_m', 'new_m_w_out_a': 'new_m', 'new_m_ssm_conv_w': 'new_m', 'new_m_ssm_conv_b': 'new_m', 'new_m_ssm_dt_bias': 'new_m', 'new_m_ssm_a_log': 'new_m', 'new_m_ssm_d': 'new_m', 'new_m_ssm_norm': 'new_m', 'new_m_w_out_ssm': 'new_m', 'new_m_w_mix_out': 'new_m', 'new_m_xattn_norm': 'new_m', 'new_m_mem_norm': 'new_m', 'new_m_w_q': 'new_m', 'new_m_w_kv': 'new_m', 'new_m_w_o_x': 'new_m', 'new_m_ffn2_norm': 'new_m', 'new_m_ffn2_w_gate_up': 'new_m', 'new_m_ffn2_w_down': 'new_m', 'new_m_final_norm': 'new_m', 'new_v_ffn1_norm': 'new_v', 'new_v_ffn1_w_gate_up': 'new_v', 'new_v_ffn1_w_down': 'new_v', 'new_v_mix_norm': 'new_v', 'new_v_w_in': 'new_v', 'new_v_conv_a_w': 'new_v', 'new_v_w_out_a': 'new_v', 'new_v_ssm_conv_w': 'new_v', 'new_v_ssm_conv_b': 'new_v', 'new_v_ssm_dt_bias': 'new_v', 'new_v_ssm_a_log': 'new_v', 'new_v_ssm_d': 'new_v', 'new_v_ssm_norm': 'new_v', 'new_v_w_out_ssm': 'new_v', 'new_v_w_mix_out': 'new_v', 'new_v_xattn_norm': 'new_v', 'new_v_mem_norm': 'new_v', 'new_v_w_q': 'new_v', 'new_v_w_kv': 'new_v', 'new_v_w_o_x': 'new_v', 'new_v_ffn2_norm': 'new_v', 'new_v_ffn2_w_gate_up': 'new_v', 'new_v_ffn2_w_down': 'new_v', 'new_v_final_norm': 'new_v'}


def _forward(args):
    return _fwd_reference(*[args[k] for k in FWD_PARAMS])


def _output_shape():
    out = _jax.eval_shape(lambda: _forward(_fwd_setup_inputs(0)))
    return out.shape, out.dtype

N_MICROBATCH = 1
ADAM_LR = 0.001
ADAM_B1 = 0.9
ADAM_B2 = 0.999
ADAM_EPS = 1e-08
ADAM_WD = 0.01
ADAM_STEP = 10
PER_EXAMPLE_BATCH_AXIS = {'x': 0, 'mem': 0, 'loss_target': 0}
SHARED_INPUTS = []
_WEIGHT_DTYPES = {'ffn1_norm': _jnp.float32, 'ffn1_w_gate_up': _jnp.float32, 'ffn1_w_down': _jnp.float32, 'mix_norm': _jnp.float32, 'w_in': _jnp.float32, 'conv_a_w': _jnp.float32, 'w_out_a': _jnp.float32, 'ssm_conv_w': _jnp.float32, 'ssm_conv_b': _jnp.float32, 'ssm_dt_bias': _jnp.float32, 'ssm_a_log': _jnp.float32, 'ssm_d': _jnp.float32, 'ssm_norm': _jnp.float32, 'w_out_ssm': _jnp.float32, 'w_mix_out': _jnp.float32, 'xattn_norm': _jnp.float32, 'mem_norm': _jnp.float32, 'w_q': _jnp.float32, 'w_kv': _jnp.float32, 'w_o_x': _jnp.float32, 'ffn2_norm': _jnp.float32, 'ffn2_w_gate_up': _jnp.float32, 'ffn2_w_down': _jnp.float32, 'final_norm': _jnp.float32}
MOMENT_SCALE = {'ffn1_norm': 1.320900e-01, 'ffn1_w_gate_up': 5.715808e-02, 'ffn1_w_down': 9.323039e-02, 'mix_norm': 2.589943e-01, 'w_in': 7.975669e-02, 'conv_a_w': 1.105252e-01, 'w_out_a': 1.079686e-01, 'ssm_conv_w': 6.594381e-02, 'ssm_conv_b': 9.841822e-02, 'ssm_dt_bias': 1.619875e-01, 'ssm_a_log': 3.155823e-01, 'ssm_d': 3.639525e-01, 'ssm_norm': 8.492871e-02, 'w_out_ssm': 1.077555e-01, 'w_mix_out': 1.525200e-01, 'xattn_norm': 2.247974e-02, 'mem_norm': 3.237157e-02, 'w_q': 2.081131e-02, 'w_kv': 2.095963e-02, 'w_o_x': 2.102550e-02, 'ffn2_norm': 8.304356e-02, 'ffn2_w_gate_up': 3.447349e-02, 'ffn2_w_down': 5.640820e-02, 'final_norm': 6.388493e+01}


def _to_microbatches(a, axis):
    t = _jnp.moveaxis(a, axis, 0)
    t = t.reshape((N_MICROBATCH, t.shape[0] // N_MICROBATCH) + t.shape[1:])
    return _jnp.moveaxis(t, 1, axis + 1)


def setup_inputs(seed: int = 0) -> dict:
    inp = _fwd_setup_inputs(seed)
    key = _jax.random.fold_in(_jax.random.key(seed), 7919)
    shape, _ = _output_shape()
    out = dict(inp)
    out["loss_target"] = _jax.random.normal(_jax.random.fold_in(key, 0), shape, _jnp.float32)
    for i, name in enumerate(TWIN_WEIGHTS):
        w = inp[name].astype(_jnp.float32)
        if MOMENT_SCALE is None:
            s = _jnp.sqrt(_jnp.mean(_jnp.square(w)) + 1e-30)
        else:
            s = MOMENT_SCALE[name]
        km, kv = _jax.random.split(_jax.random.fold_in(key, i + 1))
        out[name] = w
        out["m_" + name] = s * _jax.random.normal(km, w.shape, _jnp.float32)
        out["v_" + name] = (s * s) * _jax.random.uniform(kv, w.shape, _jnp.float32, 0.5, 1.5)
    if N_MICROBATCH > 1:
        for name, axis in PER_EXAMPLE_BATCH_AXIS.items():
            out[name] = _to_microbatches(out[name], axis)
    return {'x': out['x'], 'mem': out['mem'], 'ffn1_norm': out['ffn1_norm'], 'ffn1_w_gate_up': out['ffn1_w_gate_up'], 'ffn1_w_down': out['ffn1_w_down'], 'mix_norm': out['mix_norm'], 'w_in': out['w_in'], 'conv_a_w': out['conv_a_w'], 'w_out_a': out['w_out_a'], 'ssm_conv_w': out['ssm_conv_w'], 'ssm_conv_b': out['ssm_conv_b'], 'ssm_dt_bias': out['ssm_dt_bias'], 'ssm_a_log': out['ssm_a_log'], 'ssm_d': out['ssm_d'], 'ssm_norm': out['ssm_norm'], 'w_out_ssm': out['w_out_ssm'], 'w_mix_out': out['w_mix_out'], 'xattn_norm': out['xattn_norm'], 'mem_norm': out['mem_norm'], 'w_q': out['w_q'], 'w_kv': out['w_kv'], 'w_o_x': out['w_o_x'], 'ffn2_norm': out['ffn2_norm'], 'ffn2_w_gate_up': out['ffn2_w_gate_up'], 'ffn2_w_down': out['ffn2_w_down'], 'final_norm': out['final_norm'], 'loss_target': out['loss_target'], 'm_ffn1_norm': out['m_ffn1_norm'], 'm_ffn1_w_gate_up': out['m_ffn1_w_gate_up'], 'm_ffn1_w_down': out['m_ffn1_w_down'], 'm_mix_norm': out['m_mix_norm'], 'm_w_in': out['m_w_in'], 'm_conv_a_w': out['m_conv_a_w'], 'm_w_out_a': out['m_w_out_a'], 'm_ssm_conv_w': out['m_ssm_conv_w'], 'm_ssm_conv_b': out['m_ssm_conv_b'], 'm_ssm_dt_bias': out['m_ssm_dt_bias'], 'm_ssm_a_log': out['m_ssm_a_log'], 'm_ssm_d': out['m_ssm_d'], 'm_ssm_norm': out['m_ssm_norm'], 'm_w_out_ssm': out['m_w_out_ssm'], 'm_w_mix_out': out['m_w_mix_out'], 'm_xattn_norm': out['m_xattn_norm'], 'm_mem_norm': out['m_mem_norm'], 'm_w_q': out['m_w_q'], 'm_w_kv': out['m_w_kv'], 'm_w_o_x': out['m_w_o_x'], 'm_ffn2_norm': out['m_ffn2_norm'], 'm_ffn2_w_gate_up': out['m_ffn2_w_gate_up'], 'm_ffn2_w_down': out['m_ffn2_w_down'], 'm_final_norm': out['m_final_norm'], 'v_ffn1_norm': out['v_ffn1_norm'], 'v_ffn1_w_gate_up': out['v_ffn1_w_gate_up'], 'v_ffn1_w_down': out['v_ffn1_w_down'], 'v_mix_norm': out['v_mix_norm'], 'v_w_in': out['v_w_in'], 'v_conv_a_w': out['v_conv_a_w'], 'v_w_out_a': out['v_w_out_a'], 'v_ssm_conv_w': out['v_ssm_conv_w'], 'v_ssm_conv_b': out['v_ssm_conv_b'], 'v_ssm_dt_bias': out['v_ssm_dt_bias'], 'v_ssm_a_log': out['v_ssm_a_log'], 'v_ssm_d': out['v_ssm_d'], 'v_ssm_norm': out['v_ssm_norm'], 'v_w_out_ssm': out['v_w_out_ssm'], 'v_w_mix_out': out['v_w_mix_out'], 'v_xattn_norm': out['v_xattn_norm'], 'v_mem_norm': out['v_mem_norm'], 'v_w_q': out['v_w_q'], 'v_w_kv': out['v_w_kv'], 'v_w_o_x': out['v_w_o_x'], 'v_ffn2_norm': out['v_ffn2_norm'], 'v_ffn2_w_gate_up': out['v_ffn2_w_gate_up'], 'v_ffn2_w_down': out['v_ffn2_w_down'], 'v_final_norm': out['v_final_norm']}


def _loss(weights, diff, rest, loss_target):
    with _jax.named_scope("forward"):
        args = {**rest, TWIN_DIFF_INPUT: diff, **{k: w.astype(_WEIGHT_DTYPES[k]) for k, w in weights.items()}}
        y = _forward(args)
    with _jax.named_scope("loss_head"):
        err = _jnp.square(y.astype(_jnp.float32) - loss_target)
        return 0.5 * _jnp.sum(_jnp.mean(err, axis=-1)) if err.ndim else 0.5 * err


def _adamw(w, g, m, v):
    m = ADAM_B1 * m + (1.0 - ADAM_B1) * g
    v = ADAM_B2 * v + (1.0 - ADAM_B2) * _jnp.square(g)
    m_hat = m / (1.0 - ADAM_B1 ** ADAM_STEP)
    v_hat = v / (1.0 - ADAM_B2 ** ADAM_STEP)
    delta = -ADAM_LR * (m_hat / (_jnp.sqrt(v_hat) + ADAM_EPS) + ADAM_WD * w)
    return delta, m, v


def reference(x, mem, ffn1_norm, ffn1_w_gate_up, ffn1_w_down, mix_norm, w_in, conv_a_w, w_out_a, ssm_conv_w, ssm_conv_b, ssm_dt_bias, ssm_a_log, ssm_d, ssm_norm, w_out_ssm, w_mix_out, xattn_norm, mem_norm, w_q, w_kv, w_o_x, ffn2_norm, ffn2_w_gate_up, ffn2_w_down, final_norm, loss_target, m_ffn1_norm, m_ffn1_w_gate_up, m_ffn1_w_down, m_mix_norm, m_w_in, m_conv_a_w, m_w_out_a, m_ssm_conv_w, m_ssm_conv_b, m_ssm_dt_bias, m_ssm_a_log, m_ssm_d, m_ssm_norm, m_w_out_ssm, m_w_mix_out, m_xattn_norm, m_mem_norm, m_w_q, m_w_kv, m_w_o_x, m_ffn2_norm, m_ffn2_w_gate_up, m_ffn2_w_down, m_final_norm, v_ffn1_norm, v_ffn1_w_gate_up, v_ffn1_w_down, v_mix_norm, v_w_in, v_conv_a_w, v_w_out_a, v_ssm_conv_w, v_ssm_conv_b, v_ssm_dt_bias, v_ssm_a_log, v_ssm_d, v_ssm_norm, v_w_out_ssm, v_w_mix_out, v_xattn_norm, v_mem_norm, v_w_q, v_w_kv, v_w_o_x, v_ffn2_norm, v_ffn2_w_gate_up, v_ffn2_w_down, v_final_norm):
    given = dict(x=x, mem=mem, ffn1_norm=ffn1_norm, ffn1_w_gate_up=ffn1_w_gate_up, ffn1_w_down=ffn1_w_down, mix_norm=mix_norm, w_in=w_in, conv_a_w=conv_a_w, w_out_a=w_out_a, ssm_conv_w=ssm_conv_w, ssm_conv_b=ssm_conv_b, ssm_dt_bias=ssm_dt_bias, ssm_a_log=ssm_a_log, ssm_d=ssm_d, ssm_norm=ssm_norm, w_out_ssm=w_out_ssm, w_mix_out=w_mix_out, xattn_norm=xattn_norm, mem_norm=mem_norm, w_q=w_q, w_kv=w_kv, w_o_x=w_o_x, ffn2_norm=ffn2_norm, ffn2_w_gate_up=ffn2_w_gate_up, ffn2_w_down=ffn2_w_down, final_norm=final_norm, loss_target=loss_target, m_ffn1_norm=m_ffn1_norm, m_ffn1_w_gate_up=m_ffn1_w_gate_up, m_ffn1_w_down=m_ffn1_w_down, m_mix_norm=m_mix_norm, m_w_in=m_w_in, m_conv_a_w=m_conv_a_w, m_w_out_a=m_w_out_a, m_ssm_conv_w=m_ssm_conv_w, m_ssm_conv_b=m_ssm_conv_b, m_ssm_dt_bias=m_ssm_dt_bias, m_ssm_a_log=m_ssm_a_log, m_ssm_d=m_ssm_d, m_ssm_norm=m_ssm_norm, m_w_out_ssm=m_w_out_ssm, m_w_mix_out=m_w_mix_out, m_xattn_norm=m_xattn_norm, m_mem_norm=m_mem_norm, m_w_q=m_w_q, m_w_kv=m_w_kv, m_w_o_x=m_w_o_x, m_ffn2_norm=m_ffn2_norm, m_ffn2_w_gate_up=m_ffn2_w_gate_up, m_ffn2_w_down=m_ffn2_w_down, m_final_norm=m_final_norm, v_ffn1_norm=v_ffn1_norm, v_ffn1_w_gate_up=v_ffn1_w_gate_up, v_ffn1_w_down=v_ffn1_w_down, v_mix_norm=v_mix_norm, v_w_in=v_w_in, v_conv_a_w=v_conv_a_w, v_w_out_a=v_w_out_a, v_ssm_conv_w=v_ssm_conv_w, v_ssm_conv_b=v_ssm_conv_b, v_ssm_dt_bias=v_ssm_dt_bias, v_ssm_a_log=v_ssm_a_log, v_ssm_d=v_ssm_d, v_ssm_norm=v_ssm_norm, v_w_out_ssm=v_w_out_ssm, v_w_mix_out=v_w_mix_out, v_xattn_norm=v_xattn_norm, v_mem_norm=v_mem_norm, v_w_q=v_w_q, v_w_kv=v_w_kv, v_w_o_x=v_w_o_x, v_ffn2_norm=v_ffn2_norm, v_ffn2_w_gate_up=v_ffn2_w_gate_up, v_ffn2_w_down=v_ffn2_w_down, v_final_norm=v_final_norm)
    weights = {n: given[n] for n in TWIN_WEIGHTS}
    shared = {n: given[n] for n in SHARED_INPUTS}
    per_example = {n: given[n] for n in ['x', 'mem']}
    grad_fn = _jax.value_and_grad(_loss, argnums=(0, 1))

    def one_microbatch(ex, loss_target):
        ex = dict(ex)
        diff = ex.pop(TWIN_DIFF_INPUT)
        return grad_fn(weights, diff, {**shared, **ex}, loss_target)

    if N_MICROBATCH == 1:
        loss, (grad_w, grad_x) = one_microbatch(per_example, given["loss_target"])
    else:
        def body(carry, xs):
            loss_sum, grad_sum = carry
            l_k, (gw_k, gx_k) = one_microbatch(xs[0], xs[1])
            with _jax.named_scope("update"):
                return (loss_sum + l_k, _jax.tree.map(_jnp.add, grad_sum, gw_k)), gx_k

        init = (_jnp.zeros((), _jnp.float32), _jax.tree.map(_jnp.zeros_like, weights))
        (loss, grad_w), grad_x = _jax.lax.scan(body, init, (per_example, given["loss_target"]))
    with _jax.named_scope("update"):
        delta_w, new_m, new_v = {}, {}, {}
        for n in TWIN_WEIGHTS:
            delta_w[n], new_m[n], new_v[n] = _adamw(weights[n], grad_w[n], given["m_" + n], given["v_" + n])
    return (loss, grad_x, *[grad_w[n] for n in TWIN_WEIGHTS], *[delta_w[n] for n in TWIN_WEIGHTS],
            *[new_m[n] for n in TWIN_WEIGHTS], *[new_v[n] for n in TWIN_WEIGHTS])
```

```python
import functools
import math

import jax
import jax.numpy as jnp
from jax import lax
from jax.experimental import pallas as pl
from jax.experimental.pallas import tpu as pltpu

F32, BF16 = jnp.float32, jnp.bfloat16
HI = lax.Precision.HIGHEST
MESH = pl.DeviceIdType.MESH
AXES = ("x", "y", "c")
N_DEV = 8

EPS = 1e-6
FFN_RES_WEIGHT = 0.5
SSM_HEAD_DIM = 64
SSM_GROUPS = 4
SSM_STATE = 128
SSM_CHUNK = 128
CONV_A_K = 3
SSM_CONV_K = 4
XATTN_HEADS = 4
ADAM_LR, ADAM_B1, ADAM_B2, ADAM_EPS, ADAM_WD, ADAM_STEP = 1e-3, 0.9, 0.999, 1e-8, 0.01, 10

LANES = 128
BF16_SUBLANES = 16
VMEM_LIMIT_BYTES = 56 * 2 ** 20
NEG_BIG = -1e30

BIG_WEIGHTS = ("ffn1_w_gate_up", "ffn1_w_down", "w_in", "w_out_a", "w_out_ssm", "w_mix_out",
               "w_q", "w_kv", "w_o_x", "ffn2_w_gate_up", "ffn2_w_down")
COL_SHARDED = ("ffn1_w_gate_up", "w_in", "w_kv", "ffn2_w_gate_up")
SMALL_REPLICATED = ("ffn1_norm", "mix_norm", "ssm_conv_b", "ssm_dt_bias", "ssm_a_log", "ssm_d", "ssm_norm",
                    "xattn_norm", "mem_norm", "ffn2_norm", "final_norm")
SMALL_SHARDED = ("conv_a_w", "ssm_conv_w")
WEIGHT_ORDER = ("ffn1_norm", "ffn1_w_gate_up", "ffn1_w_down", "mix_norm", "w_in", "conv_a_w", "w_out_a",
                "ssm_conv_w", "ssm_conv_b", "ssm_dt_bias", "ssm_a_log", "ssm_d", "ssm_norm", "w_out_ssm",
                "w_mix_out", "xattn_norm", "mem_norm", "w_q", "w_kv", "w_o_x", "ffn2_norm", "ffn2_w_gate_up",
                "ffn2_w_down", "final_norm")


def _tile(dim, pref, unit):
    best = None
    t = unit
    while t <= min(dim, pref):
        if dim % t == 0:
            best = t
        t += unit
    return best if best is not None else dim


def _params(*sem):
    return pltpu.CompilerParams(dimension_semantics=sem, vmem_limit_bytes=VMEM_LIMIT_BYTES)


def _silu(x):
    return x * jax.nn.sigmoid(x)


def _dsilu(x):
    s = jax.nn.sigmoid(x)
    return s * (1.0 + x * (1.0 - s))


def _softplus(x):
    return jnp.maximum(x, 0.0) + jnp.log(1.0 + jnp.exp(-jnp.abs(x)))


def _dot(a, b, dims=(((1,), (0,)), ((), ())), precision=None):
    return lax.dot_general(a, b, dims, preferred_element_type=F32, precision=precision)


def _stack_rows(rows, width):
    r_idx = lax.broadcasted_iota(jnp.int32, (8, width), 0)
    acc = jnp.zeros((8, width), F32)
    for k, row in enumerate(rows):
        acc = jnp.where(r_idx == k, row, acc)
    return acc


NT = (((1,), (1,)), ((), ()))
TN = (((0,), (0,)), ((), ()))


def _mm(a, b, out_dtype, name, res=None, alpha=1.0, tm=1024, tn=512, tk=2816):
    m, k = a.shape
    n = b.shape[1]
    tm, tn, tk = _tile(m, tm, 8), _tile(n, tn, LANES), _tile(k, tk, LANES)
    nk = k // tk

    def body(*refs):
        if res is None:
            a_ref, b_ref, o_ref = refs[:3]
            r_ref, scr = None, refs[3:]
        else:
            a_ref, b_ref, r_ref, o_ref = refs[:4]
            scr = refs[4:]

        def finish(acc):
            if r_ref is not None:
                acc = r_ref[...] + alpha * acc
            o_ref[...] = acc.astype(out_dtype)

        part = _dot(a_ref[...].astype(BF16), b_ref[...].astype(BF16))
        if nk == 1:
            finish(part)
        else:
            acc_ref = scr[0]
            kk = pl.program_id(2)

            @pl.when(kk == 0)
            def _():
                acc_ref[...] = part

            @pl.when(kk > 0)
            def _():
                acc_ref[...] += part

            @pl.when(kk == nk - 1)
            def _():
                finish(acc_ref[...])

    in_specs = [pl.BlockSpec((tm, tk), lambda i, j, kk: (i, kk)), pl.BlockSpec((tk, tn), lambda i, j, kk: (kk, j))]
    args = [a, b]
    if res is not None:
        in_specs.append(pl.BlockSpec((tm, tn), lambda i, j, kk: (i, j)))
        args.append(res)
    return pl.pallas_call(
        body, name=name, grid=(m // tm, n // tn, nk), in_specs=in_specs,
        out_specs=pl.BlockSpec((tm, tn), lambda i, j, kk: (i, j)),
        out_shape=jax.ShapeDtypeStruct((m, n), out_dtype),
        scratch_shapes=[pltpu.VMEM((tm, tn), F32)] if nk > 1 else [],
        compiler_params=_params("parallel", "parallel", "arbitrary"),
    )(*args)


def _mm_tn(x, dy, name, tko=1408, tn=512, tt=512):
    t, k = x.shape
    n = dy.shape[1]
    tko, tn, tt = _tile(k, tko, LANES), _tile(n, tn, LANES), _tile(t, tt, 8)

    def body(x_ref, dy_ref, o_ref):
        part = _dot(x_ref[...].astype(BF16), dy_ref[...].astype(BF16), TN)
        step = pl.program_id(2)

        @pl.when(step == 0)
        def _():
            o_ref[...] = part

        @pl.when(step > 0)
        def _():
            o_ref[...] += part

    return pl.pallas_call(
        body, name=name, grid=(k // tko, n // tn, t // tt),
        in_specs=[pl.BlockSpec((tt, tko), lambda i, j, s: (s, i)), pl.BlockSpec((tt, tn), lambda i, j, s: (s, j))],
        out_specs=pl.BlockSpec((tko, tn), lambda i, j, s: (i, j)),
        out_shape=jax.ShapeDtypeStruct((k, n), F32),
        compiler_params=_params("parallel", "parallel", "arbitrary"),
    )(x, dy)


def _rms_fwd(x, g, name, tt=512):
    t, d = x.shape
    tt = _tile(t, tt, 8)

    def body(x_ref, g_ref, o_ref):
        xv = x_ref[...]
        r = lax.rsqrt(jnp.mean(xv * xv, axis=-1, keepdims=True) + EPS)
        o_ref[...] = (xv * r * g_ref[...]).astype(BF16)

    return pl.pallas_call(
        body, name=name, grid=(t // tt,),
        in_specs=[pl.BlockSpec((tt, d), lambda i: (i, 0)), pl.BlockSpec((1, d), lambda i: (0, 0))],
        out_specs=pl.BlockSpec((tt, d), lambda i: (i, 0)),
        out_shape=jax.ShapeDtypeStruct((t, d), BF16), compiler_params=_params("parallel"),
    )(x, g)


def _rms_bwd(dn, x, g, name, dres=None, tt=512):
    t, d = x.shape
    tt = _tile(t, tt, 8)

    def body(*refs):
        if dres is None:
            dn_ref, x_ref, g_ref, dx_ref, dg_ref = refs
            r_ref = None
        else:
            dn_ref, x_ref, g_ref, r_ref, dx_ref, dg_ref = refs
        xv, dnv = x_ref[...], dn_ref[...].astype(F32)
        r = lax.rsqrt(jnp.mean(xv * xv, axis=-1, keepdims=True) + EPS)
        xh = xv * r
        gy = dnv * g_ref[...]
        dx = r * (gy - xh * jnp.mean(gy * xh, axis=-1, keepdims=True))
        if r_ref is not None:
            dx = dx + r_ref[...]
        dx_ref[...] = dx
        part = jnp.sum(dnv * xh, axis=0, keepdims=True)

        @pl.when(pl.program_id(0) == 0)
        def _():
            dg_ref[...] = part

        @pl.when(pl.program_id(0) > 0)
        def _():
            dg_ref[...] += part

    row = pl.BlockSpec((tt, d), lambda i: (i, 0))
    vec = pl.BlockSpec((1, d), lambda i: (0, 0))
    in_specs, args = [row, row, vec], [dn, x, g]
    if dres is not None:
        in_specs.append(row)
        args.append(dres)
    return pl.pallas_call(
        body, name=name, grid=(t // tt,), in_specs=in_specs, out_specs=[row, vec],
        out_shape=[jax.ShapeDtypeStruct((t, d), F32), jax.ShapeDtypeStruct((1, d), F32)],
        compiler_params=_params("arbitrary"),
    )(*args)


def _swiglu_fwd(gu, name, tt=512):
    t, f2 = gu.shape
    f = f2 // 2
    tt = _tile(t, tt, 8)
    tf = _tile(f, 1408, LANES)
    nf = f // tf

    def body(g_ref, u_ref, o_ref):
        o_ref[...] = (_silu(g_ref[...].astype(F32)) * u_ref[...].astype(F32)).astype(BF16)

    return pl.pallas_call(
        body, name=name, grid=(t // tt, nf),
        in_specs=[pl.BlockSpec((tt, tf), lambda i, j: (i, j)), pl.BlockSpec((tt, tf), lambda i, j: (i, j + nf))],
        out_specs=pl.BlockSpec((tt, tf), lambda i, j: (i, j)),
        out_shape=jax.ShapeDtypeStruct((t, f), BF16), compiler_params=_params("parallel", "parallel"),
    )(gu, gu)


def _swiglu_bwd(da, gu, name, tt=512):
    t, f2 = gu.shape
    f = f2 // 2
    tt = _tile(t, tt, 8)
    tf = _tile(f, 1408, LANES)
    nf = f // tf

    def body(da_ref, g_ref, u_ref, dg_ref, du_ref):
        dav, gv, uv = da_ref[...].astype(F32), g_ref[...].astype(F32), u_ref[...].astype(F32)
        dg_ref[...] = (dav * uv * _dsilu(gv)).astype(BF16)
        du_ref[...] = (dav * _silu(gv)).astype(BF16)

    blk = pl.BlockSpec((tt, tf), lambda i, j: (i, j))
    blk_up = pl.BlockSpec((tt, tf), lambda i, j: (i, j + nf))
    dgate, dup = pl.pallas_call(
        body, name=name, grid=(t // tt, nf), in_specs=[blk, blk, blk_up], out_specs=[blk, blk],
        out_shape=[jax.ShapeDtypeStruct((t, f), BF16), jax.ShapeDtypeStruct((t, f), BF16)],
        compiler_params=_params("parallel", "parallel"),
    )(da, gu, gu)
    return dgate, dup


def _merge_fwd(ya, yb, proj, ga_blk, gb_blk, d, name, tt=512):
    t = ya.shape[0]
    tt = _tile(t, tt, 8)

    def body(ya_ref, yb_ref, ga_ref, gb_ref, o_ref):
        o_ref[...] = (jax.nn.sigmoid(ga_ref[...].astype(F32)) * ya_ref[...].astype(F32)
                      + jax.nn.sigmoid(gb_ref[...].astype(F32)) * yb_ref[...].astype(F32)).astype(BF16)

    row = pl.BlockSpec((tt, d), lambda i: (i, 0))
    return pl.pallas_call(
        body, name=name, grid=(t // tt,),
        in_specs=[row, row, pl.BlockSpec((tt, d), lambda i: (i, ga_blk)), pl.BlockSpec((tt, d), lambda i: (i, gb_blk))],
        out_specs=row, out_shape=jax.ShapeDtypeStruct((t, d), BF16), compiler_params=_params("parallel"),
    )(ya, yb, proj, proj)


def _merge_bwd(dm, ya, yb, proj, ga_blk, gb_blk, d, name, tt=512):
    t = ya.shape[0]
    tt = _tile(t, tt, 8)

    def body(dm_ref, ya_ref, yb_ref, ga_ref, gb_ref, dya_ref, dyb_ref, dga_ref, dgb_ref):
        dmv = dm_ref[...].astype(F32)
        sa, sb = jax.nn.sigmoid(ga_ref[...].astype(F32)), jax.nn.sigmoid(gb_ref[...].astype(F32))
        dya_ref[...] = (dmv * sa).astype(BF16)
        dyb_ref[...] = (dmv * sb).astype(BF16)
        dga_ref[...] = (dmv * ya_ref[...].astype(F32) * sa * (1.0 - sa)).astype(BF16)
        dgb_ref[...] = (dmv * yb_ref[...].astype(F32) * sb * (1.0 - sb)).astype(BF16)

    row = pl.BlockSpec((tt, d), lambda i: (i, 0))
    out = jax.ShapeDtypeStruct((t, d), BF16)
    return pl.pallas_call(
        body, name=name, grid=(t // tt,),
        in_specs=[row, row, row, pl.BlockSpec((tt, d), lambda i: (i, ga_blk)), pl.BlockSpec((tt, d), lambda i: (i, gb_blk))],
        out_specs=[row] * 4, out_shape=[out] * 4, compiler_params=_params("parallel"),
    )(dm, ya, yb, proj, proj)


def _loss_head(h, g, target, name, tt=512):
    t, d = h.shape
    tt = _tile(t, tt, 8)

    def body(h_ref, g_ref, tg_ref, loss_ref, dh_ref, dg_ref):
        xv = h_ref[...]
        r = lax.rsqrt(jnp.mean(xv * xv, axis=-1, keepdims=True) + EPS)
        xh = xv * r
        err = xh * g_ref[...] - tg_ref[...]
        dout = err * (1.0 / d)
        gy = dout * g_ref[...]
        dh_ref[...] = r * (gy - xh * jnp.mean(gy * xh, axis=-1, keepdims=True))
        dg_part = jnp.sum(dout * xh, axis=0, keepdims=True)
        loss_part = jnp.full((1, LANES), 0.5 / d, F32) * jnp.sum(err * err)

        @pl.when(pl.program_id(0) == 0)
        def _():
            dg_ref[...] = dg_part
            loss_ref[...] = loss_part

        @pl.when(pl.program_id(0) > 0)
        def _():
            dg_ref[...] += dg_part
            loss_ref[...] += loss_part

    row = pl.BlockSpec((tt, d), lambda i: (i, 0))
    vec = pl.BlockSpec((1, d), lambda i: (0, 0))
    return pl.pallas_call(
        body, name=name, grid=(t // tt,), in_specs=[row, vec, row],
        out_specs=[pl.BlockSpec((1, LANES), lambda i: (0, 0)), row, vec],
        out_shape=[jax.ShapeDtypeStruct((1, LANES), F32), jax.ShapeDtypeStruct((t, d), F32), jax.ShapeDtypeStruct((1, d), F32)],
        compiler_params=_params("arbitrary"),
    )(h, g, target)


def _shift_down(x, k, t_idx):
    if k == 0:
        return x
    return jnp.where(t_idx >= k, pltpu.roll(x, k, 0), 0.0)


def _shift_up(x, k, t_idx, s):
    if k == 0:
        return x
    return jnp.where(t_idx < s - k, pltpu.roll(x, s - k, 0), 0.0)


def _conv_a_fwd(proj, w, nb, s, d, name, cb=256):
    cb = _tile(d, cb, LANES)
    nd = d // cb

    def body(b_ref, c_ref, v_ref, w_ref, o_ref):
        t_idx = lax.broadcasted_iota(jnp.int32, (s, cb), 0)
        cv = c_ref[...].astype(F32) * v_ref[...].astype(F32)
        cc = sum(w_ref[k:k + 1, :] * _shift_down(cv, CONV_A_K - 1 - k, t_idx) for k in range(CONV_A_K))
        o_ref[...] = (b_ref[...].astype(F32) * cc).astype(BF16)

    def col(off):
        return pl.BlockSpec((s, cb), lambda b, j: (b, j + off * nd))

    return pl.pallas_call(
        body, name=name, grid=(nb, nd), in_specs=[col(0), col(1), col(2), pl.BlockSpec((8, cb), lambda b, j: (0, j))],
        out_specs=pl.BlockSpec((s, cb), lambda b, j: (b, j)),
        out_shape=jax.ShapeDtypeStruct((nb * s, d), BF16), compiler_params=_params("parallel", "parallel"),
    )(proj, proj, proj, w)


def _conv_a_bwd(dy, proj, w, nb, s, d, name, cb=256):
    cb = _tile(d, cb, LANES)
    nd = d // cb

    def body(dy_ref, b_ref, c_ref, v_ref, w_ref, db_ref, dc_ref, dv_ref, dw_ref):
        t_idx = lax.broadcasted_iota(jnp.int32, (s, cb), 0)
        cv_c, cv_v = c_ref[...].astype(F32), v_ref[...].astype(F32)
        cv = cv_c * cv_v
        shifted = [_shift_down(cv, CONV_A_K - 1 - k, t_idx) for k in range(CONV_A_K)]
        cc = sum(w_ref[k:k + 1, :] * shifted[k] for k in range(CONV_A_K))
        dyv = dy_ref[...].astype(F32)
        db_ref[...] = (dyv * cc).astype(BF16)
        dcc = dyv * b_ref[...].astype(F32)
        dcv = sum(w_ref[k:k + 1, :] * _shift_up(dcc, CONV_A_K - 1 - k, t_idx, s) for k in range(CONV_A_K))
        dc_ref[...] = (dcv * cv_v).astype(BF16)
        dv_ref[...] = (dcv * cv_c).astype(BF16)
        rows = [jnp.sum(dcc * shifted[k], axis=0, keepdims=True) for k in range(CONV_A_K)]
        part = _stack_rows(rows, cb)

        @pl.when(pl.program_id(1) == 0)
        def _():
            dw_ref[...] = part

        @pl.when(pl.program_id(1) > 0)
        def _():
            dw_ref[...] += part

    def col(off):
        return pl.BlockSpec((s, cb), lambda j, b: (b, j + off * nd))

    own = pl.BlockSpec((s, cb), lambda j, b: (b, j))
    wspec = pl.BlockSpec((8, cb), lambda j, b: (0, j))
    out = jax.ShapeDtypeStruct((nb * s, d), BF16)
    return pl.pallas_call(
        body, name=name, grid=(nd, nb), in_specs=[own, col(0), col(1), col(2), wspec],
        out_specs=[own, own, own, wspec], out_shape=[out, out, out, jax.ShapeDtypeStruct((8, d), F32)],
        compiler_params=_params("parallel", "arbitrary"),
    )(dy, proj, proj, proj, w)


def _conv_s_fwd(proj, col0, w, bias, nb, s, cc_width, name, cb=256):
    cb = _tile(math.gcd(cc_width, col0) if col0 else cc_width, cb, LANES)
    nd, off = cc_width // cb, col0 // cb

    def body(x_ref, w_ref, b_ref, o_ref):
        t_idx = lax.broadcasted_iota(jnp.int32, (s, cb), 0)
        xv = x_ref[...].astype(F32)
        pre = b_ref[...] + sum(w_ref[k:k + 1, :] * _shift_down(xv, SSM_CONV_K - 1 - k, t_idx) for k in range(SSM_CONV_K))
        o_ref[...] = _silu(pre).astype(BF16)

    vec = pl.BlockSpec((8, cb), lambda b, j: (0, j))
    return pl.pallas_call(
        body, name=name, grid=(nb, nd),
        in_specs=[pl.BlockSpec((s, cb), lambda b, j: (b, j + off)), vec, pl.BlockSpec((1, cb), lambda b, j: (0, j))],
        out_specs=pl.BlockSpec((s, cb), lambda b, j: (b, j)),
        out_shape=jax.ShapeDtypeStruct((nb * s, cc_width), BF16), compiler_params=_params("parallel", "parallel"),
    )(proj, w, bias)


def _conv_s_bwd(dxc, proj, col0, w, bias, nb, s, cc_width, name, cb=256):
    cb = _tile(math.gcd(cc_width, col0) if col0 else cc_width, cb, LANES)
    nd, off = cc_width // cb, col0 // cb

    def body(d_ref, x_ref, w_ref, b_ref, dx_ref, dw_ref, db_ref):
        t_idx = lax.broadcasted_iota(jnp.int32, (s, cb), 0)
        xv = x_ref[...].astype(F32)
        shifted = [_shift_down(xv, SSM_CONV_K - 1 - k, t_idx) for k in range(SSM_CONV_K)]
        pre = b_ref[...] + sum(w_ref[k:k + 1, :] * shifted[k] for k in range(SSM_CONV_K))
        dpre = d_ref[...].astype(F32) * _dsilu(pre)
        dx = sum(w_ref[k:k + 1, :] * _shift_up(dpre, SSM_CONV_K - 1 - k, t_idx, s) for k in range(SSM_CONV_K))
        dx_ref[...] = dx.astype(BF16)
        rows = [jnp.sum(dpre * shifted[k], axis=0, keepdims=True) for k in range(SSM_CONV_K)]
        dw_part = _stack_rows(rows, cb)
        db_part = jnp.sum(dpre, axis=0, keepdims=True)

        @pl.when(pl.program_id(1) == 0)
        def _():
            dw_ref[...] = dw_part
            db_ref[...] = db_part

        @pl.when(pl.program_id(1) > 0)
        def _():
            dw_ref[...] += dw_part
            db_ref[...] += db_part

    own = pl.BlockSpec((s, cb), lambda j, b: (b, j))
    wspec = pl.BlockSpec((8, cb), lambda j, b: (0, j))
    bspec = pl.BlockSpec((1, cb), lambda j, b: (0, j))
    return pl.pallas_call(
        body, name=name, grid=(nd, nb),
        in_specs=[own, pl.BlockSpec((s, cb), lambda j, b: (b, j + off)), wspec, bspec],
        out_specs=[own, wspec, bspec],
        out_shape=[jax.ShapeDtypeStruct((nb * s, cc_width), BF16), jax.ShapeDtypeStruct((8, cc_width), F32),
                   jax.ShapeDtypeStruct((1, cc_width), F32)],
        compiler_params=_params("parallel", "arbitrary"),
    )(dxc, proj, w, bias)


def _ssd_common(xc_ref, dtr_ref, dtrt_ref, prow_ref, pcol_ref, e_ref, di):
    l = SSM_CHUNK
    bias_r, a_r = prow_ref[0:1, :], -jnp.exp(prow_ref[1:2, :])
    sp_in = dtr_ref[...] + bias_r
    dt = _softplus(sp_in)
    li = lax.broadcasted_iota(jnp.int32, (l, l), 0)
    si = lax.broadcasted_iota(jnp.int32, (l, l), 1)
    lower = (li >= si).astype(F32)
    upper = (li <= si).astype(F32)
    acs = _dot(lower, dt * a_r, precision=HI)
    bias_c, a_c = pcol_ref[:, 0:1], -jnp.exp(pcol_ref[:, 1:2])
    dt_t = _softplus(dtrt_ref[...] + bias_c)
    acs_t = _dot(dt_t * a_c, upper, precision=HI)
    e_mat = e_ref[...]
    dt_exp = _dot(dt, e_mat, precision=HI)
    acs_exp = _dot(acs, e_mat, precision=HI)
    acs_last = acs_exp[l - 1:l, :]
    x = xc_ref[:, 0:di].astype(F32)
    return dict(dt=dt, a_r=a_r, sp_in=sp_in, acs=acs, acs_t=acs_t, dt_exp=dt_exp, e_exp=jnp.exp(acs_exp),
                el_exp=jnp.exp(acs_last), f_exp=jnp.exp(acs_last - acs_exp), x=x, mask=li >= si, upper=upper,
                d_exp=_dot(prow_ref[0:8, :], e_mat, precision=HI)[2:3, :])


def _decay(q, h):
    seg = q["acs"][:, h:h + 1] - q["acs_t"][h:h + 1, :]
    return jnp.exp(jnp.where(q["mask"], seg, NEG_BIG))


def _ssd_fwd(xc, dtr, dtrt, prow, pcol, e_mat, nb, nc, di, name):
    l, n, g_n, p = SSM_CHUNK, SSM_STATE, SSM_GROUPS, SSM_HEAD_DIM
    cc = xc.shape[1]
    gw = di // g_n
    assert p * 2 == LANES and gw % LANES == 0

    def body(xc_ref, dtr_ref, dtrt_ref, prow_ref, pcol_ref, e_ref, y_ref, sprev_ref, st_ref):
        @pl.when(pl.program_id(1) == 0)
        def _():
            st_ref[...] = jnp.zeros_like(st_ref)

        q = _ssd_common(xc_ref, dtr_ref, dtrt_ref, prow_ref, pcol_ref, e_ref, di)
        x = q["x"]
        xd = x * q["dt_exp"]
        xdb = xd.astype(BF16)
        xdf = (xd * q["f_exp"]).astype(BF16)
        lane = lax.broadcasted_iota(jnp.int32, (l, LANES), 1)
        for g in range(g_n):
            lo = g * gw
            bg = xc_ref[:, di + g * n: di + (g + 1) * n]
            cg = xc_ref[:, di + g_n * n + g * n: di + g_n * n + (g + 1) * n]
            cb = _dot(cg, bg, NT)
            st_g = st_ref[:, lo:lo + gw]
            y_off = q["e_exp"][:, lo:lo + gw] * _dot(cg, st_g.astype(BF16))
            for pr in range(gw // LANES):
                c0 = lo + pr * LANES
                h0 = c0 // p
                xp = xdb[:, c0:c0 + LANES]
                m0 = (cb * _decay(q, h0)).astype(BF16)
                m1 = (cb * _decay(q, h0 + 1)).astype(BF16)
                yd = _dot(m0, jnp.where(lane < p, xp, 0)) + _dot(m1, jnp.where(lane >= p, xp, 0))
                y_ref[:, c0:c0 + LANES] = (yd + y_off[:, pr * LANES:(pr + 1) * LANES]
                                           + q["d_exp"][:, c0:c0 + LANES] * x[:, c0:c0 + LANES])
            sprev_ref[:, lo:lo + gw] = st_g
            st_ref[:, lo:lo + gw] = q["el_exp"][:, lo:lo + gw] * st_g + _dot(bg, xdf[:, lo:lo + gw], TN)

    tok = lambda w: pl.BlockSpec((l, w), lambda b, c: (b * nc + c, 0))
    const = lambda r, w: pl.BlockSpec((r, w), lambda b, c: (0, 0))
    return pl.pallas_call(
        body, name=name, grid=(nb, nc),
        in_specs=[tok(cc), tok(LANES), pl.BlockSpec((LANES, l), lambda b, c: (0, b * nc + c)),
                  const(8, LANES), const(LANES, 8), const(LANES, di)],
        out_specs=[tok(di), pl.BlockSpec((None, n, di), lambda b, c: (b * nc + c, 0, 0))],
        out_shape=[jax.ShapeDtypeStruct((nb * nc * l, di), F32), jax.ShapeDtypeStruct((nb * nc, n, di), F32)],
        scratch_shapes=[pltpu.VMEM((n, di), F32)],
        compiler_params=_params("parallel", "arbitrary"),
    )(xc, dtr, dtrt, prow, pcol, e_mat)


def _ssd_bwd(dy, xc, dtr, dtrt, prow, pcol, e_mat, sprev, nb, nc, di, name):
    l, n, g_n, p = SSM_CHUNK, SSM_STATE, SSM_GROUPS, SSM_HEAD_DIM
    cc = xc.shape[1]
    gw = di // g_n

    def body(dy_ref, xc_ref, dtr_ref, dtrt_ref, prow_ref, pcol_ref, e_ref, sprev_ref,
             dxc_ref, ddtr_ref, sums_ref, dst_ref, off_ref, dxd_ref, last_ref, dla_diag_ref, vst_ref):
        first = jnp.logical_and(pl.program_id(0) == 0, pl.program_id(1) == 0)

        @pl.when(pl.program_id(1) == 0)
        def _():
            dst_ref[...] = jnp.zeros_like(dst_ref)

        dla_diag_ref[...] = jnp.zeros_like(dla_diag_ref)
        strict_lower = lax.broadcasted_iota(jnp.int32, (l, l), 0) > lax.broadcasted_iota(jnp.int32, (l, l), 1)

        q = _ssd_common(xc_ref, dtr_ref, dtrt_ref, prow_ref, pcol_ref, e_ref, di)
        x = q["x"]
        xd = x * q["dt_exp"]
        xdb = xd.astype(BF16)
        xdf = (xd * q["f_exp"]).astype(BF16)
        dyv = dy_ref[...]
        dyb = dyv.astype(BF16)
        dye = (dyv * q["e_exp"]).astype(BF16)
        upper_b = q["upper"].astype(BF16)
        lane = lax.broadcasted_iota(jnp.int32, (l, LANES), 1)
        for g in range(g_n):
            lo = g * gw
            bg = xc_ref[:, di + g * n: di + (g + 1) * n]
            cg = xc_ref[:, di + g_n * n + g * n: di + g_n * n + (g + 1) * n]
            cb = _dot(cg, bg, NT)
            st_g = sprev_ref[:, lo:lo + gw]
            st_gb = st_g.astype(BF16)
            dst_g = dst_ref[:, lo:lo + gw]
            dst_gb = dst_g.astype(BF16)
            dye_g = dye[:, lo:lo + gw]
            xdf_g = xdf[:, lo:lo + gw]
            y_off = q["e_exp"][:, lo:lo + gw] * _dot(cg, st_gb)
            dc_g = _dot(dye_g, st_gb, NT)
            db_g = _dot(xdf_g, dst_gb, NT)
            dxd_state = _dot(bg, dst_gb) * q["f_exp"][:, lo:lo + gw]
            last_ref[:, lo:lo + gw] = jnp.sum(dst_g * st_g, axis=0, keepdims=True)
            dst_ref[:, lo:lo + gw] = q["el_exp"][:, lo:lo + gw] * dst_g + _dot(cg, dye_g, TN)
            off_ref[:, lo:lo + gw] = dyv[:, lo:lo + gw] * y_off
            vst_ref[:, lo:lo + gw] = xd[:, lo:lo + gw] * dxd_state
            dcb = jnp.zeros((l, l), F32)
            for pr in range(gw // LANES):
                c0 = lo + pr * LANES
                h0 = c0 // p
                xp = xdb[:, c0:c0 + LANES]
                dyp = dyb[:, c0:c0 + LANES]
                dxd_diag = jnp.zeros((l, LANES), F32)
                for k, keep in enumerate((lane < p, lane >= p)):
                    dec = _decay(q, h0 + k)
                    dy_h = jnp.where(keep, dyp, 0)
                    dm_dec = _dot(dy_h, xp, NT) * dec
                    dcb = dcb + dm_dec
                    dxd_diag = dxd_diag + _dot((cb * dec).astype(BF16), dy_h, TN)
                    above = _dot(upper_b, (dm_dec * cb).astype(BF16))
                    dla_col = jnp.sum(jnp.where(strict_lower, above, 0.0), axis=1, keepdims=True)
                    dla_diag_ref[...] = jnp.where(lane == h0 + k, dla_col, dla_diag_ref[...])
                dxd_ref[:, c0:c0 + LANES] = dxd_diag + dxd_state[:, pr * LANES:(pr + 1) * LANES]
            dcb_b = dcb.astype(BF16)
            dxc_ref[:, di + g * n: di + (g + 1) * n] = (db_g + _dot(dcb_b, cg, TN)).astype(BF16)
            dxc_ref[:, di + g_n * n + g * n: di + g_n * n + (g + 1) * n] = (dc_g + _dot(dcb_b, bg)).astype(BF16)
        dxd = dxd_ref[...]
        e_mat = e_ref[...]
        from_y = _dot(q["upper"], _dot(off_ref[...], e_mat, NT, precision=HI), precision=HI)
        from_s = _dot(strict_lower.astype(F32), _dot(vst_ref[...], e_mat, NT, precision=HI), precision=HI)
        carried = (_dot(jnp.broadcast_to(last_ref[...], (8, di)), e_mat, NT, precision=HI)[0:1, :]
                   * jnp.exp(q["acs"][l - 1:l, :]))
        dla = from_y + from_s + carried + dla_diag_ref[...]
        ddt = dla * q["a_r"] + _dot(dxd * x, e_mat, NT, precision=HI)
        ddtr = ddt * jax.nn.sigmoid(q["sp_in"])
        ddtr_ref[...] = ddtr
        dxc_ref[:, 0:di] = (dxd * q["dt_exp"] + q["d_exp"] * dyv).astype(BF16)
        dd_exp = jnp.sum(dyv * x, axis=0, keepdims=True)
        dd = _dot(jnp.broadcast_to(dd_exp, (8, di)), e_mat, NT, precision=HI)[0:1, :]
        part = _stack_rows([jnp.sum(ddtr, axis=0, keepdims=True),
                            jnp.sum(dla * q["dt"], axis=0, keepdims=True) * q["a_r"], dd], LANES)

        @pl.when(first)
        def _():
            sums_ref[...] = part

        @pl.when(jnp.logical_not(first))
        def _():
            sums_ref[...] += part

    rev = lambda b, c: b * nc + (nc - 1 - c)
    tok = lambda w: pl.BlockSpec((l, w), lambda b, c: (rev(b, c), 0))
    const = lambda r, w: pl.BlockSpec((r, w), lambda b, c: (0, 0))
    return pl.pallas_call(
        body, name=name, grid=(nb, nc),
        in_specs=[tok(di), tok(cc), tok(LANES), pl.BlockSpec((LANES, l), lambda b, c: (0, rev(b, c))),
                  const(8, LANES), const(LANES, 8), const(LANES, di),
                  pl.BlockSpec((None, n, di), lambda b, c: (rev(b, c), 0, 0))],
        out_specs=[tok(cc), tok(LANES), const(8, LANES)],
        out_shape=[jax.ShapeDtypeStruct((nb * nc * l, cc), BF16), jax.ShapeDtypeStruct((nb * nc * l, LANES), F32),
                   jax.ShapeDtypeStruct((8, LANES), F32)],
        scratch_shapes=[pltpu.VMEM((n, di), F32), pltpu.VMEM((l, di), F32), pltpu.VMEM((l, di), F32),
                        pltpu.VMEM((1, di), F32), pltpu.VMEM((l, LANES), F32), pltpu.VMEM((l, di), F32)],
        compiler_params=_params("arbitrary", "arbitrary"),
    )(dy, xc, dtr, dtrt, prow, pcol, e_mat, sprev)


def _gate_norm_fwd(y, proj, z_col0, norm_g, di, name, tt=256):
    t = y.shape[0]
    tt = _tile(t, tt, 8)
    gw = di // SSM_GROUPS
    zw = _tile(math.gcd(di, z_col0), di, LANES)
    nz, zoff = di // zw, z_col0 // zw

    def body(*refs):
        y_ref, z_refs, g_ref, o_ref = refs[0], refs[1:1 + nz], refs[1 + nz], refs[2 + nz]
        for g in range(SSM_GROUPS):
            lo = g * gw
            zv = z_refs[lo // zw][:, lo % zw:lo % zw + gw].astype(F32)
            yg = y_ref[:, lo:lo + gw] * _silu(zv)
            r = lax.rsqrt(jnp.mean(yg * yg, axis=-1, keepdims=True) + EPS)
            o_ref[:, lo:lo + gw] = (yg * r * g_ref[:, lo:lo + gw]).astype(BF16)

    row = pl.BlockSpec((tt, di), lambda i: (i, 0))
    zspecs = [pl.BlockSpec((tt, zw), functools.partial(lambda i, k: (i, zoff + k), k=k)) for k in range(nz)]
    return pl.pallas_call(
        body, name=name, grid=(t // tt,), in_specs=[row] + zspecs + [pl.BlockSpec((1, di), lambda i: (0, 0))],
        out_specs=row, out_shape=jax.ShapeDtypeStruct((t, di), BF16), compiler_params=_params("parallel"),
    )(y, *([proj] * nz), norm_g)


def _gate_norm_bwd(dn, y, proj, z_col0, norm_g, di, name, tt=256):
    t = y.shape[0]
    tt = _tile(t, tt, 8)
    gw = di // SSM_GROUPS
    zw = _tile(math.gcd(di, z_col0), di, LANES)
    nz, zoff = di // zw, z_col0 // zw

    def body(*refs):
        dn_ref, y_ref, z_refs, g_ref = refs[0], refs[1], refs[2:2 + nz], refs[2 + nz]
        dy_ref, dz_ref, dg_ref = refs[3 + nz:]
        first = pl.program_id(0) == 0
        for g in range(SSM_GROUPS):
            lo = g * gw
            zv = z_refs[lo // zw][:, lo % zw:lo % zw + gw].astype(F32)
            yv = y_ref[:, lo:lo + gw]
            sz = _silu(zv)
            yg = yv * sz
            r = lax.rsqrt(jnp.mean(yg * yg, axis=-1, keepdims=True) + EPS)
            yh = yg * r
            dnv = dn_ref[:, lo:lo + gw].astype(F32)
            gy = dnv * g_ref[:, lo:lo + gw]
            dyg = r * (gy - yh * jnp.mean(gy * yh, axis=-1, keepdims=True))
            dy_ref[:, lo:lo + gw] = dyg * sz
            dz_ref[:, lo:lo + gw] = (dyg * yv * _dsilu(zv)).astype(BF16)
            part = jnp.sum(dnv * yh, axis=0, keepdims=True)

            @pl.when(first)
            def _():
                dg_ref[:, lo:lo + gw] = part

            @pl.when(jnp.logical_not(first))
            def _():
                dg_ref[:, lo:lo + gw] += part

    row = pl.BlockSpec((tt, di), lambda i: (i, 0))
    vec = pl.BlockSpec((1, di), lambda i: (0, 0))
    zspecs = [pl.BlockSpec((tt, zw), functools.partial(lambda i, k: (i, zoff + k), k=k)) for k in range(nz)]
    return pl.pallas_call(
        body, name=name, grid=(t // tt,), in_specs=[row, row] + zspecs + [vec], out_specs=[row, row, vec],
        out_shape=[jax.ShapeDtypeStruct((t, di), F32), jax.ShapeDtypeStruct((t, di), BF16), jax.ShapeDtypeStruct((1, di), F32)],
        compiler_params=_params("arbitrary"),
    )(dn, y, *([proj] * nz), norm_g)


def _softmax_rows(s):
    s = s - jnp.max(s, axis=-1, keepdims=True)
    e = jnp.exp(s)
    return e / jnp.sum(e, axis=-1, keepdims=True)


def _xattn_fwd(q, kv, nb, s, m, d, name, tq=512):
    tq = _tile(s, tq, 8)
    nq = s // tq
    hd = d // XATTN_HEADS
    scale = 1.0 / math.sqrt(hd)

    def body(q_ref, k_ref, v_ref, o_ref):
        for h in range(XATTN_HEADS):
            sl = slice(h * hd, (h + 1) * hd)
            prob = _softmax_rows(_dot(q_ref[:, sl], k_ref[:, sl], NT) * scale)
            o_ref[:, sl] = _dot(prob.astype(BF16), v_ref[:, sl]).astype(BF16)

    return pl.pallas_call(
        body, name=name, grid=(nb, nq),
        in_specs=[pl.BlockSpec((tq, d), lambda b, i: (b * nq + i, 0)), pl.BlockSpec((m, d), lambda b, i: (b, 0)),
                  pl.BlockSpec((m, d), lambda b, i: (b, 1))],
        out_specs=pl.BlockSpec((tq, d), lambda b, i: (b * nq + i, 0)),
        out_shape=jax.ShapeDtypeStruct((nb * s, d), BF16), compiler_params=_params("parallel", "parallel"),
    )(q, kv, kv)


def _xattn_bwd(do, q, kv, nb, s, m, d, name, tq=512):
    tq = _tile(s, tq, 8)
    nq = s // tq
    hd = d // XATTN_HEADS
    scale = 1.0 / math.sqrt(hd)

    def body(do_ref, q_ref, k_ref, v_ref, dq_ref, dk_ref, dv_ref):
        first = pl.program_id(1) == 0
        for h in range(XATTN_HEADS):
            sl = slice(h * hd, (h + 1) * hd)
            qh, kh, vh, doh = q_ref[:, sl], k_ref[:, sl], v_ref[:, sl], do_ref[:, sl]
            prob = _softmax_rows(_dot(qh, kh, NT) * scale)
            dv_h = _dot(prob.astype(BF16), doh, TN)
            dp = _dot(doh, vh, NT)
            ds = (prob * (dp - jnp.sum(dp * prob, axis=-1, keepdims=True)) * scale).astype(BF16)
            dq_ref[:, sl] = _dot(ds, kh).astype(BF16)
            dk_h = _dot(ds, qh, TN)

            @pl.when(first)
            def _():
                dk_ref[:, sl] = dk_h
                dv_ref[:, sl] = dv_h

            @pl.when(jnp.logical_not(first))
            def _():
                dk_ref[:, sl] += dk_h
                dv_ref[:, sl] += dv_h

    qspec = pl.BlockSpec((tq, d), lambda b, i: (b * nq + i, 0))
    dq, dk, dv = pl.pallas_call(
        body, name=name, grid=(nb, nq),
        in_specs=[qspec, qspec, pl.BlockSpec((m, d), lambda b, i: (b, 0)), pl.BlockSpec((m, d), lambda b, i: (b, 1))],
        out_specs=[qspec, pl.BlockSpec((m, d), lambda b, i: (b, 0)), pl.BlockSpec((m, d), lambda b, i: (b, 0))],
        out_shape=[jax.ShapeDtypeStruct((nb * s, d), BF16), jax.ShapeDtypeStruct((nb * m, d), F32),
                   jax.ShapeDtypeStruct((nb * m, d), F32)],
        compiler_params=_params("parallel", "arbitrary"),
    )(do, q, kv, kv)
    return dq, dk, dv


def _all_gather(shard, name):
    r, c_dim = shard.shape

    def body(x_ref, out_ref, send_sems, recv_sems, local_sem):
        x, y, c = lax.axis_index("x"), lax.axis_index("y"), lax.axis_index("c")
        me, sibling = (x, y, c), (x, y, 1 - c)
        chips = [(1 - x, y), (x, 1 - y), (1 - x, 1 - y)]

        def slot(px, py, pc):
            return out_ref.at[4 * px + 2 * py + pc]

        def copy(k, block, to, src=None):
            return pltpu.make_async_remote_copy(
                src_ref=slot(*block) if src is None else src, dst_ref=slot(*block),
                send_sem=send_sems.at[k], recv_sem=recv_sems.at[k], device_id=to, device_id_type=MESH)

        mine = pltpu.make_async_copy(x_ref, slot(*me), local_sem)
        mine.start()
        first = [copy(0, me, sibling, src=x_ref)]
        first += [copy(1 + j, me, (*chip, c), src=x_ref) for j, chip in enumerate(chips)]
        for cp in first:
            cp.start()
        passed = [copy(4 + j, (*chip, c), sibling) for j, chip in enumerate(chips)]
        for j, chip in enumerate(chips):
            copy(1 + j, (*chip, c), me).wait_recv()
            passed[j].start()
        copy(0, sibling, me).wait_recv()
        for j, chip in enumerate(chips):
            copy(4 + j, (*chip, 1 - c), me).wait_recv()
        for cp in first + passed:
            cp.wait_send()
        mine.wait()

    return pl.pallas_call(
        body, name=name, out_shape=jax.ShapeDtypeStruct((N_DEV, r, c_dim), shard.dtype),
        in_specs=[pl.BlockSpec(memory_space=pl.ANY)], out_specs=pl.BlockSpec(memory_space=pl.ANY),
        scratch_shapes=[pltpu.SemaphoreType.DMA((7,)), pltpu.SemaphoreType.DMA((7,)), pltpu.SemaphoreType.DMA],
    )(shard)


def _exchange(slots, name):
    _, r, c_dim = slots.shape

    def body(in_ref, out_ref, send_sems, recv_sems, local_sem):
        x, y, c = lax.axis_index("x"), lax.axis_index("y"), lax.axis_index("c")
        me = 4 * x + 2 * y + c
        mine = pltpu.make_async_copy(in_ref.at[me], out_ref.at[me], local_sem)
        mine.start()
        copies = []
        for k in range(1, N_DEV):
            px = 1 - x if k & 4 else x
            py = 1 - y if k & 2 else y
            pc = 1 - c if k & 1 else c
            peer = 4 * px + 2 * py + pc
            cp = pltpu.make_async_remote_copy(
                src_ref=in_ref.at[peer], dst_ref=out_ref.at[me], send_sem=send_sems.at[k - 1],
                recv_sem=recv_sems.at[k - 1], device_id=(px, py, pc), device_id_type=MESH)
            cp.start()
            copies.append((cp, peer))
        for k, (cp, peer) in enumerate(copies):
            pltpu.make_async_remote_copy(
                src_ref=in_ref.at[peer], dst_ref=out_ref.at[peer], send_sem=send_sems.at[k],
                recv_sem=recv_sems.at[k], device_id=(x, y, c), device_id_type=MESH).wait_recv()
        for cp, _ in copies:
            cp.wait_send()
        mine.wait()

    return pl.pallas_call(
        body, name=name, out_shape=jax.ShapeDtypeStruct(slots.shape, slots.dtype),
        in_specs=[pl.BlockSpec(memory_space=pl.ANY)], out_specs=pl.BlockSpec(memory_space=pl.ANY),
        scratch_shapes=[pltpu.SemaphoreType.DMA((7,)), pltpu.SemaphoreType.DMA((7,)), pltpu.SemaphoreType.DMA],
    )(slots)


def _adamw_math(w, g, m, v):
    m = ADAM_B1 * m + (1.0 - ADAM_B1) * g
    v = ADAM_B2 * v + (1.0 - ADAM_B2) * (g * g)
    m_hat = m / (1.0 - ADAM_B1 ** ADAM_STEP)
    v_hat = v / (1.0 - ADAM_B2 ** ADAM_STEP)
    delta = -ADAM_LR * (m_hat / (jnp.sqrt(v_hat) + ADAM_EPS) + ADAM_WD * w)
    return delta, m, v


def _sum8(parts, name, tr=512):
    _, r, c_dim = parts.shape
    tr = _tile(r, tr, BF16_SUBLANES)

    def body(p_ref, o_ref):
        acc = p_ref[0].astype(F32)
        for k in range(1, N_DEV):
            acc = acc + p_ref[k].astype(F32)
        o_ref[...] = acc

    return pl.pallas_call(
        body, name=name, grid=(r // tr,), in_specs=[pl.BlockSpec((N_DEV, tr, c_dim), lambda i: (0, i, 0))],
        out_specs=pl.BlockSpec((tr, c_dim), lambda i: (i, 0)),
        out_shape=jax.ShapeDtypeStruct((r, c_dim), F32), compiler_params=_params("parallel"),
    )(parts)


def _sum8_adamw(parts, w, m, v, name, tr=512):
    _, r, c_dim = parts.shape
    tr = _tile(r, tr, BF16_SUBLANES)

    def body(p_ref, w_ref, m_ref, v_ref, g_ref, d_ref, nm_ref, nv_ref):
        g = p_ref[0].astype(F32)
        for k in range(1, N_DEV):
            g = g + p_ref[k].astype(F32)
        g_ref[...] = g
        d_ref[...], nm_ref[...], nv_ref[...] = _adamw_math(w_ref[...], g, m_ref[...], v_ref[...])

    row = pl.BlockSpec((tr, c_dim), lambda i: (i, 0))
    out = jax.ShapeDtypeStruct((r, c_dim), F32)
    return pl.pallas_call(
        body, name=name, grid=(r // tr,), in_specs=[pl.BlockSpec((N_DEV, tr, c_dim), lambda i: (0, i, 0)), row, row, row],
        out_specs=[row] * 4, out_shape=[out] * 4, compiler_params=_params("parallel"),
    )(parts, w, m, v)


def _adamw(g, w, m, v, name):
    r, c_dim = g.shape

    def body(g_ref, w_ref, m_ref, v_ref, d_ref, nm_ref, nv_ref):
        d_ref[...], nm_ref[...], nv_ref[...] = _adamw_math(w_ref[...], g_ref[...], m_ref[...], v_ref[...])

    out = jax.ShapeDtypeStruct((r, c_dim), F32)
    return pl.pallas_call(body, name=name, out_shape=[out] * 3)(g, w, m, v)


def _pack_rows(arrays, dtype, row_unit):
    chunks, offs, r0 = [], [], 0
    for a in arrays:
        flat = a.reshape(-1).astype(dtype)
        rows = -(-flat.shape[0] // (LANES * row_unit)) * row_unit
        flat = jnp.pad(flat, (0, rows * LANES - flat.shape[0]))
        chunks.append(flat.reshape(rows, LANES))
        offs.append((r0, rows))
        r0 += rows
    return jnp.concatenate(chunks, axis=0), offs


def _unpack_rows(packed, offs, shapes):
    out = []
    for (r0, rows), shape in zip(offs, shapes):
        n = math.prod(shape)
        blk = packed[..., r0:r0 + rows, :]
        blk = blk.reshape(packed.shape[:-2] + (rows * LANES,))[..., :n]
        out.append(blk.reshape(packed.shape[:-2] + tuple(shape)))
    return out


def _full_from_slots(blk, col_sharded):
    _, r, c = blk.shape
    if col_sharded:
        return blk.transpose(1, 0, 2).reshape(r, N_DEV * c)
    return blk.reshape(N_DEV * r, c)


def _slots_from_full(full, col_sharded):
    r, c = full.shape
    if col_sharded:
        return full.reshape(r, N_DEV, c // N_DEV).transpose(1, 0, 2)
    return full.reshape(N_DEV, r // N_DEV, c)


def _ffn_fwd(h, g, w_gu, w_d, tag):
    n = _rms_fwd(h, g, f"{tag}_norm")
    gu = _mm(n, w_gu, BF16, f"{tag}_up")
    a = _swiglu_fwd(gu, f"{tag}_swiglu")
    h_out = _mm(a, w_d, F32, f"{tag}_down", res=h, alpha=FFN_RES_WEIGHT)
    return h_out, (h, n, gu, a)


def _ffn_bwd(dh_out, saved, g, w_gu_t, w_d_t, tag):
    h, n, gu, a = saved
    dyh = (FFN_RES_WEIGHT * dh_out).astype(BF16)
    dw_d = _mm_tn(a, dyh, f"{tag}_dw_down")
    da = _mm(dyh, w_d_t, BF16, f"{tag}_da")
    dgate, dup = _swiglu_bwd(da, gu, f"{tag}_dswiglu")
    dgu = jnp.concatenate([dgate, dup], axis=1)
    dw_gu = _mm_tn(n, dgu, f"{tag}_dw_up")
    dn = _mm(dgu, w_gu_t, F32, f"{tag}_dn")
    dh, dg = _rms_bwd(dn, h, g, f"{tag}_dnorm", dres=dh_out)
    return dh, dg, dw_gu, dw_d


def _local_step(x3, mem3, target3, wts, small):
    nb, s, d = x3.shape
    m_len = mem3.shape[1]
    t = nb * s
    nc = s // SSM_CHUNK
    di = small["ssm_norm"].shape[1]
    hs = di // SSM_HEAD_DIM
    cc = di + 2 * SSM_GROUPS * SSM_STATE
    x, mem, target = x3.reshape(t, d), mem3.reshape(nb * m_len, d), target3.reshape(t, d)

    sizes = (d, d, d, di, cc, hs, d, d)
    offs = [0]
    for sz in sizes:
        offs.append(offs[-1] + sz)
    w_in = wts["w_in"]
    main_cols = [w_in[:, offs[i]:offs[i + 1]] for i in (0, 1, 2, 3, 4, 6, 7)]
    w_main = jnp.concatenate(main_cols, axis=1)
    w_dt = jnp.pad(w_in[:, offs[5]:offs[6]], ((0, 0), (0, LANES - hs)))
    z_col0, xbc_col0 = 3 * d, 3 * d + di
    ga_blk, gb_blk = (3 * d + di + cc) // d, (4 * d + di + cc) // d

    pad_vec = lambda v: jnp.pad(v.reshape(1, -1), ((0, 0), (0, LANES - hs)))
    prow = jnp.concatenate([pad_vec(small["ssm_dt_bias"]), pad_vec(small["ssm_a_log"]), pad_vec(small["ssm_d"]),
                            jnp.zeros((5, LANES), F32)], axis=0)
    pcol = prow.T
    e_mat = (lax.broadcasted_iota(jnp.int32, (LANES, di), 0)
             == lax.broadcasted_iota(jnp.int32, (LANES, di), 1) // SSM_HEAD_DIM).astype(F32)
    conv_a_w8 = jnp.pad(small["conv_a_w"][0], ((0, 8 - CONV_A_K), (0, 0)))
    ssm_conv_w8 = jnp.pad(small["ssm_conv_w"][0], ((0, 8 - SSM_CONV_K), (0, 0)))

    h1, ffn1_saved = _ffn_fwd(x, small["ffn1_norm"], wts["ffn1_w_gate_up"], wts["ffn1_w_down"], "ffn1")
    u = _rms_fwd(h1, small["mix_norm"], "mix_norm")
    proj = _mm(u, w_main, BF16, "in_proj")
    dtr = _mm(u, w_dt, F32, "in_proj_dt")
    yap = _conv_a_fwd(proj, conv_a_w8, nb, s, d, "conv_a")
    y_a = _mm(yap, wts["w_out_a"], BF16, "out_a")
    xc = _conv_s_fwd(proj, xbc_col0, ssm_conv_w8, small["ssm_conv_b"], nb, s, cc, "conv_s")
    dtrt = dtr.T
    y_ssd, sprev = _ssd_fwd(xc, dtr, dtrt, prow, pcol, e_mat, nb, nc, di, "ssd")
    ygn = _gate_norm_fwd(y_ssd, proj, z_col0, small["ssm_norm"], di, "gate_norm")
    y_b = _mm(ygn, wts["w_out_ssm"], BF16, "out_ssm")
    merged = _merge_fwd(y_a, y_b, proj, ga_blk, gb_blk, d, "merge")
    h2 = _mm(merged, wts["w_mix_out"], F32, "mix_out", res=h1)
    un = _rms_fwd(h2, small["xattn_norm"], "xattn_norm")
    mn = _rms_fwd(mem, small["mem_norm"], "mem_norm")
    q = _mm(un, wts["w_q"], BF16, "q_proj")
    kv = _mm(mn, wts["w_kv"], BF16, "kv_proj")
    o = _xattn_fwd(q, kv, nb, s, m_len, d, "xattn")
    h3 = _mm(o, wts["w_o_x"], F32, "o_proj", res=h2)
    h4, ffn2_saved = _ffn_fwd(h3, small["ffn2_norm"], wts["ffn2_w_gate_up"], wts["ffn2_w_down"], "ffn2")
    loss_vec, dh4, dg_final = _loss_head(h4, small["final_norm"].reshape(1, d), target, "loss_head")

    grads = {"final_norm": dg_final.reshape(d)}
    dh3, grads["ffn2_norm"], grads["ffn2_w_gate_up"], grads["ffn2_w_down"] = _ffn_bwd(
        dh4, ffn2_saved, small["ffn2_norm"], wts["ffn2_w_gate_up"].T, wts["ffn2_w_down"].T, "ffn2")
    dh3b = dh3.astype(BF16)
    grads["w_o_x"] = _mm_tn(o, dh3b, "dw_o")
    do = _mm(dh3b, wts["w_o_x"].T, BF16, "d_o")
    dq, dk, dv = _xattn_bwd(do, q, kv, nb, s, m_len, d, "xattn_bwd")
    dkv = jnp.concatenate([dk, dv], axis=1).astype(BF16)
    grads["w_q"] = _mm_tn(un, dq, "dw_q")
    grads["w_kv"] = _mm_tn(mn, dkv, "dw_kv")
    dun = _mm(dq, wts["w_q"].T, F32, "d_un")
    dmn = _mm(dkv, wts["w_kv"].T, F32, "d_mn")
    _, grads["mem_norm"] = _rms_bwd(dmn, mem, small["mem_norm"], "mem_dnorm")
    dh2, grads["xattn_norm"] = _rms_bwd(dun, h2, small["xattn_norm"], "xattn_dnorm", dres=dh3)
    dh2b = dh2.astype(BF16)
    grads["w_mix_out"] = _mm_tn(merged, dh2b, "dw_mix")
    dmerged = _mm(dh2b, wts["w_mix_out"].T, BF16, "d_merged")
    dya, dyb, dga, dgb = _merge_bwd(dmerged, y_a, y_b, proj, ga_blk, gb_blk, d, "merge_bwd")
    grads["w_out_a"] = _mm_tn(yap, dya, "dw_out_a")
    grads["w_out_ssm"] = _mm_tn(ygn, dyb, "dw_out_ssm")
    dyap = _mm(dya, wts["w_out_a"].T, BF16, "d_yap")
    dygn = _mm(dyb, wts["w_out_ssm"].T, BF16, "d_ygn")
    dab, dac, dav, dconv_a = _conv_a_bwd(dyap, proj, conv_a_w8, nb, s, d, "conv_a_bwd")
    dy_ssd, dz, grads["ssm_norm"] = _gate_norm_bwd(dygn, y_ssd, proj, z_col0, small["ssm_norm"], di, "gate_norm_bwd")
    dxc, ddtr, ssd_sums = _ssd_bwd(dy_ssd, xc, dtr, dtrt, prow, pcol, e_mat, sprev, nb, nc, di, "ssd_bwd")
    dxbc, dconv_s, grads["ssm_conv_b"] = _conv_s_bwd(dxc, proj, xbc_col0, ssm_conv_w8, small["ssm_conv_b"], nb, s, cc, "conv_s_bwd")
    dproj = jnp.concatenate([dab, dac, dav, dz, dxbc, dga, dgb], axis=1)
    ddtr_b = ddtr.astype(BF16)
    dw_main = _mm_tn(u, dproj, "dw_in", tn=1024)
    dw_dt = _mm_tn(u, ddtr_b, "dw_in_dt")
    du_main = _mm(dproj, w_main.T, F32, "d_u", tk=1024)
    du = _mm(ddtr_b, w_dt.T, F32, "d_u_dt", res=du_main)
    dh1, grads["mix_norm"] = _rms_bwd(du, h1, small["mix_norm"], "mix_dnorm", dres=dh2)
    dx, grads["ffn1_norm"], grads["ffn1_w_gate_up"], grads["ffn1_w_down"] = _ffn_bwd(
        dh1, ffn1_saved, small["ffn1_norm"], wts["ffn1_w_gate_up"].T, wts["ffn1_w_down"].T, "ffn1")

    mo = [0]
    for i in (0, 1, 2, 3, 4, 6, 7):
        mo.append(mo[-1] + sizes[i])
    seg = lambda k: dw_main[:, mo[k]:mo[k + 1]]
    grads["w_in"] = jnp.concatenate([seg(0), seg(1), seg(2), seg(3), seg(4), dw_dt[:, :hs], seg(5), seg(6)], axis=1)
    grads["conv_a_w"] = dconv_a[:CONV_A_K]
    grads["ssm_conv_w"] = dconv_s[:SSM_CONV_K]
    grads["ssm_dt_bias"] = ssd_sums[0:1, :hs]
    grads["ssm_a_log"] = ssd_sums[1:2, :hs]
    grads["ssm_d"] = ssd_sums[2:3, :hs]
    return loss_vec[0, 0], dx.reshape(nb, s, d), grads


def _step(inputs):
    w = {k: inputs[k] for k in WEIGHT_ORDER}
    mom = {k: inputs["m_" + k] for k in WEIGHT_ORDER}
    vel = {k: inputs["v_" + k] for k in WEIGHT_ORDER}
    me = 4 * lax.axis_index("x") + 2 * lax.axis_index("y") + lax.axis_index("c")

    big_shapes = [w[k].shape[1:] for k in BIG_WEIGHTS]
    packed_w, big_offs = _pack_rows([w[k][0] for k in BIG_WEIGHTS], BF16, BF16_SUBLANES)
    gathered = _all_gather(packed_w, "gather_weights")
    blocks = _unpack_rows(gathered, big_offs, big_shapes)
    full = {k: _full_from_slots(b, k in COL_SHARDED) for k, b in zip(BIG_WEIGHTS, blocks)}

    small = {k: w[k] for k in SMALL_REPLICATED if k != "final_norm"}
    small["final_norm"] = w["final_norm"]
    conv_shapes = [w[k].shape[1:] for k in SMALL_SHARDED]
    packed_c, conv_offs = _pack_rows([w[k][0] for k in SMALL_SHARDED], F32, 8)
    conv_blocks = _unpack_rows(_all_gather(packed_c, "gather_conv_weights"), conv_offs, conv_shapes)
    for k, b in zip(SMALL_SHARDED, conv_blocks):
        small[k] = _full_from_slots(b, True)[None]

    loss_local, grad_x, grads = _local_step(inputs["x"], inputs["mem"], inputs["loss_target"], full, small)
    loss = lax.psum(loss_local, AXES)

    slot_arrays = [_slots_from_full(grads[k], k in COL_SHARDED) for k in BIG_WEIGHTS]
    chunks, grad_offs, r0 = [], [], 0
    for a in slot_arrays:
        flat = a.reshape(N_DEV, -1).astype(BF16)
        rows = -(-flat.shape[1] // (LANES * BF16_SUBLANES)) * BF16_SUBLANES
        flat = jnp.pad(flat, ((0, 0), (0, rows * LANES - flat.shape[1])))
        chunks.append(flat.reshape(N_DEV, rows, LANES))
        grad_offs.append((r0, rows))
        r0 += rows
    assert grad_offs == big_offs
    received = _exchange(jnp.concatenate(chunks, axis=1), "exchange_grads")
    pw, _ = _pack_rows([w[k][0] for k in BIG_WEIGHTS], F32, BF16_SUBLANES)
    pm, _ = _pack_rows([mom[k][0] for k in BIG_WEIGHTS], F32, BF16_SUBLANES)
    pv, _ = _pack_rows([vel[k][0] for k in BIG_WEIGHTS], F32, BF16_SUBLANES)
    big_out = [_unpack_rows(a, big_offs, big_shapes) for a in _sum8_adamw(received, pw, pm, pv, "sum_adamw")]
    out = {}
    for i, k in enumerate(BIG_WEIGHTS):
        out[k] = tuple(o[i][None] for o in big_out)

    small_names = SMALL_REPLICATED + SMALL_SHARDED
    packed_g, small_offs = _pack_rows([grads[k] for k in small_names], F32, 8)
    total = _sum8(_all_gather(packed_g, "gather_small_grads"), "sum_small_grads")
    full_grads = _unpack_rows(total, small_offs, [grads[k].shape for k in small_names])
    mine = {}
    for k, g in zip(small_names, full_grads):
        if k in SMALL_SHARDED:
            c_loc = w[k].shape[2]
            g = lax.dynamic_slice_in_dim(g, me * c_loc, c_loc, axis=1)
        mine[k] = g.reshape(w[k].shape)
    sg, s_offs = _pack_rows([mine[k] for k in small_names], F32, 8)
    sw, _ = _pack_rows([w[k] for k in small_names], F32, 8)
    sm, _ = _pack_rows([mom[k] for k in small_names], F32, 8)
    sv, _ = _pack_rows([vel[k] for k in small_names], F32, 8)
    s_shapes = [w[k].shape for k in small_names]
    small_out = [_unpack_rows(a, s_offs, s_shapes) for a in _adamw(sg, sw, sm, sv, "adamw_small")]
    for i, k in enumerate(small_names):
        out[k] = (mine[k],) + tuple(o[i] for o in small_out)

    res = [loss, grad_x]
    for j in range(4):
        res += [out[k][j] for k in WEIGHT_ORDER]
    return tuple(res)


def kernel(x, mem, ffn1_norm, ffn1_w_gate_up, ffn1_w_down, mix_norm, w_in, conv_a_w, w_out_a, ssm_conv_w, ssm_conv_b, ssm_dt_bias, ssm_a_log, ssm_d, ssm_norm, w_out_ssm, w_mix_out, xattn_norm, mem_norm, w_q, w_kv, w_o_x, ffn2_norm, ffn2_w_gate_up, ffn2_w_down, final_norm, loss_target, m_ffn1_norm, m_ffn1_w_gate_up, m_ffn1_w_down, m_mix_norm, m_w_in, m_conv_a_w, m_w_out_a, m_ssm_conv_w, m_ssm_conv_b, m_ssm_dt_bias, m_ssm_a_log, m_ssm_d, m_ssm_norm, m_w_out_ssm, m_w_mix_out, m_xattn_norm, m_mem_norm, m_w_q, m_w_kv, m_w_o_x, m_ffn2_norm, m_ffn2_w_gate_up, m_ffn2_w_down, m_final_norm, v_ffn1_norm, v_ffn1_w_gate_up, v_ffn1_w_down, v_mix_norm, v_w_in, v_conv_a_w, v_w_out_a, v_ssm_conv_w, v_ssm_conv_b, v_ssm_dt_bias, v_ssm_a_log, v_ssm_d, v_ssm_norm, v_w_out_ssm, v_w_mix_out, v_xattn_norm, v_mem_norm, v_w_q, v_w_kv, v_w_o_x, v_ffn2_norm, v_ffn2_w_gate_up, v_ffn2_w_down, v_final_norm):
    return _step(dict(locals()))
```

```python
import functools
import math

import jax
import jax.numpy as jnp
from jax import lax
from jax.experimental import pallas as pl
from jax.experimental.pallas import tpu as pltpu

F32, BF16 = jnp.float32, jnp.bfloat16
HI = lax.Precision.HIGHEST
MESH = pl.DeviceIdType.MESH
AXES = ("x", "y", "c")
N_DEV = 8

EPS = 1e-6
FFN_RES_WEIGHT = 0.5
SSM_HEAD_DIM = 64
SSM_GROUPS = 4
SSM_STATE = 128
SSM_CHUNK = 128
CONV_A_K = 3
SSM_CONV_K = 4
XATTN_HEADS = 4
ADAM_LR, ADAM_B1, ADAM_B2, ADAM_EPS, ADAM_WD, ADAM_STEP = 1e-3, 0.9, 0.999, 1e-8, 0.01, 10

LANES = 128
BF16_SUBLANES = 16
VMEM_LIMIT_BYTES = 56 * 2 ** 20
NEG_BIG = -1e30

BIG_WEIGHTS = ("ffn1_w_gate_up", "ffn1_w_down", "w_in", "w_out_a", "w_out_ssm", "w_mix_out",
               "w_q", "w_kv", "w_o_x", "ffn2_w_gate_up", "ffn2_w_down")
COL_SHARDED = ("ffn1_w_gate_up", "w_in", "w_kv", "ffn2_w_gate_up")
SMALL_REPLICATED = ("ffn1_norm", "mix_norm", "ssm_conv_b", "ssm_dt_bias", "ssm_a_log", "ssm_d", "ssm_norm",
                    "xattn_norm", "mem_norm", "ffn2_norm", "final_norm")
SMALL_SHARDED = ("conv_a_w", "ssm_conv_w")
WEIGHT_ORDER = ("ffn1_norm", "ffn1_w_gate_up", "ffn1_w_down", "mix_norm", "w_in", "conv_a_w", "w_out_a",
                "ssm_conv_w", "ssm_conv_b", "ssm_dt_bias", "ssm_a_log", "ssm_d", "ssm_norm", "w_out_ssm",
                "w_mix_out", "xattn_norm", "mem_norm", "w_q", "w_kv", "w_o_x", "ffn2_norm", "ffn2_w_gate_up",
                "ffn2_w_down", "final_norm")


def _tile(dim, pref, unit):
    best = None
    t = unit
    while t <= min(dim, pref):
        if dim % t == 0:
            best = t
        t += unit
    return best if best is not None else dim


def _params(*sem):
    return pltpu.CompilerParams(dimension_semantics=sem, vmem_limit_bytes=VMEM_LIMIT_BYTES)


def _silu(x):
    return x * jax.nn.sigmoid(x)


def _dsilu(x):
    s = jax.nn.sigmoid(x)
    return s * (1.0 + x * (1.0 - s))


def _softplus(x):
    return jnp.maximum(x, 0.0) + jnp.log(1.0 + jnp.exp(-jnp.abs(x)))


def _dot(a, b, dims=(((1,), (0,)), ((), ())), precision=None):
    return lax.dot_general(a, b, dims, preferred_element_type=F32, precision=precision)


def _stack_rows(rows, width):
    r_idx = lax.broadcasted_iota(jnp.int32, (8, width), 0)
    acc = jnp.zeros((8, width), F32)
    for k, row in enumerate(rows):
        acc = jnp.where(r_idx == k, row, acc)
    return acc


NT = (((1,), (1,)), ((), ()))
TN = (((0,), (0,)), ((), ()))


def _mm(a, b, out_dtype, name, res=None, alpha=1.0, nt=False, tm=1024, tn=512, tk=2816):
    pieces = list(a) if isinstance(a, (list, tuple)) else [a]
    m = pieces[0].shape[0]
    k = sum(p.shape[1] for p in pieces)
    n = b.shape[0] if nt else b.shape[1]
    assert (b.shape[1] if nt else b.shape[0]) == k
    tm, tn = _tile(m, tm, 8), _tile(n, tn, LANES)
    tk = _tile(math.gcd(*[p.shape[1] for p in pieces]), tk, LANES)
    nk = k // tk
    starts, s0 = [], 0
    for p in pieces:
        starts.append((s0, p.shape[1] // tk))
        s0 += p.shape[1] // tk
    n_p = len(pieces)

    def body(*refs):
        a_refs, b_ref = refs[:n_p], refs[n_p]
        r_ref = refs[n_p + 1] if res is not None else None
        o_ref = refs[n_p + 1 + (res is not None)]
        scr = refs[n_p + 2 + (res is not None):]

        def finish(acc):
            acc = alpha * acc if alpha != 1.0 else acc
            if r_ref is not None:
                acc = r_ref[...] + acc
            o_ref[...] = acc.astype(out_dtype)

        def product(a_ref):
            return _dot(a_ref[...].astype(BF16), b_ref[...].astype(BF16), NT if nt else (((1,), (0,)), ((), ())))

        if nk == 1:
            finish(product(a_refs[0]))
            return
        acc_ref = scr[0]
        kk = pl.program_id(2)
        for (s, cnt), a_ref in zip(starts, a_refs):
            if s == 0:
                @pl.when(kk == 0)
                def _():
                    acc_ref[...] = product(a_ref)

                @pl.when(jnp.logical_and(kk > 0, kk < cnt))
                def _():
                    acc_ref[...] += product(a_ref)
            else:
                @pl.when(jnp.logical_and(kk >= s, kk < s + cnt))
                def _():
                    acc_ref[...] += product(a_ref)

        @pl.when(kk == nk - 1)
        def _():
            finish(acc_ref[...])

    def a_spec(s, cnt):
        return pl.BlockSpec((tm, tk), lambda i, j, kk: (i, jnp.clip(kk - s, 0, cnt - 1)))

    in_specs = [a_spec(s, cnt) for s, cnt in starts]
    in_specs.append(pl.BlockSpec((tn, tk), lambda i, j, kk: (j, kk)) if nt else pl.BlockSpec((tk, tn), lambda i, j, kk: (kk, j)))
    args = pieces + [b]
    if res is not None:
        in_specs.append(pl.BlockSpec((tm, tn), lambda i, j, kk: (i, j)))
        args.append(res)
    return pl.pallas_call(
        body, name=name, grid=(m // tm, n // tn, nk), in_specs=in_specs,
        out_specs=pl.BlockSpec((tm, tn), lambda i, j, kk: (i, j)),
        out_shape=jax.ShapeDtypeStruct((m, n), out_dtype),
        scratch_shapes=[pltpu.VMEM((tm, tn), F32)] if nk > 1 else [],
        compiler_params=_params("parallel", "parallel", "arbitrary"),
    )(*args)


def _mm_tn(x, dy, name, out_dtype=BF16, alpha=1.0, tko=1408, tn=1024, tt=512):
    t, k = x.shape
    n = dy.shape[1]
    tko, tn, tt = _tile(k, tko, LANES), _tile(n, tn, LANES), _tile(t, tt, 8)
    nt_steps = t // tt

    def body(x_ref, dy_ref, o_ref, acc_ref):
        part = _dot(x_ref[...].astype(BF16), dy_ref[...].astype(BF16), TN)
        step = pl.program_id(2)

        @pl.when(step == 0)
        def _():
            acc_ref[...] = part

        @pl.when(step > 0)
        def _():
            acc_ref[...] += part

        @pl.when(step == nt_steps - 1)
        def _():
            acc = acc_ref[...]
            o_ref[...] = (alpha * acc if alpha != 1.0 else acc).astype(out_dtype)

    return pl.pallas_call(
        body, name=name, grid=(k // tko, n // tn, nt_steps),
        in_specs=[pl.BlockSpec((tt, tko), lambda i, j, s: (s, i)), pl.BlockSpec((tt, tn), lambda i, j, s: (s, j))],
        out_specs=pl.BlockSpec((tko, tn), lambda i, j, s: (i, j)),
        out_shape=jax.ShapeDtypeStruct((k, n), out_dtype),
        scratch_shapes=[pltpu.VMEM((tko, tn), F32)],
        compiler_params=_params("parallel", "parallel", "arbitrary"),
    )(x, dy)


def _rms_fwd(x, g, name, tt=512):
    t, d = x.shape
    tt = _tile(t, tt, 8)

    def body(x_ref, g_ref, o_ref):
        xv = x_ref[...]
        r = lax.rsqrt(jnp.mean(xv * xv, axis=-1, keepdims=True) + EPS)
        o_ref[...] = (xv * r * g_ref[...]).astype(BF16)

    return pl.pallas_call(
        body, name=name, grid=(t // tt,),
        in_specs=[pl.BlockSpec((tt, d), lambda i: (i, 0)), pl.BlockSpec((1, d), lambda i: (0, 0))],
        out_specs=pl.BlockSpec((tt, d), lambda i: (i, 0)),
        out_shape=jax.ShapeDtypeStruct((t, d), BF16), compiler_params=_params("parallel"),
    )(x, g)


def _rms_bwd(dn, x, g, name, dres=None, tt=512):
    t, d = x.shape
    tt = _tile(t, tt, 8)

    def body(*refs):
        if dres is None:
            dn_ref, x_ref, g_ref, dx_ref, dg_ref = refs
            r_ref = None
        else:
            dn_ref, x_ref, g_ref, r_ref, dx_ref, dg_ref = refs
        xv, dnv = x_ref[...], dn_ref[...].astype(F32)
        r = lax.rsqrt(jnp.mean(xv * xv, axis=-1, keepdims=True) + EPS)
        xh = xv * r
        gy = dnv * g_ref[...]
        dx = r * (gy - xh * jnp.mean(gy * xh, axis=-1, keepdims=True))
        if r_ref is not None:
            dx = dx + r_ref[...]
        dx_ref[...] = dx
        part = jnp.sum(dnv * xh, axis=0, keepdims=True)

        @pl.when(pl.program_id(0) == 0)
        def _():
            dg_ref[...] = part

        @pl.when(pl.program_id(0) > 0)
        def _():
            dg_ref[...] += part

    row = pl.BlockSpec((tt, d), lambda i: (i, 0))
    vec = pl.BlockSpec((1, d), lambda i: (0, 0))
    in_specs, args = [row, row, vec], [dn, x, g]
    if dres is not None:
        in_specs.append(row)
        args.append(dres)
    return pl.pallas_call(
        body, name=name, grid=(t // tt,), in_specs=in_specs, out_specs=[row, vec],
        out_shape=[jax.ShapeDtypeStruct((t, d), F32), jax.ShapeDtypeStruct((1, d), F32)],
        compiler_params=_params("arbitrary"),
    )(*args)


def _swiglu_fwd(gu, name, tt=512):
    t, f2 = gu.shape
    f = f2 // 2
    tt = _tile(t, tt, 8)
    tf = _tile(f, 1408, LANES)
    nf = f // tf

    def body(g_ref, u_ref, o_ref):
        o_ref[...] = (_silu(g_ref[...].astype(F32)) * u_ref[...].astype(F32)).astype(BF16)

    return pl.pallas_call(
        body, name=name, grid=(t // tt, nf),
        in_specs=[pl.BlockSpec((tt, tf), lambda i, j: (i, j)), pl.BlockSpec((tt, tf), lambda i, j: (i, j + nf))],
        out_specs=pl.BlockSpec((tt, tf), lambda i, j: (i, j)),
        out_shape=jax.ShapeDtypeStruct((t, f), BF16), compiler_params=_params("parallel", "parallel"),
    )(gu, gu)


def _swiglu_bwd(da, gu, name, tt=512):
    t, f2 = gu.shape
    f = f2 // 2
    tt = _tile(t, tt, 8)
    tf = _tile(f, 1408, LANES)
    nf = f // tf

    def body(da_ref, g_ref, u_ref, o_ref):
        dav, gv = da_ref[...].astype(F32), g_ref[...].astype(F32)

        @pl.when(pl.program_id(1) < nf)
        def _():
            o_ref[...] = (dav * u_ref[...].astype(F32) * _dsilu(gv)).astype(BF16)

        @pl.when(pl.program_id(1) >= nf)
        def _():
            o_ref[...] = (dav * _silu(gv)).astype(BF16)

    return pl.pallas_call(
        body, name=name, grid=(t // tt, 2 * nf),
        in_specs=[pl.BlockSpec((tt, tf), lambda i, j: (i, j % nf)), pl.BlockSpec((tt, tf), lambda i, j: (i, j % nf)),
                  pl.BlockSpec((tt, tf), lambda i, j: (i, j % nf + nf))],
        out_specs=pl.BlockSpec((tt, tf), lambda i, j: (i, j)),
        out_shape=jax.ShapeDtypeStruct((t, f2), BF16), compiler_params=_params("parallel", "parallel"),
    )(da, gu, gu)


def _merge_fwd(ya, yb, proj, ga_blk, gb_blk, d, name, tt=512):
    t = ya.shape[0]
    tt = _tile(t, tt, 8)

    def body(ya_ref, yb_ref, ga_ref, gb_ref, o_ref):
        o_ref[...] = (jax.nn.sigmoid(ga_ref[...].astype(F32)) * ya_ref[...].astype(F32)
                      + jax.nn.sigmoid(gb_ref[...].astype(F32)) * yb_ref[...].astype(F32)).astype(BF16)

    row = pl.BlockSpec((tt, d), lambda i: (i, 0))
    return pl.pallas_call(
        body, name=name, grid=(t // tt,),
        in_specs=[row, row, pl.BlockSpec((tt, d), lambda i: (i, ga_blk)), pl.BlockSpec((tt, d), lambda i: (i, gb_blk))],
        out_specs=row, out_shape=jax.ShapeDtypeStruct((t, d), BF16), compiler_params=_params("parallel"),
    )(ya, yb, proj, proj)


def _merge_bwd(dm, ya, yb, proj, ga_blk, gb_blk, d, name, tt=512):
    t = ya.shape[0]
    tt = _tile(t, tt, 8)

    def body(dm_ref, ya_ref, yb_ref, ga_ref, gb_ref, dya_ref, dyb_ref, dga_ref, dgb_ref):
        dmv = dm_ref[...].astype(F32)
        sa, sb = jax.nn.sigmoid(ga_ref[...].astype(F32)), jax.nn.sigmoid(gb_ref[...].astype(F32))
        dya_ref[...] = (dmv * sa).astype(BF16)
        dyb_ref[...] = (dmv * sb).astype(BF16)
        dga_ref[...] = (dmv * ya_ref[...].astype(F32) * sa * (1.0 - sa)).astype(BF16)
        dgb_ref[...] = (dmv * yb_ref[...].astype(F32) * sb * (1.0 - sb)).astype(BF16)

    row = pl.BlockSpec((tt, d), lambda i: (i, 0))
    out = jax.ShapeDtypeStruct((t, d), BF16)
    return pl.pallas_call(
        body, name=name, grid=(t // tt,),
        in_specs=[row, row, row, pl.BlockSpec((tt, d), lambda i: (i, ga_blk)), pl.BlockSpec((tt, d), lambda i: (i, gb_blk))],
        out_specs=[row] * 4, out_shape=[out] * 4, compiler_params=_params("parallel"),
    )(dm, ya, yb, proj, proj)


def _loss_head(h, g, target, name, tt=512):
    t, d = h.shape
    tt = _tile(t, tt, 8)

    def body(h_ref, g_ref, tg_ref, loss_ref, dh_ref, dg_ref):
        xv = h_ref[...]
        r = lax.rsqrt(jnp.mean(xv * xv, axis=-1, keepdims=True) + EPS)
        xh = xv * r
        err = xh * g_ref[...] - tg_ref[...]
        dout = err * (1.0 / d)
        gy = dout * g_ref[...]
        dh_ref[...] = r * (gy - xh * jnp.mean(gy * xh, axis=-1, keepdims=True))
        dg_part = jnp.sum(dout * xh, axis=0, keepdims=True)
        loss_part = jnp.full((1, LANES), 0.5 / d, F32) * jnp.sum(err * err)

        @pl.when(pl.program_id(0) == 0)
        def _():
            dg_ref[...] = dg_part
            loss_ref[...] = loss_part

        @pl.when(pl.program_id(0) > 0)
        def _():
            dg_ref[...] += dg_part
            loss_ref[...] += loss_part

    row = pl.BlockSpec((tt, d), lambda i: (i, 0))
    vec = pl.BlockSpec((1, d), lambda i: (0, 0))
    return pl.pallas_call(
        body, name=name, grid=(t // tt,), in_specs=[row, vec, row],
        out_specs=[pl.BlockSpec((1, LANES), lambda i: (0, 0)), row, vec],
        out_shape=[jax.ShapeDtypeStruct((1, LANES), F32), jax.ShapeDtypeStruct((t, d), F32), jax.ShapeDtypeStruct((1, d), F32)],
        compiler_params=_params("arbitrary"),
    )(h, g, target)


def _shift_down(x, k, t_idx):
    if k == 0:
        return x
    return jnp.where(t_idx >= k, pltpu.roll(x, k, 0), 0.0)


def _shift_up(x, k, t_idx, s):
    if k == 0:
        return x
    return jnp.where(t_idx < s - k, pltpu.roll(x, s - k, 0), 0.0)


def _conv_a_fwd(proj, w, nb, s, d, name, cb=256):
    cb = _tile(d, cb, LANES)
    nd = d // cb

    def body(b_ref, c_ref, v_ref, w_ref, o_ref):
        t_idx = lax.broadcasted_iota(jnp.int32, (s, cb), 0)
        cv = c_ref[...].astype(F32) * v_ref[...].astype(F32)
        cc = sum(w_ref[k:k + 1, :] * _shift_down(cv, CONV_A_K - 1 - k, t_idx) for k in range(CONV_A_K))
        o_ref[...] = (b_ref[...].astype(F32) * cc).astype(BF16)

    def col(off):
        return pl.BlockSpec((s, cb), lambda b, j: (b, j + off * nd))

    return pl.pallas_call(
        body, name=name, grid=(nb, nd), in_specs=[col(0), col(1), col(2), pl.BlockSpec((8, cb), lambda b, j: (0, j))],
        out_specs=pl.BlockSpec((s, cb), lambda b, j: (b, j)),
        out_shape=jax.ShapeDtypeStruct((nb * s, d), BF16), compiler_params=_params("parallel", "parallel"),
    )(proj, proj, proj, w)


def _conv_a_bwd(dy, proj, w, nb, s, d, name, cb=256):
    cb = _tile(d, cb, LANES)
    nd = d // cb

    def body(dy_ref, b_ref, c_ref, v_ref, w_ref, db_ref, dc_ref, dv_ref, dw_ref):
        t_idx = lax.broadcasted_iota(jnp.int32, (s, cb), 0)
        cv_c, cv_v = c_ref[...].astype(F32), v_ref[...].astype(F32)
        cv = cv_c * cv_v
        shifted = [_shift_down(cv, CONV_A_K - 1 - k, t_idx) for k in range(CONV_A_K)]
        cc = sum(w_ref[k:k + 1, :] * shifted[k] for k in range(CONV_A_K))
        dyv = dy_ref[...].astype(F32)
        db_ref[...] = (dyv * cc).astype(BF16)
        dcc = dyv * b_ref[...].astype(F32)
        dcv = sum(w_ref[k:k + 1, :] * _shift_up(dcc, CONV_A_K - 1 - k, t_idx, s) for k in range(CONV_A_K))
        dc_ref[...] = (dcv * cv_v).astype(BF16)
        dv_ref[...] = (dcv * cv_c).astype(BF16)
        rows = [jnp.sum(dcc * shifted[k], axis=0, keepdims=True) for k in range(CONV_A_K)]
        part = _stack_rows(rows, cb)

        @pl.when(pl.program_id(1) == 0)
        def _():
            dw_ref[...] = part

        @pl.when(pl.program_id(1) > 0)
        def _():
            dw_ref[...] += part

    def col(off):
        return pl.BlockSpec((s, cb), lambda j, b: (b, j + off * nd))

    own = pl.BlockSpec((s, cb), lambda j, b: (b, j))
    wspec = pl.BlockSpec((8, cb), lambda j, b: (0, j))
    out = jax.ShapeDtypeStruct((nb * s, d), BF16)
    return pl.pallas_call(
        body, name=name, grid=(nd, nb), in_specs=[own, col(0), col(1), col(2), wspec],
        out_specs=[own, own, own, wspec], out_shape=[out, out, out, jax.ShapeDtypeStruct((8, d), F32)],
        compiler_params=_params("parallel", "arbitrary"),
    )(dy, proj, proj, proj, w)


def _conv_s_fwd(proj, col0, w, bias, nb, s, cc_width, name, cb=256):
    cb = _tile(math.gcd(cc_width, col0) if col0 else cc_width, cb, LANES)
    nd, off = cc_width // cb, col0 // cb

    def body(x_ref, w_ref, b_ref, o_ref):
        t_idx = lax.broadcasted_iota(jnp.int32, (s, cb), 0)
        xv = x_ref[...].astype(F32)
        pre = b_ref[...] + sum(w_ref[k:k + 1, :] * _shift_down(xv, SSM_CONV_K - 1 - k, t_idx) for k in range(SSM_CONV_K))
        o_ref[...] = _silu(pre).astype(BF16)

    vec = pl.BlockSpec((8, cb), lambda b, j: (0, j))
    return pl.pallas_call(
        body, name=name, grid=(nb, nd),
        in_specs=[pl.BlockSpec((s, cb), lambda b, j: (b, j + off)), vec, pl.BlockSpec((1, cb), lambda b, j: (0, j))],
        out_specs=pl.BlockSpec((s, cb), lambda b, j: (b, j)),
        out_shape=jax.ShapeDtypeStruct((nb * s, cc_width), BF16), compiler_params=_params("parallel", "parallel"),
    )(proj, w, bias)


def _conv_s_bwd(dxc, proj, col0, w, bias, nb, s, cc_width, name, cb=256):
    cb = _tile(math.gcd(cc_width, col0) if col0 else cc_width, cb, LANES)
    nd, off = cc_width // cb, col0 // cb

    def body(d_ref, x_ref, w_ref, b_ref, dx_ref, dw_ref, db_ref):
        t_idx = lax.broadcasted_iota(jnp.int32, (s, cb), 0)
        xv = x_ref[...].astype(F32)
        shifted = [_shift_down(xv, SSM_CONV_K - 1 - k, t_idx) for k in range(SSM_CONV_K)]
        pre = b_ref[...] + sum(w_ref[k:k + 1, :] * shifted[k] for k in range(SSM_CONV_K))
        dpre = d_ref[...].astype(F32) * _dsilu(pre)
        dx = sum(w_ref[k:k + 1, :] * _shift_up(dpre, SSM_CONV_K - 1 - k, t_idx, s) for k in range(SSM_CONV_K))
        dx_ref[...] = dx.astype(BF16)
        rows = [jnp.sum(dpre * shifted[k], axis=0, keepdims=True) for k in range(SSM_CONV_K)]
        dw_part = _stack_rows(rows, cb)
        db_part = jnp.sum(dpre, axis=0, keepdims=True)

        @pl.when(pl.program_id(1) == 0)
        def _():
            dw_ref[...] = dw_part
            db_ref[...] = db_part

        @pl.when(pl.program_id(1) > 0)
        def _():
            dw_ref[...] += dw_part
            db_ref[...] += db_part

    own = pl.BlockSpec((s, cb), lambda j, b: (b, j))
    wspec = pl.BlockSpec((8, cb), lambda j, b: (0, j))
    bspec = pl.BlockSpec((1, cb), lambda j, b: (0, j))
    return pl.pallas_call(
        body, name=name, grid=(nd, nb),
        in_specs=[own, pl.BlockSpec((s, cb), lambda j, b: (b, j + off)), wspec, bspec],
        out_specs=[own, wspec, bspec],
        out_shape=[jax.ShapeDtypeStruct((nb * s, cc_width), BF16), jax.ShapeDtypeStruct((8, cc_width), F32),
                   jax.ShapeDtypeStruct((1, cc_width), F32)],
        compiler_params=_params("parallel", "arbitrary"),
    )(dxc, proj, w, bias)


def _ssd_common(xc_ref, dtr_ref, dtrt_ref, prow_ref, pcol_ref, e_ref, di):
    l = SSM_CHUNK
    bias_r, a_r = prow_ref[0:1, :], -jnp.exp(prow_ref[1:2, :])
    sp_in = dtr_ref[...] + bias_r
    dt = _softplus(sp_in)
    li = lax.broadcasted_iota(jnp.int32, (l, l), 0)
    si = lax.broadcasted_iota(jnp.int32, (l, l), 1)
    lower = (li >= si).astype(F32)
    upper = (li <= si).astype(F32)
    acs = _dot(lower, dt * a_r, precision=HI)
    bias_c, a_c = pcol_ref[:, 0:1], -jnp.exp(pcol_ref[:, 1:2])
    dt_t = _softplus(dtrt_ref[...] + bias_c)
    acs_t = _dot(dt_t * a_c, upper, precision=HI)
    e_mat = e_ref[...]
    dt_exp = _dot(dt, e_mat, precision=HI)
    acs_exp = _dot(acs, e_mat, precision=HI)
    acs_last = acs_exp[l - 1:l, :]
    x = xc_ref[:, 0:di].astype(F32)
    return dict(dt=dt, a_r=a_r, sp_in=sp_in, acs=acs, acs_t=acs_t, dt_exp=dt_exp, e_exp=jnp.exp(acs_exp),
                el_exp=jnp.exp(acs_last), f_exp=jnp.exp(acs_last - acs_exp), x=x, mask=li >= si, upper=upper,
                d_exp=_dot(prow_ref[0:8, :], e_mat, precision=HI)[2:3, :])


def _decay(q, h):
    seg = q["acs"][:, h:h + 1] - q["acs_t"][h:h + 1, :]
    return jnp.exp(jnp.where(q["mask"], seg, NEG_BIG))


def _ssd_fwd(xc, dtr, dtrt, prow, pcol, e_mat, nb, nc, di, name):
    l, n, g_n, p = SSM_CHUNK, SSM_STATE, SSM_GROUPS, SSM_HEAD_DIM
    cc = xc.shape[1]
    gw = di // g_n
    assert p * 2 == LANES and gw % LANES == 0

    def body(xc_ref, dtr_ref, dtrt_ref, prow_ref, pcol_ref, e_ref, y_ref, sprev_ref, st_ref):
        @pl.when(pl.program_id(1) == 0)
        def _():
            st_ref[...] = jnp.zeros_like(st_ref)

        q = _ssd_common(xc_ref, dtr_ref, dtrt_ref, prow_ref, pcol_ref, e_ref, di)
        x = q["x"]
        xd = x * q["dt_exp"]
        xdb = xd.astype(BF16)
        xdf = (xd * q["f_exp"]).astype(BF16)
        lane = lax.broadcasted_iota(jnp.int32, (l, LANES), 1)
        for g in range(g_n):
            lo = g * gw
            bg = xc_ref[:, di + g * n: di + (g + 1) * n]
            cg = xc_ref[:, di + g_n * n + g * n: di + g_n * n + (g + 1) * n]
            cb = _dot(cg, bg, NT)
            st_g = st_ref[:, lo:lo + gw]
            y_off = q["e_exp"][:, lo:lo + gw] * _dot(cg, st_g.astype(BF16))
            for pr in range(gw // LANES):
                c0 = lo + pr * LANES
                h0 = c0 // p
                xp = xdb[:, c0:c0 + LANES]
                m0 = (cb * _decay(q, h0)).astype(BF16)
                m1 = (cb * _decay(q, h0 + 1)).astype(BF16)
                yd = _dot(m0, jnp.where(lane < p, xp, 0)) + _dot(m1, jnp.where(lane >= p, xp, 0))
                y_ref[:, c0:c0 + LANES] = (yd + y_off[:, pr * LANES:(pr + 1) * LANES]
                                           + q["d_exp"][:, c0:c0 + LANES] * x[:, c0:c0 + LANES])
            sprev_ref[:, lo:lo + gw] = st_g
            st_ref[:, lo:lo + gw] = q["el_exp"][:, lo:lo + gw] * st_g + _dot(bg, xdf[:, lo:lo + gw], TN)

    tok = lambda w: pl.BlockSpec((l, w), lambda b, c: (b * nc + c, 0))
    const = lambda r, w: pl.BlockSpec((r, w), lambda b, c: (0, 0))
    return pl.pallas_call(
        body, name=name, grid=(nb, nc),
        in_specs=[tok(cc), tok(LANES), pl.BlockSpec((LANES, l), lambda b, c: (0, b * nc + c)),
                  const(8, LANES), const(LANES, 8), const(LANES, di)],
        out_specs=[tok(di), pl.BlockSpec((None, n, di), lambda b, c: (b * nc + c, 0, 0))],
        out_shape=[jax.ShapeDtypeStruct((nb * nc * l, di), F32), jax.ShapeDtypeStruct((nb * nc, n, di), F32)],
        scratch_shapes=[pltpu.VMEM((n, di), F32)],
        compiler_params=_params("parallel", "arbitrary"),
    )(xc, dtr, dtrt, prow, pcol, e_mat)


def _ssd_bwd(dy, xc, dtr, dtrt, prow, pcol, e_mat, sprev, nb, nc, di, name):
    l, n, g_n, p = SSM_CHUNK, SSM_STATE, SSM_GROUPS, SSM_HEAD_DIM
    cc = xc.shape[1]
    gw = di // g_n

    def body(dy_ref, xc_ref, dtr_ref, dtrt_ref, prow_ref, pcol_ref, e_ref, sprev_ref,
             dxc_ref, ddtr_ref, sums_ref, dst_ref, off_ref, dxd_ref, last_ref, dla_diag_ref, vst_ref):
        first = jnp.logical_and(pl.program_id(0) == 0, pl.program_id(1) == 0)

        @pl.when(pl.program_id(1) == 0)
        def _():
            dst_ref[...] = jnp.zeros_like(dst_ref)

        dla_diag_ref[...] = jnp.zeros_like(dla_diag_ref)
        strict_lower = lax.broadcasted_iota(jnp.int32, (l, l), 0) > lax.broadcasted_iota(jnp.int32, (l, l), 1)

        q = _ssd_common(xc_ref, dtr_ref, dtrt_ref, prow_ref, pcol_ref, e_ref, di)
        x = q["x"]
        xd = x * q["dt_exp"]
        xdb = xd.astype(BF16)
        xdf = (xd * q["f_exp"]).astype(BF16)
        dyv = dy_ref[...]
        dyb = dyv.astype(BF16)
        dye = (dyv * q["e_exp"]).astype(BF16)
        upper_b = q["upper"].astype(BF16)
        lane = lax.broadcasted_iota(jnp.int32, (l, LANES), 1)
        for g in range(g_n):
            lo = g * gw
            bg = xc_ref[:, di + g * n: di + (g + 1) * n]
            cg = xc_ref[:, di + g_n * n + g * n: di + g_n * n + (g + 1) * n]
            cb = _dot(cg, bg, NT)
            st_g = sprev_ref[:, lo:lo + gw]
            st_gb = st_g.astype(BF16)
            dst_g = dst_ref[:, lo:lo + gw]
            dst_gb = dst_g.astype(BF16)
            dye_g = dye[:, lo:lo + gw]
            xdf_g = xdf[:, lo:lo + gw]
            y_off = q["e_exp"][:, lo:lo + gw] * _dot(cg, st_gb)
            dc_g = _dot(dye_g, st_gb, NT)
            db_g = _dot(xdf_g, dst_gb, NT)
            dxd_state = _dot(bg, dst_gb) * q["f_exp"][:, lo:lo + gw]
            last_ref[:, lo:lo + gw] = jnp.sum(dst_g * st_g, axis=0, keepdims=True)
            dst_ref[:, lo:lo + gw] = q["el_exp"][:, lo:lo + gw] * dst_g + _dot(cg, dye_g, TN)
            off_ref[:, lo:lo + gw] = dyv[:, lo:lo + gw] * y_off
            vst_ref[:, lo:lo + gw] = xd[:, lo:lo + gw] * dxd_state
            dcb = jnp.zeros((l, l), F32)
            for pr in range(gw // LANES):
                c0 = lo + pr * LANES
                h0 = c0 // p
                xp = xdb[:, c0:c0 + LANES]
                dyp = dyb[:, c0:c0 + LANES]
                dxd_diag = jnp.zeros((l, LANES), F32)
                for k, keep in enumerate((lane < p, lane >= p)):
                    dec = _decay(q, h0 + k)
                    dy_h = jnp.where(keep, dyp, 0)
                    dm_dec = _dot(dy_h, xp, NT) * dec
                    dcb = dcb + dm_dec
                    dxd_diag = dxd_diag + _dot((cb * dec).astype(BF16), dy_h, TN)
                    above = _dot(upper_b, (dm_dec * cb).astype(BF16))
                    dla_col = jnp.sum(jnp.where(strict_lower, above, 0.0), axis=1, keepdims=True)
                    dla_diag_ref[...] = jnp.where(lane == h0 + k, dla_col, dla_diag_ref[...])
                dxd_ref[:, c0:c0 + LANES] = dxd_diag + dxd_state[:, pr * LANES:(pr + 1) * LANES]
            dcb_b = dcb.astype(BF16)
            dxc_ref[:, di + g * n: di + (g + 1) * n] = (db_g + _dot(dcb_b, cg, TN)).astype(BF16)
            dxc_ref[:, di + g_n * n + g * n: di + g_n * n + (g + 1) * n] = (dc_g + _dot(dcb_b, bg)).astype(BF16)
        dxd = dxd_ref[...]
        e_mat = e_ref[...]
        from_y = _dot(q["upper"], _dot(off_ref[...], e_mat, NT, precision=HI), precision=HI)
        from_s = _dot(strict_lower.astype(F32), _dot(vst_ref[...], e_mat, NT, precision=HI), precision=HI)
        carried = (_dot(jnp.broadcast_to(last_ref[...], (8, di)), e_mat, NT, precision=HI)[0:1, :]
                   * jnp.exp(q["acs"][l - 1:l, :]))
        dla = from_y + from_s + carried + dla_diag_ref[...]
        ddt = dla * q["a_r"] + _dot(dxd * x, e_mat, NT, precision=HI)
        ddtr = ddt * jax.nn.sigmoid(q["sp_in"])
        ddtr_ref[...] = ddtr
        dxc_ref[:, 0:di] = (dxd * q["dt_exp"] + q["d_exp"] * dyv).astype(BF16)
        dd_exp = jnp.sum(dyv * x, axis=0, keepdims=True)
        dd = _dot(jnp.broadcast_to(dd_exp, (8, di)), e_mat, NT, precision=HI)[0:1, :]
        part = _stack_rows([jnp.sum(ddtr, axis=0, keepdims=True),
                            jnp.sum(dla * q["dt"], axis=0, keepdims=True) * q["a_r"], dd], LANES)

        @pl.when(first)
        def _():
            sums_ref[...] = part

        @pl.when(jnp.logical_not(first))
        def _():
            sums_ref[...] += part

    rev = lambda b, c: b * nc + (nc - 1 - c)
    tok = lambda w: pl.BlockSpec((l, w), lambda b, c: (rev(b, c), 0))
    const = lambda r, w: pl.BlockSpec((r, w), lambda b, c: (0, 0))
    return pl.pallas_call(
        body, name=name, grid=(nb, nc),
        in_specs=[tok(di), tok(cc), tok(LANES), pl.BlockSpec((LANES, l), lambda b, c: (0, rev(b, c))),
                  const(8, LANES), const(LANES, 8), const(LANES, di),
                  pl.BlockSpec((None, n, di), lambda b, c: (rev(b, c), 0, 0))],
        out_specs=[tok(cc), tok(LANES), const(8, LANES)],
        out_shape=[jax.ShapeDtypeStruct((nb * nc * l, cc), BF16), jax.ShapeDtypeStruct((nb * nc * l, LANES), F32),
                   jax.ShapeDtypeStruct((8, LANES), F32)],
        scratch_shapes=[pltpu.VMEM((n, di), F32), pltpu.VMEM((l, di), F32), pltpu.VMEM((l, di), F32),
                        pltpu.VMEM((1, di), F32), pltpu.VMEM((l, LANES), F32), pltpu.VMEM((l, di), F32)],
        compiler_params=_params("arbitrary", "arbitrary"),
    )(dy, xc, dtr, dtrt, prow, pcol, e_mat, sprev)


def _gate_norm_fwd(y, proj, z_col0, norm_g, di, name, tt=256):
    t = y.shape[0]
    tt = _tile(t, tt, 8)
    gw = di // SSM_GROUPS
    zw = _tile(math.gcd(di, z_col0), di, LANES)
    nz, zoff = di // zw, z_col0 // zw

    def body(*refs):
        y_ref, z_refs, g_ref, o_ref = refs[0], refs[1:1 + nz], refs[1 + nz], refs[2 + nz]
        for g in range(SSM_GROUPS):
            lo = g * gw
            zv = z_refs[lo // zw][:, lo % zw:lo % zw + gw].astype(F32)
            yg = y_ref[:, lo:lo + gw] * _silu(zv)
            r = lax.rsqrt(jnp.mean(yg * yg, axis=-1, keepdims=True) + EPS)
            o_ref[:, lo:lo + gw] = (yg * r * g_ref[:, lo:lo + gw]).astype(BF16)

    row = pl.BlockSpec((tt, di), lambda i: (i, 0))
    zspecs = [pl.BlockSpec((tt, zw), functools.partial(lambda i, k: (i, zoff + k), k=k)) for k in range(nz)]
    return pl.pallas_call(
        body, name=name, grid=(t // tt,), in_specs=[row] + zspecs + [pl.BlockSpec((1, di), lambda i: (0, 0))],
        out_specs=row, out_shape=jax.ShapeDtypeStruct((t, di), BF16), compiler_params=_params("parallel"),
    )(y, *([proj] * nz), norm_g)


def _gate_norm_bwd(dn, y, proj, z_col0, norm_g, di, name, tt=256):
    t = y.shape[0]
    tt = _tile(t, tt, 8)
    gw = di // SSM_GROUPS
    zw = _tile(math.gcd(di, z_col0), di, LANES)
    nz, zoff = di // zw, z_col0 // zw

    def body(*refs):
        dn_ref, y_ref, z_refs, g_ref = refs[0], refs[1], refs[2:2 + nz], refs[2 + nz]
        dy_ref, dz_ref, dg_ref = refs[3 + nz:]
        first = pl.program_id(0) == 0
        for g in range(SSM_GROUPS):
            lo = g * gw
            zv = z_refs[lo // zw][:, lo % zw:lo % zw + gw].astype(F32)
            yv = y_ref[:, lo:lo + gw]
            sz = _silu(zv)
            yg = yv * sz
            r = lax.rsqrt(jnp.mean(yg * yg, axis=-1, keepdims=True) + EPS)
            yh = yg * r
            dnv = dn_ref[:, lo:lo + gw].astype(F32)
            gy = dnv * g_ref[:, lo:lo + gw]
            dyg = r * (gy - yh * jnp.mean(gy * yh, axis=-1, keepdims=True))
            dy_ref[:, lo:lo + gw] = dyg * sz
            dz_ref[:, lo:lo + gw] = (dyg * yv * _dsilu(zv)).astype(BF16)
            part = jnp.sum(dnv * yh, axis=0, keepdims=True)

            @pl.when(first)
            def _():
                dg_ref[:, lo:lo + gw] = part

            @pl.when(jnp.logical_not(first))
            def _():
                dg_ref[:, lo:lo + gw] += part

    row = pl.BlockSpec((tt, di), lambda i: (i, 0))
    vec = pl.BlockSpec((1, di), lambda i: (0, 0))
    zspecs = [pl.BlockSpec((tt, zw), functools.partial(lambda i, k: (i, zoff + k), k=k)) for k in range(nz)]
    return pl.pallas_call(
        body, name=name, grid=(t // tt,), in_specs=[row, row] + zspecs + [vec], out_specs=[row, row, vec],
        out_shape=[jax.ShapeDtypeStruct((t, di), F32), jax.ShapeDtypeStruct((t, di), BF16), jax.ShapeDtypeStruct((1, di), F32)],
        compiler_params=_params("arbitrary"),
    )(dn, y, *([proj] * nz), norm_g)


def _softmax_rows(s):
    s = s - jnp.max(s, axis=-1, keepdims=True)
    e = jnp.exp(s)
    return e / jnp.sum(e, axis=-1, keepdims=True)


def _xattn_fwd(q, kv, nb, s, m, d, name, tq=512):
    tq = _tile(s, tq, 8)
    nq = s // tq
    hd = d // XATTN_HEADS
    scale = 1.0 / math.sqrt(hd)

    def body(q_ref, k_ref, v_ref, o_ref):
        for h in range(XATTN_HEADS):
            sl = slice(h * hd, (h + 1) * hd)
            prob = _softmax_rows(_dot(q_ref[:, sl], k_ref[:, sl], NT) * scale)
            o_ref[:, sl] = _dot(prob.astype(BF16), v_ref[:, sl]).astype(BF16)

    return pl.pallas_call(
        body, name=name, grid=(nb, nq),
        in_specs=[pl.BlockSpec((tq, d), lambda b, i: (b * nq + i, 0)), pl.BlockSpec((m, d), lambda b, i: (b, 0)),
                  pl.BlockSpec((m, d), lambda b, i: (b, 1))],
        out_specs=pl.BlockSpec((tq, d), lambda b, i: (b * nq + i, 0)),
        out_shape=jax.ShapeDtypeStruct((nb * s, d), BF16), compiler_params=_params("parallel", "parallel"),
    )(q, kv, kv)


def _xattn_bwd(do, q, kv, nb, s, m, d, name, tq=512):
    tq = _tile(s, tq, 8)
    nq = s // tq
    hd = d // XATTN_HEADS
    scale = 1.0 / math.sqrt(hd)

    def body(do_ref, q_ref, k_ref, v_ref, dq_ref, dk_ref, dv_ref):
        first = pl.program_id(1) == 0
        for h in range(XATTN_HEADS):
            sl = slice(h * hd, (h + 1) * hd)
            qh, kh, vh, doh = q_ref[:, sl], k_ref[:, sl], v_ref[:, sl], do_ref[:, sl]
            prob = _softmax_rows(_dot(qh, kh, NT) * scale)
            dv_h = _dot(prob.astype(BF16), doh, TN)
            dp = _dot(doh, vh, NT)
            ds = (prob * (dp - jnp.sum(dp * prob, axis=-1, keepdims=True)) * scale).astype(BF16)
            dq_ref[:, sl] = _dot(ds, kh).astype(BF16)
            dk_h = _dot(ds, qh, TN)

            @pl.when(first)
            def _():
                dk_ref[:, sl] = dk_h
                dv_ref[:, sl] = dv_h

            @pl.when(jnp.logical_not(first))
            def _():
                dk_ref[:, sl] += dk_h
                dv_ref[:, sl] += dv_h

    qspec = pl.BlockSpec((tq, d), lambda b, i: (b * nq + i, 0))
    dq, dk, dv = pl.pallas_call(
        body, name=name, grid=(nb, nq),
        in_specs=[qspec, qspec, pl.BlockSpec((m, d), lambda b, i: (b, 0)), pl.BlockSpec((m, d), lambda b, i: (b, 1))],
        out_specs=[qspec, pl.BlockSpec((m, d), lambda b, i: (b, 0)), pl.BlockSpec((m, d), lambda b, i: (b, 0))],
        out_shape=[jax.ShapeDtypeStruct((nb * s, d), BF16), jax.ShapeDtypeStruct((nb * m, d), F32),
                   jax.ShapeDtypeStruct((nb * m, d), F32)],
        compiler_params=_params("parallel", "arbitrary"),
    )(do, q, kv, kv)
    return dq, dk, dv


def _all_gather(shards, name):
    n_arr = len(shards)

    def body(*refs):
        x_refs, out_refs = refs[:n_arr], refs[n_arr:2 * n_arr]
        send_sems, recv_sems, local_sems = refs[2 * n_arr:]
        x, y, c = lax.axis_index("x"), lax.axis_index("y"), lax.axis_index("c")
        me, sibling = (x, y, c), (x, y, 1 - c)
        chips = [(1 - x, y), (x, 1 - y), (1 - x, 1 - y)]

        def copy(w, k, block, to, from_input=False):
            px, py, pc = block
            rows = out_refs[w].at[4 * px + 2 * py + pc]
            return pltpu.make_async_remote_copy(
                src_ref=x_refs[w] if from_input else rows, dst_ref=rows,
                send_sem=send_sems.at[7 * w + k], recv_sem=recv_sems.at[7 * w + k], device_id=to, device_id_type=MESH)

        started = []
        for w in range(n_arr):
            mine = pltpu.make_async_copy(x_refs[w], out_refs[w].at[4 * x + 2 * y + c], local_sems.at[w])
            mine.start()
            started.append(mine)
        sends = []
        for w in range(n_arr):
            sends.append(copy(w, 0, me, sibling, from_input=True))
            sends += [copy(w, 1 + j, me, (*chip, c), from_input=True) for j, chip in enumerate(chips)]
        for cp in sends:
            cp.start()
        for j, chip in enumerate(chips):
            for w in range(n_arr):
                copy(w, 1 + j, (*chip, c), me).wait_recv()
                passed = copy(w, 4 + j, (*chip, c), sibling)
                passed.start()
                sends.append(passed)
        for w in range(n_arr):
            copy(w, 0, sibling, me).wait_recv()
            for j, chip in enumerate(chips):
                copy(w, 4 + j, (*chip, 1 - c), me).wait_recv()
        for cp in sends:
            cp.wait_send()
        for mine in started:
            mine.wait()

    hbm = pl.BlockSpec(memory_space=pl.ANY)
    return pl.pallas_call(
        body, name=name, out_shape=[jax.ShapeDtypeStruct((N_DEV,) + s.shape, s.dtype) for s in shards],
        in_specs=[hbm] * n_arr, out_specs=[hbm] * n_arr,
        scratch_shapes=[pltpu.SemaphoreType.DMA((7 * n_arr,)), pltpu.SemaphoreType.DMA((7 * n_arr,)),
                        pltpu.SemaphoreType.DMA((n_arr,))],
    )(*shards)


def _exchange(slot_arrays, name):
    n_arr = len(slot_arrays)

    def body(*refs):
        in_refs, out_refs = refs[:n_arr], refs[n_arr:2 * n_arr]
        send_sems, recv_sems, local_sems = refs[2 * n_arr:]
        x, y, c = lax.axis_index("x"), lax.axis_index("y"), lax.axis_index("c")
        me = 4 * x + 2 * y + c
        started, sends, peers = [], [], []
        for k in range(1, N_DEV):
            peers.append((1 - x if k & 4 else x, 1 - y if k & 2 else y, 1 - c if k & 1 else c))
        for w in range(n_arr):
            mine = pltpu.make_async_copy(in_refs[w].at[me], out_refs[w].at[me], local_sems.at[w])
            mine.start()
            started.append(mine)
            for k, (px, py, pc) in enumerate(peers):
                cp = pltpu.make_async_remote_copy(
                    src_ref=in_refs[w].at[4 * px + 2 * py + pc], dst_ref=out_refs[w].at[me],
                    send_sem=send_sems.at[7 * w + k], recv_sem=recv_sems.at[7 * w + k],
                    device_id=(px, py, pc), device_id_type=MESH)
                cp.start()
                sends.append(cp)
        for w in range(n_arr):
            for k, (px, py, pc) in enumerate(peers):
                landed = out_refs[w].at[4 * px + 2 * py + pc]
                pltpu.make_async_remote_copy(
                    src_ref=landed, dst_ref=landed, send_sem=send_sems.at[7 * w + k], recv_sem=recv_sems.at[7 * w + k],
                    device_id=(x, y, c), device_id_type=MESH).wait_recv()
        for cp in sends:
            cp.wait_send()
        for mine in started:
            mine.wait()

    hbm = pl.BlockSpec(memory_space=pl.ANY)
    return pl.pallas_call(
        body, name=name, out_shape=[jax.ShapeDtypeStruct(s.shape, s.dtype) for s in slot_arrays],
        in_specs=[hbm] * n_arr, out_specs=[hbm] * n_arr,
        scratch_shapes=[pltpu.SemaphoreType.DMA((7 * n_arr,)), pltpu.SemaphoreType.DMA((7 * n_arr,)),
                        pltpu.SemaphoreType.DMA((n_arr,))],
    )(*slot_arrays)


def _adamw_math(w, g, m, v):
    m = ADAM_B1 * m + (1.0 - ADAM_B1) * g
    v = ADAM_B2 * v + (1.0 - ADAM_B2) * (g * g)
    m_hat = m / (1.0 - ADAM_B1 ** ADAM_STEP)
    v_hat = v / (1.0 - ADAM_B2 ** ADAM_STEP)
    delta = -ADAM_LR * (m_hat / (jnp.sqrt(v_hat) + ADAM_EPS) + ADAM_WD * w)
    return delta, m, v


def _sum8(parts, name, tr=512):
    _, r, c_dim = parts.shape
    tr = _tile(r, tr, BF16_SUBLANES)

    def body(p_ref, o_ref):
        acc = p_ref[0].astype(F32)
        for k in range(1, N_DEV):
            acc = acc + p_ref[k].astype(F32)
        o_ref[...] = acc

    return pl.pallas_call(
        body, name=name, grid=(r // tr,), in_specs=[pl.BlockSpec((N_DEV, tr, c_dim), lambda i: (0, i, 0))],
        out_specs=pl.BlockSpec((tr, c_dim), lambda i: (i, 0)),
        out_shape=jax.ShapeDtypeStruct((r, c_dim), F32), compiler_params=_params("parallel"),
    )(parts)


def _sum8_adamw(parts, w, m, v, name, tr=128):
    _, r, c_dim = parts.shape
    tr = _tile(r, tr, BF16_SUBLANES)

    def body(p_ref, w_ref, m_ref, v_ref, g_ref, d_ref, nm_ref, nv_ref):
        g = p_ref[0].astype(F32)
        for k in range(1, N_DEV):
            g = g + p_ref[k].astype(F32)
        g_ref[...] = g
        d_ref[...], nm_ref[...], nv_ref[...] = _adamw_math(w_ref[...], g, m_ref[...], v_ref[...])

    row = pl.BlockSpec((tr, c_dim), lambda i: (i, 0))
    out = jax.ShapeDtypeStruct((r, c_dim), F32)
    return pl.pallas_call(
        body, name=name, grid=(r // tr,), in_specs=[pl.BlockSpec((N_DEV, tr, c_dim), lambda i: (0, i, 0)), row, row, row],
        out_specs=[row] * 4, out_shape=[out] * 4, compiler_params=_params("parallel"),
    )(parts, w, m, v)


def _adamw(g, w, m, v, name):
    r, c_dim = g.shape

    def body(g_ref, w_ref, m_ref, v_ref, d_ref, nm_ref, nv_ref):
        d_ref[...], nm_ref[...], nv_ref[...] = _adamw_math(w_ref[...], g_ref[...], m_ref[...], v_ref[...])

    out = jax.ShapeDtypeStruct((r, c_dim), F32)
    return pl.pallas_call(body, name=name, out_shape=[out] * 3)(g, w, m, v)


def _pack_rows(arrays, dtype, row_unit):
    chunks, offs, r0 = [], [], 0
    for a in arrays:
        flat = a.reshape(-1).astype(dtype)
        rows = -(-flat.shape[0] // (LANES * row_unit)) * row_unit
        flat = jnp.pad(flat, (0, rows * LANES - flat.shape[0]))
        chunks.append(flat.reshape(rows, LANES))
        offs.append((r0, rows))
        r0 += rows
    return jnp.concatenate(chunks, axis=0), offs


def _unpack_rows(packed, offs, shapes):
    out = []
    for (r0, rows), shape in zip(offs, shapes):
        n = math.prod(shape)
        blk = packed[..., r0:r0 + rows, :]
        blk = blk.reshape(packed.shape[:-2] + (rows * LANES,))[..., :n]
        out.append(blk.reshape(packed.shape[:-2] + tuple(shape)))
    return out


def _full_from_slots(blk, col_sharded):
    _, r, c = blk.shape
    if col_sharded:
        return blk.transpose(1, 0, 2).reshape(r, N_DEV * c)
    return blk.reshape(N_DEV * r, c)


def _slots_from_full(full, col_sharded):
    r, c = full.shape
    if col_sharded:
        return full.reshape(r, N_DEV, c // N_DEV).transpose(1, 0, 2)
    return full.reshape(N_DEV, r // N_DEV, c)


def _ffn_fwd(h, g, w_gu_t, w_d, tag):
    n = _rms_fwd(h, g, f"{tag}_norm")
    gu = _mm(n, w_gu_t, BF16, f"{tag}_up", nt=True)
    a = _swiglu_fwd(gu, f"{tag}_swiglu")
    h_out = _mm(a, w_d, F32, f"{tag}_down", res=h, alpha=FFN_RES_WEIGHT)
    return h_out, (h, n, gu, a)


def _ffn_bwd(dh_out, saved, g, w_gu_t, w_d, tag):
    h, n, gu, a = saved
    dw_d = _mm_tn(a, dh_out, f"{tag}_dw_down", alpha=FFN_RES_WEIGHT)
    da = _mm(dh_out, w_d, BF16, f"{tag}_da", nt=True, alpha=FFN_RES_WEIGHT)
    dgu = _swiglu_bwd(da, gu, f"{tag}_dswiglu")
    dw_gu_t = _mm_tn(dgu, n, f"{tag}_dw_up")
    dn = _mm(dgu, w_gu_t, F32, f"{tag}_dn")
    dh, dg = _rms_bwd(dn, h, g, f"{tag}_dnorm", dres=dh_out)
    return dh, dg, dw_gu_t, dw_d


def _local_step(x3, mem3, target3, wts, small):
    nb, s, d = x3.shape
    m_len = mem3.shape[1]
    t = nb * s
    nc = s // SSM_CHUNK
    di = small["ssm_norm"].shape[1]
    hs = di // SSM_HEAD_DIM
    cc = di + 2 * SSM_GROUPS * SSM_STATE
    x, mem, target = x3.reshape(t, d), mem3.reshape(nb * m_len, d), target3.reshape(t, d)

    sizes = (d, d, d, di, cc, hs, d, d)
    offs = [0]
    for sz in sizes:
        offs.append(offs[-1] + sz)
    w_in_t = wts["w_in"]
    w_main_t = jnp.concatenate([w_in_t[offs[i]:offs[i + 1]] for i in (0, 1, 2, 3, 4, 6, 7)], axis=0)
    w_dt_t = jnp.pad(w_in_t[offs[5]:offs[6]], ((0, LANES - hs), (0, 0)))
    z_col0, xbc_col0 = 3 * d, 3 * d + di
    ga_blk, gb_blk = (3 * d + di + cc) // d, (4 * d + di + cc) // d

    pad_vec = lambda v: jnp.pad(v.reshape(1, -1), ((0, 0), (0, LANES - hs)))
    prow = jnp.concatenate([pad_vec(small["ssm_dt_bias"]), pad_vec(small["ssm_a_log"]), pad_vec(small["ssm_d"]),
                            jnp.zeros((5, LANES), F32)], axis=0)
    pcol = prow.T
    e_mat = (lax.broadcasted_iota(jnp.int32, (LANES, di), 0)
             == lax.broadcasted_iota(jnp.int32, (LANES, di), 1) // SSM_HEAD_DIM).astype(F32)
    conv_a_w8 = jnp.pad(small["conv_a_w"][0], ((0, 8 - CONV_A_K), (0, 0)))
    ssm_conv_w8 = jnp.pad(small["ssm_conv_w"][0], ((0, 8 - SSM_CONV_K), (0, 0)))

    h1, ffn1_saved = _ffn_fwd(x, small["ffn1_norm"], wts["ffn1_w_gate_up"], wts["ffn1_w_down"], "ffn1")
    u = _rms_fwd(h1, small["mix_norm"], "mix_norm")
    proj = _mm(u, w_main_t, BF16, "in_proj", nt=True)
    dtr = _mm(u, w_dt_t, F32, "in_proj_dt", nt=True)
    yap = _conv_a_fwd(proj, conv_a_w8, nb, s, d, "conv_a")
    y_a = _mm(yap, wts["w_out_a"], BF16, "out_a")
    xc = _conv_s_fwd(proj, xbc_col0, ssm_conv_w8, small["ssm_conv_b"], nb, s, cc, "conv_s")
    dtrt = dtr.T
    y_ssd, sprev = _ssd_fwd(xc, dtr, dtrt, prow, pcol, e_mat, nb, nc, di, "ssd")
    ygn = _gate_norm_fwd(y_ssd, proj, z_col0, small["ssm_norm"], di, "gate_norm")
    y_b = _mm(ygn, wts["w_out_ssm"], BF16, "out_ssm")
    merged = _merge_fwd(y_a, y_b, proj, ga_blk, gb_blk, d, "merge")
    h2 = _mm(merged, wts["w_mix_out"], F32, "mix_out", res=h1)
    un = _rms_fwd(h2, small["xattn_norm"], "xattn_norm")
    mn = _rms_fwd(mem, small["mem_norm"], "mem_norm")
    q = _mm(un, wts["w_q"], BF16, "q_proj")
    kv = _mm(mn, wts["w_kv"], BF16, "kv_proj", nt=True)
    o = _xattn_fwd(q, kv, nb, s, m_len, d, "xattn")
    h3 = _mm(o, wts["w_o_x"], F32, "o_proj", res=h2)
    h4, ffn2_saved = _ffn_fwd(h3, small["ffn2_norm"], wts["ffn2_w_gate_up"], wts["ffn2_w_down"], "ffn2")
    loss_vec, dh4, dg_final = _loss_head(h4, small["final_norm"].reshape(1, d), target, "loss_head")

    grads = {"final_norm": dg_final.reshape(d)}
    dh3, grads["ffn2_norm"], grads["ffn2_w_gate_up"], grads["ffn2_w_down"] = _ffn_bwd(
        dh4, ffn2_saved, small["ffn2_norm"], wts["ffn2_w_gate_up"], wts["ffn2_w_down"], "ffn2")
    grads["w_o_x"] = _mm_tn(o, dh3, "dw_o")
    do = _mm(dh3, wts["w_o_x"], BF16, "d_o", nt=True)
    dq, dk, dv = _xattn_bwd(do, q, kv, nb, s, m_len, d, "xattn_bwd")
    grads["w_q"] = _mm_tn(un, dq, "dw_q")
    grads["w_kv"] = jnp.concatenate([_mm_tn(dk, mn, "dw_k"), _mm_tn(dv, mn, "dw_v")], axis=0)
    dun = _mm(dq, wts["w_q"], F32, "d_un", nt=True)
    dmn = _mm([dk, dv], wts["w_kv"], F32, "d_mn")
    _, grads["mem_norm"] = _rms_bwd(dmn, mem, small["mem_norm"], "mem_dnorm")
    dh2, grads["xattn_norm"] = _rms_bwd(dun, h2, small["xattn_norm"], "xattn_dnorm", dres=dh3)
    grads["w_mix_out"] = _mm_tn(merged, dh2, "dw_mix")
    dmerged = _mm(dh2, wts["w_mix_out"], BF16, "d_merged", nt=True)
    dya, dyb, dga, dgb = _merge_bwd(dmerged, y_a, y_b, proj, ga_blk, gb_blk, d, "merge_bwd")
    grads["w_out_a"] = _mm_tn(yap, dya, "dw_out_a")
    grads["w_out_ssm"] = _mm_tn(ygn, dyb, "dw_out_ssm")
    dyap = _mm(dya, wts["w_out_a"], BF16, "d_yap", nt=True)
    dygn = _mm(dyb, wts["w_out_ssm"], BF16, "d_ygn", nt=True)
    dab, dac, dav, dconv_a = _conv_a_bwd(dyap, proj, conv_a_w8, nb, s, d, "conv_a_bwd")
    dy_ssd, dz, grads["ssm_norm"] = _gate_norm_bwd(dygn, y_ssd, proj, z_col0, small["ssm_norm"], di, "gate_norm_bwd")
    dxc, ddtr, ssd_sums = _ssd_bwd(dy_ssd, xc, dtr, dtrt, prow, pcol, e_mat, sprev, nb, nc, di, "ssd_bwd")
    dxbc, dconv_s, grads["ssm_conv_b"] = _conv_s_bwd(dxc, proj, xbc_col0, ssm_conv_w8, small["ssm_conv_b"], nb, s, cc, "conv_s_bwd")
    dpieces = [("ab", dab), ("ac", dac), ("av", dav), ("z", dz), ("xbc", dxbc), ("ga", dga), ("gb", dgb)]
    dw = {tag: _mm_tn(piece, u, f"dw_in_{tag}") for tag, piece in dpieces}
    dw_dt = _mm_tn(ddtr, u, "dw_in_dt")[:hs]
    du_main = _mm([piece for _, piece in dpieces], w_main_t, F32, "d_u", tk=1024)
    du = _mm(ddtr, w_dt_t, F32, "d_u_dt", res=du_main)
    dh1, grads["mix_norm"] = _rms_bwd(du, h1, small["mix_norm"], "mix_dnorm", dres=dh2)
    dx, grads["ffn1_norm"], grads["ffn1_w_gate_up"], grads["ffn1_w_down"] = _ffn_bwd(
        dh1, ffn1_saved, small["ffn1_norm"], wts["ffn1_w_gate_up"], wts["ffn1_w_down"], "ffn1")

    grads["w_in"] = jnp.concatenate([dw["ab"], dw["ac"], dw["av"], dw["z"], dw["xbc"], dw_dt, dw["ga"], dw["gb"]], axis=0)
    grads["conv_a_w"] = dconv_a[:CONV_A_K]
    grads["ssm_conv_w"] = dconv_s[:SSM_CONV_K]
    grads["ssm_dt_bias"] = ssd_sums[0:1, :hs]
    grads["ssm_a_log"] = ssd_sums[1:2, :hs]
    grads["ssm_d"] = ssd_sums[2:3, :hs]
    return loss_vec[0, 0], dx.reshape(nb, s, d), grads


def _step(inputs):
    w = {k: inputs[k] for k in WEIGHT_ORDER}
    mom = {k: inputs["m_" + k] for k in WEIGHT_ORDER}
    vel = {k: inputs["v_" + k] for k in WEIGHT_ORDER}
    me = 4 * lax.axis_index("x") + 2 * lax.axis_index("y") + lax.axis_index("c")

    send = [(w[k][0].T if k in COL_SHARDED else w[k][0]).astype(BF16) for k in BIG_WEIGHTS]
    gathered = _all_gather(send, "gather_weights")
    full = {k: g.reshape(N_DEV * g.shape[1], g.shape[2]) for k, g in zip(BIG_WEIGHTS, gathered)}

    small = {k: w[k] for k in SMALL_REPLICATED}
    conv_shapes = [w[k].shape[1:] for k in SMALL_SHARDED]
    packed_c, conv_offs = _pack_rows([w[k][0] for k in SMALL_SHARDED], F32, 8)
    conv_blocks = _unpack_rows(_all_gather([packed_c], "gather_conv_weights")[0], conv_offs, conv_shapes)
    for k, b in zip(SMALL_SHARDED, conv_blocks):
        small[k] = _full_from_slots(b, True)[None]

    loss_local, grad_x, grads = _local_step(inputs["x"], inputs["mem"], inputs["loss_target"], full, small)
    loss = lax.psum(loss_local, AXES)

    received = _exchange([grads[k].reshape((N_DEV,) + s.shape) for k, s in zip(BIG_WEIGHTS, send)], "exchange_grads")
    out = {}
    for k, parts in zip(BIG_WEIGHTS, received):
        if k in COL_SHARDED:
            parts = parts.transpose(0, 2, 1)
        out[k] = tuple(o[None] for o in _sum8_adamw(parts, w[k][0], mom[k][0], vel[k][0], f"sum_adamw_{k}"))

    small_names = SMALL_REPLICATED + SMALL_SHARDED
    packed_g, small_offs = _pack_rows([grads[k] for k in small_names], F32, 8)
    total = _sum8(_all_gather([packed_g], "gather_small_grads")[0], "sum_small_grads")
    full_grads = _unpack_rows(total, small_offs, [grads[k].shape for k in small_names])
    mine = {}
    for k, g in zip(small_names, full_grads):
        if k in SMALL_SHARDED:
            c_loc = w[k].shape[2]
            g = lax.dynamic_slice_in_dim(g, me * c_loc, c_loc, axis=1)
        mine[k] = g.reshape(w[k].shape)
    sg, s_offs = _pack_rows([mine[k] for k in small_names], F32, 8)
    sw, _ = _pack_rows([w[k] for k in small_names], F32, 8)
    sm, _ = _pack_rows([mom[k] for k in small_names], F32, 8)
    sv, _ = _pack_rows([vel[k] for k in small_names], F32, 8)
    s_shapes = [w[k].shape for k in small_names]
    small_out = [_unpack_rows(a, s_offs, s_shapes) for a in _adamw(sg, sw, sm, sv, "adamw_small")]
    for i, k in enumerate(small_names):
        out[k] = (mine[k],) + tuple(o[i] for o in small_out)

    res = [loss, grad_x]
    for j in range(4):
        res += [out[k][j] for k in WEIGHT_ORDER]
    return tuple(res)


def kernel(x, mem, ffn1_norm, ffn1_w_gate_up, ffn1_w_down, mix_norm, w_in, conv_a_w, w_out_a, ssm_conv_w, ssm_conv_b, ssm_dt_bias, ssm_a_log, ssm_d, ssm_norm, w_out_ssm, w_mix_out, xattn_norm, mem_norm, w_q, w_kv, w_o_x, ffn2_norm, ffn2_w_gate_up, ffn2_w_down, final_norm, loss_target, m_ffn1_norm, m_ffn1_w_gate_up, m_ffn1_w_down, m_mix_norm, m_w_in, m_conv_a_w, m_w_out_a, m_ssm_conv_w, m_ssm_conv_b, m_ssm_dt_bias, m_ssm_a_log, m_ssm_d, m_ssm_norm, m_w_out_ssm, m_w_mix_out, m_xattn_norm, m_mem_norm, m_w_q, m_w_kv, m_w_o_x, m_ffn2_norm, m_ffn2_w_gate_up, m_ffn2_w_down, m_final_norm, v_ffn1_norm, v_ffn1_w_gate_up, v_ffn1_w_down, v_mix_norm, v_w_in, v_conv_a_w, v_w_out_a, v_ssm_conv_w, v_ssm_conv_b, v_ssm_dt_bias, v_ssm_a_log, v_ssm_d, v_ssm_norm, v_w_out_ssm, v_w_mix_out, v_xattn_norm, v_mem_norm, v_w_q, v_w_kv, v_w_o_x, v_ffn2_norm, v_ffn2_w_gate_up, v_ffn2_w_down, v_final_norm):
    return _step(dict(locals()))
```

```python
import functools
import math
import types

import jax
import jax.numpy as jnp
from jax import lax
from jax.experimental import pallas as pl
from jax.experimental.pallas import tpu as pltpu

F32, BF16 = jnp.float32, jnp.bfloat16
HI = lax.Precision.HIGHEST
MESH = pl.DeviceIdType.MESH
AXES = ("x", "y", "c")
N_DEV = 8

EPS = 1e-6
FFN_RES_WEIGHT = 0.5
SSM_HEAD_DIM = 64
SSM_GROUPS = 4
SSM_STATE = 128
SSM_CHUNK = 128
CONV_A_K = 3
SSM_CONV_K = 4
XATTN_HEADS = 4
ADAM_LR, ADAM_B1, ADAM_B2, ADAM_EPS, ADAM_WD, ADAM_STEP = 1e-3, 0.9, 0.999, 1e-8, 0.01, 10

LANES = 128
BF16_SUBLANES = 16
VMEM_LIMIT_BYTES = 56 * 2 ** 20
NEG_BIG = -1e30

BIG_WEIGHTS = ("ffn1_w_gate_up", "ffn1_w_down", "w_in", "w_out_a", "w_out_ssm", "w_mix_out",
               "w_q", "w_kv", "w_o_x", "ffn2_w_gate_up", "ffn2_w_down")
COL_SHARDED = ("ffn1_w_gate_up", "w_in", "w_kv", "ffn2_w_gate_up")
SMALL_REPLICATED = ("ffn1_norm", "mix_norm", "ssm_conv_b", "ssm_dt_bias", "ssm_a_log", "ssm_d", "ssm_norm",
                    "xattn_norm", "mem_norm", "ffn2_norm", "final_norm")
SMALL_SHARDED = ("conv_a_w", "ssm_conv_w")
WEIGHT_ORDER = ("ffn1_norm", "ffn1_w_gate_up", "ffn1_w_down", "mix_norm", "w_in", "conv_a_w", "w_out_a",
                "ssm_conv_w", "ssm_conv_b", "ssm_dt_bias", "ssm_a_log", "ssm_d", "ssm_norm", "w_out_ssm",
                "w_mix_out", "xattn_norm", "mem_norm", "w_q", "w_kv", "w_o_x", "ffn2_norm", "ffn2_w_gate_up",
                "ffn2_w_down", "final_norm")


def _tile(dim, pref, unit):
    best = None
    t = unit
    while t <= min(dim, pref):
        if dim % t == 0:
            best = t
        t += unit
    return best if best is not None else dim


def _params(*sem):
    return pltpu.CompilerParams(dimension_semantics=sem, vmem_limit_bytes=VMEM_LIMIT_BYTES)


def _silu(x):
    return x * jax.nn.sigmoid(x)


def _dsilu(x):
    s = jax.nn.sigmoid(x)
    return s * (1.0 + x * (1.0 - s))


def _softplus(x):
    return jnp.maximum(x, 0.0) + jnp.log(1.0 + jnp.exp(-jnp.abs(x)))


def _dot(a, b, dims=(((1,), (0,)), ((), ())), precision=None):
    return lax.dot_general(a, b, dims, preferred_element_type=F32, precision=precision)


def _stack_rows(rows, width):
    r_idx = lax.broadcasted_iota(jnp.int32, (8, width), 0)
    acc = jnp.zeros((8, width), F32)
    for k, row in enumerate(rows):
        acc = jnp.where(r_idx == k, row, acc)
    return acc


NT = (((1,), (1,)), ((), ()))
TN = (((0,), (0,)), ((), ()))


def _mm(a, b, out_dtype, name, res=None, alpha=1.0, nt=False, dep=None, tm=1024, tn=512, tk=2816):
    pieces = list(a) if isinstance(a, (list, tuple)) else [a]
    m = pieces[0].shape[0]
    k = sum(p.shape[1] for p in pieces)
    n = b.shape[0] if nt else b.shape[1]
    assert (b.shape[1] if nt else b.shape[0]) == k
    tm, tn = _tile(m, tm, 8), _tile(n, tn, LANES)
    tk = _tile(math.gcd(*[p.shape[1] for p in pieces]), tk, LANES)
    nk = k // tk
    starts, s0 = [], 0
    for p in pieces:
        starts.append((s0, p.shape[1] // tk))
        s0 += p.shape[1] // tk
    n_p = len(pieces)

    def body(*refs):
        a_refs, b_ref = refs[:n_p], refs[n_p]
        r_ref = refs[n_p + 1] if res is not None else None
        n_in = n_p + 1 + (res is not None) + (dep is not None)
        o_ref = refs[n_in]
        scr = refs[n_in + 1:]

        def finish(acc):
            acc = alpha * acc if alpha != 1.0 else acc
            if r_ref is not None:
                acc = r_ref[...] + acc
            o_ref[...] = acc.astype(out_dtype)

        def product(a_ref):
            return _dot(a_ref[...].astype(BF16), b_ref[...].astype(BF16), NT if nt else (((1,), (0,)), ((), ())))

        if nk == 1:
            finish(product(a_refs[0]))
            return
        acc_ref = scr[0]
        kk = pl.program_id(2)
        for (s, cnt), a_ref in zip(starts, a_refs):
            if s == 0:
                @pl.when(kk == 0)
                def _():
                    acc_ref[...] = product(a_ref)

                @pl.when(jnp.logical_and(kk > 0, kk < cnt))
                def _():
                    acc_ref[...] += product(a_ref)
            else:
                @pl.when(jnp.logical_and(kk >= s, kk < s + cnt))
                def _():
                    acc_ref[...] += product(a_ref)

        @pl.when(kk == nk - 1)
        def _():
            finish(acc_ref[...])

    def a_spec(s, cnt):
        return pl.BlockSpec((tm, tk), lambda i, j, kk: (i, jnp.clip(kk - s, 0, cnt - 1)))

    in_specs = [a_spec(s, cnt) for s, cnt in starts]
    in_specs.append(pl.BlockSpec((tn, tk), lambda i, j, kk: (j, kk)) if nt else pl.BlockSpec((tk, tn), lambda i, j, kk: (kk, j)))
    args = pieces + [b]
    if res is not None:
        in_specs.append(pl.BlockSpec((tm, tn), lambda i, j, kk: (i, j)))
        args.append(res)
    if dep is not None:
        in_specs.append(pl.BlockSpec((8, LANES), lambda i, j, kk: (0, 0)))
        args.append(dep)
    return pl.pallas_call(
        body, name=name, grid=(m // tm, n // tn, nk), in_specs=in_specs,
        out_specs=pl.BlockSpec((tm, tn), lambda i, j, kk: (i, j)),
        out_shape=jax.ShapeDtypeStruct((m, n), out_dtype),
        scratch_shapes=[pltpu.VMEM((tm, tn), F32)] if nk > 1 else [],
        compiler_params=_params("parallel", "parallel", "arbitrary"),
    )(*args)


def _mm_tn(x, dy, name, out_dtype=BF16, alpha=1.0, dep=None, tko=1408, tn=1024, tt=512):
    t, k = x.shape
    n = dy.shape[1]
    tko, tn, tt = _tile(k, tko, LANES), _tile(n, tn, LANES), _tile(t, tt, 8)
    nt_steps = t // tt

    def body(*refs):
        x_ref, dy_ref = refs[:2]
        o_ref, acc_ref = refs[-2:]
        part = _dot(x_ref[...].astype(BF16), dy_ref[...].astype(BF16), TN)
        step = pl.program_id(2)

        @pl.when(step == 0)
        def _():
            acc_ref[...] = part

        @pl.when(step > 0)
        def _():
            acc_ref[...] += part

        @pl.when(step == nt_steps - 1)
        def _():
            acc = acc_ref[...]
            o_ref[...] = (alpha * acc if alpha != 1.0 else acc).astype(out_dtype)

    in_specs = [pl.BlockSpec((tt, tko), lambda i, j, s: (s, i)), pl.BlockSpec((tt, tn), lambda i, j, s: (s, j))]
    args = [x, dy]
    if dep is not None:
        in_specs.append(pl.BlockSpec((8, LANES), lambda i, j, s: (0, 0)))
        args.append(dep)
    return pl.pallas_call(
        body, name=name, grid=(k // tko, n // tn, nt_steps), in_specs=in_specs,
        out_specs=pl.BlockSpec((tko, tn), lambda i, j, s: (i, j)),
        out_shape=jax.ShapeDtypeStruct((k, n), out_dtype),
        scratch_shapes=[pltpu.VMEM((tko, tn), F32)],
        compiler_params=_params("parallel", "parallel", "arbitrary"),
    )(*args)


def _rms_fwd(x, g, name, tt=512):
    t, d = x.shape
    tt = _tile(t, tt, 8)

    def body(x_ref, g_ref, o_ref):
        xv = x_ref[...]
        r = lax.rsqrt(jnp.mean(xv * xv, axis=-1, keepdims=True) + EPS)
        o_ref[...] = (xv * r * g_ref[...]).astype(BF16)

    return pl.pallas_call(
        body, name=name, grid=(t // tt,),
        in_specs=[pl.BlockSpec((tt, d), lambda i: (i, 0)), pl.BlockSpec((1, d), lambda i: (0, 0))],
        out_specs=pl.BlockSpec((tt, d), lambda i: (i, 0)),
        out_shape=jax.ShapeDtypeStruct((t, d), BF16), compiler_params=_params("parallel"),
    )(x, g)


def _rms_bwd(dn, x, g, name, dres=None, tt=512):
    t, d = x.shape
    tt = _tile(t, tt, 8)

    def body(*refs):
        if dres is None:
            dn_ref, x_ref, g_ref, dx_ref, dg_ref = refs
            r_ref = None
        else:
            dn_ref, x_ref, g_ref, r_ref, dx_ref, dg_ref = refs
        xv, dnv = x_ref[...], dn_ref[...].astype(F32)
        r = lax.rsqrt(jnp.mean(xv * xv, axis=-1, keepdims=True) + EPS)
        xh = xv * r
        gy = dnv * g_ref[...]
        dx = r * (gy - xh * jnp.mean(gy * xh, axis=-1, keepdims=True))
        if r_ref is not None:
            dx = dx + r_ref[...]
        dx_ref[...] = dx
        part = jnp.sum(dnv * xh, axis=0, keepdims=True)

        @pl.when(pl.program_id(0) == 0)
        def _():
            dg_ref[...] = part

        @pl.when(pl.program_id(0) > 0)
        def _():
            dg_ref[...] += part

    row = pl.BlockSpec((tt, d), lambda i: (i, 0))
    vec = pl.BlockSpec((1, d), lambda i: (0, 0))
    in_specs, args = [row, row, vec], [dn, x, g]
    if dres is not None:
        in_specs.append(row)
        args.append(dres)
    return pl.pallas_call(
        body, name=name, grid=(t // tt,), in_specs=in_specs, out_specs=[row, vec],
        out_shape=[jax.ShapeDtypeStruct((t, d), F32), jax.ShapeDtypeStruct((1, d), F32)],
        compiler_params=_params("arbitrary"),
    )(*args)


def _swiglu_fwd(gu, name, tt=512):
    t, f2 = gu.shape
    f = f2 // 2
    tt = _tile(t, tt, 8)
    tf = _tile(f, 1408, LANES)
    nf = f // tf

    def body(g_ref, u_ref, o_ref):
        o_ref[...] = (_silu(g_ref[...].astype(F32)) * u_ref[...].astype(F32)).astype(BF16)

    return pl.pallas_call(
        body, name=name, grid=(t // tt, nf),
        in_specs=[pl.BlockSpec((tt, tf), lambda i, j: (i, j)), pl.BlockSpec((tt, tf), lambda i, j: (i, j + nf))],
        out_specs=pl.BlockSpec((tt, tf), lambda i, j: (i, j)),
        out_shape=jax.ShapeDtypeStruct((t, f), BF16), compiler_params=_params("parallel", "parallel"),
    )(gu, gu)


def _swiglu_bwd(da, gu, name, tt=512):
    t, f2 = gu.shape
    f = f2 // 2
    tt = _tile(t, tt, 8)
    tf = _tile(f, 1408, LANES)
    nf = f // tf

    def body(da_ref, g_ref, u_ref, o_ref):
        dav, gv = da_ref[...].astype(F32), g_ref[...].astype(F32)

        @pl.when(pl.program_id(1) < nf)
        def _():
            o_ref[...] = (dav * u_ref[...].astype(F32) * _dsilu(gv)).astype(BF16)

        @pl.when(pl.program_id(1) >= nf)
        def _():
            o_ref[...] = (dav * _silu(gv)).astype(BF16)

    return pl.pallas_call(
        body, name=name, grid=(t // tt, 2 * nf),
        in_specs=[pl.BlockSpec((tt, tf), lambda i, j: (i, j % nf)), pl.BlockSpec((tt, tf), lambda i, j: (i, j % nf)),
                  pl.BlockSpec((tt, tf), lambda i, j: (i, j % nf + nf))],
        out_specs=pl.BlockSpec((tt, tf), lambda i, j: (i, j)),
        out_shape=jax.ShapeDtypeStruct((t, f2), BF16), compiler_params=_params("parallel", "parallel"),
    )(da, gu, gu)


def _merge_fwd(ya, yb, proj, ga_blk, gb_blk, d, name, tt=512):
    t = ya.shape[0]
    tt = _tile(t, tt, 8)

    def body(ya_ref, yb_ref, ga_ref, gb_ref, o_ref):
        o_ref[...] = (jax.nn.sigmoid(ga_ref[...].astype(F32)) * ya_ref[...].astype(F32)
                      + jax.nn.sigmoid(gb_ref[...].astype(F32)) * yb_ref[...].astype(F32)).astype(BF16)

    row = pl.BlockSpec((tt, d), lambda i: (i, 0))
    return pl.pallas_call(
        body, name=name, grid=(t // tt,),
        in_specs=[row, row, pl.BlockSpec((tt, d), lambda i: (i, ga_blk)), pl.BlockSpec((tt, d), lambda i: (i, gb_blk))],
        out_specs=row, out_shape=jax.ShapeDtypeStruct((t, d), BF16), compiler_params=_params("parallel"),
    )(ya, yb, proj, proj)


def _merge_bwd(dm, ya, yb, proj, ga_blk, gb_blk, d, name, tt=512):
    t = ya.shape[0]
    tt = _tile(t, tt, 8)

    def body(dm_ref, ya_ref, yb_ref, ga_ref, gb_ref, dya_ref, dyb_ref, dga_ref, dgb_ref):
        dmv = dm_ref[...].astype(F32)
        sa, sb = jax.nn.sigmoid(ga_ref[...].astype(F32)), jax.nn.sigmoid(gb_ref[...].astype(F32))
        dya_ref[...] = (dmv * sa).astype(BF16)
        dyb_ref[...] = (dmv * sb).astype(BF16)
        dga_ref[...] = (dmv * ya_ref[...].astype(F32) * sa * (1.0 - sa)).astype(BF16)
        dgb_ref[...] = (dmv * yb_ref[...].astype(F32) * sb * (1.0 - sb)).astype(BF16)

    row = pl.BlockSpec((tt, d), lambda i: (i, 0))
    out = jax.ShapeDtypeStruct((t, d), BF16)
    return pl.pallas_call(
        body, name=name, grid=(t // tt,),
        in_specs=[row, row, row, pl.BlockSpec((tt, d), lambda i: (i, ga_blk)), pl.BlockSpec((tt, d), lambda i: (i, gb_blk))],
        out_specs=[row] * 4, out_shape=[out] * 4, compiler_params=_params("parallel"),
    )(dm, ya, yb, proj, proj)


def _loss_head(h, g, target, name, tt=512):
    t, d = h.shape
    tt = _tile(t, tt, 8)

    def body(h_ref, g_ref, tg_ref, loss_ref, dh_ref, dg_ref):
        xv = h_ref[...]
        r = lax.rsqrt(jnp.mean(xv * xv, axis=-1, keepdims=True) + EPS)
        xh = xv * r
        err = xh * g_ref[...] - tg_ref[...]
        dout = err * (1.0 / d)
        gy = dout * g_ref[...]
        dh_ref[...] = r * (gy - xh * jnp.mean(gy * xh, axis=-1, keepdims=True))
        dg_part = jnp.sum(dout * xh, axis=0, keepdims=True)
        loss_part = jnp.full((1, LANES), 0.5 / d, F32) * jnp.sum(err * err)

        @pl.when(pl.program_id(0) == 0)
        def _():
            dg_ref[...] = dg_part
            loss_ref[...] = loss_part

        @pl.when(pl.program_id(0) > 0)
        def _():
            dg_ref[...] += dg_part
            loss_ref[...] += loss_part

    row = pl.BlockSpec((tt, d), lambda i: (i, 0))
    vec = pl.BlockSpec((1, d), lambda i: (0, 0))
    return pl.pallas_call(
        body, name=name, grid=(t // tt,), in_specs=[row, vec, row],
        out_specs=[pl.BlockSpec((1, LANES), lambda i: (0, 0)), row, vec],
        out_shape=[jax.ShapeDtypeStruct((1, LANES), F32), jax.ShapeDtypeStruct((t, d), F32), jax.ShapeDtypeStruct((1, d), F32)],
        compiler_params=_params("arbitrary"),
    )(h, g, target)


def _shift_down(x, k, t_idx):
    if k == 0:
        return x
    return jnp.where(t_idx >= k, pltpu.roll(x, k, 0), 0.0)


def _shift_up(x, k, t_idx, s):
    if k == 0:
        return x
    return jnp.where(t_idx < s - k, pltpu.roll(x, s - k, 0), 0.0)


def _conv_a_fwd(proj, w, nb, s, d, name, cb=256):
    cb = _tile(d, cb, LANES)
    nd = d // cb

    def body(b_ref, c_ref, v_ref, w_ref, o_ref):
        t_idx = lax.broadcasted_iota(jnp.int32, (s, cb), 0)
        cv = c_ref[...].astype(F32) * v_ref[...].astype(F32)
        cc = sum(w_ref[k:k + 1, :] * _shift_down(cv, CONV_A_K - 1 - k, t_idx) for k in range(CONV_A_K))
        o_ref[...] = (b_ref[...].astype(F32) * cc).astype(BF16)

    def col(off):
        return pl.BlockSpec((s, cb), lambda b, j: (b, j + off * nd))

    return pl.pallas_call(
        body, name=name, grid=(nb, nd), in_specs=[col(0), col(1), col(2), pl.BlockSpec((8, cb), lambda b, j: (0, j))],
        out_specs=pl.BlockSpec((s, cb), lambda b, j: (b, j)),
        out_shape=jax.ShapeDtypeStruct((nb * s, d), BF16), compiler_params=_params("parallel", "parallel"),
    )(proj, proj, proj, w)


def _conv_a_bwd(dy, proj, w, nb, s, d, name, cb=256):
    cb = _tile(d, cb, LANES)
    nd = d // cb

    def body(dy_ref, b_ref, c_ref, v_ref, w_ref, db_ref, dc_ref, dv_ref, dw_ref):
        t_idx = lax.broadcasted_iota(jnp.int32, (s, cb), 0)
        cv_c, cv_v = c_ref[...].astype(F32), v_ref[...].astype(F32)
        cv = cv_c * cv_v
        shifted = [_shift_down(cv, CONV_A_K - 1 - k, t_idx) for k in range(CONV_A_K)]
        cc = sum(w_ref[k:k + 1, :] * shifted[k] for k in range(CONV_A_K))
        dyv = dy_ref[...].astype(F32)
        db_ref[...] = (dyv * cc).astype(BF16)
        dcc = dyv * b_ref[...].astype(F32)
        dcv = sum(w_ref[k:k + 1, :] * _shift_up(dcc, CONV_A_K - 1 - k, t_idx, s) for k in range(CONV_A_K))
        dc_ref[...] = (dcv * cv_v).astype(BF16)
        dv_ref[...] = (dcv * cv_c).astype(BF16)
        rows = [jnp.sum(dcc * shifted[k], axis=0, keepdims=True) for k in range(CONV_A_K)]
        part = _stack_rows(rows, cb)

        @pl.when(pl.program_id(1) == 0)
        def _():
            dw_ref[...] = part

        @pl.when(pl.program_id(1) > 0)
        def _():
            dw_ref[...] += part

    def col(off):
        return pl.BlockSpec((s, cb), lambda j, b: (b, j + off * nd))

    own = pl.BlockSpec((s, cb), lambda j, b: (b, j))
    wspec = pl.BlockSpec((8, cb), lambda j, b: (0, j))
    out = jax.ShapeDtypeStruct((nb * s, d), BF16)
    return pl.pallas_call(
        body, name=name, grid=(nd, nb), in_specs=[own, col(0), col(1), col(2), wspec],
        out_specs=[own, own, own, wspec], out_shape=[out, out, out, jax.ShapeDtypeStruct((8, d), F32)],
        compiler_params=_params("parallel", "arbitrary"),
    )(dy, proj, proj, proj, w)


def _conv_s_fwd(proj, col0, w, bias, nb, s, cc_width, name, cb=256):
    cb = _tile(math.gcd(cc_width, col0) if col0 else cc_width, cb, LANES)
    nd, off = cc_width // cb, col0 // cb

    def body(x_ref, w_ref, b_ref, o_ref):
        t_idx = lax.broadcasted_iota(jnp.int32, (s, cb), 0)
        xv = x_ref[...].astype(F32)
        pre = b_ref[...] + sum(w_ref[k:k + 1, :] * _shift_down(xv, SSM_CONV_K - 1 - k, t_idx) for k in range(SSM_CONV_K))
        o_ref[...] = _silu(pre).astype(BF16)

    vec = pl.BlockSpec((8, cb), lambda b, j: (0, j))
    return pl.pallas_call(
        body, name=name, grid=(nb, nd),
        in_specs=[pl.BlockSpec((s, cb), lambda b, j: (b, j + off)), vec, pl.BlockSpec((1, cb), lambda b, j: (0, j))],
        out_specs=pl.BlockSpec((s, cb), lambda b, j: (b, j)),
        out_shape=jax.ShapeDtypeStruct((nb * s, cc_width), BF16), compiler_params=_params("parallel", "parallel"),
    )(proj, w, bias)


def _conv_s_bwd(dxc, proj, col0, w, bias, nb, s, cc_width, name, cb=256):
    cb = _tile(math.gcd(cc_width, col0) if col0 else cc_width, cb, LANES)
    nd, off = cc_width // cb, col0 // cb

    def body(d_ref, x_ref, w_ref, b_ref, dx_ref, dw_ref, db_ref):
        t_idx = lax.broadcasted_iota(jnp.int32, (s, cb), 0)
        xv = x_ref[...].astype(F32)
        shifted = [_shift_down(xv, SSM_CONV_K - 1 - k, t_idx) for k in range(SSM_CONV_K)]
        pre = b_ref[...] + sum(w_ref[k:k + 1, :] * shifted[k] for k in range(SSM_CONV_K))
        dpre = d_ref[...].astype(F32) * _dsilu(pre)
        dx = sum(w_ref[k:k + 1, :] * _shift_up(dpre, SSM_CONV_K - 1 - k, t_idx, s) for k in range(SSM_CONV_K))
        dx_ref[...] = dx.astype(BF16)
        rows = [jnp.sum(dpre * shifted[k], axis=0, keepdims=True) for k in range(SSM_CONV_K)]
        dw_part = _stack_rows(rows, cb)
        db_part = jnp.sum(dpre, axis=0, keepdims=True)

        @pl.when(pl.program_id(1) == 0)
        def _():
            dw_ref[...] = dw_part
            db_ref[...] = db_part

        @pl.when(pl.program_id(1) > 0)
        def _():
            dw_ref[...] += dw_part
            db_ref[...] += db_part

    own = pl.BlockSpec((s, cb), lambda j, b: (b, j))
    wspec = pl.BlockSpec((8, cb), lambda j, b: (0, j))
    bspec = pl.BlockSpec((1, cb), lambda j, b: (0, j))
    return pl.pallas_call(
        body, name=name, grid=(nd, nb),
        in_specs=[own, pl.BlockSpec((s, cb), lambda j, b: (b, j + off)), wspec, bspec],
        out_specs=[own, wspec, bspec],
        out_shape=[jax.ShapeDtypeStruct((nb * s, cc_width), BF16), jax.ShapeDtypeStruct((8, cc_width), F32),
                   jax.ShapeDtypeStruct((1, cc_width), F32)],
        compiler_params=_params("parallel", "arbitrary"),
    )(dxc, proj, w, bias)


def _ssd_common(xc_ref, dtr_ref, dtrt_ref, prow_ref, pcol_ref, e_ref, di):
    l = SSM_CHUNK
    bias_r, a_r = prow_ref[0:1, :], -jnp.exp(prow_ref[1:2, :])
    sp_in = dtr_ref[...] + bias_r
    dt = _softplus(sp_in)
    li = lax.broadcasted_iota(jnp.int32, (l, l), 0)
    si = lax.broadcasted_iota(jnp.int32, (l, l), 1)
    lower = (li >= si).astype(F32)
    upper = (li <= si).astype(F32)
    acs = _dot(lower, dt * a_r, precision=HI)
    bias_c, a_c = pcol_ref[:, 0:1], -jnp.exp(pcol_ref[:, 1:2])
    dt_t = _softplus(dtrt_ref[...] + bias_c)
    acs_t = _dot(dt_t * a_c, upper, precision=HI)
    e_mat = e_ref[...]
    dt_exp = _dot(dt, e_mat, precision=HI)
    acs_exp = _dot(acs, e_mat, precision=HI)
    acs_last = acs_exp[l - 1:l, :]
    x = xc_ref[:, 0:di].astype(F32)
    return dict(dt=dt, a_r=a_r, sp_in=sp_in, acs=acs, acs_t=acs_t, dt_exp=dt_exp, e_exp=jnp.exp(acs_exp),
                el_exp=jnp.exp(acs_last), f_exp=jnp.exp(acs_last - acs_exp), x=x, mask=li >= si, upper=upper,
                d_exp=_dot(prow_ref[0:8, :], e_mat, precision=HI)[2:3, :])


def _decay(q, h):
    seg = q["acs"][:, h:h + 1] - q["acs_t"][h:h + 1, :]
    return jnp.exp(jnp.where(q["mask"], seg, NEG_BIG))


def _ssd_fwd(xc, dtr, dtrt, prow, pcol, e_mat, nb, nc, di, name):
    l, n, g_n, p = SSM_CHUNK, SSM_STATE, SSM_GROUPS, SSM_HEAD_DIM
    cc = xc.shape[1]
    gw = di // g_n
    assert p * 2 == LANES and gw % LANES == 0

    def body(xc_ref, dtr_ref, dtrt_ref, prow_ref, pcol_ref, e_ref, y_ref, sprev_ref, st_ref):
        @pl.when(pl.program_id(1) == 0)
        def _():
            st_ref[...] = jnp.zeros_like(st_ref)

        q = _ssd_common(xc_ref, dtr_ref, dtrt_ref, prow_ref, pcol_ref, e_ref, di)
        x = q["x"]
        xd = x * q["dt_exp"]
        xdb = xd.astype(BF16)
        xdf = (xd * q["f_exp"]).astype(BF16)
        lane = lax.broadcasted_iota(jnp.int32, (l, LANES), 1)
        for g in range(g_n):
            lo = g * gw
            bg = xc_ref[:, di + g * n: di + (g + 1) * n]
            cg = xc_ref[:, di + g_n * n + g * n: di + g_n * n + (g + 1) * n]
            cb = _dot(cg, bg, NT)
            st_g = st_ref[:, lo:lo + gw]
            y_off = q["e_exp"][:, lo:lo + gw] * _dot(cg, st_g.astype(BF16))
            for pr in range(gw // LANES):
                c0 = lo + pr * LANES
                h0 = c0 // p
                xp = xdb[:, c0:c0 + LANES]
                m0 = (cb * _decay(q, h0)).astype(BF16)
                m1 = (cb * _decay(q, h0 + 1)).astype(BF16)
                yd = _dot(m0, jnp.where(lane < p, xp, 0)) + _dot(m1, jnp.where(lane >= p, xp, 0))
                y_ref[:, c0:c0 + LANES] = (yd + y_off[:, pr * LANES:(pr + 1) * LANES]
                                           + q["d_exp"][:, c0:c0 + LANES] * x[:, c0:c0 + LANES])
            sprev_ref[:, lo:lo + gw] = st_g
            st_ref[:, lo:lo + gw] = q["el_exp"][:, lo:lo + gw] * st_g + _dot(bg, xdf[:, lo:lo + gw], TN)

    tok = lambda w: pl.BlockSpec((l, w), lambda b, c: (b * nc + c, 0))
    const = lambda r, w: pl.BlockSpec((r, w), lambda b, c: (0, 0))
    return pl.pallas_call(
        body, name=name, grid=(nb, nc),
        in_specs=[tok(cc), tok(LANES), pl.BlockSpec((LANES, l), lambda b, c: (0, b * nc + c)),
                  const(8, LANES), const(LANES, 8), const(LANES, di)],
        out_specs=[tok(di), pl.BlockSpec((None, n, di), lambda b, c: (b * nc + c, 0, 0))],
        out_shape=[jax.ShapeDtypeStruct((nb * nc * l, di), F32), jax.ShapeDtypeStruct((nb * nc, n, di), F32)],
        scratch_shapes=[pltpu.VMEM((n, di), F32)],
        compiler_params=_params("parallel", "arbitrary"),
    )(xc, dtr, dtrt, prow, pcol, e_mat)


def _ssd_bwd(dy, xc, dtr, dtrt, prow, pcol, e_mat, sprev, nb, nc, di, name):
    l, n, g_n, p = SSM_CHUNK, SSM_STATE, SSM_GROUPS, SSM_HEAD_DIM
    cc = xc.shape[1]
    gw = di // g_n

    def body(dy_ref, xc_ref, dtr_ref, dtrt_ref, prow_ref, pcol_ref, e_ref, sprev_ref,
             dxc_ref, ddtr_ref, sums_ref, dst_ref, off_ref, dxd_ref, last_ref, dla_diag_ref, vst_ref):
        first = jnp.logical_and(pl.program_id(0) == 0, pl.program_id(1) == 0)

        @pl.when(pl.program_id(1) == 0)
        def _():
            dst_ref[...] = jnp.zeros_like(dst_ref)

        dla_diag_ref[...] = jnp.zeros_like(dla_diag_ref)
        strict_lower = lax.broadcasted_iota(jnp.int32, (l, l), 0) > lax.broadcasted_iota(jnp.int32, (l, l), 1)

        q = _ssd_common(xc_ref, dtr_ref, dtrt_ref, prow_ref, pcol_ref, e_ref, di)
        x = q["x"]
        xd = x * q["dt_exp"]
        xdb = xd.astype(BF16)
        xdf = (xd * q["f_exp"]).astype(BF16)
        dyv = dy_ref[...]
        dyb = dyv.astype(BF16)
        dye = (dyv * q["e_exp"]).astype(BF16)
        upper_b = q["upper"].astype(BF16)
        lane = lax.broadcasted_iota(jnp.int32, (l, LANES), 1)
        for g in range(g_n):
            lo = g * gw
            bg = xc_ref[:, di + g * n: di + (g + 1) * n]
            cg = xc_ref[:, di + g_n * n + g * n: di + g_n * n + (g + 1) * n]
            cb = _dot(cg, bg, NT)
            st_g = sprev_ref[:, lo:lo + gw]
            st_gb = st_g.astype(BF16)
            dst_g = dst_ref[:, lo:lo + gw]
            dst_gb = dst_g.astype(BF16)
            dye_g = dye[:, lo:lo + gw]
            xdf_g = xdf[:, lo:lo + gw]
            y_off = q["e_exp"][:, lo:lo + gw] * _dot(cg, st_gb)
            dc_g = _dot(dye_g, st_gb, NT)
            db_g = _dot(xdf_g, dst_gb, NT)
            dxd_state = _dot(bg, dst_gb) * q["f_exp"][:, lo:lo + gw]
            last_ref[:, lo:lo + gw] = jnp.sum(dst_g * st_g, axis=0, keepdims=True)
            dst_ref[:, lo:lo + gw] = q["el_exp"][:, lo:lo + gw] * dst_g + _dot(cg, dye_g, TN)
            off_ref[:, lo:lo + gw] = dyv[:, lo:lo + gw] * y_off
            vst_ref[:, lo:lo + gw] = xd[:, lo:lo + gw] * dxd_state
            dcb = jnp.zeros((l, l), F32)
            for pr in range(gw // LANES):
                c0 = lo + pr * LANES
                h0 = c0 // p
                xp = xdb[:, c0:c0 + LANES]
                dyp = dyb[:, c0:c0 + LANES]
                dxd_diag = jnp.zeros((l, LANES), F32)
                for k, keep in enumerate((lane < p, lane >= p)):
                    dec = _decay(q, h0 + k)
                    dy_h = jnp.where(keep, dyp, 0)
                    dm_dec = _dot(dy_h, xp, NT) * dec
                    dcb = dcb + dm_dec
                    dxd_diag = dxd_diag + _dot((cb * dec).astype(BF16), dy_h, TN)
                    above = _dot(upper_b, (dm_dec * cb).astype(BF16))
                    dla_col = jnp.sum(jnp.where(strict_lower, above, 0.0), axis=1, keepdims=True)
                    dla_diag_ref[...] = jnp.where(lane == h0 + k, dla_col, dla_diag_ref[...])
                dxd_ref[:, c0:c0 + LANES] = dxd_diag + dxd_state[:, pr * LANES:(pr + 1) * LANES]
            dcb_b = dcb.astype(BF16)
            dxc_ref[:, di + g * n: di + (g + 1) * n] = (db_g + _dot(dcb_b, cg, TN)).astype(BF16)
            dxc_ref[:, di + g_n * n + g * n: di + g_n * n + (g + 1) * n] = (dc_g + _dot(dcb_b, bg)).astype(BF16)
        dxd = dxd_ref[...]
        e_mat = e_ref[...]
        from_y = _dot(q["upper"], _dot(off_ref[...], e_mat, NT, precision=HI), precision=HI)
        from_s = _dot(strict_lower.astype(F32), _dot(vst_ref[...], e_mat, NT, precision=HI), precision=HI)
        carried = (_dot(jnp.broadcast_to(last_ref[...], (8, di)), e_mat, NT, precision=HI)[0:1, :]
                   * jnp.exp(q["acs"][l - 1:l, :]))
        dla = from_y + from_s + carried + dla_diag_ref[...]
        ddt = dla * q["a_r"] + _dot(dxd * x, e_mat, NT, precision=HI)
        ddtr = ddt * jax.nn.sigmoid(q["sp_in"])
        ddtr_ref[...] = ddtr
        dxc_ref[:, 0:di] = (dxd * q["dt_exp"] + q["d_exp"] * dyv).astype(BF16)
        dd_exp = jnp.sum(dyv * x, axis=0, keepdims=True)
        dd = _dot(jnp.broadcast_to(dd_exp, (8, di)), e_mat, NT, precision=HI)[0:1, :]
        part = _stack_rows([jnp.sum(ddtr, axis=0, keepdims=True),
                            jnp.sum(dla * q["dt"], axis=0, keepdims=True) * q["a_r"], dd], LANES)

        @pl.when(first)
        def _():
            sums_ref[...] = part

        @pl.when(jnp.logical_not(first))
        def _():
            sums_ref[...] += part

    rev = lambda b, c: b * nc + (nc - 1 - c)
    tok = lambda w: pl.BlockSpec((l, w), lambda b, c: (rev(b, c), 0))
    const = lambda r, w: pl.BlockSpec((r, w), lambda b, c: (0, 0))
    return pl.pallas_call(
        body, name=name, grid=(nb, nc),
        in_specs=[tok(di), tok(cc), tok(LANES), pl.BlockSpec((LANES, l), lambda b, c: (0, rev(b, c))),
                  const(8, LANES), const(LANES, 8), const(LANES, di),
                  pl.BlockSpec((None, n, di), lambda b, c: (rev(b, c), 0, 0))],
        out_specs=[tok(cc), tok(LANES), const(8, LANES)],
        out_shape=[jax.ShapeDtypeStruct((nb * nc * l, cc), BF16), jax.ShapeDtypeStruct((nb * nc * l, LANES), F32),
                   jax.ShapeDtypeStruct((8, LANES), F32)],
        scratch_shapes=[pltpu.VMEM((n, di), F32), pltpu.VMEM((l, di), F32), pltpu.VMEM((l, di), F32),
                        pltpu.VMEM((1, di), F32), pltpu.VMEM((l, LANES), F32), pltpu.VMEM((l, di), F32)],
        compiler_params=_params("arbitrary", "arbitrary"),
    )(dy, xc, dtr, dtrt, prow, pcol, e_mat, sprev)


def _gate_norm_fwd(y, proj, z_col0, norm_g, di, name, tt=256):
    t = y.shape[0]
    tt = _tile(t, tt, 8)
    gw = di // SSM_GROUPS
    zw = _tile(math.gcd(di, z_col0), di, LANES)
    nz, zoff = di // zw, z_col0 // zw

    def body(*refs):
        y_ref, z_refs, g_ref, o_ref = refs[0], refs[1:1 + nz], refs[1 + nz], refs[2 + nz]
        for g in range(SSM_GROUPS):
            lo = g * gw
            zv = z_refs[lo // zw][:, lo % zw:lo % zw + gw].astype(F32)
            yg = y_ref[:, lo:lo + gw] * _silu(zv)
            r = lax.rsqrt(jnp.mean(yg * yg, axis=-1, keepdims=True) + EPS)
            o_ref[:, lo:lo + gw] = (yg * r * g_ref[:, lo:lo + gw]).astype(BF16)

    row = pl.BlockSpec((tt, di), lambda i: (i, 0))
    zspecs = [pl.BlockSpec((tt, zw), functools.partial(lambda i, k: (i, zoff + k), k=k)) for k in range(nz)]
    return pl.pallas_call(
        body, name=name, grid=(t // tt,), in_specs=[row] + zspecs + [pl.BlockSpec((1, di), lambda i: (0, 0))],
        out_specs=row, out_shape=jax.ShapeDtypeStruct((t, di), BF16), compiler_params=_params("parallel"),
    )(y, *([proj] * nz), norm_g)


def _gate_norm_bwd(dn, y, proj, z_col0, norm_g, di, name, tt=256):
    t = y.shape[0]
    tt = _tile(t, tt, 8)
    gw = di // SSM_GROUPS
    zw = _tile(math.gcd(di, z_col0), di, LANES)
    nz, zoff = di // zw, z_col0 // zw

    def body(*refs):
        dn_ref, y_ref, z_refs, g_ref = refs[0], refs[1], refs[2:2 + nz], refs[2 + nz]
        dy_ref, dz_ref, dg_ref = refs[3 + nz:]
        first = pl.program_id(0) == 0
        for g in range(SSM_GROUPS):
            lo = g * gw
            zv = z_refs[lo // zw][:, lo % zw:lo % zw + gw].astype(F32)
            yv = y_ref[:, lo:lo + gw]
            sz = _silu(zv)
            yg = yv * sz
            r = lax.rsqrt(jnp.mean(yg * yg, axis=-1, keepdims=True) + EPS)
            yh = yg * r
            dnv = dn_ref[:, lo:lo + gw].astype(F32)
            gy = dnv * g_ref[:, lo:lo + gw]
            dyg = r * (gy - yh * jnp.mean(gy * yh, axis=-1, keepdims=True))
            dy_ref[:, lo:lo + gw] = dyg * sz
            dz_ref[:, lo:lo + gw] = (dyg * yv * _dsilu(zv)).astype(BF16)
            part = jnp.sum(dnv * yh, axis=0, keepdims=True)

            @pl.when(first)
            def _():
                dg_ref[:, lo:lo + gw] = part

            @pl.when(jnp.logical_not(first))
            def _():
                dg_ref[:, lo:lo + gw] += part

    row = pl.BlockSpec((tt, di), lambda i: (i, 0))
    vec = pl.BlockSpec((1, di), lambda i: (0, 0))
    zspecs = [pl.BlockSpec((tt, zw), functools.partial(lambda i, k: (i, zoff + k), k=k)) for k in range(nz)]
    return pl.pallas_call(
        body, name=name, grid=(t // tt,), in_specs=[row, row] + zspecs + [vec], out_specs=[row, row, vec],
        out_shape=[jax.ShapeDtypeStruct((t, di), F32), jax.ShapeDtypeStruct((t, di), BF16), jax.ShapeDtypeStruct((1, di), F32)],
        compiler_params=_params("arbitrary"),
    )(dn, y, *([proj] * nz), norm_g)


def _softmax_rows(s):
    s = s - jnp.max(s, axis=-1, keepdims=True)
    e = jnp.exp(s)
    return e / jnp.sum(e, axis=-1, keepdims=True)


def _xattn_fwd(q, kv, nb, s, m, d, name, tq=512):
    tq = _tile(s, tq, 8)
    nq = s // tq
    hd = d // XATTN_HEADS
    scale = 1.0 / math.sqrt(hd)

    def body(q_ref, k_ref, v_ref, o_ref):
        for h in range(XATTN_HEADS):
            sl = slice(h * hd, (h + 1) * hd)
            prob = _softmax_rows(_dot(q_ref[:, sl], k_ref[:, sl], NT) * scale)
            o_ref[:, sl] = _dot(prob.astype(BF16), v_ref[:, sl]).astype(BF16)

    return pl.pallas_call(
        body, name=name, grid=(nb, nq),
        in_specs=[pl.BlockSpec((tq, d), lambda b, i: (b * nq + i, 0)), pl.BlockSpec((m, d), lambda b, i: (b, 0)),
                  pl.BlockSpec((m, d), lambda b, i: (b, 1))],
        out_specs=pl.BlockSpec((tq, d), lambda b, i: (b * nq + i, 0)),
        out_shape=jax.ShapeDtypeStruct((nb * s, d), BF16), compiler_params=_params("parallel", "parallel"),
    )(q, kv, kv)


def _xattn_bwd(do, q, kv, nb, s, m, d, name, tq=512):
    tq = _tile(s, tq, 8)
    nq = s // tq
    hd = d // XATTN_HEADS
    scale = 1.0 / math.sqrt(hd)

    def body(do_ref, q_ref, k_ref, v_ref, dq_ref, dk_ref, dv_ref):
        first = pl.program_id(1) == 0
        for h in range(XATTN_HEADS):
            sl = slice(h * hd, (h + 1) * hd)
            qh, kh, vh, doh = q_ref[:, sl], k_ref[:, sl], v_ref[:, sl], do_ref[:, sl]
            prob = _softmax_rows(_dot(qh, kh, NT) * scale)
            dv_h = _dot(prob.astype(BF16), doh, TN)
            dp = _dot(doh, vh, NT)
            ds = (prob * (dp - jnp.sum(dp * prob, axis=-1, keepdims=True)) * scale).astype(BF16)
            dq_ref[:, sl] = _dot(ds, kh).astype(BF16)
            dk_h = _dot(ds, qh, TN)

            @pl.when(first)
            def _():
                dk_ref[:, sl] = dk_h
                dv_ref[:, sl] = dv_h

            @pl.when(jnp.logical_not(first))
            def _():
                dk_ref[:, sl] += dk_h
                dv_ref[:, sl] += dv_h

    qspec = pl.BlockSpec((tq, d), lambda b, i: (b * nq + i, 0))
    dq, dk, dv = pl.pallas_call(
        body, name=name, grid=(nb, nq),
        in_specs=[qspec, qspec, pl.BlockSpec((m, d), lambda b, i: (b, 0)), pl.BlockSpec((m, d), lambda b, i: (b, 1))],
        out_specs=[qspec, pl.BlockSpec((m, d), lambda b, i: (b, 0)), pl.BlockSpec((m, d), lambda b, i: (b, 0))],
        out_shape=[jax.ShapeDtypeStruct((nb * s, d), BF16), jax.ShapeDtypeStruct((nb * m, d), F32),
                   jax.ShapeDtypeStruct((nb * m, d), F32)],
        compiler_params=_params("parallel", "arbitrary"),
    )(do, q, kv, kv)
    return dq, dk, dv


def _all_gather(shards, name):
    n_arr = len(shards)

    def body(*refs):
        x_refs, out_refs = refs[:n_arr], refs[n_arr:2 * n_arr]
        send_sems, recv_sems, local_sems = refs[2 * n_arr:]
        x, y, c = lax.axis_index("x"), lax.axis_index("y"), lax.axis_index("c")
        me, sibling = (x, y, c), (x, y, 1 - c)
        chips = [(1 - x, y), (x, 1 - y), (1 - x, 1 - y)]

        def copy(w, k, block, to, from_input=False):
            px, py, pc = block
            rows = out_refs[w].at[4 * px + 2 * py + pc]
            return pltpu.make_async_remote_copy(
                src_ref=x_refs[w] if from_input else rows, dst_ref=rows,
                send_sem=send_sems.at[7 * w + k], recv_sem=recv_sems.at[7 * w + k], device_id=to, device_id_type=MESH)

        started = []
        for w in range(n_arr):
            mine = pltpu.make_async_copy(x_refs[w], out_refs[w].at[4 * x + 2 * y + c], local_sems.at[w])
            mine.start()
            started.append(mine)
        sends = []
        for w in range(n_arr):
            sends.append(copy(w, 0, me, sibling, from_input=True))
            sends += [copy(w, 1 + j, me, (*chip, c), from_input=True) for j, chip in enumerate(chips)]
        for cp in sends:
            cp.start()
        for j, chip in enumerate(chips):
            for w in range(n_arr):
                copy(w, 1 + j, (*chip, c), me).wait_recv()
                passed = copy(w, 4 + j, (*chip, c), sibling)
                passed.start()
                sends.append(passed)
        for w in range(n_arr):
            copy(w, 0, sibling, me).wait_recv()
            for j, chip in enumerate(chips):
                copy(w, 4 + j, (*chip, 1 - c), me).wait_recv()
        for cp in sends:
            cp.wait_send()
        for mine in started:
            mine.wait()

    hbm = pl.BlockSpec(memory_space=pl.ANY)
    return pl.pallas_call(
        body, name=name, out_shape=[jax.ShapeDtypeStruct((N_DEV,) + s.shape, s.dtype) for s in shards],
        in_specs=[hbm] * n_arr, out_specs=[hbm] * n_arr,
        scratch_shapes=[pltpu.SemaphoreType.DMA((7 * n_arr,)), pltpu.SemaphoreType.DMA((7 * n_arr,)),
                        pltpu.SemaphoreType.DMA((n_arr,))],
    )(*shards)


_HBM = pl.BlockSpec(memory_space=pltpu.HBM)
_SEM = pl.BlockSpec(memory_space=pltpu.SEMAPHORE)
_DATAFLOW = pltpu.SideEffectType.DATAFLOW_SIDE_EFFECTING


def _peer_list(x, y, c):
    return [(1 - x if k & 4 else x, 1 - y if k & 2 else y, 1 - c if k & 1 else c) for k in range(1, N_DEV)]


def _push_copy(src_ref, land_ref, send_sems, recv_sems, w, k, peer, me, per_peer_src, receiving):
    px, py, pc = peer
    peer_slot = 4 * px + 2 * py + pc
    return pltpu.make_async_remote_copy(
        src_ref=src_ref.at[peer_slot] if per_peer_src else src_ref,
        dst_ref=land_ref.at[peer_slot if receiving else me],
        send_sem=send_sems.at[7 * w + k], recv_sem=recv_sems.at[7 * w + k], device_id=peer, device_id_type=MESH)


def _push_start(srcs, per_peer_src, after, name):
    n_arr = len(srcs)
    land_shapes = [s.shape if per_peer_src else (N_DEV,) + s.shape for s in srcs]

    def body(*refs):
        src_refs, land_refs = refs[:n_arr], refs[n_arr:2 * n_arr]
        send_sems, recv_sems = refs[2 * n_arr + 1], refs[2 * n_arr + 2]
        token = refs[-1]
        x, y, c = lax.axis_index("x"), lax.axis_index("y"), lax.axis_index("c")
        me = 4 * x + 2 * y + c
        for w in range(n_arr):
            for k, peer in enumerate(_peer_list(x, y, c)):
                _push_copy(src_refs[w], land_refs[w], send_sems, recv_sems, w, k, peer, me, per_peer_src, False).start()
        token[...] = jnp.zeros_like(token)

    lands = [pltpu.with_memory_space_constraint(lax.empty(ls, s.dtype), pltpu.HBM) for ls, s in zip(land_shapes, srcs)]
    srcs_hbm = [pltpu.with_memory_space_constraint(s, pltpu.HBM) for s in srcs]
    out = pl.pallas_call(
        body, name=name,
        out_shape=(pltpu.SemaphoreType.DMA((7 * n_arr,)), pltpu.SemaphoreType.DMA((7 * n_arr,)),
                   *[pltpu.HBM(s.shape, s.dtype) for s in srcs], *[pltpu.HBM(ls, s.dtype) for ls, s in zip(land_shapes, srcs)],
                   jax.ShapeDtypeStruct((8, LANES), F32)),
        in_specs=[_HBM] * (2 * n_arr) + [pl.BlockSpec(memory_space=pl.ANY)],
        out_specs=(_SEM, _SEM, *([_HBM] * (2 * n_arr)), pl.BlockSpec(memory_space=pltpu.VMEM)),
        input_output_aliases={i: 2 + i for i in range(2 * n_arr)},
        compiler_params=pltpu.CompilerParams(has_side_effects=_DATAFLOW),
    )(*srcs_hbm, *lands, after)
    return dict(send=out[0], recv=out[1], srcs=list(out[2:2 + n_arr]), lands=list(out[2 + n_arr:2 + 2 * n_arr]),
                token=out[-1])


def _push_wait(pending, per_peer_src, after, name):
    n_arr = len(pending["srcs"])

    def body(*refs):
        src_refs, land_refs = refs[:n_arr], refs[n_arr:2 * n_arr]
        send_sems, recv_sems = refs[2 * n_arr], refs[2 * n_arr + 1]
        x, y, c = lax.axis_index("x"), lax.axis_index("y"), lax.axis_index("c")
        me = 4 * x + 2 * y + c
        for w in range(n_arr):
            for k, peer in enumerate(_peer_list(x, y, c)):
                cp = _push_copy(src_refs[w], land_refs[w], send_sems, recv_sems, w, k, peer, me, per_peer_src, True)
                cp.wait_send()
                cp.wait_recv()

    out = pl.pallas_call(
        body, name=name,
        out_shape=tuple(pltpu.HBM(a.shape, a.dtype) for a in pending["srcs"] + pending["lands"]),
        in_specs=[_HBM] * (2 * n_arr) + [_SEM, _SEM, pl.BlockSpec(memory_space=pl.ANY)],
        out_specs=tuple([_HBM] * (2 * n_arr)),
        input_output_aliases={i: i for i in range(2 * n_arr)},
        compiler_params=pltpu.CompilerParams(has_side_effects=_DATAFLOW),
    )(*pending["srcs"], *pending["lands"], pending["send"], pending["recv"], after)
    return list(out[:n_arr]), list(out[n_arr:])


def _adamw_math(w, g, m, v):
    m = ADAM_B1 * m + (1.0 - ADAM_B1) * g
    v = ADAM_B2 * v + (1.0 - ADAM_B2) * (g * g)
    m_hat = m / (1.0 - ADAM_B1 ** ADAM_STEP)
    v_hat = v / (1.0 - ADAM_B2 ** ADAM_STEP)
    delta = -ADAM_LR * (m_hat / (jnp.sqrt(v_hat) + ADAM_EPS) + ADAM_WD * w)
    return delta, m, v


def _sum8(parts, name, tr=512):
    _, r, c_dim = parts.shape
    tr = _tile(r, tr, BF16_SUBLANES)

    def body(p_ref, o_ref):
        acc = p_ref[0].astype(F32)
        for k in range(1, N_DEV):
            acc = acc + p_ref[k].astype(F32)
        o_ref[...] = acc

    return pl.pallas_call(
        body, name=name, grid=(r // tr,), in_specs=[pl.BlockSpec((N_DEV, tr, c_dim), lambda i: (0, i, 0))],
        out_specs=pl.BlockSpec((tr, c_dim), lambda i: (i, 0)),
        out_shape=jax.ShapeDtypeStruct((r, c_dim), F32), compiler_params=_params("parallel"),
    )(parts)


def _sum8_adamw(parts, w, m, v, name, tr=128):
    _, r, c_dim = parts.shape
    tr = _tile(r, tr, BF16_SUBLANES)

    def body(p_ref, w_ref, m_ref, v_ref, g_ref, d_ref, nm_ref, nv_ref):
        g = p_ref[0].astype(F32)
        for k in range(1, N_DEV):
            g = g + p_ref[k].astype(F32)
        g_ref[...] = g
        d_ref[...], nm_ref[...], nv_ref[...] = _adamw_math(w_ref[...], g, m_ref[...], v_ref[...])

    row = pl.BlockSpec((tr, c_dim), lambda i: (i, 0))
    out = jax.ShapeDtypeStruct((r, c_dim), F32)
    return pl.pallas_call(
        body, name=name, grid=(r // tr,), in_specs=[pl.BlockSpec((N_DEV, tr, c_dim), lambda i: (0, i, 0)), row, row, row],
        out_specs=[row] * 4, out_shape=[out] * 4, compiler_params=_params("parallel"),
    )(parts, w, m, v)


def _adamw(g, w, m, v, name):
    r, c_dim = g.shape

    def body(g_ref, w_ref, m_ref, v_ref, d_ref, nm_ref, nv_ref):
        d_ref[...], nm_ref[...], nv_ref[...] = _adamw_math(w_ref[...], g_ref[...], m_ref[...], v_ref[...])

    out = jax.ShapeDtypeStruct((r, c_dim), F32)
    return pl.pallas_call(body, name=name, out_shape=[out] * 3)(g, w, m, v)


def _pack_rows(arrays, dtype, row_unit):
    chunks, offs, r0 = [], [], 0
    for a in arrays:
        flat = a.reshape(-1).astype(dtype)
        rows = -(-flat.shape[0] // (LANES * row_unit)) * row_unit
        flat = jnp.pad(flat, (0, rows * LANES - flat.shape[0]))
        chunks.append(flat.reshape(rows, LANES))
        offs.append((r0, rows))
        r0 += rows
    return jnp.concatenate(chunks, axis=0), offs


def _unpack_rows(packed, offs, shapes):
    out = []
    for (r0, rows), shape in zip(offs, shapes):
        n = math.prod(shape)
        blk = packed[..., r0:r0 + rows, :]
        blk = blk.reshape(packed.shape[:-2] + (rows * LANES,))[..., :n]
        out.append(blk.reshape(packed.shape[:-2] + tuple(shape)))
    return out


def _full_from_slots(blk, col_sharded):
    _, r, c = blk.shape
    if col_sharded:
        return blk.transpose(1, 0, 2).reshape(r, N_DEV * c)
    return blk.reshape(N_DEV * r, c)


def _slots_from_full(full, col_sharded):
    r, c = full.shape
    if col_sharded:
        return full.reshape(r, N_DEV, c // N_DEV).transpose(1, 0, 2)
    return full.reshape(N_DEV, r // N_DEV, c)


def _ffn_fwd(h, g, w_gu_t, w_d, tag):
    n = _rms_fwd(h, g, f"{tag}_norm")
    gu = _mm(n, w_gu_t, BF16, f"{tag}_up", nt=True)
    a = _swiglu_fwd(gu, f"{tag}_swiglu")
    h_out = _mm(a, w_d, F32, f"{tag}_down", res=h, alpha=FFN_RES_WEIGHT)
    return h_out, (h, n, gu, a)


def _ffn_bwd(dh_out, saved, g, w_gu_t, w_d, tag, dep, send_grads):
    h, n, gu, a = saved
    dw_d = _mm_tn(a, dh_out, f"{tag}_dw_down", alpha=FFN_RES_WEIGHT, dep=dep)
    da = _mm(dh_out, w_d, BF16, f"{tag}_da", nt=True, alpha=FFN_RES_WEIGHT)
    dgu = _swiglu_bwd(da, gu, f"{tag}_dswiglu")
    dw_gu_t = _mm_tn(dgu, n, f"{tag}_dw_up")
    dep = send_grads(dw_gu_t, dw_d)
    dn = _mm(dgu, w_gu_t, F32, f"{tag}_dn", dep=dep)
    dh, dg = _rms_bwd(dn, h, g, f"{tag}_dnorm", dres=dh_out)
    return dh, dg


W_GROUPS = (("ffn1_w_gate_up", "ffn1_w_down"),
            ("w_in", "w_out_a", "w_out_ssm", "w_mix_out"),
            ("w_q", "w_kv", "w_o_x", "ffn2_w_gate_up", "ffn2_w_down"))
G_GROUPS = (("ffn2_w_gate_up", "ffn2_w_down"),
            ("w_o_x", "w_q", "w_kv", "w_mix_out", "w_out_a", "w_out_ssm", "w_in"),
            ("ffn1_w_gate_up", "ffn1_w_down"))


def _local_step(x3, mem3, target3, small, comm):
    nb, s, d = x3.shape
    m_len = mem3.shape[1]
    t = nb * s
    nc = s // SSM_CHUNK
    di = small["ssm_norm"].shape[1]
    hs = di // SSM_HEAD_DIM
    cc = di + 2 * SSM_GROUPS * SSM_STATE
    x, mem, target = x3.reshape(t, d), mem3.reshape(nb * m_len, d), target3.reshape(t, d)

    sizes = (d, d, d, di, cc, hs, d, d)
    offs = [0]
    for sz in sizes:
        offs.append(offs[-1] + sz)
    z_col0, xbc_col0 = 3 * d, 3 * d + di
    ga_blk, gb_blk = (3 * d + di + cc) // d, (4 * d + di + cc) // d

    pad_vec = lambda v: jnp.pad(v.reshape(1, -1), ((0, 0), (0, LANES - hs)))
    prow = jnp.concatenate([pad_vec(small["ssm_dt_bias"]), pad_vec(small["ssm_a_log"]), pad_vec(small["ssm_d"]),
                            jnp.zeros((5, LANES), F32)], axis=0)
    pcol = prow.T
    e_mat = (lax.broadcasted_iota(jnp.int32, (LANES, di), 0)
             == lax.broadcasted_iota(jnp.int32, (LANES, di), 1) // SSM_HEAD_DIM).astype(F32)
    conv_a_w8 = jnp.pad(small["conv_a_w"][0], ((0, 8 - CONV_A_K), (0, 0)))
    ssm_conv_w8 = jnp.pad(small["ssm_conv_w"][0], ((0, 8 - SSM_CONV_K), (0, 0)))

    wts, dep = comm.weights(0, None)
    h1, ffn1_saved = _ffn_fwd(x, small["ffn1_norm"] + dep[0, 0], wts["ffn1_w_gate_up"], wts["ffn1_w_down"], "ffn1")
    got, dep = comm.weights(1, h1)
    wts.update(got)
    w_in_t = wts["w_in"]
    w_main_t = jnp.concatenate([w_in_t[offs[i]:offs[i + 1]] for i in (0, 1, 2, 3, 4, 6, 7)], axis=0)
    w_dt_t = jnp.pad(w_in_t[offs[5]:offs[6]], ((0, LANES - hs), (0, 0)))
    u = _rms_fwd(h1, small["mix_norm"] + dep[0, 0], "mix_norm")
    proj = _mm(u, w_main_t, BF16, "in_proj", nt=True)
    dtr = _mm(u, w_dt_t, F32, "in_proj_dt", nt=True)
    yap = _conv_a_fwd(proj, conv_a_w8, nb, s, d, "conv_a")
    y_a = _mm(yap, wts["w_out_a"], BF16, "out_a")
    xc = _conv_s_fwd(proj, xbc_col0, ssm_conv_w8, small["ssm_conv_b"], nb, s, cc, "conv_s")
    dtrt = dtr.T
    y_ssd, sprev = _ssd_fwd(xc, dtr, dtrt, prow, pcol, e_mat, nb, nc, di, "ssd")
    ygn = _gate_norm_fwd(y_ssd, proj, z_col0, small["ssm_norm"], di, "gate_norm")
    y_b = _mm(ygn, wts["w_out_ssm"], BF16, "out_ssm")
    merged = _merge_fwd(y_a, y_b, proj, ga_blk, gb_blk, d, "merge")
    h2 = _mm(merged, wts["w_mix_out"], F32, "mix_out", res=h1)
    got, _ = comm.weights(2, h2)
    wts.update(got)
    un = _rms_fwd(h2, small["xattn_norm"], "xattn_norm")
    mn = _rms_fwd(mem, small["mem_norm"], "mem_norm")
    q = _mm(un, wts["w_q"], BF16, "q_proj")
    kv = _mm(mn, wts["w_kv"], BF16, "kv_proj", nt=True)
    o = _xattn_fwd(q, kv, nb, s, m_len, d, "xattn")
    h3 = _mm(o, wts["w_o_x"], F32, "o_proj", res=h2)
    h4, ffn2_saved = _ffn_fwd(h3, small["ffn2_norm"], wts["ffn2_w_gate_up"], wts["ffn2_w_down"], "ffn2")
    loss_vec, dh4, dg_final = _loss_head(h4, small["final_norm"].reshape(1, d), target, "loss_head")

    grads = {"final_norm": dg_final.reshape(d)}
    big = {}
    dh3, grads["ffn2_norm"] = _ffn_bwd(
        dh4, ffn2_saved, small["ffn2_norm"], wts["ffn2_w_gate_up"], wts["ffn2_w_down"], "ffn2", None,
        lambda dw_gu_t, dw_d: comm.grads(0, {"ffn2_w_gate_up": dw_gu_t, "ffn2_w_down": dw_d}))
    big["w_o_x"] = _mm_tn(o, dh3, "dw_o")
    do = _mm(dh3, wts["w_o_x"], BF16, "d_o", nt=True)
    dq, dk, dv = _xattn_bwd(do, q, kv, nb, s, m_len, d, "xattn_bwd")
    big["w_q"] = _mm_tn(un, dq, "dw_q")
    big["w_kv"] = jnp.concatenate([_mm_tn(dk, mn, "dw_k"), _mm_tn(dv, mn, "dw_v")], axis=0)
    dun = _mm(dq, wts["w_q"], F32, "d_un", nt=True)
    dmn = _mm([dk, dv], wts["w_kv"], F32, "d_mn")
    _, grads["mem_norm"] = _rms_bwd(dmn, mem, small["mem_norm"], "mem_dnorm")
    dh2, grads["xattn_norm"] = _rms_bwd(dun, h2, small["xattn_norm"], "xattn_dnorm", dres=dh3)
    big["w_mix_out"] = _mm_tn(merged, dh2, "dw_mix")
    dmerged = _mm(dh2, wts["w_mix_out"], BF16, "d_merged", nt=True)
    dya, dyb, dga, dgb = _merge_bwd(dmerged, y_a, y_b, proj, ga_blk, gb_blk, d, "merge_bwd")
    big["w_out_a"] = _mm_tn(yap, dya, "dw_out_a")
    big["w_out_ssm"] = _mm_tn(ygn, dyb, "dw_out_ssm")
    dyap = _mm(dya, wts["w_out_a"], BF16, "d_yap", nt=True)
    dygn = _mm(dyb, wts["w_out_ssm"], BF16, "d_ygn", nt=True)
    dab, dac, dav, dconv_a = _conv_a_bwd(dyap, proj, conv_a_w8, nb, s, d, "conv_a_bwd")
    dy_ssd, dz, grads["ssm_norm"] = _gate_norm_bwd(dygn, y_ssd, proj, z_col0, small["ssm_norm"], di, "gate_norm_bwd")
    dxc, ddtr, ssd_sums = _ssd_bwd(dy_ssd, xc, dtr, dtrt, prow, pcol, e_mat, sprev, nb, nc, di, "ssd_bwd")
    dxbc, dconv_s, grads["ssm_conv_b"] = _conv_s_bwd(dxc, proj, xbc_col0, ssm_conv_w8, small["ssm_conv_b"], nb, s, cc, "conv_s_bwd")
    dpieces = [("ab", dab), ("ac", dac), ("av", dav), ("z", dz), ("xbc", dxbc), ("ga", dga), ("gb", dgb)]
    dw = {tag: _mm_tn(piece, u, f"dw_in_{tag}") for tag, piece in dpieces}
    dw_dt = _mm_tn(ddtr, u, "dw_in_dt")[:hs]
    du_main = _mm([piece for _, piece in dpieces], w_main_t, F32, "d_u", tk=1024)
    du = _mm(ddtr, w_dt_t, F32, "d_u_dt", res=du_main)
    dh1, grads["mix_norm"] = _rms_bwd(du, h1, small["mix_norm"], "mix_dnorm", dres=dh2)
    big["w_in"] = jnp.concatenate([dw["ab"], dw["ac"], dw["av"], dw["z"], dw["xbc"], dw_dt, dw["ga"], dw["gb"]], axis=0)
    dep = comm.grads(1, big)
    dx, grads["ffn1_norm"] = _ffn_bwd(
        dh1, ffn1_saved, small["ffn1_norm"], wts["ffn1_w_gate_up"], wts["ffn1_w_down"], "ffn1", dep,
        lambda dw_gu_t, dw_d: comm.grads(2, {"ffn1_w_gate_up": dw_gu_t, "ffn1_w_down": dw_d}))

    grads["conv_a_w"] = dconv_a[:CONV_A_K]
    grads["ssm_conv_w"] = dconv_s[:SSM_CONV_K]
    grads["ssm_dt_bias"] = ssd_sums[0:1, :hs]
    grads["ssm_a_log"] = ssd_sums[1:2, :hs]
    grads["ssm_d"] = ssd_sums[2:3, :hs]
    return loss_vec[0, 0], dx.reshape(nb, s, d), grads


def _step(inputs):
    w = {k: inputs[k] for k in WEIGHT_ORDER}
    mom = {k: inputs["m_" + k] for k in WEIGHT_ORDER}
    vel = {k: inputs["v_" + k] for k in WEIGHT_ORDER}
    me = 4 * lax.axis_index("x") + 2 * lax.axis_index("y") + lax.axis_index("c")

    send = {k: (w[k][0].T if k in COL_SHARDED else w[k][0]).astype(BF16) for k in BIG_WEIGHTS}

    def own_slot(land, mine):
        return lax.dynamic_update_slice(land, mine[None], (me, 0, 0))

    gathers = {0: _push_start([send[k] for k in W_GROUPS[0]], False, jnp.zeros((8, LANES), F32), "gather0_start")}
    exchanges = {}

    def weights(i, after):
        pending = gathers[i]
        sent, lands = _push_wait(pending, False, pending["token"] if after is None else after, f"gather{i}_wait")
        full = {k: own_slot(land, mine).reshape(N_DEV * mine.shape[0], mine.shape[1])
                for k, land, mine in zip(W_GROUPS[i], lands, sent)}
        dep = pending["token"]
        if i + 1 < len(W_GROUPS):
            gathers[i + 1] = _push_start([send[k] for k in W_GROUPS[i + 1]], False, lands[0], f"gather{i + 1}_start")
            dep = gathers[i + 1]["token"]
        return full, dep

    def send_grads(i, by_name):
        slots = [by_name[k].reshape((N_DEV,) + send[k].shape) for k in G_GROUPS[i]]
        exchanges[i] = _push_start(slots, True, slots[0], f"exchange{i}_start")
        return exchanges[i]["token"]

    comm = types.SimpleNamespace(weights=weights, grads=send_grads)
    small = {k: w[k] for k in SMALL_REPLICATED}
    conv_shapes = [w[k].shape[1:] for k in SMALL_SHARDED]
    packed_c, conv_offs = _pack_rows([w[k][0] for k in SMALL_SHARDED], F32, 8)
    conv_blocks = _unpack_rows(_all_gather([packed_c], "gather_conv_weights")[0], conv_offs, conv_shapes)
    for k, b in zip(SMALL_SHARDED, conv_blocks):
        small[k] = _full_from_slots(b, True)[None]

    loss_local, grad_x, grads = _local_step(inputs["x"], inputs["mem"], inputs["loss_target"], small, comm)
    loss = lax.psum(loss_local, AXES)

    out = {}
    for i, names in enumerate(G_GROUPS):
        sent, lands = _push_wait(exchanges[i], True, grad_x, f"exchange{i}_wait")
        for k, land, slots in zip(names, lands, sent):
            parts = own_slot(land, lax.dynamic_index_in_dim(slots, me, 0, keepdims=False))
            if k in COL_SHARDED:
                parts = parts.transpose(0, 2, 1)
            out[k] = tuple(o[None] for o in _sum8_adamw(parts, w[k][0], mom[k][0], vel[k][0], f"sum_adamw_{k}"))

    small_names = SMALL_REPLICATED + SMALL_SHARDED
    packed_g, small_offs = _pack_rows([grads[k] for k in small_names], F32, 8)
    total = _sum8(_all_gather([packed_g], "gather_small_grads")[0], "sum_small_grads")
    full_grads = _unpack_rows(total, small_offs, [grads[k].shape for k in small_names])
    mine = {}
    for k, g in zip(small_names, full_grads):
        if k in SMALL_SHARDED:
            c_loc = w[k].shape[2]
            g = lax.dynamic_slice_in_dim(g, me * c_loc, c_loc, axis=1)
        mine[k] = g.reshape(w[k].shape)
    sg, s_offs = _pack_rows([mine[k] for k in small_names], F32, 8)
    sw, _ = _pack_rows([w[k] for k in small_names], F32, 8)
    sm, _ = _pack_rows([mom[k] for k in small_names], F32, 8)
    sv, _ = _pack_rows([vel[k] for k in small_names], F32, 8)
    s_shapes = [w[k].shape for k in small_names]
    small_out = [_unpack_rows(a, s_offs, s_shapes) for a in _adamw(sg, sw, sm, sv, "adamw_small")]
    for i, k in enumerate(small_names):
        out[k] = (mine[k],) + tuple(o[i] for o in small_out)

    res = [loss, grad_x]
    for j in range(4):
        res += [out[k][j] for k in WEIGHT_ORDER]
    return tuple(res)


def kernel(x, mem, ffn1_norm, ffn1_w_gate_up, ffn1_w_down, mix_norm, w_in, conv_a_w, w_out_a, ssm_conv_w, ssm_conv_b, ssm_dt_bias, ssm_a_log, ssm_d, ssm_norm, w_out_ssm, w_mix_out, xattn_norm, mem_norm, w_q, w_kv, w_o_x, ffn2_norm, ffn2_w_gate_up, ffn2_w_down, final_norm, loss_target, m_ffn1_norm, m_ffn1_w_gate_up, m_ffn1_w_down, m_mix_norm, m_w_in, m_conv_a_w, m_w_out_a, m_ssm_conv_w, m_ssm_conv_b, m_ssm_dt_bias, m_ssm_a_log, m_ssm_d, m_ssm_norm, m_w_out_ssm, m_w_mix_out, m_xattn_norm, m_mem_norm, m_w_q, m_w_kv, m_w_o_x, m_ffn2_norm, m_ffn2_w_gate_up, m_ffn2_w_down, m_final_norm, v_ffn1_norm, v_ffn1_w_gate_up, v_ffn1_w_down, v_mix_norm, v_w_in, v_conv_a_w, v_w_out_a, v_ssm_conv_w, v_ssm_conv_b, v_ssm_dt_bias, v_ssm_a_log, v_ssm_d, v_ssm_norm, v_w_out_ssm, v_w_mix_out, v_xattn_norm, v_mem_norm, v_w_q, v_w_kv, v_w_o_x, v_ffn2_norm, v_ffn2_w_gate_up, v_ffn2_w_down, v_final_norm):
    return _step(dict(locals()))
```

```python
import functools
import math
import types

import jax
import jax.numpy as jnp
from jax import lax
from jax.experimental import pallas as pl
from jax.experimental.pallas import tpu as pltpu

F32, BF16 = jnp.float32, jnp.bfloat16
HI = lax.Precision.HIGHEST
MESH = pl.DeviceIdType.MESH
AXES = ("x", "y", "c")
N_DEV = 8

EPS = 1e-6
FFN_RES_WEIGHT = 0.5
SSM_HEAD_DIM = 64
SSM_GROUPS = 4
SSM_STATE = 128
SSM_CHUNK = 128
CONV_A_K = 3
SSM_CONV_K = 4
XATTN_HEADS = 4
ADAM_LR, ADAM_B1, ADAM_B2, ADAM_EPS, ADAM_WD, ADAM_STEP = 1e-3, 0.9, 0.999, 1e-8, 0.01, 10

LANES = 128
BF16_SUBLANES = 16
VMEM_LIMIT_BYTES = 56 * 2 ** 20
NEG_BIG = -1e30

BIG_WEIGHTS = ("ffn1_w_gate_up", "ffn1_w_down", "w_in", "w_out_a", "w_out_ssm", "w_mix_out",
               "w_q", "w_kv", "w_o_x", "ffn2_w_gate_up", "ffn2_w_down")
COL_SHARDED = ("ffn1_w_gate_up", "w_in", "w_kv", "ffn2_w_gate_up")
SMALL_REPLICATED = ("ffn1_norm", "mix_norm", "ssm_conv_b", "ssm_dt_bias", "ssm_a_log", "ssm_d", "ssm_norm",
                    "xattn_norm", "mem_norm", "ffn2_norm", "final_norm")
SMALL_SHARDED = ("conv_a_w", "ssm_conv_w")
WEIGHT_ORDER = ("ffn1_norm", "ffn1_w_gate_up", "ffn1_w_down", "mix_norm", "w_in", "conv_a_w", "w_out_a",
                "ssm_conv_w", "ssm_conv_b", "ssm_dt_bias", "ssm_a_log", "ssm_d", "ssm_norm", "w_out_ssm",
                "w_mix_out", "xattn_norm", "mem_norm", "w_q", "w_kv", "w_o_x", "ffn2_norm", "ffn2_w_gate_up",
                "ffn2_w_down", "final_norm")


def _tile(dim, pref, unit):
    best = None
    t = unit
    while t <= min(dim, pref):
        if dim % t == 0:
            best = t
        t += unit
    return best if best is not None else dim


def _params(*sem):
    return pltpu.CompilerParams(dimension_semantics=sem, vmem_limit_bytes=VMEM_LIMIT_BYTES)


def _silu(x):
    return x * jax.nn.sigmoid(x)


def _dsilu(x):
    s = jax.nn.sigmoid(x)
    return s * (1.0 + x * (1.0 - s))


def _softplus(x):
    return jnp.maximum(x, 0.0) + jnp.log(1.0 + jnp.exp(-jnp.abs(x)))


def _dot(a, b, dims=(((1,), (0,)), ((), ())), precision=None):
    return lax.dot_general(a, b, dims, preferred_element_type=F32, precision=precision)


def _stack_rows(rows, width):
    r_idx = lax.broadcasted_iota(jnp.int32, (8, width), 0)
    acc = jnp.zeros((8, width), F32)
    for k, row in enumerate(rows):
        acc = jnp.where(r_idx == k, row, acc)
    return acc


NT = (((1,), (1,)), ((), ()))
TN = (((0,), (0,)), ((), ()))


def _mm(a, b, out_dtype, name, res=None, alpha=1.0, nt=False, dep=None, tm=1024, tn=512, tk=2816):
    pieces = list(a) if isinstance(a, (list, tuple)) else [a]
    m = pieces[0].shape[0]
    k = sum(p.shape[1] for p in pieces)
    n = b.shape[0] if nt else b.shape[1]
    assert (b.shape[1] if nt else b.shape[0]) == k
    tm, tn = _tile(m, tm, 8), _tile(n, tn, LANES)
    tk = _tile(math.gcd(*[p.shape[1] for p in pieces]), tk, LANES)
    nk = k // tk
    starts, s0 = [], 0
    for p in pieces:
        starts.append((s0, p.shape[1] // tk))
        s0 += p.shape[1] // tk
    n_p = len(pieces)

    def body(*refs):
        a_refs, b_ref = refs[:n_p], refs[n_p]
        r_ref = refs[n_p + 1] if res is not None else None
        n_in = n_p + 1 + (res is not None) + (dep is not None)
        o_ref = refs[n_in]
        scr = refs[n_in + 1:]

        def finish(acc):
            acc = alpha * acc if alpha != 1.0 else acc
            if r_ref is not None:
                acc = r_ref[...] + acc
            o_ref[...] = acc.astype(out_dtype)

        def product(a_ref):
            return _dot(a_ref[...].astype(BF16), b_ref[...].astype(BF16), NT if nt else (((1,), (0,)), ((), ())))

        if nk == 1:
            finish(product(a_refs[0]))
            return
        acc_ref = scr[0]
        kk = pl.program_id(2)
        for (s, cnt), a_ref in zip(starts, a_refs):
            if s == 0:
                @pl.when(kk == 0)
                def _():
                    acc_ref[...] = product(a_ref)

                @pl.when(jnp.logical_and(kk > 0, kk < cnt))
                def _():
                    acc_ref[...] += product(a_ref)
            else:
                @pl.when(jnp.logical_and(kk >= s, kk < s + cnt))
                def _():
                    acc_ref[...] += product(a_ref)

        @pl.when(kk == nk - 1)
        def _():
            finish(acc_ref[...])

    def a_spec(s, cnt):
        return pl.BlockSpec((tm, tk), lambda i, j, kk: (i, jnp.clip(kk - s, 0, cnt - 1)))

    in_specs = [a_spec(s, cnt) for s, cnt in starts]
    in_specs.append(pl.BlockSpec((tn, tk), lambda i, j, kk: (j, kk)) if nt else pl.BlockSpec((tk, tn), lambda i, j, kk: (kk, j)))
    args = pieces + [b]
    if res is not None:
        in_specs.append(pl.BlockSpec((tm, tn), lambda i, j, kk: (i, j)))
        args.append(res)
    if dep is not None:
        in_specs.append(pl.BlockSpec((8, LANES), lambda i, j, kk: (0, 0)))
        args.append(dep)
    return pl.pallas_call(
        body, name=name, grid=(m // tm, n // tn, nk), in_specs=in_specs,
        out_specs=pl.BlockSpec((tm, tn), lambda i, j, kk: (i, j)),
        out_shape=jax.ShapeDtypeStruct((m, n), out_dtype),
        scratch_shapes=[pltpu.VMEM((tm, tn), F32)] if nk > 1 else [],
        compiler_params=_params("parallel", "parallel", "arbitrary"),
    )(*args)


def _mm_tn(x, dy, name, out_dtype=BF16, alpha=1.0, dep=None, tko=1408, tn=1024, tt=512):
    t, k = x.shape
    n = dy.shape[1]
    tko, tn, tt = _tile(k, tko, LANES), _tile(n, tn, LANES), _tile(t, tt, 8)
    nt_steps = t // tt

    def body(*refs):
        x_ref, dy_ref = refs[:2]
        o_ref, acc_ref = refs[-2:]
        part = _dot(x_ref[...].astype(BF16), dy_ref[...].astype(BF16), TN)
        step = pl.program_id(2)

        @pl.when(step == 0)
        def _():
            acc_ref[...] = part

        @pl.when(step > 0)
        def _():
            acc_ref[...] += part

        @pl.when(step == nt_steps - 1)
        def _():
            acc = acc_ref[...]
            o_ref[...] = (alpha * acc if alpha != 1.0 else acc).astype(out_dtype)

    in_specs = [pl.BlockSpec((tt, tko), lambda i, j, s: (s, i)), pl.BlockSpec((tt, tn), lambda i, j, s: (s, j))]
    args = [x, dy]
    if dep is not None:
        in_specs.append(pl.BlockSpec((8, LANES), lambda i, j, s: (0, 0)))
        args.append(dep)
    return pl.pallas_call(
        body, name=name, grid=(k // tko, n // tn, nt_steps), in_specs=in_specs,
        out_specs=pl.BlockSpec((tko, tn), lambda i, j, s: (i, j)),
        out_shape=jax.ShapeDtypeStruct((k, n), out_dtype),
        scratch_shapes=[pltpu.VMEM((tko, tn), F32)],
        compiler_params=_params("parallel", "parallel", "arbitrary"),
    )(*args)


def _rms_fwd(x, g, name, tt=512):
    t, d = x.shape
    tt = _tile(t, tt, 8)

    def body(x_ref, g_ref, o_ref):
        xv = x_ref[...]
        r = lax.rsqrt(jnp.mean(xv * xv, axis=-1, keepdims=True) + EPS)
        o_ref[...] = (xv * r * g_ref[...]).astype(BF16)

    return pl.pallas_call(
        body, name=name, grid=(t // tt,),
        in_specs=[pl.BlockSpec((tt, d), lambda i: (i, 0)), pl.BlockSpec((1, d), lambda i: (0, 0))],
        out_specs=pl.BlockSpec((tt, d), lambda i: (i, 0)),
        out_shape=jax.ShapeDtypeStruct((t, d), BF16), compiler_params=_params("parallel"),
    )(x, g)


def _rms_bwd(dn, x, g, name, dres=None, tt=512):
    t, d = x.shape
    tt = _tile(t, tt, 8)

    def body(*refs):
        if dres is None:
            dn_ref, x_ref, g_ref, dx_ref, dg_ref = refs
            r_ref = None
        else:
            dn_ref, x_ref, g_ref, r_ref, dx_ref, dg_ref = refs
        xv, dnv = x_ref[...], dn_ref[...].astype(F32)
        r = lax.rsqrt(jnp.mean(xv * xv, axis=-1, keepdims=True) + EPS)
        xh = xv * r
        gy = dnv * g_ref[...]
        dx = r * (gy - xh * jnp.mean(gy * xh, axis=-1, keepdims=True))
        if r_ref is not None:
            dx = dx + r_ref[...]
        dx_ref[...] = dx
        part = jnp.sum(dnv * xh, axis=0, keepdims=True)

        @pl.when(pl.program_id(0) == 0)
        def _():
            dg_ref[...] = part

        @pl.when(pl.program_id(0) > 0)
        def _():
            dg_ref[...] += part

    row = pl.BlockSpec((tt, d), lambda i: (i, 0))
    vec = pl.BlockSpec((1, d), lambda i: (0, 0))
    in_specs, args = [row, row, vec], [dn, x, g]
    if dres is not None:
        in_specs.append(row)
        args.append(dres)
    return pl.pallas_call(
        body, name=name, grid=(t // tt,), in_specs=in_specs, out_specs=[row, vec],
        out_shape=[jax.ShapeDtypeStruct((t, d), F32), jax.ShapeDtypeStruct((1, d), F32)],
        compiler_params=_params("arbitrary"),
    )(*args)


def _swiglu_fwd(gu, name, tt=512):
    t, f2 = gu.shape
    f = f2 // 2
    tt = _tile(t, tt, 8)
    tf = _tile(f, 1408, LANES)
    nf = f // tf

    def body(g_ref, u_ref, o_ref):
        o_ref[...] = (_silu(g_ref[...].astype(F32)) * u_ref[...].astype(F32)).astype(BF16)

    return pl.pallas_call(
        body, name=name, grid=(t // tt, nf),
        in_specs=[pl.BlockSpec((tt, tf), lambda i, j: (i, j)), pl.BlockSpec((tt, tf), lambda i, j: (i, j + nf))],
        out_specs=pl.BlockSpec((tt, tf), lambda i, j: (i, j)),
        out_shape=jax.ShapeDtypeStruct((t, f), BF16), compiler_params=_params("parallel", "parallel"),
    )(gu, gu)


def _swiglu_bwd(da, gu, name, tt=512):
    t, f2 = gu.shape
    f = f2 // 2
    tt = _tile(t, tt, 8)
    tf = _tile(f, 1408, LANES)
    nf = f // tf

    def body(da_ref, g_ref, u_ref, o_ref):
        dav, gv = da_ref[...].astype(F32), g_ref[...].astype(F32)

        @pl.when(pl.program_id(1) < nf)
        def _():
            o_ref[...] = (dav * u_ref[...].astype(F32) * _dsilu(gv)).astype(BF16)

        @pl.when(pl.program_id(1) >= nf)
        def _():
            o_ref[...] = (dav * _silu(gv)).astype(BF16)

    return pl.pallas_call(
        body, name=name, grid=(t // tt, 2 * nf),
        in_specs=[pl.BlockSpec((tt, tf), lambda i, j: (i, j % nf)), pl.BlockSpec((tt, tf), lambda i, j: (i, j % nf)),
                  pl.BlockSpec((tt, tf), lambda i, j: (i, j % nf + nf))],
        out_specs=pl.BlockSpec((tt, tf), lambda i, j: (i, j)),
        out_shape=jax.ShapeDtypeStruct((t, f2), BF16), compiler_params=_params("parallel", "parallel"),
    )(da, gu, gu)


def _merge_fwd(ya, yb, proj, ga_blk, gb_blk, d, name, tt=512):
    t = ya.shape[0]
    tt = _tile(t, tt, 8)

    def body(ya_ref, yb_ref, ga_ref, gb_ref, o_ref):
        o_ref[...] = (jax.nn.sigmoid(ga_ref[...].astype(F32)) * ya_ref[...].astype(F32)
                      + jax.nn.sigmoid(gb_ref[...].astype(F32)) * yb_ref[...].astype(F32)).astype(BF16)

    row = pl.BlockSpec((tt, d), lambda i: (i, 0))
    return pl.pallas_call(
        body, name=name, grid=(t // tt,),
        in_specs=[row, row, pl.BlockSpec((tt, d), lambda i: (i, ga_blk)), pl.BlockSpec((tt, d), lambda i: (i, gb_blk))],
        out_specs=row, out_shape=jax.ShapeDtypeStruct((t, d), BF16), compiler_params=_params("parallel"),
    )(ya, yb, proj, proj)


def _merge_bwd(dm, ya, yb, proj, ga_blk, gb_blk, d, name, tt=512):
    t = ya.shape[0]
    tt = _tile(t, tt, 8)

    def body(dm_ref, ya_ref, yb_ref, ga_ref, gb_ref, dya_ref, dyb_ref, dga_ref, dgb_ref):
        dmv = dm_ref[...].astype(F32)
        sa, sb = jax.nn.sigmoid(ga_ref[...].astype(F32)), jax.nn.sigmoid(gb_ref[...].astype(F32))
        dya_ref[...] = (dmv * sa).astype(BF16)
        dyb_ref[...] = (dmv * sb).astype(BF16)
        dga_ref[...] = (dmv * ya_ref[...].astype(F32) * sa * (1.0 - sa)).astype(BF16)
        dgb_ref[...] = (dmv * yb_ref[...].astype(F32) * sb * (1.0 - sb)).astype(BF16)

    row = pl.BlockSpec((tt, d), lambda i: (i, 0))
    out = jax.ShapeDtypeStruct((t, d), BF16)
    return pl.pallas_call(
        body, name=name, grid=(t // tt,),
        in_specs=[row, row, row, pl.BlockSpec((tt, d), lambda i: (i, ga_blk)), pl.BlockSpec((tt, d), lambda i: (i, gb_blk))],
        out_specs=[row] * 4, out_shape=[out] * 4, compiler_params=_params("parallel"),
    )(dm, ya, yb, proj, proj)


def _loss_head(h, g, target, name, tt=512):
    t, d = h.shape
    tt = _tile(t, tt, 8)

    def body(h_ref, g_ref, tg_ref, loss_ref, dh_ref, dg_ref):
        xv = h_ref[...]
        r = lax.rsqrt(jnp.mean(xv * xv, axis=-1, keepdims=True) + EPS)
        xh = xv * r
        err = xh * g_ref[...] - tg_ref[...]
        dout = err * (1.0 / d)
        gy = dout * g_ref[...]
        dh_ref[...] = r * (gy - xh * jnp.mean(gy * xh, axis=-1, keepdims=True))
        dg_part = jnp.sum(dout * xh, axis=0, keepdims=True)
        loss_part = jnp.full((1, LANES), 0.5 / d, F32) * jnp.sum(err * err)

        @pl.when(pl.program_id(0) == 0)
        def _():
            dg_ref[...] = dg_part
            loss_ref[...] = loss_part

        @pl.when(pl.program_id(0) > 0)
        def _():
            dg_ref[...] += dg_part
            loss_ref[...] += loss_part

    row = pl.BlockSpec((tt, d), lambda i: (i, 0))
    vec = pl.BlockSpec((1, d), lambda i: (0, 0))
    return pl.pallas_call(
        body, name=name, grid=(t // tt,), in_specs=[row, vec, row],
        out_specs=[pl.BlockSpec((1, LANES), lambda i: (0, 0)), row, vec],
        out_shape=[jax.ShapeDtypeStruct((1, LANES), F32), jax.ShapeDtypeStruct((t, d), F32), jax.ShapeDtypeStruct((1, d), F32)],
        compiler_params=_params("arbitrary"),
    )(h, g, target)


def _shift_down(x, k, t_idx):
    if k == 0:
        return x
    return jnp.where(t_idx >= k, pltpu.roll(x, k, 0), 0.0)


def _shift_up(x, k, t_idx, s):
    if k == 0:
        return x
    return jnp.where(t_idx < s - k, pltpu.roll(x, s - k, 0), 0.0)


def _conv_a_fwd(proj, w, nb, s, d, name, cb=256):
    cb = _tile(d, cb, LANES)
    nd = d // cb

    def body(b_ref, c_ref, v_ref, w_ref, o_ref):
        t_idx = lax.broadcasted_iota(jnp.int32, (s, cb), 0)
        cv = c_ref[...].astype(F32) * v_ref[...].astype(F32)
        cc = sum(w_ref[k:k + 1, :] * _shift_down(cv, CONV_A_K - 1 - k, t_idx) for k in range(CONV_A_K))
        o_ref[...] = (b_ref[...].astype(F32) * cc).astype(BF16)

    def col(off):
        return pl.BlockSpec((s, cb), lambda b, j: (b, j + off * nd))

    return pl.pallas_call(
        body, name=name, grid=(nb, nd), in_specs=[col(0), col(1), col(2), pl.BlockSpec((8, cb), lambda b, j: (0, j))],
        out_specs=pl.BlockSpec((s, cb), lambda b, j: (b, j)),
        out_shape=jax.ShapeDtypeStruct((nb * s, d), BF16), compiler_params=_params("parallel", "parallel"),
    )(proj, proj, proj, w)


def _conv_a_bwd(dy, proj, w, nb, s, d, name, cb=256):
    cb = _tile(d, cb, LANES)
    nd = d // cb

    def body(dy_ref, b_ref, c_ref, v_ref, w_ref, db_ref, dc_ref, dv_ref, dw_ref):
        t_idx = lax.broadcasted_iota(jnp.int32, (s, cb), 0)
        cv_c, cv_v = c_ref[...].astype(F32), v_ref[...].astype(F32)
        cv = cv_c * cv_v
        shifted = [_shift_down(cv, CONV_A_K - 1 - k, t_idx) for k in range(CONV_A_K)]
        cc = sum(w_ref[k:k + 1, :] * shifted[k] for k in range(CONV_A_K))
        dyv = dy_ref[...].astype(F32)
        db_ref[...] = (dyv * cc).astype(BF16)
        dcc = dyv * b_ref[...].astype(F32)
        dcv = sum(w_ref[k:k + 1, :] * _shift_up(dcc, CONV_A_K - 1 - k, t_idx, s) for k in range(CONV_A_K))
        dc_ref[...] = (dcv * cv_v).astype(BF16)
        dv_ref[...] = (dcv * cv_c).astype(BF16)
        rows = [jnp.sum(dcc * shifted[k], axis=0, keepdims=True) for k in range(CONV_A_K)]
        part = _stack_rows(rows, cb)

        @pl.when(pl.program_id(1) == 0)
        def _():
            dw_ref[...] = part

        @pl.when(pl.program_id(1) > 0)
        def _():
            dw_ref[...] += part

    def col(off):
        return pl.BlockSpec((s, cb), lambda j, b: (b, j + off * nd))

    own = pl.BlockSpec((s, cb), lambda j, b: (b, j))
    wspec = pl.BlockSpec((8, cb), lambda j, b: (0, j))
    out = jax.ShapeDtypeStruct((nb * s, d), BF16)
    return pl.pallas_call(
        body, name=name, grid=(nd, nb), in_specs=[own, col(0), col(1), col(2), wspec],
        out_specs=[own, own, own, wspec], out_shape=[out, out, out, jax.ShapeDtypeStruct((8, d), F32)],
        compiler_params=_params("parallel", "arbitrary"),
    )(dy, proj, proj, proj, w)


def _conv_s_fwd(proj, col0, w, bias, nb, s, cc_width, name, cb=256):
    cb = _tile(math.gcd(cc_width, col0) if col0 else cc_width, cb, LANES)
    nd, off = cc_width // cb, col0 // cb

    def body(x_ref, w_ref, b_ref, o_ref):
        t_idx = lax.broadcasted_iota(jnp.int32, (s, cb), 0)
        xv = x_ref[...].astype(F32)
        pre = b_ref[...] + sum(w_ref[k:k + 1, :] * _shift_down(xv, SSM_CONV_K - 1 - k, t_idx) for k in range(SSM_CONV_K))
        o_ref[...] = _silu(pre).astype(BF16)

    vec = pl.BlockSpec((8, cb), lambda b, j: (0, j))
    return pl.pallas_call(
        body, name=name, grid=(nb, nd),
        in_specs=[pl.BlockSpec((s, cb), lambda b, j: (b, j + off)), vec, pl.BlockSpec((1, cb), lambda b, j: (0, j))],
        out_specs=pl.BlockSpec((s, cb), lambda b, j: (b, j)),
        out_shape=jax.ShapeDtypeStruct((nb * s, cc_width), BF16), compiler_params=_params("parallel", "parallel"),
    )(proj, w, bias)


def _conv_s_bwd(dxc, proj, col0, w, bias, nb, s, cc_width, name, cb=256):
    cb = _tile(math.gcd(cc_width, col0) if col0 else cc_width, cb, LANES)
    nd, off = cc_width // cb, col0 // cb

    def body(d_ref, x_ref, w_ref, b_ref, dx_ref, dw_ref, db_ref):
        t_idx = lax.broadcasted_iota(jnp.int32, (s, cb), 0)
        xv = x_ref[...].astype(F32)
        shifted = [_shift_down(xv, SSM_CONV_K - 1 - k, t_idx) for k in range(SSM_CONV_K)]
        pre = b_ref[...] + sum(w_ref[k:k + 1, :] * shifted[k] for k in range(SSM_CONV_K))
        dpre = d_ref[...].astype(F32) * _dsilu(pre)
        dx = sum(w_ref[k:k + 1, :] * _shift_up(dpre, SSM_CONV_K - 1 - k, t_idx, s) for k in range(SSM_CONV_K))
        dx_ref[...] = dx.astype(BF16)
        rows = [jnp.sum(dpre * shifted[k], axis=0, keepdims=True) for k in range(SSM_CONV_K)]
        dw_part = _stack_rows(rows, cb)
        db_part = jnp.sum(dpre, axis=0, keepdims=True)

        @pl.when(pl.program_id(1) == 0)
        def _():
            dw_ref[...] = dw_part
            db_ref[...] = db_part

        @pl.when(pl.program_id(1) > 0)
        def _():
            dw_ref[...] += dw_part
            db_ref[...] += db_part

    own = pl.BlockSpec((s, cb), lambda j, b: (b, j))
    wspec = pl.BlockSpec((8, cb), lambda j, b: (0, j))
    bspec = pl.BlockSpec((1, cb), lambda j, b: (0, j))
    return pl.pallas_call(
        body, name=name, grid=(nd, nb),
        in_specs=[own, pl.BlockSpec((s, cb), lambda j, b: (b, j + off)), wspec, bspec],
        out_specs=[own, wspec, bspec],
        out_shape=[jax.ShapeDtypeStruct((nb * s, cc_width), BF16), jax.ShapeDtypeStruct((8, cc_width), F32),
                   jax.ShapeDtypeStruct((1, cc_width), F32)],
        compiler_params=_params("parallel", "arbitrary"),
    )(dxc, proj, w, bias)


def _split3(v):
    hi = v.astype(BF16)
    r1 = v - hi.astype(F32)
    mid = r1.astype(BF16)
    return hi, mid, (r1 - mid.astype(F32)).astype(BF16)


def _exact_left(mask_b, v):
    return sum(_dot(mask_b, t) for t in _split3(v))


def _exact_right(v, mask_b):
    return sum(_dot(t, mask_b) for t in _split3(v))


def _head_sums(v, e_b):
    hi = v.astype(BF16)
    lo = (v - hi.astype(F32)).astype(BF16)
    return _dot(hi, e_b, NT) + _dot(lo, e_b, NT)


def _spread(v, out_ref, di):
    lane = lax.broadcasted_iota(jnp.int32, (v.shape[0], LANES), 1)
    for pr in range(di // LANES):
        h0 = pr * (LANES // SSM_HEAD_DIM)
        out_ref[:, pr * LANES:(pr + 1) * LANES] = jnp.where(lane < SSM_HEAD_DIM, v[:, h0:h0 + 1], v[:, h0 + 1:h0 + 2])


def _ssd_common(xc_ref, dtr_ref, dtrt_ref, prow_ref, pcol_ref, dtx_ref, acsx_ref, dx_ref, di):
    l = SSM_CHUNK
    bias_r, a_r = prow_ref[0:1, :], -jnp.exp(prow_ref[1:2, :])
    sp_in = dtr_ref[...] + bias_r
    dt = _softplus(sp_in)
    li = lax.broadcasted_iota(jnp.int32, (l, l), 0)
    si = lax.broadcasted_iota(jnp.int32, (l, l), 1)
    lower_b = (li >= si).astype(BF16)
    upper_b = (li <= si).astype(BF16)
    acs = _exact_left(lower_b, dt * a_r)
    bias_c, a_c = pcol_ref[:, 0:1], -jnp.exp(pcol_ref[:, 1:2])
    dt_t = _softplus(dtrt_ref[...] + bias_c)
    acs_t = _exact_right(dt_t * a_c, upper_b)
    _spread(dt, dtx_ref, di)
    _spread(acs, acsx_ref, di)
    _spread(prow_ref[0:8, :], dx_ref, di)
    acs_exp = acsx_ref[...]
    acs_last = acs_exp[l - 1:l, :]
    x = xc_ref[:, 0:di].astype(F32)
    return dict(dt=dt, a_r=a_r, sp_in=sp_in, acs=acs, acs_t=acs_t, dt_exp=dtx_ref[...], e_exp=jnp.exp(acs_exp),
                el_exp=jnp.exp(acs_last), f_exp=jnp.exp(acs_last - acs_exp), x=x, mask=li >= si, upper_b=upper_b,
                d_exp=dx_ref[2:3, :])


def _decay(q, h):
    seg = q["acs"][:, h:h + 1] - q["acs_t"][h:h + 1, :]
    return jnp.exp(jnp.where(q["mask"], seg, NEG_BIG))


def _ssd_fwd(xc, dtr, dtrt, prow, pcol, nb, nc, di, name):
    l, n, g_n, p = SSM_CHUNK, SSM_STATE, SSM_GROUPS, SSM_HEAD_DIM
    cc = xc.shape[1]
    gw = di // g_n
    assert p * 2 == LANES and gw % LANES == 0

    def body(xc_ref, dtr_ref, dtrt_ref, prow_ref, pcol_ref, y_ref, sprev_ref, st_ref, dtx_ref, acsx_ref, dx_ref):
        @pl.when(pl.program_id(1) == 0)
        def _():
            st_ref[...] = jnp.zeros_like(st_ref)

        q = _ssd_common(xc_ref, dtr_ref, dtrt_ref, prow_ref, pcol_ref, dtx_ref, acsx_ref, dx_ref, di)
        x = q["x"]
        xd = x * q["dt_exp"]
        xdb = xd.astype(BF16)
        xdf = (xd * q["f_exp"]).astype(BF16)
        lane = lax.broadcasted_iota(jnp.int32, (l, LANES), 1)
        for g in range(g_n):
            lo = g * gw
            bg = xc_ref[:, di + g * n: di + (g + 1) * n]
            cg = xc_ref[:, di + g_n * n + g * n: di + g_n * n + (g + 1) * n]
            cb = _dot(cg, bg, NT)
            st_g = st_ref[:, lo:lo + gw]
            y_off = q["e_exp"][:, lo:lo + gw] * _dot(cg, st_g.astype(BF16))
            for pr in range(gw // LANES):
                c0 = lo + pr * LANES
                h0 = c0 // p
                xp = xdb[:, c0:c0 + LANES]
                m0 = (cb * _decay(q, h0)).astype(BF16)
                m1 = (cb * _decay(q, h0 + 1)).astype(BF16)
                yd = _dot(m0, jnp.where(lane < p, xp, 0)) + _dot(m1, jnp.where(lane >= p, xp, 0))
                y_ref[:, c0:c0 + LANES] = (yd + y_off[:, pr * LANES:(pr + 1) * LANES]
                                           + q["d_exp"][:, c0:c0 + LANES] * x[:, c0:c0 + LANES])
            sprev_ref[:, lo:lo + gw] = st_g
            st_ref[:, lo:lo + gw] = q["el_exp"][:, lo:lo + gw] * st_g + _dot(bg, xdf[:, lo:lo + gw], TN)

    tok = lambda w: pl.BlockSpec((l, w), lambda b, c: (b * nc + c, 0))
    const = lambda r, w: pl.BlockSpec((r, w), lambda b, c: (0, 0))
    return pl.pallas_call(
        body, name=name, grid=(nb, nc),
        in_specs=[tok(cc), tok(LANES), pl.BlockSpec((LANES, l), lambda b, c: (0, b * nc + c)),
                  const(8, LANES), const(LANES, 8)],
        out_specs=[tok(di), pl.BlockSpec((None, n, di), lambda b, c: (b * nc + c, 0, 0))],
        out_shape=[jax.ShapeDtypeStruct((nb * nc * l, di), F32), jax.ShapeDtypeStruct((nb * nc, n, di), F32)],
        scratch_shapes=[pltpu.VMEM((n, di), F32), pltpu.VMEM((l, di), F32), pltpu.VMEM((l, di), F32),
                        pltpu.VMEM((8, di), F32)],
        compiler_params=_params("parallel", "arbitrary"),
    )(xc, dtr, dtrt, prow, pcol)


def _ssd_bwd(dy, xc, dtr, dtrt, prow, pcol, e_mat, sprev, nb, nc, di, name):
    l, n, g_n, p = SSM_CHUNK, SSM_STATE, SSM_GROUPS, SSM_HEAD_DIM
    cc = xc.shape[1]
    gw = di // g_n

    def body(dy_ref, xc_ref, dtr_ref, dtrt_ref, prow_ref, pcol_ref, e_ref, sprev_ref,
             dxc_ref, ddtr_ref, sums_ref, dst_ref, off_ref, dxd_ref, last_ref, dla_diag_ref, vst_ref,
             dtx_ref, acsx_ref, dx_ref):
        first = jnp.logical_and(pl.program_id(0) == 0, pl.program_id(1) == 0)

        @pl.when(pl.program_id(1) == 0)
        def _():
            dst_ref[...] = jnp.zeros_like(dst_ref)

        dla_diag_ref[...] = jnp.zeros_like(dla_diag_ref)
        strict_lower = lax.broadcasted_iota(jnp.int32, (l, l), 0) > lax.broadcasted_iota(jnp.int32, (l, l), 1)

        q = _ssd_common(xc_ref, dtr_ref, dtrt_ref, prow_ref, pcol_ref, dtx_ref, acsx_ref, dx_ref, di)
        x = q["x"]
        xd = x * q["dt_exp"]
        xdb = xd.astype(BF16)
        xdf = (xd * q["f_exp"]).astype(BF16)
        dyv = dy_ref[...]
        dyb = dyv.astype(BF16)
        dye = (dyv * q["e_exp"]).astype(BF16)
        upper_b = q["upper_b"]
        lane = lax.broadcasted_iota(jnp.int32, (l, LANES), 1)
        for g in range(g_n):
            lo = g * gw
            bg = xc_ref[:, di + g * n: di + (g + 1) * n]
            cg = xc_ref[:, di + g_n * n + g * n: di + g_n * n + (g + 1) * n]
            cb = _dot(cg, bg, NT)
            st_g = sprev_ref[:, lo:lo + gw]
            st_gb = st_g.astype(BF16)
            dst_g = dst_ref[:, lo:lo + gw]
            dst_gb = dst_g.astype(BF16)
            dye_g = dye[:, lo:lo + gw]
            xdf_g = xdf[:, lo:lo + gw]
            y_off = q["e_exp"][:, lo:lo + gw] * _dot(cg, st_gb)
            dc_g = _dot(dye_g, st_gb, NT)
            db_g = _dot(xdf_g, dst_gb, NT)
            dxd_state = _dot(bg, dst_gb) * q["f_exp"][:, lo:lo + gw]
            last_ref[:, lo:lo + gw] = jnp.sum(dst_g * st_g, axis=0, keepdims=True)
            dst_ref[:, lo:lo + gw] = q["el_exp"][:, lo:lo + gw] * dst_g + _dot(cg, dye_g, TN)
            off_ref[:, lo:lo + gw] = dyv[:, lo:lo + gw] * y_off
            vst_ref[:, lo:lo + gw] = xd[:, lo:lo + gw] * dxd_state
            dcb = jnp.zeros((l, l), F32)
            for pr in range(gw // LANES):
                c0 = lo + pr * LANES
                h0 = c0 // p
                xp = xdb[:, c0:c0 + LANES]
                dyp = dyb[:, c0:c0 + LANES]
                dxd_diag = jnp.zeros((l, LANES), F32)
                for k, keep in enumerate((lane < p, lane >= p)):
                    dec = _decay(q, h0 + k)
                    dy_h = jnp.where(keep, dyp, 0)
                    dm_dec = _dot(dy_h, xp, NT) * dec
                    dcb = dcb + dm_dec
                    dxd_diag = dxd_diag + _dot((cb * dec).astype(BF16), dy_h, TN)
                    above = _dot(upper_b, (dm_dec * cb).astype(BF16))
                    dla_col = jnp.sum(jnp.where(strict_lower, above, 0.0), axis=1, keepdims=True)
                    dla_diag_ref[...] = jnp.where(lane == h0 + k, dla_col, dla_diag_ref[...])
                dxd_ref[:, c0:c0 + LANES] = dxd_diag + dxd_state[:, pr * LANES:(pr + 1) * LANES]
            dcb_b = dcb.astype(BF16)
            dxc_ref[:, di + g * n: di + (g + 1) * n] = (db_g + _dot(dcb_b, cg, TN)).astype(BF16)
            dxc_ref[:, di + g_n * n + g * n: di + g_n * n + (g + 1) * n] = (dc_g + _dot(dcb_b, bg)).astype(BF16)
        dxd = dxd_ref[...]
        e_b = e_ref[...]
        from_y = _exact_left(upper_b, _head_sums(off_ref[...], e_b))
        from_s = _exact_left(strict_lower.astype(BF16), _head_sums(vst_ref[...], e_b))
        carried = _head_sums(jnp.broadcast_to(last_ref[...], (8, di)), e_b)[0:1, :] * jnp.exp(q["acs"][l - 1:l, :])
        dla = from_y + from_s + carried + dla_diag_ref[...]
        ddt = dla * q["a_r"] + _head_sums(dxd * x, e_b)
        ddtr = ddt * jax.nn.sigmoid(q["sp_in"])
        ddtr_ref[...] = ddtr
        dxc_ref[:, 0:di] = (dxd * q["dt_exp"] + q["d_exp"] * dyv).astype(BF16)
        dd_exp = jnp.sum(dyv * x, axis=0, keepdims=True)
        dd = _head_sums(jnp.broadcast_to(dd_exp, (8, di)), e_b)[0:1, :]
        part = _stack_rows([jnp.sum(ddtr, axis=0, keepdims=True),
                            jnp.sum(dla * q["dt"], axis=0, keepdims=True) * q["a_r"], dd], LANES)

        @pl.when(first)
        def _():
            sums_ref[...] = part

        @pl.when(jnp.logical_not(first))
        def _():
            sums_ref[...] += part

    rev = lambda b, c: b * nc + (nc - 1 - c)
    tok = lambda w: pl.BlockSpec((l, w), lambda b, c: (rev(b, c), 0))
    const = lambda r, w: pl.BlockSpec((r, w), lambda b, c: (0, 0))
    return pl.pallas_call(
        body, name=name, grid=(nb, nc),
        in_specs=[tok(di), tok(cc), tok(LANES), pl.BlockSpec((LANES, l), lambda b, c: (0, rev(b, c))),
                  const(8, LANES), const(LANES, 8), const(LANES, di),
                  pl.BlockSpec((None, n, di), lambda b, c: (rev(b, c), 0, 0))],
        out_specs=[tok(cc), tok(LANES), const(8, LANES)],
        out_shape=[jax.ShapeDtypeStruct((nb * nc * l, cc), BF16), jax.ShapeDtypeStruct((nb * nc * l, LANES), F32),
                   jax.ShapeDtypeStruct((8, LANES), F32)],
        scratch_shapes=[pltpu.VMEM((n, di), F32), pltpu.VMEM((l, di), F32), pltpu.VMEM((l, di), F32),
                        pltpu.VMEM((1, di), F32), pltpu.VMEM((l, LANES), F32), pltpu.VMEM((l, di), F32),
                        pltpu.VMEM((l, di), F32), pltpu.VMEM((l, di), F32), pltpu.VMEM((8, di), F32)],
        compiler_params=_params("arbitrary", "arbitrary"),
    )(dy, xc, dtr, dtrt, prow, pcol, e_mat, sprev)


def _gate_norm_fwd(y, proj, z_col0, norm_g, di, name, tt=256):
    t = y.shape[0]
    tt = _tile(t, tt, 8)
    gw = di // SSM_GROUPS
    zw = _tile(math.gcd(di, z_col0), di, LANES)
    nz, zoff = di // zw, z_col0 // zw

    def body(*refs):
        y_ref, z_refs, g_ref, o_ref = refs[0], refs[1:1 + nz], refs[1 + nz], refs[2 + nz]
        for g in range(SSM_GROUPS):
            lo = g * gw
            zv = z_refs[lo // zw][:, lo % zw:lo % zw + gw].astype(F32)
            yg = y_ref[:, lo:lo + gw] * _silu(zv)
            r = lax.rsqrt(jnp.mean(yg * yg, axis=-1, keepdims=True) + EPS)
            o_ref[:, lo:lo + gw] = (yg * r * g_ref[:, lo:lo + gw]).astype(BF16)

    row = pl.BlockSpec((tt, di), lambda i: (i, 0))
    zspecs = [pl.BlockSpec((tt, zw), functools.partial(lambda i, k: (i, zoff + k), k=k)) for k in range(nz)]
    return pl.pallas_call(
        body, name=name, grid=(t // tt,), in_specs=[row] + zspecs + [pl.BlockSpec((1, di), lambda i: (0, 0))],
        out_specs=row, out_shape=jax.ShapeDtypeStruct((t, di), BF16), compiler_params=_params("parallel"),
    )(y, *([proj] * nz), norm_g)


def _gate_norm_bwd(dn, y, proj, z_col0, norm_g, di, name, tt=256):
    t = y.shape[0]
    tt = _tile(t, tt, 8)
    gw = di // SSM_GROUPS
    zw = _tile(math.gcd(di, z_col0), di, LANES)
    nz, zoff = di // zw, z_col0 // zw

    def body(*refs):
        dn_ref, y_ref, z_refs, g_ref = refs[0], refs[1], refs[2:2 + nz], refs[2 + nz]
        dy_ref, dz_ref, dg_ref = refs[3 + nz:]
        first = pl.program_id(0) == 0
        for g in range(SSM_GROUPS):
            lo = g * gw
            zv = z_refs[lo // zw][:, lo % zw:lo % zw + gw].astype(F32)
            yv = y_ref[:, lo:lo + gw]
            sz = _silu(zv)
            yg = yv * sz
            r = lax.rsqrt(jnp.mean(yg * yg, axis=-1, keepdims=True) + EPS)
            yh = yg * r
            dnv = dn_ref[:, lo:lo + gw].astype(F32)
            gy = dnv * g_ref[:, lo:lo + gw]
            dyg = r * (gy - yh * jnp.mean(gy * yh, axis=-1, keepdims=True))
            dy_ref[:, lo:lo + gw] = dyg * sz
            dz_ref[:, lo:lo + gw] = (dyg * yv * _dsilu(zv)).astype(BF16)
            part = jnp.sum(dnv * yh, axis=0, keepdims=True)

            @pl.when(first)
            def _():
                dg_ref[:, lo:lo + gw] = part

            @pl.when(jnp.logical_not(first))
            def _():
                dg_ref[:, lo:lo + gw] += part

    row = pl.BlockSpec((tt, di), lambda i: (i, 0))
    vec = pl.BlockSpec((1, di), lambda i: (0, 0))
    zspecs = [pl.BlockSpec((tt, zw), functools.partial(lambda i, k: (i, zoff + k), k=k)) for k in range(nz)]
    return pl.pallas_call(
        body, name=name, grid=(t // tt,), in_specs=[row, row] + zspecs + [vec], out_specs=[row, row, vec],
        out_shape=[jax.ShapeDtypeStruct((t, di), F32), jax.ShapeDtypeStruct((t, di), BF16), jax.ShapeDtypeStruct((1, di), F32)],
        compiler_params=_params("arbitrary"),
    )(dn, y, *([proj] * nz), norm_g)


def _softmax_rows(s):
    s = s - jnp.max(s, axis=-1, keepdims=True)
    e = jnp.exp(s)
    return e / jnp.sum(e, axis=-1, keepdims=True)


def _xattn_fwd(q, kv, nb, s, m, d, name, tq=512):
    tq = _tile(s, tq, 8)
    nq = s // tq
    hd = d // XATTN_HEADS
    scale = 1.0 / math.sqrt(hd)

    def body(q_ref, k_ref, v_ref, o_ref):
        for h in range(XATTN_HEADS):
            sl = slice(h * hd, (h + 1) * hd)
            prob = _softmax_rows(_dot(q_ref[:, sl], k_ref[:, sl], NT) * scale)
            o_ref[:, sl] = _dot(prob.astype(BF16), v_ref[:, sl]).astype(BF16)

    return pl.pallas_call(
        body, name=name, grid=(nb, nq),
        in_specs=[pl.BlockSpec((tq, d), lambda b, i: (b * nq + i, 0)), pl.BlockSpec((m, d), lambda b, i: (b, 0)),
                  pl.BlockSpec((m, d), lambda b, i: (b, 1))],
        out_specs=pl.BlockSpec((tq, d), lambda b, i: (b * nq + i, 0)),
        out_shape=jax.ShapeDtypeStruct((nb * s, d), BF16), compiler_params=_params("parallel", "parallel"),
    )(q, kv, kv)


def _xattn_bwd(do, q, kv, nb, s, m, d, name, tq=512):
    tq = _tile(s, tq, 8)
    nq = s // tq
    hd = d // XATTN_HEADS
    scale = 1.0 / math.sqrt(hd)

    def body(do_ref, q_ref, k_ref, v_ref, dq_ref, dk_ref, dv_ref):
        first = pl.program_id(1) == 0
        for h in range(XATTN_HEADS):
            sl = slice(h * hd, (h + 1) * hd)
            qh, kh, vh, doh = q_ref[:, sl], k_ref[:, sl], v_ref[:, sl], do_ref[:, sl]
            prob = _softmax_rows(_dot(qh, kh, NT) * scale)
            dv_h = _dot(prob.astype(BF16), doh, TN)
            dp = _dot(doh, vh, NT)
            ds = (prob * (dp - jnp.sum(dp * prob, axis=-1, keepdims=True)) * scale).astype(BF16)
            dq_ref[:, sl] = _dot(ds, kh).astype(BF16)
            dk_h = _dot(ds, qh, TN)

            @pl.when(first)
            def _():
                dk_ref[:, sl] = dk_h
                dv_ref[:, sl] = dv_h

            @pl.when(jnp.logical_not(first))
            def _():
                dk_ref[:, sl] += dk_h
                dv_ref[:, sl] += dv_h

    qspec = pl.BlockSpec((tq, d), lambda b, i: (b * nq + i, 0))
    dq, dk, dv = pl.pallas_call(
        body, name=name, grid=(nb, nq),
        in_specs=[qspec, qspec, pl.BlockSpec((m, d), lambda b, i: (b, 0)), pl.BlockSpec((m, d), lambda b, i: (b, 1))],
        out_specs=[qspec, pl.BlockSpec((m, d), lambda b, i: (b, 0)), pl.BlockSpec((m, d), lambda b, i: (b, 0))],
        out_shape=[jax.ShapeDtypeStruct((nb * s, d), BF16), jax.ShapeDtypeStruct((nb * m, d), F32),
                   jax.ShapeDtypeStruct((nb * m, d), F32)],
        compiler_params=_params("parallel", "arbitrary"),
    )(do, q, kv, kv)
    return dq, dk, dv


def _all_gather(shards, name):
    n_arr = len(shards)

    def body(*refs):
        x_refs, out_refs = refs[:n_arr], refs[n_arr:2 * n_arr]
        send_sems, recv_sems, local_sems = refs[2 * n_arr:]
        x, y, c = lax.axis_index("x"), lax.axis_index("y"), lax.axis_index("c")
        me, sibling = (x, y, c), (x, y, 1 - c)
        chips = [(1 - x, y), (x, 1 - y), (1 - x, 1 - y)]

        def copy(w, k, block, to, from_input=False):
            px, py, pc = block
            rows = out_refs[w].at[4 * px + 2 * py + pc]
            return pltpu.make_async_remote_copy(
                src_ref=x_refs[w] if from_input else rows, dst_ref=rows,
                send_sem=send_sems.at[7 * w + k], recv_sem=recv_sems.at[7 * w + k], device_id=to, device_id_type=MESH)

        started = []
        for w in range(n_arr):
            mine = pltpu.make_async_copy(x_refs[w], out_refs[w].at[4 * x + 2 * y + c], local_sems.at[w])
            mine.start()
            started.append(mine)
        sends = []
        for w in range(n_arr):
            sends.append(copy(w, 0, me, sibling, from_input=True))
            sends += [copy(w, 1 + j, me, (*chip, c), from_input=True) for j, chip in enumerate(chips)]
        for cp in sends:
            cp.start()
        for j, chip in enumerate(chips):
            for w in range(n_arr):
                copy(w, 1 + j, (*chip, c), me).wait_recv()
                passed = copy(w, 4 + j, (*chip, c), sibling)
                passed.start()
                sends.append(passed)
        for w in range(n_arr):
            copy(w, 0, sibling, me).wait_recv()
            for j, chip in enumerate(chips):
                copy(w, 4 + j, (*chip, 1 - c), me).wait_recv()
        for cp in sends:
            cp.wait_send()
        for mine in started:
            mine.wait()

    hbm = pl.BlockSpec(memory_space=pl.ANY)
    return pl.pallas_call(
        body, name=name, out_shape=[jax.ShapeDtypeStruct((N_DEV,) + s.shape, s.dtype) for s in shards],
        in_specs=[hbm] * n_arr, out_specs=[hbm] * n_arr,
        scratch_shapes=[pltpu.SemaphoreType.DMA((7 * n_arr,)), pltpu.SemaphoreType.DMA((7 * n_arr,)),
                        pltpu.SemaphoreType.DMA((n_arr,))],
    )(*shards)


_HBM = pl.BlockSpec(memory_space=pltpu.HBM)
_SEM = pl.BlockSpec(memory_space=pltpu.SEMAPHORE)
_DATAFLOW = pltpu.SideEffectType.DATAFLOW_SIDE_EFFECTING


def _peer_list(x, y, c):
    return [(1 - x if k & 4 else x, 1 - y if k & 2 else y, 1 - c if k & 1 else c) for k in range(1, N_DEV)]


def _push_copy(src_ref, land_ref, send_sems, recv_sems, w, k, peer, me, per_peer_src, receiving):
    px, py, pc = peer
    peer_slot = 4 * px + 2 * py + pc
    return pltpu.make_async_remote_copy(
        src_ref=src_ref.at[peer_slot] if per_peer_src else src_ref,
        dst_ref=land_ref.at[peer_slot if receiving else me],
        send_sem=send_sems.at[7 * w + k], recv_sem=recv_sems.at[7 * w + k], device_id=peer, device_id_type=MESH)


def _push_start(srcs, per_peer_src, after, name):
    n_arr = len(srcs)
    land_shapes = [s.shape if per_peer_src else (N_DEV,) + s.shape for s in srcs]

    def body(*refs):
        src_refs, land_refs = refs[:n_arr], refs[n_arr:2 * n_arr]
        send_sems, recv_sems = refs[2 * n_arr + 1], refs[2 * n_arr + 2]
        token = refs[-1]
        x, y, c = lax.axis_index("x"), lax.axis_index("y"), lax.axis_index("c")
        me = 4 * x + 2 * y + c
        for w in range(n_arr):
            for k, peer in enumerate(_peer_list(x, y, c)):
                _push_copy(src_refs[w], land_refs[w], send_sems, recv_sems, w, k, peer, me, per_peer_src, False).start()
        token[...] = jnp.zeros_like(token)

    lands = [pltpu.with_memory_space_constraint(lax.empty(ls, s.dtype), pltpu.HBM) for ls, s in zip(land_shapes, srcs)]
    srcs_hbm = [pltpu.with_memory_space_constraint(s, pltpu.HBM) for s in srcs]
    out = pl.pallas_call(
        body, name=name,
        out_shape=(pltpu.SemaphoreType.DMA((7 * n_arr,)), pltpu.SemaphoreType.DMA((7 * n_arr,)),
                   *[pltpu.HBM(s.shape, s.dtype) for s in srcs], *[pltpu.HBM(ls, s.dtype) for ls, s in zip(land_shapes, srcs)],
                   jax.ShapeDtypeStruct((8, LANES), F32)),
        in_specs=[_HBM] * (2 * n_arr) + [pl.BlockSpec(memory_space=pl.ANY)],
        out_specs=(_SEM, _SEM, *([_HBM] * (2 * n_arr)), pl.BlockSpec(memory_space=pltpu.VMEM)),
        input_output_aliases={i: 2 + i for i in range(2 * n_arr)},
        compiler_params=pltpu.CompilerParams(has_side_effects=_DATAFLOW),
    )(*srcs_hbm, *lands, after)
    return dict(send=out[0], recv=out[1], srcs=list(out[2:2 + n_arr]), lands=list(out[2 + n_arr:2 + 2 * n_arr]),
                token=out[-1])


def _push_wait(pending, per_peer_src, after, name):
    n_arr = len(pending["srcs"])

    def body(*refs):
        src_refs, land_refs = refs[:n_arr], refs[n_arr:2 * n_arr]
        send_sems, recv_sems = refs[2 * n_arr], refs[2 * n_arr + 1]
        x, y, c = lax.axis_index("x"), lax.axis_index("y"), lax.axis_index("c")
        me = 4 * x + 2 * y + c
        for w in range(n_arr):
            for k, peer in enumerate(_peer_list(x, y, c)):
                cp = _push_copy(src_refs[w], land_refs[w], send_sems, recv_sems, w, k, peer, me, per_peer_src, True)
                cp.wait_send()
                cp.wait_recv()

    out = pl.pallas_call(
        body, name=name,
        out_shape=tuple(pltpu.HBM(a.shape, a.dtype) for a in pending["srcs"] + pending["lands"]),
        in_specs=[_HBM] * (2 * n_arr) + [_SEM, _SEM, pl.BlockSpec(memory_space=pl.ANY)],
        out_specs=tuple([_HBM] * (2 * n_arr)),
        input_output_aliases={i: i for i in range(2 * n_arr)},
        compiler_params=pltpu.CompilerParams(has_side_effects=_DATAFLOW),
    )(*pending["srcs"], *pending["lands"], pending["send"], pending["recv"], after)
    return list(out[:n_arr]), list(out[n_arr:])


def _adamw_math(w, g, m, v):
    m = ADAM_B1 * m + (1.0 - ADAM_B1) * g
    v = ADAM_B2 * v + (1.0 - ADAM_B2) * (g * g)
    m_hat = m / (1.0 - ADAM_B1 ** ADAM_STEP)
    v_hat = v / (1.0 - ADAM_B2 ** ADAM_STEP)
    delta = -ADAM_LR * (m_hat / (jnp.sqrt(v_hat) + ADAM_EPS) + ADAM_WD * w)
    return delta, m, v


def _sum8(parts, name, tr=512):
    _, r, c_dim = parts.shape
    tr = _tile(r, tr, BF16_SUBLANES)

    def body(p_ref, o_ref):
        acc = p_ref[0].astype(F32)
        for k in range(1, N_DEV):
            acc = acc + p_ref[k].astype(F32)
        o_ref[...] = acc

    return pl.pallas_call(
        body, name=name, grid=(r // tr,), in_specs=[pl.BlockSpec((N_DEV, tr, c_dim), lambda i: (0, i, 0))],
        out_specs=pl.BlockSpec((tr, c_dim), lambda i: (i, 0)),
        out_shape=jax.ShapeDtypeStruct((r, c_dim), F32), compiler_params=_params("parallel"),
    )(parts)


def _sum8_adamw(parts, w, m, v, name, tr=128):
    _, r, c_dim = parts.shape
    tr = _tile(r, tr, BF16_SUBLANES)

    def body(p_ref, w_ref, m_ref, v_ref, g_ref, d_ref, nm_ref, nv_ref):
        g = p_ref[0].astype(F32)
        for k in range(1, N_DEV):
            g = g + p_ref[k].astype(F32)
        g_ref[...] = g
        d_ref[...], nm_ref[...], nv_ref[...] = _adamw_math(w_ref[...], g, m_ref[...], v_ref[...])

    row = pl.BlockSpec((tr, c_dim), lambda i: (i, 0))
    out = jax.ShapeDtypeStruct((r, c_dim), F32)
    return pl.pallas_call(
        body, name=name, grid=(r // tr,), in_specs=[pl.BlockSpec((N_DEV, tr, c_dim), lambda i: (0, i, 0)), row, row, row],
        out_specs=[row] * 4, out_shape=[out] * 4, compiler_params=_params("parallel"),
    )(parts, w, m, v)


def _adamw(g, w, m, v, name):
    r, c_dim = g.shape

    def body(g_ref, w_ref, m_ref, v_ref, d_ref, nm_ref, nv_ref):
        d_ref[...], nm_ref[...], nv_ref[...] = _adamw_math(w_ref[...], g_ref[...], m_ref[...], v_ref[...])

    out = jax.ShapeDtypeStruct((r, c_dim), F32)
    return pl.pallas_call(body, name=name, out_shape=[out] * 3)(g, w, m, v)


def _pack_rows(arrays, dtype, row_unit):
    chunks, offs, r0 = [], [], 0
    for a in arrays:
        flat = a.reshape(-1).astype(dtype)
        rows = -(-flat.shape[0] // (LANES * row_unit)) * row_unit
        flat = jnp.pad(flat, (0, rows * LANES - flat.shape[0]))
        chunks.append(flat.reshape(rows, LANES))
        offs.append((r0, rows))
        r0 += rows
    return jnp.concatenate(chunks, axis=0), offs


def _unpack_rows(packed, offs, shapes):
    out = []
    for (r0, rows), shape in zip(offs, shapes):
        n = math.prod(shape)
        blk = packed[..., r0:r0 + rows, :]
        blk = blk.reshape(packed.shape[:-2] + (rows * LANES,))[..., :n]
        out.append(blk.reshape(packed.shape[:-2] + tuple(shape)))
    return out


def _full_from_slots(blk, col_sharded):
    _, r, c = blk.shape
    if col_sharded:
        return blk.transpose(1, 0, 2).reshape(r, N_DEV * c)
    return blk.reshape(N_DEV * r, c)


def _slots_from_full(full, col_sharded):
    r, c = full.shape
    if col_sharded:
        return full.reshape(r, N_DEV, c // N_DEV).transpose(1, 0, 2)
    return full.reshape(N_DEV, r // N_DEV, c)


def _ffn_fwd(h, g, w_gu_t, w_d, tag):
    n = _rms_fwd(h, g, f"{tag}_norm")
    gu = _mm(n, w_gu_t, BF16, f"{tag}_up", nt=True)
    a = _swiglu_fwd(gu, f"{tag}_swiglu")
    h_out = _mm(a, w_d, F32, f"{tag}_down", res=h, alpha=FFN_RES_WEIGHT)
    return h_out, (h, n, gu, a)


def _ffn_bwd(dh_out, saved, g, w_gu_t, w_d, tag, dep, send_grads):
    h, n, gu, a = saved
    dw_d = _mm_tn(a, dh_out, f"{tag}_dw_down", alpha=FFN_RES_WEIGHT, dep=dep)
    da = _mm(dh_out, w_d, BF16, f"{tag}_da", nt=True, alpha=FFN_RES_WEIGHT)
    dgu = _swiglu_bwd(da, gu, f"{tag}_dswiglu")
    dw_gu_t = _mm_tn(dgu, n, f"{tag}_dw_up")
    dep = send_grads(dw_gu_t, dw_d)
    dn = _mm(dgu, w_gu_t, F32, f"{tag}_dn", dep=dep)
    dh, dg = _rms_bwd(dn, h, g, f"{tag}_dnorm", dres=dh_out)
    return dh, dg


W_GROUPS = (("ffn1_w_gate_up", "ffn1_w_down"),
            ("w_in", "w_out_a", "w_out_ssm", "w_mix_out"),
            ("w_q", "w_kv", "w_o_x", "ffn2_w_gate_up", "ffn2_w_down"))
G_GROUPS = (("ffn2_w_gate_up", "ffn2_w_down"),
            ("w_o_x", "w_q", "w_kv", "w_mix_out", "w_out_a", "w_out_ssm", "w_in"),
            ("ffn1_w_gate_up", "ffn1_w_down"))


def _local_step(x3, mem3, target3, small, comm):
    nb, s, d = x3.shape
    m_len = mem3.shape[1]
    t = nb * s
    nc = s // SSM_CHUNK
    di = small["ssm_norm"].shape[1]
    hs = di // SSM_HEAD_DIM
    cc = di + 2 * SSM_GROUPS * SSM_STATE
    x, mem, target = x3.reshape(t, d), mem3.reshape(nb * m_len, d), target3.reshape(t, d)

    sizes = (d, d, d, di, cc, hs, d, d)
    offs = [0]
    for sz in sizes:
        offs.append(offs[-1] + sz)
    z_col0, xbc_col0 = 3 * d, 3 * d + di
    ga_blk, gb_blk = (3 * d + di + cc) // d, (4 * d + di + cc) // d

    pad_vec = lambda v: jnp.pad(v.reshape(1, -1), ((0, 0), (0, LANES - hs)))
    prow = jnp.concatenate([pad_vec(small["ssm_dt_bias"]), pad_vec(small["ssm_a_log"]), pad_vec(small["ssm_d"]),
                            jnp.zeros((5, LANES), F32)], axis=0)
    pcol = prow.T
    e_mat = (lax.broadcasted_iota(jnp.int32, (LANES, di), 0)
             == lax.broadcasted_iota(jnp.int32, (LANES, di), 1) // SSM_HEAD_DIM).astype(BF16)
    conv_a_w8 = jnp.pad(small["conv_a_w"][0], ((0, 8 - CONV_A_K), (0, 0)))
    ssm_conv_w8 = jnp.pad(small["ssm_conv_w"][0], ((0, 8 - SSM_CONV_K), (0, 0)))

    wts, dep = comm.weights(0, None)
    h1, ffn1_saved = _ffn_fwd(x, small["ffn1_norm"] + dep[0, 0], wts["ffn1_w_gate_up"], wts["ffn1_w_down"], "ffn1")
    got, dep = comm.weights(1, h1)
    wts.update(got)
    w_in_t = wts["w_in"]
    w_main_t = jnp.concatenate([w_in_t[offs[i]:offs[i + 1]] for i in (0, 1, 2, 3, 4, 6, 7)], axis=0)
    w_dt_t = jnp.pad(w_in_t[offs[5]:offs[6]], ((0, LANES - hs), (0, 0)))
    u = _rms_fwd(h1, small["mix_norm"] + dep[0, 0], "mix_norm")
    proj = _mm(u, w_main_t, BF16, "in_proj", nt=True)
    dtr = _mm(u, w_dt_t, F32, "in_proj_dt", nt=True)
    yap = _conv_a_fwd(proj, conv_a_w8, nb, s, d, "conv_a")
    y_a = _mm(yap, wts["w_out_a"], BF16, "out_a")
    xc = _conv_s_fwd(proj, xbc_col0, ssm_conv_w8, small["ssm_conv_b"], nb, s, cc, "conv_s")
    dtrt = dtr.T
    y_ssd, sprev = _ssd_fwd(xc, dtr, dtrt, prow, pcol, nb, nc, di, "ssd")
    ygn = _gate_norm_fwd(y_ssd, proj, z_col0, small["ssm_norm"], di, "gate_norm")
    y_b = _mm(ygn, wts["w_out_ssm"], BF16, "out_ssm")
    merged = _merge_fwd(y_a, y_b, proj, ga_blk, gb_blk, d, "merge")
    h2 = _mm(merged, wts["w_mix_out"], F32, "mix_out", res=h1)
    got, _ = comm.weights(2, h2)
    wts.update(got)
    un = _rms_fwd(h2, small["xattn_norm"], "xattn_norm")
    mn = _rms_fwd(mem, small["mem_norm"], "mem_norm")
    q = _mm(un, wts["w_q"], BF16, "q_proj")
    kv = _mm(mn, wts["w_kv"], BF16, "kv_proj", nt=True)
    o = _xattn_fwd(q, kv, nb, s, m_len, d, "xattn")
    h3 = _mm(o, wts["w_o_x"], F32, "o_proj", res=h2)
    h4, ffn2_saved = _ffn_fwd(h3, small["ffn2_norm"], wts["ffn2_w_gate_up"], wts["ffn2_w_down"], "ffn2")
    loss_vec, dh4, dg_final = _loss_head(h4, small["final_norm"].reshape(1, d), target, "loss_head")

    grads = {"final_norm": dg_final.reshape(d)}
    big = {}
    dh3, grads["ffn2_norm"] = _ffn_bwd(
        dh4, ffn2_saved, small["ffn2_norm"], wts["ffn2_w_gate_up"], wts["ffn2_w_down"], "ffn2", None,
        lambda dw_gu_t, dw_d: comm.grads(0, {"ffn2_w_gate_up": dw_gu_t, "ffn2_w_down": dw_d}))
    big["w_o_x"] = _mm_tn(o, dh3, "dw_o")
    do = _mm(dh3, wts["w_o_x"], BF16, "d_o", nt=True)
    dq, dk, dv = _xattn_bwd(do, q, kv, nb, s, m_len, d, "xattn_bwd")
    big["w_q"] = _mm_tn(un, dq, "dw_q")
    big["w_kv"] = jnp.concatenate([_mm_tn(dk, mn, "dw_k"), _mm_tn(dv, mn, "dw_v")], axis=0)
    dun = _mm(dq, wts["w_q"], F32, "d_un", nt=True)
    dmn = _mm([dk, dv], wts["w_kv"], F32, "d_mn")
    _, grads["mem_norm"] = _rms_bwd(dmn, mem, small["mem_norm"], "mem_dnorm")
    dh2, grads["xattn_norm"] = _rms_bwd(dun, h2, small["xattn_norm"], "xattn_dnorm", dres=dh3)
    big["w_mix_out"] = _mm_tn(merged, dh2, "dw_mix")
    dmerged = _mm(dh2, wts["w_mix_out"], BF16, "d_merged", nt=True)
    dya, dyb, dga, dgb = _merge_bwd(dmerged, y_a, y_b, proj, ga_blk, gb_blk, d, "merge_bwd")
    big["w_out_a"] = _mm_tn(yap, dya, "dw_out_a")
    big["w_out_ssm"] = _mm_tn(ygn, dyb, "dw_out_ssm")
    dyap = _mm(dya, wts["w_out_a"], BF16, "d_yap", nt=True)
    dygn = _mm(dyb, wts["w_out_ssm"], BF16, "d_ygn", nt=True)
    dab, dac, dav, dconv_a = _conv_a_bwd(dyap, proj, conv_a_w8, nb, s, d, "conv_a_bwd")
    dy_ssd, dz, grads["ssm_norm"] = _gate_norm_bwd(dygn, y_ssd, proj, z_col0, small["ssm_norm"], di, "gate_norm_bwd")
    dxc, ddtr, ssd_sums = _ssd_bwd(dy_ssd, xc, dtr, dtrt, prow, pcol, e_mat, sprev, nb, nc, di, "ssd_bwd")
    dxbc, dconv_s, grads["ssm_conv_b"] = _conv_s_bwd(dxc, proj, xbc_col0, ssm_conv_w8, small["ssm_conv_b"], nb, s, cc, "conv_s_bwd")
    dpieces = [("ab", dab), ("ac", dac), ("av", dav), ("z", dz), ("xbc", dxbc), ("ga", dga), ("gb", dgb)]
    dw = {tag: _mm_tn(piece, u, f"dw_in_{tag}") for tag, piece in dpieces}
    dw_dt = _mm_tn(ddtr, u, "dw_in_dt")[:hs]
    du_main = _mm([piece for _, piece in dpieces], w_main_t, F32, "d_u", tk=1024)
    du = _mm(ddtr, w_dt_t, F32, "d_u_dt", res=du_main)
    dh1, grads["mix_norm"] = _rms_bwd(du, h1, small["mix_norm"], "mix_dnorm", dres=dh2)
    big["w_in"] = jnp.concatenate([dw["ab"], dw["ac"], dw["av"], dw["z"], dw["xbc"], dw_dt, dw["ga"], dw["gb"]], axis=0)
    dep = comm.grads(1, big)
    dx, grads["ffn1_norm"] = _ffn_bwd(
        dh1, ffn1_saved, small["ffn1_norm"], wts["ffn1_w_gate_up"], wts["ffn1_w_down"], "ffn1", dep,
        lambda dw_gu_t, dw_d: comm.grads(2, {"ffn1_w_gate_up": dw_gu_t, "ffn1_w_down": dw_d}))

    grads["conv_a_w"] = dconv_a[:CONV_A_K]
    grads["ssm_conv_w"] = dconv_s[:SSM_CONV_K]
    grads["ssm_dt_bias"] = ssd_sums[0:1, :hs]
    grads["ssm_a_log"] = ssd_sums[1:2, :hs]
    grads["ssm_d"] = ssd_sums[2:3, :hs]
    return loss_vec[0, 0], dx.reshape(nb, s, d), grads


def _step(inputs):
    w = {k: inputs[k] for k in WEIGHT_ORDER}
    mom = {k: inputs["m_" + k] for k in WEIGHT_ORDER}
    vel = {k: inputs["v_" + k] for k in WEIGHT_ORDER}
    me = 4 * lax.axis_index("x") + 2 * lax.axis_index("y") + lax.axis_index("c")

    send = {k: (w[k][0].T if k in COL_SHARDED else w[k][0]).astype(BF16) for k in BIG_WEIGHTS}

    def own_slot(land, mine):
        return lax.dynamic_update_slice(land, mine[None], (me, 0, 0))

    gathers = {0: _push_start([send[k] for k in W_GROUPS[0]], False, jnp.zeros((8, LANES), F32), "gather0_start")}
    exchanges = {}

    def weights(i, after):
        pending = gathers[i]
        sent, lands = _push_wait(pending, False, pending["token"] if after is None else after, f"gather{i}_wait")
        full = {k: own_slot(land, mine).reshape(N_DEV * mine.shape[0], mine.shape[1])
                for k, land, mine in zip(W_GROUPS[i], lands, sent)}
        dep = pending["token"]
        if i + 1 < len(W_GROUPS):
            gathers[i + 1] = _push_start([send[k] for k in W_GROUPS[i + 1]], False, lands[0], f"gather{i + 1}_start")
            dep = gathers[i + 1]["token"]
        return full, dep

    def send_grads(i, by_name):
        slots = [by_name[k].reshape((N_DEV,) + send[k].shape) for k in G_GROUPS[i]]
        exchanges[i] = _push_start(slots, True, slots[0], f"exchange{i}_start")
        return exchanges[i]["token"]

    comm = types.SimpleNamespace(weights=weights, grads=send_grads)
    small = {k: w[k] for k in SMALL_REPLICATED}
    conv_shapes = [w[k].shape[1:] for k in SMALL_SHARDED]
    packed_c, conv_offs = _pack_rows([w[k][0] for k in SMALL_SHARDED], F32, 8)
    conv_blocks = _unpack_rows(_all_gather([packed_c], "gather_conv_weights")[0], conv_offs, conv_shapes)
    for k, b in zip(SMALL_SHARDED, conv_blocks):
        small[k] = _full_from_slots(b, True)[None]

    loss_local, grad_x, grads = _local_step(inputs["x"], inputs["mem"], inputs["loss_target"], small, comm)
    loss = lax.psum(loss_local, AXES)

    out = {}
    for i, names in enumerate(G_GROUPS):
        sent, lands = _push_wait(exchanges[i], True, grad_x, f"exchange{i}_wait")
        for k, land, slots in zip(names, lands, sent):
            parts = own_slot(land, lax.dynamic_index_in_dim(slots, me, 0, keepdims=False))
            if k in COL_SHARDED:
                parts = parts.transpose(0, 2, 1)
            out[k] = tuple(o[None] for o in _sum8_adamw(parts, w[k][0], mom[k][0], vel[k][0], f"sum_adamw_{k}"))

    small_names = SMALL_REPLICATED + SMALL_SHARDED
    packed_g, small_offs = _pack_rows([grads[k] for k in small_names], F32, 8)
    total = _sum8(_all_gather([packed_g], "gather_small_grads")[0], "sum_small_grads")
    full_grads = _unpack_rows(total, small_offs, [grads[k].shape for k in small_names])
    mine = {}
    for k, g in zip(small_names, full_grads):
        if k in SMALL_SHARDED:
            c_loc = w[k].shape[2]
            g = lax.dynamic_slice_in_dim(g, me * c_loc, c_loc, axis=1)
        mine[k] = g.reshape(w[k].shape)
    sg, s_offs = _pack_rows([mine[k] for k in small_names], F32, 8)
    sw, _ = _pack_rows([w[k] for k in small_names], F32, 8)
    sm, _ = _pack_rows([mom[k] for k in small_names], F32, 8)
    sv, _ = _pack_rows([vel[k] for k in small_names], F32, 8)
    s_shapes = [w[k].shape for k in small_names]
    small_out = [_unpack_rows(a, s_offs, s_shapes) for a in _adamw(sg, sw, sm, sv, "adamw_small")]
    for i, k in enumerate(small_names):
        out[k] = (mine[k],) + tuple(o[i] for o in small_out)

    res = [loss, grad_x]
    for j in range(4):
        res += [out[k][j] for k in WEIGHT_ORDER]
    return tuple(res)


def kernel(x, mem, ffn1_norm, ffn1_w_gate_up, ffn1_w_down, mix_norm, w_in, conv_a_w, w_out_a, ssm_conv_w, ssm_conv_b, ssm_dt_bias, ssm_a_log, ssm_d, ssm_norm, w_out_ssm, w_mix_out, xattn_norm, mem_norm, w_q, w_kv, w_o_x, ffn2_norm, ffn2_w_gate_up, ffn2_w_down, final_norm, loss_target, m_ffn1_norm, m_ffn1_w_gate_up, m_ffn1_w_down, m_mix_norm, m_w_in, m_conv_a_w, m_w_out_a, m_ssm_conv_w, m_ssm_conv_b, m_ssm_dt_bias, m_ssm_a_log, m_ssm_d, m_ssm_norm, m_w_out_ssm, m_w_mix_out, m_xattn_norm, m_mem_norm, m_w_q, m_w_kv, m_w_o_x, m_ffn2_norm, m_ffn2_w_gate_up, m_ffn2_w_down, m_final_norm, v_ffn1_norm, v_ffn1_w_gate_up, v_ffn1_w_down, v_mix_norm, v_w_in, v_conv_a_w, v_w_out_a, v_ssm_conv_w, v_ssm_conv_b, v_ssm_dt_bias, v_ssm_a_log, v_ssm_d, v_ssm_norm, v_w_out_ssm, v_w_mix_out, v_xattn_norm, v_mem_norm, v_w_q, v_w_kv, v_w_o_x, v_ffn2_norm, v_ffn2_w_gate_up, v_ffn2_w_down, v_final_norm):
    return _step(dict(locals()))
```

```python
import functools
import math
import types

import jax
import jax.numpy as jnp
from jax import lax
from jax.experimental import pallas as pl
from jax.experimental.pallas import tpu as pltpu

F32, BF16 = jnp.float32, jnp.bfloat16
HI = lax.Precision.HIGHEST
MESH = pl.DeviceIdType.MESH
AXES = ("x", "y", "c")
N_DEV = 8

EPS = 1e-6
FFN_RES_WEIGHT = 0.5
SSM_HEAD_DIM = 64
SSM_GROUPS = 4
SSM_STATE = 128
SSM_CHUNK = 128
CONV_A_K = 3
SSM_CONV_K = 4
XATTN_HEADS = 4
ADAM_LR, ADAM_B1, ADAM_B2, ADAM_EPS, ADAM_WD, ADAM_STEP = 1e-3, 0.9, 0.999, 1e-8, 0.01, 10

LANES = 128
BF16_SUBLANES = 16
VMEM_LIMIT_BYTES = 56 * 2 ** 20
NEG_BIG = -1e30

BIG_WEIGHTS = ("ffn1_w_gate_up", "ffn1_w_down", "w_in", "w_out_a", "w_out_ssm", "w_mix_out",
               "w_q", "w_kv", "w_o_x", "ffn2_w_gate_up", "ffn2_w_down")
COL_SHARDED = ("ffn1_w_gate_up", "w_in", "w_kv", "ffn2_w_gate_up")
SMALL_REPLICATED = ("ffn1_norm", "mix_norm", "ssm_conv_b", "ssm_dt_bias", "ssm_a_log", "ssm_d", "ssm_norm",
                    "xattn_norm", "mem_norm", "ffn2_norm", "final_norm")
SMALL_SHARDED = ("conv_a_w", "ssm_conv_w")
WEIGHT_ORDER = ("ffn1_norm", "ffn1_w_gate_up", "ffn1_w_down", "mix_norm", "w_in", "conv_a_w", "w_out_a",
                "ssm_conv_w", "ssm_conv_b", "ssm_dt_bias", "ssm_a_log", "ssm_d", "ssm_norm", "w_out_ssm",
                "w_mix_out", "xattn_norm", "mem_norm", "w_q", "w_kv", "w_o_x", "ffn2_norm", "ffn2_w_gate_up",
                "ffn2_w_down", "final_norm")


def _tile(dim, pref, unit):
    best = None
    t = unit
    while t <= min(dim, pref):
        if dim % t == 0:
            best = t
        t += unit
    return best if best is not None else dim


def _params(*sem):
    return pltpu.CompilerParams(dimension_semantics=sem, vmem_limit_bytes=VMEM_LIMIT_BYTES)


def _silu(x):
    return x * jax.nn.sigmoid(x)


def _dsilu(x):
    s = jax.nn.sigmoid(x)
    return s * (1.0 + x * (1.0 - s))


def _softplus(x):
    return jnp.maximum(x, 0.0) + jnp.log(1.0 + jnp.exp(-jnp.abs(x)))


def _dot(a, b, dims=(((1,), (0,)), ((), ())), precision=None):
    return lax.dot_general(a, b, dims, preferred_element_type=F32, precision=precision)


def _stack_rows(rows, width):
    r_idx = lax.broadcasted_iota(jnp.int32, (8, width), 0)
    acc = jnp.zeros((8, width), F32)
    for k, row in enumerate(rows):
        acc = jnp.where(r_idx == k, row, acc)
    return acc


NT = (((1,), (1,)), ((), ()))
TN = (((0,), (0,)), ((), ()))


def _mm(a, b, out_dtype, name, res=None, alpha=1.0, nt=False, dep=None, tm=1024, tn=2048, tk=2816):
    pieces = list(a) if isinstance(a, (list, tuple)) else [a]
    m = pieces[0].shape[0]
    k = sum(p.shape[1] for p in pieces)
    n = b.shape[0] if nt else b.shape[1]
    assert (b.shape[1] if nt else b.shape[0]) == k
    tm, tn = _tile(m, tm, 8), _tile(n, tn, LANES)
    tk = _tile(math.gcd(*[p.shape[1] for p in pieces]), tk, LANES)
    nk = k // tk
    starts, s0 = [], 0
    for p in pieces:
        starts.append((s0, p.shape[1] // tk))
        s0 += p.shape[1] // tk
    n_p = len(pieces)

    def body(*refs):
        a_refs, b_ref = refs[:n_p], refs[n_p]
        r_ref = refs[n_p + 1] if res is not None else None
        n_in = n_p + 1 + (res is not None) + (dep is not None)
        o_ref = refs[n_in]
        scr = refs[n_in + 1:]

        def finish(acc):
            acc = alpha * acc if alpha != 1.0 else acc
            if r_ref is not None:
                acc = r_ref[...] + acc
            o_ref[...] = acc.astype(out_dtype)

        def product(a_ref):
            return _dot(a_ref[...].astype(BF16), b_ref[...].astype(BF16), NT if nt else (((1,), (0,)), ((), ())))

        if nk == 1:
            finish(product(a_refs[0]))
            return
        acc_ref = scr[0]
        kk = pl.program_id(2)
        for (s, cnt), a_ref in zip(starts, a_refs):
            if s == 0:
                @pl.when(kk == 0)
                def _():
                    acc_ref[...] = product(a_ref)

                @pl.when(jnp.logical_and(kk > 0, kk < cnt))
                def _():
                    acc_ref[...] += product(a_ref)
            else:
                @pl.when(jnp.logical_and(kk >= s, kk < s + cnt))
                def _():
                    acc_ref[...] += product(a_ref)

        @pl.when(kk == nk - 1)
        def _():
            finish(acc_ref[...])

    def a_spec(s, cnt):
        return pl.BlockSpec((tm, tk), lambda i, j, kk: (i, jnp.clip(kk - s, 0, cnt - 1)))

    in_specs = [a_spec(s, cnt) for s, cnt in starts]
    in_specs.append(pl.BlockSpec((tn, tk), lambda i, j, kk: (j, kk)) if nt else pl.BlockSpec((tk, tn), lambda i, j, kk: (kk, j)))
    args = pieces + [b]
    if res is not None:
        in_specs.append(pl.BlockSpec((tm, tn), lambda i, j, kk: (i, j)))
        args.append(res)
    if dep is not None:
        in_specs.append(pl.BlockSpec((8, LANES), lambda i, j, kk: (0, 0)))
        args.append(dep)
    return pl.pallas_call(
        body, name=name, grid=(m // tm, n // tn, nk), in_specs=in_specs,
        out_specs=pl.BlockSpec((tm, tn), lambda i, j, kk: (i, j)),
        out_shape=jax.ShapeDtypeStruct((m, n), out_dtype),
        scratch_shapes=[pltpu.VMEM((tm, tn), F32)] if nk > 1 else [],
        compiler_params=_params("parallel", "parallel", "arbitrary"),
    )(*args)


def _mm_tn(x, dy, name, out_dtype=BF16, alpha=1.0, dep=None, tko=1408, tn=1024, tt=1024):
    t, k = x.shape
    n = dy.shape[1]
    tko, tn, tt = _tile(k, tko, LANES), _tile(n, tn, LANES), _tile(t, tt, 8)
    nt_steps = t // tt

    def body(*refs):
        x_ref, dy_ref = refs[:2]
        o_ref, acc_ref = refs[-2:]
        part = _dot(x_ref[...].astype(BF16), dy_ref[...].astype(BF16), TN)
        step = pl.program_id(2)

        @pl.when(step == 0)
        def _():
            acc_ref[...] = part

        @pl.when(step > 0)
        def _():
            acc_ref[...] += part

        @pl.when(step == nt_steps - 1)
        def _():
            acc = acc_ref[...]
            o_ref[...] = (alpha * acc if alpha != 1.0 else acc).astype(out_dtype)

    in_specs = [pl.BlockSpec((tt, tko), lambda i, j, s: (s, i)), pl.BlockSpec((tt, tn), lambda i, j, s: (s, j))]
    args = [x, dy]
    if dep is not None:
        in_specs.append(pl.BlockSpec((8, LANES), lambda i, j, s: (0, 0)))
        args.append(dep)
    return pl.pallas_call(
        body, name=name, grid=(k // tko, n // tn, nt_steps), in_specs=in_specs,
        out_specs=pl.BlockSpec((tko, tn), lambda i, j, s: (i, j)),
        out_shape=jax.ShapeDtypeStruct((k, n), out_dtype),
        scratch_shapes=[pltpu.VMEM((tko, tn), F32)],
        compiler_params=_params("parallel", "parallel", "arbitrary"),
    )(*args)


def _rms_fwd(x, g, name, tt=512):
    t, d = x.shape
    tt = _tile(t, tt, 8)

    def body(x_ref, g_ref, o_ref):
        xv = x_ref[...]
        r = lax.rsqrt(jnp.mean(xv * xv, axis=-1, keepdims=True) + EPS)
        o_ref[...] = (xv * r * g_ref[...]).astype(BF16)

    return pl.pallas_call(
        body, name=name, grid=(t // tt,),
        in_specs=[pl.BlockSpec((tt, d), lambda i: (i, 0)), pl.BlockSpec((1, d), lambda i: (0, 0))],
        out_specs=pl.BlockSpec((tt, d), lambda i: (i, 0)),
        out_shape=jax.ShapeDtypeStruct((t, d), BF16), compiler_params=_params("parallel"),
    )(x, g)


def _rms_bwd(dn, x, g, name, dres=None, tt=512):
    t, d = x.shape
    tt = _tile(t, tt, 8)

    def body(*refs):
        if dres is None:
            dn_ref, x_ref, g_ref, dx_ref, dg_ref = refs
            r_ref = None
        else:
            dn_ref, x_ref, g_ref, r_ref, dx_ref, dg_ref = refs
        xv, dnv = x_ref[...], dn_ref[...].astype(F32)
        r = lax.rsqrt(jnp.mean(xv * xv, axis=-1, keepdims=True) + EPS)
        xh = xv * r
        gy = dnv * g_ref[...]
        dx = r * (gy - xh * jnp.mean(gy * xh, axis=-1, keepdims=True))
        if r_ref is not None:
            dx = dx + r_ref[...]
        dx_ref[...] = dx
        part = jnp.sum(dnv * xh, axis=0, keepdims=True)

        @pl.when(pl.program_id(0) == 0)
        def _():
            dg_ref[...] = part

        @pl.when(pl.program_id(0) > 0)
        def _():
            dg_ref[...] += part

    row = pl.BlockSpec((tt, d), lambda i: (i, 0))
    vec = pl.BlockSpec((1, d), lambda i: (0, 0))
    in_specs, args = [row, row, vec], [dn, x, g]
    if dres is not None:
        in_specs.append(row)
        args.append(dres)
    return pl.pallas_call(
        body, name=name, grid=(t // tt,), in_specs=in_specs, out_specs=[row, vec],
        out_shape=[jax.ShapeDtypeStruct((t, d), F32), jax.ShapeDtypeStruct((1, d), F32)],
        compiler_params=_params("arbitrary"),
    )(*args)


def _ffn_up(n, w_gu_t, name, tm=1024, tf=1408):
    t, d = n.shape
    f = w_gu_t.shape[0] // 2
    tm, tf = _tile(t, tm, 8), _tile(f, tf, LANES)
    nf = f // tf

    def body(n_ref, wg_ref, wu_ref, g_ref, u_ref, a_ref):
        nv = n_ref[...]
        gate, up = _dot(nv, wg_ref[...], NT), _dot(nv, wu_ref[...], NT)
        g_ref[...] = gate.astype(BF16)
        u_ref[...] = up.astype(BF16)
        a_ref[...] = (_silu(gate) * up).astype(BF16)

    blk = pl.BlockSpec((tm, tf), lambda i, j: (i, j))
    out = jax.ShapeDtypeStruct((t, f), BF16)
    return pl.pallas_call(
        body, name=name, grid=(t // tm, nf),
        in_specs=[pl.BlockSpec((tm, d), lambda i, j: (i, 0)), pl.BlockSpec((tf, d), lambda i, j: (j, 0)),
                  pl.BlockSpec((tf, d), lambda i, j: (j + nf, 0))],
        out_specs=[blk, blk, blk], out_shape=[out, out, out], compiler_params=_params("parallel", "parallel"),
    )(n, w_gu_t, w_gu_t)


def _ffn_da(dh, w_d, gate, up, name, alpha, dep=None, tm=1024, tf=1408):
    t, d = dh.shape
    f = w_d.shape[0]
    tm, tf = _tile(t, tm, 8), _tile(f, tf, LANES)

    def body(*refs):
        dh_ref, w_ref, g_ref, u_ref = refs[:4]
        dg_ref, du_ref = refs[-2:]
        da = alpha * _dot(dh_ref[...].astype(BF16), w_ref[...], NT)
        gv = g_ref[...].astype(F32)
        dg_ref[...] = (da * u_ref[...].astype(F32) * _dsilu(gv)).astype(BF16)
        du_ref[...] = (da * _silu(gv)).astype(BF16)

    blk = pl.BlockSpec((tm, tf), lambda i, j: (i, j))
    in_specs = [pl.BlockSpec((tm, d), lambda i, j: (i, 0)), pl.BlockSpec((tf, d), lambda i, j: (j, 0)), blk, blk]
    args = [dh, w_d, gate, up]
    if dep is not None:
        in_specs.append(pl.BlockSpec((8, LANES), lambda i, j: (0, 0)))
        args.append(dep)
    out = jax.ShapeDtypeStruct((t, f), BF16)
    return pl.pallas_call(
        body, name=name, grid=(t // tm, f // tf), in_specs=in_specs, out_specs=[blk, blk], out_shape=[out, out],
        compiler_params=_params("parallel", "parallel"),
    )(*args)


def _merge_fwd(ya, yb, proj, ga_blk, gb_blk, d, name, tt=512):
    t = ya.shape[0]
    tt = _tile(t, tt, 8)

    def body(ya_ref, yb_ref, ga_ref, gb_ref, o_ref):
        o_ref[...] = (jax.nn.sigmoid(ga_ref[...].astype(F32)) * ya_ref[...].astype(F32)
                      + jax.nn.sigmoid(gb_ref[...].astype(F32)) * yb_ref[...].astype(F32)).astype(BF16)

    row = pl.BlockSpec((tt, d), lambda i: (i, 0))
    return pl.pallas_call(
        body, name=name, grid=(t // tt,),
        in_specs=[row, row, pl.BlockSpec((tt, d), lambda i: (i, ga_blk)), pl.BlockSpec((tt, d), lambda i: (i, gb_blk))],
        out_specs=row, out_shape=jax.ShapeDtypeStruct((t, d), BF16), compiler_params=_params("parallel"),
    )(ya, yb, proj, proj)


def _merge_bwd(dm, ya, yb, proj, ga_blk, gb_blk, d, name, tt=512):
    t = ya.shape[0]
    tt = _tile(t, tt, 8)

    def body(dm_ref, ya_ref, yb_ref, ga_ref, gb_ref, dya_ref, dyb_ref, dga_ref, dgb_ref):
        dmv = dm_ref[...].astype(F32)
        sa, sb = jax.nn.sigmoid(ga_ref[...].astype(F32)), jax.nn.sigmoid(gb_ref[...].astype(F32))
        dya_ref[...] = (dmv * sa).astype(BF16)
        dyb_ref[...] = (dmv * sb).astype(BF16)
        dga_ref[...] = (dmv * ya_ref[...].astype(F32) * sa * (1.0 - sa)).astype(BF16)
        dgb_ref[...] = (dmv * yb_ref[...].astype(F32) * sb * (1.0 - sb)).astype(BF16)

    row = pl.BlockSpec((tt, d), lambda i: (i, 0))
    out = jax.ShapeDtypeStruct((t, d), BF16)
    return pl.pallas_call(
        body, name=name, grid=(t // tt,),
        in_specs=[row, row, row, pl.BlockSpec((tt, d), lambda i: (i, ga_blk)), pl.BlockSpec((tt, d), lambda i: (i, gb_blk))],
        out_specs=[row] * 4, out_shape=[out] * 4, compiler_params=_params("parallel"),
    )(dm, ya, yb, proj, proj)


def _loss_head(h, g, target, name, tt=512):
    t, d = h.shape
    tt = _tile(t, tt, 8)

    def body(h_ref, g_ref, tg_ref, loss_ref, dh_ref, dg_ref):
        xv = h_ref[...]
        r = lax.rsqrt(jnp.mean(xv * xv, axis=-1, keepdims=True) + EPS)
        xh = xv * r
        err = xh * g_ref[...] - tg_ref[...]
        dout = err * (1.0 / d)
        gy = dout * g_ref[...]
        dh_ref[...] = r * (gy - xh * jnp.mean(gy * xh, axis=-1, keepdims=True))
        dg_part = jnp.sum(dout * xh, axis=0, keepdims=True)
        loss_part = jnp.full((1, LANES), 0.5 / d, F32) * jnp.sum(err * err)

        @pl.when(pl.program_id(0) == 0)
        def _():
            dg_ref[...] = dg_part
            loss_ref[...] = loss_part

        @pl.when(pl.program_id(0) > 0)
        def _():
            dg_ref[...] += dg_part
            loss_ref[...] += loss_part

    row = pl.BlockSpec((tt, d), lambda i: (i, 0))
    vec = pl.BlockSpec((1, d), lambda i: (0, 0))
    return pl.pallas_call(
        body, name=name, grid=(t // tt,), in_specs=[row, vec, row],
        out_specs=[pl.BlockSpec((1, LANES), lambda i: (0, 0)), row, vec],
        out_shape=[jax.ShapeDtypeStruct((1, LANES), F32), jax.ShapeDtypeStruct((t, d), F32), jax.ShapeDtypeStruct((1, d), F32)],
        compiler_params=_params("arbitrary"),
    )(h, g, target)


def _shift_down(x, k, t_idx):
    if k == 0:
        return x
    return jnp.where(t_idx >= k, pltpu.roll(x, k, 0), 0.0)


def _shift_up(x, k, t_idx, s):
    if k == 0:
        return x
    return jnp.where(t_idx < s - k, pltpu.roll(x, s - k, 0), 0.0)


def _conv_a_fwd(proj, w, nb, s, d, name, cb=256):
    cb = _tile(d, cb, LANES)
    nd = d // cb

    def body(b_ref, c_ref, v_ref, w_ref, o_ref):
        t_idx = lax.broadcasted_iota(jnp.int32, (s, cb), 0)
        cv = c_ref[...].astype(F32) * v_ref[...].astype(F32)
        cc = sum(w_ref[k:k + 1, :] * _shift_down(cv, CONV_A_K - 1 - k, t_idx) for k in range(CONV_A_K))
        o_ref[...] = (b_ref[...].astype(F32) * cc).astype(BF16)

    def col(off):
        return pl.BlockSpec((s, cb), lambda b, j: (b, j + off * nd))

    return pl.pallas_call(
        body, name=name, grid=(nb, nd), in_specs=[col(0), col(1), col(2), pl.BlockSpec((8, cb), lambda b, j: (0, j))],
        out_specs=pl.BlockSpec((s, cb), lambda b, j: (b, j)),
        out_shape=jax.ShapeDtypeStruct((nb * s, d), BF16), compiler_params=_params("parallel", "parallel"),
    )(proj, proj, proj, w)


def _conv_a_bwd(dy, proj, w, nb, s, d, name, cb=256):
    cb = _tile(d, cb, LANES)
    nd = d // cb

    def body(dy_ref, b_ref, c_ref, v_ref, w_ref, db_ref, dc_ref, dv_ref, dw_ref):
        t_idx = lax.broadcasted_iota(jnp.int32, (s, cb), 0)
        cv_c, cv_v = c_ref[...].astype(F32), v_ref[...].astype(F32)
        cv = cv_c * cv_v
        shifted = [_shift_down(cv, CONV_A_K - 1 - k, t_idx) for k in range(CONV_A_K)]
        cc = sum(w_ref[k:k + 1, :] * shifted[k] for k in range(CONV_A_K))
        dyv = dy_ref[...].astype(F32)
        db_ref[...] = (dyv * cc).astype(BF16)
        dcc = dyv * b_ref[...].astype(F32)
        dcv = sum(w_ref[k:k + 1, :] * _shift_up(dcc, CONV_A_K - 1 - k, t_idx, s) for k in range(CONV_A_K))
        dc_ref[...] = (dcv * cv_v).astype(BF16)
        dv_ref[...] = (dcv * cv_c).astype(BF16)
        rows = [jnp.sum(dcc * shifted[k], axis=0, keepdims=True) for k in range(CONV_A_K)]
        part = _stack_rows(rows, cb)

        @pl.when(pl.program_id(1) == 0)
        def _():
            dw_ref[...] = part

        @pl.when(pl.program_id(1) > 0)
        def _():
            dw_ref[...] += part

    def col(off):
        return pl.BlockSpec((s, cb), lambda j, b: (b, j + off * nd))

    own = pl.BlockSpec((s, cb), lambda j, b: (b, j))
    wspec = pl.BlockSpec((8, cb), lambda j, b: (0, j))
    out = jax.ShapeDtypeStruct((nb * s, d), BF16)
    return pl.pallas_call(
        body, name=name, grid=(nd, nb), in_specs=[own, col(0), col(1), col(2), wspec],
        out_specs=[own, own, own, wspec], out_shape=[out, out, out, jax.ShapeDtypeStruct((8, d), F32)],
        compiler_params=_params("parallel", "arbitrary"),
    )(dy, proj, proj, proj, w)


def _conv_s_fwd(proj, col0, w, bias, nb, s, cc_width, name, cb=256):
    cb = _tile(math.gcd(cc_width, col0) if col0 else cc_width, cb, LANES)
    nd, off = cc_width // cb, col0 // cb

    def body(x_ref, w_ref, b_ref, o_ref):
        t_idx = lax.broadcasted_iota(jnp.int32, (s, cb), 0)
        xv = x_ref[...].astype(F32)
        pre = b_ref[...] + sum(w_ref[k:k + 1, :] * _shift_down(xv, SSM_CONV_K - 1 - k, t_idx) for k in range(SSM_CONV_K))
        o_ref[...] = _silu(pre).astype(BF16)

    vec = pl.BlockSpec((8, cb), lambda b, j: (0, j))
    return pl.pallas_call(
        body, name=name, grid=(nb, nd),
        in_specs=[pl.BlockSpec((s, cb), lambda b, j: (b, j + off)), vec, pl.BlockSpec((1, cb), lambda b, j: (0, j))],
        out_specs=pl.BlockSpec((s, cb), lambda b, j: (b, j)),
        out_shape=jax.ShapeDtypeStruct((nb * s, cc_width), BF16), compiler_params=_params("parallel", "parallel"),
    )(proj, w, bias)


def _conv_s_bwd(dxc, proj, col0, w, bias, nb, s, cc_width, name, cb=256):
    cb = _tile(math.gcd(cc_width, col0) if col0 else cc_width, cb, LANES)
    nd, off = cc_width // cb, col0 // cb

    def body(d_ref, x_ref, w_ref, b_ref, dx_ref, dw_ref, db_ref):
        t_idx = lax.broadcasted_iota(jnp.int32, (s, cb), 0)
        xv = x_ref[...].astype(F32)
        shifted = [_shift_down(xv, SSM_CONV_K - 1 - k, t_idx) for k in range(SSM_CONV_K)]
        pre = b_ref[...] + sum(w_ref[k:k + 1, :] * shifted[k] for k in range(SSM_CONV_K))
        dpre = d_ref[...].astype(F32) * _dsilu(pre)
        dx = sum(w_ref[k:k + 1, :] * _shift_up(dpre, SSM_CONV_K - 1 - k, t_idx, s) for k in range(SSM_CONV_K))
        dx_ref[...] = dx.astype(BF16)
        rows = [jnp.sum(dpre * shifted[k], axis=0, keepdims=True) for k in range(SSM_CONV_K)]
        dw_part = _stack_rows(rows, cb)
        db_part = jnp.sum(dpre, axis=0, keepdims=True)

        @pl.when(pl.program_id(1) == 0)
        def _():
            dw_ref[...] = dw_part
            db_ref[...] = db_part

        @pl.when(pl.program_id(1) > 0)
        def _():
            dw_ref[...] += dw_part
            db_ref[...] += db_part

    own = pl.BlockSpec((s, cb), lambda j, b: (b, j))
    wspec = pl.BlockSpec((8, cb), lambda j, b: (0, j))
    bspec = pl.BlockSpec((1, cb), lambda j, b: (0, j))
    return pl.pallas_call(
        body, name=name, grid=(nd, nb),
        in_specs=[own, pl.BlockSpec((s, cb), lambda j, b: (b, j + off)), wspec, bspec],
        out_specs=[own, wspec, bspec],
        out_shape=[jax.ShapeDtypeStruct((nb * s, cc_width), BF16), jax.ShapeDtypeStruct((8, cc_width), F32),
                   jax.ShapeDtypeStruct((1, cc_width), F32)],
        compiler_params=_params("parallel", "arbitrary"),
    )(dxc, proj, w, bias)


def _split3(v):
    hi = v.astype(BF16)
    r1 = v - hi.astype(F32)
    mid = r1.astype(BF16)
    return hi, mid, (r1 - mid.astype(F32)).astype(BF16)


def _exact_left(mask_b, v):
    return sum(_dot(mask_b, t) for t in _split3(v))


def _exact_right(v, mask_b):
    return sum(_dot(t, mask_b) for t in _split3(v))


def _head_sums(v, e_b):
    hi = v.astype(BF16)
    lo = (v - hi.astype(F32)).astype(BF16)
    return _dot(hi, e_b, NT) + _dot(lo, e_b, NT)


def _spread(v, out_ref, di):
    lane = lax.broadcasted_iota(jnp.int32, (v.shape[0], LANES), 1)
    for pr in range(di // LANES):
        h0 = pr * (LANES // SSM_HEAD_DIM)
        out_ref[:, pr * LANES:(pr + 1) * LANES] = jnp.where(lane < SSM_HEAD_DIM, v[:, h0:h0 + 1], v[:, h0 + 1:h0 + 2])


def _ssd_common(xc_ref, dtr_ref, dtrt_ref, prow_ref, pcol_ref, dtx_ref, acsx_ref, dx_ref, di):
    l = SSM_CHUNK
    bias_r, a_r = prow_ref[0:1, :], -jnp.exp(prow_ref[1:2, :])
    sp_in = dtr_ref[...] + bias_r
    dt = _softplus(sp_in)
    li = lax.broadcasted_iota(jnp.int32, (l, l), 0)
    si = lax.broadcasted_iota(jnp.int32, (l, l), 1)
    lower_b = (li >= si).astype(BF16)
    upper_b = (li <= si).astype(BF16)
    acs = _exact_left(lower_b, dt * a_r)
    bias_c, a_c = pcol_ref[:, 0:1], -jnp.exp(pcol_ref[:, 1:2])
    dt_t = _softplus(dtrt_ref[...] + bias_c)
    acs_t = _exact_right(dt_t * a_c, upper_b)
    _spread(dt, dtx_ref, di)
    _spread(acs, acsx_ref, di)
    _spread(prow_ref[0:8, :], dx_ref, di)
    acs_exp = acsx_ref[...]
    acs_last = acs_exp[l - 1:l, :]
    x = xc_ref[:, 0:di].astype(F32)
    return dict(dt=dt, a_r=a_r, sp_in=sp_in, acs=acs, acs_t=acs_t, dt_exp=dtx_ref[...], e_exp=jnp.exp(acs_exp),
                el_exp=jnp.exp(acs_last), f_exp=jnp.exp(acs_last - acs_exp), x=x, mask=li >= si, upper_b=upper_b,
                d_exp=dx_ref[2:3, :])


def _decay(q, h):
    seg = q["acs"][:, h:h + 1] - q["acs_t"][h:h + 1, :]
    return jnp.exp(jnp.where(q["mask"], seg, NEG_BIG))


def _ssd_fwd(xc, dtr, dtrt, prow, pcol, nb, nc, di, name):
    l, n, g_n, p = SSM_CHUNK, SSM_STATE, SSM_GROUPS, SSM_HEAD_DIM
    cc = xc.shape[1]
    gw = di // g_n
    assert p * 2 == LANES and gw % LANES == 0

    def body(xc_ref, dtr_ref, dtrt_ref, prow_ref, pcol_ref, y_ref, sprev_ref, st_ref, dtx_ref, acsx_ref, dx_ref):
        @pl.when(pl.program_id(1) == 0)
        def _():
            st_ref[...] = jnp.zeros_like(st_ref)

        q = _ssd_common(xc_ref, dtr_ref, dtrt_ref, prow_ref, pcol_ref, dtx_ref, acsx_ref, dx_ref, di)
        x = q["x"]
        xd = x * q["dt_exp"]
        xdb = xd.astype(BF16)
        xdf = (xd * q["f_exp"]).astype(BF16)
        lane = lax.broadcasted_iota(jnp.int32, (l, LANES), 1)
        for g in range(g_n):
            lo = g * gw
            bg = xc_ref[:, di + g * n: di + (g + 1) * n]
            cg = xc_ref[:, di + g_n * n + g * n: di + g_n * n + (g + 1) * n]
            cb = _dot(cg, bg, NT)
            st_g = st_ref[:, lo:lo + gw]
            y_off = q["e_exp"][:, lo:lo + gw] * _dot(cg, st_g.astype(BF16))
            for pr in range(gw // LANES):
                c0 = lo + pr * LANES
                h0 = c0 // p
                xp = xdb[:, c0:c0 + LANES]
                m0 = (cb * _decay(q, h0)).astype(BF16)
                m1 = (cb * _decay(q, h0 + 1)).astype(BF16)
                yd = _dot(m0, jnp.where(lane < p, xp, 0)) + _dot(m1, jnp.where(lane >= p, xp, 0))
                y_ref[:, c0:c0 + LANES] = (yd + y_off[:, pr * LANES:(pr + 1) * LANES]
                                           + q["d_exp"][:, c0:c0 + LANES] * x[:, c0:c0 + LANES])
            sprev_ref[:, lo:lo + gw] = st_g
            st_ref[:, lo:lo + gw] = q["el_exp"][:, lo:lo + gw] * st_g + _dot(bg, xdf[:, lo:lo + gw], TN)

    tok = lambda w: pl.BlockSpec((l, w), lambda b, c: (b * nc + c, 0))
    const = lambda r, w: pl.BlockSpec((r, w), lambda b, c: (0, 0))
    return pl.pallas_call(
        body, name=name, grid=(nb, nc),
        in_specs=[tok(cc), tok(LANES), pl.BlockSpec((LANES, l), lambda b, c: (0, b * nc + c)),
                  const(8, LANES), const(LANES, 8)],
        out_specs=[tok(di), pl.BlockSpec((None, n, di), lambda b, c: (b * nc + c, 0, 0))],
        out_shape=[jax.ShapeDtypeStruct((nb * nc * l, di), F32), jax.ShapeDtypeStruct((nb * nc, n, di), F32)],
        scratch_shapes=[pltpu.VMEM((n, di), F32), pltpu.VMEM((l, di), F32), pltpu.VMEM((l, di), F32),
                        pltpu.VMEM((8, di), F32)],
        compiler_params=_params("parallel", "arbitrary"),
    )(xc, dtr, dtrt, prow, pcol)


def _ssd_bwd(dy, xc, dtr, dtrt, prow, pcol, e_mat, sprev, nb, nc, di, name):
    l, n, g_n, p = SSM_CHUNK, SSM_STATE, SSM_GROUPS, SSM_HEAD_DIM
    cc = xc.shape[1]
    gw = di // g_n

    def body(dy_ref, xc_ref, dtr_ref, dtrt_ref, prow_ref, pcol_ref, e_ref, sprev_ref,
             dxc_ref, ddtr_ref, sums_ref, dst_ref, off_ref, dxd_ref, last_ref, dla_diag_ref, vst_ref,
             dtx_ref, acsx_ref, dx_ref):
        first = jnp.logical_and(pl.program_id(0) == 0, pl.program_id(1) == 0)

        @pl.when(pl.program_id(1) == 0)
        def _():
            dst_ref[...] = jnp.zeros_like(dst_ref)

        dla_diag_ref[...] = jnp.zeros_like(dla_diag_ref)
        strict_lower = lax.broadcasted_iota(jnp.int32, (l, l), 0) > lax.broadcasted_iota(jnp.int32, (l, l), 1)

        q = _ssd_common(xc_ref, dtr_ref, dtrt_ref, prow_ref, pcol_ref, dtx_ref, acsx_ref, dx_ref, di)
        x = q["x"]
        xd = x * q["dt_exp"]
        xdb = xd.astype(BF16)
        xdf = (xd * q["f_exp"]).astype(BF16)
        dyv = dy_ref[...]
        dyb = dyv.astype(BF16)
        dye = (dyv * q["e_exp"]).astype(BF16)
        upper_b = q["upper_b"]
        lane = lax.broadcasted_iota(jnp.int32, (l, LANES), 1)
        for g in range(g_n):
            lo = g * gw
            bg = xc_ref[:, di + g * n: di + (g + 1) * n]
            cg = xc_ref[:, di + g_n * n + g * n: di + g_n * n + (g + 1) * n]
            cb = _dot(cg, bg, NT)
            st_g = sprev_ref[:, lo:lo + gw]
            st_gb = st_g.astype(BF16)
            dst_g = dst_ref[:, lo:lo + gw]
            dst_gb = dst_g.astype(BF16)
            dye_g = dye[:, lo:lo + gw]
            xdf_g = xdf[:, lo:lo + gw]
            y_off = q["e_exp"][:, lo:lo + gw] * _dot(cg, st_gb)
            dc_g = _dot(dye_g, st_gb, NT)
            db_g = _dot(xdf_g, dst_gb, NT)
            dxd_state = _dot(bg, dst_gb) * q["f_exp"][:, lo:lo + gw]
            last_ref[:, lo:lo + gw] = jnp.sum(dst_g * st_g, axis=0, keepdims=True)
            dst_ref[:, lo:lo + gw] = q["el_exp"][:, lo:lo + gw] * dst_g + _dot(cg, dye_g, TN)
            off_ref[:, lo:lo + gw] = dyv[:, lo:lo + gw] * y_off
            vst_ref[:, lo:lo + gw] = xd[:, lo:lo + gw] * dxd_state
            dcb = jnp.zeros((l, l), F32)
            for pr in range(gw // LANES):
                c0 = lo + pr * LANES
                h0 = c0 // p
                xp = xdb[:, c0:c0 + LANES]
                dyp = dyb[:, c0:c0 + LANES]
                dxd_diag = jnp.zeros((l, LANES), F32)
                for k, keep in enumerate((lane < p, lane >= p)):
                    dec = _decay(q, h0 + k)
                    dy_h = jnp.where(keep, dyp, 0)
                    dm_dec = _dot(dy_h, xp, NT) * dec
                    dcb = dcb + dm_dec
                    dxd_diag = dxd_diag + _dot((cb * dec).astype(BF16), dy_h, TN)
                    above = _dot(upper_b, (dm_dec * cb).astype(BF16))
                    dla_col = jnp.sum(jnp.where(strict_lower, above, 0.0), axis=1, keepdims=True)
                    dla_diag_ref[...] = jnp.where(lane == h0 + k, dla_col, dla_diag_ref[...])
                dxd_ref[:, c0:c0 + LANES] = dxd_diag + dxd_state[:, pr * LANES:(pr + 1) * LANES]
            dcb_b = dcb.astype(BF16)
            dxc_ref[:, di + g * n: di + (g + 1) * n] = (db_g + _dot(dcb_b, cg, TN)).astype(BF16)
            dxc_ref[:, di + g_n * n + g * n: di + g_n * n + (g + 1) * n] = (dc_g + _dot(dcb_b, bg)).astype(BF16)
        dxd = dxd_ref[...]
        e_b = e_ref[...]
        from_y = _exact_left(upper_b, _head_sums(off_ref[...], e_b))
        from_s = _exact_left(strict_lower.astype(BF16), _head_sums(vst_ref[...], e_b))
        carried = _head_sums(jnp.broadcast_to(last_ref[...], (8, di)), e_b)[0:1, :] * jnp.exp(q["acs"][l - 1:l, :])
        dla = from_y + from_s + carried + dla_diag_ref[...]
        ddt = dla * q["a_r"] + _head_sums(dxd * x, e_b)
        ddtr = ddt * jax.nn.sigmoid(q["sp_in"])
        ddtr_ref[...] = ddtr
        dxc_ref[:, 0:di] = (dxd * q["dt_exp"] + q["d_exp"] * dyv).astype(BF16)
        dd_exp = jnp.sum(dyv * x, axis=0, keepdims=True)
        dd = _head_sums(jnp.broadcast_to(dd_exp, (8, di)), e_b)[0:1, :]
        part = _stack_rows([jnp.sum(ddtr, axis=0, keepdims=True),
                            jnp.sum(dla * q["dt"], axis=0, keepdims=True) * q["a_r"], dd], LANES)

        @pl.when(first)
        def _():
            sums_ref[...] = part

        @pl.when(jnp.logical_not(first))
        def _():
            sums_ref[...] += part

    rev = lambda b, c: b * nc + (nc - 1 - c)
    tok = lambda w: pl.BlockSpec((l, w), lambda b, c: (rev(b, c), 0))
    const = lambda r, w: pl.BlockSpec((r, w), lambda b, c: (0, 0))
    return pl.pallas_call(
        body, name=name, grid=(nb, nc),
        in_specs=[tok(di), tok(cc), tok(LANES), pl.BlockSpec((LANES, l), lambda b, c: (0, rev(b, c))),
                  const(8, LANES), const(LANES, 8), const(LANES, di),
                  pl.BlockSpec((None, n, di), lambda b, c: (rev(b, c), 0, 0))],
        out_specs=[tok(cc), tok(LANES), const(8, LANES)],
        out_shape=[jax.ShapeDtypeStruct((nb * nc * l, cc), BF16), jax.ShapeDtypeStruct((nb * nc * l, LANES), F32),
                   jax.ShapeDtypeStruct((8, LANES), F32)],
        scratch_shapes=[pltpu.VMEM((n, di), F32), pltpu.VMEM((l, di), F32), pltpu.VMEM((l, di), F32),
                        pltpu.VMEM((1, di), F32), pltpu.VMEM((l, LANES), F32), pltpu.VMEM((l, di), F32),
                        pltpu.VMEM((l, di), F32), pltpu.VMEM((l, di), F32), pltpu.VMEM((8, di), F32)],
        compiler_params=_params("arbitrary", "arbitrary"),
    )(dy, xc, dtr, dtrt, prow, pcol, e_mat, sprev)


def _gate_norm_fwd(y, proj, z_col0, norm_g, di, name, tt=256):
    t = y.shape[0]
    tt = _tile(t, tt, 8)
    gw = di // SSM_GROUPS
    zw = _tile(math.gcd(di, z_col0), di, LANES)
    nz, zoff = di // zw, z_col0 // zw

    def body(*refs):
        y_ref, z_refs, g_ref, o_ref = refs[0], refs[1:1 + nz], refs[1 + nz], refs[2 + nz]
        for g in range(SSM_GROUPS):
            lo = g * gw
            zv = z_refs[lo // zw][:, lo % zw:lo % zw + gw].astype(F32)
            yg = y_ref[:, lo:lo + gw] * _silu(zv)
            r = lax.rsqrt(jnp.mean(yg * yg, axis=-1, keepdims=True) + EPS)
            o_ref[:, lo:lo + gw] = (yg * r * g_ref[:, lo:lo + gw]).astype(BF16)

    row = pl.BlockSpec((tt, di), lambda i: (i, 0))
    zspecs = [pl.BlockSpec((tt, zw), functools.partial(lambda i, k: (i, zoff + k), k=k)) for k in range(nz)]
    return pl.pallas_call(
        body, name=name, grid=(t // tt,), in_specs=[row] + zspecs + [pl.BlockSpec((1, di), lambda i: (0, 0))],
        out_specs=row, out_shape=jax.ShapeDtypeStruct((t, di), BF16), compiler_params=_params("parallel"),
    )(y, *([proj] * nz), norm_g)


def _gate_norm_bwd(dn, y, proj, z_col0, norm_g, di, name, tt=256):
    t = y.shape[0]
    tt = _tile(t, tt, 8)
    gw = di // SSM_GROUPS
    zw = _tile(math.gcd(di, z_col0), di, LANES)
    nz, zoff = di // zw, z_col0 // zw

    def body(*refs):
        dn_ref, y_ref, z_refs, g_ref = refs[0], refs[1], refs[2:2 + nz], refs[2 + nz]
        dy_ref, dz_ref, dg_ref = refs[3 + nz:]
        first = pl.program_id(0) == 0
        for g in range(SSM_GROUPS):
            lo = g * gw
            zv = z_refs[lo // zw][:, lo % zw:lo % zw + gw].astype(F32)
            yv = y_ref[:, lo:lo + gw]
            sz = _silu(zv)
            yg = yv * sz
            r = lax.rsqrt(jnp.mean(yg * yg, axis=-1, keepdims=True) + EPS)
            yh = yg * r
            dnv = dn_ref[:, lo:lo + gw].astype(F32)
            gy = dnv * g_ref[:, lo:lo + gw]
            dyg = r * (gy - yh * jnp.mean(gy * yh, axis=-1, keepdims=True))
            dy_ref[:, lo:lo + gw] = dyg * sz
            dz_ref[:, lo:lo + gw] = (dyg * yv * _dsilu(zv)).astype(BF16)
            part = jnp.sum(dnv * yh, axis=0, keepdims=True)

            @pl.when(first)
            def _():
                dg_ref[:, lo:lo + gw] = part

            @pl.when(jnp.logical_not(first))
            def _():
                dg_ref[:, lo:lo + gw] += part

    row = pl.BlockSpec((tt, di), lambda i: (i, 0))
    vec = pl.BlockSpec((1, di), lambda i: (0, 0))
    zspecs = [pl.BlockSpec((tt, zw), functools.partial(lambda i, k: (i, zoff + k), k=k)) for k in range(nz)]
    return pl.pallas_call(
        body, name=name, grid=(t // tt,), in_specs=[row, row] + zspecs + [vec], out_specs=[row, row, vec],
        out_shape=[jax.ShapeDtypeStruct((t, di), F32), jax.ShapeDtypeStruct((t, di), BF16), jax.ShapeDtypeStruct((1, di), F32)],
        compiler_params=_params("arbitrary"),
    )(dn, y, *([proj] * nz), norm_g)


def _softmax_rows(s):
    s = s - jnp.max(s, axis=-1, keepdims=True)
    e = jnp.exp(s)
    return e / jnp.sum(e, axis=-1, keepdims=True)


def _xattn_fwd(q, kv, nb, s, m, d, name, tq=512):
    tq = _tile(s, tq, 8)
    nq = s // tq
    hd = d // XATTN_HEADS
    scale = 1.0 / math.sqrt(hd)

    def body(q_ref, k_ref, v_ref, o_ref):
        for h in range(XATTN_HEADS):
            sl = slice(h * hd, (h + 1) * hd)
            prob = _softmax_rows(_dot(q_ref[:, sl], k_ref[:, sl], NT) * scale)
            o_ref[:, sl] = _dot(prob.astype(BF16), v_ref[:, sl]).astype(BF16)

    return pl.pallas_call(
        body, name=name, grid=(nb, nq),
        in_specs=[pl.BlockSpec((tq, d), lambda b, i: (b * nq + i, 0)), pl.BlockSpec((m, d), lambda b, i: (b, 0)),
                  pl.BlockSpec((m, d), lambda b, i: (b, 1))],
        out_specs=pl.BlockSpec((tq, d), lambda b, i: (b * nq + i, 0)),
        out_shape=jax.ShapeDtypeStruct((nb * s, d), BF16), compiler_params=_params("parallel", "parallel"),
    )(q, kv, kv)


def _xattn_bwd(do, q, kv, nb, s, m, d, name, tq=512):
    tq = _tile(s, tq, 8)
    nq = s // tq
    hd = d // XATTN_HEADS
    scale = 1.0 / math.sqrt(hd)

    def body(do_ref, q_ref, k_ref, v_ref, dq_ref, dk_ref, dv_ref):
        first = pl.program_id(1) == 0
        for h in range(XATTN_HEADS):
            sl = slice(h * hd, (h + 1) * hd)
            qh, kh, vh, doh = q_ref[:, sl], k_ref[:, sl], v_ref[:, sl], do_ref[:, sl]
            prob = _softmax_rows(_dot(qh, kh, NT) * scale)
            dv_h = _dot(prob.astype(BF16), doh, TN)
            dp = _dot(doh, vh, NT)
            ds = (prob * (dp - jnp.sum(dp * prob, axis=-1, keepdims=True)) * scale).astype(BF16)
            dq_ref[:, sl] = _dot(ds, kh).astype(BF16)
            dk_h = _dot(ds, qh, TN)

            @pl.when(first)
            def _():
                dk_ref[:, sl] = dk_h
                dv_ref[:, sl] = dv_h

            @pl.when(jnp.logical_not(first))
            def _():
                dk_ref[:, sl] += dk_h
                dv_ref[:, sl] += dv_h

    qspec = pl.BlockSpec((tq, d), lambda b, i: (b * nq + i, 0))
    dq, dk, dv = pl.pallas_call(
        body, name=name, grid=(nb, nq),
        in_specs=[qspec, qspec, pl.BlockSpec((m, d), lambda b, i: (b, 0)), pl.BlockSpec((m, d), lambda b, i: (b, 1))],
        out_specs=[qspec, pl.BlockSpec((m, d), lambda b, i: (b, 0)), pl.BlockSpec((m, d), lambda b, i: (b, 0))],
        out_shape=[jax.ShapeDtypeStruct((nb * s, d), BF16), jax.ShapeDtypeStruct((nb * m, d), F32),
                   jax.ShapeDtypeStruct((nb * m, d), F32)],
        compiler_params=_params("parallel", "arbitrary"),
    )(do, q, kv, kv)
    return dq, dk, dv


def _all_gather(shards, name):
    n_arr = len(shards)

    def body(*refs):
        x_refs, out_refs = refs[:n_arr], refs[n_arr:2 * n_arr]
        send_sems, recv_sems, local_sems = refs[2 * n_arr:]
        x, y, c = lax.axis_index("x"), lax.axis_index("y"), lax.axis_index("c")
        me, sibling = (x, y, c), (x, y, 1 - c)
        chips = [(1 - x, y), (x, 1 - y), (1 - x, 1 - y)]

        def copy(w, k, block, to, from_input=False):
            px, py, pc = block
            rows = out_refs[w].at[4 * px + 2 * py + pc]
            return pltpu.make_async_remote_copy(
                src_ref=x_refs[w] if from_input else rows, dst_ref=rows,
                send_sem=send_sems.at[7 * w + k], recv_sem=recv_sems.at[7 * w + k], device_id=to, device_id_type=MESH)

        started = []
        for w in range(n_arr):
            mine = pltpu.make_async_copy(x_refs[w], out_refs[w].at[4 * x + 2 * y + c], local_sems.at[w])
            mine.start()
            started.append(mine)
        sends = []
        for w in range(n_arr):
            sends.append(copy(w, 0, me, sibling, from_input=True))
            sends += [copy(w, 1 + j, me, (*chip, c), from_input=True) for j, chip in enumerate(chips)]
        for cp in sends:
            cp.start()
        for j, chip in enumerate(chips):
            for w in range(n_arr):
                copy(w, 1 + j, (*chip, c), me).wait_recv()
                passed = copy(w, 4 + j, (*chip, c), sibling)
                passed.start()
                sends.append(passed)
        for w in range(n_arr):
            copy(w, 0, sibling, me).wait_recv()
            for j, chip in enumerate(chips):
                copy(w, 4 + j, (*chip, 1 - c), me).wait_recv()
        for cp in sends:
            cp.wait_send()
        for mine in started:
            mine.wait()

    hbm = pl.BlockSpec(memory_space=pl.ANY)
    return pl.pallas_call(
        body, name=name, out_shape=[jax.ShapeDtypeStruct((N_DEV,) + s.shape, s.dtype) for s in shards],
        in_specs=[hbm] * n_arr, out_specs=[hbm] * n_arr,
        scratch_shapes=[pltpu.SemaphoreType.DMA((7 * n_arr,)), pltpu.SemaphoreType.DMA((7 * n_arr,)),
                        pltpu.SemaphoreType.DMA((n_arr,))],
    )(*shards)


_HBM = pl.BlockSpec(memory_space=pltpu.HBM)
_SEM = pl.BlockSpec(memory_space=pltpu.SEMAPHORE)
_DATAFLOW = pltpu.SideEffectType.DATAFLOW_SIDE_EFFECTING


def _peer_list(x, y, c):
    return [(1 - x if k & 4 else x, 1 - y if k & 2 else y, 1 - c if k & 1 else c) for k in range(1, N_DEV)]


def _push_copy(src_ref, land_ref, send_sems, recv_sems, w, k, peer, me, per_peer_src, receiving):
    px, py, pc = peer
    peer_slot = 4 * px + 2 * py + pc
    return pltpu.make_async_remote_copy(
        src_ref=src_ref.at[peer_slot] if per_peer_src else src_ref,
        dst_ref=land_ref.at[peer_slot if receiving else me],
        send_sem=send_sems.at[7 * w + k], recv_sem=recv_sems.at[7 * w + k], device_id=peer, device_id_type=MESH)


def _push_start(srcs, per_peer_src, after, name):
    n_arr = len(srcs)
    land_shapes = [s.shape if per_peer_src else (N_DEV,) + s.shape for s in srcs]

    def body(*refs):
        src_refs, land_refs = refs[:n_arr], refs[n_arr:2 * n_arr]
        send_sems, recv_sems = refs[2 * n_arr + 1], refs[2 * n_arr + 2]
        token = refs[-1]
        x, y, c = lax.axis_index("x"), lax.axis_index("y"), lax.axis_index("c")
        me = 4 * x + 2 * y + c
        for w in range(n_arr):
            for k, peer in enumerate(_peer_list(x, y, c)):
                _push_copy(src_refs[w], land_refs[w], send_sems, recv_sems, w, k, peer, me, per_peer_src, False).start()
        token[...] = jnp.zeros_like(token)

    lands = [pltpu.with_memory_space_constraint(lax.empty(ls, s.dtype), pltpu.HBM) for ls, s in zip(land_shapes, srcs)]
    srcs_hbm = [pltpu.with_memory_space_constraint(s, pltpu.HBM) for s in srcs]
    out = pl.pallas_call(
        body, name=name,
        out_shape=(pltpu.SemaphoreType.DMA((7 * n_arr,)), pltpu.SemaphoreType.DMA((7 * n_arr,)),
                   *[pltpu.HBM(s.shape, s.dtype) for s in srcs], *[pltpu.HBM(ls, s.dtype) for ls, s in zip(land_shapes, srcs)],
                   jax.ShapeDtypeStruct((8, LANES), F32)),
        in_specs=[_HBM] * (2 * n_arr) + [pl.BlockSpec(memory_space=pl.ANY)],
        out_specs=(_SEM, _SEM, *([_HBM] * (2 * n_arr)), pl.BlockSpec(memory_space=pltpu.VMEM)),
        input_output_aliases={i: 2 + i for i in range(2 * n_arr)},
        compiler_params=pltpu.CompilerParams(has_side_effects=_DATAFLOW),
    )(*srcs_hbm, *lands, after)
    return dict(send=out[0], recv=out[1], srcs=list(out[2:2 + n_arr]), lands=list(out[2 + n_arr:2 + 2 * n_arr]),
                token=out[-1])


def _push_wait(pending, per_peer_src, after, name):
    n_arr = len(pending["srcs"])

    def body(*refs):
        src_refs, land_refs = refs[:n_arr], refs[n_arr:2 * n_arr]
        send_sems, recv_sems = refs[2 * n_arr], refs[2 * n_arr + 1]
        x, y, c = lax.axis_index("x"), lax.axis_index("y"), lax.axis_index("c")
        me = 4 * x + 2 * y + c
        for w in range(n_arr):
            for k, peer in enumerate(_peer_list(x, y, c)):
                cp = _push_copy(src_refs[w], land_refs[w], send_sems, recv_sems, w, k, peer, me, per_peer_src, True)
                cp.wait_send()
                cp.wait_recv()

    out = pl.pallas_call(
        body, name=name,
        out_shape=tuple(pltpu.HBM(a.shape, a.dtype) for a in pending["srcs"] + pending["lands"]),
        in_specs=[_HBM] * (2 * n_arr) + [_SEM, _SEM, pl.BlockSpec(memory_space=pl.ANY)],
        out_specs=tuple([_HBM] * (2 * n_arr)),
        input_output_aliases={i: i for i in range(2 * n_arr)},
        compiler_params=pltpu.CompilerParams(has_side_effects=_DATAFLOW),
    )(*pending["srcs"], *pending["lands"], pending["send"], pending["recv"], after)
    return list(out[:n_arr]), list(out[n_arr:])


def _adamw_math(w, g, m, v):
    m = ADAM_B1 * m + (1.0 - ADAM_B1) * g
    v = ADAM_B2 * v + (1.0 - ADAM_B2) * (g * g)
    m_hat = m / (1.0 - ADAM_B1 ** ADAM_STEP)
    v_hat = v / (1.0 - ADAM_B2 ** ADAM_STEP)
    delta = -ADAM_LR * (m_hat / (jnp.sqrt(v_hat) + ADAM_EPS) + ADAM_WD * w)
    return delta, m, v


def _sum8(parts, name, tr=512):
    _, r, c_dim = parts.shape
    tr = _tile(r, tr, BF16_SUBLANES)

    def body(p_ref, o_ref):
        acc = p_ref[0].astype(F32)
        for k in range(1, N_DEV):
            acc = acc + p_ref[k].astype(F32)
        o_ref[...] = acc

    return pl.pallas_call(
        body, name=name, grid=(r // tr,), in_specs=[pl.BlockSpec((N_DEV, tr, c_dim), lambda i: (0, i, 0))],
        out_specs=pl.BlockSpec((tr, c_dim), lambda i: (i, 0)),
        out_shape=jax.ShapeDtypeStruct((r, c_dim), F32), compiler_params=_params("parallel"),
    )(parts)


def _sum8_adamw(parts, w, m, v, name, tr=128):
    _, r, c_dim = parts.shape
    tr = _tile(r, tr, BF16_SUBLANES)

    def body(p_ref, w_ref, m_ref, v_ref, g_ref, d_ref, nm_ref, nv_ref):
        g = p_ref[0].astype(F32)
        for k in range(1, N_DEV):
            g = g + p_ref[k].astype(F32)
        g_ref[...] = g
        d_ref[...], nm_ref[...], nv_ref[...] = _adamw_math(w_ref[...], g, m_ref[...], v_ref[...])

    row = pl.BlockSpec((tr, c_dim), lambda i: (i, 0))
    out = jax.ShapeDtypeStruct((r, c_dim), F32)
    return pl.pallas_call(
        body, name=name, grid=(r // tr,), in_specs=[pl.BlockSpec((N_DEV, tr, c_dim), lambda i: (0, i, 0)), row, row, row],
        out_specs=[row] * 4, out_shape=[out] * 4, compiler_params=_params("parallel"),
    )(parts, w, m, v)


def _adamw(g, w, m, v, name):
    r, c_dim = g.shape

    def body(g_ref, w_ref, m_ref, v_ref, d_ref, nm_ref, nv_ref):
        d_ref[...], nm_ref[...], nv_ref[...] = _adamw_math(w_ref[...], g_ref[...], m_ref[...], v_ref[...])

    out = jax.ShapeDtypeStruct((r, c_dim), F32)
    return pl.pallas_call(body, name=name, out_shape=[out] * 3)(g, w, m, v)


def _pack_rows(arrays, dtype, row_unit):
    chunks, offs, r0 = [], [], 0
    for a in arrays:
        flat = a.reshape(-1).astype(dtype)
        rows = -(-flat.shape[0] // (LANES * row_unit)) * row_unit
        flat = jnp.pad(flat, (0, rows * LANES - flat.shape[0]))
        chunks.append(flat.reshape(rows, LANES))
        offs.append((r0, rows))
        r0 += rows
    return jnp.concatenate(chunks, axis=0), offs


def _unpack_rows(packed, offs, shapes):
    out = []
    for (r0, rows), shape in zip(offs, shapes):
        n = math.prod(shape)
        blk = packed[..., r0:r0 + rows, :]
        blk = blk.reshape(packed.shape[:-2] + (rows * LANES,))[..., :n]
        out.append(blk.reshape(packed.shape[:-2] + tuple(shape)))
    return out


def _full_from_slots(blk, col_sharded):
    _, r, c = blk.shape
    if col_sharded:
        return blk.transpose(1, 0, 2).reshape(r, N_DEV * c)
    return blk.reshape(N_DEV * r, c)


def _slots_from_full(full, col_sharded):
    r, c = full.shape
    if col_sharded:
        return full.reshape(r, N_DEV, c // N_DEV).transpose(1, 0, 2)
    return full.reshape(N_DEV, r // N_DEV, c)


def _ffn_fwd(h, g, w_gu_t, w_d, tag):
    n = _rms_fwd(h, g, f"{tag}_norm")
    gate, up, a = _ffn_up(n, w_gu_t, f"{tag}_up")
    h_out = _mm(a, w_d, F32, f"{tag}_down", res=h, alpha=FFN_RES_WEIGHT)
    return h_out, (h, n, gate, up, a)


def _ffn_bwd(dh_out, saved, g, w_gu_t, w_d, tag, dep, send_grads):
    h, n, gate, up, a = saved
    dw_d = _mm_tn(a, dh_out, f"{tag}_dw_down", alpha=FFN_RES_WEIGHT, dep=dep)
    dgate, dup = _ffn_da(dh_out, w_d, gate, up, f"{tag}_da", FFN_RES_WEIGHT)
    dw_gu_t = jnp.concatenate([_mm_tn(dgate, n, f"{tag}_dw_gate"), _mm_tn(dup, n, f"{tag}_dw_up")], axis=0)
    dep = send_grads(dw_gu_t, dw_d)
    dn = _mm([dgate, dup], w_gu_t, F32, f"{tag}_dn", dep=dep, tk=1408)
    dh, dg = _rms_bwd(dn, h, g, f"{tag}_dnorm", dres=dh_out)
    return dh, dg


W_GROUPS = (("ffn1_w_gate_up", "ffn1_w_down"),
            ("w_in", "w_out_a", "w_out_ssm", "w_mix_out"),
            ("w_q", "w_kv", "w_o_x", "ffn2_w_gate_up", "ffn2_w_down"))
G_GROUPS = (("ffn2_w_gate_up", "ffn2_w_down"),
            ("w_o_x", "w_q", "w_kv", "w_mix_out", "w_out_a", "w_out_ssm", "w_in"),
            ("ffn1_w_gate_up", "ffn1_w_down"))


def _local_step(x3, mem3, target3, small, comm):
    nb, s, d = x3.shape
    m_len = mem3.shape[1]
    t = nb * s
    nc = s // SSM_CHUNK
    di = small["ssm_norm"].shape[1]
    hs = di // SSM_HEAD_DIM
    cc = di + 2 * SSM_GROUPS * SSM_STATE
    x, mem, target = x3.reshape(t, d), mem3.reshape(nb * m_len, d), target3.reshape(t, d)

    sizes = (d, d, d, di, cc, hs, d, d)
    offs = [0]
    for sz in sizes:
        offs.append(offs[-1] + sz)
    z_col0, xbc_col0 = 3 * d, 3 * d + di
    ga_blk, gb_blk = (3 * d + di + cc) // d, (4 * d + di + cc) // d

    pad_vec = lambda v: jnp.pad(v.reshape(1, -1), ((0, 0), (0, LANES - hs)))
    prow = jnp.concatenate([pad_vec(small["ssm_dt_bias"]), pad_vec(small["ssm_a_log"]), pad_vec(small["ssm_d"]),
                            jnp.zeros((5, LANES), F32)], axis=0)
    pcol = prow.T
    e_mat = (lax.broadcasted_iota(jnp.int32, (LANES, di), 0)
             == lax.broadcasted_iota(jnp.int32, (LANES, di), 1) // SSM_HEAD_DIM).astype(BF16)
    conv_a_w8 = jnp.pad(small["conv_a_w"][0], ((0, 8 - CONV_A_K), (0, 0)))
    ssm_conv_w8 = jnp.pad(small["ssm_conv_w"][0], ((0, 8 - SSM_CONV_K), (0, 0)))

    wts, dep = comm.weights(0, None)
    h1, ffn1_saved = _ffn_fwd(x, small["ffn1_norm"] + dep[0, 0], wts["ffn1_w_gate_up"], wts["ffn1_w_down"], "ffn1")
    got, dep = comm.weights(1, h1)
    wts.update(got)
    w_in_t = wts["w_in"]
    w_main_t = jnp.concatenate([w_in_t[offs[i]:offs[i + 1]] for i in (0, 1, 2, 3, 4, 6, 7)], axis=0)
    w_dt_t = jnp.pad(w_in_t[offs[5]:offs[6]], ((0, LANES - hs), (0, 0)))
    u = _rms_fwd(h1, small["mix_norm"] + dep[0, 0], "mix_norm")
    proj = _mm(u, w_main_t, BF16, "in_proj", nt=True)
    dtr = _mm(u, w_dt_t, F32, "in_proj_dt", nt=True)
    yap = _conv_a_fwd(proj, conv_a_w8, nb, s, d, "conv_a")
    y_a = _mm(yap, wts["w_out_a"], BF16, "out_a")
    xc = _conv_s_fwd(proj, xbc_col0, ssm_conv_w8, small["ssm_conv_b"], nb, s, cc, "conv_s")
    dtrt = dtr.T
    y_ssd, sprev = _ssd_fwd(xc, dtr, dtrt, prow, pcol, nb, nc, di, "ssd")
    ygn = _gate_norm_fwd(y_ssd, proj, z_col0, small["ssm_norm"], di, "gate_norm")
    y_b = _mm(ygn, wts["w_out_ssm"], BF16, "out_ssm")
    merged = _merge_fwd(y_a, y_b, proj, ga_blk, gb_blk, d, "merge")
    h2 = _mm(merged, wts["w_mix_out"], F32, "mix_out", res=h1)
    got, _ = comm.weights(2, h2)
    wts.update(got)
    un = _rms_fwd(h2, small["xattn_norm"], "xattn_norm")
    mn = _rms_fwd(mem, small["mem_norm"], "mem_norm")
    q = _mm(un, wts["w_q"], BF16, "q_proj")
    kv = _mm(mn, wts["w_kv"], BF16, "kv_proj", nt=True)
    o = _xattn_fwd(q, kv, nb, s, m_len, d, "xattn")
    h3 = _mm(o, wts["w_o_x"], F32, "o_proj", res=h2)
    h4, ffn2_saved = _ffn_fwd(h3, small["ffn2_norm"], wts["ffn2_w_gate_up"], wts["ffn2_w_down"], "ffn2")
    loss_vec, dh4, dg_final = _loss_head(h4, small["final_norm"].reshape(1, d), target, "loss_head")

    grads = {"final_norm": dg_final.reshape(d)}
    big = {}
    dh3, grads["ffn2_norm"] = _ffn_bwd(
        dh4, ffn2_saved, small["ffn2_norm"], wts["ffn2_w_gate_up"], wts["ffn2_w_down"], "ffn2", None,
        lambda dw_gu_t, dw_d: comm.grads(0, {"ffn2_w_gate_up": dw_gu_t, "ffn2_w_down": dw_d}))
    big["w_o_x"] = _mm_tn(o, dh3, "dw_o")
    do = _mm(dh3, wts["w_o_x"], BF16, "d_o", nt=True)
    dq, dk, dv = _xattn_bwd(do, q, kv, nb, s, m_len, d, "xattn_bwd")
    big["w_q"] = _mm_tn(un, dq, "dw_q")
    big["w_kv"] = jnp.concatenate([_mm_tn(dk, mn, "dw_k"), _mm_tn(dv, mn, "dw_v")], axis=0)
    dun = _mm(dq, wts["w_q"], F32, "d_un", nt=True)
    dmn = _mm([dk, dv], wts["w_kv"], F32, "d_mn")
    _, grads["mem_norm"] = _rms_bwd(dmn, mem, small["mem_norm"], "mem_dnorm")
    dh2, grads["xattn_norm"] = _rms_bwd(dun, h2, small["xattn_norm"], "xattn_dnorm", dres=dh3)
    big["w_mix_out"] = _mm_tn(merged, dh2, "dw_mix")
    dmerged = _mm(dh2, wts["w_mix_out"], BF16, "d_merged", nt=True)
    dya, dyb, dga, dgb = _merge_bwd(dmerged, y_a, y_b, proj, ga_blk, gb_blk, d, "merge_bwd")
    big["w_out_a"] = _mm_tn(yap, dya, "dw_out_a")
    big["w_out_ssm"] = _mm_tn(ygn, dyb, "dw_out_ssm")
    dyap = _mm(dya, wts["w_out_a"], BF16, "d_yap", nt=True)
    dygn = _mm(dyb, wts["w_out_ssm"], BF16, "d_ygn", nt=True)
    dab, dac, dav, dconv_a = _conv_a_bwd(dyap, proj, conv_a_w8, nb, s, d, "conv_a_bwd")
    dy_ssd, dz, grads["ssm_norm"] = _gate_norm_bwd(dygn, y_ssd, proj, z_col0, small["ssm_norm"], di, "gate_norm_bwd")
    dxc, ddtr, ssd_sums = _ssd_bwd(dy_ssd, xc, dtr, dtrt, prow, pcol, e_mat, sprev, nb, nc, di, "ssd_bwd")
    dxbc, dconv_s, grads["ssm_conv_b"] = _conv_s_bwd(dxc, proj, xbc_col0, ssm_conv_w8, small["ssm_conv_b"], nb, s, cc, "conv_s_bwd")
    dpieces = [("ab", dab), ("ac", dac), ("av", dav), ("z", dz), ("xbc", dxbc), ("ga", dga), ("gb", dgb)]
    dw = {tag: _mm_tn(piece, u, f"dw_in_{tag}") for tag, piece in dpieces}
    dw_dt = _mm_tn(ddtr, u, "dw_in_dt")[:hs]
    du_main = _mm([piece for _, piece in dpieces], w_main_t, F32, "d_u", tk=1024)
    du = _mm(ddtr, w_dt_t, F32, "d_u_dt", res=du_main)
    dh1, grads["mix_norm"] = _rms_bwd(du, h1, small["mix_norm"], "mix_dnorm", dres=dh2)
    big["w_in"] = jnp.concatenate([dw["ab"], dw["ac"], dw["av"], dw["z"], dw["xbc"], dw_dt, dw["ga"], dw["gb"]], axis=0)
    dep = comm.grads(1, big)
    dx, grads["ffn1_norm"] = _ffn_bwd(
        dh1, ffn1_saved, small["ffn1_norm"], wts["ffn1_w_gate_up"], wts["ffn1_w_down"], "ffn1", dep,
        lambda dw_gu_t, dw_d: comm.grads(2, {"ffn1_w_gate_up": dw_gu_t, "ffn1_w_down": dw_d}))

    grads["conv_a_w"] = dconv_a[:CONV_A_K]
    grads["ssm_conv_w"] = dconv_s[:SSM_CONV_K]
    grads["ssm_dt_bias"] = ssd_sums[0:1, :hs]
    grads["ssm_a_log"] = ssd_sums[1:2, :hs]
    grads["ssm_d"] = ssd_sums[2:3, :hs]
    return loss_vec[0, 0], dx.reshape(nb, s, d), grads


def _step(inputs):
    w = {k: inputs[k] for k in WEIGHT_ORDER}
    mom = {k: inputs["m_" + k] for k in WEIGHT_ORDER}
    vel = {k: inputs["v_" + k] for k in WEIGHT_ORDER}
    me = 4 * lax.axis_index("x") + 2 * lax.axis_index("y") + lax.axis_index("c")

    send = {k: (w[k][0].T if k in COL_SHARDED else w[k][0]).astype(BF16) for k in BIG_WEIGHTS}

    def own_slot(land, mine):
        return lax.dynamic_update_slice(land, mine[None], (me, 0, 0))

    gathers = {0: _push_start([send[k] for k in W_GROUPS[0]], False, jnp.zeros((8, LANES), F32), "gather0_start")}
    exchanges = {}

    def weights(i, after):
        pending = gathers[i]
        sent, lands = _push_wait(pending, False, pending["token"] if after is None else after, f"gather{i}_wait")
        full = {k: own_slot(land, mine).reshape(N_DEV * mine.shape[0], mine.shape[1])
                for k, land, mine in zip(W_GROUPS[i], lands, sent)}
        dep = pending["token"]
        if i + 1 < len(W_GROUPS):
            gathers[i + 1] = _push_start([send[k] for k in W_GROUPS[i + 1]], False, lands[0], f"gather{i + 1}_start")
            dep = gathers[i + 1]["token"]
        return full, dep

    def send_grads(i, by_name):
        slots = [by_name[k].reshape((N_DEV,) + send[k].shape) for k in G_GROUPS[i]]
        exchanges[i] = _push_start(slots, True, slots[0], f"exchange{i}_start")
        return exchanges[i]["token"]

    comm = types.SimpleNamespace(weights=weights, grads=send_grads)
    small = {k: w[k] for k in SMALL_REPLICATED}
    conv_shapes = [w[k].shape[1:] for k in SMALL_SHARDED]
    packed_c, conv_offs = _pack_rows([w[k][0] for k in SMALL_SHARDED], F32, 8)
    conv_blocks = _unpack_rows(_all_gather([packed_c], "gather_conv_weights")[0], conv_offs, conv_shapes)
    for k, b in zip(SMALL_SHARDED, conv_blocks):
        small[k] = _full_from_slots(b, True)[None]

    loss_local, grad_x, grads = _local_step(inputs["x"], inputs["mem"], inputs["loss_target"], small, comm)
    loss = lax.psum(loss_local, AXES)

    out = {}
    for i, names in enumerate(G_GROUPS):
        sent, lands = _push_wait(exchanges[i], True, grad_x, f"exchange{i}_wait")
        for k, land, slots in zip(names, lands, sent):
            parts = own_slot(land, lax.dynamic_index_in_dim(slots, me, 0, keepdims=False))
            if k in COL_SHARDED:
                parts = parts.transpose(0, 2, 1)
            out[k] = tuple(o[None] for o in _sum8_adamw(parts, w[k][0], mom[k][0], vel[k][0], f"sum_adamw_{k}"))

    small_names = SMALL_REPLICATED + SMALL_SHARDED
    packed_g, small_offs = _pack_rows([grads[k] for k in small_names], F32, 8)
    total = _sum8(_all_gather([packed_g], "gather_small_grads")[0], "sum_small_grads")
    full_grads = _unpack_rows(total, small_offs, [grads[k].shape for k in small_names])
    mine = {}
    for k, g in zip(small_names, full_grads):
        if k in SMALL_SHARDED:
            c_loc = w[k].shape[2]
            g = lax.dynamic_slice_in_dim(g, me * c_loc, c_loc, axis=1)
        mine[k] = g.reshape(w[k].shape)
    sg, s_offs = _pack_rows([mine[k] for k in small_names], F32, 8)
    sw, _ = _pack_rows([w[k] for k in small_names], F32, 8)
    sm, _ = _pack_rows([mom[k] for k in small_names], F32, 8)
    sv, _ = _pack_rows([vel[k] for k in small_names], F32, 8)
    s_shapes = [w[k].shape for k in small_names]
    small_out = [_unpack_rows(a, s_offs, s_shapes) for a in _adamw(sg, sw, sm, sv, "adamw_small")]
    for i, k in enumerate(small_names):
        out[k] = (mine[k],) + tuple(o[i] for o in small_out)

    res = [loss, grad_x]
    for j in range(4):
        res += [out[k][j] for k in WEIGHT_ORDER]
    return tuple(res)


def kernel(x, mem, ffn1_norm, ffn1_w_gate_up, ffn1_w_down, mix_norm, w_in, conv_a_w, w_out_a, ssm_conv_w, ssm_conv_b, ssm_dt_bias, ssm_a_log, ssm_d, ssm_norm, w_out_ssm, w_mix_out, xattn_norm, mem_norm, w_q, w_kv, w_o_x, ffn2_norm, ffn2_w_gate_up, ffn2_w_down, final_norm, loss_target, m_ffn1_norm, m_ffn1_w_gate_up, m_ffn1_w_down, m_mix_norm, m_w_in, m_conv_a_w, m_w_out_a, m_ssm_conv_w, m_ssm_conv_b, m_ssm_dt_bias, m_ssm_a_log, m_ssm_d, m_ssm_norm, m_w_out_ssm, m_w_mix_out, m_xattn_norm, m_mem_norm, m_w_q, m_w_kv, m_w_o_x, m_ffn2_norm, m_ffn2_w_gate_up, m_ffn2_w_down, m_final_norm, v_ffn1_norm, v_ffn1_w_gate_up, v_ffn1_w_down, v_mix_norm, v_w_in, v_conv_a_w, v_w_out_a, v_ssm_conv_w, v_ssm_conv_b, v_ssm_dt_bias, v_ssm_a_log, v_ssm_d, v_ssm_norm, v_w_out_ssm, v_w_mix_out, v_xattn_norm, v_mem_norm, v_w_q, v_w_kv, v_w_o_x, v_ffn2_norm, v_ffn2_w_gate_up, v_ffn2_w_down, v_final_norm):
    return _step(dict(locals()))
```

```python
import functools
import math
import types

import jax
import jax.numpy as jnp
from jax import lax
from jax.experimental import pallas as pl
from jax.experimental.pallas import tpu as pltpu

F32, BF16 = jnp.float32, jnp.bfloat16
HI = lax.Precision.HIGHEST
MESH = pl.DeviceIdType.MESH
AXES = ("x", "y", "c")
N_DEV = 8

EPS = 1e-6
FFN_RES_WEIGHT = 0.5
SSM_HEAD_DIM = 64
SSM_GROUPS = 4
SSM_STATE = 128
SSM_CHUNK = 128
CONV_A_K = 3
SSM_CONV_K = 4
XATTN_HEADS = 4
ADAM_LR, ADAM_B1, ADAM_B2, ADAM_EPS, ADAM_WD, ADAM_STEP = 1e-3, 0.9, 0.999, 1e-8, 0.01, 10

LANES = 128
BF16_SUBLANES = 16
VMEM_LIMIT_BYTES = 56 * 2 ** 20
NEG_BIG = -1e30

BIG_WEIGHTS = ("ffn1_w_gate_up", "ffn1_w_down", "w_in", "w_out_a", "w_out_ssm", "w_mix_out",
               "w_q", "w_kv", "w_o_x", "ffn2_w_gate_up", "ffn2_w_down")
COL_SHARDED = ("ffn1_w_gate_up", "w_in", "w_kv", "ffn2_w_gate_up")
SMALL_REPLICATED = ("ffn1_norm", "mix_norm", "ssm_conv_b", "ssm_dt_bias", "ssm_a_log", "ssm_d", "ssm_norm",
                    "xattn_norm", "mem_norm", "ffn2_norm", "final_norm")
SMALL_SHARDED = ("conv_a_w", "ssm_conv_w")
WEIGHT_ORDER = ("ffn1_norm", "ffn1_w_gate_up", "ffn1_w_down", "mix_norm", "w_in", "conv_a_w", "w_out_a",
                "ssm_conv_w", "ssm_conv_b", "ssm_dt_bias", "ssm_a_log", "ssm_d", "ssm_norm", "w_out_ssm",
                "w_mix_out", "xattn_norm", "mem_norm", "w_q", "w_kv", "w_o_x", "ffn2_norm", "ffn2_w_gate_up",
                "ffn2_w_down", "final_norm")


def _tile(dim, pref, unit):
    best = None
    t = unit
    while t <= min(dim, pref):
        if dim % t == 0:
            best = t
        t += unit
    return best if best is not None else dim


def _params(*sem):
    return pltpu.CompilerParams(dimension_semantics=sem, vmem_limit_bytes=VMEM_LIMIT_BYTES)


def _silu(x):
    return x * jax.nn.sigmoid(x)


def _dsilu(x):
    s = jax.nn.sigmoid(x)
    return s * (1.0 + x * (1.0 - s))


def _softplus(x):
    return jnp.maximum(x, 0.0) + jnp.log(1.0 + jnp.exp(-jnp.abs(x)))


def _dot(a, b, dims=(((1,), (0,)), ((), ())), precision=None):
    return lax.dot_general(a, b, dims, preferred_element_type=F32, precision=precision)


def _stack_rows(rows, width):
    r_idx = lax.broadcasted_iota(jnp.int32, (8, width), 0)
    acc = jnp.zeros((8, width), F32)
    for k, row in enumerate(rows):
        acc = jnp.where(r_idx == k, row, acc)
    return acc


NT = (((1,), (1,)), ((), ()))
TN = (((0,), (0,)), ((), ()))


def _mm(a, b, out_dtype, name, res=None, alpha=1.0, nt=False, dep=None, tm=1024, tn=2048, tk=2816):
    pieces = list(a) if isinstance(a, (list, tuple)) else [a]
    m = pieces[0].shape[0]
    k = sum(p.shape[1] for p in pieces)
    n = b.shape[0] if nt else b.shape[1]
    assert (b.shape[1] if nt else b.shape[0]) == k
    tm, tn = _tile(m, tm, 8), _tile(n, tn, LANES)
    tk = _tile(math.gcd(*[p.shape[1] for p in pieces]), tk, LANES)
    nk = k // tk
    starts, s0 = [], 0
    for p in pieces:
        starts.append((s0, p.shape[1] // tk))
        s0 += p.shape[1] // tk
    n_p = len(pieces)

    def body(*refs):
        a_refs, b_ref = refs[:n_p], refs[n_p]
        r_ref = refs[n_p + 1] if res is not None else None
        n_in = n_p + 1 + (res is not None) + (dep is not None)
        o_ref = refs[n_in]
        scr = refs[n_in + 1:]

        def finish(acc):
            acc = alpha * acc if alpha != 1.0 else acc
            if r_ref is not None:
                acc = r_ref[...] + acc
            o_ref[...] = acc.astype(out_dtype)

        def product(a_ref):
            return _dot(a_ref[...].astype(BF16), b_ref[...].astype(BF16), NT if nt else (((1,), (0,)), ((), ())))

        if nk == 1:
            finish(product(a_refs[0]))
            return
        acc_ref = scr[0]
        kk = pl.program_id(2)
        for (s, cnt), a_ref in zip(starts, a_refs):
            if s == 0:
                @pl.when(kk == 0)
                def _():
                    acc_ref[...] = product(a_ref)

                @pl.when(jnp.logical_and(kk > 0, kk < cnt))
                def _():
                    acc_ref[...] += product(a_ref)
            else:
                @pl.when(jnp.logical_and(kk >= s, kk < s + cnt))
                def _():
                    acc_ref[...] += product(a_ref)

        @pl.when(kk == nk - 1)
        def _():
            finish(acc_ref[...])

    def a_spec(s, cnt):
        return pl.BlockSpec((tm, tk), lambda i, j, kk: (i, jnp.clip(kk - s, 0, cnt - 1)))

    in_specs = [a_spec(s, cnt) for s, cnt in starts]
    in_specs.append(pl.BlockSpec((tn, tk), lambda i, j, kk: (j, kk)) if nt else pl.BlockSpec((tk, tn), lambda i, j, kk: (kk, j)))
    args = pieces + [b]
    if res is not None:
        in_specs.append(pl.BlockSpec((tm, tn), lambda i, j, kk: (i, j)))
        args.append(res)
    if dep is not None:
        in_specs.append(pl.BlockSpec((8, LANES), lambda i, j, kk: (0, 0)))
        args.append(dep)
    return pl.pallas_call(
        body, name=name, grid=(m // tm, n // tn, nk), in_specs=in_specs,
        out_specs=pl.BlockSpec((tm, tn), lambda i, j, kk: (i, j)),
        out_shape=jax.ShapeDtypeStruct((m, n), out_dtype),
        scratch_shapes=[pltpu.VMEM((tm, tn), F32)] if nk > 1 else [],
        compiler_params=_params("parallel", "parallel", "arbitrary"),
    )(*args)


def _mm_tn(x, dy, name, out_dtype=BF16, alpha=1.0, dep=None, tko=1408, tn=1024, tt=1024):
    t, k = x.shape
    n = dy.shape[1]
    tko, tn, tt = _tile(k, tko, LANES), _tile(n, tn, LANES), _tile(t, tt, 8)
    nt_steps = t // tt

    def body(*refs):
        x_ref, dy_ref = refs[:2]
        o_ref, acc_ref = refs[-2:]
        part = _dot(x_ref[...].astype(BF16), dy_ref[...].astype(BF16), TN)
        step = pl.program_id(2)

        @pl.when(step == 0)
        def _():
            acc_ref[...] = part

        @pl.when(step > 0)
        def _():
            acc_ref[...] += part

        @pl.when(step == nt_steps - 1)
        def _():
            acc = acc_ref[...]
            o_ref[...] = (alpha * acc if alpha != 1.0 else acc).astype(out_dtype)

    in_specs = [pl.BlockSpec((tt, tko), lambda i, j, s: (s, i)), pl.BlockSpec((tt, tn), lambda i, j, s: (s, j))]
    args = [x, dy]
    if dep is not None:
        in_specs.append(pl.BlockSpec((8, LANES), lambda i, j, s: (0, 0)))
        args.append(dep)
    return pl.pallas_call(
        body, name=name, grid=(k // tko, n // tn, nt_steps), in_specs=in_specs,
        out_specs=pl.BlockSpec((tko, tn), lambda i, j, s: (i, j)),
        out_shape=jax.ShapeDtypeStruct((k, n), out_dtype),
        scratch_shapes=[pltpu.VMEM((tko, tn), F32)],
        compiler_params=_params("parallel", "parallel", "arbitrary"),
    )(*args)


def _rms_fwd(x, g, name, tt=512):
    t, d = x.shape
    tt = _tile(t, tt, 8)

    def body(x_ref, g_ref, o_ref):
        xv = x_ref[...]
        r = lax.rsqrt(jnp.mean(xv * xv, axis=-1, keepdims=True) + EPS)
        o_ref[...] = (xv * r * g_ref[...]).astype(BF16)

    return pl.pallas_call(
        body, name=name, grid=(t // tt,),
        in_specs=[pl.BlockSpec((tt, d), lambda i: (i, 0)), pl.BlockSpec((1, d), lambda i: (0, 0))],
        out_specs=pl.BlockSpec((tt, d), lambda i: (i, 0)),
        out_shape=jax.ShapeDtypeStruct((t, d), BF16), compiler_params=_params("parallel"),
    )(x, g)


def _rms_bwd(dn, x, g, name, dres=None, tt=512):
    t, d = x.shape
    tt = _tile(t, tt, 8)

    def body(*refs):
        if dres is None:
            dn_ref, x_ref, g_ref, dx_ref, dg_ref = refs
            r_ref = None
        else:
            dn_ref, x_ref, g_ref, r_ref, dx_ref, dg_ref = refs
        xv, dnv = x_ref[...], dn_ref[...].astype(F32)
        r = lax.rsqrt(jnp.mean(xv * xv, axis=-1, keepdims=True) + EPS)
        xh = xv * r
        gy = dnv * g_ref[...]
        dx = r * (gy - xh * jnp.mean(gy * xh, axis=-1, keepdims=True))
        if r_ref is not None:
            dx = dx + r_ref[...]
        dx_ref[...] = dx
        part = jnp.sum(dnv * xh, axis=0, keepdims=True)

        @pl.when(pl.program_id(0) == 0)
        def _():
            dg_ref[...] = part

        @pl.when(pl.program_id(0) > 0)
        def _():
            dg_ref[...] += part

    row = pl.BlockSpec((tt, d), lambda i: (i, 0))
    vec = pl.BlockSpec((1, d), lambda i: (0, 0))
    in_specs, args = [row, row, vec], [dn, x, g]
    if dres is not None:
        in_specs.append(row)
        args.append(dres)
    return pl.pallas_call(
        body, name=name, grid=(t // tt,), in_specs=in_specs, out_specs=[row, vec],
        out_shape=[jax.ShapeDtypeStruct((t, d), F32), jax.ShapeDtypeStruct((1, d), F32)],
        compiler_params=_params("arbitrary"),
    )(*args)


def _ffn_up(n, w_gu_t, name, tm=1024, tf=1408):
    t, d = n.shape
    f = w_gu_t.shape[0] // 2
    tm, tf = _tile(t, tm, 8), _tile(f, tf, LANES)
    nf = f // tf

    def body(n_ref, wg_ref, wu_ref, g_ref, u_ref, a_ref):
        nv = n_ref[...]
        gate, up = _dot(nv, wg_ref[...], NT), _dot(nv, wu_ref[...], NT)
        g_ref[...] = gate.astype(BF16)
        u_ref[...] = up.astype(BF16)
        a_ref[...] = (_silu(gate) * up).astype(BF16)

    blk = pl.BlockSpec((tm, tf), lambda i, j: (i, j))
    out = jax.ShapeDtypeStruct((t, f), BF16)
    return pl.pallas_call(
        body, name=name, grid=(t // tm, nf),
        in_specs=[pl.BlockSpec((tm, d), lambda i, j: (i, 0)), pl.BlockSpec((tf, d), lambda i, j: (j, 0)),
                  pl.BlockSpec((tf, d), lambda i, j: (j + nf, 0))],
        out_specs=[blk, blk, blk], out_shape=[out, out, out], compiler_params=_params("parallel", "parallel"),
    )(n, w_gu_t, w_gu_t)


def _ffn_da(dh, w_d, gate, up, name, alpha, dep=None, tm=1024, tf=1408):
    t, d = dh.shape
    f = w_d.shape[0]
    tm, tf = _tile(t, tm, 8), _tile(f, tf, LANES)

    def body(*refs):
        dh_ref, w_ref, g_ref, u_ref = refs[:4]
        dg_ref, du_ref = refs[-2:]
        da = alpha * _dot(dh_ref[...].astype(BF16), w_ref[...], NT)
        gv = g_ref[...].astype(F32)
        dg_ref[...] = (da * u_ref[...].astype(F32) * _dsilu(gv)).astype(BF16)
        du_ref[...] = (da * _silu(gv)).astype(BF16)

    blk = pl.BlockSpec((tm, tf), lambda i, j: (i, j))
    in_specs = [pl.BlockSpec((tm, d), lambda i, j: (i, 0)), pl.BlockSpec((tf, d), lambda i, j: (j, 0)), blk, blk]
    args = [dh, w_d, gate, up]
    if dep is not None:
        in_specs.append(pl.BlockSpec((8, LANES), lambda i, j: (0, 0)))
        args.append(dep)
    out = jax.ShapeDtypeStruct((t, f), BF16)
    return pl.pallas_call(
        body, name=name, grid=(t // tm, f // tf), in_specs=in_specs, out_specs=[blk, blk], out_shape=[out, out],
        compiler_params=_params("parallel", "parallel"),
    )(*args)


def _merge_fwd(ya, yb, proj, ga_blk, gb_blk, d, name, tt=512):
    t = ya.shape[0]
    tt = _tile(t, tt, 8)

    def body(ya_ref, yb_ref, ga_ref, gb_ref, o_ref):
        o_ref[...] = (jax.nn.sigmoid(ga_ref[...].astype(F32)) * ya_ref[...].astype(F32)
                      + jax.nn.sigmoid(gb_ref[...].astype(F32)) * yb_ref[...].astype(F32)).astype(BF16)

    row = pl.BlockSpec((tt, d), lambda i: (i, 0))
    return pl.pallas_call(
        body, name=name, grid=(t // tt,),
        in_specs=[row, row, pl.BlockSpec((tt, d), lambda i: (i, ga_blk)), pl.BlockSpec((tt, d), lambda i: (i, gb_blk))],
        out_specs=row, out_shape=jax.ShapeDtypeStruct((t, d), BF16), compiler_params=_params("parallel"),
    )(ya, yb, proj, proj)


def _merge_bwd(dm, ya, yb, proj, ga_blk, gb_blk, d, name, tt=512):
    t = ya.shape[0]
    tt = _tile(t, tt, 8)

    def body(dm_ref, ya_ref, yb_ref, ga_ref, gb_ref, dya_ref, dyb_ref, dga_ref, dgb_ref):
        dmv = dm_ref[...].astype(F32)
        sa, sb = jax.nn.sigmoid(ga_ref[...].astype(F32)), jax.nn.sigmoid(gb_ref[...].astype(F32))
        dya_ref[...] = (dmv * sa).astype(BF16)
        dyb_ref[...] = (dmv * sb).astype(BF16)
        dga_ref[...] = (dmv * ya_ref[...].astype(F32) * sa * (1.0 - sa)).astype(BF16)
        dgb_ref[...] = (dmv * yb_ref[...].astype(F32) * sb * (1.0 - sb)).astype(BF16)

    row = pl.BlockSpec((tt, d), lambda i: (i, 0))
    out = jax.ShapeDtypeStruct((t, d), BF16)
    return pl.pallas_call(
        body, name=name, grid=(t // tt,),
        in_specs=[row, row, row, pl.BlockSpec((tt, d), lambda i: (i, ga_blk)), pl.BlockSpec((tt, d), lambda i: (i, gb_blk))],
        out_specs=[row] * 4, out_shape=[out] * 4, compiler_params=_params("parallel"),
    )(dm, ya, yb, proj, proj)


def _loss_head(h, g, target, name, tt=512):
    t, d = h.shape
    tt = _tile(t, tt, 8)

    def body(h_ref, g_ref, tg_ref, loss_ref, dh_ref, dg_ref):
        xv = h_ref[...]
        r = lax.rsqrt(jnp.mean(xv * xv, axis=-1, keepdims=True) + EPS)
        xh = xv * r
        err = xh * g_ref[...] - tg_ref[...]
        dout = err * (1.0 / d)
        gy = dout * g_ref[...]
        dh_ref[...] = r * (gy - xh * jnp.mean(gy * xh, axis=-1, keepdims=True))
        dg_part = jnp.sum(dout * xh, axis=0, keepdims=True)
        loss_part = jnp.full((1, LANES), 0.5 / d, F32) * jnp.sum(err * err)

        @pl.when(pl.program_id(0) == 0)
        def _():
            dg_ref[...] = dg_part
            loss_ref[...] = loss_part

        @pl.when(pl.program_id(0) > 0)
        def _():
            dg_ref[...] += dg_part
            loss_ref[...] += loss_part

    row = pl.BlockSpec((tt, d), lambda i: (i, 0))
    vec = pl.BlockSpec((1, d), lambda i: (0, 0))
    return pl.pallas_call(
        body, name=name, grid=(t // tt,), in_specs=[row, vec, row],
        out_specs=[pl.BlockSpec((1, LANES), lambda i: (0, 0)), row, vec],
        out_shape=[jax.ShapeDtypeStruct((1, LANES), F32), jax.ShapeDtypeStruct((t, d), F32), jax.ShapeDtypeStruct((1, d), F32)],
        compiler_params=_params("arbitrary"),
    )(h, g, target)


def _shift_down(x, k, t_idx):
    if k == 0:
        return x
    return jnp.where(t_idx >= k, pltpu.roll(x, k, 0), 0.0)


def _shift_up(x, k, t_idx, s):
    if k == 0:
        return x
    return jnp.where(t_idx < s - k, pltpu.roll(x, s - k, 0), 0.0)


def _conv_a_fwd(proj, w, nb, s, d, name, cb=256):
    cb = _tile(d, cb, LANES)
    nd = d // cb

    def body(b_ref, c_ref, v_ref, w_ref, o_ref):
        t_idx = lax.broadcasted_iota(jnp.int32, (s, cb), 0)
        cv = c_ref[...].astype(F32) * v_ref[...].astype(F32)
        cc = sum(w_ref[k:k + 1, :] * _shift_down(cv, CONV_A_K - 1 - k, t_idx) for k in range(CONV_A_K))
        o_ref[...] = (b_ref[...].astype(F32) * cc).astype(BF16)

    def col(off):
        return pl.BlockSpec((s, cb), lambda b, j: (b, j + off * nd))

    return pl.pallas_call(
        body, name=name, grid=(nb, nd), in_specs=[col(0), col(1), col(2), pl.BlockSpec((8, cb), lambda b, j: (0, j))],
        out_specs=pl.BlockSpec((s, cb), lambda b, j: (b, j)),
        out_shape=jax.ShapeDtypeStruct((nb * s, d), BF16), compiler_params=_params("parallel", "parallel"),
    )(proj, proj, proj, w)


def _conv_a_bwd(dy, proj, w, nb, s, d, name, cb=256):
    cb = _tile(d, cb, LANES)
    nd = d // cb

    def body(dy_ref, b_ref, c_ref, v_ref, w_ref, db_ref, dc_ref, dv_ref, dw_ref):
        t_idx = lax.broadcasted_iota(jnp.int32, (s, cb), 0)
        cv_c, cv_v = c_ref[...].astype(F32), v_ref[...].astype(F32)
        cv = cv_c * cv_v
        shifted = [_shift_down(cv, CONV_A_K - 1 - k, t_idx) for k in range(CONV_A_K)]
        cc = sum(w_ref[k:k + 1, :] * shifted[k] for k in range(CONV_A_K))
        dyv = dy_ref[...].astype(F32)
        db_ref[...] = (dyv * cc).astype(BF16)
        dcc = dyv * b_ref[...].astype(F32)
        dcv = sum(w_ref[k:k + 1, :] * _shift_up(dcc, CONV_A_K - 1 - k, t_idx, s) for k in range(CONV_A_K))
        dc_ref[...] = (dcv * cv_v).astype(BF16)
        dv_ref[...] = (dcv * cv_c).astype(BF16)
        rows = [jnp.sum(dcc * shifted[k], axis=0, keepdims=True) for k in range(CONV_A_K)]
        part = _stack_rows(rows, cb)

        @pl.when(pl.program_id(1) == 0)
        def _():
            dw_ref[...] = part

        @pl.when(pl.program_id(1) > 0)
        def _():
            dw_ref[...] += part

    def col(off):
        return pl.BlockSpec((s, cb), lambda j, b: (b, j + off * nd))

    own = pl.BlockSpec((s, cb), lambda j, b: (b, j))
    wspec = pl.BlockSpec((8, cb), lambda j, b: (0, j))
    out = jax.ShapeDtypeStruct((nb * s, d), BF16)
    return pl.pallas_call(
        body, name=name, grid=(nd, nb), in_specs=[own, col(0), col(1), col(2), wspec],
        out_specs=[own, own, own, wspec], out_shape=[out, out, out, jax.ShapeDtypeStruct((8, d), F32)],
        compiler_params=_params("parallel", "arbitrary"),
    )(dy, proj, proj, proj, w)


def _conv_s_fwd(proj, col0, w, bias, nb, s, cc_width, name, cb=256):
    cb = _tile(math.gcd(cc_width, col0) if col0 else cc_width, cb, LANES)
    nd, off = cc_width // cb, col0 // cb

    def body(x_ref, w_ref, b_ref, o_ref):
        t_idx = lax.broadcasted_iota(jnp.int32, (s, cb), 0)
        xv = x_ref[...].astype(F32)
        pre = b_ref[...] + sum(w_ref[k:k + 1, :] * _shift_down(xv, SSM_CONV_K - 1 - k, t_idx) for k in range(SSM_CONV_K))
        o_ref[...] = _silu(pre).astype(BF16)

    vec = pl.BlockSpec((8, cb), lambda b, j: (0, j))
    return pl.pallas_call(
        body, name=name, grid=(nb, nd),
        in_specs=[pl.BlockSpec((s, cb), lambda b, j: (b, j + off)), vec, pl.BlockSpec((1, cb), lambda b, j: (0, j))],
        out_specs=pl.BlockSpec((s, cb), lambda b, j: (b, j)),
        out_shape=jax.ShapeDtypeStruct((nb * s, cc_width), BF16), compiler_params=_params("parallel", "parallel"),
    )(proj, w, bias)


def _conv_s_bwd(dxc, proj, col0, w, bias, nb, s, cc_width, name, cb=256):
    cb = _tile(math.gcd(cc_width, col0) if col0 else cc_width, cb, LANES)
    nd, off = cc_width // cb, col0 // cb

    def body(d_ref, x_ref, w_ref, b_ref, dx_ref, dw_ref, db_ref):
        t_idx = lax.broadcasted_iota(jnp.int32, (s, cb), 0)
        xv = x_ref[...].astype(F32)
        shifted = [_shift_down(xv, SSM_CONV_K - 1 - k, t_idx) for k in range(SSM_CONV_K)]
        pre = b_ref[...] + sum(w_ref[k:k + 1, :] * shifted[k] for k in range(SSM_CONV_K))
        dpre = d_ref[...].astype(F32) * _dsilu(pre)
        dx = sum(w_ref[k:k + 1, :] * _shift_up(dpre, SSM_CONV_K - 1 - k, t_idx, s) for k in range(SSM_CONV_K))
        dx_ref[...] = dx.astype(BF16)
        rows = [jnp.sum(dpre * shifted[k], axis=0, keepdims=True) for k in range(SSM_CONV_K)]
        dw_part = _stack_rows(rows, cb)
        db_part = jnp.sum(dpre, axis=0, keepdims=True)

        @pl.when(pl.program_id(1) == 0)
        def _():
            dw_ref[...] = dw_part
            db_ref[...] = db_part

        @pl.when(pl.program_id(1) > 0)
        def _():
            dw_ref[...] += dw_part
            db_ref[...] += db_part

    own = pl.BlockSpec((s, cb), lambda j, b: (b, j))
    wspec = pl.BlockSpec((8, cb), lambda j, b: (0, j))
    bspec = pl.BlockSpec((1, cb), lambda j, b: (0, j))
    return pl.pallas_call(
        body, name=name, grid=(nd, nb),
        in_specs=[own, pl.BlockSpec((s, cb), lambda j, b: (b, j + off)), wspec, bspec],
        out_specs=[own, wspec, bspec],
        out_shape=[jax.ShapeDtypeStruct((nb * s, cc_width), BF16), jax.ShapeDtypeStruct((8, cc_width), F32),
                   jax.ShapeDtypeStruct((1, cc_width), F32)],
        compiler_params=_params("parallel", "arbitrary"),
    )(dxc, proj, w, bias)


def _split3(v):
    hi = v.astype(BF16)
    r1 = v - hi.astype(F32)
    mid = r1.astype(BF16)
    return hi, mid, (r1 - mid.astype(F32)).astype(BF16)


def _exact_left(mask_b, v):
    return sum(_dot(mask_b, t) for t in _split3(v))


def _exact_right(v, mask_b):
    return sum(_dot(t, mask_b) for t in _split3(v))


def _head_sums(v, e_b):
    return _dot(v.astype(BF16), e_b, NT)


def _spread(v, out_ref, di):
    lane = lax.broadcasted_iota(jnp.int32, (v.shape[0], LANES), 1)
    for pr in range(di // LANES):
        h0 = pr * (LANES // SSM_HEAD_DIM)
        out_ref[:, pr * LANES:(pr + 1) * LANES] = jnp.where(lane < SSM_HEAD_DIM, v[:, h0:h0 + 1], v[:, h0 + 1:h0 + 2])


def _ssd_common(xc_ref, dtr_ref, dtrt_ref, prow_ref, pcol_ref, dtx_ref, acsx_ref, dx_ref, di):
    l = SSM_CHUNK
    bias_r, a_r = prow_ref[0:1, :], -jnp.exp(prow_ref[1:2, :])
    sp_in = dtr_ref[...] + bias_r
    dt = _softplus(sp_in)
    li = lax.broadcasted_iota(jnp.int32, (l, l), 0)
    si = lax.broadcasted_iota(jnp.int32, (l, l), 1)
    lower_b = (li >= si).astype(BF16)
    upper_b = (li <= si).astype(BF16)
    acs = _exact_left(lower_b, dt * a_r)
    bias_c, a_c = pcol_ref[:, 0:1], -jnp.exp(pcol_ref[:, 1:2])
    dt_t = _softplus(dtrt_ref[...] + bias_c)
    acs_t = _exact_right(dt_t * a_c, upper_b)
    _spread(dt, dtx_ref, di)
    _spread(acs, acsx_ref, di)
    _spread(prow_ref[0:8, :], dx_ref, di)
    acs_exp = acsx_ref[...]
    acs_last = acs_exp[l - 1:l, :]
    x = xc_ref[:, 0:di].astype(F32)
    return dict(dt=dt, a_r=a_r, sp_in=sp_in, acs=acs, acs_t=acs_t, dt_exp=dtx_ref[...], e_exp=jnp.exp(acs_exp),
                el_exp=jnp.exp(acs_last), f_exp=jnp.exp(acs_last - acs_exp), x=x, mask=li >= si, upper_b=upper_b,
                d_exp=dx_ref[2:3, :])


def _decay(q, h):
    seg = q["acs"][:, h:h + 1] - q["acs_t"][h:h + 1, :]
    return jnp.exp(jnp.where(q["mask"], seg, NEG_BIG))


def _ssd_fwd(xc, dtr, dtrt, prow, pcol, nb, nc, di, name):
    l, n, g_n, p = SSM_CHUNK, SSM_STATE, SSM_GROUPS, SSM_HEAD_DIM
    cc = xc.shape[1]
    gw = di // g_n
    assert p * 2 == LANES and gw % LANES == 0

    def body(xc_ref, dtr_ref, dtrt_ref, prow_ref, pcol_ref, y_ref, sprev_ref, st_ref, dtx_ref, acsx_ref, dx_ref):
        @pl.when(pl.program_id(1) == 0)
        def _():
            st_ref[...] = jnp.zeros_like(st_ref)

        q = _ssd_common(xc_ref, dtr_ref, dtrt_ref, prow_ref, pcol_ref, dtx_ref, acsx_ref, dx_ref, di)
        x = q["x"]
        xd = x * q["dt_exp"]
        xdb = xd.astype(BF16)
        xdf = (xd * q["f_exp"]).astype(BF16)
        lane = lax.broadcasted_iota(jnp.int32, (l, LANES), 1)
        for g in range(g_n):
            lo = g * gw
            bg = xc_ref[:, di + g * n: di + (g + 1) * n]
            cg = xc_ref[:, di + g_n * n + g * n: di + g_n * n + (g + 1) * n]
            cb = _dot(cg, bg, NT)
            st_g = st_ref[:, lo:lo + gw]
            y_off = q["e_exp"][:, lo:lo + gw] * _dot(cg, st_g.astype(BF16))
            for pr in range(gw // LANES):
                c0 = lo + pr * LANES
                h0 = c0 // p
                xp = xdb[:, c0:c0 + LANES]
                m0 = (cb * _decay(q, h0)).astype(BF16)
                m1 = (cb * _decay(q, h0 + 1)).astype(BF16)
                yd = _dot(m0, jnp.where(lane < p, xp, 0)) + _dot(m1, jnp.where(lane >= p, xp, 0))
                y_ref[:, c0:c0 + LANES] = (yd + y_off[:, pr * LANES:(pr + 1) * LANES]
                                           + q["d_exp"][:, c0:c0 + LANES] * x[:, c0:c0 + LANES])
            sprev_ref[:, lo:lo + gw] = st_g
            st_ref[:, lo:lo + gw] = q["el_exp"][:, lo:lo + gw] * st_g + _dot(bg, xdf[:, lo:lo + gw], TN)

    tok = lambda w: pl.BlockSpec((l, w), lambda b, c: (b * nc + c, 0))
    const = lambda r, w: pl.BlockSpec((r, w), lambda b, c: (0, 0))
    return pl.pallas_call(
        body, name=name, grid=(nb, nc),
        in_specs=[tok(cc), tok(LANES), pl.BlockSpec((LANES, l), lambda b, c: (0, b * nc + c)),
                  const(8, LANES), const(LANES, 8)],
        out_specs=[tok(di), pl.BlockSpec((None, n, di), lambda b, c: (b * nc + c, 0, 0))],
        out_shape=[jax.ShapeDtypeStruct((nb * nc * l, di), F32), jax.ShapeDtypeStruct((nb * nc, n, di), F32)],
        scratch_shapes=[pltpu.VMEM((n, di), F32), pltpu.VMEM((l, di), F32), pltpu.VMEM((l, di), F32),
                        pltpu.VMEM((8, di), F32)],
        compiler_params=_params("parallel", "arbitrary"),
    )(xc, dtr, dtrt, prow, pcol)


def _ssd_bwd(dy, xc, dtr, dtrt, prow, pcol, e_mat, sprev, nb, nc, di, name):
    l, n, g_n, p = SSM_CHUNK, SSM_STATE, SSM_GROUPS, SSM_HEAD_DIM
    cc = xc.shape[1]
    gw = di // g_n

    def body(dy_ref, xc_ref, dtr_ref, dtrt_ref, prow_ref, pcol_ref, e_ref, sprev_ref,
             dxc_ref, ddtr_ref, sums_ref, dst_ref, off_ref, dxd_ref, last_ref, dla_diag_ref, vst_ref,
             dtx_ref, acsx_ref, dx_ref):
        first = jnp.logical_and(pl.program_id(0) == 0, pl.program_id(1) == 0)

        @pl.when(pl.program_id(1) == 0)
        def _():
            dst_ref[...] = jnp.zeros_like(dst_ref)

        dla_diag_ref[...] = jnp.zeros_like(dla_diag_ref)
        strict_lower = lax.broadcasted_iota(jnp.int32, (l, l), 0) > lax.broadcasted_iota(jnp.int32, (l, l), 1)

        q = _ssd_common(xc_ref, dtr_ref, dtrt_ref, prow_ref, pcol_ref, dtx_ref, acsx_ref, dx_ref, di)
        x = q["x"]
        xd = x * q["dt_exp"]
        xdb = xd.astype(BF16)
        xdf = (xd * q["f_exp"]).astype(BF16)
        dyv = dy_ref[...]
        dyb = dyv.astype(BF16)
        dye = (dyv * q["e_exp"]).astype(BF16)
        upper_b = q["upper_b"]
        lane = lax.broadcasted_iota(jnp.int32, (l, LANES), 1)
        for g in range(g_n):
            lo = g * gw
            bg = xc_ref[:, di + g * n: di + (g + 1) * n]
            cg = xc_ref[:, di + g_n * n + g * n: di + g_n * n + (g + 1) * n]
            cb = _dot(cg, bg, NT)
            st_g = sprev_ref[:, lo:lo + gw]
            st_gb = st_g.astype(BF16)
            dst_g = dst_ref[:, lo:lo + gw]
            dst_gb = dst_g.astype(BF16)
            dye_g = dye[:, lo:lo + gw]
            xdf_g = xdf[:, lo:lo + gw]
            y_off = q["e_exp"][:, lo:lo + gw] * _dot(cg, st_gb)
            dc_g = _dot(dye_g, st_gb, NT)
            db_g = _dot(xdf_g, dst_gb, NT)
            dxd_state = _dot(bg, dst_gb) * q["f_exp"][:, lo:lo + gw]
            last_ref[:, lo:lo + gw] = jnp.sum(dst_g * st_g, axis=0, keepdims=True)
            dst_ref[:, lo:lo + gw] = q["el_exp"][:, lo:lo + gw] * dst_g + _dot(cg, dye_g, TN)
            off_ref[:, lo:lo + gw] = dyv[:, lo:lo + gw] * y_off
            vst_ref[:, lo:lo + gw] = xd[:, lo:lo + gw] * dxd_state
            dcb = jnp.zeros((l, l), F32)
            for pr in range(gw // LANES):
                c0 = lo + pr * LANES
                h0 = c0 // p
                xp = xdb[:, c0:c0 + LANES]
                dyp = dyb[:, c0:c0 + LANES]
                dxd_diag = jnp.zeros((l, LANES), F32)
                for k, keep in enumerate((lane < p, lane >= p)):
                    dec = _decay(q, h0 + k)
                    dy_h = jnp.where(keep, dyp, 0)
                    dm_dec = _dot(dy_h, xp, NT) * dec
                    dcb = dcb + dm_dec
                    dxd_diag = dxd_diag + _dot((cb * dec).astype(BF16), dy_h, TN)
                    above = _dot(upper_b, (dm_dec * cb).astype(BF16))
                    dla_col = jnp.sum(jnp.where(strict_lower, above, 0.0), axis=1, keepdims=True)
                    dla_diag_ref[...] = jnp.where(lane == h0 + k, dla_col, dla_diag_ref[...])
                dxd_ref[:, c0:c0 + LANES] = dxd_diag + dxd_state[:, pr * LANES:(pr + 1) * LANES]
            dcb_b = dcb.astype(BF16)
            dxc_ref[:, di + g * n: di + (g + 1) * n] = (db_g + _dot(dcb_b, cg, TN)).astype(BF16)
            dxc_ref[:, di + g_n * n + g * n: di + g_n * n + (g + 1) * n] = (dc_g + _dot(dcb_b, bg)).astype(BF16)
        dxd = dxd_ref[...]
        e_b = e_ref[...]
        from_y = _exact_left(upper_b, _head_sums(off_ref[...], e_b))
        from_s = _exact_left(strict_lower.astype(BF16), _head_sums(vst_ref[...], e_b))
        carried = _head_sums(jnp.broadcast_to(last_ref[...], (8, di)), e_b)[0:1, :] * jnp.exp(q["acs"][l - 1:l, :])
        dla = from_y + from_s + carried + dla_diag_ref[...]
        ddt = dla * q["a_r"] + _head_sums(dxd * x, e_b)
        ddtr = ddt * jax.nn.sigmoid(q["sp_in"])
        ddtr_ref[...] = ddtr
        dxc_ref[:, 0:di] = (dxd * q["dt_exp"] + q["d_exp"] * dyv).astype(BF16)
        dd_exp = jnp.sum(dyv * x, axis=0, keepdims=True)
        dd = _head_sums(jnp.broadcast_to(dd_exp, (8, di)), e_b)[0:1, :]
        part = _stack_rows([jnp.sum(ddtr, axis=0, keepdims=True),
                            jnp.sum(dla * q["dt"], axis=0, keepdims=True) * q["a_r"], dd], LANES)

        @pl.when(first)
        def _():
            sums_ref[...] = part

        @pl.when(jnp.logical_not(first))
        def _():
            sums_ref[...] += part

    rev = lambda b, c: b * nc + (nc - 1 - c)
    tok = lambda w: pl.BlockSpec((l, w), lambda b, c: (rev(b, c), 0))
    const = lambda r, w: pl.BlockSpec((r, w), lambda b, c: (0, 0))
    return pl.pallas_call(
        body, name=name, grid=(nb, nc),
        in_specs=[tok(di), tok(cc), tok(LANES), pl.BlockSpec((LANES, l), lambda b, c: (0, rev(b, c))),
                  const(8, LANES), const(LANES, 8), const(LANES, di),
                  pl.BlockSpec((None, n, di), lambda b, c: (rev(b, c), 0, 0))],
        out_specs=[tok(cc), tok(LANES), const(8, LANES)],
        out_shape=[jax.ShapeDtypeStruct((nb * nc * l, cc), BF16), jax.ShapeDtypeStruct((nb * nc * l, LANES), F32),
                   jax.ShapeDtypeStruct((8, LANES), F32)],
        scratch_shapes=[pltpu.VMEM((n, di), F32), pltpu.VMEM((l, di), F32), pltpu.VMEM((l, di), F32),
                        pltpu.VMEM((1, di), F32), pltpu.VMEM((l, LANES), F32), pltpu.VMEM((l, di), F32),
                        pltpu.VMEM((l, di), F32), pltpu.VMEM((l, di), F32), pltpu.VMEM((8, di), F32)],
        compiler_params=_params("arbitrary", "arbitrary"),
    )(dy, xc, dtr, dtrt, prow, pcol, e_mat, sprev)


def _gate_norm_fwd(y, proj, z_col0, norm_g, di, name, tt=256):
    t = y.shape[0]
    tt = _tile(t, tt, 8)
    gw = di // SSM_GROUPS
    zw = _tile(math.gcd(di, z_col0), di, LANES)
    nz, zoff = di // zw, z_col0 // zw

    def body(*refs):
        y_ref, z_refs, g_ref, o_ref = refs[0], refs[1:1 + nz], refs[1 + nz], refs[2 + nz]
        for g in range(SSM_GROUPS):
            lo = g * gw
            zv = z_refs[lo // zw][:, lo % zw:lo % zw + gw].astype(F32)
            yg = y_ref[:, lo:lo + gw] * _silu(zv)
            r = lax.rsqrt(jnp.mean(yg * yg, axis=-1, keepdims=True) + EPS)
            o_ref[:, lo:lo + gw] = (yg * r * g_ref[:, lo:lo + gw]).astype(BF16)

    row = pl.BlockSpec((tt, di), lambda i: (i, 0))
    zspecs = [pl.BlockSpec((tt, zw), functools.partial(lambda i, k: (i, zoff + k), k=k)) for k in range(nz)]
    return pl.pallas_call(
        body, name=name, grid=(t // tt,), in_specs=[row] + zspecs + [pl.BlockSpec((1, di), lambda i: (0, 0))],
        out_specs=row, out_shape=jax.ShapeDtypeStruct((t, di), BF16), compiler_params=_params("parallel"),
    )(y, *([proj] * nz), norm_g)


def _gate_norm_bwd(dn, y, proj, z_col0, norm_g, di, name, tt=256):
    t = y.shape[0]
    tt = _tile(t, tt, 8)
    gw = di // SSM_GROUPS
    zw = _tile(math.gcd(di, z_col0), di, LANES)
    nz, zoff = di // zw, z_col0 // zw

    def body(*refs):
        dn_ref, y_ref, z_refs, g_ref = refs[0], refs[1], refs[2:2 + nz], refs[2 + nz]
        dy_ref, dz_ref, dg_ref = refs[3 + nz:]
        first = pl.program_id(0) == 0
        for g in range(SSM_GROUPS):
            lo = g * gw
            zv = z_refs[lo // zw][:, lo % zw:lo % zw + gw].astype(F32)
            yv = y_ref[:, lo:lo + gw]
            sz = _silu(zv)
            yg = yv * sz
            r = lax.rsqrt(jnp.mean(yg * yg, axis=-1, keepdims=True) + EPS)
            yh = yg * r
            dnv = dn_ref[:, lo:lo + gw].astype(F32)
            gy = dnv * g_ref[:, lo:lo + gw]
            dyg = r * (gy - yh * jnp.mean(gy * yh, axis=-1, keepdims=True))
            dy_ref[:, lo:lo + gw] = dyg * sz
            dz_ref[:, lo:lo + gw] = (dyg * yv * _dsilu(zv)).astype(BF16)
            part = jnp.sum(dnv * yh, axis=0, keepdims=True)

            @pl.when(first)
            def _():
                dg_ref[:, lo:lo + gw] = part

            @pl.when(jnp.logical_not(first))
            def _():
                dg_ref[:, lo:lo + gw] += part

    row = pl.BlockSpec((tt, di), lambda i: (i, 0))
    vec = pl.BlockSpec((1, di), lambda i: (0, 0))
    zspecs = [pl.BlockSpec((tt, zw), functools.partial(lambda i, k: (i, zoff + k), k=k)) for k in range(nz)]
    return pl.pallas_call(
        body, name=name, grid=(t // tt,), in_specs=[row, row] + zspecs + [vec], out_specs=[row, row, vec],
        out_shape=[jax.ShapeDtypeStruct((t, di), F32), jax.ShapeDtypeStruct((t, di), BF16), jax.ShapeDtypeStruct((1, di), F32)],
        compiler_params=_params("arbitrary"),
    )(dn, y, *([proj] * nz), norm_g)


def _softmax_rows(s):
    s = s - jnp.max(s, axis=-1, keepdims=True)
    e = jnp.exp(s)
    return e / jnp.sum(e, axis=-1, keepdims=True)


def _xattn_fwd(q, kv, nb, s, m, d, name, tq=512):
    tq = _tile(s, tq, 8)
    nq = s // tq
    hd = d // XATTN_HEADS
    scale = 1.0 / math.sqrt(hd)

    def body(q_ref, k_ref, v_ref, o_ref):
        for h in range(XATTN_HEADS):
            sl = slice(h * hd, (h + 1) * hd)
            prob = _softmax_rows(_dot(q_ref[:, sl], k_ref[:, sl], NT) * scale)
            o_ref[:, sl] = _dot(prob.astype(BF16), v_ref[:, sl]).astype(BF16)

    return pl.pallas_call(
        body, name=name, grid=(nb, nq),
        in_specs=[pl.BlockSpec((tq, d), lambda b, i: (b * nq + i, 0)), pl.BlockSpec((m, d), lambda b, i: (b, 0)),
                  pl.BlockSpec((m, d), lambda b, i: (b, 1))],
        out_specs=pl.BlockSpec((tq, d), lambda b, i: (b * nq + i, 0)),
        out_shape=jax.ShapeDtypeStruct((nb * s, d), BF16), compiler_params=_params("parallel", "parallel"),
    )(q, kv, kv)


def _xattn_bwd(do, q, kv, nb, s, m, d, name, tq=512):
    tq = _tile(s, tq, 8)
    nq = s // tq
    hd = d // XATTN_HEADS
    scale = 1.0 / math.sqrt(hd)

    def body(do_ref, q_ref, k_ref, v_ref, dq_ref, dk_ref, dv_ref):
        first = pl.program_id(1) == 0
        for h in range(XATTN_HEADS):
            sl = slice(h * hd, (h + 1) * hd)
            qh, kh, vh, doh = q_ref[:, sl], k_ref[:, sl], v_ref[:, sl], do_ref[:, sl]
            prob = _softmax_rows(_dot(qh, kh, NT) * scale)
            dv_h = _dot(prob.astype(BF16), doh, TN)
            dp = _dot(doh, vh, NT)
            ds = (prob * (dp - jnp.sum(dp * prob, axis=-1, keepdims=True)) * scale).astype(BF16)
            dq_ref[:, sl] = _dot(ds, kh).astype(BF16)
            dk_h = _dot(ds, qh, TN)

            @pl.when(first)
            def _():
                dk_ref[:, sl] = dk_h
                dv_ref[:, sl] = dv_h

            @pl.when(jnp.logical_not(first))
            def _():
                dk_ref[:, sl] += dk_h
                dv_ref[:, sl] += dv_h

    qspec = pl.BlockSpec((tq, d), lambda b, i: (b * nq + i, 0))
    dq, dk, dv = pl.pallas_call(
        body, name=name, grid=(nb, nq),
        in_specs=[qspec, qspec, pl.BlockSpec((m, d), lambda b, i: (b, 0)), pl.BlockSpec((m, d), lambda b, i: (b, 1))],
        out_specs=[qspec, pl.BlockSpec((m, d), lambda b, i: (b, 0)), pl.BlockSpec((m, d), lambda b, i: (b, 0))],
        out_shape=[jax.ShapeDtypeStruct((nb * s, d), BF16), jax.ShapeDtypeStruct((nb * m, d), F32),
                   jax.ShapeDtypeStruct((nb * m, d), F32)],
        compiler_params=_params("parallel", "arbitrary"),
    )(do, q, kv, kv)
    return dq, dk, dv


def _all_gather(shards, name):
    n_arr = len(shards)

    def body(*refs):
        x_refs, out_refs = refs[:n_arr], refs[n_arr:2 * n_arr]
        send_sems, recv_sems, local_sems = refs[2 * n_arr:]
        x, y, c = lax.axis_index("x"), lax.axis_index("y"), lax.axis_index("c")
        me, sibling = (x, y, c), (x, y, 1 - c)
        chips = [(1 - x, y), (x, 1 - y), (1 - x, 1 - y)]

        def copy(w, k, block, to, from_input=False):
            px, py, pc = block
            rows = out_refs[w].at[4 * px + 2 * py + pc]
            return pltpu.make_async_remote_copy(
                src_ref=x_refs[w] if from_input else rows, dst_ref=rows,
                send_sem=send_sems.at[7 * w + k], recv_sem=recv_sems.at[7 * w + k], device_id=to, device_id_type=MESH)

        started = []
        for w in range(n_arr):
            mine = pltpu.make_async_copy(x_refs[w], out_refs[w].at[4 * x + 2 * y + c], local_sems.at[w])
            mine.start()
            started.append(mine)
        sends = []
        for w in range(n_arr):
            sends.append(copy(w, 0, me, sibling, from_input=True))
            sends += [copy(w, 1 + j, me, (*chip, c), from_input=True) for j, chip in enumerate(chips)]
        for cp in sends:
            cp.start()
        for j, chip in enumerate(chips):
            for w in range(n_arr):
                copy(w, 1 + j, (*chip, c), me).wait_recv()
                passed = copy(w, 4 + j, (*chip, c), sibling)
                passed.start()
                sends.append(passed)
        for w in range(n_arr):
            copy(w, 0, sibling, me).wait_recv()
            for j, chip in enumerate(chips):
                copy(w, 4 + j, (*chip, 1 - c), me).wait_recv()
        for cp in sends:
            cp.wait_send()
        for mine in started:
            mine.wait()

    hbm = pl.BlockSpec(memory_space=pl.ANY)
    return pl.pallas_call(
        body, name=name, out_shape=[jax.ShapeDtypeStruct((N_DEV,) + s.shape, s.dtype) for s in shards],
        in_specs=[hbm] * n_arr, out_specs=[hbm] * n_arr,
        scratch_shapes=[pltpu.SemaphoreType.DMA((7 * n_arr,)), pltpu.SemaphoreType.DMA((7 * n_arr,)),
                        pltpu.SemaphoreType.DMA((n_arr,))],
    )(*shards)


_HBM = pl.BlockSpec(memory_space=pltpu.HBM)
_SEM = pl.BlockSpec(memory_space=pltpu.SEMAPHORE)
_DATAFLOW = pltpu.SideEffectType.DATAFLOW_SIDE_EFFECTING


def _peer_list(x, y, c):
    return [(1 - x if k & 4 else x, 1 - y if k & 2 else y, 1 - c if k & 1 else c) for k in range(1, N_DEV)]


def _push_copy(src_ref, land_ref, send_sems, recv_sems, w, k, peer, me, per_peer_src, receiving):
    px, py, pc = peer
    peer_slot = 4 * px + 2 * py + pc
    return pltpu.make_async_remote_copy(
        src_ref=src_ref.at[peer_slot] if per_peer_src else src_ref,
        dst_ref=land_ref.at[peer_slot if receiving else me],
        send_sem=send_sems.at[7 * w + k], recv_sem=recv_sems.at[7 * w + k], device_id=peer, device_id_type=MESH)


def _push_start(srcs, per_peer_src, after, name):
    n_arr = len(srcs)
    land_shapes = [s.shape if per_peer_src else (N_DEV,) + s.shape for s in srcs]

    def body(*refs):
        src_refs, land_refs = refs[:n_arr], refs[n_arr:2 * n_arr]
        send_sems, recv_sems = refs[2 * n_arr + 1], refs[2 * n_arr + 2]
        token = refs[-1]
        x, y, c = lax.axis_index("x"), lax.axis_index("y"), lax.axis_index("c")
        me = 4 * x + 2 * y + c
        for w in range(n_arr):
            for k, peer in enumerate(_peer_list(x, y, c)):
                _push_copy(src_refs[w], land_refs[w], send_sems, recv_sems, w, k, peer, me, per_peer_src, False).start()
        token[...] = jnp.zeros_like(token)

    lands = [pltpu.with_memory_space_constraint(lax.empty(ls, s.dtype), pltpu.HBM) for ls, s in zip(land_shapes, srcs)]
    srcs_hbm = [pltpu.with_memory_space_constraint(s, pltpu.HBM) for s in srcs]
    out = pl.pallas_call(
        body, name=name,
        out_shape=(pltpu.SemaphoreType.DMA((7 * n_arr,)), pltpu.SemaphoreType.DMA((7 * n_arr,)),
                   *[pltpu.HBM(s.shape, s.dtype) for s in srcs], *[pltpu.HBM(ls, s.dtype) for ls, s in zip(land_shapes, srcs)],
                   jax.ShapeDtypeStruct((8, LANES), F32)),
        in_specs=[_HBM] * (2 * n_arr) + [pl.BlockSpec(memory_space=pl.ANY)],
        out_specs=(_SEM, _SEM, *([_HBM] * (2 * n_arr)), pl.BlockSpec(memory_space=pltpu.VMEM)),
        input_output_aliases={i: 2 + i for i in range(2 * n_arr)},
        compiler_params=pltpu.CompilerParams(has_side_effects=_DATAFLOW),
    )(*srcs_hbm, *lands, after)
    return dict(send=out[0], recv=out[1], srcs=list(out[2:2 + n_arr]), lands=list(out[2 + n_arr:2 + 2 * n_arr]),
                token=out[-1])


def _push_wait(pending, per_peer_src, after, name):
    n_arr = len(pending["srcs"])

    def body(*refs):
        src_refs, land_refs = refs[:n_arr], refs[n_arr:2 * n_arr]
        send_sems, recv_sems = refs[2 * n_arr], refs[2 * n_arr + 1]
        x, y, c = lax.axis_index("x"), lax.axis_index("y"), lax.axis_index("c")
        me = 4 * x + 2 * y + c
        for w in range(n_arr):
            for k, peer in enumerate(_peer_list(x, y, c)):
                cp = _push_copy(src_refs[w], land_refs[w], send_sems, recv_sems, w, k, peer, me, per_peer_src, True)
                cp.wait_send()
                cp.wait_recv()

    out = pl.pallas_call(
        body, name=name,
        out_shape=tuple(pltpu.HBM(a.shape, a.dtype) for a in pending["srcs"] + pending["lands"]),
        in_specs=[_HBM] * (2 * n_arr) + [_SEM, _SEM, pl.BlockSpec(memory_space=pl.ANY)],
        out_specs=tuple([_HBM] * (2 * n_arr)),
        input_output_aliases={i: i for i in range(2 * n_arr)},
        compiler_params=pltpu.CompilerParams(has_side_effects=_DATAFLOW),
    )(*pending["srcs"], *pending["lands"], pending["send"], pending["recv"], after)
    return list(out[:n_arr]), list(out[n_arr:])


def _adamw_math(w, g, m, v):
    m = ADAM_B1 * m + (1.0 - ADAM_B1) * g
    v = ADAM_B2 * v + (1.0 - ADAM_B2) * (g * g)
    m_hat = m / (1.0 - ADAM_B1 ** ADAM_STEP)
    v_hat = v / (1.0 - ADAM_B2 ** ADAM_STEP)
    delta = -ADAM_LR * (m_hat / (jnp.sqrt(v_hat) + ADAM_EPS) + ADAM_WD * w)
    return delta, m, v


def _sum8(parts, name, tr=512):
    _, r, c_dim = parts.shape
    tr = _tile(r, tr, BF16_SUBLANES)

    def body(p_ref, o_ref):
        acc = p_ref[0].astype(F32)
        for k in range(1, N_DEV):
            acc = acc + p_ref[k].astype(F32)
        o_ref[...] = acc

    return pl.pallas_call(
        body, name=name, grid=(r // tr,), in_specs=[pl.BlockSpec((N_DEV, tr, c_dim), lambda i: (0, i, 0))],
        out_specs=pl.BlockSpec((tr, c_dim), lambda i: (i, 0)),
        out_shape=jax.ShapeDtypeStruct((r, c_dim), F32), compiler_params=_params("parallel"),
    )(parts)


def _sum8_adamw(parts, w, m, v, name, tr=128):
    _, r, c_dim = parts.shape
    tr = _tile(r, tr, BF16_SUBLANES)

    def body(p_ref, w_ref, m_ref, v_ref, g_ref, d_ref, nm_ref, nv_ref):
        g = p_ref[0].astype(F32)
        for k in range(1, N_DEV):
            g = g + p_ref[k].astype(F32)
        g_ref[...] = g
        d_ref[...], nm_ref[...], nv_ref[...] = _adamw_math(w_ref[...], g, m_ref[...], v_ref[...])

    row = pl.BlockSpec((None, tr, c_dim), lambda i: (0, i, 0))
    out = jax.ShapeDtypeStruct((1, r, c_dim), F32)
    return pl.pallas_call(
        body, name=name, grid=(r // tr,), in_specs=[pl.BlockSpec((N_DEV, tr, c_dim), lambda i: (0, i, 0)), row, row, row],
        out_specs=[row] * 4, out_shape=[out] * 4, compiler_params=_params("parallel"),
    )(parts, w, m, v)


def _adamw(g, w, m, v, name):
    r, c_dim = g.shape

    def body(g_ref, w_ref, m_ref, v_ref, d_ref, nm_ref, nv_ref):
        d_ref[...], nm_ref[...], nv_ref[...] = _adamw_math(w_ref[...], g_ref[...], m_ref[...], v_ref[...])

    out = jax.ShapeDtypeStruct((r, c_dim), F32)
    return pl.pallas_call(body, name=name, out_shape=[out] * 3)(g, w, m, v)


def _pack_rows(arrays, dtype, row_unit):
    chunks, offs, r0 = [], [], 0
    for a in arrays:
        flat = a.reshape(-1).astype(dtype)
        rows = -(-flat.shape[0] // (LANES * row_unit)) * row_unit
        flat = jnp.pad(flat, (0, rows * LANES - flat.shape[0]))
        chunks.append(flat.reshape(rows, LANES))
        offs.append((r0, rows))
        r0 += rows
    return jnp.concatenate(chunks, axis=0), offs


def _unpack_rows(packed, offs, shapes):
    out = []
    for (r0, rows), shape in zip(offs, shapes):
        n = math.prod(shape)
        blk = packed[..., r0:r0 + rows, :]
        blk = blk.reshape(packed.shape[:-2] + (rows * LANES,))[..., :n]
        out.append(blk.reshape(packed.shape[:-2] + tuple(shape)))
    return out


def _full_from_slots(blk, col_sharded):
    _, r, c = blk.shape
    if col_sharded:
        return blk.transpose(1, 0, 2).reshape(r, N_DEV * c)
    return blk.reshape(N_DEV * r, c)


def _slots_from_full(full, col_sharded):
    r, c = full.shape
    if col_sharded:
        return full.reshape(r, N_DEV, c // N_DEV).transpose(1, 0, 2)
    return full.reshape(N_DEV, r // N_DEV, c)


def _ffn_fwd(h, g, w_gu_t, w_d, tag):
    n = _rms_fwd(h, g, f"{tag}_norm")
    gate, up, a = _ffn_up(n, w_gu_t, f"{tag}_up")
    h_out = _mm(a, w_d, F32, f"{tag}_down", res=h, alpha=FFN_RES_WEIGHT)
    return h_out, (h, n, gate, up, a)


def _ffn_bwd(dh_out, saved, g, w_gu_t, w_d, tag, dep, send_grads):
    h, n, gate, up, a = saved
    dw_d = _mm_tn(a, dh_out, f"{tag}_dw_down", alpha=FFN_RES_WEIGHT, dep=dep)
    dgate, dup = _ffn_da(dh_out, w_d, gate, up, f"{tag}_da", FFN_RES_WEIGHT)
    dw_gu_t = jnp.concatenate([_mm_tn(dgate, n, f"{tag}_dw_gate"), _mm_tn(dup, n, f"{tag}_dw_up")], axis=0)
    dep = send_grads(dw_gu_t, dw_d)
    dn = _mm([dgate, dup], w_gu_t, F32, f"{tag}_dn", dep=dep, tk=1408)
    dh, dg = _rms_bwd(dn, h, g, f"{tag}_dnorm", dres=dh_out)
    return dh, dg


W_GROUPS = (("ffn1_w_gate_up", "ffn1_w_down"),
            ("w_in",),
            ("w_out_a", "w_out_ssm", "w_mix_out"),
            ("w_q", "w_kv", "w_o_x", "ffn2_w_gate_up", "ffn2_w_down"))
G_GROUPS = (("ffn2_w_gate_up", "ffn2_w_down"),
            ("w_o_x", "w_q", "w_kv", "w_mix_out", "w_out_a", "w_out_ssm", "w_in"),
            ("ffn1_w_gate_up", "ffn1_w_down"))


def _local_step(x3, mem3, target3, small, comm):
    nb, s, d = x3.shape
    m_len = mem3.shape[1]
    t = nb * s
    nc = s // SSM_CHUNK
    di = small["ssm_norm"].shape[1]
    hs = di // SSM_HEAD_DIM
    cc = di + 2 * SSM_GROUPS * SSM_STATE
    x, mem, target = x3.reshape(t, d), mem3.reshape(nb * m_len, d), target3.reshape(t, d)

    sizes = (d, d, d, di, cc, hs, d, d)
    offs = [0]
    for sz in sizes:
        offs.append(offs[-1] + sz)
    z_col0, xbc_col0 = 3 * d, 3 * d + di
    ga_blk, gb_blk = (3 * d + di + cc) // d, (4 * d + di + cc) // d

    pad_vec = lambda v: jnp.pad(v.reshape(1, -1), ((0, 0), (0, LANES - hs)))
    prow = jnp.concatenate([pad_vec(small["ssm_dt_bias"]), pad_vec(small["ssm_a_log"]), pad_vec(small["ssm_d"]),
                            jnp.zeros((5, LANES), F32)], axis=0)
    pcol = prow.T
    e_mat = (lax.broadcasted_iota(jnp.int32, (LANES, di), 0)
             == lax.broadcasted_iota(jnp.int32, (LANES, di), 1) // SSM_HEAD_DIM).astype(BF16)
    conv_a_w8 = jnp.pad(small["conv_a_w"][0], ((0, 8 - CONV_A_K), (0, 0)))
    ssm_conv_w8 = jnp.pad(small["ssm_conv_w"][0], ((0, 8 - SSM_CONV_K), (0, 0)))

    wts, dep = comm.weights(0, None)
    h1, ffn1_saved = _ffn_fwd(x, small["ffn1_norm"] + dep[0, 0], wts["ffn1_w_gate_up"], wts["ffn1_w_down"], "ffn1")
    got, dep = comm.weights(1, h1)
    wts.update(got)
    w_in_t = wts["w_in"]
    w_main_t = jnp.concatenate([w_in_t[offs[i]:offs[i + 1]] for i in (0, 1, 2, 3, 4, 6, 7)], axis=0)
    w_dt_t = jnp.pad(w_in_t[offs[5]:offs[6]], ((0, LANES - hs), (0, 0)))
    u = _rms_fwd(h1, small["mix_norm"] + dep[0, 0], "mix_norm")
    proj = _mm(u, w_main_t, BF16, "in_proj", nt=True)
    dtr = _mm(u, w_dt_t, F32, "in_proj_dt", nt=True)
    yap = _conv_a_fwd(proj, conv_a_w8, nb, s, d, "conv_a")
    got, dep = comm.weights(2, yap)
    wts.update(got)
    y_a = _mm(yap, wts["w_out_a"], BF16, "out_a", dep=dep)
    xc = _conv_s_fwd(proj, xbc_col0, ssm_conv_w8, small["ssm_conv_b"], nb, s, cc, "conv_s")
    dtrt = dtr.T
    y_ssd, sprev = _ssd_fwd(xc, dtr, dtrt, prow, pcol, nb, nc, di, "ssd")
    ygn = _gate_norm_fwd(y_ssd, proj, z_col0, small["ssm_norm"], di, "gate_norm")
    y_b = _mm(ygn, wts["w_out_ssm"], BF16, "out_ssm")
    merged = _merge_fwd(y_a, y_b, proj, ga_blk, gb_blk, d, "merge")
    h2 = _mm(merged, wts["w_mix_out"], F32, "mix_out", res=h1)
    got, _ = comm.weights(3, h2)
    wts.update(got)
    un = _rms_fwd(h2, small["xattn_norm"], "xattn_norm")
    mn = _rms_fwd(mem, small["mem_norm"], "mem_norm")
    q = _mm(un, wts["w_q"], BF16, "q_proj")
    kv = _mm(mn, wts["w_kv"], BF16, "kv_proj", nt=True)
    o = _xattn_fwd(q, kv, nb, s, m_len, d, "xattn")
    h3 = _mm(o, wts["w_o_x"], F32, "o_proj", res=h2)
    h4, ffn2_saved = _ffn_fwd(h3, small["ffn2_norm"], wts["ffn2_w_gate_up"], wts["ffn2_w_down"], "ffn2")
    loss_vec, dh4, dg_final = _loss_head(h4, small["final_norm"].reshape(1, d), target, "loss_head")

    grads = {"final_norm": dg_final.reshape(d)}
    big = {}
    dh3, grads["ffn2_norm"] = _ffn_bwd(
        dh4, ffn2_saved, small["ffn2_norm"], wts["ffn2_w_gate_up"], wts["ffn2_w_down"], "ffn2", None,
        lambda dw_gu_t, dw_d: comm.grads(0, {"ffn2_w_gate_up": dw_gu_t, "ffn2_w_down": dw_d}))
    big["w_o_x"] = _mm_tn(o, dh3, "dw_o")
    do = _mm(dh3, wts["w_o_x"], BF16, "d_o", nt=True)
    dq, dk, dv = _xattn_bwd(do, q, kv, nb, s, m_len, d, "xattn_bwd")
    big["w_q"] = _mm_tn(un, dq, "dw_q")
    big["w_kv"] = jnp.concatenate([_mm_tn(dk, mn, "dw_k"), _mm_tn(dv, mn, "dw_v")], axis=0)
    dun = _mm(dq, wts["w_q"], F32, "d_un", nt=True)
    dmn = _mm([dk, dv], wts["w_kv"], F32, "d_mn")
    _, grads["mem_norm"] = _rms_bwd(dmn, mem, small["mem_norm"], "mem_dnorm")
    dh2, grads["xattn_norm"] = _rms_bwd(dun, h2, small["xattn_norm"], "xattn_dnorm", dres=dh3)
    big["w_mix_out"] = _mm_tn(merged, dh2, "dw_mix")
    dmerged = _mm(dh2, wts["w_mix_out"], BF16, "d_merged", nt=True)
    dya, dyb, dga, dgb = _merge_bwd(dmerged, y_a, y_b, proj, ga_blk, gb_blk, d, "merge_bwd")
    big["w_out_a"] = _mm_tn(yap, dya, "dw_out_a")
    big["w_out_ssm"] = _mm_tn(ygn, dyb, "dw_out_ssm")
    dyap = _mm(dya, wts["w_out_a"], BF16, "d_yap", nt=True)
    dygn = _mm(dyb, wts["w_out_ssm"], BF16, "d_ygn", nt=True)
    dab, dac, dav, dconv_a = _conv_a_bwd(dyap, proj, conv_a_w8, nb, s, d, "conv_a_bwd")
    dy_ssd, dz, grads["ssm_norm"] = _gate_norm_bwd(dygn, y_ssd, proj, z_col0, small["ssm_norm"], di, "gate_norm_bwd")
    dxc, ddtr, ssd_sums = _ssd_bwd(dy_ssd, xc, dtr, dtrt, prow, pcol, e_mat, sprev, nb, nc, di, "ssd_bwd")
    dxbc, dconv_s, grads["ssm_conv_b"] = _conv_s_bwd(dxc, proj, xbc_col0, ssm_conv_w8, small["ssm_conv_b"], nb, s, cc, "conv_s_bwd")
    dpieces = [("ab", dab), ("ac", dac), ("av", dav), ("z", dz), ("xbc", dxbc), ("ga", dga), ("gb", dgb)]
    dw = {tag: _mm_tn(piece, u, f"dw_in_{tag}") for tag, piece in dpieces}
    dw_dt = _mm_tn(ddtr, u, "dw_in_dt")[:hs]
    du_main = _mm([piece for _, piece in dpieces], w_main_t, F32, "d_u", tk=1024)
    du = _mm(ddtr, w_dt_t, F32, "d_u_dt", res=du_main)
    dh1, grads["mix_norm"] = _rms_bwd(du, h1, small["mix_norm"], "mix_dnorm", dres=dh2)
    big["w_in"] = jnp.concatenate([dw["ab"], dw["ac"], dw["av"], dw["z"], dw["xbc"], dw_dt, dw["ga"], dw["gb"]], axis=0)
    dep = comm.grads(1, big)
    dx, grads["ffn1_norm"] = _ffn_bwd(
        dh1, ffn1_saved, small["ffn1_norm"], wts["ffn1_w_gate_up"], wts["ffn1_w_down"], "ffn1", dep,
        lambda dw_gu_t, dw_d: comm.grads(2, {"ffn1_w_gate_up": dw_gu_t, "ffn1_w_down": dw_d}))

    grads["conv_a_w"] = dconv_a[:CONV_A_K]
    grads["ssm_conv_w"] = dconv_s[:SSM_CONV_K]
    grads["ssm_dt_bias"] = ssd_sums[0:1, :hs]
    grads["ssm_a_log"] = ssd_sums[1:2, :hs]
    grads["ssm_d"] = ssd_sums[2:3, :hs]
    return loss_vec[0, 0], dx.reshape(nb, s, d), grads


def _step(inputs):
    w = {k: inputs[k] for k in WEIGHT_ORDER}
    mom = {k: inputs["m_" + k] for k in WEIGHT_ORDER}
    vel = {k: inputs["v_" + k] for k in WEIGHT_ORDER}
    me = 4 * lax.axis_index("x") + 2 * lax.axis_index("y") + lax.axis_index("c")

    send = {k: (w[k][0].T if k in COL_SHARDED else w[k][0]).astype(BF16) for k in BIG_WEIGHTS}

    def own_slot(land, mine):
        return lax.dynamic_update_slice(land, mine[None], (me, 0, 0))

    gathers, exchanges = {}, {}

    def weights(i, after):
        if i == 0:
            lands = _all_gather([send[k] for k in W_GROUPS[0]], "gather0")
        else:
            sent, lands = _push_wait(gathers[i], False, after, f"gather{i}_wait")
            lands = [own_slot(land, mine) for land, mine in zip(lands, sent)]
        full = {k: land.reshape(N_DEV * land.shape[1], land.shape[2]) for k, land in zip(W_GROUPS[i], lands)}
        dep = jnp.zeros((8, LANES), F32)
        if i + 1 < len(W_GROUPS):
            gathers[i + 1] = _push_start([send[k] for k in W_GROUPS[i + 1]], False, lands[0], f"gather{i + 1}_start")
            dep = gathers[i + 1]["token"]
        return full, dep

    def send_grads(i, by_name):
        slots = [by_name[k].reshape((N_DEV,) + send[k].shape) for k in G_GROUPS[i]]
        exchanges[i] = _push_start(slots, True, slots[0], f"exchange{i}_start")
        return exchanges[i]["token"]

    comm = types.SimpleNamespace(weights=weights, grads=send_grads)
    small = {k: w[k] for k in SMALL_REPLICATED}
    conv_shapes = [w[k].shape[1:] for k in SMALL_SHARDED]
    packed_c, conv_offs = _pack_rows([w[k][0] for k in SMALL_SHARDED], F32, 8)
    conv_blocks = _unpack_rows(_all_gather([packed_c], "gather_conv_weights")[0], conv_offs, conv_shapes)
    for k, b in zip(SMALL_SHARDED, conv_blocks):
        small[k] = _full_from_slots(b, True)[None]

    loss_local, grad_x, grads = _local_step(inputs["x"], inputs["mem"], inputs["loss_target"], small, comm)
    loss = lax.psum(loss_local, AXES)

    out = {}
    for i, names in enumerate(G_GROUPS):
        sent, lands = _push_wait(exchanges[i], True, grad_x, f"exchange{i}_wait")
        for k, land, slots in zip(names, lands, sent):
            parts = own_slot(land, lax.dynamic_index_in_dim(slots, me, 0, keepdims=False))
            if k in COL_SHARDED:
                parts = parts.transpose(0, 2, 1)
            out[k] = tuple(_sum8_adamw(parts, w[k], mom[k], vel[k], f"sum_adamw_{k}"))

    small_names = SMALL_REPLICATED + SMALL_SHARDED
    packed_g, small_offs = _pack_rows([grads[k] for k in small_names], F32, 8)
    total = _sum8(_all_gather([packed_g], "gather_small_grads")[0], "sum_small_grads")
    full_grads = _unpack_rows(total, small_offs, [grads[k].shape for k in small_names])
    mine = {}
    for k, g in zip(small_names, full_grads):
        if k in SMALL_SHARDED:
            c_loc = w[k].shape[2]
            g = lax.dynamic_slice_in_dim(g, me * c_loc, c_loc, axis=1)
        mine[k] = g.reshape(w[k].shape)
    sg, s_offs = _pack_rows([mine[k] for k in small_names], F32, 8)
    sw, _ = _pack_rows([w[k] for k in small_names], F32, 8)
    sm, _ = _pack_rows([mom[k] for k in small_names], F32, 8)
    sv, _ = _pack_rows([vel[k] for k in small_names], F32, 8)
    s_shapes = [w[k].shape for k in small_names]
    small_out = [_unpack_rows(a, s_offs, s_shapes) for a in _adamw(sg, sw, sm, sv, "adamw_small")]
    for i, k in enumerate(small_names):
        out[k] = (mine[k],) + tuple(o[i] for o in small_out)

    res = [loss, grad_x]
    for j in range(4):
        res += [out[k][j] for k in WEIGHT_ORDER]
    return tuple(res)


def kernel(x, mem, ffn1_norm, ffn1_w_gate_up, ffn1_w_down, mix_norm, w_in, conv_a_w, w_out_a, ssm_conv_w, ssm_conv_b, ssm_dt_bias, ssm_a_log, ssm_d, ssm_norm, w_out_ssm, w_mix_out, xattn_norm, mem_norm, w_q, w_kv, w_o_x, ffn2_norm, ffn2_w_gate_up, ffn2_w_down, final_norm, loss_target, m_ffn1_norm, m_ffn1_w_gate_up, m_ffn1_w_down, m_mix_norm, m_w_in, m_conv_a_w, m_w_out_a, m_ssm_conv_w, m_ssm_conv_b, m_ssm_dt_bias, m_ssm_a_log, m_ssm_d, m_ssm_norm, m_w_out_ssm, m_w_mix_out, m_xattn_norm, m_mem_norm, m_w_q, m_w_kv, m_w_o_x, m_ffn2_norm, m_ffn2_w_gate_up, m_ffn2_w_down, m_final_norm, v_ffn1_norm, v_ffn1_w_gate_up, v_ffn1_w_down, v_mix_norm, v_w_in, v_conv_a_w, v_w_out_a, v_ssm_conv_w, v_ssm_conv_b, v_ssm_dt_bias, v_ssm_a_log, v_ssm_d, v_ssm_norm, v_w_out_ssm, v_w_mix_out, v_xattn_norm, v_mem_norm, v_w_q, v_w_kv, v_w_o_x, v_ffn2_norm, v_ffn2_w_gate_up, v_ffn2_w_down, v_final_norm):
    return _step(dict(locals()))
```

```python
import functools
import math
import types

import jax
import jax.numpy as jnp
from jax import lax
from jax.experimental import pallas as pl
from jax.experimental.pallas import tpu as pltpu

F32, BF16 = jnp.float32, jnp.bfloat16
HI = lax.Precision.HIGHEST
MESH = pl.DeviceIdType.MESH
AXES = ("x", "y", "c")
N_DEV = 8

EPS = 1e-6
FFN_RES_WEIGHT = 0.5
SSM_HEAD_DIM = 64
SSM_GROUPS = 4
SSM_STATE = 128
SSM_CHUNK = 128
CONV_A_K = 3
SSM_CONV_K = 4
XATTN_HEADS = 4
ADAM_LR, ADAM_B1, ADAM_B2, ADAM_EPS, ADAM_WD, ADAM_STEP = 1e-3, 0.9, 0.999, 1e-8, 0.01, 10

LANES = 128
BF16_SUBLANES = 16
VMEM_LIMIT_BYTES = 56 * 2 ** 20
NEG_BIG = -1e30

BIG_WEIGHTS = ("ffn1_w_gate_up", "ffn1_w_down", "w_in", "w_out_a", "w_out_ssm", "w_mix_out",
               "w_q", "w_kv", "w_o_x", "ffn2_w_gate_up", "ffn2_w_down")
COL_SHARDED = ("ffn1_w_gate_up", "w_in", "w_kv", "ffn2_w_gate_up")
SMALL_REPLICATED = ("ffn1_norm", "mix_norm", "ssm_conv_b", "ssm_dt_bias", "ssm_a_log", "ssm_d", "ssm_norm",
                    "xattn_norm", "mem_norm", "ffn2_norm", "final_norm")
SMALL_SHARDED = ("conv_a_w", "ssm_conv_w")
WEIGHT_ORDER = ("ffn1_norm", "ffn1_w_gate_up", "ffn1_w_down", "mix_norm", "w_in", "conv_a_w", "w_out_a",
                "ssm_conv_w", "ssm_conv_b", "ssm_dt_bias", "ssm_a_log", "ssm_d", "ssm_norm", "w_out_ssm",
                "w_mix_out", "xattn_norm", "mem_norm", "w_q", "w_kv", "w_o_x", "ffn2_norm", "ffn2_w_gate_up",
                "ffn2_w_down", "final_norm")


def _tile(dim, pref, unit):
    best = None
    t = unit
    while t <= min(dim, pref):
        if dim % t == 0:
            best = t
        t += unit
    return best if best is not None else dim


def _params(*sem):
    return pltpu.CompilerParams(dimension_semantics=sem, vmem_limit_bytes=VMEM_LIMIT_BYTES)


def _silu(x):
    return x * jax.nn.sigmoid(x)


def _dsilu(x):
    s = jax.nn.sigmoid(x)
    return s * (1.0 + x * (1.0 - s))


def _softplus(x):
    return jnp.maximum(x, 0.0) + jnp.log(1.0 + jnp.exp(-jnp.abs(x)))


def _dot(a, b, dims=(((1,), (0,)), ((), ())), precision=None):
    return lax.dot_general(a, b, dims, preferred_element_type=F32, precision=precision)


def _stack_rows(rows, width):
    r_idx = lax.broadcasted_iota(jnp.int32, (8, width), 0)
    acc = jnp.zeros((8, width), F32)
    for k, row in enumerate(rows):
        acc = jnp.where(r_idx == k, row, acc)
    return acc


NT = (((1,), (1,)), ((), ()))
TN = (((0,), (0,)), ((), ()))


def _mm(a, b, out_dtype, name, res=None, alpha=1.0, nt=False, dep=None, tm=1024, tn=2048, tk=2816):
    pieces = list(a) if isinstance(a, (list, tuple)) else [a]
    m = pieces[0].shape[0]
    k = sum(p.shape[1] for p in pieces)
    n = b.shape[0] if nt else b.shape[1]
    assert (b.shape[1] if nt else b.shape[0]) == k
    tm, tn = _tile(m, tm, 8), _tile(n, tn, LANES)
    tk = _tile(math.gcd(*[p.shape[1] for p in pieces]), tk, LANES)
    nk = k // tk
    starts, s0 = [], 0
    for p in pieces:
        starts.append((s0, p.shape[1] // tk))
        s0 += p.shape[1] // tk
    n_p = len(pieces)

    def body(*refs):
        a_refs, b_ref = refs[:n_p], refs[n_p]
        r_ref = refs[n_p + 1] if res is not None else None
        n_in = n_p + 1 + (res is not None) + (dep is not None)
        o_ref = refs[n_in]
        scr = refs[n_in + 1:]

        def finish(acc):
            acc = alpha * acc if alpha != 1.0 else acc
            if r_ref is not None:
                acc = r_ref[...] + acc
            o_ref[...] = acc.astype(out_dtype)

        def product(a_ref):
            return _dot(a_ref[...].astype(BF16), b_ref[...].astype(BF16), NT if nt else (((1,), (0,)), ((), ())))

        if nk == 1:
            finish(product(a_refs[0]))
            return
        acc_ref = scr[0]
        kk = pl.program_id(2)
        for (s, cnt), a_ref in zip(starts, a_refs):
            if s == 0:
                @pl.when(kk == 0)
                def _():
                    acc_ref[...] = product(a_ref)

                @pl.when(jnp.logical_and(kk > 0, kk < cnt))
                def _():
                    acc_ref[...] += product(a_ref)
            else:
                @pl.when(jnp.logical_and(kk >= s, kk < s + cnt))
                def _():
                    acc_ref[...] += product(a_ref)

        @pl.when(kk == nk - 1)
        def _():
            finish(acc_ref[...])

    def a_spec(s, cnt):
        return pl.BlockSpec((tm, tk), lambda i, j, kk: (i, jnp.clip(kk - s, 0, cnt - 1)))

    in_specs = [a_spec(s, cnt) for s, cnt in starts]
    in_specs.append(pl.BlockSpec((tn, tk), lambda i, j, kk: (j, kk)) if nt else pl.BlockSpec((tk, tn), lambda i, j, kk: (kk, j)))
    args = pieces + [b]
    if res is not None:
        in_specs.append(pl.BlockSpec((tm, tn), lambda i, j, kk: (i, j)))
        args.append(res)
    if dep is not None:
        in_specs.append(pl.BlockSpec((8, LANES), lambda i, j, kk: (0, 0)))
        args.append(dep)
    return pl.pallas_call(
        body, name=name, grid=(m // tm, n // tn, nk), in_specs=in_specs,
        out_specs=pl.BlockSpec((tm, tn), lambda i, j, kk: (i, j)),
        out_shape=jax.ShapeDtypeStruct((m, n), out_dtype),
        scratch_shapes=[pltpu.VMEM((tm, tn), F32)] if nk > 1 else [],
        compiler_params=_params("parallel", "parallel", "arbitrary"),
    )(*args)


def _mm_tn(x, dy, name, out_dtype=BF16, alpha=1.0, dep=None, tko=1408, tn=1024, tt=1024):
    t, k = x.shape
    n = dy.shape[1]
    tko, tn, tt = _tile(k, tko, LANES), _tile(n, tn, LANES), _tile(t, tt, 8)
    nt_steps = t // tt

    def body(*refs):
        x_ref, dy_ref = refs[:2]
        o_ref, acc_ref = refs[-2:]
        part = _dot(x_ref[...].astype(BF16), dy_ref[...].astype(BF16), TN)
        step = pl.program_id(2)

        @pl.when(step == 0)
        def _():
            acc_ref[...] = part

        @pl.when(step > 0)
        def _():
            acc_ref[...] += part

        @pl.when(step == nt_steps - 1)
        def _():
            acc = acc_ref[...]
            o_ref[...] = (alpha * acc if alpha != 1.0 else acc).astype(out_dtype)

    in_specs = [pl.BlockSpec((tt, tko), lambda i, j, s: (s, i)), pl.BlockSpec((tt, tn), lambda i, j, s: (s, j))]
    args = [x, dy]
    if dep is not None:
        in_specs.append(pl.BlockSpec((8, LANES), lambda i, j, s: (0, 0)))
        args.append(dep)
    return pl.pallas_call(
        body, name=name, grid=(k // tko, n // tn, nt_steps), in_specs=in_specs,
        out_specs=pl.BlockSpec((tko, tn), lambda i, j, s: (i, j)),
        out_shape=jax.ShapeDtypeStruct((k, n), out_dtype),
        scratch_shapes=[pltpu.VMEM((tko, tn), F32)],
        compiler_params=_params("parallel", "parallel", "arbitrary"),
    )(*args)


def _rms_fwd(x, g, name, tt=512):
    t, d = x.shape
    tt = _tile(t, tt, 8)

    def body(x_ref, g_ref, o_ref):
        xv = x_ref[...]
        r = lax.rsqrt(jnp.mean(xv * xv, axis=-1, keepdims=True) + EPS)
        o_ref[...] = (xv * r * g_ref[...]).astype(BF16)

    return pl.pallas_call(
        body, name=name, grid=(t // tt,),
        in_specs=[pl.BlockSpec((tt, d), lambda i: (i, 0)), pl.BlockSpec((1, d), lambda i: (0, 0))],
        out_specs=pl.BlockSpec((tt, d), lambda i: (i, 0)),
        out_shape=jax.ShapeDtypeStruct((t, d), BF16), compiler_params=_params("parallel"),
    )(x, g)


def _rms_bwd(dn, x, g, name, dres=None, tt=512):
    t, d = x.shape
    tt = _tile(t, tt, 8)

    def body(*refs):
        if dres is None:
            dn_ref, x_ref, g_ref, dx_ref, dg_ref = refs
            r_ref = None
        else:
            dn_ref, x_ref, g_ref, r_ref, dx_ref, dg_ref = refs
        xv, dnv = x_ref[...], dn_ref[...].astype(F32)
        r = lax.rsqrt(jnp.mean(xv * xv, axis=-1, keepdims=True) + EPS)
        xh = xv * r
        gy = dnv * g_ref[...]
        dx = r * (gy - xh * jnp.mean(gy * xh, axis=-1, keepdims=True))
        if r_ref is not None:
            dx = dx + r_ref[...]
        dx_ref[...] = dx
        part = jnp.sum(dnv * xh, axis=0, keepdims=True)

        @pl.when(pl.program_id(0) == 0)
        def _():
            dg_ref[...] = part

        @pl.when(pl.program_id(0) > 0)
        def _():
            dg_ref[...] += part

    row = pl.BlockSpec((tt, d), lambda i: (i, 0))
    vec = pl.BlockSpec((1, d), lambda i: (0, 0))
    in_specs, args = [row, row, vec], [dn, x, g]
    if dres is not None:
        in_specs.append(row)
        args.append(dres)
    return pl.pallas_call(
        body, name=name, grid=(t // tt,), in_specs=in_specs, out_specs=[row, vec],
        out_shape=[jax.ShapeDtypeStruct((t, d), F32), jax.ShapeDtypeStruct((1, d), F32)],
        compiler_params=_params("arbitrary"),
    )(*args)


def _ffn_up(n, w_gu_t, name, tm=1024, tf=1408):
    t, d = n.shape
    f = w_gu_t.shape[0] // 2
    tm, tf = _tile(t, tm, 8), _tile(f, tf, LANES)
    nf = f // tf

    def body(n_ref, wg_ref, wu_ref, g_ref, u_ref, a_ref):
        nv = n_ref[...]
        gate, up = _dot(nv, wg_ref[...], NT), _dot(nv, wu_ref[...], NT)
        g_ref[...] = gate.astype(BF16)
        u_ref[...] = up.astype(BF16)
        a_ref[...] = (_silu(gate) * up).astype(BF16)

    blk = pl.BlockSpec((tm, tf), lambda i, j: (i, j))
    out = jax.ShapeDtypeStruct((t, f), BF16)
    return pl.pallas_call(
        body, name=name, grid=(t // tm, nf),
        in_specs=[pl.BlockSpec((tm, d), lambda i, j: (i, 0)), pl.BlockSpec((tf, d), lambda i, j: (j, 0)),
                  pl.BlockSpec((tf, d), lambda i, j: (j + nf, 0))],
        out_specs=[blk, blk, blk], out_shape=[out, out, out], compiler_params=_params("parallel", "parallel"),
    )(n, w_gu_t, w_gu_t)


def _ffn_da(dh, w_d, gate, up, name, alpha, dep=None, tm=1024, tf=1408):
    t, d = dh.shape
    f = w_d.shape[0]
    tm, tf = _tile(t, tm, 8), _tile(f, tf, LANES)

    def body(*refs):
        dh_ref, w_ref, g_ref, u_ref = refs[:4]
        dg_ref, du_ref = refs[-2:]
        da = alpha * _dot(dh_ref[...].astype(BF16), w_ref[...], NT)
        gv = g_ref[...].astype(F32)
        dg_ref[...] = (da * u_ref[...].astype(F32) * _dsilu(gv)).astype(BF16)
        du_ref[...] = (da * _silu(gv)).astype(BF16)

    blk = pl.BlockSpec((tm, tf), lambda i, j: (i, j))
    in_specs = [pl.BlockSpec((tm, d), lambda i, j: (i, 0)), pl.BlockSpec((tf, d), lambda i, j: (j, 0)), blk, blk]
    args = [dh, w_d, gate, up]
    if dep is not None:
        in_specs.append(pl.BlockSpec((8, LANES), lambda i, j: (0, 0)))
        args.append(dep)
    out = jax.ShapeDtypeStruct((t, f), BF16)
    return pl.pallas_call(
        body, name=name, grid=(t // tm, f // tf), in_specs=in_specs, out_specs=[blk, blk], out_shape=[out, out],
        compiler_params=_params("parallel", "parallel"),
    )(*args)


def _merge_fwd(ya, yb, proj, ga_blk, gb_blk, d, name, tt=512):
    t = ya.shape[0]
    tt = _tile(t, tt, 8)

    def body(ya_ref, yb_ref, ga_ref, gb_ref, o_ref):
        o_ref[...] = (jax.nn.sigmoid(ga_ref[...].astype(F32)) * ya_ref[...].astype(F32)
                      + jax.nn.sigmoid(gb_ref[...].astype(F32)) * yb_ref[...].astype(F32)).astype(BF16)

    row = pl.BlockSpec((tt, d), lambda i: (i, 0))
    return pl.pallas_call(
        body, name=name, grid=(t // tt,),
        in_specs=[row, row, pl.BlockSpec((tt, d), lambda i: (i, ga_blk)), pl.BlockSpec((tt, d), lambda i: (i, gb_blk))],
        out_specs=row, out_shape=jax.ShapeDtypeStruct((t, d), BF16), compiler_params=_params("parallel"),
    )(ya, yb, proj, proj)


def _merge_bwd(dm, ya, yb, proj, ga_blk, gb_blk, d, name, tt=512):
    t = ya.shape[0]
    tt = _tile(t, tt, 8)

    def body(dm_ref, ya_ref, yb_ref, ga_ref, gb_ref, dya_ref, dyb_ref, dga_ref, dgb_ref):
        dmv = dm_ref[...].astype(F32)
        sa, sb = jax.nn.sigmoid(ga_ref[...].astype(F32)), jax.nn.sigmoid(gb_ref[...].astype(F32))
        dya_ref[...] = (dmv * sa).astype(BF16)
        dyb_ref[...] = (dmv * sb).astype(BF16)
        dga_ref[...] = (dmv * ya_ref[...].astype(F32) * sa * (1.0 - sa)).astype(BF16)
        dgb_ref[...] = (dmv * yb_ref[...].astype(F32) * sb * (1.0 - sb)).astype(BF16)

    row = pl.BlockSpec((tt, d), lambda i: (i, 0))
    out = jax.ShapeDtypeStruct((t, d), BF16)
    return pl.pallas_call(
        body, name=name, grid=(t // tt,),
        in_specs=[row, row, row, pl.BlockSpec((tt, d), lambda i: (i, ga_blk)), pl.BlockSpec((tt, d), lambda i: (i, gb_blk))],
        out_specs=[row] * 4, out_shape=[out] * 4, compiler_params=_params("parallel"),
    )(dm, ya, yb, proj, proj)


def _loss_head(h, g, target, name, tt=512):
    t, d = h.shape
    tt = _tile(t, tt, 8)

    def body(h_ref, g_ref, tg_ref, loss_ref, dh_ref, dg_ref):
        xv = h_ref[...]
        r = lax.rsqrt(jnp.mean(xv * xv, axis=-1, keepdims=True) + EPS)
        xh = xv * r
        err = xh * g_ref[...] - tg_ref[...]
        dout = err * (1.0 / d)
        gy = dout * g_ref[...]
        dh_ref[...] = r * (gy - xh * jnp.mean(gy * xh, axis=-1, keepdims=True))
        dg_part = jnp.sum(dout * xh, axis=0, keepdims=True)
        loss_part = jnp.full((1, LANES), 0.5 / d, F32) * jnp.sum(err * err)

        @pl.when(pl.program_id(0) == 0)
        def _():
            dg_ref[...] = dg_part
            loss_ref[...] = loss_part

        @pl.when(pl.program_id(0) > 0)
        def _():
            dg_ref[...] += dg_part
            loss_ref[...] += loss_part

    row = pl.BlockSpec((tt, d), lambda i: (i, 0))
    vec = pl.BlockSpec((1, d), lambda i: (0, 0))
    return pl.pallas_call(
        body, name=name, grid=(t // tt,), in_specs=[row, vec, row],
        out_specs=[pl.BlockSpec((1, LANES), lambda i: (0, 0)), row, vec],
        out_shape=[jax.ShapeDtypeStruct((1, LANES), F32), jax.ShapeDtypeStruct((t, d), F32), jax.ShapeDtypeStruct((1, d), F32)],
        compiler_params=_params("arbitrary"),
    )(h, g, target)


def _shift_down(x, k, t_idx):
    if k == 0:
        return x
    return jnp.where(t_idx >= k, pltpu.roll(x, k, 0), 0.0)


def _shift_up(x, k, t_idx, s):
    if k == 0:
        return x
    return jnp.where(t_idx < s - k, pltpu.roll(x, s - k, 0), 0.0)


def _conv_a_fwd(proj, w, nb, s, d, name, cb=256):
    cb = _tile(d, cb, LANES)
    nd = d // cb

    def body(b_ref, c_ref, v_ref, w_ref, o_ref):
        t_idx = lax.broadcasted_iota(jnp.int32, (s, cb), 0)
        cv = c_ref[...].astype(F32) * v_ref[...].astype(F32)
        cc = sum(w_ref[k:k + 1, :] * _shift_down(cv, CONV_A_K - 1 - k, t_idx) for k in range(CONV_A_K))
        o_ref[...] = (b_ref[...].astype(F32) * cc).astype(BF16)

    def col(off):
        return pl.BlockSpec((s, cb), lambda b, j: (b, j + off * nd))

    return pl.pallas_call(
        body, name=name, grid=(nb, nd), in_specs=[col(0), col(1), col(2), pl.BlockSpec((8, cb), lambda b, j: (0, j))],
        out_specs=pl.BlockSpec((s, cb), lambda b, j: (b, j)),
        out_shape=jax.ShapeDtypeStruct((nb * s, d), BF16), compiler_params=_params("parallel", "parallel"),
    )(proj, proj, proj, w)


def _conv_a_bwd(dy, proj, w, nb, s, d, name, cb=256):
    cb = _tile(d, cb, LANES)
    nd = d // cb

    def body(dy_ref, b_ref, c_ref, v_ref, w_ref, db_ref, dc_ref, dv_ref, dw_ref):
        t_idx = lax.broadcasted_iota(jnp.int32, (s, cb), 0)
        cv_c, cv_v = c_ref[...].astype(F32), v_ref[...].astype(F32)
        cv = cv_c * cv_v
        shifted = [_shift_down(cv, CONV_A_K - 1 - k, t_idx) for k in range(CONV_A_K)]
        cc = sum(w_ref[k:k + 1, :] * shifted[k] for k in range(CONV_A_K))
        dyv = dy_ref[...].astype(F32)
        db_ref[...] = (dyv * cc).astype(BF16)
        dcc = dyv * b_ref[...].astype(F32)
        dcv = sum(w_ref[k:k + 1, :] * _shift_up(dcc, CONV_A_K - 1 - k, t_idx, s) for k in range(CONV_A_K))
        dc_ref[...] = (dcv * cv_v).astype(BF16)
        dv_ref[...] = (dcv * cv_c).astype(BF16)
        rows = [jnp.sum(dcc * shifted[k], axis=0, keepdims=True) for k in range(CONV_A_K)]
        part = _stack_rows(rows, cb)

        @pl.when(pl.program_id(1) == 0)
        def _():
            dw_ref[...] = part

        @pl.when(pl.program_id(1) > 0)
        def _():
            dw_ref[...] += part

    def col(off):
        return pl.BlockSpec((s, cb), lambda j, b: (b, j + off * nd))

    own = pl.BlockSpec((s, cb), lambda j, b: (b, j))
    wspec = pl.BlockSpec((8, cb), lambda j, b: (0, j))
    out = jax.ShapeDtypeStruct((nb * s, d), BF16)
    return pl.pallas_call(
        body, name=name, grid=(nd, nb), in_specs=[own, col(0), col(1), col(2), wspec],
        out_specs=[own, own, own, wspec], out_shape=[out, out, out, jax.ShapeDtypeStruct((8, d), F32)],
        compiler_params=_params("parallel", "arbitrary"),
    )(dy, proj, proj, proj, w)


def _conv_s_fwd(proj, col0, w, bias, nb, s, cc_width, name, cb=256):
    cb = _tile(math.gcd(cc_width, col0) if col0 else cc_width, cb, LANES)
    nd, off = cc_width // cb, col0 // cb

    def body(x_ref, w_ref, b_ref, o_ref, pre_ref):
        t_idx = lax.broadcasted_iota(jnp.int32, (s, cb), 0)
        xv = x_ref[...].astype(F32)
        pre = b_ref[...] + sum(w_ref[k:k + 1, :] * _shift_down(xv, SSM_CONV_K - 1 - k, t_idx) for k in range(SSM_CONV_K))
        o_ref[...] = _silu(pre).astype(BF16)
        pre_ref[...] = pre.astype(BF16)

    vec = pl.BlockSpec((8, cb), lambda b, j: (0, j))
    own = pl.BlockSpec((s, cb), lambda b, j: (b, j))
    out = jax.ShapeDtypeStruct((nb * s, cc_width), BF16)
    return pl.pallas_call(
        body, name=name, grid=(nb, nd),
        in_specs=[pl.BlockSpec((s, cb), lambda b, j: (b, j + off)), vec, pl.BlockSpec((1, cb), lambda b, j: (0, j))],
        out_specs=[own, own], out_shape=[out, out], compiler_params=_params("parallel", "parallel"),
    )(proj, w, bias)


def _conv_s_bwd(dxc, pre, proj, col0, w, nb, s, cc_width, name, cb=256):
    cb = _tile(math.gcd(cc_width, col0) if col0 else cc_width, cb, LANES)
    nd, off = cc_width // cb, col0 // cb

    def body(d_ref, pre_ref, x_ref, w_ref, dx_ref, dw_ref, db_ref):
        t_idx = lax.broadcasted_iota(jnp.int32, (s, cb), 0)
        xv = x_ref[...].astype(F32)
        dpre = d_ref[...].astype(F32) * _dsilu(pre_ref[...].astype(F32))
        ahead = [_shift_up(dpre, j, t_idx, s) for j in range(SSM_CONV_K)]
        dx_ref[...] = sum(w_ref[k:k + 1, :] * ahead[SSM_CONV_K - 1 - k] for k in range(SSM_CONV_K)).astype(BF16)
        rows = [jnp.sum(ahead[SSM_CONV_K - 1 - k] * xv, axis=0, keepdims=True) for k in range(SSM_CONV_K)]
        dw_part = _stack_rows(rows, cb)
        db_part = jnp.sum(dpre, axis=0, keepdims=True)

        @pl.when(pl.program_id(1) == 0)
        def _():
            dw_ref[...] = dw_part
            db_ref[...] = db_part

        @pl.when(pl.program_id(1) > 0)
        def _():
            dw_ref[...] += dw_part
            db_ref[...] += db_part

    own = pl.BlockSpec((s, cb), lambda j, b: (b, j))
    wspec = pl.BlockSpec((8, cb), lambda j, b: (0, j))
    bspec = pl.BlockSpec((1, cb), lambda j, b: (0, j))
    return pl.pallas_call(
        body, name=name, grid=(nd, nb),
        in_specs=[own, own, pl.BlockSpec((s, cb), lambda j, b: (b, j + off)), wspec],
        out_specs=[own, wspec, bspec],
        out_shape=[jax.ShapeDtypeStruct((nb * s, cc_width), BF16), jax.ShapeDtypeStruct((8, cc_width), F32),
                   jax.ShapeDtypeStruct((1, cc_width), F32)],
        compiler_params=_params("parallel", "arbitrary"),
    )(dxc, pre, proj, w)


def _split3(v):
    hi = v.astype(BF16)
    r1 = v - hi.astype(F32)
    mid = r1.astype(BF16)
    return hi, mid, (r1 - mid.astype(F32)).astype(BF16)


def _exact_left(mask_b, v):
    return sum(_dot(mask_b, t) for t in _split3(v))


def _exact_right(v, mask_b):
    return sum(_dot(t, mask_b) for t in _split3(v))


def _head_sums(v, e_b):
    return _dot(v.astype(BF16), e_b, NT)


def _spread(v, out_ref, di):
    lane = lax.broadcasted_iota(jnp.int32, (v.shape[0], LANES), 1)
    for pr in range(di // LANES):
        h0 = pr * (LANES // SSM_HEAD_DIM)
        out_ref[:, pr * LANES:(pr + 1) * LANES] = jnp.where(lane < SSM_HEAD_DIM, v[:, h0:h0 + 1], v[:, h0 + 1:h0 + 2])


def _ssd_common(xc_ref, dtr_ref, dtrt_ref, prow_ref, pcol_ref, dtx_ref, acsx_ref, dx_ref, di):
    l = SSM_CHUNK
    bias_r, a_r = prow_ref[0:1, :], -jnp.exp(prow_ref[1:2, :])
    sp_in = dtr_ref[...] + bias_r
    dt = _softplus(sp_in)
    li = lax.broadcasted_iota(jnp.int32, (l, l), 0)
    si = lax.broadcasted_iota(jnp.int32, (l, l), 1)
    lower_b = (li >= si).astype(BF16)
    upper_b = (li <= si).astype(BF16)
    acs = _exact_left(lower_b, dt * a_r)
    bias_c, a_c = pcol_ref[:, 0:1], -jnp.exp(pcol_ref[:, 1:2])
    dt_t = _softplus(dtrt_ref[...] + bias_c)
    acs_t = _exact_right(dt_t * a_c, upper_b)
    _spread(dt, dtx_ref, di)
    _spread(acs, acsx_ref, di)
    _spread(prow_ref[0:8, :], dx_ref, di)
    acs_exp = acsx_ref[...]
    acs_last = acs_exp[l - 1:l, :]
    x = xc_ref[:, 0:di].astype(F32)
    return dict(dt=dt, a_r=a_r, sp_in=sp_in, acs=acs, acs_t=acs_t, dt_exp=dtx_ref[...], e_exp=jnp.exp(acs_exp),
                el_exp=jnp.exp(acs_last), f_exp=jnp.exp(acs_last - acs_exp), x=x, mask=li >= si, upper_b=upper_b,
                d_exp=dx_ref[2:3, :])


def _decay(q, h):
    seg = q["acs"][:, h:h + 1] - q["acs_t"][h:h + 1, :]
    return jnp.exp(jnp.where(q["mask"], seg, NEG_BIG))


def _ssd_fwd(xc, dtr, dtrt, prow, pcol, nb, nc, di, name):
    l, n, g_n, p = SSM_CHUNK, SSM_STATE, SSM_GROUPS, SSM_HEAD_DIM
    cc = xc.shape[1]
    gw = di // g_n
    assert p * 2 == LANES and gw % LANES == 0

    def body(xc_ref, dtr_ref, dtrt_ref, prow_ref, pcol_ref, y_ref, sprev_ref, st_ref, dtx_ref, acsx_ref, dx_ref):
        @pl.when(pl.program_id(1) == 0)
        def _():
            st_ref[...] = jnp.zeros_like(st_ref)

        q = _ssd_common(xc_ref, dtr_ref, dtrt_ref, prow_ref, pcol_ref, dtx_ref, acsx_ref, dx_ref, di)
        x = q["x"]
        xd = x * q["dt_exp"]
        xdb = xd.astype(BF16)
        xdf = (xd * q["f_exp"]).astype(BF16)
        lane = lax.broadcasted_iota(jnp.int32, (l, LANES), 1)
        for g in range(g_n):
            lo = g * gw
            bg = xc_ref[:, di + g * n: di + (g + 1) * n]
            cg = xc_ref[:, di + g_n * n + g * n: di + g_n * n + (g + 1) * n]
            cb = _dot(cg, bg, NT)
            st_g = st_ref[:, lo:lo + gw]
            y_off = q["e_exp"][:, lo:lo + gw] * _dot(cg, st_g.astype(BF16))
            for pr in range(gw // LANES):
                c0 = lo + pr * LANES
                h0 = c0 // p
                xp = xdb[:, c0:c0 + LANES]
                m0 = (cb * _decay(q, h0)).astype(BF16)
                m1 = (cb * _decay(q, h0 + 1)).astype(BF16)
                yd = _dot(m0, jnp.where(lane < p, xp, 0)) + _dot(m1, jnp.where(lane >= p, xp, 0))
                y_ref[:, c0:c0 + LANES] = (yd + y_off[:, pr * LANES:(pr + 1) * LANES]
                                           + q["d_exp"][:, c0:c0 + LANES] * x[:, c0:c0 + LANES])
            sprev_ref[:, lo:lo + gw] = st_g
            st_ref[:, lo:lo + gw] = q["el_exp"][:, lo:lo + gw] * st_g + _dot(bg, xdf[:, lo:lo + gw], TN)

    tok = lambda w: pl.BlockSpec((l, w), lambda b, c: (b * nc + c, 0))
    const = lambda r, w: pl.BlockSpec((r, w), lambda b, c: (0, 0))
    return pl.pallas_call(
        body, name=name, grid=(nb, nc),
        in_specs=[tok(cc), tok(LANES), pl.BlockSpec((LANES, l), lambda b, c: (0, b * nc + c)),
                  const(8, LANES), const(LANES, 8)],
        out_specs=[tok(di), pl.BlockSpec((None, n, di), lambda b, c: (b * nc + c, 0, 0))],
        out_shape=[jax.ShapeDtypeStruct((nb * nc * l, di), F32), jax.ShapeDtypeStruct((nb * nc, n, di), F32)],
        scratch_shapes=[pltpu.VMEM((n, di), F32), pltpu.VMEM((l, di), F32), pltpu.VMEM((l, di), F32),
                        pltpu.VMEM((8, di), F32)],
        compiler_params=_params("parallel", "arbitrary"),
    )(xc, dtr, dtrt, prow, pcol)


def _ssd_bwd(dy, xc, dtr, dtrt, prow, pcol, e_mat, sprev, nb, nc, di, name):
    l, n, g_n, p = SSM_CHUNK, SSM_STATE, SSM_GROUPS, SSM_HEAD_DIM
    cc = xc.shape[1]
    gw = di // g_n

    def body(dy_ref, xc_ref, dtr_ref, dtrt_ref, prow_ref, pcol_ref, e_ref, sprev_ref,
             dxc_ref, ddtr_ref, sums_ref, dst_ref, off_ref, dxd_ref, last_ref, vst_ref,
             dtx_ref, acsx_ref, dx_ref):
        first = jnp.logical_and(pl.program_id(0) == 0, pl.program_id(1) == 0)

        @pl.when(pl.program_id(1) == 0)
        def _():
            dst_ref[...] = jnp.zeros_like(dst_ref)

        head_row = lax.broadcasted_iota(jnp.int32, (LANES, l), 0)
        row_sums, col_sums = jnp.zeros((l, LANES), F32), jnp.zeros((LANES, l), F32)
        strict_lower = lax.broadcasted_iota(jnp.int32, (l, l), 0) > lax.broadcasted_iota(jnp.int32, (l, l), 1)

        q = _ssd_common(xc_ref, dtr_ref, dtrt_ref, prow_ref, pcol_ref, dtx_ref, acsx_ref, dx_ref, di)
        x = q["x"]
        xd = x * q["dt_exp"]
        xdb = xd.astype(BF16)
        xdf = (xd * q["f_exp"]).astype(BF16)
        dyv = dy_ref[...]
        dyb = dyv.astype(BF16)
        dye = (dyv * q["e_exp"]).astype(BF16)
        upper_b = q["upper_b"]
        lane = lax.broadcasted_iota(jnp.int32, (l, LANES), 1)
        for g in range(g_n):
            lo = g * gw
            bg = xc_ref[:, di + g * n: di + (g + 1) * n]
            cg = xc_ref[:, di + g_n * n + g * n: di + g_n * n + (g + 1) * n]
            cb = _dot(cg, bg, NT)
            st_g = sprev_ref[:, lo:lo + gw]
            st_gb = st_g.astype(BF16)
            dst_g = dst_ref[:, lo:lo + gw]
            dst_gb = dst_g.astype(BF16)
            dye_g = dye[:, lo:lo + gw]
            xdf_g = xdf[:, lo:lo + gw]
            y_off = q["e_exp"][:, lo:lo + gw] * _dot(cg, st_gb)
            dc_g = _dot(dye_g, st_gb, NT)
            db_g = _dot(xdf_g, dst_gb, NT)
            dxd_state = _dot(bg, dst_gb) * q["f_exp"][:, lo:lo + gw]
            last_ref[:, lo:lo + gw] = jnp.sum(dst_g * st_g, axis=0, keepdims=True)
            dst_ref[:, lo:lo + gw] = q["el_exp"][:, lo:lo + gw] * dst_g + _dot(cg, dye_g, TN)
            off_ref[:, lo:lo + gw] = dyv[:, lo:lo + gw] * y_off
            vst_ref[:, lo:lo + gw] = xd[:, lo:lo + gw] * dxd_state
            dcb = jnp.zeros((l, l), F32)
            for pr in range(gw // LANES):
                c0 = lo + pr * LANES
                h0 = c0 // p
                xp = xdb[:, c0:c0 + LANES]
                dyp = dyb[:, c0:c0 + LANES]
                dxd_diag = jnp.zeros((l, LANES), F32)
                for k, keep in enumerate((lane < p, lane >= p)):
                    dec = _decay(q, h0 + k)
                    dy_h = jnp.where(keep, dyp, 0)
                    dm_dec = _dot(dy_h, xp, NT) * dec
                    dcb = dcb + dm_dec
                    dxd_diag = dxd_diag + _dot((cb * dec).astype(BF16), dy_h, TN)
                    qm = dm_dec * cb
                    row_sums = jnp.where(lane == h0 + k, jnp.sum(qm, axis=1, keepdims=True), row_sums)
                    col_sums = jnp.where(head_row == h0 + k, jnp.sum(qm, axis=0, keepdims=True), col_sums)
                dxd_ref[:, c0:c0 + LANES] = dxd_diag + dxd_state[:, pr * LANES:(pr + 1) * LANES]
            dcb_b = dcb.astype(BF16)
            dxc_ref[:, di + g * n: di + (g + 1) * n] = (db_g + _dot(dcb_b, cg, TN)).astype(BF16)
            dxc_ref[:, di + g_n * n + g * n: di + g_n * n + (g + 1) * n] = (dc_g + _dot(dcb_b, bg)).astype(BF16)
        dxd = dxd_ref[...]
        e_b = e_ref[...]
        from_y = _exact_left(upper_b, _head_sums(off_ref[...], e_b) + row_sums - col_sums.T)
        from_s = _exact_left(strict_lower.astype(BF16), _head_sums(vst_ref[...], e_b))
        carried = _head_sums(jnp.broadcast_to(last_ref[...], (8, di)), e_b)[0:1, :] * jnp.exp(q["acs"][l - 1:l, :])
        dla = from_y + from_s + carried
        ddt = dla * q["a_r"] + _head_sums(dxd * x, e_b)
        ddtr = ddt * jax.nn.sigmoid(q["sp_in"])
        ddtr_ref[...] = ddtr
        dxc_ref[:, 0:di] = (dxd * q["dt_exp"] + q["d_exp"] * dyv).astype(BF16)
        dd_exp = jnp.sum(dyv * x, axis=0, keepdims=True)
        dd = _head_sums(jnp.broadcast_to(dd_exp, (8, di)), e_b)[0:1, :]
        part = _stack_rows([jnp.sum(ddtr, axis=0, keepdims=True),
                            jnp.sum(dla * q["dt"], axis=0, keepdims=True) * q["a_r"], dd], LANES)

        @pl.when(first)
        def _():
            sums_ref[...] = part

        @pl.when(jnp.logical_not(first))
        def _():
            sums_ref[...] += part

    rev = lambda b, c: b * nc + (nc - 1 - c)
    tok = lambda w: pl.BlockSpec((l, w), lambda b, c: (rev(b, c), 0))
    const = lambda r, w: pl.BlockSpec((r, w), lambda b, c: (0, 0))
    return pl.pallas_call(
        body, name=name, grid=(nb, nc),
        in_specs=[tok(di), tok(cc), tok(LANES), pl.BlockSpec((LANES, l), lambda b, c: (0, rev(b, c))),
                  const(8, LANES), const(LANES, 8), const(LANES, di),
                  pl.BlockSpec((None, n, di), lambda b, c: (rev(b, c), 0, 0))],
        out_specs=[tok(cc), tok(LANES), const(8, LANES)],
        out_shape=[jax.ShapeDtypeStruct((nb * nc * l, cc), BF16), jax.ShapeDtypeStruct((nb * nc * l, LANES), F32),
                   jax.ShapeDtypeStruct((8, LANES), F32)],
        scratch_shapes=[pltpu.VMEM((n, di), F32), pltpu.VMEM((l, di), F32), pltpu.VMEM((l, di), F32),
                        pltpu.VMEM((1, di), F32), pltpu.VMEM((l, di), F32),
                        pltpu.VMEM((l, di), F32), pltpu.VMEM((l, di), F32), pltpu.VMEM((8, di), F32)],
        compiler_params=_params("arbitrary", "arbitrary"),
    )(dy, xc, dtr, dtrt, prow, pcol, e_mat, sprev)


def _gate_norm_fwd(y, proj, z_col0, norm_g, di, name, tt=256):
    t = y.shape[0]
    tt = _tile(t, tt, 8)
    gw = di // SSM_GROUPS
    zw = _tile(math.gcd(di, z_col0), di, LANES)
    nz, zoff = di // zw, z_col0 // zw

    def body(*refs):
        y_ref, z_refs, g_ref, o_ref = refs[0], refs[1:1 + nz], refs[1 + nz], refs[2 + nz]
        for g in range(SSM_GROUPS):
            lo = g * gw
            zv = z_refs[lo // zw][:, lo % zw:lo % zw + gw].astype(F32)
            yg = y_ref[:, lo:lo + gw] * _silu(zv)
            r = lax.rsqrt(jnp.mean(yg * yg, axis=-1, keepdims=True) + EPS)
            o_ref[:, lo:lo + gw] = (yg * r * g_ref[:, lo:lo + gw]).astype(BF16)

    row = pl.BlockSpec((tt, di), lambda i: (i, 0))
    zspecs = [pl.BlockSpec((tt, zw), functools.partial(lambda i, k: (i, zoff + k), k=k)) for k in range(nz)]
    return pl.pallas_call(
        body, name=name, grid=(t // tt,), in_specs=[row] + zspecs + [pl.BlockSpec((1, di), lambda i: (0, 0))],
        out_specs=row, out_shape=jax.ShapeDtypeStruct((t, di), BF16), compiler_params=_params("parallel"),
    )(y, *([proj] * nz), norm_g)


def _gate_norm_bwd(dn, y, proj, z_col0, norm_g, di, name, tt=256):
    t = y.shape[0]
    tt = _tile(t, tt, 8)
    gw = di // SSM_GROUPS
    zw = _tile(math.gcd(di, z_col0), di, LANES)
    nz, zoff = di // zw, z_col0 // zw

    def body(*refs):
        dn_ref, y_ref, z_refs, g_ref = refs[0], refs[1], refs[2:2 + nz], refs[2 + nz]
        dy_ref, dz_ref, dg_ref = refs[3 + nz:]
        first = pl.program_id(0) == 0
        for g in range(SSM_GROUPS):
            lo = g * gw
            zv = z_refs[lo // zw][:, lo % zw:lo % zw + gw].astype(F32)
            yv = y_ref[:, lo:lo + gw]
            sz = _silu(zv)
            yg = yv * sz
            r = lax.rsqrt(jnp.mean(yg * yg, axis=-1, keepdims=True) + EPS)
            yh = yg * r
            dnv = dn_ref[:, lo:lo + gw].astype(F32)
            gy = dnv * g_ref[:, lo:lo + gw]
            dyg = r * (gy - yh * jnp.mean(gy * yh, axis=-1, keepdims=True))
            dy_ref[:, lo:lo + gw] = dyg * sz
            dz_ref[:, lo:lo + gw] = (dyg * yv * _dsilu(zv)).astype(BF16)
            part = jnp.sum(dnv * yh, axis=0, keepdims=True)

            @pl.when(first)
            def _():
                dg_ref[:, lo:lo + gw] = part

            @pl.when(jnp.logical_not(first))
            def _():
                dg_ref[:, lo:lo + gw] += part

    row = pl.BlockSpec((tt, di), lambda i: (i, 0))
    vec = pl.BlockSpec((1, di), lambda i: (0, 0))
    zspecs = [pl.BlockSpec((tt, zw), functools.partial(lambda i, k: (i, zoff + k), k=k)) for k in range(nz)]
    return pl.pallas_call(
        body, name=name, grid=(t // tt,), in_specs=[row, row] + zspecs + [vec], out_specs=[row, row, vec],
        out_shape=[jax.ShapeDtypeStruct((t, di), F32), jax.ShapeDtypeStruct((t, di), BF16), jax.ShapeDtypeStruct((1, di), F32)],
        compiler_params=_params("arbitrary"),
    )(dn, y, *([proj] * nz), norm_g)


def _softmax_rows(s):
    s = s - jnp.max(s, axis=-1, keepdims=True)
    e = jnp.exp(s)
    return e / jnp.sum(e, axis=-1, keepdims=True)


def _xattn_fwd(q, kv, nb, s, m, d, name, tq=512):
    tq = _tile(s, tq, 8)
    nq = s // tq
    hd = d // XATTN_HEADS
    scale = 1.0 / math.sqrt(hd)

    def body(q_ref, k_ref, v_ref, o_ref):
        for h in range(XATTN_HEADS):
            sl = slice(h * hd, (h + 1) * hd)
            prob = _softmax_rows(_dot(q_ref[:, sl], k_ref[:, sl], NT) * scale)
            o_ref[:, sl] = _dot(prob.astype(BF16), v_ref[:, sl]).astype(BF16)

    return pl.pallas_call(
        body, name=name, grid=(nb, nq),
        in_specs=[pl.BlockSpec((tq, d), lambda b, i: (b * nq + i, 0)), pl.BlockSpec((m, d), lambda b, i: (b, 0)),
                  pl.BlockSpec((m, d), lambda b, i: (b, 1))],
        out_specs=pl.BlockSpec((tq, d), lambda b, i: (b * nq + i, 0)),
        out_shape=jax.ShapeDtypeStruct((nb * s, d), BF16), compiler_params=_params("parallel", "parallel"),
    )(q, kv, kv)


def _xattn_bwd(do, q, kv, nb, s, m, d, name, tq=512):
    tq = _tile(s, tq, 8)
    nq = s // tq
    hd = d // XATTN_HEADS
    scale = 1.0 / math.sqrt(hd)

    def body(do_ref, q_ref, k_ref, v_ref, dq_ref, dk_ref, dv_ref):
        first = pl.program_id(1) == 0
        for h in range(XATTN_HEADS):
            sl = slice(h * hd, (h + 1) * hd)
            qh, kh, vh, doh = q_ref[:, sl], k_ref[:, sl], v_ref[:, sl], do_ref[:, sl]
            prob = _softmax_rows(_dot(qh, kh, NT) * scale)
            dv_h = _dot(prob.astype(BF16), doh, TN)
            dp = _dot(doh, vh, NT)
            ds = (prob * (dp - jnp.sum(dp * prob, axis=-1, keepdims=True)) * scale).astype(BF16)
            dq_ref[:, sl] = _dot(ds, kh).astype(BF16)
            dk_h = _dot(ds, qh, TN)

            @pl.when(first)
            def _():
                dk_ref[:, sl] = dk_h
                dv_ref[:, sl] = dv_h

            @pl.when(jnp.logical_not(first))
            def _():
                dk_ref[:, sl] += dk_h
                dv_ref[:, sl] += dv_h

    qspec = pl.BlockSpec((tq, d), lambda b, i: (b * nq + i, 0))
    dq, dk, dv = pl.pallas_call(
        body, name=name, grid=(nb, nq),
        in_specs=[qspec, qspec, pl.BlockSpec((m, d), lambda b, i: (b, 0)), pl.BlockSpec((m, d), lambda b, i: (b, 1))],
        out_specs=[qspec, pl.BlockSpec((m, d), lambda b, i: (b, 0)), pl.BlockSpec((m, d), lambda b, i: (b, 0))],
        out_shape=[jax.ShapeDtypeStruct((nb * s, d), BF16), jax.ShapeDtypeStruct((nb * m, d), F32),
                   jax.ShapeDtypeStruct((nb * m, d), F32)],
        compiler_params=_params("parallel", "arbitrary"),
    )(do, q, kv, kv)
    return dq, dk, dv


def _all_gather(shards, name):
    n_arr = len(shards)

    def body(*refs):
        x_refs, out_refs = refs[:n_arr], refs[n_arr:2 * n_arr]
        send_sems, recv_sems, local_sems = refs[2 * n_arr:]
        x, y, c = lax.axis_index("x"), lax.axis_index("y"), lax.axis_index("c")
        me, sibling = (x, y, c), (x, y, 1 - c)
        chips = [(1 - x, y), (x, 1 - y), (1 - x, 1 - y)]

        def copy(w, k, block, to, from_input=False):
            px, py, pc = block
            rows = out_refs[w].at[4 * px + 2 * py + pc]
            return pltpu.make_async_remote_copy(
                src_ref=x_refs[w] if from_input else rows, dst_ref=rows,
                send_sem=send_sems.at[7 * w + k], recv_sem=recv_sems.at[7 * w + k], device_id=to, device_id_type=MESH)

        started = []
        for w in range(n_arr):
            mine = pltpu.make_async_copy(x_refs[w], out_refs[w].at[4 * x + 2 * y + c], local_sems.at[w])
            mine.start()
            started.append(mine)
        sends = []
        for w in range(n_arr):
            sends.append(copy(w, 0, me, sibling, from_input=True))
            sends += [copy(w, 1 + j, me, (*chip, c), from_input=True) for j, chip in enumerate(chips)]
        for cp in sends:
            cp.start()
        for j, chip in enumerate(chips):
            for w in range(n_arr):
                copy(w, 1 + j, (*chip, c), me).wait_recv()
                passed = copy(w, 4 + j, (*chip, c), sibling)
                passed.start()
                sends.append(passed)
        for w in range(n_arr):
            copy(w, 0, sibling, me).wait_recv()
            for j, chip in enumerate(chips):
                copy(w, 4 + j, (*chip, 1 - c), me).wait_recv()
        for cp in sends:
            cp.wait_send()
        for mine in started:
            mine.wait()

    hbm = pl.BlockSpec(memory_space=pl.ANY)
    return pl.pallas_call(
        body, name=name, out_shape=[jax.ShapeDtypeStruct((N_DEV,) + s.shape, s.dtype) for s in shards],
        in_specs=[hbm] * n_arr, out_specs=[hbm] * n_arr,
        scratch_shapes=[pltpu.SemaphoreType.DMA((7 * n_arr,)), pltpu.SemaphoreType.DMA((7 * n_arr,)),
                        pltpu.SemaphoreType.DMA((n_arr,))],
    )(*shards)


_HBM = pl.BlockSpec(memory_space=pltpu.HBM)
_SEM = pl.BlockSpec(memory_space=pltpu.SEMAPHORE)
_DATAFLOW = pltpu.SideEffectType.DATAFLOW_SIDE_EFFECTING


def _peer_list(x, y, c):
    return [(1 - x if k & 4 else x, 1 - y if k & 2 else y, 1 - c if k & 1 else c) for k in range(1, N_DEV)]


def _push_copy(src_ref, land_ref, send_sems, recv_sems, w, k, peer, me, per_peer_src, receiving):
    px, py, pc = peer
    peer_slot = 4 * px + 2 * py + pc
    return pltpu.make_async_remote_copy(
        src_ref=src_ref.at[peer_slot] if per_peer_src else src_ref,
        dst_ref=land_ref.at[peer_slot if receiving else me],
        send_sem=send_sems.at[7 * w + k], recv_sem=recv_sems.at[7 * w + k], device_id=peer, device_id_type=MESH)


def _push_start(srcs, per_peer_src, after, name):
    n_arr = len(srcs)
    land_shapes = [s.shape if per_peer_src else (N_DEV,) + s.shape for s in srcs]

    def body(*refs):
        src_refs, land_refs = refs[:n_arr], refs[n_arr:2 * n_arr]
        send_sems, recv_sems = refs[2 * n_arr + 1], refs[2 * n_arr + 2]
        token = refs[-1]
        x, y, c = lax.axis_index("x"), lax.axis_index("y"), lax.axis_index("c")
        me = 4 * x + 2 * y + c
        for w in range(n_arr):
            for k, peer in enumerate(_peer_list(x, y, c)):
                _push_copy(src_refs[w], land_refs[w], send_sems, recv_sems, w, k, peer, me, per_peer_src, False).start()
        token[...] = jnp.zeros_like(token)

    lands = [pltpu.with_memory_space_constraint(lax.empty(ls, s.dtype), pltpu.HBM) for ls, s in zip(land_shapes, srcs)]
    srcs_hbm = [pltpu.with_memory_space_constraint(s, pltpu.HBM) for s in srcs]
    out = pl.pallas_call(
        body, name=name,
        out_shape=(pltpu.SemaphoreType.DMA((7 * n_arr,)), pltpu.SemaphoreType.DMA((7 * n_arr,)),
                   *[pltpu.HBM(s.shape, s.dtype) for s in srcs], *[pltpu.HBM(ls, s.dtype) for ls, s in zip(land_shapes, srcs)],
                   jax.ShapeDtypeStruct((8, LANES), F32)),
        in_specs=[_HBM] * (2 * n_arr) + [pl.BlockSpec(memory_space=pl.ANY)],
        out_specs=(_SEM, _SEM, *([_HBM] * (2 * n_arr)), pl.BlockSpec(memory_space=pltpu.VMEM)),
        input_output_aliases={i: 2 + i for i in range(2 * n_arr)},
        compiler_params=pltpu.CompilerParams(has_side_effects=_DATAFLOW),
    )(*srcs_hbm, *lands, after)
    return dict(send=out[0], recv=out[1], srcs=list(out[2:2 + n_arr]), lands=list(out[2 + n_arr:2 + 2 * n_arr]),
                token=out[-1])


def _push_wait(pending, per_peer_src, after, name):
    n_arr = len(pending["srcs"])

    def body(*refs):
        src_refs, land_refs = refs[:n_arr], refs[n_arr:2 * n_arr]
        send_sems, recv_sems = refs[2 * n_arr], refs[2 * n_arr + 1]
        x, y, c = lax.axis_index("x"), lax.axis_index("y"), lax.axis_index("c")
        me = 4 * x + 2 * y + c
        for w in range(n_arr):
            for k, peer in enumerate(_peer_list(x, y, c)):
                cp = _push_copy(src_refs[w], land_refs[w], send_sems, recv_sems, w, k, peer, me, per_peer_src, True)
                cp.wait_send()
                cp.wait_recv()

    out = pl.pallas_call(
        body, name=name,
        out_shape=tuple(pltpu.HBM(a.shape, a.dtype) for a in pending["srcs"] + pending["lands"]),
        in_specs=[_HBM] * (2 * n_arr) + [_SEM, _SEM, pl.BlockSpec(memory_space=pl.ANY)],
        out_specs=tuple([_HBM] * (2 * n_arr)),
        input_output_aliases={i: i for i in range(2 * n_arr)},
        compiler_params=pltpu.CompilerParams(has_side_effects=_DATAFLOW),
    )(*pending["srcs"], *pending["lands"], pending["send"], pending["recv"], after)
    return list(out[:n_arr]), list(out[n_arr:])


def _adamw_math(w, g, m, v):
    m = ADAM_B1 * m + (1.0 - ADAM_B1) * g
    v = ADAM_B2 * v + (1.0 - ADAM_B2) * (g * g)
    m_hat = m / (1.0 - ADAM_B1 ** ADAM_STEP)
    v_hat = v / (1.0 - ADAM_B2 ** ADAM_STEP)
    delta = -ADAM_LR * (m_hat / (jnp.sqrt(v_hat) + ADAM_EPS) + ADAM_WD * w)
    return delta, m, v


def _sum8(parts, name, tr=512):
    _, r, c_dim = parts.shape
    tr = _tile(r, tr, BF16_SUBLANES)

    def body(p_ref, o_ref):
        acc = p_ref[0].astype(F32)
        for k in range(1, N_DEV):
            acc = acc + p_ref[k].astype(F32)
        o_ref[...] = acc

    return pl.pallas_call(
        body, name=name, grid=(r // tr,), in_specs=[pl.BlockSpec((N_DEV, tr, c_dim), lambda i: (0, i, 0))],
        out_specs=pl.BlockSpec((tr, c_dim), lambda i: (i, 0)),
        out_shape=jax.ShapeDtypeStruct((r, c_dim), F32), compiler_params=_params("parallel"),
    )(parts)


def _sum8_adamw(parts, w, m, v, name, tr=128):
    _, r, c_dim = parts.shape
    tr = _tile(r, tr, BF16_SUBLANES)
    tc = c_dim if tr <= 2 * LANES else _tile(c_dim, LANES, LANES)

    def body(p_ref, w_ref, m_ref, v_ref, g_ref, d_ref, nm_ref, nv_ref):
        g = p_ref[0].astype(F32)
        for k in range(1, N_DEV):
            g = g + p_ref[k].astype(F32)
        g_ref[...] = g
        d_ref[...], nm_ref[...], nv_ref[...] = _adamw_math(w_ref[...], g, m_ref[...], v_ref[...])

    blk = pl.BlockSpec((None, tr, tc), lambda i, j: (0, i, j))
    out = jax.ShapeDtypeStruct((1, r, c_dim), F32)
    return pl.pallas_call(
        body, name=name, grid=(r // tr, c_dim // tc),
        in_specs=[pl.BlockSpec((N_DEV, tr, tc), lambda i, j: (0, i, j)), blk, blk, blk],
        out_specs=[blk] * 4, out_shape=[out] * 4, compiler_params=_params("parallel", "parallel"),
    )(parts, w, m, v)


def _adamw(g, w, m, v, name):
    r, c_dim = g.shape

    def body(g_ref, w_ref, m_ref, v_ref, d_ref, nm_ref, nv_ref):
        d_ref[...], nm_ref[...], nv_ref[...] = _adamw_math(w_ref[...], g_ref[...], m_ref[...], v_ref[...])

    out = jax.ShapeDtypeStruct((r, c_dim), F32)
    return pl.pallas_call(body, name=name, out_shape=[out] * 3)(g, w, m, v)


def _pack_rows(arrays, dtype, row_unit):
    chunks, offs, r0 = [], [], 0
    for a in arrays:
        flat = a.reshape(-1).astype(dtype)
        rows = -(-flat.shape[0] // (LANES * row_unit)) * row_unit
        flat = jnp.pad(flat, (0, rows * LANES - flat.shape[0]))
        chunks.append(flat.reshape(rows, LANES))
        offs.append((r0, rows))
        r0 += rows
    return jnp.concatenate(chunks, axis=0), offs


def _unpack_rows(packed, offs, shapes):
    out = []
    for (r0, rows), shape in zip(offs, shapes):
        n = math.prod(shape)
        blk = packed[..., r0:r0 + rows, :]
        blk = blk.reshape(packed.shape[:-2] + (rows * LANES,))[..., :n]
        out.append(blk.reshape(packed.shape[:-2] + tuple(shape)))
    return out


def _full_from_slots(blk, col_sharded):
    _, r, c = blk.shape
    if col_sharded:
        return blk.transpose(1, 0, 2).reshape(r, N_DEV * c)
    return blk.reshape(N_DEV * r, c)


def _slots_from_full(full, col_sharded):
    r, c = full.shape
    if col_sharded:
        return full.reshape(r, N_DEV, c // N_DEV).transpose(1, 0, 2)
    return full.reshape(N_DEV, r // N_DEV, c)


def _ffn_fwd(h, g, w_gu_t, w_d, tag):
    n = _rms_fwd(h, g, f"{tag}_norm")
    gate, up, a = _ffn_up(n, w_gu_t, f"{tag}_up")
    h_out = _mm(a, w_d, F32, f"{tag}_down", res=h, alpha=FFN_RES_WEIGHT)
    return h_out, (h, n, gate, up, a)


def _ffn_bwd(dh_out, saved, g, w_gu_t, w_d, tag, dep, send_grads):
    h, n, gate, up, a = saved
    dw_d = _mm_tn(a, dh_out, f"{tag}_dw_down", alpha=FFN_RES_WEIGHT, dep=dep)
    dgate, dup = _ffn_da(dh_out, w_d, gate, up, f"{tag}_da", FFN_RES_WEIGHT)
    dw_gu_t = jnp.concatenate([_mm_tn(dgate, n, f"{tag}_dw_gate"), _mm_tn(dup, n, f"{tag}_dw_up")], axis=0)
    dep = send_grads(dw_gu_t, dw_d)
    dn = _mm([dgate, dup], w_gu_t, F32, f"{tag}_dn", dep=dep, tk=1408)
    dh, dg = _rms_bwd(dn, h, g, f"{tag}_dnorm", dres=dh_out)
    return dh, dg


W_GROUPS = (("ffn1_w_gate_up", "ffn1_w_down"),
            ("w_in",),
            ("w_out_a", "w_out_ssm", "w_mix_out"),
            ("w_q", "w_kv", "w_o_x", "ffn2_w_gate_up", "ffn2_w_down"))
G_GROUPS = (("ffn2_w_gate_up", "ffn2_w_down"),
            ("w_o_x", "w_q", "w_kv", "w_mix_out", "w_out_a", "w_out_ssm", "w_in"),
            ("ffn1_w_gate_up", "ffn1_w_down"))


def _local_step(x3, mem3, target3, small, comm):
    nb, s, d = x3.shape
    m_len = mem3.shape[1]
    t = nb * s
    nc = s // SSM_CHUNK
    di = small["ssm_norm"].shape[1]
    hs = di // SSM_HEAD_DIM
    cc = di + 2 * SSM_GROUPS * SSM_STATE
    x, mem, target = x3.reshape(t, d), mem3.reshape(nb * m_len, d), target3.reshape(t, d)

    sizes = (d, d, d, di, cc, hs, d, d)
    offs = [0]
    for sz in sizes:
        offs.append(offs[-1] + sz)
    z_col0, xbc_col0 = 3 * d, 3 * d + di
    ga_blk, gb_blk = (3 * d + di + cc) // d, (4 * d + di + cc) // d

    pad_vec = lambda v: jnp.pad(v.reshape(1, -1), ((0, 0), (0, LANES - hs)))
    prow = jnp.concatenate([pad_vec(small["ssm_dt_bias"]), pad_vec(small["ssm_a_log"]), pad_vec(small["ssm_d"]),
                            jnp.zeros((5, LANES), F32)], axis=0)
    pcol = prow.T
    e_mat = (lax.broadcasted_iota(jnp.int32, (LANES, di), 0)
             == lax.broadcasted_iota(jnp.int32, (LANES, di), 1) // SSM_HEAD_DIM).astype(BF16)
    conv_a_w8 = jnp.pad(small["conv_a_w"][0], ((0, 8 - CONV_A_K), (0, 0)))
    ssm_conv_w8 = jnp.pad(small["ssm_conv_w"][0], ((0, 8 - SSM_CONV_K), (0, 0)))

    wts, dep = comm.weights(0, None)
    h1, ffn1_saved = _ffn_fwd(x, small["ffn1_norm"] + dep[0, 0], wts["ffn1_w_gate_up"], wts["ffn1_w_down"], "ffn1")
    got, dep = comm.weights(1, h1)
    wts.update(got)
    w_in_t = wts["w_in"]
    w_main_t = jnp.concatenate([w_in_t[offs[i]:offs[i + 1]] for i in (0, 1, 2, 3, 4, 6, 7)], axis=0)
    w_dt_t = jnp.pad(w_in_t[offs[5]:offs[6]], ((0, LANES - hs), (0, 0)))
    u = _rms_fwd(h1, small["mix_norm"] + dep[0, 0], "mix_norm")
    proj = _mm(u, w_main_t, BF16, "in_proj", nt=True)
    dtr = _mm(u, w_dt_t, F32, "in_proj_dt", nt=True)
    yap = _conv_a_fwd(proj, conv_a_w8, nb, s, d, "conv_a")
    got, dep = comm.weights(2, yap)
    wts.update(got)
    y_a = _mm(yap, wts["w_out_a"], BF16, "out_a", dep=dep)
    xc, conv_pre = _conv_s_fwd(proj, xbc_col0, ssm_conv_w8, small["ssm_conv_b"] + dep[0, 0], nb, s, cc, "conv_s")
    dtrt = dtr.T
    y_ssd, sprev = _ssd_fwd(xc, dtr, dtrt, prow, pcol, nb, nc, di, "ssd")
    ygn = _gate_norm_fwd(y_ssd, proj, z_col0, small["ssm_norm"], di, "gate_norm")
    y_b = _mm(ygn, wts["w_out_ssm"], BF16, "out_ssm")
    merged = _merge_fwd(y_a, y_b, proj, ga_blk, gb_blk, d, "merge")
    h2 = _mm(merged, wts["w_mix_out"], F32, "mix_out", res=h1)
    got, _ = comm.weights(3, h2)
    wts.update(got)
    un = _rms_fwd(h2, small["xattn_norm"], "xattn_norm")
    mn = _rms_fwd(mem, small["mem_norm"], "mem_norm")
    q = _mm(un, wts["w_q"], BF16, "q_proj")
    kv = _mm(mn, wts["w_kv"], BF16, "kv_proj", nt=True)
    o = _xattn_fwd(q, kv, nb, s, m_len, d, "xattn")
    h3 = _mm(o, wts["w_o_x"], F32, "o_proj", res=h2)
    h4, ffn2_saved = _ffn_fwd(h3, small["ffn2_norm"], wts["ffn2_w_gate_up"], wts["ffn2_w_down"], "ffn2")
    loss_vec, dh4, dg_final = _loss_head(h4, small["final_norm"].reshape(1, d), target, "loss_head")

    grads = {"final_norm": dg_final.reshape(d)}
    big = {}
    dh3, grads["ffn2_norm"] = _ffn_bwd(
        dh4, ffn2_saved, small["ffn2_norm"], wts["ffn2_w_gate_up"], wts["ffn2_w_down"], "ffn2", None,
        lambda dw_gu_t, dw_d: comm.grads(0, {"ffn2_w_gate_up": dw_gu_t, "ffn2_w_down": dw_d}))
    big["w_o_x"] = _mm_tn(o, dh3, "dw_o")
    do = _mm(dh3, wts["w_o_x"], BF16, "d_o", nt=True)
    dq, dk, dv = _xattn_bwd(do, q, kv, nb, s, m_len, d, "xattn_bwd")
    big["w_q"] = _mm_tn(un, dq, "dw_q")
    big["w_kv"] = jnp.concatenate([_mm_tn(dk, mn, "dw_k"), _mm_tn(dv, mn, "dw_v")], axis=0)
    dun = _mm(dq, wts["w_q"], F32, "d_un", nt=True)
    dmn = _mm([dk, dv], wts["w_kv"], F32, "d_mn")
    _, grads["mem_norm"] = _rms_bwd(dmn, mem, small["mem_norm"], "mem_dnorm")
    dh2, grads["xattn_norm"] = _rms_bwd(dun, h2, small["xattn_norm"], "xattn_dnorm", dres=dh3)
    big["w_mix_out"] = _mm_tn(merged, dh2, "dw_mix")
    dmerged = _mm(dh2, wts["w_mix_out"], BF16, "d_merged", nt=True)
    dya, dyb, dga, dgb = _merge_bwd(dmerged, y_a, y_b, proj, ga_blk, gb_blk, d, "merge_bwd")
    big["w_out_a"] = _mm_tn(yap, dya, "dw_out_a")
    big["w_out_ssm"] = _mm_tn(ygn, dyb, "dw_out_ssm")
    dyap = _mm(dya, wts["w_out_a"], BF16, "d_yap", nt=True)
    dygn = _mm(dyb, wts["w_out_ssm"], BF16, "d_ygn", nt=True)
    dab, dac, dav, dconv_a = _conv_a_bwd(dyap, proj, conv_a_w8, nb, s, d, "conv_a_bwd")
    dy_ssd, dz, grads["ssm_norm"] = _gate_norm_bwd(dygn, y_ssd, proj, z_col0, small["ssm_norm"], di, "gate_norm_bwd")
    dxc, ddtr, ssd_sums = _ssd_bwd(dy_ssd, xc, dtr, dtrt, prow, pcol, e_mat, sprev, nb, nc, di, "ssd_bwd")
    dxbc, dconv_s, grads["ssm_conv_b"] = _conv_s_bwd(dxc, conv_pre, proj, xbc_col0, ssm_conv_w8, nb, s, cc, "conv_s_bwd")
    dpieces = [("ab", dab), ("ac", dac), ("av", dav), ("z", dz), ("xbc", dxbc), ("ga", dga), ("gb", dgb)]
    dw = {tag: _mm_tn(piece, u, f"dw_in_{tag}") for tag, piece in dpieces}
    dw_dt = _mm_tn(ddtr, u, "dw_in_dt")[:hs]
    du_main = _mm([piece for _, piece in dpieces], w_main_t, F32, "d_u", tk=1024)
    du = _mm(ddtr, w_dt_t, F32, "d_u_dt", res=du_main)
    dh1, grads["mix_norm"] = _rms_bwd(du, h1, small["mix_norm"], "mix_dnorm", dres=dh2)
    big["w_in"] = jnp.concatenate([dw["ab"], dw["ac"], dw["av"], dw["z"], dw["xbc"], dw_dt, dw["ga"], dw["gb"]], axis=0)
    dep = comm.grads(1, big)
    dx, grads["ffn1_norm"] = _ffn_bwd(
        dh1, ffn1_saved, small["ffn1_norm"], wts["ffn1_w_gate_up"], wts["ffn1_w_down"], "ffn1", dep,
        lambda dw_gu_t, dw_d: comm.grads(2, {"ffn1_w_gate_up": dw_gu_t, "ffn1_w_down": dw_d}))

    grads["conv_a_w"] = dconv_a[:CONV_A_K]
    grads["ssm_conv_w"] = dconv_s[:SSM_CONV_K]
    grads["ssm_dt_bias"] = ssd_sums[0:1, :hs]
    grads["ssm_a_log"] = ssd_sums[1:2, :hs]
    grads["ssm_d"] = ssd_sums[2:3, :hs]
    return loss_vec[0, 0], dx.reshape(nb, s, d), grads


def _step(inputs):
    w = {k: inputs[k] for k in WEIGHT_ORDER}
    mom = {k: inputs["m_" + k] for k in WEIGHT_ORDER}
    vel = {k: inputs["v_" + k] for k in WEIGHT_ORDER}
    me = 4 * lax.axis_index("x") + 2 * lax.axis_index("y") + lax.axis_index("c")

    send = {k: (w[k][0].T if k in COL_SHARDED else w[k][0]).astype(BF16) for k in BIG_WEIGHTS}

    def own_slot(land, mine):
        return lax.dynamic_update_slice(land, mine[None], (me, 0, 0))

    gathers, exchanges = {}, {}

    def weights(i, after):
        if i == 0:
            lands = _all_gather([send[k] for k in W_GROUPS[0]], "gather0")
        else:
            sent, lands = _push_wait(gathers[i], False, after, f"gather{i}_wait")
            lands = [own_slot(land, mine) for land, mine in zip(lands, sent)]
        full = {k: land.reshape(N_DEV * land.shape[1], land.shape[2]) for k, land in zip(W_GROUPS[i], lands)}
        dep = jnp.zeros((8, LANES), F32)
        if i + 1 < len(W_GROUPS):
            gathers[i + 1] = _push_start([send[k] for k in W_GROUPS[i + 1]], False, lands[0], f"gather{i + 1}_start")
            dep = gathers[i + 1]["token"]
        return full, dep

    def send_grads(i, by_name):
        slots = [by_name[k].reshape((N_DEV,) + send[k].shape) for k in G_GROUPS[i]]
        exchanges[i] = _push_start(slots, True, slots[0], f"exchange{i}_start")
        return exchanges[i]["token"]

    comm = types.SimpleNamespace(weights=weights, grads=send_grads)
    small = {k: w[k] for k in SMALL_REPLICATED}
    conv_shapes = [w[k].shape[1:] for k in SMALL_SHARDED]
    packed_c, conv_offs = _pack_rows([w[k][0] for k in SMALL_SHARDED], F32, 8)
    conv_blocks = _unpack_rows(_all_gather([packed_c], "gather_conv_weights")[0], conv_offs, conv_shapes)
    for k, b in zip(SMALL_SHARDED, conv_blocks):
        small[k] = _full_from_slots(b, True)[None]

    loss_local, grad_x, grads = _local_step(inputs["x"], inputs["mem"], inputs["loss_target"], small, comm)
    loss = lax.psum(loss_local, AXES)

    out = {}
    for i, names in enumerate(G_GROUPS):
        sent, lands = _push_wait(exchanges[i], True, grad_x, f"exchange{i}_wait")
        for k, land, slots in zip(names, lands, sent):
            parts = own_slot(land, lax.dynamic_index_in_dim(slots, me, 0, keepdims=False))
            if k in COL_SHARDED and w[k].shape[2] % LANES:
                flip = lambda a: a.transpose(0, 2, 1)
                out[k] = tuple(flip(o) for o in _sum8_adamw(parts, flip(w[k]), flip(mom[k]), flip(vel[k]), f"sum_adamw_{k}"))
                continue
            if k in COL_SHARDED:
                parts = parts.transpose(0, 2, 1)
            out[k] = tuple(_sum8_adamw(parts, w[k], mom[k], vel[k], f"sum_adamw_{k}"))

    small_names = SMALL_REPLICATED + SMALL_SHARDED
    packed_g, small_offs = _pack_rows([grads[k] for k in small_names], F32, 8)
    total = _sum8(_all_gather([packed_g], "gather_small_grads")[0], "sum_small_grads")
    full_grads = _unpack_rows(total, small_offs, [grads[k].shape for k in small_names])
    mine = {}
    for k, g in zip(small_names, full_grads):
        if k in SMALL_SHARDED:
            c_loc = w[k].shape[2]
            g = lax.dynamic_slice_in_dim(g, me * c_loc, c_loc, axis=1)
        mine[k] = g.reshape(w[k].shape)
    sg, s_offs = _pack_rows([mine[k] for k in small_names], F32, 8)
    sw, _ = _pack_rows([w[k] for k in small_names], F32, 8)
    sm, _ = _pack_rows([mom[k] for k in small_names], F32, 8)
    sv, _ = _pack_rows([vel[k] for k in small_names], F32, 8)
    s_shapes = [w[k].shape for k in small_names]
    small_out = [_unpack_rows(a, s_offs, s_shapes) for a in _adamw(sg, sw, sm, sv, "adamw_small")]
    for i, k in enumerate(small_names):
        out[k] = (mine[k],) + tuple(o[i] for o in small_out)

    res = [loss, grad_x]
    for j in range(4):
        res += [out[k][j] for k in WEIGHT_ORDER]
    return tuple(res)


def kernel(x, mem, ffn1_norm, ffn1_w_gate_up, ffn1_w_down, mix_norm, w_in, conv_a_w, w_out_a, ssm_conv_w, ssm_conv_b, ssm_dt_bias, ssm_a_log, ssm_d, ssm_norm, w_out_ssm, w_mix_out, xattn_norm, mem_norm, w_q, w_kv, w_o_x, ffn2_norm, ffn2_w_gate_up, ffn2_w_down, final_norm, loss_target, m_ffn1_norm, m_ffn1_w_gate_up, m_ffn1_w_down, m_mix_norm, m_w_in, m_conv_a_w, m_w_out_a, m_ssm_conv_w, m_ssm_conv_b, m_ssm_dt_bias, m_ssm_a_log, m_ssm_d, m_ssm_norm, m_w_out_ssm, m_w_mix_out, m_xattn_norm, m_mem_norm, m_w_q, m_w_kv, m_w_o_x, m_ffn2_norm, m_ffn2_w_gate_up, m_ffn2_w_down, m_final_norm, v_ffn1_norm, v_ffn1_w_gate_up, v_ffn1_w_down, v_mix_norm, v_w_in, v_conv_a_w, v_w_out_a, v_ssm_conv_w, v_ssm_conv_b, v_ssm_dt_bias, v_ssm_a_log, v_ssm_d, v_ssm_norm, v_w_out_ssm, v_w_mix_out, v_xattn_norm, v_mem_norm, v_w_q, v_w_kv, v_w_o_x, v_ffn2_norm, v_ffn2_w_gate_up, v_ffn2_w_down, v_final_norm):
    return _step(dict(locals()))
```

```python
import functools
import math
import types

import jax
import jax.numpy as jnp
from jax import lax
from jax.experimental import pallas as pl
from jax.experimental.pallas import tpu as pltpu

F32, BF16 = jnp.float32, jnp.bfloat16
HI = lax.Precision.HIGHEST
MESH = pl.DeviceIdType.MESH
AXES = ("x", "y", "c")
N_DEV = 8

EPS = 1e-6
FFN_RES_WEIGHT = 0.5
SSM_HEAD_DIM = 64
SSM_GROUPS = 4
SSM_STATE = 128
SSM_CHUNK = 128
CONV_A_K = 3
SSM_CONV_K = 4
XATTN_HEADS = 4
ADAM_LR, ADAM_B1, ADAM_B2, ADAM_EPS, ADAM_WD, ADAM_STEP = 1e-3, 0.9, 0.999, 1e-8, 0.01, 10

LANES = 128
BF16_SUBLANES = 16
VMEM_LIMIT_BYTES = 56 * 2 ** 20
NEG_BIG = -1e30

BIG_WEIGHTS = ("ffn1_w_gate_up", "ffn1_w_down", "w_in", "w_out_a", "w_out_ssm", "w_mix_out",
               "w_q", "w_kv", "w_o_x", "ffn2_w_gate_up", "ffn2_w_down")
COL_SHARDED = ("ffn1_w_gate_up", "w_in", "w_kv", "ffn2_w_gate_up")
SMALL_REPLICATED = ("ffn1_norm", "mix_norm", "ssm_conv_b", "ssm_dt_bias", "ssm_a_log", "ssm_d", "ssm_norm",
                    "xattn_norm", "mem_norm", "ffn2_norm", "final_norm")
SMALL_SHARDED = ("conv_a_w", "ssm_conv_w")
WEIGHT_ORDER = ("ffn1_norm", "ffn1_w_gate_up", "ffn1_w_down", "mix_norm", "w_in", "conv_a_w", "w_out_a",
                "ssm_conv_w", "ssm_conv_b", "ssm_dt_bias", "ssm_a_log", "ssm_d", "ssm_norm", "w_out_ssm",
                "w_mix_out", "xattn_norm", "mem_norm", "w_q", "w_kv", "w_o_x", "ffn2_norm", "ffn2_w_gate_up",
                "ffn2_w_down", "final_norm")


def _tile(dim, pref, unit):
    best = None
    t = unit
    while t <= min(dim, pref):
        if dim % t == 0:
            best = t
        t += unit
    return best if best is not None else dim


def _params(*sem):
    return pltpu.CompilerParams(dimension_semantics=sem, vmem_limit_bytes=VMEM_LIMIT_BYTES)


def _sigmoid(x):
    return pl.reciprocal(1.0 + jnp.exp(-x), approx=True)


def _silu(x):
    return x * _sigmoid(x)


def _dsilu(x):
    s = _sigmoid(x)
    return s * (1.0 + x * (1.0 - s))


def _softplus(x):
    return jnp.maximum(x, 0.0) + jnp.log(1.0 + jnp.exp(-jnp.abs(x)))


def _dot(a, b, dims=(((1,), (0,)), ((), ())), precision=None):
    return lax.dot_general(a, b, dims, preferred_element_type=F32, precision=precision)


def _stack_rows(rows, width):
    r_idx = lax.broadcasted_iota(jnp.int32, (8, width), 0)
    acc = jnp.zeros((8, width), F32)
    for k, row in enumerate(rows):
        acc = jnp.where(r_idx == k, row, acc)
    return acc


NT = (((1,), (1,)), ((), ()))
TN = (((0,), (0,)), ((), ()))


def _mm(a, b, out_dtype, name, res=None, alpha=1.0, nt=False, dep=None, tm=1024, tn=2048, tk=2816):
    pieces = list(a) if isinstance(a, (list, tuple)) else [a]
    m = pieces[0].shape[0]
    k = sum(p.shape[1] for p in pieces)
    n = b.shape[0] if nt else b.shape[1]
    assert (b.shape[1] if nt else b.shape[0]) == k
    tm, tn = _tile(m, tm, 8), _tile(n, tn, LANES)
    tk = _tile(math.gcd(*[p.shape[1] for p in pieces]), tk, LANES)
    nk = k // tk
    starts, s0 = [], 0
    for p in pieces:
        starts.append((s0, p.shape[1] // tk))
        s0 += p.shape[1] // tk
    n_p = len(pieces)

    def body(*refs):
        a_refs, b_ref = refs[:n_p], refs[n_p]
        r_ref = refs[n_p + 1] if res is not None else None
        n_in = n_p + 1 + (res is not None) + (dep is not None)
        o_ref = refs[n_in]
        scr = refs[n_in + 1:]

        def finish(acc):
            acc = alpha * acc if alpha != 1.0 else acc
            if r_ref is not None:
                acc = r_ref[...] + acc
            o_ref[...] = acc.astype(out_dtype)

        def product(a_ref):
            return _dot(a_ref[...].astype(BF16), b_ref[...].astype(BF16), NT if nt else (((1,), (0,)), ((), ())))

        if nk == 1:
            finish(product(a_refs[0]))
            return
        acc_ref = scr[0]
        kk = pl.program_id(2)
        for (s, cnt), a_ref in zip(starts, a_refs):
            if s == 0:
                @pl.when(kk == 0)
                def _():
                    acc_ref[...] = product(a_ref)

                @pl.when(jnp.logical_and(kk > 0, kk < cnt))
                def _():
                    acc_ref[...] += product(a_ref)
            else:
                @pl.when(jnp.logical_and(kk >= s, kk < s + cnt))
                def _():
                    acc_ref[...] += product(a_ref)

        @pl.when(kk == nk - 1)
        def _():
            finish(acc_ref[...])

    def a_spec(s, cnt):
        return pl.BlockSpec((tm, tk), lambda i, j, kk: (i, jnp.clip(kk - s, 0, cnt - 1)))

    in_specs = [a_spec(s, cnt) for s, cnt in starts]
    in_specs.append(pl.BlockSpec((tn, tk), lambda i, j, kk: (j, kk)) if nt else pl.BlockSpec((tk, tn), lambda i, j, kk: (kk, j)))
    args = pieces + [b]
    if res is not None:
        in_specs.append(pl.BlockSpec((tm, tn), lambda i, j, kk: (i, j)))
        args.append(res)
    if dep is not None:
        in_specs.append(pl.BlockSpec((8, LANES), lambda i, j, kk: (0, 0)))
        args.append(dep)
    return pl.pallas_call(
        body, name=name, grid=(m // tm, n // tn, nk), in_specs=in_specs,
        out_specs=pl.BlockSpec((tm, tn), lambda i, j, kk: (i, j)),
        out_shape=jax.ShapeDtypeStruct((m, n), out_dtype),
        scratch_shapes=[pltpu.VMEM((tm, tn), F32)] if nk > 1 else [],
        compiler_params=_params("parallel", "parallel", "arbitrary"),
    )(*args)


def _mm_tn(x, dy, name, out_dtype=BF16, alpha=1.0, dep=None, out_rows=None, row_off=0, into=None,
           tko=1408, tn=1024, tt=1024):
    t, k = x.shape
    n = dy.shape[1]
    tko, tn, tt = _tile(k, tko, LANES), _tile(n, tn, LANES), _tile(t, tt, 8)
    nt_steps = t // tt

    def body(*refs):
        x_ref, dy_ref = refs[:2]
        o_ref, acc_ref = refs[-2:]
        part = _dot(x_ref[...].astype(BF16), dy_ref[...].astype(BF16), TN)
        step = pl.program_id(2)

        @pl.when(step == 0)
        def _():
            acc_ref[...] = part

        @pl.when(step > 0)
        def _():
            acc_ref[...] += part

        @pl.when(step == nt_steps - 1)
        def _():
            acc = acc_ref[...]
            o_ref[...] = (alpha * acc if alpha != 1.0 else acc).astype(out_dtype)

    in_specs = [pl.BlockSpec((tt, tko), lambda i, j, s: (s, i)), pl.BlockSpec((tt, tn), lambda i, j, s: (s, j))]
    args = [x, dy]
    if dep is not None:
        in_specs.append(pl.BlockSpec((8, LANES), lambda i, j, s: (0, 0)))
        args.append(dep)
    aliases = {}
    if into is not None:
        in_specs.append(pl.BlockSpec(memory_space=pl.ANY))
        args.append(into)
        aliases = {len(args) - 1: 0}
    band = row_off // tko
    assert row_off % tko == 0
    return pl.pallas_call(
        body, name=name, grid=(k // tko, n // tn, nt_steps), in_specs=in_specs,
        out_specs=pl.BlockSpec((tko, tn), lambda i, j, s: (i + band, j)),
        out_shape=jax.ShapeDtypeStruct((out_rows or k, n), out_dtype),
        scratch_shapes=[pltpu.VMEM((tko, tn), F32)], input_output_aliases=aliases,
        compiler_params=_params("parallel", "parallel", "arbitrary"),
    )(*args)


def _rms_fwd(x, g, name, tt=512):
    t, d = x.shape
    tt = _tile(t, tt, 8)

    def body(x_ref, g_ref, o_ref):
        xv = x_ref[...]
        r = lax.rsqrt(jnp.mean(xv * xv, axis=-1, keepdims=True) + EPS)
        o_ref[...] = (xv * r * g_ref[...]).astype(BF16)

    return pl.pallas_call(
        body, name=name, grid=(t // tt,),
        in_specs=[pl.BlockSpec((tt, d), lambda i: (i, 0)), pl.BlockSpec((1, d), lambda i: (0, 0))],
        out_specs=pl.BlockSpec((tt, d), lambda i: (i, 0)),
        out_shape=jax.ShapeDtypeStruct((t, d), BF16), compiler_params=_params("parallel"),
    )(x, g)


def _rms_bwd(dn, x, g, name, dres=None, tt=512):
    t, d = x.shape
    tt = _tile(t, tt, 8)

    def body(*refs):
        if dres is None:
            dn_ref, x_ref, g_ref, dx_ref, dg_ref = refs
            r_ref = None
        else:
            dn_ref, x_ref, g_ref, r_ref, dx_ref, dg_ref = refs
        xv, dnv = x_ref[...], dn_ref[...].astype(F32)
        r = lax.rsqrt(jnp.mean(xv * xv, axis=-1, keepdims=True) + EPS)
        xh = xv * r
        gy = dnv * g_ref[...]
        dx = r * (gy - xh * jnp.mean(gy * xh, axis=-1, keepdims=True))
        if r_ref is not None:
            dx = dx + r_ref[...]
        dx_ref[...] = dx
        part = jnp.sum(dnv * xh, axis=0, keepdims=True)

        @pl.when(pl.program_id(0) == 0)
        def _():
            dg_ref[...] = part

        @pl.when(pl.program_id(0) > 0)
        def _():
            dg_ref[...] += part

    row = pl.BlockSpec((tt, d), lambda i: (i, 0))
    vec = pl.BlockSpec((1, d), lambda i: (0, 0))
    in_specs, args = [row, row, vec], [dn, x, g]
    if dres is not None:
        in_specs.append(row)
        args.append(dres)
    return pl.pallas_call(
        body, name=name, grid=(t // tt,), in_specs=in_specs, out_specs=[row, vec],
        out_shape=[jax.ShapeDtypeStruct((t, d), F32), jax.ShapeDtypeStruct((1, d), F32)],
        compiler_params=_params("arbitrary"),
    )(*args)


def _ffn_up(n, w_gu_t, name, tm=1024, tf=1408):
    t, d = n.shape
    f = w_gu_t.shape[0] // 2
    tm, tf = _tile(t, tm, 8), _tile(f, tf, LANES)
    nf = f // tf

    def body(n_ref, wg_ref, wu_ref, g_ref, u_ref, a_ref):
        nv = n_ref[...]
        gate, up = _dot(nv, wg_ref[...], NT), _dot(nv, wu_ref[...], NT)
        g_ref[...] = gate.astype(BF16)
        u_ref[...] = up.astype(BF16)
        a_ref[...] = (_silu(gate) * up).astype(BF16)

    blk = pl.BlockSpec((tm, tf), lambda i, j: (i, j))
    out = jax.ShapeDtypeStruct((t, f), BF16)
    return pl.pallas_call(
        body, name=name, grid=(t // tm, nf),
        in_specs=[pl.BlockSpec((tm, d), lambda i, j: (i, 0)), pl.BlockSpec((tf, d), lambda i, j: (j, 0)),
                  pl.BlockSpec((tf, d), lambda i, j: (j + nf, 0))],
        out_specs=[blk, blk, blk], out_shape=[out, out, out], compiler_params=_params("parallel", "parallel"),
    )(n, w_gu_t, w_gu_t)


def _ffn_da(dh, w_d, gate, up, name, alpha, dep=None, tm=1024, tf=1408):
    t, d = dh.shape
    f = w_d.shape[0]
    tm, tf = _tile(t, tm, 8), _tile(f, tf, LANES)

    def body(*refs):
        dh_ref, w_ref, g_ref, u_ref = refs[:4]
        dg_ref, du_ref = refs[-2:]
        da = alpha * _dot(dh_ref[...].astype(BF16), w_ref[...], NT)
        gv = g_ref[...].astype(F32)
        dg_ref[...] = (da * u_ref[...].astype(F32) * _dsilu(gv)).astype(BF16)
        du_ref[...] = (da * _silu(gv)).astype(BF16)

    blk = pl.BlockSpec((tm, tf), lambda i, j: (i, j))
    in_specs = [pl.BlockSpec((tm, d), lambda i, j: (i, 0)), pl.BlockSpec((tf, d), lambda i, j: (j, 0)), blk, blk]
    args = [dh, w_d, gate, up]
    if dep is not None:
        in_specs.append(pl.BlockSpec((8, LANES), lambda i, j: (0, 0)))
        args.append(dep)
    out = jax.ShapeDtypeStruct((t, f), BF16)
    return pl.pallas_call(
        body, name=name, grid=(t // tm, f // tf), in_specs=in_specs, out_specs=[blk, blk], out_shape=[out, out],
        compiler_params=_params("parallel", "parallel"),
    )(*args)


def _merge_fwd(ya, yb, proj, ga_blk, gb_blk, d, name, tt=512):
    t = ya.shape[0]
    tt = _tile(t, tt, 8)

    def body(ya_ref, yb_ref, ga_ref, gb_ref, o_ref):
        o_ref[...] = (_sigmoid(ga_ref[...].astype(F32)) * ya_ref[...].astype(F32)
                      + _sigmoid(gb_ref[...].astype(F32)) * yb_ref[...].astype(F32)).astype(BF16)

    row = pl.BlockSpec((tt, d), lambda i: (i, 0))
    return pl.pallas_call(
        body, name=name, grid=(t // tt,),
        in_specs=[row, row, pl.BlockSpec((tt, d), lambda i: (i, ga_blk)), pl.BlockSpec((tt, d), lambda i: (i, gb_blk))],
        out_specs=row, out_shape=jax.ShapeDtypeStruct((t, d), BF16), compiler_params=_params("parallel"),
    )(ya, yb, proj, proj)


def _merge_bwd(dm, ya, yb, proj, ga_blk, gb_blk, d, name, tt=512):
    t = ya.shape[0]
    tt = _tile(t, tt, 8)

    def body(dm_ref, ya_ref, yb_ref, ga_ref, gb_ref, dya_ref, dyb_ref, dga_ref, dgb_ref):
        dmv = dm_ref[...].astype(F32)
        sa, sb = _sigmoid(ga_ref[...].astype(F32)), _sigmoid(gb_ref[...].astype(F32))
        dya_ref[...] = (dmv * sa).astype(BF16)
        dyb_ref[...] = (dmv * sb).astype(BF16)
        dga_ref[...] = (dmv * ya_ref[...].astype(F32) * sa * (1.0 - sa)).astype(BF16)
        dgb_ref[...] = (dmv * yb_ref[...].astype(F32) * sb * (1.0 - sb)).astype(BF16)

    row = pl.BlockSpec((tt, d), lambda i: (i, 0))
    out = jax.ShapeDtypeStruct((t, d), BF16)
    return pl.pallas_call(
        body, name=name, grid=(t // tt,),
        in_specs=[row, row, row, pl.BlockSpec((tt, d), lambda i: (i, ga_blk)), pl.BlockSpec((tt, d), lambda i: (i, gb_blk))],
        out_specs=[row] * 4, out_shape=[out] * 4, compiler_params=_params("parallel"),
    )(dm, ya, yb, proj, proj)


def _loss_head(h, g, target, name, tt=512):
    t, d = h.shape
    tt = _tile(t, tt, 8)

    def body(h_ref, g_ref, tg_ref, loss_ref, dh_ref, dg_ref):
        xv = h_ref[...]
        r = lax.rsqrt(jnp.mean(xv * xv, axis=-1, keepdims=True) + EPS)
        xh = xv * r
        err = xh * g_ref[...] - tg_ref[...]
        dout = err * (1.0 / d)
        gy = dout * g_ref[...]
        dh_ref[...] = r * (gy - xh * jnp.mean(gy * xh, axis=-1, keepdims=True))
        dg_part = jnp.sum(dout * xh, axis=0, keepdims=True)
        loss_part = jnp.full((1, LANES), 0.5 / d, F32) * jnp.sum(err * err)

        @pl.when(pl.program_id(0) == 0)
        def _():
            dg_ref[...] = dg_part
            loss_ref[...] = loss_part

        @pl.when(pl.program_id(0) > 0)
        def _():
            dg_ref[...] += dg_part
            loss_ref[...] += loss_part

    row = pl.BlockSpec((tt, d), lambda i: (i, 0))
    vec = pl.BlockSpec((1, d), lambda i: (0, 0))
    return pl.pallas_call(
        body, name=name, grid=(t // tt,), in_specs=[row, vec, row],
        out_specs=[pl.BlockSpec((1, LANES), lambda i: (0, 0)), row, vec],
        out_shape=[jax.ShapeDtypeStruct((1, LANES), F32), jax.ShapeDtypeStruct((t, d), F32), jax.ShapeDtypeStruct((1, d), F32)],
        compiler_params=_params("arbitrary"),
    )(h, g, target)


def _shift_down(x, k, t_idx):
    if k == 0:
        return x
    return jnp.where(t_idx >= k, pltpu.roll(x, k, 0), 0.0)


def _shift_up(x, k, t_idx, s):
    if k == 0:
        return x
    return jnp.where(t_idx < s - k, pltpu.roll(x, s - k, 0), 0.0)


def _conv_a_fwd(proj, w, nb, s, d, name, cb=256):
    cb = _tile(d, cb, LANES)
    nd = d // cb

    def body(b_ref, c_ref, v_ref, w_ref, o_ref):
        t_idx = lax.broadcasted_iota(jnp.int32, (s, cb), 0)
        cv = c_ref[...].astype(F32) * v_ref[...].astype(F32)
        cc = sum(w_ref[k:k + 1, :] * _shift_down(cv, CONV_A_K - 1 - k, t_idx) for k in range(CONV_A_K))
        o_ref[...] = (b_ref[...].astype(F32) * cc).astype(BF16)

    def col(off):
        return pl.BlockSpec((s, cb), lambda b, j: (b, j + off * nd))

    return pl.pallas_call(
        body, name=name, grid=(nb, nd), in_specs=[col(0), col(1), col(2), pl.BlockSpec((8, cb), lambda b, j: (0, j))],
        out_specs=pl.BlockSpec((s, cb), lambda b, j: (b, j)),
        out_shape=jax.ShapeDtypeStruct((nb * s, d), BF16), compiler_params=_params("parallel", "parallel"),
    )(proj, proj, proj, w)


def _conv_a_bwd(dy, proj, w, nb, s, d, name, cb=256):
    cb = _tile(d, cb, LANES)
    nd = d // cb

    def body(dy_ref, b_ref, c_ref, v_ref, w_ref, db_ref, dc_ref, dv_ref, dw_ref):
        t_idx = lax.broadcasted_iota(jnp.int32, (s, cb), 0)
        cv_c, cv_v = c_ref[...].astype(F32), v_ref[...].astype(F32)
        cv = cv_c * cv_v
        shifted = [_shift_down(cv, CONV_A_K - 1 - k, t_idx) for k in range(CONV_A_K)]
        cc = sum(w_ref[k:k + 1, :] * shifted[k] for k in range(CONV_A_K))
        dyv = dy_ref[...].astype(F32)
        db_ref[...] = (dyv * cc).astype(BF16)
        dcc = dyv * b_ref[...].astype(F32)
        dcv = sum(w_ref[k:k + 1, :] * _shift_up(dcc, CONV_A_K - 1 - k, t_idx, s) for k in range(CONV_A_K))
        dc_ref[...] = (dcv * cv_v).astype(BF16)
        dv_ref[...] = (dcv * cv_c).astype(BF16)
        rows = [jnp.sum(dcc * shifted[k], axis=0, keepdims=True) for k in range(CONV_A_K)]
        part = _stack_rows(rows, cb)

        @pl.when(pl.program_id(1) == 0)
        def _():
            dw_ref[...] = part

        @pl.when(pl.program_id(1) > 0)
        def _():
            dw_ref[...] += part

    def col(off):
        return pl.BlockSpec((s, cb), lambda j, b: (b, j + off * nd))

    own = pl.BlockSpec((s, cb), lambda j, b: (b, j))
    wspec = pl.BlockSpec((8, cb), lambda j, b: (0, j))
    out = jax.ShapeDtypeStruct((nb * s, d), BF16)
    return pl.pallas_call(
        body, name=name, grid=(nd, nb), in_specs=[own, col(0), col(1), col(2), wspec],
        out_specs=[own, own, own, wspec], out_shape=[out, out, out, jax.ShapeDtypeStruct((8, d), F32)],
        compiler_params=_params("parallel", "arbitrary"),
    )(dy, proj, proj, proj, w)


def _conv_s_fwd(proj, col0, w, bias, nb, s, cc_width, name, cb=256):
    cb = _tile(math.gcd(cc_width, col0) if col0 else cc_width, cb, LANES)
    nd, off = cc_width // cb, col0 // cb

    def body(x_ref, w_ref, b_ref, o_ref, pre_ref):
        t_idx = lax.broadcasted_iota(jnp.int32, (s, cb), 0)
        xv = x_ref[...].astype(F32)
        pre = b_ref[...] + sum(w_ref[k:k + 1, :] * _shift_down(xv, SSM_CONV_K - 1 - k, t_idx) for k in range(SSM_CONV_K))
        o_ref[...] = _silu(pre).astype(BF16)
        pre_ref[...] = pre.astype(BF16)

    vec = pl.BlockSpec((8, cb), lambda b, j: (0, j))
    own = pl.BlockSpec((s, cb), lambda b, j: (b, j))
    out = jax.ShapeDtypeStruct((nb * s, cc_width), BF16)
    return pl.pallas_call(
        body, name=name, grid=(nb, nd),
        in_specs=[pl.BlockSpec((s, cb), lambda b, j: (b, j + off)), vec, pl.BlockSpec((1, cb), lambda b, j: (0, j))],
        out_specs=[own, own], out_shape=[out, out], compiler_params=_params("parallel", "parallel"),
    )(proj, w, bias)


def _conv_s_bwd(dxc, pre, proj, col0, w, nb, s, cc_width, name, cb=256):
    cb = _tile(math.gcd(cc_width, col0) if col0 else cc_width, cb, LANES)
    nd, off = cc_width // cb, col0 // cb

    def body(d_ref, pre_ref, x_ref, w_ref, dx_ref, dw_ref, db_ref):
        t_idx = lax.broadcasted_iota(jnp.int32, (s, cb), 0)
        xv = x_ref[...].astype(F32)
        dpre = d_ref[...].astype(F32) * _dsilu(pre_ref[...].astype(F32))
        ahead = [_shift_up(dpre, j, t_idx, s) for j in range(SSM_CONV_K)]
        dx_ref[...] = sum(w_ref[k:k + 1, :] * ahead[SSM_CONV_K - 1 - k] for k in range(SSM_CONV_K)).astype(BF16)
        rows = [jnp.sum(ahead[SSM_CONV_K - 1 - k] * xv, axis=0, keepdims=True) for k in range(SSM_CONV_K)]
        dw_part = _stack_rows(rows, cb)
        db_part = jnp.sum(dpre, axis=0, keepdims=True)

        @pl.when(pl.program_id(1) == 0)
        def _():
            dw_ref[...] = dw_part
            db_ref[...] = db_part

        @pl.when(pl.program_id(1) > 0)
        def _():
            dw_ref[...] += dw_part
            db_ref[...] += db_part

    own = pl.BlockSpec((s, cb), lambda j, b: (b, j))
    wspec = pl.BlockSpec((8, cb), lambda j, b: (0, j))
    bspec = pl.BlockSpec((1, cb), lambda j, b: (0, j))
    return pl.pallas_call(
        body, name=name, grid=(nd, nb),
        in_specs=[own, own, pl.BlockSpec((s, cb), lambda j, b: (b, j + off)), wspec],
        out_specs=[own, wspec, bspec],
        out_shape=[jax.ShapeDtypeStruct((nb * s, cc_width), BF16), jax.ShapeDtypeStruct((8, cc_width), F32),
                   jax.ShapeDtypeStruct((1, cc_width), F32)],
        compiler_params=_params("parallel", "arbitrary"),
    )(dxc, pre, proj, w)


def _split3(v):
    hi = v.astype(BF16)
    r1 = v - hi.astype(F32)
    mid = r1.astype(BF16)
    return hi, mid, (r1 - mid.astype(F32)).astype(BF16)


def _exact_left(mask_b, v):
    return sum(_dot(mask_b, t) for t in _split3(v))


def _exact_right(v, mask_b):
    return sum(_dot(t, mask_b) for t in _split3(v))


def _head_sums(v, e_b):
    return _dot(v.astype(BF16), e_b, NT)


def _spread(v, out_ref, di):
    lane = lax.broadcasted_iota(jnp.int32, (v.shape[0], LANES), 1)
    for pr in range(di // LANES):
        h0 = pr * (LANES // SSM_HEAD_DIM)
        out_ref[:, pr * LANES:(pr + 1) * LANES] = jnp.where(lane < SSM_HEAD_DIM, v[:, h0:h0 + 1], v[:, h0 + 1:h0 + 2])


def _ssd_common(xc_ref, dtr_ref, dtrt_ref, prow_ref, pcol_ref, dtx_ref, acsx_ref, dx_ref, di):
    l = SSM_CHUNK
    bias_r, a_r = prow_ref[0:1, :], -jnp.exp(prow_ref[1:2, :])
    sp_in = dtr_ref[...] + bias_r
    dt = _softplus(sp_in)
    li = lax.broadcasted_iota(jnp.int32, (l, l), 0)
    si = lax.broadcasted_iota(jnp.int32, (l, l), 1)
    lower_b = (li >= si).astype(BF16)
    upper_b = (li <= si).astype(BF16)
    acs = _exact_left(lower_b, dt * a_r)
    bias_c, a_c = pcol_ref[:, 0:1], -jnp.exp(pcol_ref[:, 1:2])
    dt_t = _softplus(dtrt_ref[...] + bias_c)
    acs_t = _exact_right(dt_t * a_c, upper_b)
    _spread(dt, dtx_ref, di)
    _spread(acs, acsx_ref, di)
    _spread(prow_ref[0:8, :], dx_ref, di)
    acs_exp = acsx_ref[...]
    acs_last = acs_exp[l - 1:l, :]
    x = xc_ref[:, 0:di].astype(F32)
    return dict(dt=dt, a_r=a_r, sp_in=sp_in, acs=acs, acs_t=acs_t, dt_exp=dtx_ref[...], e_exp=jnp.exp(acs_exp),
                el_exp=jnp.exp(acs_last), f_exp=jnp.exp(acs_last - acs_exp), x=x, mask=li >= si, upper_b=upper_b,
                d_exp=dx_ref[2:3, :])


def _decay(q, h):
    seg = q["acs"][:, h:h + 1] - q["acs_t"][h:h + 1, :]
    return jnp.exp(jnp.where(q["mask"], seg, NEG_BIG))


def _ssd_fwd(xc, dtr, dtrt, prow, pcol, nb, nc, di, name):
    l, n, g_n, p = SSM_CHUNK, SSM_STATE, SSM_GROUPS, SSM_HEAD_DIM
    cc = xc.shape[1]
    gw = di // g_n
    assert p * 2 == LANES and gw % LANES == 0

    def body(xc_ref, dtr_ref, dtrt_ref, prow_ref, pcol_ref, y_ref, sprev_ref, st_ref, dtx_ref, acsx_ref, dx_ref):
        @pl.when(pl.program_id(1) == 0)
        def _():
            st_ref[...] = jnp.zeros_like(st_ref)

        q = _ssd_common(xc_ref, dtr_ref, dtrt_ref, prow_ref, pcol_ref, dtx_ref, acsx_ref, dx_ref, di)
        x = q["x"]
        xd = x * q["dt_exp"]
        xdb = xd.astype(BF16)
        xdf = (xd * q["f_exp"]).astype(BF16)
        lane = lax.broadcasted_iota(jnp.int32, (l, LANES), 1)
        for g in range(g_n):
            lo = g * gw
            bg = xc_ref[:, di + g * n: di + (g + 1) * n]
            cg = xc_ref[:, di + g_n * n + g * n: di + g_n * n + (g + 1) * n]
            cb = _dot(cg, bg, NT)
            st_g = st_ref[:, lo:lo + gw]
            y_off = q["e_exp"][:, lo:lo + gw] * _dot(cg, st_g.astype(BF16))
            for pr in range(gw // LANES):
                c0 = lo + pr * LANES
                h0 = c0 // p
                xp = xdb[:, c0:c0 + LANES]
                m0 = (cb * _decay(q, h0)).astype(BF16)
                m1 = (cb * _decay(q, h0 + 1)).astype(BF16)
                yd = _dot(m0, jnp.where(lane < p, xp, 0)) + _dot(m1, jnp.where(lane >= p, xp, 0))
                y_ref[:, c0:c0 + LANES] = (yd + y_off[:, pr * LANES:(pr + 1) * LANES]
                                           + q["d_exp"][:, c0:c0 + LANES] * x[:, c0:c0 + LANES])
            sprev_ref[:, lo:lo + gw] = st_g
            st_ref[:, lo:lo + gw] = q["el_exp"][:, lo:lo + gw] * st_g + _dot(bg, xdf[:, lo:lo + gw], TN)

    tok = lambda w: pl.BlockSpec((l, w), lambda b, c: (b * nc + c, 0))
    const = lambda r, w: pl.BlockSpec((r, w), lambda b, c: (0, 0))
    return pl.pallas_call(
        body, name=name, grid=(nb, nc),
        in_specs=[tok(cc), tok(LANES), pl.BlockSpec((LANES, l), lambda b, c: (0, b * nc + c)),
                  const(8, LANES), const(LANES, 8)],
        out_specs=[tok(di), pl.BlockSpec((None, n, di), lambda b, c: (b * nc + c, 0, 0))],
        out_shape=[jax.ShapeDtypeStruct((nb * nc * l, di), F32), jax.ShapeDtypeStruct((nb * nc, n, di), F32)],
        scratch_shapes=[pltpu.VMEM((n, di), F32), pltpu.VMEM((l, di), F32), pltpu.VMEM((l, di), F32),
                        pltpu.VMEM((8, di), F32)],
        compiler_params=_params("parallel", "arbitrary"),
    )(xc, dtr, dtrt, prow, pcol)


def _ssd_bwd(dy, xc, dtr, dtrt, prow, pcol, e_mat, sprev, nb, nc, di, name):
    l, n, g_n, p = SSM_CHUNK, SSM_STATE, SSM_GROUPS, SSM_HEAD_DIM
    cc = xc.shape[1]
    gw = di // g_n

    def body(dy_ref, xc_ref, dtr_ref, dtrt_ref, prow_ref, pcol_ref, e_ref, sprev_ref,
             dxc_ref, ddtr_ref, sums_ref, dst_ref, off_ref, dxd_ref, last_ref, vst_ref,
             dtx_ref, acsx_ref, dx_ref):
        first = jnp.logical_and(pl.program_id(0) == 0, pl.program_id(1) == 0)

        @pl.when(pl.program_id(1) == 0)
        def _():
            dst_ref[...] = jnp.zeros_like(dst_ref)

        head_row = lax.broadcasted_iota(jnp.int32, (LANES, l), 0)
        row_sums, col_sums = jnp.zeros((l, LANES), F32), jnp.zeros((LANES, l), F32)
        strict_lower = lax.broadcasted_iota(jnp.int32, (l, l), 0) > lax.broadcasted_iota(jnp.int32, (l, l), 1)

        q = _ssd_common(xc_ref, dtr_ref, dtrt_ref, prow_ref, pcol_ref, dtx_ref, acsx_ref, dx_ref, di)
        x = q["x"]
        xd = x * q["dt_exp"]
        xdb = xd.astype(BF16)
        xdf = (xd * q["f_exp"]).astype(BF16)
        dyv = dy_ref[...]
        dyb = dyv.astype(BF16)
        dye = (dyv * q["e_exp"]).astype(BF16)
        upper_b = q["upper_b"]
        lane = lax.broadcasted_iota(jnp.int32, (l, LANES), 1)
        for g in range(g_n):
            lo = g * gw
            bg = xc_ref[:, di + g * n: di + (g + 1) * n]
            cg = xc_ref[:, di + g_n * n + g * n: di + g_n * n + (g + 1) * n]
            cb = _dot(cg, bg, NT)
            st_g = sprev_ref[:, lo:lo + gw]
            st_gb = st_g.astype(BF16)
            dst_g = dst_ref[:, lo:lo + gw]
            dst_gb = dst_g.astype(BF16)
            dye_g = dye[:, lo:lo + gw]
            xdf_g = xdf[:, lo:lo + gw]
            y_off = q["e_exp"][:, lo:lo + gw] * _dot(cg, st_gb)
            dc_g = _dot(dye_g, st_gb, NT)
            db_g = _dot(xdf_g, dst_gb, NT)
            dxd_state = _dot(bg, dst_gb) * q["f_exp"][:, lo:lo + gw]
            last_ref[:, lo:lo + gw] = jnp.sum(dst_g * st_g, axis=0, keepdims=True)
            dst_ref[:, lo:lo + gw] = q["el_exp"][:, lo:lo + gw] * dst_g + _dot(cg, dye_g, TN)
            off_ref[:, lo:lo + gw] = dyv[:, lo:lo + gw] * y_off
            vst_ref[:, lo:lo + gw] = xd[:, lo:lo + gw] * dxd_state
            dcb = jnp.zeros((l, l), F32)
            for pr in range(gw // LANES):
                c0 = lo + pr * LANES
                h0 = c0 // p
                xp = xdb[:, c0:c0 + LANES]
                dyp = dyb[:, c0:c0 + LANES]
                dxd_diag = jnp.zeros((l, LANES), F32)
                for k, keep in enumerate((lane < p, lane >= p)):
                    dec = _decay(q, h0 + k)
                    dy_h = jnp.where(keep, dyp, 0)
                    dm_dec = _dot(dy_h, xp, NT) * dec
                    dcb = dcb + dm_dec
                    dxd_diag = dxd_diag + _dot((cb * dec).astype(BF16), dy_h, TN)
                    qm = dm_dec * cb
                    row_sums = jnp.where(lane == h0 + k, jnp.sum(qm, axis=1, keepdims=True), row_sums)
                    col_sums = jnp.where(head_row == h0 + k, jnp.sum(qm, axis=0, keepdims=True), col_sums)
                dxd_ref[:, c0:c0 + LANES] = dxd_diag + dxd_state[:, pr * LANES:(pr + 1) * LANES]
            dcb_b = dcb.astype(BF16)
            dxc_ref[:, di + g * n: di + (g + 1) * n] = (db_g + _dot(dcb_b, cg, TN)).astype(BF16)
            dxc_ref[:, di + g_n * n + g * n: di + g_n * n + (g + 1) * n] = (dc_g + _dot(dcb_b, bg)).astype(BF16)
        dxd = dxd_ref[...]
        e_b = e_ref[...]
        from_y = _exact_left(upper_b, _head_sums(off_ref[...], e_b) + row_sums - col_sums.T)
        from_s = _exact_left(strict_lower.astype(BF16), _head_sums(vst_ref[...], e_b))
        carried = _head_sums(jnp.broadcast_to(last_ref[...], (8, di)), e_b)[0:1, :] * jnp.exp(q["acs"][l - 1:l, :])
        dla = from_y + from_s + carried
        ddt = dla * q["a_r"] + _head_sums(dxd * x, e_b)
        ddtr = ddt * jax.nn.sigmoid(q["sp_in"])
        ddtr_ref[...] = ddtr
        dxc_ref[:, 0:di] = (dxd * q["dt_exp"] + q["d_exp"] * dyv).astype(BF16)
        dd_exp = jnp.sum(dyv * x, axis=0, keepdims=True)
        dd = _head_sums(jnp.broadcast_to(dd_exp, (8, di)), e_b)[0:1, :]
        part = _stack_rows([jnp.sum(ddtr, axis=0, keepdims=True),
                            jnp.sum(dla * q["dt"], axis=0, keepdims=True) * q["a_r"], dd], LANES)

        @pl.when(first)
        def _():
            sums_ref[...] = part

        @pl.when(jnp.logical_not(first))
        def _():
            sums_ref[...] += part

    rev = lambda b, c: b * nc + (nc - 1 - c)
    tok = lambda w: pl.BlockSpec((l, w), lambda b, c: (rev(b, c), 0))
    const = lambda r, w: pl.BlockSpec((r, w), lambda b, c: (0, 0))
    return pl.pallas_call(
        body, name=name, grid=(nb, nc),
        in_specs=[tok(di), tok(cc), tok(LANES), pl.BlockSpec((LANES, l), lambda b, c: (0, rev(b, c))),
                  const(8, LANES), const(LANES, 8), const(LANES, di),
                  pl.BlockSpec((None, n, di), lambda b, c: (rev(b, c), 0, 0))],
        out_specs=[tok(cc), tok(LANES), const(8, LANES)],
        out_shape=[jax.ShapeDtypeStruct((nb * nc * l, cc), BF16), jax.ShapeDtypeStruct((nb * nc * l, LANES), F32),
                   jax.ShapeDtypeStruct((8, LANES), F32)],
        scratch_shapes=[pltpu.VMEM((n, di), F32), pltpu.VMEM((l, di), F32), pltpu.VMEM((l, di), F32),
                        pltpu.VMEM((1, di), F32), pltpu.VMEM((l, di), F32),
                        pltpu.VMEM((l, di), F32), pltpu.VMEM((l, di), F32), pltpu.VMEM((8, di), F32)],
        compiler_params=_params("arbitrary", "arbitrary"),
    )(dy, xc, dtr, dtrt, prow, pcol, e_mat, sprev)


def _gate_norm_fwd(y, proj, z_col0, norm_g, di, name, tt=256):
    t = y.shape[0]
    tt = _tile(t, tt, 8)
    gw = di // SSM_GROUPS
    zw = _tile(math.gcd(di, z_col0), di, LANES)
    nz, zoff = di // zw, z_col0 // zw

    def body(*refs):
        y_ref, z_refs, g_ref, o_ref = refs[0], refs[1:1 + nz], refs[1 + nz], refs[2 + nz]
        for g in range(SSM_GROUPS):
            lo = g * gw
            zv = z_refs[lo // zw][:, lo % zw:lo % zw + gw].astype(F32)
            yg = y_ref[:, lo:lo + gw] * _silu(zv)
            r = lax.rsqrt(jnp.mean(yg * yg, axis=-1, keepdims=True) + EPS)
            o_ref[:, lo:lo + gw] = (yg * r * g_ref[:, lo:lo + gw]).astype(BF16)

    row = pl.BlockSpec((tt, di), lambda i: (i, 0))
    zspecs = [pl.BlockSpec((tt, zw), functools.partial(lambda i, k: (i, zoff + k), k=k)) for k in range(nz)]
    return pl.pallas_call(
        body, name=name, grid=(t // tt,), in_specs=[row] + zspecs + [pl.BlockSpec((1, di), lambda i: (0, 0))],
        out_specs=row, out_shape=jax.ShapeDtypeStruct((t, di), BF16), compiler_params=_params("parallel"),
    )(y, *([proj] * nz), norm_g)


def _gate_norm_bwd(dn, y, proj, z_col0, norm_g, di, name, tt=256):
    t = y.shape[0]
    tt = _tile(t, tt, 8)
    gw = di // SSM_GROUPS
    zw = _tile(math.gcd(di, z_col0), di, LANES)
    nz, zoff = di // zw, z_col0 // zw

    def body(*refs):
        dn_ref, y_ref, z_refs, g_ref = refs[0], refs[1], refs[2:2 + nz], refs[2 + nz]
        dy_ref, dz_ref, dg_ref = refs[3 + nz:]
        first = pl.program_id(0) == 0
        for g in range(SSM_GROUPS):
            lo = g * gw
            zv = z_refs[lo // zw][:, lo % zw:lo % zw + gw].astype(F32)
            yv = y_ref[:, lo:lo + gw]
            sz = _silu(zv)
            yg = yv * sz
            r = lax.rsqrt(jnp.mean(yg * yg, axis=-1, keepdims=True) + EPS)
            yh = yg * r
            dnv = dn_ref[:, lo:lo + gw].astype(F32)
            gy = dnv * g_ref[:, lo:lo + gw]
            dyg = r * (gy - yh * jnp.mean(gy * yh, axis=-1, keepdims=True))
            dy_ref[:, lo:lo + gw] = dyg * sz
            dz_ref[:, lo:lo + gw] = (dyg * yv * _dsilu(zv)).astype(BF16)
            part = jnp.sum(dnv * yh, axis=0, keepdims=True)

            @pl.when(first)
            def _():
                dg_ref[:, lo:lo + gw] = part

            @pl.when(jnp.logical_not(first))
            def _():
                dg_ref[:, lo:lo + gw] += part

    row = pl.BlockSpec((tt, di), lambda i: (i, 0))
    vec = pl.BlockSpec((1, di), lambda i: (0, 0))
    zspecs = [pl.BlockSpec((tt, zw), functools.partial(lambda i, k: (i, zoff + k), k=k)) for k in range(nz)]
    return pl.pallas_call(
        body, name=name, grid=(t // tt,), in_specs=[row, row] + zspecs + [vec], out_specs=[row, row, vec],
        out_shape=[jax.ShapeDtypeStruct((t, di), F32), jax.ShapeDtypeStruct((t, di), BF16), jax.ShapeDtypeStruct((1, di), F32)],
        compiler_params=_params("arbitrary"),
    )(dn, y, *([proj] * nz), norm_g)


def _softmax_rows(s):
    s = s - jnp.max(s, axis=-1, keepdims=True)
    e = jnp.exp(s)
    return e * (1.0 / jnp.sum(e, axis=-1, keepdims=True))


def _xattn_fwd(q, kv, nb, s, m, d, name, tq=512):
    tq = _tile(s, tq, 8)
    nq = s // tq
    hd = d // XATTN_HEADS
    scale = 1.0 / math.sqrt(hd)

    def body(q_ref, k_ref, v_ref, o_ref):
        for h in range(XATTN_HEADS):
            sl = slice(h * hd, (h + 1) * hd)
            prob = _softmax_rows(_dot(q_ref[:, sl], k_ref[:, sl], NT) * scale)
            o_ref[:, sl] = _dot(prob.astype(BF16), v_ref[:, sl]).astype(BF16)

    return pl.pallas_call(
        body, name=name, grid=(nb, nq),
        in_specs=[pl.BlockSpec((tq, d), lambda b, i: (b * nq + i, 0)), pl.BlockSpec((m, d), lambda b, i: (b, 0)),
                  pl.BlockSpec((m, d), lambda b, i: (b, 1))],
        out_specs=pl.BlockSpec((tq, d), lambda b, i: (b * nq + i, 0)),
        out_shape=jax.ShapeDtypeStruct((nb * s, d), BF16), compiler_params=_params("parallel", "parallel"),
    )(q, kv, kv)


def _xattn_bwd(do, q, kv, nb, s, m, d, name, tq=512):
    tq = _tile(s, tq, 8)
    nq = s // tq
    hd = d // XATTN_HEADS
    scale = 1.0 / math.sqrt(hd)

    def body(do_ref, q_ref, k_ref, v_ref, dq_ref, dk_ref, dv_ref):
        first = pl.program_id(1) == 0
        for h in range(XATTN_HEADS):
            sl = slice(h * hd, (h + 1) * hd)
            qh, kh, vh, doh = q_ref[:, sl], k_ref[:, sl], v_ref[:, sl], do_ref[:, sl]
            prob = _softmax_rows(_dot(qh, kh, NT) * scale)
            dv_h = _dot(prob.astype(BF16), doh, TN)
            dp = _dot(doh, vh, NT)
            ds = (prob * (dp - jnp.sum(dp * prob, axis=-1, keepdims=True)) * scale).astype(BF16)
            dq_ref[:, sl] = _dot(ds, kh).astype(BF16)
            dk_h = _dot(ds, qh, TN)

            @pl.when(first)
            def _():
                dk_ref[:, sl] = dk_h
                dv_ref[:, sl] = dv_h

            @pl.when(jnp.logical_not(first))
            def _():
                dk_ref[:, sl] += dk_h
                dv_ref[:, sl] += dv_h

    qspec = pl.BlockSpec((tq, d), lambda b, i: (b * nq + i, 0))
    dq, dk, dv = pl.pallas_call(
        body, name=name, grid=(nb, nq),
        in_specs=[qspec, qspec, pl.BlockSpec((m, d), lambda b, i: (b, 0)), pl.BlockSpec((m, d), lambda b, i: (b, 1))],
        out_specs=[qspec, pl.BlockSpec((m, d), lambda b, i: (b, 0)), pl.BlockSpec((m, d), lambda b, i: (b, 0))],
        out_shape=[jax.ShapeDtypeStruct((nb * s, d), BF16), jax.ShapeDtypeStruct((nb * m, d), F32),
                   jax.ShapeDtypeStruct((nb * m, d), F32)],
        compiler_params=_params("parallel", "arbitrary"),
    )(do, q, kv, kv)
    return dq, dk, dv


def _all_gather(shards, name):
    n_arr = len(shards)

    def body(*refs):
        x_refs, out_refs = refs[:n_arr], refs[n_arr:2 * n_arr]
        send_sems, recv_sems, local_sems = refs[2 * n_arr:]
        x, y, c = lax.axis_index("x"), lax.axis_index("y"), lax.axis_index("c")
        me, sibling = (x, y, c), (x, y, 1 - c)
        chips = [(1 - x, y), (x, 1 - y), (1 - x, 1 - y)]

        def copy(w, k, block, to, from_input=False):
            px, py, pc = block
            rows = out_refs[w].at[4 * px + 2 * py + pc]
            return pltpu.make_async_remote_copy(
                src_ref=x_refs[w] if from_input else rows, dst_ref=rows,
                send_sem=send_sems.at[7 * w + k], recv_sem=recv_sems.at[7 * w + k], device_id=to, device_id_type=MESH)

        started = []
        for w in range(n_arr):
            mine = pltpu.make_async_copy(x_refs[w], out_refs[w].at[4 * x + 2 * y + c], local_sems.at[w])
            mine.start()
            started.append(mine)
        sends = []
        for w in range(n_arr):
            sends.append(copy(w, 0, me, sibling, from_input=True))
            sends += [copy(w, 1 + j, me, (*chip, c), from_input=True) for j, chip in enumerate(chips)]
        for cp in sends:
            cp.start()
        for j, chip in enumerate(chips):
            for w in range(n_arr):
                copy(w, 1 + j, (*chip, c), me).wait_recv()
                passed = copy(w, 4 + j, (*chip, c), sibling)
                passed.start()
                sends.append(passed)
        for w in range(n_arr):
            copy(w, 0, sibling, me).wait_recv()
            for j, chip in enumerate(chips):
                copy(w, 4 + j, (*chip, 1 - c), me).wait_recv()
        for cp in sends:
            cp.wait_send()
        for mine in started:
            mine.wait()

    hbm = pl.BlockSpec(memory_space=pl.ANY)
    return pl.pallas_call(
        body, name=name, out_shape=[jax.ShapeDtypeStruct((N_DEV,) + s.shape, s.dtype) for s in shards],
        in_specs=[hbm] * n_arr, out_specs=[hbm] * n_arr,
        scratch_shapes=[pltpu.SemaphoreType.DMA((7 * n_arr,)), pltpu.SemaphoreType.DMA((7 * n_arr,)),
                        pltpu.SemaphoreType.DMA((n_arr,))],
    )(*shards)


_HBM = pl.BlockSpec(memory_space=pltpu.HBM)
_SEM = pl.BlockSpec(memory_space=pltpu.SEMAPHORE)
_DATAFLOW = pltpu.SideEffectType.DATAFLOW_SIDE_EFFECTING


def _peer_list(x, y, c):
    return [(1 - x if k & 4 else x, 1 - y if k & 2 else y, 1 - c if k & 1 else c) for k in range(1, N_DEV)]


def _push_copy(src_ref, land_ref, send_sems, recv_sems, w, k, peer, me, per_peer_src, receiving):
    px, py, pc = peer
    peer_slot = 4 * px + 2 * py + pc
    return pltpu.make_async_remote_copy(
        src_ref=src_ref.at[peer_slot] if per_peer_src else src_ref,
        dst_ref=land_ref.at[peer_slot if receiving else me],
        send_sem=send_sems.at[7 * w + k], recv_sem=recv_sems.at[7 * w + k], device_id=peer, device_id_type=MESH)


def _push_start(srcs, per_peer_src, after, name):
    n_arr = len(srcs)
    land_shapes = [s.shape if per_peer_src else (N_DEV,) + s.shape for s in srcs]

    def body(*refs):
        src_refs, land_refs = refs[:n_arr], refs[n_arr:2 * n_arr]
        send_sems, recv_sems = refs[2 * n_arr + 1], refs[2 * n_arr + 2]
        token = refs[-1]
        x, y, c = lax.axis_index("x"), lax.axis_index("y"), lax.axis_index("c")
        me = 4 * x + 2 * y + c
        for w in range(n_arr):
            for k, peer in enumerate(_peer_list(x, y, c)):
                _push_copy(src_refs[w], land_refs[w], send_sems, recv_sems, w, k, peer, me, per_peer_src, False).start()
        token[...] = jnp.zeros_like(token)

    lands = [pltpu.with_memory_space_constraint(lax.empty(ls, s.dtype), pltpu.HBM) for ls, s in zip(land_shapes, srcs)]
    srcs_hbm = [pltpu.with_memory_space_constraint(s, pltpu.HBM) for s in srcs]
    out = pl.pallas_call(
        body, name=name,
        out_shape=(pltpu.SemaphoreType.DMA((7 * n_arr,)), pltpu.SemaphoreType.DMA((7 * n_arr,)),
                   *[pltpu.HBM(s.shape, s.dtype) for s in srcs], *[pltpu.HBM(ls, s.dtype) for ls, s in zip(land_shapes, srcs)],
                   jax.ShapeDtypeStruct((8, LANES), F32)),
        in_specs=[_HBM] * (2 * n_arr) + [pl.BlockSpec(memory_space=pl.ANY)],
        out_specs=(_SEM, _SEM, *([_HBM] * (2 * n_arr)), pl.BlockSpec(memory_space=pltpu.VMEM)),
        input_output_aliases={i: 2 + i for i in range(2 * n_arr)},
        compiler_params=pltpu.CompilerParams(has_side_effects=_DATAFLOW),
    )(*srcs_hbm, *lands, after)
    return dict(send=out[0], recv=out[1], srcs=list(out[2:2 + n_arr]), lands=list(out[2 + n_arr:2 + 2 * n_arr]),
                token=out[-1])


def _push_wait(pending, per_peer_src, after, name):
    n_arr = len(pending["srcs"])

    def body(*refs):
        src_refs, land_refs = refs[:n_arr], refs[n_arr:2 * n_arr]
        send_sems, recv_sems = refs[2 * n_arr], refs[2 * n_arr + 1]
        x, y, c = lax.axis_index("x"), lax.axis_index("y"), lax.axis_index("c")
        me = 4 * x + 2 * y + c
        for w in range(n_arr):
            for k, peer in enumerate(_peer_list(x, y, c)):
                cp = _push_copy(src_refs[w], land_refs[w], send_sems, recv_sems, w, k, peer, me, per_peer_src, True)
                cp.wait_send()
                cp.wait_recv()

    out = pl.pallas_call(
        body, name=name,
        out_shape=tuple(pltpu.HBM(a.shape, a.dtype) for a in pending["srcs"] + pending["lands"]),
        in_specs=[_HBM] * (2 * n_arr) + [_SEM, _SEM, pl.BlockSpec(memory_space=pl.ANY)],
        out_specs=tuple([_HBM] * (2 * n_arr)),
        input_output_aliases={i: i for i in range(2 * n_arr)},
        compiler_params=pltpu.CompilerParams(has_side_effects=_DATAFLOW),
    )(*pending["srcs"], *pending["lands"], pending["send"], pending["recv"], after)
    return list(out[:n_arr]), list(out[n_arr:])


def _adamw_math(w, g, m, v):
    m = ADAM_B1 * m + (1.0 - ADAM_B1) * g
    v = ADAM_B2 * v + (1.0 - ADAM_B2) * (g * g)
    m_hat = m / (1.0 - ADAM_B1 ** ADAM_STEP)
    v_hat = v / (1.0 - ADAM_B2 ** ADAM_STEP)
    delta = -ADAM_LR * (m_hat / (jnp.sqrt(v_hat) + ADAM_EPS) + ADAM_WD * w)
    return delta, m, v


def _sum8(parts, name, tr=512):
    _, r, c_dim = parts.shape
    tr = _tile(r, tr, BF16_SUBLANES)

    def body(p_ref, o_ref):
        acc = p_ref[0].astype(F32)
        for k in range(1, N_DEV):
            acc = acc + p_ref[k].astype(F32)
        o_ref[...] = acc

    return pl.pallas_call(
        body, name=name, grid=(r // tr,), in_specs=[pl.BlockSpec((N_DEV, tr, c_dim), lambda i: (0, i, 0))],
        out_specs=pl.BlockSpec((tr, c_dim), lambda i: (i, 0)),
        out_shape=jax.ShapeDtypeStruct((r, c_dim), F32), compiler_params=_params("parallel"),
    )(parts)


def _sum8_adamw(parts, w, m, v, name, tr=128):
    _, r, c_dim = parts.shape
    tr = _tile(r, tr, BF16_SUBLANES)
    tc = c_dim if tr <= 2 * LANES else _tile(c_dim, LANES, LANES)

    def body(p_ref, w_ref, m_ref, v_ref, g_ref, d_ref, nm_ref, nv_ref):
        g = p_ref[0].astype(F32)
        for k in range(1, N_DEV):
            g = g + p_ref[k].astype(F32)
        g_ref[...] = g
        d_ref[...], nm_ref[...], nv_ref[...] = _adamw_math(w_ref[...], g, m_ref[...], v_ref[...])

    blk = pl.BlockSpec((None, tr, tc), lambda i, j: (0, i, j))
    out = jax.ShapeDtypeStruct((1, r, c_dim), F32)
    return pl.pallas_call(
        body, name=name, grid=(r // tr, c_dim // tc),
        in_specs=[pl.BlockSpec((N_DEV, tr, tc), lambda i, j: (0, i, j)), blk, blk, blk],
        out_specs=[blk] * 4, out_shape=[out] * 4, compiler_params=_params("parallel", "parallel"),
    )(parts, w, m, v)


def _adamw(g, w, m, v, name):
    r, c_dim = g.shape

    def body(g_ref, w_ref, m_ref, v_ref, d_ref, nm_ref, nv_ref):
        d_ref[...], nm_ref[...], nv_ref[...] = _adamw_math(w_ref[...], g_ref[...], m_ref[...], v_ref[...])

    out = jax.ShapeDtypeStruct((r, c_dim), F32)
    return pl.pallas_call(body, name=name, out_shape=[out] * 3)(g, w, m, v)


def _pack_rows(arrays, dtype, row_unit):
    chunks, offs, r0 = [], [], 0
    for a in arrays:
        flat = a.reshape(-1).astype(dtype)
        rows = -(-flat.shape[0] // (LANES * row_unit)) * row_unit
        flat = jnp.pad(flat, (0, rows * LANES - flat.shape[0]))
        chunks.append(flat.reshape(rows, LANES))
        offs.append((r0, rows))
        r0 += rows
    return jnp.concatenate(chunks, axis=0), offs


def _unpack_rows(packed, offs, shapes):
    out = []
    for (r0, rows), shape in zip(offs, shapes):
        n = math.prod(shape)
        blk = packed[..., r0:r0 + rows, :]
        blk = blk.reshape(packed.shape[:-2] + (rows * LANES,))[..., :n]
        out.append(blk.reshape(packed.shape[:-2] + tuple(shape)))
    return out


def _full_from_slots(blk, col_sharded):
    _, r, c = blk.shape
    if col_sharded:
        return blk.transpose(1, 0, 2).reshape(r, N_DEV * c)
    return blk.reshape(N_DEV * r, c)


def _slots_from_full(full, col_sharded):
    r, c = full.shape
    if col_sharded:
        return full.reshape(r, N_DEV, c // N_DEV).transpose(1, 0, 2)
    return full.reshape(N_DEV, r // N_DEV, c)


def _ffn_fwd(h, g, w_gu_t, w_d, tag):
    n = _rms_fwd(h, g, f"{tag}_norm")
    gate, up, a = _ffn_up(n, w_gu_t, f"{tag}_up")
    h_out = _mm(a, w_d, F32, f"{tag}_down", res=h, alpha=FFN_RES_WEIGHT)
    return h_out, (h, n, gate, up, a)


def _ffn_bwd(dh_out, saved, g, w_gu_t, w_d, tag, dep, send_grads):
    h, n, gate, up, a = saved
    dw_d = _mm_tn(a, dh_out, f"{tag}_dw_down", alpha=FFN_RES_WEIGHT, dep=dep)
    dgate, dup = _ffn_da(dh_out, w_d, gate, up, f"{tag}_da", FFN_RES_WEIGHT)
    f = dgate.shape[1]
    dw_gu_t = _mm_tn(dgate, n, f"{tag}_dw_gate", out_rows=2 * f)
    dw_gu_t = _mm_tn(dup, n, f"{tag}_dw_up", out_rows=2 * f, row_off=f, into=dw_gu_t)
    dep = send_grads(dw_gu_t, dw_d)
    dn = _mm([dgate, dup], w_gu_t, F32, f"{tag}_dn", dep=dep, tk=1408)
    dh, dg = _rms_bwd(dn, h, g, f"{tag}_dnorm", dres=dh_out)
    return dh, dg


W_GROUPS = (("ffn1_w_gate_up", "ffn1_w_down"),
            ("w_in",),
            ("w_out_a", "w_out_ssm", "w_mix_out"),
            ("w_q", "w_kv", "w_o_x", "ffn2_w_gate_up", "ffn2_w_down"))
G_GROUPS = (("ffn2_w_gate_up", "ffn2_w_down"),
            ("w_o_x", "w_q", "w_kv", "w_mix_out", "w_out_a", "w_out_ssm", "w_in"),
            ("ffn1_w_gate_up", "ffn1_w_down"))


def _local_step(x3, mem3, target3, small, comm):
    nb, s, d = x3.shape
    m_len = mem3.shape[1]
    t = nb * s
    nc = s // SSM_CHUNK
    di = small["ssm_norm"].shape[1]
    hs = di // SSM_HEAD_DIM
    cc = di + 2 * SSM_GROUPS * SSM_STATE
    x, mem, target = x3.reshape(t, d), mem3.reshape(nb * m_len, d), target3.reshape(t, d)

    sizes = (d, d, d, di, cc, hs, d, d)
    offs = [0]
    for sz in sizes:
        offs.append(offs[-1] + sz)
    z_col0, xbc_col0 = 3 * d, 3 * d + di
    ga_blk, gb_blk = (3 * d + di + cc) // d, (4 * d + di + cc) // d

    pad_vec = lambda v: jnp.pad(v.reshape(1, -1), ((0, 0), (0, LANES - hs)))
    prow = jnp.concatenate([pad_vec(small["ssm_dt_bias"]), pad_vec(small["ssm_a_log"]), pad_vec(small["ssm_d"]),
                            jnp.zeros((5, LANES), F32)], axis=0)
    pcol = prow.T
    e_mat = (lax.broadcasted_iota(jnp.int32, (LANES, di), 0)
             == lax.broadcasted_iota(jnp.int32, (LANES, di), 1) // SSM_HEAD_DIM).astype(BF16)
    conv_a_w8 = jnp.pad(small["conv_a_w"][0], ((0, 8 - CONV_A_K), (0, 0)))
    ssm_conv_w8 = jnp.pad(small["ssm_conv_w"][0], ((0, 8 - SSM_CONV_K), (0, 0)))

    wts, dep = comm.weights(0, None)
    h1, ffn1_saved = _ffn_fwd(x, small["ffn1_norm"] + dep[0, 0], wts["ffn1_w_gate_up"], wts["ffn1_w_down"], "ffn1")
    got, dep = comm.weights(1, h1)
    wts.update(got)
    w_in_t = wts["w_in"]
    w_main_t = jnp.concatenate([w_in_t[offs[i]:offs[i + 1]] for i in (0, 1, 2, 3, 4, 6, 7)], axis=0)
    w_dt_t = jnp.pad(w_in_t[offs[5]:offs[6]], ((0, LANES - hs), (0, 0)))
    u = _rms_fwd(h1, small["mix_norm"] + dep[0, 0], "mix_norm")
    proj = _mm(u, w_main_t, BF16, "in_proj", nt=True)
    dtr = _mm(u, w_dt_t, F32, "in_proj_dt", nt=True)
    yap = _conv_a_fwd(proj, conv_a_w8, nb, s, d, "conv_a")
    got, dep = comm.weights(2, yap)
    wts.update(got)
    y_a = _mm(yap, wts["w_out_a"], BF16, "out_a", dep=dep)
    xc, conv_pre = _conv_s_fwd(proj, xbc_col0, ssm_conv_w8, small["ssm_conv_b"] + dep[0, 0], nb, s, cc, "conv_s")
    dtrt = dtr.T
    y_ssd, sprev = _ssd_fwd(xc, dtr, dtrt, prow, pcol, nb, nc, di, "ssd")
    ygn = _gate_norm_fwd(y_ssd, proj, z_col0, small["ssm_norm"], di, "gate_norm")
    y_b = _mm(ygn, wts["w_out_ssm"], BF16, "out_ssm")
    merged = _merge_fwd(y_a, y_b, proj, ga_blk, gb_blk, d, "merge")
    h2 = _mm(merged, wts["w_mix_out"], F32, "mix_out", res=h1)
    got, _ = comm.weights(3, h2)
    wts.update(got)
    un = _rms_fwd(h2, small["xattn_norm"], "xattn_norm")
    mn = _rms_fwd(mem, small["mem_norm"], "mem_norm")
    q = _mm(un, wts["w_q"], BF16, "q_proj")
    kv = _mm(mn, wts["w_kv"], BF16, "kv_proj", nt=True)
    o = _xattn_fwd(q, kv, nb, s, m_len, d, "xattn")
    h3 = _mm(o, wts["w_o_x"], F32, "o_proj", res=h2)
    h4, ffn2_saved = _ffn_fwd(h3, small["ffn2_norm"], wts["ffn2_w_gate_up"], wts["ffn2_w_down"], "ffn2")
    loss_vec, dh4, dg_final = _loss_head(h4, small["final_norm"].reshape(1, d), target, "loss_head")

    grads = {"final_norm": dg_final.reshape(d)}
    big = {}
    dh3, grads["ffn2_norm"] = _ffn_bwd(
        dh4, ffn2_saved, small["ffn2_norm"], wts["ffn2_w_gate_up"], wts["ffn2_w_down"], "ffn2", None,
        lambda dw_gu_t, dw_d: comm.grads(0, {"ffn2_w_gate_up": dw_gu_t, "ffn2_w_down": dw_d}))
    big["w_o_x"] = _mm_tn(o, dh3, "dw_o")
    do = _mm(dh3, wts["w_o_x"], BF16, "d_o", nt=True)
    dq, dk, dv = _xattn_bwd(do, q, kv, nb, s, m_len, d, "xattn_bwd")
    big["w_q"] = _mm_tn(un, dq, "dw_q")
    big["w_kv"] = _mm_tn(dv, mn, "dw_v", out_rows=2 * d, row_off=d, into=_mm_tn(dk, mn, "dw_k", out_rows=2 * d))
    dun = _mm(dq, wts["w_q"], F32, "d_un", nt=True)
    dmn = _mm([dk, dv], wts["w_kv"], F32, "d_mn")
    _, grads["mem_norm"] = _rms_bwd(dmn, mem, small["mem_norm"], "mem_dnorm")
    dh2, grads["xattn_norm"] = _rms_bwd(dun, h2, small["xattn_norm"], "xattn_dnorm", dres=dh3)
    big["w_mix_out"] = _mm_tn(merged, dh2, "dw_mix")
    dmerged = _mm(dh2, wts["w_mix_out"], BF16, "d_merged", nt=True)
    dya, dyb, dga, dgb = _merge_bwd(dmerged, y_a, y_b, proj, ga_blk, gb_blk, d, "merge_bwd")
    big["w_out_a"] = _mm_tn(yap, dya, "dw_out_a")
    big["w_out_ssm"] = _mm_tn(ygn, dyb, "dw_out_ssm")
    dyap = _mm(dya, wts["w_out_a"], BF16, "d_yap", nt=True)
    dygn = _mm(dyb, wts["w_out_ssm"], BF16, "d_ygn", nt=True)
    dab, dac, dav, dconv_a = _conv_a_bwd(dyap, proj, conv_a_w8, nb, s, d, "conv_a_bwd")
    dy_ssd, dz, grads["ssm_norm"] = _gate_norm_bwd(dygn, y_ssd, proj, z_col0, small["ssm_norm"], di, "gate_norm_bwd")
    dxc, ddtr, ssd_sums = _ssd_bwd(dy_ssd, xc, dtr, dtrt, prow, pcol, e_mat, sprev, nb, nc, di, "ssd_bwd")
    dxbc, dconv_s, grads["ssm_conv_b"] = _conv_s_bwd(dxc, conv_pre, proj, xbc_col0, ssm_conv_w8, nb, s, cc, "conv_s_bwd")
    dpieces = [("ab", dab), ("ac", dac), ("av", dav), ("z", dz), ("xbc", dxbc), ("ga", dga), ("gb", dgb)]
    dw = {tag: _mm_tn(piece, u, f"dw_in_{tag}") for tag, piece in dpieces}
    dw_dt = _mm_tn(ddtr, u, "dw_in_dt")[:hs]
    du_main = _mm([piece for _, piece in dpieces], w_main_t, F32, "d_u", tk=1024)
    du = _mm(ddtr, w_dt_t, F32, "d_u_dt", res=du_main)
    dh1, grads["mix_norm"] = _rms_bwd(du, h1, small["mix_norm"], "mix_dnorm", dres=dh2)
    big["w_in"] = jnp.concatenate([dw["ab"], dw["ac"], dw["av"], dw["z"], dw["xbc"], dw_dt, dw["ga"], dw["gb"]], axis=0)
    dep = comm.grads(1, big)
    dx, grads["ffn1_norm"] = _ffn_bwd(
        dh1, ffn1_saved, small["ffn1_norm"], wts["ffn1_w_gate_up"], wts["ffn1_w_down"], "ffn1", dep,
        lambda dw_gu_t, dw_d: comm.grads(2, {"ffn1_w_gate_up": dw_gu_t, "ffn1_w_down": dw_d}))

    grads["conv_a_w"] = dconv_a[:CONV_A_K]
    grads["ssm_conv_w"] = dconv_s[:SSM_CONV_K]
    grads["ssm_dt_bias"] = ssd_sums[0:1, :hs]
    grads["ssm_a_log"] = ssd_sums[1:2, :hs]
    grads["ssm_d"] = ssd_sums[2:3, :hs]
    return loss_vec[0, 0], dx.reshape(nb, s, d), grads


def _step(inputs):
    w = {k: inputs[k] for k in WEIGHT_ORDER}
    mom = {k: inputs["m_" + k] for k in WEIGHT_ORDER}
    vel = {k: inputs["v_" + k] for k in WEIGHT_ORDER}
    me = 4 * lax.axis_index("x") + 2 * lax.axis_index("y") + lax.axis_index("c")

    send = {k: (w[k][0].T if k in COL_SHARDED else w[k][0]).astype(BF16) for k in BIG_WEIGHTS}

    def own_slot(land, mine):
        return lax.dynamic_update_slice(land, mine[None], (me, 0, 0))

    gathers, exchanges = {}, {}

    def weights(i, after):
        if i == 0:
            lands = _all_gather([send[k] for k in W_GROUPS[0]], "gather0")
        else:
            sent, lands = _push_wait(gathers[i], False, after, f"gather{i}_wait")
            lands = [own_slot(land, mine) for land, mine in zip(lands, sent)]
        full = {k: land.reshape(N_DEV * land.shape[1], land.shape[2]) for k, land in zip(W_GROUPS[i], lands)}
        dep = jnp.zeros((8, LANES), F32)
        if i + 1 < len(W_GROUPS):
            gathers[i + 1] = _push_start([send[k] for k in W_GROUPS[i + 1]], False, lands[0], f"gather{i + 1}_start")
            dep = gathers[i + 1]["token"]
        return full, dep

    def send_grads(i, by_name):
        slots = [by_name[k].reshape((N_DEV,) + send[k].shape) for k in G_GROUPS[i]]
        exchanges[i] = _push_start(slots, True, slots[0], f"exchange{i}_start")
        return exchanges[i]["token"]

    comm = types.SimpleNamespace(weights=weights, grads=send_grads)
    small = {k: w[k] for k in SMALL_REPLICATED}
    conv_shapes = [w[k].shape[1:] for k in SMALL_SHARDED]
    packed_c, conv_offs = _pack_rows([w[k][0] for k in SMALL_SHARDED], F32, 8)
    conv_blocks = _unpack_rows(_all_gather([packed_c], "gather_conv_weights")[0], conv_offs, conv_shapes)
    for k, b in zip(SMALL_SHARDED, conv_blocks):
        small[k] = _full_from_slots(b, True)[None]

    loss_local, grad_x, grads = _local_step(inputs["x"], inputs["mem"], inputs["loss_target"], small, comm)
    loss = lax.psum(loss_local, AXES)

    out = {}
    for i, names in enumerate(G_GROUPS):
        sent, lands = _push_wait(exchanges[i], True, grad_x, f"exchange{i}_wait")
        for k, land, slots in zip(names, lands, sent):
            parts = own_slot(land, lax.dynamic_index_in_dim(slots, me, 0, keepdims=False))
            if k in COL_SHARDED and w[k].shape[2] % LANES:
                flip = lambda a: a.transpose(0, 2, 1)
                out[k] = tuple(flip(o) for o in _sum8_adamw(parts, flip(w[k]), flip(mom[k]), flip(vel[k]), f"sum_adamw_{k}"))
                continue
            if k in COL_SHARDED:
                parts = parts.transpose(0, 2, 1)
            out[k] = tuple(_sum8_adamw(parts, w[k], mom[k], vel[k], f"sum_adamw_{k}"))

    small_names = SMALL_REPLICATED + SMALL_SHARDED
    packed_g, small_offs = _pack_rows([grads[k] for k in small_names], F32, 8)
    total = _sum8(_all_gather([packed_g], "gather_small_grads")[0], "sum_small_grads")
    full_grads = _unpack_rows(total, small_offs, [grads[k].shape for k in small_names])
    mine = {}
    for k, g in zip(small_names, full_grads):
        if k in SMALL_SHARDED:
            c_loc = w[k].shape[2]
            g = lax.dynamic_slice_in_dim(g, me * c_loc, c_loc, axis=1)
        mine[k] = g.reshape(w[k].shape)
    sg, s_offs = _pack_rows([mine[k] for k in small_names], F32, 8)
    sw, _ = _pack_rows([w[k] for k in small_names], F32, 8)
    sm, _ = _pack_rows([mom[k] for k in small_names], F32, 8)
    sv, _ = _pack_rows([vel[k] for k in small_names], F32, 8)
    s_shapes = [w[k].shape for k in small_names]
    small_out = [_unpack_rows(a, s_offs, s_shapes) for a in _adamw(sg, sw, sm, sv, "adamw_small")]
    for i, k in enumerate(small_names):
        out[k] = (mine[k],) + tuple(o[i] for o in small_out)

    res = [loss, grad_x]
    for j in range(4):
        res += [out[k][j] for k in WEIGHT_ORDER]
    return tuple(res)


def kernel(x, mem, ffn1_norm, ffn1_w_gate_up, ffn1_w_down, mix_norm, w_in, conv_a_w, w_out_a, ssm_conv_w, ssm_conv_b, ssm_dt_bias, ssm_a_log, ssm_d, ssm_norm, w_out_ssm, w_mix_out, xattn_norm, mem_norm, w_q, w_kv, w_o_x, ffn2_norm, ffn2_w_gate_up, ffn2_w_down, final_norm, loss_target, m_ffn1_norm, m_ffn1_w_gate_up, m_ffn1_w_down, m_mix_norm, m_w_in, m_conv_a_w, m_w_out_a, m_ssm_conv_w, m_ssm_conv_b, m_ssm_dt_bias, m_ssm_a_log, m_ssm_d, m_ssm_norm, m_w_out_ssm, m_w_mix_out, m_xattn_norm, m_mem_norm, m_w_q, m_w_kv, m_w_o_x, m_ffn2_norm, m_ffn2_w_gate_up, m_ffn2_w_down, m_final_norm, v_ffn1_norm, v_ffn1_w_gate_up, v_ffn1_w_down, v_mix_norm, v_w_in, v_conv_a_w, v_w_out_a, v_ssm_conv_w, v_ssm_conv_b, v_ssm_dt_bias, v_ssm_a_log, v_ssm_d, v_ssm_norm, v_w_out_ssm, v_w_mix_out, v_xattn_norm, v_mem_norm, v_w_q, v_w_kv, v_w_o_x, v_ffn2_norm, v_ffn2_w_gate_up, v_ffn2_w_down, v_final_norm):
    return _step(dict(locals()))
```

```python
import functools
import math
import types

import jax
import jax.numpy as jnp
from jax import lax
from jax.experimental import pallas as pl
from jax.experimental.pallas import tpu as pltpu

F32, BF16 = jnp.float32, jnp.bfloat16
HI = lax.Precision.HIGHEST
MESH = pl.DeviceIdType.MESH
AXES = ("x", "y", "c")
N_DEV = 8

EPS = 1e-6
FFN_RES_WEIGHT = 0.5
SSM_HEAD_DIM = 64
SSM_GROUPS = 4
SSM_STATE = 128
SSM_CHUNK = 128
CONV_A_K = 3
SSM_CONV_K = 4
XATTN_HEADS = 4
ADAM_LR, ADAM_B1, ADAM_B2, ADAM_EPS, ADAM_WD, ADAM_STEP = 1e-3, 0.9, 0.999, 1e-8, 0.01, 10

LANES = 128
BF16_SUBLANES = 16
VMEM_LIMIT_BYTES = 56 * 2 ** 20
NEG_BIG = -1e30

BIG_WEIGHTS = ("ffn1_w_gate_up", "ffn1_w_down", "w_in", "w_out_a", "w_out_ssm", "w_mix_out",
               "w_q", "w_kv", "w_o_x", "ffn2_w_gate_up", "ffn2_w_down")
COL_SHARDED = ("ffn1_w_gate_up", "w_in", "w_kv", "ffn2_w_gate_up")
SMALL_REPLICATED = ("ffn1_norm", "mix_norm", "ssm_conv_b", "ssm_dt_bias", "ssm_a_log", "ssm_d", "ssm_norm",
                    "xattn_norm", "mem_norm", "ffn2_norm", "final_norm")
SMALL_SHARDED = ("conv_a_w", "ssm_conv_w")
WEIGHT_ORDER = ("ffn1_norm", "ffn1_w_gate_up", "ffn1_w_down", "mix_norm", "w_in", "conv_a_w", "w_out_a",
                "ssm_conv_w", "ssm_conv_b", "ssm_dt_bias", "ssm_a_log", "ssm_d", "ssm_norm", "w_out_ssm",
                "w_mix_out", "xattn_norm", "mem_norm", "w_q", "w_kv", "w_o_x", "ffn2_norm", "ffn2_w_gate_up",
                "ffn2_w_down", "final_norm")


def _tile(dim, pref, unit):
    best = None
    t = unit
    while t <= min(dim, pref):
        if dim % t == 0:
            best = t
        t += unit
    return best if best is not None else dim


def _params(*sem):
    return pltpu.CompilerParams(dimension_semantics=sem, vmem_limit_bytes=VMEM_LIMIT_BYTES)


def _sigmoid(x):
    return pl.reciprocal(1.0 + jnp.exp(-x), approx=True)


def _silu(x):
    return x * _sigmoid(x)


def _dsilu(x):
    s = _sigmoid(x)
    return s * (1.0 + x * (1.0 - s))


def _softplus(x):
    return jnp.maximum(x, 0.0) + jnp.log(1.0 + jnp.exp(-jnp.abs(x)))


def _dot(a, b, dims=(((1,), (0,)), ((), ())), precision=None):
    return lax.dot_general(a, b, dims, preferred_element_type=F32, precision=precision)


def _stack_rows(rows, width):
    r_idx = lax.broadcasted_iota(jnp.int32, (8, width), 0)
    acc = jnp.zeros((8, width), F32)
    for k, row in enumerate(rows):
        acc = jnp.where(r_idx == k, row, acc)
    return acc


NT = (((1,), (1,)), ((), ()))
TN = (((0,), (0,)), ((), ()))


def _mm(a, b, out_dtype, name, res=None, alpha=1.0, nt=False, dep=None, tm=1024, tn=2048, tk=2816):
    pieces = list(a) if isinstance(a, (list, tuple)) else [a]
    m = pieces[0].shape[0]
    k = sum(p.shape[1] for p in pieces)
    n = b.shape[0] if nt else b.shape[1]
    assert (b.shape[1] if nt else b.shape[0]) == k
    tm, tn = _tile(m, tm, 8), _tile(n, tn, LANES)
    tk = _tile(math.gcd(*[p.shape[1] for p in pieces]), tk, LANES)
    nk = k // tk
    starts, s0 = [], 0
    for p in pieces:
        starts.append((s0, p.shape[1] // tk))
        s0 += p.shape[1] // tk
    n_p = len(pieces)

    def body(*refs):
        a_refs, b_ref = refs[:n_p], refs[n_p]
        r_ref = refs[n_p + 1] if res is not None else None
        n_in = n_p + 1 + (res is not None) + (dep is not None)
        o_ref = refs[n_in]
        scr = refs[n_in + 1:]

        def finish(acc):
            acc = alpha * acc if alpha != 1.0 else acc
            if r_ref is not None:
                acc = r_ref[...] + acc
            o_ref[...] = acc.astype(out_dtype)

        def product(a_ref):
            return _dot(a_ref[...].astype(BF16), b_ref[...].astype(BF16), NT if nt else (((1,), (0,)), ((), ())))

        if nk == 1:
            finish(product(a_refs[0]))
            return
        acc_ref = scr[0]
        kk = pl.program_id(2)
        for (s, cnt), a_ref in zip(starts, a_refs):
            if s == 0:
                @pl.when(kk == 0)
                def _():
                    acc_ref[...] = product(a_ref)

                @pl.when(jnp.logical_and(kk > 0, kk < cnt))
                def _():
                    acc_ref[...] += product(a_ref)
            else:
                @pl.when(jnp.logical_and(kk >= s, kk < s + cnt))
                def _():
                    acc_ref[...] += product(a_ref)

        @pl.when(kk == nk - 1)
        def _():
            finish(acc_ref[...])

    def a_spec(s, cnt):
        return pl.BlockSpec((tm, tk), lambda i, j, kk: (i, jnp.clip(kk - s, 0, cnt - 1)))

    in_specs = [a_spec(s, cnt) for s, cnt in starts]
    in_specs.append(pl.BlockSpec((tn, tk), lambda i, j, kk: (j, kk)) if nt else pl.BlockSpec((tk, tn), lambda i, j, kk: (kk, j)))
    args = pieces + [b]
    if res is not None:
        in_specs.append(pl.BlockSpec((tm, tn), lambda i, j, kk: (i, j)))
        args.append(res)
    if dep is not None:
        in_specs.append(pl.BlockSpec((8, LANES), lambda i, j, kk: (0, 0)))
        args.append(dep)
    return pl.pallas_call(
        body, name=name, grid=(m // tm, n // tn, nk), in_specs=in_specs,
        out_specs=pl.BlockSpec((tm, tn), lambda i, j, kk: (i, j)),
        out_shape=jax.ShapeDtypeStruct((m, n), out_dtype),
        scratch_shapes=[pltpu.VMEM((tm, tn), F32)] if nk > 1 else [],
        compiler_params=_params("parallel", "parallel", "arbitrary"),
    )(*args)


def _mm_tn(x, dy, name, out_dtype=BF16, alpha=1.0, dep=None, out_rows=None, row_off=0, into=None,
           tko=1408, tn=1024, tt=2048):
    t, k = x.shape
    n = dy.shape[1]
    tko, tn, tt = _tile(k, tko, LANES), _tile(n, tn, LANES), _tile(t, tt, 8)
    nt_steps = t // tt

    def body(*refs):
        x_ref, dy_ref = refs[:2]
        o_ref, acc_ref = refs[-2:]
        part = _dot(x_ref[...].astype(BF16), dy_ref[...].astype(BF16), TN)
        step = pl.program_id(2)

        @pl.when(step == 0)
        def _():
            acc_ref[...] = part

        @pl.when(step > 0)
        def _():
            acc_ref[...] += part

        @pl.when(step == nt_steps - 1)
        def _():
            acc = acc_ref[...]
            o_ref[...] = (alpha * acc if alpha != 1.0 else acc).astype(out_dtype)

    in_specs = [pl.BlockSpec((tt, tko), lambda i, j, s: (s, i)), pl.BlockSpec((tt, tn), lambda i, j, s: (s, j))]
    args = [x, dy]
    if dep is not None:
        in_specs.append(pl.BlockSpec((8, LANES), lambda i, j, s: (0, 0)))
        args.append(dep)
    aliases = {}
    if into is not None:
        in_specs.append(pl.BlockSpec(memory_space=pl.ANY))
        args.append(into)
        aliases = {len(args) - 1: 0}
    band = row_off // tko
    assert row_off % tko == 0
    return pl.pallas_call(
        body, name=name, grid=(k // tko, n // tn, nt_steps), in_specs=in_specs,
        out_specs=pl.BlockSpec((tko, tn), lambda i, j, s: (i + band, j)),
        out_shape=jax.ShapeDtypeStruct((out_rows or k, n), out_dtype),
        scratch_shapes=[pltpu.VMEM((tko, tn), F32)], input_output_aliases=aliases,
        compiler_params=_params("parallel", "parallel", "arbitrary"),
    )(*args)


def _rms_fwd(x, g, name, tt=512):
    t, d = x.shape
    tt = _tile(t, tt, 8)

    def body(x_ref, g_ref, o_ref):
        xv = x_ref[...]
        r = lax.rsqrt(jnp.mean(xv * xv, axis=-1, keepdims=True) + EPS)
        o_ref[...] = (xv * r * g_ref[...]).astype(BF16)

    return pl.pallas_call(
        body, name=name, grid=(t // tt,),
        in_specs=[pl.BlockSpec((tt, d), lambda i: (i, 0)), pl.BlockSpec((1, d), lambda i: (0, 0))],
        out_specs=pl.BlockSpec((tt, d), lambda i: (i, 0)),
        out_shape=jax.ShapeDtypeStruct((t, d), BF16), compiler_params=_params("parallel"),
    )(x, g)


def _rms_bwd(dn, x, g, name, dres=None, tt=512):
    t, d = x.shape
    tt = _tile(t, tt, 8)

    def body(*refs):
        if dres is None:
            dn_ref, x_ref, g_ref, dx_ref, dg_ref = refs
            r_ref = None
        else:
            dn_ref, x_ref, g_ref, r_ref, dx_ref, dg_ref = refs
        xv, dnv = x_ref[...], dn_ref[...].astype(F32)
        r = lax.rsqrt(jnp.mean(xv * xv, axis=-1, keepdims=True) + EPS)
        xh = xv * r
        gy = dnv * g_ref[...]
        dx = r * (gy - xh * jnp.mean(gy * xh, axis=-1, keepdims=True))
        if r_ref is not None:
            dx = dx + r_ref[...]
        dx_ref[...] = dx
        part = jnp.sum(dnv * xh, axis=0, keepdims=True)

        @pl.when(pl.program_id(0) == 0)
        def _():
            dg_ref[...] = part

        @pl.when(pl.program_id(0) > 0)
        def _():
            dg_ref[...] += part

    row = pl.BlockSpec((tt, d), lambda i: (i, 0))
    vec = pl.BlockSpec((1, d), lambda i: (0, 0))
    in_specs, args = [row, row, vec], [dn, x, g]
    if dres is not None:
        in_specs.append(row)
        args.append(dres)
    return pl.pallas_call(
        body, name=name, grid=(t // tt,), in_specs=in_specs, out_specs=[row, vec],
        out_shape=[jax.ShapeDtypeStruct((t, d), F32), jax.ShapeDtypeStruct((1, d), F32)],
        compiler_params=_params("arbitrary"),
    )(*args)


def _ffn_up(n, w_gu_t, name, tm=1024, tf=1408):
    t, d = n.shape
    f = w_gu_t.shape[0] // 2
    tm, tf = _tile(t, tm, 8), _tile(f, tf, LANES)
    nf = f // tf

    def body(n_ref, wg_ref, wu_ref, g_ref, u_ref, a_ref):
        nv = n_ref[...]
        gate, up = _dot(nv, wg_ref[...], NT), _dot(nv, wu_ref[...], NT)
        s = _sigmoid(gate)
        sg = gate * s
        g_ref[...] = (up * (s * (1.0 + gate * (1.0 - s)))).astype(BF16)
        u_ref[...] = sg.astype(BF16)
        a_ref[...] = (sg * up).astype(BF16)

    blk = pl.BlockSpec((tm, tf), lambda i, j: (i, j))
    out = jax.ShapeDtypeStruct((t, f), BF16)
    return pl.pallas_call(
        body, name=name, grid=(t // tm, nf),
        in_specs=[pl.BlockSpec((tm, d), lambda i, j: (i, 0)), pl.BlockSpec((tf, d), lambda i, j: (j, 0)),
                  pl.BlockSpec((tf, d), lambda i, j: (j + nf, 0))],
        out_specs=[blk, blk, blk], out_shape=[out, out, out], compiler_params=_params("parallel", "parallel"),
    )(n, w_gu_t, w_gu_t)


def _ffn_da(dh, w_d, gate, up, name, alpha, dep=None, tm=1024, tf=1408):
    t, d = dh.shape
    f = w_d.shape[0]
    tm, tf = _tile(t, tm, 8), _tile(f, tf, LANES)

    def body(*refs):
        dh_ref, w_ref, g_ref, u_ref = refs[:4]
        dg_ref, du_ref = refs[-2:]
        da = alpha * _dot(dh_ref[...].astype(BF16), w_ref[...], NT)
        dg_ref[...] = (da * g_ref[...].astype(F32)).astype(BF16)
        du_ref[...] = (da * u_ref[...].astype(F32)).astype(BF16)

    blk = pl.BlockSpec((tm, tf), lambda i, j: (i, j))
    in_specs = [pl.BlockSpec((tm, d), lambda i, j: (i, 0)), pl.BlockSpec((tf, d), lambda i, j: (j, 0)), blk, blk]
    args = [dh, w_d, gate, up]
    if dep is not None:
        in_specs.append(pl.BlockSpec((8, LANES), lambda i, j: (0, 0)))
        args.append(dep)
    out = jax.ShapeDtypeStruct((t, f), BF16)
    return pl.pallas_call(
        body, name=name, grid=(t // tm, f // tf), in_specs=in_specs, out_specs=[blk, blk], out_shape=[out, out],
        compiler_params=_params("parallel", "parallel"),
    )(*args)


def _merge_fwd(ya, yb, proj, ga_blk, gb_blk, d, name, tt=512):
    t = ya.shape[0]
    tt = _tile(t, tt, 8)

    def body(ya_ref, yb_ref, ga_ref, gb_ref, o_ref):
        o_ref[...] = (_sigmoid(ga_ref[...].astype(F32)) * ya_ref[...].astype(F32)
                      + _sigmoid(gb_ref[...].astype(F32)) * yb_ref[...].astype(F32)).astype(BF16)

    row = pl.BlockSpec((tt, d), lambda i: (i, 0))
    return pl.pallas_call(
        body, name=name, grid=(t // tt,),
        in_specs=[row, row, pl.BlockSpec((tt, d), lambda i: (i, ga_blk)), pl.BlockSpec((tt, d), lambda i: (i, gb_blk))],
        out_specs=row, out_shape=jax.ShapeDtypeStruct((t, d), BF16), compiler_params=_params("parallel"),
    )(ya, yb, proj, proj)


def _merge_bwd(dm, ya, yb, proj, ga_blk, gb_blk, d, name, tt=512):
    t = ya.shape[0]
    tt = _tile(t, tt, 8)

    def body(dm_ref, ya_ref, yb_ref, ga_ref, gb_ref, dya_ref, dyb_ref, dga_ref, dgb_ref):
        dmv = dm_ref[...].astype(F32)
        sa, sb = _sigmoid(ga_ref[...].astype(F32)), _sigmoid(gb_ref[...].astype(F32))
        dya_ref[...] = (dmv * sa).astype(BF16)
        dyb_ref[...] = (dmv * sb).astype(BF16)
        dga_ref[...] = (dmv * ya_ref[...].astype(F32) * sa * (1.0 - sa)).astype(BF16)
        dgb_ref[...] = (dmv * yb_ref[...].astype(F32) * sb * (1.0 - sb)).astype(BF16)

    row = pl.BlockSpec((tt, d), lambda i: (i, 0))
    out = jax.ShapeDtypeStruct((t, d), BF16)
    return pl.pallas_call(
        body, name=name, grid=(t // tt,),
        in_specs=[row, row, row, pl.BlockSpec((tt, d), lambda i: (i, ga_blk)), pl.BlockSpec((tt, d), lambda i: (i, gb_blk))],
        out_specs=[row] * 4, out_shape=[out] * 4, compiler_params=_params("parallel"),
    )(dm, ya, yb, proj, proj)


def _loss_head(h, g, target, name, tt=512):
    t, d = h.shape
    tt = _tile(t, tt, 8)

    def body(h_ref, g_ref, tg_ref, loss_ref, dh_ref, dg_ref):
        xv = h_ref[...]
        r = lax.rsqrt(jnp.mean(xv * xv, axis=-1, keepdims=True) + EPS)
        xh = xv * r
        err = xh * g_ref[...] - tg_ref[...]
        dout = err * (1.0 / d)
        gy = dout * g_ref[...]
        dh_ref[...] = r * (gy - xh * jnp.mean(gy * xh, axis=-1, keepdims=True))
        dg_part = jnp.sum(dout * xh, axis=0, keepdims=True)
        loss_part = jnp.full((1, LANES), 0.5 / d, F32) * jnp.sum(err * err)

        @pl.when(pl.program_id(0) == 0)
        def _():
            dg_ref[...] = dg_part
            loss_ref[...] = loss_part

        @pl.when(pl.program_id(0) > 0)
        def _():
            dg_ref[...] += dg_part
            loss_ref[...] += loss_part

    row = pl.BlockSpec((tt, d), lambda i: (i, 0))
    vec = pl.BlockSpec((1, d), lambda i: (0, 0))
    return pl.pallas_call(
        body, name=name, grid=(t // tt,), in_specs=[row, vec, row],
        out_specs=[pl.BlockSpec((1, LANES), lambda i: (0, 0)), row, vec],
        out_shape=[jax.ShapeDtypeStruct((1, LANES), F32), jax.ShapeDtypeStruct((t, d), F32), jax.ShapeDtypeStruct((1, d), F32)],
        compiler_params=_params("arbitrary"),
    )(h, g, target)


def _shift_down(x, k, t_idx):
    if k == 0:
        return x
    return jnp.where(t_idx >= k, pltpu.roll(x, k, 0), 0.0)


def _shift_up(x, k, t_idx, s):
    if k == 0:
        return x
    return jnp.where(t_idx < s - k, pltpu.roll(x, s - k, 0), 0.0)


def _conv_a_fwd(proj, w, nb, s, d, name, cb=256):
    cb = _tile(d, cb, LANES)
    nd = d // cb

    def body(b_ref, c_ref, v_ref, w_ref, o_ref):
        t_idx = lax.broadcasted_iota(jnp.int32, (s, cb), 0)
        cv = c_ref[...].astype(F32) * v_ref[...].astype(F32)
        cc = sum(w_ref[k:k + 1, :] * _shift_down(cv, CONV_A_K - 1 - k, t_idx) for k in range(CONV_A_K))
        o_ref[...] = (b_ref[...].astype(F32) * cc).astype(BF16)

    def col(off):
        return pl.BlockSpec((s, cb), lambda b, j: (b, j + off * nd))

    return pl.pallas_call(
        body, name=name, grid=(nb, nd), in_specs=[col(0), col(1), col(2), pl.BlockSpec((8, cb), lambda b, j: (0, j))],
        out_specs=pl.BlockSpec((s, cb), lambda b, j: (b, j)),
        out_shape=jax.ShapeDtypeStruct((nb * s, d), BF16), compiler_params=_params("parallel", "parallel"),
    )(proj, proj, proj, w)


def _conv_a_bwd(dy, proj, w, nb, s, d, name, cb=256):
    cb = _tile(d, cb, LANES)
    nd = d // cb

    def body(dy_ref, b_ref, c_ref, v_ref, w_ref, db_ref, dc_ref, dv_ref, dw_ref):
        t_idx = lax.broadcasted_iota(jnp.int32, (s, cb), 0)
        cv_c, cv_v = c_ref[...].astype(F32), v_ref[...].astype(F32)
        cv = cv_c * cv_v
        shifted = [_shift_down(cv, CONV_A_K - 1 - k, t_idx) for k in range(CONV_A_K)]
        cc = sum(w_ref[k:k + 1, :] * shifted[k] for k in range(CONV_A_K))
        dyv = dy_ref[...].astype(F32)
        db_ref[...] = (dyv * cc).astype(BF16)
        dcc = dyv * b_ref[...].astype(F32)
        dcv = sum(w_ref[k:k + 1, :] * _shift_up(dcc, CONV_A_K - 1 - k, t_idx, s) for k in range(CONV_A_K))
        dc_ref[...] = (dcv * cv_v).astype(BF16)
        dv_ref[...] = (dcv * cv_c).astype(BF16)
        rows = [jnp.sum(dcc * shifted[k], axis=0, keepdims=True) for k in range(CONV_A_K)]
        part = _stack_rows(rows, cb)

        @pl.when(pl.program_id(1) == 0)
        def _():
            dw_ref[...] = part

        @pl.when(pl.program_id(1) > 0)
        def _():
            dw_ref[...] += part

    def col(off):
        return pl.BlockSpec((s, cb), lambda j, b: (b, j + off * nd))

    own = pl.BlockSpec((s, cb), lambda j, b: (b, j))
    wspec = pl.BlockSpec((8, cb), lambda j, b: (0, j))
    out = jax.ShapeDtypeStruct((nb * s, d), BF16)
    return pl.pallas_call(
        body, name=name, grid=(nd, nb), in_specs=[own, col(0), col(1), col(2), wspec],
        out_specs=[own, own, own, wspec], out_shape=[out, out, out, jax.ShapeDtypeStruct((8, d), F32)],
        compiler_params=_params("parallel", "arbitrary"),
    )(dy, proj, proj, proj, w)


def _conv_s_fwd(proj, col0, w, bias, nb, s, cc_width, name, cb=256):
    cb = _tile(math.gcd(cc_width, col0) if col0 else cc_width, cb, LANES)
    nd, off = cc_width // cb, col0 // cb

    def body(x_ref, w_ref, b_ref, o_ref, pre_ref):
        t_idx = lax.broadcasted_iota(jnp.int32, (s, cb), 0)
        xv = x_ref[...].astype(F32)
        pre = b_ref[...] + sum(w_ref[k:k + 1, :] * _shift_down(xv, SSM_CONV_K - 1 - k, t_idx) for k in range(SSM_CONV_K))
        o_ref[...] = _silu(pre).astype(BF16)
        pre_ref[...] = pre.astype(BF16)

    vec = pl.BlockSpec((8, cb), lambda b, j: (0, j))
    own = pl.BlockSpec((s, cb), lambda b, j: (b, j))
    out = jax.ShapeDtypeStruct((nb * s, cc_width), BF16)
    return pl.pallas_call(
        body, name=name, grid=(nb, nd),
        in_specs=[pl.BlockSpec((s, cb), lambda b, j: (b, j + off)), vec, pl.BlockSpec((1, cb), lambda b, j: (0, j))],
        out_specs=[own, own], out_shape=[out, out], compiler_params=_params("parallel", "parallel"),
    )(proj, w, bias)


def _conv_s_bwd(dxc, pre, proj, col0, w, nb, s, cc_width, name, cb=256):
    cb = _tile(math.gcd(cc_width, col0) if col0 else cc_width, cb, LANES)
    nd, off = cc_width // cb, col0 // cb

    def body(d_ref, pre_ref, x_ref, w_ref, dx_ref, dw_ref, db_ref):
        t_idx = lax.broadcasted_iota(jnp.int32, (s, cb), 0)
        xv = x_ref[...].astype(F32)
        dpre = d_ref[...].astype(F32) * _dsilu(pre_ref[...].astype(F32))
        ahead = [_shift_up(dpre, j, t_idx, s) for j in range(SSM_CONV_K)]
        dx_ref[...] = sum(w_ref[k:k + 1, :] * ahead[SSM_CONV_K - 1 - k] for k in range(SSM_CONV_K)).astype(BF16)
        rows = [jnp.sum(ahead[SSM_CONV_K - 1 - k] * xv, axis=0, keepdims=True) for k in range(SSM_CONV_K)]
        dw_part = _stack_rows(rows, cb)
        db_part = jnp.sum(dpre, axis=0, keepdims=True)

        @pl.when(pl.program_id(1) == 0)
        def _():
            dw_ref[...] = dw_part
            db_ref[...] = db_part

        @pl.when(pl.program_id(1) > 0)
        def _():
            dw_ref[...] += dw_part
            db_ref[...] += db_part

    own = pl.BlockSpec((s, cb), lambda j, b: (b, j))
    wspec = pl.BlockSpec((8, cb), lambda j, b: (0, j))
    bspec = pl.BlockSpec((1, cb), lambda j, b: (0, j))
    return pl.pallas_call(
        body, name=name, grid=(nd, nb),
        in_specs=[own, own, pl.BlockSpec((s, cb), lambda j, b: (b, j + off)), wspec],
        out_specs=[own, wspec, bspec],
        out_shape=[jax.ShapeDtypeStruct((nb * s, cc_width), BF16), jax.ShapeDtypeStruct((8, cc_width), F32),
                   jax.ShapeDtypeStruct((1, cc_width), F32)],
        compiler_params=_params("parallel", "arbitrary"),
    )(dxc, pre, proj, w)


def _split3(v):
    hi = v.astype(BF16)
    r1 = v - hi.astype(F32)
    mid = r1.astype(BF16)
    return hi, mid, (r1 - mid.astype(F32)).astype(BF16)


def _exact_left(mask_b, v):
    return sum(_dot(mask_b, t) for t in _split3(v))


def _exact_right(v, mask_b):
    return sum(_dot(t, mask_b) for t in _split3(v))


def _head_sums(v, e_b):
    return _dot(v.astype(BF16), e_b, NT)


def _spread(v, out_ref, di):
    lane = lax.broadcasted_iota(jnp.int32, (v.shape[0], LANES), 1)
    for pr in range(di // LANES):
        h0 = pr * (LANES // SSM_HEAD_DIM)
        out_ref[:, pr * LANES:(pr + 1) * LANES] = jnp.where(lane < SSM_HEAD_DIM, v[:, h0:h0 + 1], v[:, h0 + 1:h0 + 2])


def _ssd_common(xc_ref, dtr_ref, dtrt_ref, prow_ref, pcol_ref, dtx_ref, acsx_ref, dx_ref, di):
    l = SSM_CHUNK
    bias_r, a_r = prow_ref[0:1, :], -jnp.exp(prow_ref[1:2, :])
    sp_in = dtr_ref[...] + bias_r
    dt = _softplus(sp_in)
    li = lax.broadcasted_iota(jnp.int32, (l, l), 0)
    si = lax.broadcasted_iota(jnp.int32, (l, l), 1)
    lower_b = (li >= si).astype(BF16)
    upper_b = (li <= si).astype(BF16)
    acs = _exact_left(lower_b, dt * a_r)
    bias_c, a_c = pcol_ref[:, 0:1], -jnp.exp(pcol_ref[:, 1:2])
    dt_t = _softplus(dtrt_ref[...] + bias_c)
    acs_t = _exact_right(dt_t * a_c, upper_b)
    _spread(dt, dtx_ref, di)
    _spread(acs, acsx_ref, di)
    _spread(prow_ref[0:8, :], dx_ref, di)
    acs_exp = acsx_ref[...]
    acs_last = acs_exp[l - 1:l, :]
    x = xc_ref[:, 0:di].astype(F32)
    return dict(dt=dt, a_r=a_r, sp_in=sp_in, acs=acs, acs_t=acs_t, dt_exp=dtx_ref[...], e_exp=jnp.exp(acs_exp),
                el_exp=jnp.exp(acs_last), f_exp=jnp.exp(acs_last - acs_exp), x=x, mask=li >= si, upper_b=upper_b,
                d_exp=dx_ref[2:3, :])


def _decay(q, h):
    seg = q["acs"][:, h:h + 1] - q["acs_t"][h:h + 1, :]
    return jnp.exp(jnp.where(q["mask"], seg, NEG_BIG))


def _ssd_fwd(xc, dtr, dtrt, prow, pcol, nb, nc, di, name):
    l, n, g_n, p = SSM_CHUNK, SSM_STATE, SSM_GROUPS, SSM_HEAD_DIM
    cc = xc.shape[1]
    gw = di // g_n
    assert p * 2 == LANES and gw % LANES == 0

    def body(xc_ref, dtr_ref, dtrt_ref, prow_ref, pcol_ref, y_ref, sprev_ref, st_ref, dtx_ref, acsx_ref, dx_ref):
        @pl.when(pl.program_id(1) == 0)
        def _():
            st_ref[...] = jnp.zeros_like(st_ref)

        q = _ssd_common(xc_ref, dtr_ref, dtrt_ref, prow_ref, pcol_ref, dtx_ref, acsx_ref, dx_ref, di)
        x = q["x"]
        xd = x * q["dt_exp"]
        xdb = xd.astype(BF16)
        xdf = (xd * q["f_exp"]).astype(BF16)
        lane = lax.broadcasted_iota(jnp.int32, (l, LANES), 1)
        for g in range(g_n):
            lo = g * gw
            bg = xc_ref[:, di + g * n: di + (g + 1) * n]
            cg = xc_ref[:, di + g_n * n + g * n: di + g_n * n + (g + 1) * n]
            cb = _dot(cg, bg, NT)
            st_g = st_ref[:, lo:lo + gw]
            y_off = q["e_exp"][:, lo:lo + gw] * _dot(cg, st_g.astype(BF16))
            for pr in range(gw // LANES):
                c0 = lo + pr * LANES
                h0 = c0 // p
                xp = xdb[:, c0:c0 + LANES]
                m0 = (cb * _decay(q, h0)).astype(BF16)
                m1 = (cb * _decay(q, h0 + 1)).astype(BF16)
                yd = _dot(m0, jnp.where(lane < p, xp, 0)) + _dot(m1, jnp.where(lane >= p, xp, 0))
                y_ref[:, c0:c0 + LANES] = (yd + y_off[:, pr * LANES:(pr + 1) * LANES]
                                           + q["d_exp"][:, c0:c0 + LANES] * x[:, c0:c0 + LANES])
            sprev_ref[:, lo:lo + gw] = st_g
            st_ref[:, lo:lo + gw] = q["el_exp"][:, lo:lo + gw] * st_g + _dot(bg, xdf[:, lo:lo + gw], TN)

    tok = lambda w: pl.BlockSpec((l, w), lambda b, c: (b * nc + c, 0))
    const = lambda r, w: pl.BlockSpec((r, w), lambda b, c: (0, 0))
    return pl.pallas_call(
        body, name=name, grid=(nb, nc),
        in_specs=[tok(cc), tok(LANES), pl.BlockSpec((LANES, l), lambda b, c: (0, b * nc + c)),
                  const(8, LANES), const(LANES, 8)],
        out_specs=[tok(di), pl.BlockSpec((None, n, di), lambda b, c: (b * nc + c, 0, 0))],
        out_shape=[jax.ShapeDtypeStruct((nb * nc * l, di), F32), jax.ShapeDtypeStruct((nb * nc, n, di), F32)],
        scratch_shapes=[pltpu.VMEM((n, di), F32), pltpu.VMEM((l, di), F32), pltpu.VMEM((l, di), F32),
                        pltpu.VMEM((8, di), F32)],
        compiler_params=_params("parallel", "arbitrary"),
    )(xc, dtr, dtrt, prow, pcol)


def _ssd_bwd(dy, xc, dtr, dtrt, prow, pcol, e_mat, sprev, nb, nc, di, name):
    l, n, g_n, p = SSM_CHUNK, SSM_STATE, SSM_GROUPS, SSM_HEAD_DIM
    cc = xc.shape[1]
    gw = di // g_n

    def body(dy_ref, xc_ref, dtr_ref, dtrt_ref, prow_ref, pcol_ref, e_ref, sprev_ref,
             dxc_ref, ddtr_ref, sums_ref, dst_ref, off_ref, dxd_ref, last_ref, vst_ref,
             dtx_ref, acsx_ref, dx_ref):
        first = jnp.logical_and(pl.program_id(0) == 0, pl.program_id(1) == 0)

        @pl.when(pl.program_id(1) == 0)
        def _():
            dst_ref[...] = jnp.zeros_like(dst_ref)

        head_row = lax.broadcasted_iota(jnp.int32, (LANES, l), 0)
        row_sums, col_sums = jnp.zeros((l, LANES), F32), jnp.zeros((LANES, l), F32)
        strict_lower = lax.broadcasted_iota(jnp.int32, (l, l), 0) > lax.broadcasted_iota(jnp.int32, (l, l), 1)

        q = _ssd_common(xc_ref, dtr_ref, dtrt_ref, prow_ref, pcol_ref, dtx_ref, acsx_ref, dx_ref, di)
        x = q["x"]
        xd = x * q["dt_exp"]
        xdb = xd.astype(BF16)
        xdf = (xd * q["f_exp"]).astype(BF16)
        dyv = dy_ref[...]
        dyb = dyv.astype(BF16)
        dye = (dyv * q["e_exp"]).astype(BF16)
        upper_b = q["upper_b"]
        lane = lax.broadcasted_iota(jnp.int32, (l, LANES), 1)
        for g in range(g_n):
            lo = g * gw
            bg = xc_ref[:, di + g * n: di + (g + 1) * n]
            cg = xc_ref[:, di + g_n * n + g * n: di + g_n * n + (g + 1) * n]
            cb = _dot(cg, bg, NT)
            st_g = sprev_ref[:, lo:lo + gw]
            st_gb = st_g.astype(BF16)
            dst_g = dst_ref[:, lo:lo + gw]
            dst_gb = dst_g.astype(BF16)
            dye_g = dye[:, lo:lo + gw]
            xdf_g = xdf[:, lo:lo + gw]
            y_off = q["e_exp"][:, lo:lo + gw] * _dot(cg, st_gb)
            dc_g = _dot(dye_g, st_gb, NT)
            db_g = _dot(xdf_g, dst_gb, NT)
            dxd_state = _dot(bg, dst_gb) * q["f_exp"][:, lo:lo + gw]
            last_ref[:, lo:lo + gw] = jnp.sum(dst_g * st_g, axis=0, keepdims=True)
            dst_ref[:, lo:lo + gw] = q["el_exp"][:, lo:lo + gw] * dst_g + _dot(cg, dye_g, TN)
            off_ref[:, lo:lo + gw] = dyv[:, lo:lo + gw] * y_off
            vst_ref[:, lo:lo + gw] = xd[:, lo:lo + gw] * dxd_state
            dcb = jnp.zeros((l, l), F32)
            for pr in range(gw // LANES):
                c0 = lo + pr * LANES
                h0 = c0 // p
                xp = xdb[:, c0:c0 + LANES]
                dyp = dyb[:, c0:c0 + LANES]
                dxd_diag = jnp.zeros((l, LANES), F32)
                for k, keep in enumerate((lane < p, lane >= p)):
                    dec = _decay(q, h0 + k)
                    dy_h = jnp.where(keep, dyp, 0)
                    dm_dec = _dot(dy_h, xp, NT) * dec
                    dcb = dcb + dm_dec
                    dxd_diag = dxd_diag + _dot((cb * dec).astype(BF16), dy_h, TN)
                    qm = dm_dec * cb
                    row_sums = jnp.where(lane == h0 + k, jnp.sum(qm, axis=1, keepdims=True), row_sums)
                    col_sums = jnp.where(head_row == h0 + k, jnp.sum(qm, axis=0, keepdims=True), col_sums)
                dxd_ref[:, c0:c0 + LANES] = dxd_diag + dxd_state[:, pr * LANES:(pr + 1) * LANES]
            dcb_b = dcb.astype(BF16)
            dxc_ref[:, di + g * n: di + (g + 1) * n] = (db_g + _dot(dcb_b, cg, TN)).astype(BF16)
            dxc_ref[:, di + g_n * n + g * n: di + g_n * n + (g + 1) * n] = (dc_g + _dot(dcb_b, bg)).astype(BF16)
        dxd = dxd_ref[...]
        e_b = e_ref[...]
        from_y = _exact_left(upper_b, _head_sums(off_ref[...], e_b) + row_sums - col_sums.T)
        from_s = _exact_left(strict_lower.astype(BF16), _head_sums(vst_ref[...], e_b))
        carried = _head_sums(jnp.broadcast_to(last_ref[...], (8, di)), e_b)[0:1, :] * jnp.exp(q["acs"][l - 1:l, :])
        dla = from_y + from_s + carried
        ddt = dla * q["a_r"] + _head_sums(dxd * x, e_b)
        ddtr = ddt * jax.nn.sigmoid(q["sp_in"])
        ddtr_ref[...] = ddtr
        dxc_ref[:, 0:di] = (dxd * q["dt_exp"] + q["d_exp"] * dyv).astype(BF16)
        dd_exp = jnp.sum(dyv * x, axis=0, keepdims=True)
        dd = _head_sums(jnp.broadcast_to(dd_exp, (8, di)), e_b)[0:1, :]
        part = _stack_rows([jnp.sum(ddtr, axis=0, keepdims=True),
                            jnp.sum(dla * q["dt"], axis=0, keepdims=True) * q["a_r"], dd], LANES)

        @pl.when(first)
        def _():
            sums_ref[...] = part

        @pl.when(jnp.logical_not(first))
        def _():
            sums_ref[...] += part

    rev = lambda b, c: b * nc + (nc - 1 - c)
    tok = lambda w: pl.BlockSpec((l, w), lambda b, c: (rev(b, c), 0))
    const = lambda r, w: pl.BlockSpec((r, w), lambda b, c: (0, 0))
    return pl.pallas_call(
        body, name=name, grid=(nb, nc),
        in_specs=[tok(di), tok(cc), tok(LANES), pl.BlockSpec((LANES, l), lambda b, c: (0, rev(b, c))),
                  const(8, LANES), const(LANES, 8), const(LANES, di),
                  pl.BlockSpec((None, n, di), lambda b, c: (rev(b, c), 0, 0))],
        out_specs=[tok(cc), tok(LANES), const(8, LANES)],
        out_shape=[jax.ShapeDtypeStruct((nb * nc * l, cc), BF16), jax.ShapeDtypeStruct((nb * nc * l, LANES), F32),
                   jax.ShapeDtypeStruct((8, LANES), F32)],
        scratch_shapes=[pltpu.VMEM((n, di), F32), pltpu.VMEM((l, di), F32), pltpu.VMEM((l, di), F32),
                        pltpu.VMEM((1, di), F32), pltpu.VMEM((l, di), F32),
                        pltpu.VMEM((l, di), F32), pltpu.VMEM((l, di), F32), pltpu.VMEM((8, di), F32)],
        compiler_params=_params("arbitrary", "arbitrary"),
    )(dy, xc, dtr, dtrt, prow, pcol, e_mat, sprev)


def _gate_norm_fwd(y, proj, z_col0, norm_g, di, name, tt=256):
    t = y.shape[0]
    tt = _tile(t, tt, 8)
    gw = di // SSM_GROUPS
    zw = _tile(math.gcd(di, z_col0), di, LANES)
    nz, zoff = di // zw, z_col0 // zw

    def body(*refs):
        y_ref, z_refs, g_ref, o_ref = refs[0], refs[1:1 + nz], refs[1 + nz], refs[2 + nz]
        for g in range(SSM_GROUPS):
            lo = g * gw
            zv = z_refs[lo // zw][:, lo % zw:lo % zw + gw].astype(F32)
            yg = y_ref[:, lo:lo + gw] * _silu(zv)
            r = lax.rsqrt(jnp.mean(yg * yg, axis=-1, keepdims=True) + EPS)
            o_ref[:, lo:lo + gw] = (yg * r * g_ref[:, lo:lo + gw]).astype(BF16)

    row = pl.BlockSpec((tt, di), lambda i: (i, 0))
    zspecs = [pl.BlockSpec((tt, zw), functools.partial(lambda i, k: (i, zoff + k), k=k)) for k in range(nz)]
    return pl.pallas_call(
        body, name=name, grid=(t // tt,), in_specs=[row] + zspecs + [pl.BlockSpec((1, di), lambda i: (0, 0))],
        out_specs=row, out_shape=jax.ShapeDtypeStruct((t, di), BF16), compiler_params=_params("parallel"),
    )(y, *([proj] * nz), norm_g)


def _gate_norm_bwd(dn, y, proj, z_col0, norm_g, di, name, tt=256):
    t = y.shape[0]
    tt = _tile(t, tt, 8)
    gw = di // SSM_GROUPS
    zw = _tile(math.gcd(di, z_col0), di, LANES)
    nz, zoff = di // zw, z_col0 // zw

    def body(*refs):
        dn_ref, y_ref, z_refs, g_ref = refs[0], refs[1], refs[2:2 + nz], refs[2 + nz]
        dy_ref, dz_ref, dg_ref = refs[3 + nz:]
        first = pl.program_id(0) == 0
        for g in range(SSM_GROUPS):
            lo = g * gw
            zv = z_refs[lo // zw][:, lo % zw:lo % zw + gw].astype(F32)
            yv = y_ref[:, lo:lo + gw]
            sz = _silu(zv)
            yg = yv * sz
            r = lax.rsqrt(jnp.mean(yg * yg, axis=-1, keepdims=True) + EPS)
            yh = yg * r
            dnv = dn_ref[:, lo:lo + gw].astype(F32)
            gy = dnv * g_ref[:, lo:lo + gw]
            dyg = r * (gy - yh * jnp.mean(gy * yh, axis=-1, keepdims=True))
            dy_ref[:, lo:lo + gw] = dyg * sz
            dz_ref[:, lo:lo + gw] = (dyg * yv * _dsilu(zv)).astype(BF16)
            part = jnp.sum(dnv * yh, axis=0, keepdims=True)

            @pl.when(first)
            def _():
                dg_ref[:, lo:lo + gw] = part

            @pl.when(jnp.logical_not(first))
            def _():
                dg_ref[:, lo:lo + gw] += part

    row = pl.BlockSpec((tt, di), lambda i: (i, 0))
    vec = pl.BlockSpec((1, di), lambda i: (0, 0))
    zspecs = [pl.BlockSpec((tt, zw), functools.partial(lambda i, k: (i, zoff + k), k=k)) for k in range(nz)]
    return pl.pallas_call(
        body, name=name, grid=(t // tt,), in_specs=[row, row] + zspecs + [vec], out_specs=[row, row, vec],
        out_shape=[jax.ShapeDtypeStruct((t, di), F32), jax.ShapeDtypeStruct((t, di), BF16), jax.ShapeDtypeStruct((1, di), F32)],
        compiler_params=_params("arbitrary"),
    )(dn, y, *([proj] * nz), norm_g)


def _softmax_rows(s):
    s = s - jnp.max(s, axis=-1, keepdims=True)
    e = jnp.exp(s)
    return e * (1.0 / jnp.sum(e, axis=-1, keepdims=True))


def _xattn_fwd(q, kv, nb, s, m, d, name, tq=512):
    tq = _tile(s, tq, 8)
    nq = s // tq
    hd = d // XATTN_HEADS
    scale = 1.0 / math.sqrt(hd)

    def body(q_ref, k_ref, v_ref, o_ref):
        for h in range(XATTN_HEADS):
            sl = slice(h * hd, (h + 1) * hd)
            prob = _softmax_rows(_dot(q_ref[:, sl], k_ref[:, sl], NT) * scale)
            o_ref[:, sl] = _dot(prob.astype(BF16), v_ref[:, sl]).astype(BF16)

    return pl.pallas_call(
        body, name=name, grid=(nb, nq),
        in_specs=[pl.BlockSpec((tq, d), lambda b, i: (b * nq + i, 0)), pl.BlockSpec((m, d), lambda b, i: (b, 0)),
                  pl.BlockSpec((m, d), lambda b, i: (b, 1))],
        out_specs=pl.BlockSpec((tq, d), lambda b, i: (b * nq + i, 0)),
        out_shape=jax.ShapeDtypeStruct((nb * s, d), BF16), compiler_params=_params("parallel", "parallel"),
    )(q, kv, kv)


def _xattn_bwd(do, q, kv, nb, s, m, d, name, tq=512):
    tq = _tile(s, tq, 8)
    nq = s // tq
    hd = d // XATTN_HEADS
    scale = 1.0 / math.sqrt(hd)

    def body(do_ref, q_ref, k_ref, v_ref, dq_ref, dk_ref, dv_ref):
        first = pl.program_id(1) == 0
        for h in range(XATTN_HEADS):
            sl = slice(h * hd, (h + 1) * hd)
            qh, kh, vh, doh = q_ref[:, sl], k_ref[:, sl], v_ref[:, sl], do_ref[:, sl]
            prob = _softmax_rows(_dot(qh, kh, NT) * scale)
            dv_h = _dot(prob.astype(BF16), doh, TN)
            dp = _dot(doh, vh, NT)
            ds = (prob * (dp - jnp.sum(dp * prob, axis=-1, keepdims=True)) * scale).astype(BF16)
            dq_ref[:, sl] = _dot(ds, kh).astype(BF16)
            dk_h = _dot(ds, qh, TN)

            @pl.when(first)
            def _():
                dk_ref[:, sl] = dk_h
                dv_ref[:, sl] = dv_h

            @pl.when(jnp.logical_not(first))
            def _():
                dk_ref[:, sl] += dk_h
                dv_ref[:, sl] += dv_h

    qspec = pl.BlockSpec((tq, d), lambda b, i: (b * nq + i, 0))
    dq, dk, dv = pl.pallas_call(
        body, name=name, grid=(nb, nq),
        in_specs=[qspec, qspec, pl.BlockSpec((m, d), lambda b, i: (b, 0)), pl.BlockSpec((m, d), lambda b, i: (b, 1))],
        out_specs=[qspec, pl.BlockSpec((m, d), lambda b, i: (b, 0)), pl.BlockSpec((m, d), lambda b, i: (b, 0))],
        out_shape=[jax.ShapeDtypeStruct((nb * s, d), BF16), jax.ShapeDtypeStruct((nb * m, d), F32),
                   jax.ShapeDtypeStruct((nb * m, d), F32)],
        compiler_params=_params("parallel", "arbitrary"),
    )(do, q, kv, kv)
    return dq, dk, dv


def _all_gather(shards, name):
    n_arr = len(shards)

    def body(*refs):
        x_refs, out_refs = refs[:n_arr], refs[n_arr:2 * n_arr]
        send_sems, recv_sems, local_sems = refs[2 * n_arr:]
        x, y, c = lax.axis_index("x"), lax.axis_index("y"), lax.axis_index("c")
        me, sibling = (x, y, c), (x, y, 1 - c)
        chips = [(1 - x, y), (x, 1 - y), (1 - x, 1 - y)]

        def copy(w, k, block, to, from_input=False):
            px, py, pc = block
            rows = out_refs[w].at[4 * px + 2 * py + pc]
            return pltpu.make_async_remote_copy(
                src_ref=x_refs[w] if from_input else rows, dst_ref=rows,
                send_sem=send_sems.at[7 * w + k], recv_sem=recv_sems.at[7 * w + k], device_id=to, device_id_type=MESH)

        started = []
        for w in range(n_arr):
            mine = pltpu.make_async_copy(x_refs[w], out_refs[w].at[4 * x + 2 * y + c], local_sems.at[w])
            mine.start()
            started.append(mine)
        sends = []
        for w in range(n_arr):
            sends.append(copy(w, 0, me, sibling, from_input=True))
            sends += [copy(w, 1 + j, me, (*chip, c), from_input=True) for j, chip in enumerate(chips)]
        for cp in sends:
            cp.start()
        for j, chip in enumerate(chips):
            for w in range(n_arr):
                copy(w, 1 + j, (*chip, c), me).wait_recv()
                passed = copy(w, 4 + j, (*chip, c), sibling)
                passed.start()
                sends.append(passed)
        for w in range(n_arr):
            copy(w, 0, sibling, me).wait_recv()
            for j, chip in enumerate(chips):
                copy(w, 4 + j, (*chip, 1 - c), me).wait_recv()
        for cp in sends:
            cp.wait_send()
        for mine in started:
            mine.wait()

    hbm = pl.BlockSpec(memory_space=pl.ANY)
    return pl.pallas_call(
        body, name=name, out_shape=[jax.ShapeDtypeStruct((N_DEV,) + s.shape, s.dtype) for s in shards],
        in_specs=[hbm] * n_arr, out_specs=[hbm] * n_arr,
        scratch_shapes=[pltpu.SemaphoreType.DMA((7 * n_arr,)), pltpu.SemaphoreType.DMA((7 * n_arr,)),
                        pltpu.SemaphoreType.DMA((n_arr,))],
    )(*shards)


_HBM = pl.BlockSpec(memory_space=pltpu.HBM)
_SEM = pl.BlockSpec(memory_space=pltpu.SEMAPHORE)
_DATAFLOW = pltpu.SideEffectType.DATAFLOW_SIDE_EFFECTING


def _peer_list(x, y, c):
    return [(1 - x if k & 4 else x, 1 - y if k & 2 else y, 1 - c if k & 1 else c) for k in range(1, N_DEV)]


def _push_copy(src_ref, land_ref, send_sems, recv_sems, w, k, peer, me, per_peer_src, receiving):
    px, py, pc = peer
    peer_slot = 4 * px + 2 * py + pc
    return pltpu.make_async_remote_copy(
        src_ref=src_ref.at[peer_slot] if per_peer_src else src_ref,
        dst_ref=land_ref.at[peer_slot if receiving else me],
        send_sem=send_sems.at[7 * w + k], recv_sem=recv_sems.at[7 * w + k], device_id=peer, device_id_type=MESH)


def _push_start(srcs, per_peer_src, after, name):
    n_arr = len(srcs)
    land_shapes = [s.shape if per_peer_src else (N_DEV,) + s.shape for s in srcs]

    def body(*refs):
        src_refs, land_refs = refs[:n_arr], refs[n_arr:2 * n_arr]
        send_sems, recv_sems = refs[2 * n_arr + 1], refs[2 * n_arr + 2]
        token = refs[-1]
        x, y, c = lax.axis_index("x"), lax.axis_index("y"), lax.axis_index("c")
        me = 4 * x + 2 * y + c
        for w in range(n_arr):
            for k, peer in enumerate(_peer_list(x, y, c)):
                _push_copy(src_refs[w], land_refs[w], send_sems, recv_sems, w, k, peer, me, per_peer_src, False).start()
        token[...] = jnp.zeros_like(token)

    lands = [pltpu.with_memory_space_constraint(lax.empty(ls, s.dtype), pltpu.HBM) for ls, s in zip(land_shapes, srcs)]
    srcs_hbm = [pltpu.with_memory_space_constraint(s, pltpu.HBM) for s in srcs]
    out = pl.pallas_call(
        body, name=name,
        out_shape=(pltpu.SemaphoreType.DMA((7 * n_arr,)), pltpu.SemaphoreType.DMA((7 * n_arr,)),
                   *[pltpu.HBM(s.shape, s.dtype) for s in srcs], *[pltpu.HBM(ls, s.dtype) for ls, s in zip(land_shapes, srcs)],
                   jax.ShapeDtypeStruct((8, LANES), F32)),
        in_specs=[_HBM] * (2 * n_arr) + [pl.BlockSpec(memory_space=pl.ANY)],
        out_specs=(_SEM, _SEM, *([_HBM] * (2 * n_arr)), pl.BlockSpec(memory_space=pltpu.VMEM)),
        input_output_aliases={i: 2 + i for i in range(2 * n_arr)},
        compiler_params=pltpu.CompilerParams(has_side_effects=_DATAFLOW),
    )(*srcs_hbm, *lands, after)
    return dict(send=out[0], recv=out[1], srcs=list(out[2:2 + n_arr]), lands=list(out[2 + n_arr:2 + 2 * n_arr]),
                token=out[-1])


def _push_wait(pending, per_peer_src, after, name):
    n_arr = len(pending["srcs"])

    def body(*refs):
        src_refs, land_refs = refs[:n_arr], refs[n_arr:2 * n_arr]
        send_sems, recv_sems = refs[2 * n_arr], refs[2 * n_arr + 1]
        x, y, c = lax.axis_index("x"), lax.axis_index("y"), lax.axis_index("c")
        me = 4 * x + 2 * y + c
        for w in range(n_arr):
            for k, peer in enumerate(_peer_list(x, y, c)):
                cp = _push_copy(src_refs[w], land_refs[w], send_sems, recv_sems, w, k, peer, me, per_peer_src, True)
                cp.wait_send()
                cp.wait_recv()

    out = pl.pallas_call(
        body, name=name,
        out_shape=tuple(pltpu.HBM(a.shape, a.dtype) for a in pending["srcs"] + pending["lands"]),
        in_specs=[_HBM] * (2 * n_arr) + [_SEM, _SEM, pl.BlockSpec(memory_space=pl.ANY)],
        out_specs=tuple([_HBM] * (2 * n_arr)),
        input_output_aliases={i: i for i in range(2 * n_arr)},
        compiler_params=pltpu.CompilerParams(has_side_effects=_DATAFLOW),
    )(*pending["srcs"], *pending["lands"], pending["send"], pending["recv"], after)
    return list(out[:n_arr]), list(out[n_arr:])


def _adamw_math(w, g, m, v):
    m = ADAM_B1 * m + (1.0 - ADAM_B1) * g
    v = ADAM_B2 * v + (1.0 - ADAM_B2) * (g * g)
    m_hat = m / (1.0 - ADAM_B1 ** ADAM_STEP)
    v_hat = v / (1.0 - ADAM_B2 ** ADAM_STEP)
    delta = -ADAM_LR * (m_hat / (jnp.sqrt(v_hat) + ADAM_EPS) + ADAM_WD * w)
    return delta, m, v


def _sum8(parts, name, tr=512):
    _, r, c_dim = parts.shape
    tr = _tile(r, tr, BF16_SUBLANES)

    def body(p_ref, o_ref):
        acc = p_ref[0].astype(F32)
        for k in range(1, N_DEV):
            acc = acc + p_ref[k].astype(F32)
        o_ref[...] = acc

    return pl.pallas_call(
        body, name=name, grid=(r // tr,), in_specs=[pl.BlockSpec((N_DEV, tr, c_dim), lambda i: (0, i, 0))],
        out_specs=pl.BlockSpec((tr, c_dim), lambda i: (i, 0)),
        out_shape=jax.ShapeDtypeStruct((r, c_dim), F32), compiler_params=_params("parallel"),
    )(parts)


def _sum8_adamw(parts, w, m, v, name, tr=128):
    _, r, c_dim = parts.shape
    tr = _tile(r, tr, BF16_SUBLANES)
    tc = c_dim if tr <= 2 * LANES else _tile(c_dim, LANES, LANES)

    def body(p_ref, w_ref, m_ref, v_ref, g_ref, d_ref, nm_ref, nv_ref):
        g = p_ref[0].astype(F32)
        for k in range(1, N_DEV):
            g = g + p_ref[k].astype(F32)
        g_ref[...] = g
        d_ref[...], nm_ref[...], nv_ref[...] = _adamw_math(w_ref[...], g, m_ref[...], v_ref[...])

    blk = pl.BlockSpec((None, tr, tc), lambda i, j: (0, i, j))
    out = jax.ShapeDtypeStruct((1, r, c_dim), F32)
    return pl.pallas_call(
        body, name=name, grid=(r // tr, c_dim // tc),
        in_specs=[pl.BlockSpec((N_DEV, tr, tc), lambda i, j: (0, i, j)), blk, blk, blk],
        out_specs=[blk] * 4, out_shape=[out] * 4, compiler_params=_params("parallel", "parallel"),
    )(parts, w, m, v)


def _adamw(g, w, m, v, name):
    r, c_dim = g.shape

    def body(g_ref, w_ref, m_ref, v_ref, d_ref, nm_ref, nv_ref):
        d_ref[...], nm_ref[...], nv_ref[...] = _adamw_math(w_ref[...], g_ref[...], m_ref[...], v_ref[...])

    out = jax.ShapeDtypeStruct((r, c_dim), F32)
    return pl.pallas_call(body, name=name, out_shape=[out] * 3)(g, w, m, v)


def _pack_rows(arrays, dtype, row_unit):
    chunks, offs, r0 = [], [], 0
    for a in arrays:
        flat = a.reshape(-1).astype(dtype)
        rows = -(-flat.shape[0] // (LANES * row_unit)) * row_unit
        flat = jnp.pad(flat, (0, rows * LANES - flat.shape[0]))
        chunks.append(flat.reshape(rows, LANES))
        offs.append((r0, rows))
        r0 += rows
    return jnp.concatenate(chunks, axis=0), offs


def _unpack_rows(packed, offs, shapes):
    out = []
    for (r0, rows), shape in zip(offs, shapes):
        n = math.prod(shape)
        blk = packed[..., r0:r0 + rows, :]
        blk = blk.reshape(packed.shape[:-2] + (rows * LANES,))[..., :n]
        out.append(blk.reshape(packed.shape[:-2] + tuple(shape)))
    return out


def _full_from_slots(blk, col_sharded):
    _, r, c = blk.shape
    if col_sharded:
        return blk.transpose(1, 0, 2).reshape(r, N_DEV * c)
    return blk.reshape(N_DEV * r, c)


def _slots_from_full(full, col_sharded):
    r, c = full.shape
    if col_sharded:
        return full.reshape(r, N_DEV, c // N_DEV).transpose(1, 0, 2)
    return full.reshape(N_DEV, r // N_DEV, c)


def _ffn_fwd(h, g, w_gu_t, w_d, tag):
    n = _rms_fwd(h, g, f"{tag}_norm")
    gate, up, a = _ffn_up(n, w_gu_t, f"{tag}_up")
    h_out = _mm(a, w_d, F32, f"{tag}_down", res=h, alpha=FFN_RES_WEIGHT)
    return h_out, (h, n, gate, up, a)


def _ffn_bwd(dh_out, saved, g, w_gu_t, w_d, tag, dep, send_grads):
    h, n, gate, up, a = saved
    dw_d = _mm_tn(a, dh_out, f"{tag}_dw_down", alpha=FFN_RES_WEIGHT, dep=dep)
    dgate, dup = _ffn_da(dh_out, w_d, gate, up, f"{tag}_da", FFN_RES_WEIGHT)
    f = dgate.shape[1]
    dw_gu_t = _mm_tn(dgate, n, f"{tag}_dw_gate", out_rows=2 * f)
    dw_gu_t = _mm_tn(dup, n, f"{tag}_dw_up", out_rows=2 * f, row_off=f, into=dw_gu_t)
    dep = send_grads(dw_gu_t, dw_d)
    dn = _mm([dgate, dup], w_gu_t, F32, f"{tag}_dn", dep=dep, tm=512)
    dh, dg = _rms_bwd(dn, h, g, f"{tag}_dnorm", dres=dh_out)
    return dh, dg


W_GROUPS = (("ffn1_w_gate_up", "ffn1_w_down"),
            ("w_in",),
            ("w_out_a", "w_out_ssm", "w_mix_out"),
            ("w_q", "w_kv", "w_o_x", "ffn2_w_gate_up", "ffn2_w_down"))
G_GROUPS = (("ffn2_w_gate_up", "ffn2_w_down"),
            ("w_o_x", "w_q", "w_kv", "w_mix_out", "w_out_a", "w_out_ssm", "w_in"),
            ("ffn1_w_gate_up", "ffn1_w_down"))


def _local_step(x3, mem3, target3, small, comm):
    nb, s, d = x3.shape
    m_len = mem3.shape[1]
    t = nb * s
    nc = s // SSM_CHUNK
    di = small["ssm_norm"].shape[1]
    hs = di // SSM_HEAD_DIM
    cc = di + 2 * SSM_GROUPS * SSM_STATE
    x, mem, target = x3.reshape(t, d), mem3.reshape(nb * m_len, d), target3.reshape(t, d)

    sizes = (d, d, d, di, cc, hs, d, d)
    offs = [0]
    for sz in sizes:
        offs.append(offs[-1] + sz)
    z_col0, xbc_col0 = 3 * d, 3 * d + di
    ga_blk, gb_blk = (3 * d + di + cc) // d, (4 * d + di + cc) // d

    pad_vec = lambda v: jnp.pad(v.reshape(1, -1), ((0, 0), (0, LANES - hs)))
    prow = jnp.concatenate([pad_vec(small["ssm_dt_bias"]), pad_vec(small["ssm_a_log"]), pad_vec(small["ssm_d"]),
                            jnp.zeros((5, LANES), F32)], axis=0)
    pcol = prow.T
    e_mat = (lax.broadcasted_iota(jnp.int32, (LANES, di), 0)
             == lax.broadcasted_iota(jnp.int32, (LANES, di), 1) // SSM_HEAD_DIM).astype(BF16)
    conv_a_w8 = jnp.pad(small["conv_a_w"][0], ((0, 8 - CONV_A_K), (0, 0)))
    ssm_conv_w8 = jnp.pad(small["ssm_conv_w"][0], ((0, 8 - SSM_CONV_K), (0, 0)))

    wts, dep = comm.weights(0, None)
    h1, ffn1_saved = _ffn_fwd(x, small["ffn1_norm"] + dep[0, 0], wts["ffn1_w_gate_up"], wts["ffn1_w_down"], "ffn1")
    got, dep = comm.weights(1, h1)
    wts.update(got)
    w_in_t = wts["w_in"]
    w_main_t = jnp.concatenate([w_in_t[offs[i]:offs[i + 1]] for i in (0, 1, 2, 3, 4, 6, 7)], axis=0)
    w_dt_t = jnp.pad(w_in_t[offs[5]:offs[6]], ((0, LANES - hs), (0, 0)))
    u = _rms_fwd(h1, small["mix_norm"] + dep[0, 0], "mix_norm")
    proj = _mm(u, w_main_t, BF16, "in_proj", nt=True)
    dtr = _mm(u, w_dt_t, F32, "in_proj_dt", nt=True)
    yap = _conv_a_fwd(proj, conv_a_w8, nb, s, d, "conv_a")
    got, dep = comm.weights(2, yap)
    wts.update(got)
    y_a = _mm(yap, wts["w_out_a"], BF16, "out_a", dep=dep)
    xc, conv_pre = _conv_s_fwd(proj, xbc_col0, ssm_conv_w8, small["ssm_conv_b"] + dep[0, 0], nb, s, cc, "conv_s")
    dtrt = dtr.T
    y_ssd, sprev = _ssd_fwd(xc, dtr, dtrt, prow, pcol, nb, nc, di, "ssd")
    ygn = _gate_norm_fwd(y_ssd, proj, z_col0, small["ssm_norm"], di, "gate_norm")
    y_b = _mm(ygn, wts["w_out_ssm"], BF16, "out_ssm")
    merged = _merge_fwd(y_a, y_b, proj, ga_blk, gb_blk, d, "merge")
    h2 = _mm(merged, wts["w_mix_out"], F32, "mix_out", res=h1)
    got, _ = comm.weights(3, h2)
    wts.update(got)
    un = _rms_fwd(h2, small["xattn_norm"], "xattn_norm")
    mn = _rms_fwd(mem, small["mem_norm"], "mem_norm")
    q = _mm(un, wts["w_q"], BF16, "q_proj")
    kv = _mm(mn, wts["w_kv"], BF16, "kv_proj", nt=True)
    o = _xattn_fwd(q, kv, nb, s, m_len, d, "xattn")
    h3 = _mm(o, wts["w_o_x"], F32, "o_proj", res=h2)
    h4, ffn2_saved = _ffn_fwd(h3, small["ffn2_norm"], wts["ffn2_w_gate_up"], wts["ffn2_w_down"], "ffn2")
    loss_vec, dh4, dg_final = _loss_head(h4, small["final_norm"].reshape(1, d), target, "loss_head")

    grads = {"final_norm": dg_final.reshape(d)}
    big = {}
    dh3, grads["ffn2_norm"] = _ffn_bwd(
        dh4, ffn2_saved, small["ffn2_norm"], wts["ffn2_w_gate_up"], wts["ffn2_w_down"], "ffn2", None,
        lambda dw_gu_t, dw_d: comm.grads(0, {"ffn2_w_gate_up": dw_gu_t, "ffn2_w_down": dw_d}))
    big["w_o_x"] = _mm_tn(o, dh3, "dw_o")
    do = _mm(dh3, wts["w_o_x"], BF16, "d_o", nt=True)
    dq, dk, dv = _xattn_bwd(do, q, kv, nb, s, m_len, d, "xattn_bwd")
    big["w_q"] = _mm_tn(un, dq, "dw_q")
    big["w_kv"] = _mm_tn(dv, mn, "dw_v", out_rows=2 * d, row_off=d, into=_mm_tn(dk, mn, "dw_k", out_rows=2 * d))
    dun = _mm(dq, wts["w_q"], F32, "d_un", nt=True)
    dmn = _mm([dk, dv], wts["w_kv"], F32, "d_mn")
    _, grads["mem_norm"] = _rms_bwd(dmn, mem, small["mem_norm"], "mem_dnorm")
    dh2, grads["xattn_norm"] = _rms_bwd(dun, h2, small["xattn_norm"], "xattn_dnorm", dres=dh3)
    big["w_mix_out"] = _mm_tn(merged, dh2, "dw_mix")
    dmerged = _mm(dh2, wts["w_mix_out"], BF16, "d_merged", nt=True)
    dya, dyb, dga, dgb = _merge_bwd(dmerged, y_a, y_b, proj, ga_blk, gb_blk, d, "merge_bwd")
    big["w_out_a"] = _mm_tn(yap, dya, "dw_out_a")
    big["w_out_ssm"] = _mm_tn(ygn, dyb, "dw_out_ssm")
    dyap = _mm(dya, wts["w_out_a"], BF16, "d_yap", nt=True)
    dygn = _mm(dyb, wts["w_out_ssm"], BF16, "d_ygn", nt=True)
    dab, dac, dav, dconv_a = _conv_a_bwd(dyap, proj, conv_a_w8, nb, s, d, "conv_a_bwd")
    dy_ssd, dz, grads["ssm_norm"] = _gate_norm_bwd(dygn, y_ssd, proj, z_col0, small["ssm_norm"], di, "gate_norm_bwd")
    dxc, ddtr, ssd_sums = _ssd_bwd(dy_ssd, xc, dtr, dtrt, prow, pcol, e_mat, sprev, nb, nc, di, "ssd_bwd")
    dxbc, dconv_s, grads["ssm_conv_b"] = _conv_s_bwd(dxc, conv_pre, proj, xbc_col0, ssm_conv_w8, nb, s, cc, "conv_s_bwd")
    dpieces = [("ab", dab), ("ac", dac), ("av", dav), ("z", dz), ("xbc", dxbc), ("ga", dga), ("gb", dgb)]
    dw = {tag: _mm_tn(piece, u, f"dw_in_{tag}") for tag, piece in dpieces}
    dw_dt = _mm_tn(ddtr, u, "dw_in_dt")[:hs]
    du_main = _mm([piece for _, piece in dpieces], w_main_t, F32, "d_u", tk=1024)
    du = _mm(ddtr, w_dt_t, F32, "d_u_dt", res=du_main)
    dh1, grads["mix_norm"] = _rms_bwd(du, h1, small["mix_norm"], "mix_dnorm", dres=dh2)
    big["w_in"] = jnp.concatenate([dw["ab"], dw["ac"], dw["av"], dw["z"], dw["xbc"], dw_dt, dw["ga"], dw["gb"]], axis=0)
    dep = comm.grads(1, big)
    dx, grads["ffn1_norm"] = _ffn_bwd(
        dh1, ffn1_saved, small["ffn1_norm"], wts["ffn1_w_gate_up"], wts["ffn1_w_down"], "ffn1", dep,
        lambda dw_gu_t, dw_d: comm.grads(2, {"ffn1_w_gate_up": dw_gu_t, "ffn1_w_down": dw_d}))

    grads["conv_a_w"] = dconv_a[:CONV_A_K]
    grads["ssm_conv_w"] = dconv_s[:SSM_CONV_K]
    grads["ssm_dt_bias"] = ssd_sums[0:1, :hs]
    grads["ssm_a_log"] = ssd_sums[1:2, :hs]
    grads["ssm_d"] = ssd_sums[2:3, :hs]
    return loss_vec[0, 0], dx.reshape(nb, s, d), grads


def _step(inputs):
    w = {k: inputs[k] for k in WEIGHT_ORDER}
    mom = {k: inputs["m_" + k] for k in WEIGHT_ORDER}
    vel = {k: inputs["v_" + k] for k in WEIGHT_ORDER}
    me = 4 * lax.axis_index("x") + 2 * lax.axis_index("y") + lax.axis_index("c")

    send = {k: (w[k][0].T if k in COL_SHARDED else w[k][0]).astype(BF16) for k in BIG_WEIGHTS}

    def own_slot(land, mine):
        return lax.dynamic_update_slice(land, mine[None], (me, 0, 0))

    gathers, exchanges = {}, {}

    def weights(i, after):
        if i == 0:
            lands = _all_gather([send[k] for k in W_GROUPS[0]], "gather0")
        else:
            sent, lands = _push_wait(gathers[i], False, after, f"gather{i}_wait")
            lands = [own_slot(land, mine) for land, mine in zip(lands, sent)]
        full = {k: land.reshape(N_DEV * land.shape[1], land.shape[2]) for k, land in zip(W_GROUPS[i], lands)}
        dep = jnp.zeros((8, LANES), F32)
        if i + 1 < len(W_GROUPS):
            gathers[i + 1] = _push_start([send[k] for k in W_GROUPS[i + 1]], False, lands[0], f"gather{i + 1}_start")
            dep = gathers[i + 1]["token"]
        return full, dep

    def send_grads(i, by_name):
        slots = [by_name[k].reshape((N_DEV,) + send[k].shape) for k in G_GROUPS[i]]
        exchanges[i] = _push_start(slots, True, slots[0], f"exchange{i}_start")
        return exchanges[i]["token"]

    comm = types.SimpleNamespace(weights=weights, grads=send_grads)
    small = {k: w[k] for k in SMALL_REPLICATED}
    conv_shapes = [w[k].shape[1:] for k in SMALL_SHARDED]
    packed_c, conv_offs = _pack_rows([w[k][0] for k in SMALL_SHARDED], F32, 8)
    conv_blocks = _unpack_rows(_all_gather([packed_c], "gather_conv_weights")[0], conv_offs, conv_shapes)
    for k, b in zip(SMALL_SHARDED, conv_blocks):
        small[k] = _full_from_slots(b, True)[None]

    loss_local, grad_x, grads = _local_step(inputs["x"], inputs["mem"], inputs["loss_target"], small, comm)
    loss = lax.psum(loss_local, AXES)

    out = {}
    for i, names in enumerate(G_GROUPS):
        sent, lands = _push_wait(exchanges[i], True, grad_x, f"exchange{i}_wait")
        for k, land, slots in zip(names, lands, sent):
            parts = own_slot(land, lax.dynamic_index_in_dim(slots, me, 0, keepdims=False))
            if k in COL_SHARDED and w[k].shape[2] % LANES:
                flip = lambda a: a.transpose(0, 2, 1)
                out[k] = tuple(flip(o) for o in _sum8_adamw(parts, flip(w[k]), flip(mom[k]), flip(vel[k]), f"sum_adamw_{k}"))
                continue
            if k in COL_SHARDED:
                parts = parts.transpose(0, 2, 1)
            out[k] = tuple(_sum8_adamw(parts, w[k], mom[k], vel[k], f"sum_adamw_{k}"))

    small_names = SMALL_REPLICATED + SMALL_SHARDED
    packed_g, small_offs = _pack_rows([grads[k] for k in small_names], F32, 8)
    total = _sum8(_all_gather([packed_g], "gather_small_grads")[0], "sum_small_grads")
    full_grads = _unpack_rows(total, small_offs, [grads[k].shape for k in small_names])
    mine = {}
    for k, g in zip(small_names, full_grads):
        if k in SMALL_SHARDED:
            c_loc = w[k].shape[2]
            g = lax.dynamic_slice_in_dim(g, me * c_loc, c_loc, axis=1)
        mine[k] = g.reshape(w[k].shape)
    sg, s_offs = _pack_rows([mine[k] for k in small_names], F32, 8)
    sw, _ = _pack_rows([w[k] for k in small_names], F32, 8)
    sm, _ = _pack_rows([mom[k] for k in small_names], F32, 8)
    sv, _ = _pack_rows([vel[k] for k in small_names], F32, 8)
    s_shapes = [w[k].shape for k in small_names]
    small_out = [_unpack_rows(a, s_offs, s_shapes) for a in _adamw(sg, sw, sm, sv, "adamw_small")]
    for i, k in enumerate(small_names):
        out[k] = (mine[k],) + tuple(o[i] for o in small_out)

    res = [loss, grad_x]
    for j in range(4):
        res += [out[k][j] for k in WEIGHT_ORDER]
    return tuple(res)


def kernel(x, mem, ffn1_norm, ffn1_w_gate_up, ffn1_w_down, mix_norm, w_in, conv_a_w, w_out_a, ssm_conv_w, ssm_conv_b, ssm_dt_bias, ssm_a_log, ssm_d, ssm_norm, w_out_ssm, w_mix_out, xattn_norm, mem_norm, w_q, w_kv, w_o_x, ffn2_norm, ffn2_w_gate_up, ffn2_w_down, final_norm, loss_target, m_ffn1_norm, m_ffn1_w_gate_up, m_ffn1_w_down, m_mix_norm, m_w_in, m_conv_a_w, m_w_out_a, m_ssm_conv_w, m_ssm_conv_b, m_ssm_dt_bias, m_ssm_a_log, m_ssm_d, m_ssm_norm, m_w_out_ssm, m_w_mix_out, m_xattn_norm, m_mem_norm, m_w_q, m_w_kv, m_w_o_x, m_ffn2_norm, m_ffn2_w_gate_up, m_ffn2_w_down, m_final_norm, v_ffn1_norm, v_ffn1_w_gate_up, v_ffn1_w_down, v_mix_norm, v_w_in, v_conv_a_w, v_w_out_a, v_ssm_conv_w, v_ssm_conv_b, v_ssm_dt_bias, v_ssm_a_log, v_ssm_d, v_ssm_norm, v_w_out_ssm, v_w_mix_out, v_xattn_norm, v_mem_norm, v_w_q, v_w_kv, v_w_o_x, v_ffn2_norm, v_ffn2_w_gate_up, v_ffn2_w_down, v_final_norm):
    return _step(dict(locals()))
```

```python
import functools
import math
import types

import jax
import jax.numpy as jnp
from jax import lax
from jax.experimental import pallas as pl
from jax.experimental.pallas import tpu as pltpu

F32, BF16 = jnp.float32, jnp.bfloat16
HI = lax.Precision.HIGHEST
MESH = pl.DeviceIdType.MESH
AXES = ("x", "y", "c")
N_DEV = 8

EPS = 1e-6
FFN_RES_WEIGHT = 0.5
SSM_HEAD_DIM = 64
SSM_GROUPS = 4
SSM_STATE = 128
SSM_CHUNK = 128
CONV_A_K = 3
SSM_CONV_K = 4
XATTN_HEADS = 4
ADAM_LR, ADAM_B1, ADAM_B2, ADAM_EPS, ADAM_WD, ADAM_STEP = 1e-3, 0.9, 0.999, 1e-8, 0.01, 10

LANES = 128
BF16_SUBLANES = 16
VMEM_LIMIT_BYTES = 56 * 2 ** 20
NEG_BIG = -1e30

BIG_WEIGHTS = ("ffn1_w_gate_up", "ffn1_w_down", "w_in", "w_out_a", "w_out_ssm", "w_mix_out",
               "w_q", "w_kv", "w_o_x", "ffn2_w_gate_up", "ffn2_w_down")
COL_SHARDED = ("ffn1_w_gate_up", "w_in", "w_kv", "ffn2_w_gate_up")
SMALL_REPLICATED = ("ffn1_norm", "mix_norm", "ssm_conv_b", "ssm_dt_bias", "ssm_a_log", "ssm_d", "ssm_norm",
                    "xattn_norm", "mem_norm", "ffn2_norm", "final_norm")
SMALL_SHARDED = ("conv_a_w", "ssm_conv_w")
WEIGHT_ORDER = ("ffn1_norm", "ffn1_w_gate_up", "ffn1_w_down", "mix_norm", "w_in", "conv_a_w", "w_out_a",
                "ssm_conv_w", "ssm_conv_b", "ssm_dt_bias", "ssm_a_log", "ssm_d", "ssm_norm", "w_out_ssm",
                "w_mix_out", "xattn_norm", "mem_norm", "w_q", "w_kv", "w_o_x", "ffn2_norm", "ffn2_w_gate_up",
                "ffn2_w_down", "final_norm")


def _tile(dim, pref, unit):
    best = None
    t = unit
    while t <= min(dim, pref):
        if dim % t == 0:
            best = t
        t += unit
    return best if best is not None else dim


def _params(*sem):
    return pltpu.CompilerParams(dimension_semantics=sem, vmem_limit_bytes=VMEM_LIMIT_BYTES)


def _sigmoid(x):
    return pl.reciprocal(1.0 + jnp.exp(-x), approx=True)


def _silu(x):
    return x * _sigmoid(x)


def _dsilu(x):
    s = _sigmoid(x)
    return s * (1.0 + x * (1.0 - s))


def _softplus(x):
    return jnp.maximum(x, 0.0) + jnp.log(1.0 + jnp.exp(-jnp.abs(x)))


def _dot(a, b, dims=(((1,), (0,)), ((), ())), precision=None):
    return lax.dot_general(a, b, dims, preferred_element_type=F32, precision=precision)


def _stack_rows(rows, width):
    r_idx = lax.broadcasted_iota(jnp.int32, (8, width), 0)
    acc = jnp.zeros((8, width), F32)
    for k, row in enumerate(rows):
        acc = jnp.where(r_idx == k, row, acc)
    return acc


NT = (((1,), (1,)), ((), ()))
TN = (((0,), (0,)), ((), ()))


def _mm(a, b, out_dtype, name, res=None, alpha=1.0, nt=False, dep=None, norm_out=None, norm_bwd=None,
        tm=1024, tn=2048, tk=2816):
    pieces = list(a) if isinstance(a, (list, tuple)) else [a]
    m = pieces[0].shape[0]
    k = sum(p.shape[1] for p in pieces)
    n = b.shape[0] if nt else b.shape[1]
    assert (b.shape[1] if nt else b.shape[0]) == k
    tm, tn = _tile(m, tm, 8), _tile(n, tn, LANES)
    tk = _tile(math.gcd(*[p.shape[1] for p in pieces]), tk, LANES)
    nk = k // tk
    starts, s0 = [], 0
    for p in pieces:
        starts.append((s0, p.shape[1] // tk))
        s0 += p.shape[1] // tk
    n_p = len(pieces)
    assert norm_out is None or norm_bwd is None
    whole_rows = norm_out is not None or norm_bwd is not None
    assert not whole_rows or tn == n

    def body(*refs):
        a_refs, b_ref = refs[:n_p], refs[n_p]
        nxt = n_p + 1
        r_ref = refs[nxt] if res is not None else None
        nxt += (res is not None) + (dep is not None)
        g_ref = refs[nxt] if whole_rows else None
        x_ref = refs[nxt + 1] if norm_bwd is not None else None
        nxt += whole_rows + (norm_bwd is not None)
        o_ref = refs[nxt]
        o2_ref = refs[nxt + 1] if whole_rows else None
        scr = refs[nxt + 1 + whole_rows:]
        first_rows = pl.program_id(0) == 0

        def finish(acc):
            acc = alpha * acc if alpha != 1.0 else acc
            if norm_bwd is not None:
                xv = x_ref[...]
                r = lax.rsqrt(jnp.mean(xv * xv, axis=-1, keepdims=True) + EPS)
                xh = xv * r
                gy = acc * g_ref[...]
                part = jnp.sum(acc * xh, axis=0, keepdims=True)
                acc = r * (gy - xh * jnp.mean(gy * xh, axis=-1, keepdims=True))

                @pl.when(first_rows)
                def _():
                    o2_ref[...] = part

                @pl.when(jnp.logical_not(first_rows))
                def _():
                    o2_ref[...] += part
            if r_ref is not None:
                acc = r_ref[...] + acc
            o_ref[...] = acc.astype(out_dtype)
            if norm_out is not None:
                r = lax.rsqrt(jnp.mean(acc * acc, axis=-1, keepdims=True) + EPS)
                o2_ref[...] = (acc * r * g_ref[...]).astype(BF16)

        def product(a_ref):
            return _dot(a_ref[...].astype(BF16), b_ref[...].astype(BF16), NT if nt else (((1,), (0,)), ((), ())))

        if nk == 1:
            finish(product(a_refs[0]))
            return
        acc_ref = scr[0]
        kk = pl.program_id(2)
        for (s, cnt), a_ref in zip(starts, a_refs):
            if s == 0:
                @pl.when(kk == 0)
                def _():
                    acc_ref[...] = product(a_ref)

                @pl.when(jnp.logical_and(kk > 0, kk < cnt))
                def _():
                    acc_ref[...] += product(a_ref)
            else:
                @pl.when(jnp.logical_and(kk >= s, kk < s + cnt))
                def _():
                    acc_ref[...] += product(a_ref)

        @pl.when(kk == nk - 1)
        def _():
            finish(acc_ref[...])

    def a_spec(s, cnt):
        return pl.BlockSpec((tm, tk), lambda i, j, kk: (i, jnp.clip(kk - s, 0, cnt - 1)))

    in_specs = [a_spec(s, cnt) for s, cnt in starts]
    in_specs.append(pl.BlockSpec((tn, tk), lambda i, j, kk: (j, kk)) if nt else pl.BlockSpec((tk, tn), lambda i, j, kk: (kk, j)))
    args = pieces + [b]
    if res is not None:
        in_specs.append(pl.BlockSpec((tm, tn), lambda i, j, kk: (i, j)))
        args.append(res)
    if dep is not None:
        in_specs.append(pl.BlockSpec((8, LANES), lambda i, j, kk: (0, 0)))
        args.append(dep)
    tile = pl.BlockSpec((tm, tn), lambda i, j, kk: (i, j))
    vec = pl.BlockSpec((1, n), lambda i, j, kk: (0, 0))
    out_specs, out_shape = tile, jax.ShapeDtypeStruct((m, n), out_dtype)
    if norm_out is not None:
        in_specs.append(vec)
        args.append(norm_out)
        out_specs, out_shape = [tile, tile], [out_shape, jax.ShapeDtypeStruct((m, n), BF16)]
    if norm_bwd is not None:
        in_specs += [vec, tile]
        args += [norm_bwd[1], norm_bwd[0]]
        out_specs, out_shape = [tile, vec], [out_shape, jax.ShapeDtypeStruct((1, n), F32)]
    return pl.pallas_call(
        body, name=name, grid=(m // tm, n // tn, nk), in_specs=in_specs, out_specs=out_specs, out_shape=out_shape,
        scratch_shapes=[pltpu.VMEM((tm, tn), F32)] if nk > 1 else [],
        compiler_params=_params("arbitrary" if norm_bwd is not None else "parallel", "parallel", "arbitrary"),
    )(*args)


def _mm_tn(x, dy, name, out_dtype=BF16, alpha=1.0, dep=None, out_rows=None, row_off=0, into=None,
           tko=1408, tn=1024, tt=2048):
    t, k = x.shape
    n = dy.shape[1]
    tko, tn, tt = _tile(k, tko, LANES), _tile(n, tn, LANES), _tile(t, tt, 8)
    nt_steps = t // tt

    def body(*refs):
        x_ref, dy_ref = refs[:2]
        o_ref, acc_ref = refs[-2:]
        part = _dot(x_ref[...].astype(BF16), dy_ref[...].astype(BF16), TN)
        step = pl.program_id(2)

        @pl.when(step == 0)
        def _():
            acc_ref[...] = part

        @pl.when(step > 0)
        def _():
            acc_ref[...] += part

        @pl.when(step == nt_steps - 1)
        def _():
            acc = acc_ref[...]
            o_ref[...] = (alpha * acc if alpha != 1.0 else acc).astype(out_dtype)

    in_specs = [pl.BlockSpec((tt, tko), lambda i, j, s: (s, i)), pl.BlockSpec((tt, tn), lambda i, j, s: (s, j))]
    args = [x, dy]
    if dep is not None:
        in_specs.append(pl.BlockSpec((8, LANES), lambda i, j, s: (0, 0)))
        args.append(dep)
    aliases = {}
    if into is not None:
        in_specs.append(pl.BlockSpec(memory_space=pl.ANY))
        args.append(into)
        aliases = {len(args) - 1: 0}
    band = row_off // tko
    assert row_off % tko == 0
    return pl.pallas_call(
        body, name=name, grid=(k // tko, n // tn, nt_steps), in_specs=in_specs,
        out_specs=pl.BlockSpec((tko, tn), lambda i, j, s: (i + band, j)),
        out_shape=jax.ShapeDtypeStruct((out_rows or k, n), out_dtype),
        scratch_shapes=[pltpu.VMEM((tko, tn), F32)], input_output_aliases=aliases,
        compiler_params=_params("parallel", "parallel", "arbitrary"),
    )(*args)


def _rms_fwd(x, g, name, tt=512):
    t, d = x.shape
    tt = _tile(t, tt, 8)

    def body(x_ref, g_ref, o_ref):
        xv = x_ref[...]
        r = lax.rsqrt(jnp.mean(xv * xv, axis=-1, keepdims=True) + EPS)
        o_ref[...] = (xv * r * g_ref[...]).astype(BF16)

    return pl.pallas_call(
        body, name=name, grid=(t // tt,),
        in_specs=[pl.BlockSpec((tt, d), lambda i: (i, 0)), pl.BlockSpec((1, d), lambda i: (0, 0))],
        out_specs=pl.BlockSpec((tt, d), lambda i: (i, 0)),
        out_shape=jax.ShapeDtypeStruct((t, d), BF16), compiler_params=_params("parallel"),
    )(x, g)


def _rms_bwd(dn, x, g, name, dres=None, tt=512):
    t, d = x.shape
    tt = _tile(t, tt, 8)

    def body(*refs):
        if dres is None:
            dn_ref, x_ref, g_ref, dx_ref, dg_ref = refs
            r_ref = None
        else:
            dn_ref, x_ref, g_ref, r_ref, dx_ref, dg_ref = refs
        xv, dnv = x_ref[...], dn_ref[...].astype(F32)
        r = lax.rsqrt(jnp.mean(xv * xv, axis=-1, keepdims=True) + EPS)
        xh = xv * r
        gy = dnv * g_ref[...]
        dx = r * (gy - xh * jnp.mean(gy * xh, axis=-1, keepdims=True))
        if r_ref is not None:
            dx = dx + r_ref[...]
        dx_ref[...] = dx
        part = jnp.sum(dnv * xh, axis=0, keepdims=True)

        @pl.when(pl.program_id(0) == 0)
        def _():
            dg_ref[...] = part

        @pl.when(pl.program_id(0) > 0)
        def _():
            dg_ref[...] += part

    row = pl.BlockSpec((tt, d), lambda i: (i, 0))
    vec = pl.BlockSpec((1, d), lambda i: (0, 0))
    in_specs, args = [row, row, vec], [dn, x, g]
    if dres is not None:
        in_specs.append(row)
        args.append(dres)
    return pl.pallas_call(
        body, name=name, grid=(t // tt,), in_specs=in_specs, out_specs=[row, vec],
        out_shape=[jax.ShapeDtypeStruct((t, d), F32), jax.ShapeDtypeStruct((1, d), F32)],
        compiler_params=_params("arbitrary"),
    )(*args)


def _ffn_up(n, w_gu_t, name, tm=1024, tf=1408):
    t, d = n.shape
    f = w_gu_t.shape[0] // 2
    tm, tf = _tile(t, tm, 8), _tile(f, tf, LANES)
    nf = f // tf

    def body(n_ref, wg_ref, wu_ref, g_ref, u_ref, a_ref):
        nv = n_ref[...]
        gate, up = _dot(nv, wg_ref[...], NT), _dot(nv, wu_ref[...], NT)
        s = _sigmoid(gate)
        sg = gate * s
        g_ref[...] = (up * (s * (1.0 + gate * (1.0 - s)))).astype(BF16)
        u_ref[...] = sg.astype(BF16)
        a_ref[...] = (sg * up).astype(BF16)

    blk = pl.BlockSpec((tm, tf), lambda i, j: (i, j))
    out = jax.ShapeDtypeStruct((t, f), BF16)
    return pl.pallas_call(
        body, name=name, grid=(t // tm, nf),
        in_specs=[pl.BlockSpec((tm, d), lambda i, j: (i, 0)), pl.BlockSpec((tf, d), lambda i, j: (j, 0)),
                  pl.BlockSpec((tf, d), lambda i, j: (j + nf, 0))],
        out_specs=[blk, blk, blk], out_shape=[out, out, out], compiler_params=_params("parallel", "parallel"),
    )(n, w_gu_t, w_gu_t)


def _ffn_da(dh, w_d, gate, up, name, alpha, dep=None, tm=1024, tf=1408):
    t, d = dh.shape
    f = w_d.shape[0]
    tm, tf = _tile(t, tm, 8), _tile(f, tf, LANES)

    def body(*refs):
        dh_ref, w_ref, g_ref, u_ref = refs[:4]
        dg_ref, du_ref = refs[-2:]
        da = alpha * _dot(dh_ref[...].astype(BF16), w_ref[...], NT)
        dg_ref[...] = (da * g_ref[...].astype(F32)).astype(BF16)
        du_ref[...] = (da * u_ref[...].astype(F32)).astype(BF16)

    blk = pl.BlockSpec((tm, tf), lambda i, j: (i, j))
    in_specs = [pl.BlockSpec((tm, d), lambda i, j: (i, 0)), pl.BlockSpec((tf, d), lambda i, j: (j, 0)), blk, blk]
    args = [dh, w_d, gate, up]
    if dep is not None:
        in_specs.append(pl.BlockSpec((8, LANES), lambda i, j: (0, 0)))
        args.append(dep)
    out = jax.ShapeDtypeStruct((t, f), BF16)
    return pl.pallas_call(
        body, name=name, grid=(t // tm, f // tf), in_specs=in_specs, out_specs=[blk, blk], out_shape=[out, out],
        compiler_params=_params("parallel", "parallel"),
    )(*args)


def _merge_fwd(ya, yb, proj, ga_blk, gb_blk, d, name, tt=512):
    t = ya.shape[0]
    tt = _tile(t, tt, 8)

    def body(ya_ref, yb_ref, ga_ref, gb_ref, o_ref):
        o_ref[...] = (_sigmoid(ga_ref[...].astype(F32)) * ya_ref[...].astype(F32)
                      + _sigmoid(gb_ref[...].astype(F32)) * yb_ref[...].astype(F32)).astype(BF16)

    row = pl.BlockSpec((tt, d), lambda i: (i, 0))
    return pl.pallas_call(
        body, name=name, grid=(t // tt,),
        in_specs=[row, row, pl.BlockSpec((tt, d), lambda i: (i, ga_blk)), pl.BlockSpec((tt, d), lambda i: (i, gb_blk))],
        out_specs=row, out_shape=jax.ShapeDtypeStruct((t, d), BF16), compiler_params=_params("parallel"),
    )(ya, yb, proj, proj)


def _merge_bwd(dm, ya, yb, proj, ga_blk, gb_blk, d, name, tt=512):
    t = ya.shape[0]
    tt = _tile(t, tt, 8)

    def body(dm_ref, ya_ref, yb_ref, ga_ref, gb_ref, dya_ref, dyb_ref, dga_ref, dgb_ref):
        dmv = dm_ref[...].astype(F32)
        sa, sb = _sigmoid(ga_ref[...].astype(F32)), _sigmoid(gb_ref[...].astype(F32))
        dya_ref[...] = (dmv * sa).astype(BF16)
        dyb_ref[...] = (dmv * sb).astype(BF16)
        dga_ref[...] = (dmv * ya_ref[...].astype(F32) * sa * (1.0 - sa)).astype(BF16)
        dgb_ref[...] = (dmv * yb_ref[...].astype(F32) * sb * (1.0 - sb)).astype(BF16)

    row = pl.BlockSpec((tt, d), lambda i: (i, 0))
    out = jax.ShapeDtypeStruct((t, d), BF16)
    return pl.pallas_call(
        body, name=name, grid=(t // tt,),
        in_specs=[row, row, row, pl.BlockSpec((tt, d), lambda i: (i, ga_blk)), pl.BlockSpec((tt, d), lambda i: (i, gb_blk))],
        out_specs=[row] * 4, out_shape=[out] * 4, compiler_params=_params("parallel"),
    )(dm, ya, yb, proj, proj)


def _loss_head(h, g, target, name, tt=512):
    t, d = h.shape
    tt = _tile(t, tt, 8)

    def body(h_ref, g_ref, tg_ref, loss_ref, dh_ref, dg_ref):
        xv = h_ref[...]
        r = lax.rsqrt(jnp.mean(xv * xv, axis=-1, keepdims=True) + EPS)
        xh = xv * r
        err = xh * g_ref[...] - tg_ref[...]
        dout = err * (1.0 / d)
        gy = dout * g_ref[...]
        dh_ref[...] = r * (gy - xh * jnp.mean(gy * xh, axis=-1, keepdims=True))
        dg_part = jnp.sum(dout * xh, axis=0, keepdims=True)
        loss_part = jnp.full((1, LANES), 0.5 / d, F32) * jnp.sum(err * err)

        @pl.when(pl.program_id(0) == 0)
        def _():
            dg_ref[...] = dg_part
            loss_ref[...] = loss_part

        @pl.when(pl.program_id(0) > 0)
        def _():
            dg_ref[...] += dg_part
            loss_ref[...] += loss_part

    row = pl.BlockSpec((tt, d), lambda i: (i, 0))
    vec = pl.BlockSpec((1, d), lambda i: (0, 0))
    return pl.pallas_call(
        body, name=name, grid=(t // tt,), in_specs=[row, vec, row],
        out_specs=[pl.BlockSpec((1, LANES), lambda i: (0, 0)), row, vec],
        out_shape=[jax.ShapeDtypeStruct((1, LANES), F32), jax.ShapeDtypeStruct((t, d), F32), jax.ShapeDtypeStruct((1, d), F32)],
        compiler_params=_params("arbitrary"),
    )(h, g, target)


def _shift_down(x, k, t_idx):
    if k == 0:
        return x
    return jnp.where(t_idx >= k, pltpu.roll(x, k, 0), 0.0)


def _shift_up(x, k, t_idx, s):
    if k == 0:
        return x
    return jnp.where(t_idx < s - k, pltpu.roll(x, s - k, 0), 0.0)


def _conv_a_fwd(proj, w, nb, s, d, name, cb=256):
    cb = _tile(d, cb, LANES)
    nd = d // cb

    def body(b_ref, c_ref, v_ref, w_ref, o_ref):
        t_idx = lax.broadcasted_iota(jnp.int32, (s, cb), 0)
        cv = c_ref[...].astype(F32) * v_ref[...].astype(F32)
        cc = sum(w_ref[k:k + 1, :] * _shift_down(cv, CONV_A_K - 1 - k, t_idx) for k in range(CONV_A_K))
        o_ref[...] = (b_ref[...].astype(F32) * cc).astype(BF16)

    def col(off):
        return pl.BlockSpec((s, cb), lambda b, j: (b, j + off * nd))

    return pl.pallas_call(
        body, name=name, grid=(nb, nd), in_specs=[col(0), col(1), col(2), pl.BlockSpec((8, cb), lambda b, j: (0, j))],
        out_specs=pl.BlockSpec((s, cb), lambda b, j: (b, j)),
        out_shape=jax.ShapeDtypeStruct((nb * s, d), BF16), compiler_params=_params("parallel", "parallel"),
    )(proj, proj, proj, w)


def _conv_a_bwd(dy, proj, w, nb, s, d, name, cb=256):
    cb = _tile(d, cb, LANES)
    nd = d // cb

    def body(dy_ref, b_ref, c_ref, v_ref, w_ref, db_ref, dc_ref, dv_ref, dw_ref):
        t_idx = lax.broadcasted_iota(jnp.int32, (s, cb), 0)
        cv_c, cv_v = c_ref[...].astype(F32), v_ref[...].astype(F32)
        cv = cv_c * cv_v
        shifted = [_shift_down(cv, CONV_A_K - 1 - k, t_idx) for k in range(CONV_A_K)]
        cc = sum(w_ref[k:k + 1, :] * shifted[k] for k in range(CONV_A_K))
        dyv = dy_ref[...].astype(F32)
        db_ref[...] = (dyv * cc).astype(BF16)
        dcc = dyv * b_ref[...].astype(F32)
        dcv = sum(w_ref[k:k + 1, :] * _shift_up(dcc, CONV_A_K - 1 - k, t_idx, s) for k in range(CONV_A_K))
        dc_ref[...] = (dcv * cv_v).astype(BF16)
        dv_ref[...] = (dcv * cv_c).astype(BF16)
        rows = [jnp.sum(dcc * shifted[k], axis=0, keepdims=True) for k in range(CONV_A_K)]
        part = _stack_rows(rows, cb)

        @pl.when(pl.program_id(1) == 0)
        def _():
            dw_ref[...] = part

        @pl.when(pl.program_id(1) > 0)
        def _():
            dw_ref[...] += part

    def col(off):
        return pl.BlockSpec((s, cb), lambda j, b: (b, j + off * nd))

    own = pl.BlockSpec((s, cb), lambda j, b: (b, j))
    wspec = pl.BlockSpec((8, cb), lambda j, b: (0, j))
    out = jax.ShapeDtypeStruct((nb * s, d), BF16)
    return pl.pallas_call(
        body, name=name, grid=(nd, nb), in_specs=[own, col(0), col(1), col(2), wspec],
        out_specs=[own, own, own, wspec], out_shape=[out, out, out, jax.ShapeDtypeStruct((8, d), F32)],
        compiler_params=_params("parallel", "arbitrary"),
    )(dy, proj, proj, proj, w)


def _conv_s_fwd(proj, col0, w, bias, nb, s, cc_width, name, cb=256):
    cb = _tile(math.gcd(cc_width, col0) if col0 else cc_width, cb, LANES)
    nd, off = cc_width // cb, col0 // cb

    def body(x_ref, w_ref, b_ref, o_ref, pre_ref):
        t_idx = lax.broadcasted_iota(jnp.int32, (s, cb), 0)
        xv = x_ref[...].astype(F32)
        pre = b_ref[...] + sum(w_ref[k:k + 1, :] * _shift_down(xv, SSM_CONV_K - 1 - k, t_idx) for k in range(SSM_CONV_K))
        o_ref[...] = _silu(pre).astype(BF16)
        pre_ref[...] = pre.astype(BF16)

    vec = pl.BlockSpec((8, cb), lambda b, j: (0, j))
    own = pl.BlockSpec((s, cb), lambda b, j: (b, j))
    out = jax.ShapeDtypeStruct((nb * s, cc_width), BF16)
    return pl.pallas_call(
        body, name=name, grid=(nb, nd),
        in_specs=[pl.BlockSpec((s, cb), lambda b, j: (b, j + off)), vec, pl.BlockSpec((1, cb), lambda b, j: (0, j))],
        out_specs=[own, own], out_shape=[out, out], compiler_params=_params("parallel", "parallel"),
    )(proj, w, bias)


def _conv_s_bwd(dxc, pre, proj, col0, w, nb, s, cc_width, name, cb=256):
    cb = _tile(math.gcd(cc_width, col0) if col0 else cc_width, cb, LANES)
    nd, off = cc_width // cb, col0 // cb

    def body(d_ref, pre_ref, x_ref, w_ref, dx_ref, dw_ref, db_ref):
        t_idx = lax.broadcasted_iota(jnp.int32, (s, cb), 0)
        xv = x_ref[...].astype(F32)
        dpre = d_ref[...].astype(F32) * _dsilu(pre_ref[...].astype(F32))
        ahead = [_shift_up(dpre, j, t_idx, s) for j in range(SSM_CONV_K)]
        dx_ref[...] = sum(w_ref[k:k + 1, :] * ahead[SSM_CONV_K - 1 - k] for k in range(SSM_CONV_K)).astype(BF16)
        rows = [jnp.sum(ahead[SSM_CONV_K - 1 - k] * xv, axis=0, keepdims=True) for k in range(SSM_CONV_K)]
        dw_part = _stack_rows(rows, cb)
        db_part = jnp.sum(dpre, axis=0, keepdims=True)

        @pl.when(pl.program_id(1) == 0)
        def _():
            dw_ref[...] = dw_part
            db_ref[...] = db_part

        @pl.when(pl.program_id(1) > 0)
        def _():
            dw_ref[...] += dw_part
            db_ref[...] += db_part

    own = pl.BlockSpec((s, cb), lambda j, b: (b, j))
    wspec = pl.BlockSpec((8, cb), lambda j, b: (0, j))
    bspec = pl.BlockSpec((1, cb), lambda j, b: (0, j))
    return pl.pallas_call(
        body, name=name, grid=(nd, nb),
        in_specs=[own, own, pl.BlockSpec((s, cb), lambda j, b: (b, j + off)), wspec],
        out_specs=[own, wspec, bspec],
        out_shape=[jax.ShapeDtypeStruct((nb * s, cc_width), BF16), jax.ShapeDtypeStruct((8, cc_width), F32),
                   jax.ShapeDtypeStruct((1, cc_width), F32)],
        compiler_params=_params("parallel", "arbitrary"),
    )(dxc, pre, proj, w)


def _split3(v):
    hi = v.astype(BF16)
    r1 = v - hi.astype(F32)
    mid = r1.astype(BF16)
    return hi, mid, (r1 - mid.astype(F32)).astype(BF16)


def _exact_left(mask_b, v):
    return sum(_dot(mask_b, t) for t in _split3(v))


def _exact_right(v, mask_b):
    return sum(_dot(t, mask_b) for t in _split3(v))


def _head_sums(v, e_b):
    return _dot(v.astype(BF16), e_b, NT)


def _spread(v, out_ref, di):
    lane = lax.broadcasted_iota(jnp.int32, (v.shape[0], LANES), 1)
    for pr in range(di // LANES):
        h0 = pr * (LANES // SSM_HEAD_DIM)
        out_ref[:, pr * LANES:(pr + 1) * LANES] = jnp.where(lane < SSM_HEAD_DIM, v[:, h0:h0 + 1], v[:, h0 + 1:h0 + 2])


def _ssd_common(xc_ref, dtr_ref, dtrt_ref, prow_ref, pcol_ref, dtx_ref, acsx_ref, dx_ref, di):
    l = SSM_CHUNK
    bias_r, a_r = prow_ref[0:1, :], -jnp.exp(prow_ref[1:2, :])
    sp_in = dtr_ref[...] + bias_r
    dt = _softplus(sp_in)
    li = lax.broadcasted_iota(jnp.int32, (l, l), 0)
    si = lax.broadcasted_iota(jnp.int32, (l, l), 1)
    lower_b = (li >= si).astype(BF16)
    upper_b = (li <= si).astype(BF16)
    acs = _exact_left(lower_b, dt * a_r)
    bias_c, a_c = pcol_ref[:, 0:1], -jnp.exp(pcol_ref[:, 1:2])
    dt_t = _softplus(dtrt_ref[...] + bias_c)
    acs_t = _exact_right(dt_t * a_c, upper_b)
    _spread(dt, dtx_ref, di)
    _spread(acs, acsx_ref, di)
    _spread(prow_ref[0:8, :], dx_ref, di)
    acs_exp = acsx_ref[...]
    acs_last = acs_exp[l - 1:l, :]
    x = xc_ref[:, 0:di].astype(F32)
    return dict(dt=dt, a_r=a_r, sp_in=sp_in, acs=acs, acs_t=acs_t, dt_exp=dtx_ref[...], e_exp=jnp.exp(acs_exp),
                el_exp=jnp.exp(acs_last), f_exp=jnp.exp(acs_last - acs_exp), x=x, mask=li >= si, upper_b=upper_b,
                d_exp=dx_ref[2:3, :])


def _decay(q, h):
    seg = q["acs"][:, h:h + 1] - q["acs_t"][h:h + 1, :]
    return jnp.exp(jnp.where(q["mask"], seg, NEG_BIG))


def _ssd_fwd(xc, dtr, dtrt, prow, pcol, nb, nc, di, name):
    l, n, g_n, p = SSM_CHUNK, SSM_STATE, SSM_GROUPS, SSM_HEAD_DIM
    cc = xc.shape[1]
    gw = di // g_n
    assert p * 2 == LANES and gw % LANES == 0

    def body(xc_ref, dtr_ref, dtrt_ref, prow_ref, pcol_ref, y_ref, sprev_ref, st_ref, dtx_ref, acsx_ref, dx_ref):
        @pl.when(pl.program_id(1) == 0)
        def _():
            st_ref[...] = jnp.zeros_like(st_ref)

        q = _ssd_common(xc_ref, dtr_ref, dtrt_ref, prow_ref, pcol_ref, dtx_ref, acsx_ref, dx_ref, di)
        x = q["x"]
        xd = x * q["dt_exp"]
        xdb = xd.astype(BF16)
        xdf = (xd * q["f_exp"]).astype(BF16)
        lane = lax.broadcasted_iota(jnp.int32, (l, LANES), 1)
        for g in range(g_n):
            lo = g * gw
            bg = xc_ref[:, di + g * n: di + (g + 1) * n]
            cg = xc_ref[:, di + g_n * n + g * n: di + g_n * n + (g + 1) * n]
            cb = _dot(cg, bg, NT)
            st_g = st_ref[:, lo:lo + gw]
            y_off = q["e_exp"][:, lo:lo + gw] * _dot(cg, st_g.astype(BF16))
            for pr in range(gw // LANES):
                c0 = lo + pr * LANES
                h0 = c0 // p
                xp = xdb[:, c0:c0 + LANES]
                m0 = (cb * _decay(q, h0)).astype(BF16)
                m1 = (cb * _decay(q, h0 + 1)).astype(BF16)
                yd = _dot(m0, jnp.where(lane < p, xp, 0)) + _dot(m1, jnp.where(lane >= p, xp, 0))
                y_ref[:, c0:c0 + LANES] = (yd + y_off[:, pr * LANES:(pr + 1) * LANES]
                                           + q["d_exp"][:, c0:c0 + LANES] * x[:, c0:c0 + LANES])
            sprev_ref[:, lo:lo + gw] = st_g
            st_ref[:, lo:lo + gw] = q["el_exp"][:, lo:lo + gw] * st_g + _dot(bg, xdf[:, lo:lo + gw], TN)

    tok = lambda w: pl.BlockSpec((l, w), lambda b, c: (b * nc + c, 0))
    const = lambda r, w: pl.BlockSpec((r, w), lambda b, c: (0, 0))
    return pl.pallas_call(
        body, name=name, grid=(nb, nc),
        in_specs=[tok(cc), tok(LANES), pl.BlockSpec((LANES, l), lambda b, c: (0, b * nc + c)),
                  const(8, LANES), const(LANES, 8)],
        out_specs=[tok(di), pl.BlockSpec((None, n, di), lambda b, c: (b * nc + c, 0, 0))],
        out_shape=[jax.ShapeDtypeStruct((nb * nc * l, di), F32), jax.ShapeDtypeStruct((nb * nc, n, di), F32)],
        scratch_shapes=[pltpu.VMEM((n, di), F32), pltpu.VMEM((l, di), F32), pltpu.VMEM((l, di), F32),
                        pltpu.VMEM((8, di), F32)],
        compiler_params=_params("parallel", "arbitrary"),
    )(xc, dtr, dtrt, prow, pcol)


def _ssd_bwd(dy, xc, dtr, dtrt, prow, pcol, e_mat, sprev, nb, nc, di, name):
    l, n, g_n, p = SSM_CHUNK, SSM_STATE, SSM_GROUPS, SSM_HEAD_DIM
    cc = xc.shape[1]
    gw = di // g_n

    def body(dy_ref, xc_ref, dtr_ref, dtrt_ref, prow_ref, pcol_ref, e_ref, sprev_ref,
             dxc_ref, ddtr_ref, sums_ref, dst_ref, off_ref, dxd_ref, last_ref, vst_ref,
             dtx_ref, acsx_ref, dx_ref):
        first = jnp.logical_and(pl.program_id(0) == 0, pl.program_id(1) == 0)

        @pl.when(pl.program_id(1) == 0)
        def _():
            dst_ref[...] = jnp.zeros_like(dst_ref)

        head_row = lax.broadcasted_iota(jnp.int32, (LANES, l), 0)
        row_sums, col_sums = jnp.zeros((l, LANES), F32), jnp.zeros((LANES, l), F32)
        strict_lower = lax.broadcasted_iota(jnp.int32, (l, l), 0) > lax.broadcasted_iota(jnp.int32, (l, l), 1)

        q = _ssd_common(xc_ref, dtr_ref, dtrt_ref, prow_ref, pcol_ref, dtx_ref, acsx_ref, dx_ref, di)
        x = q["x"]
        xd = x * q["dt_exp"]
        xdb = xd.astype(BF16)
        xdf = (xd * q["f_exp"]).astype(BF16)
        dyv = dy_ref[...]
        dyb = dyv.astype(BF16)
        dye = (dyv * q["e_exp"]).astype(BF16)
        upper_b = q["upper_b"]
        lane = lax.broadcasted_iota(jnp.int32, (l, LANES), 1)
        for g in range(g_n):
            lo = g * gw
            bg = xc_ref[:, di + g * n: di + (g + 1) * n]
            cg = xc_ref[:, di + g_n * n + g * n: di + g_n * n + (g + 1) * n]
            cb = _dot(cg, bg, NT)
            st_g = sprev_ref[:, lo:lo + gw]
            st_gb = st_g.astype(BF16)
            dst_g = dst_ref[:, lo:lo + gw]
            dst_gb = dst_g.astype(BF16)
            dye_g = dye[:, lo:lo + gw]
            xdf_g = xdf[:, lo:lo + gw]
            y_off = q["e_exp"][:, lo:lo + gw] * _dot(cg, st_gb)
            dc_g = _dot(dye_g, st_gb, NT)
            db_g = _dot(xdf_g, dst_gb, NT)
            dxd_state = _dot(bg, dst_gb) * q["f_exp"][:, lo:lo + gw]
            last_ref[:, lo:lo + gw] = jnp.sum(dst_g * st_g, axis=0, keepdims=True)
            dst_ref[:, lo:lo + gw] = q["el_exp"][:, lo:lo + gw] * dst_g + _dot(cg, dye_g, TN)
            off_ref[:, lo:lo + gw] = dyv[:, lo:lo + gw] * y_off
            vst_ref[:, lo:lo + gw] = xd[:, lo:lo + gw] * dxd_state
            dcb = jnp.zeros((l, l), F32)
            for pr in range(gw // LANES):
                c0 = lo + pr * LANES
                h0 = c0 // p
                xp = xdb[:, c0:c0 + LANES]
                dyp = dyb[:, c0:c0 + LANES]
                dxd_diag = jnp.zeros((l, LANES), F32)
                for k, keep in enumerate((lane < p, lane >= p)):
                    dec = _decay(q, h0 + k)
                    dy_h = jnp.where(keep, dyp, 0)
                    dm_dec = _dot(dy_h, xp, NT) * dec
                    dcb = dcb + dm_dec
                    dxd_diag = dxd_diag + _dot((cb * dec).astype(BF16), dy_h, TN)
                    qm = dm_dec * cb
                    row_sums = jnp.where(lane == h0 + k, jnp.sum(qm, axis=1, keepdims=True), row_sums)
                    col_sums = jnp.where(head_row == h0 + k, jnp.sum(qm, axis=0, keepdims=True), col_sums)
                dxd_ref[:, c0:c0 + LANES] = dxd_diag + dxd_state[:, pr * LANES:(pr + 1) * LANES]
            dcb_b = dcb.astype(BF16)
            dxc_ref[:, di + g * n: di + (g + 1) * n] = (db_g + _dot(dcb_b, cg, TN)).astype(BF16)
            dxc_ref[:, di + g_n * n + g * n: di + g_n * n + (g + 1) * n] = (dc_g + _dot(dcb_b, bg)).astype(BF16)
        dxd = dxd_ref[...]
        e_b = e_ref[...]
        from_y = _exact_left(upper_b, _head_sums(off_ref[...], e_b) + row_sums - col_sums.T)
        from_s = _exact_left(strict_lower.astype(BF16), _head_sums(vst_ref[...], e_b))
        carried = _head_sums(jnp.broadcast_to(last_ref[...], (8, di)), e_b)[0:1, :] * jnp.exp(q["acs"][l - 1:l, :])
        dla = from_y + from_s + carried
        ddt = dla * q["a_r"] + _head_sums(dxd * x, e_b)
        ddtr = ddt * jax.nn.sigmoid(q["sp_in"])
        ddtr_ref[...] = ddtr
        dxc_ref[:, 0:di] = (dxd * q["dt_exp"] + q["d_exp"] * dyv).astype(BF16)
        dd_exp = jnp.sum(dyv * x, axis=0, keepdims=True)
        dd = _head_sums(jnp.broadcast_to(dd_exp, (8, di)), e_b)[0:1, :]
        part = _stack_rows([jnp.sum(ddtr, axis=0, keepdims=True),
                            jnp.sum(dla * q["dt"], axis=0, keepdims=True) * q["a_r"], dd], LANES)

        @pl.when(first)
        def _():
            sums_ref[...] = part

        @pl.when(jnp.logical_not(first))
        def _():
            sums_ref[...] += part

    rev = lambda b, c: b * nc + (nc - 1 - c)
    tok = lambda w: pl.BlockSpec((l, w), lambda b, c: (rev(b, c), 0))
    const = lambda r, w: pl.BlockSpec((r, w), lambda b, c: (0, 0))
    return pl.pallas_call(
        body, name=name, grid=(nb, nc),
        in_specs=[tok(di), tok(cc), tok(LANES), pl.BlockSpec((LANES, l), lambda b, c: (0, rev(b, c))),
                  const(8, LANES), const(LANES, 8), const(LANES, di),
                  pl.BlockSpec((None, n, di), lambda b, c: (rev(b, c), 0, 0))],
        out_specs=[tok(cc), tok(LANES), const(8, LANES)],
        out_shape=[jax.ShapeDtypeStruct((nb * nc * l, cc), BF16), jax.ShapeDtypeStruct((nb * nc * l, LANES), F32),
                   jax.ShapeDtypeStruct((8, LANES), F32)],
        scratch_shapes=[pltpu.VMEM((n, di), F32), pltpu.VMEM((l, di), F32), pltpu.VMEM((l, di), F32),
                        pltpu.VMEM((1, di), F32), pltpu.VMEM((l, di), F32),
                        pltpu.VMEM((l, di), F32), pltpu.VMEM((l, di), F32), pltpu.VMEM((8, di), F32)],
        compiler_params=_params("arbitrary", "arbitrary"),
    )(dy, xc, dtr, dtrt, prow, pcol, e_mat, sprev)


def _gate_norm_fwd(y, proj, z_col0, norm_g, di, name, tt=256):
    t = y.shape[0]
    tt = _tile(t, tt, 8)
    gw = di // SSM_GROUPS
    zw = _tile(math.gcd(di, z_col0), di, LANES)
    nz, zoff = di // zw, z_col0 // zw

    def body(*refs):
        y_ref, z_refs, g_ref, o_ref = refs[0], refs[1:1 + nz], refs[1 + nz], refs[2 + nz]
        for g in range(SSM_GROUPS):
            lo = g * gw
            zv = z_refs[lo // zw][:, lo % zw:lo % zw + gw].astype(F32)
            yg = y_ref[:, lo:lo + gw] * _silu(zv)
            r = lax.rsqrt(jnp.mean(yg * yg, axis=-1, keepdims=True) + EPS)
            o_ref[:, lo:lo + gw] = (yg * r * g_ref[:, lo:lo + gw]).astype(BF16)

    row = pl.BlockSpec((tt, di), lambda i: (i, 0))
    zspecs = [pl.BlockSpec((tt, zw), functools.partial(lambda i, k: (i, zoff + k), k=k)) for k in range(nz)]
    return pl.pallas_call(
        body, name=name, grid=(t // tt,), in_specs=[row] + zspecs + [pl.BlockSpec((1, di), lambda i: (0, 0))],
        out_specs=row, out_shape=jax.ShapeDtypeStruct((t, di), BF16), compiler_params=_params("parallel"),
    )(y, *([proj] * nz), norm_g)


def _gate_norm_bwd(dn, y, proj, z_col0, norm_g, di, name, tt=256):
    t = y.shape[0]
    tt = _tile(t, tt, 8)
    gw = di // SSM_GROUPS
    zw = _tile(math.gcd(di, z_col0), di, LANES)
    nz, zoff = di // zw, z_col0 // zw

    def body(*refs):
        dn_ref, y_ref, z_refs, g_ref = refs[0], refs[1], refs[2:2 + nz], refs[2 + nz]
        dy_ref, dz_ref, dg_ref = refs[3 + nz:]
        first = pl.program_id(0) == 0
        for g in range(SSM_GROUPS):
            lo = g * gw
            zv = z_refs[lo // zw][:, lo % zw:lo % zw + gw].astype(F32)
            yv = y_ref[:, lo:lo + gw]
            sz = _silu(zv)
            yg = yv * sz
            r = lax.rsqrt(jnp.mean(yg * yg, axis=-1, keepdims=True) + EPS)
            yh = yg * r
            dnv = dn_ref[:, lo:lo + gw].astype(F32)
            gy = dnv * g_ref[:, lo:lo + gw]
            dyg = r * (gy - yh * jnp.mean(gy * yh, axis=-1, keepdims=True))
            dy_ref[:, lo:lo + gw] = dyg * sz
            dz_ref[:, lo:lo + gw] = (dyg * yv * _dsilu(zv)).astype(BF16)
            part = jnp.sum(dnv * yh, axis=0, keepdims=True)

            @pl.when(first)
            def _():
                dg_ref[:, lo:lo + gw] = part

            @pl.when(jnp.logical_not(first))
            def _():
                dg_ref[:, lo:lo + gw] += part

    row = pl.BlockSpec((tt, di), lambda i: (i, 0))
    vec = pl.BlockSpec((1, di), lambda i: (0, 0))
    zspecs = [pl.BlockSpec((tt, zw), functools.partial(lambda i, k: (i, zoff + k), k=k)) for k in range(nz)]
    return pl.pallas_call(
        body, name=name, grid=(t // tt,), in_specs=[row, row] + zspecs + [vec], out_specs=[row, row, vec],
        out_shape=[jax.ShapeDtypeStruct((t, di), F32), jax.ShapeDtypeStruct((t, di), BF16), jax.ShapeDtypeStruct((1, di), F32)],
        compiler_params=_params("arbitrary"),
    )(dn, y, *([proj] * nz), norm_g)


def _softmax_rows(s):
    s = s - jnp.max(s, axis=-1, keepdims=True)
    e = jnp.exp(s)
    return e * (1.0 / jnp.sum(e, axis=-1, keepdims=True))


def _xattn_fwd(q, kv, nb, s, m, d, name, tq=512):
    tq = _tile(s, tq, 8)
    nq = s // tq
    hd = d // XATTN_HEADS
    scale = 1.0 / math.sqrt(hd)

    def body(q_ref, k_ref, v_ref, o_ref):
        for h in range(XATTN_HEADS):
            sl = slice(h * hd, (h + 1) * hd)
            prob = _softmax_rows(_dot(q_ref[:, sl], k_ref[:, sl], NT) * scale)
            o_ref[:, sl] = _dot(prob.astype(BF16), v_ref[:, sl]).astype(BF16)

    return pl.pallas_call(
        body, name=name, grid=(nb, nq),
        in_specs=[pl.BlockSpec((tq, d), lambda b, i: (b * nq + i, 0)), pl.BlockSpec((m, d), lambda b, i: (b, 0)),
                  pl.BlockSpec((m, d), lambda b, i: (b, 1))],
        out_specs=pl.BlockSpec((tq, d), lambda b, i: (b * nq + i, 0)),
        out_shape=jax.ShapeDtypeStruct((nb * s, d), BF16), compiler_params=_params("parallel", "parallel"),
    )(q, kv, kv)


def _xattn_bwd(do, q, kv, nb, s, m, d, name, tq=512):
    tq = _tile(s, tq, 8)
    nq = s // tq
    hd = d // XATTN_HEADS
    scale = 1.0 / math.sqrt(hd)

    def body(do_ref, q_ref, k_ref, v_ref, dq_ref, dk_ref, dv_ref):
        first = pl.program_id(1) == 0
        for h in range(XATTN_HEADS):
            sl = slice(h * hd, (h + 1) * hd)
            qh, kh, vh, doh = q_ref[:, sl], k_ref[:, sl], v_ref[:, sl], do_ref[:, sl]
            prob = _softmax_rows(_dot(qh, kh, NT) * scale)
            dv_h = _dot(prob.astype(BF16), doh, TN)
            dp = _dot(doh, vh, NT)
            ds = (prob * (dp - jnp.sum(dp * prob, axis=-1, keepdims=True)) * scale).astype(BF16)
            dq_ref[:, sl] = _dot(ds, kh).astype(BF16)
            dk_h = _dot(ds, qh, TN)

            @pl.when(first)
            def _():
                dk_ref[:, sl] = dk_h
                dv_ref[:, sl] = dv_h

            @pl.when(jnp.logical_not(first))
            def _():
                dk_ref[:, sl] += dk_h
                dv_ref[:, sl] += dv_h

    qspec = pl.BlockSpec((tq, d), lambda b, i: (b * nq + i, 0))
    dq, dk, dv = pl.pallas_call(
        body, name=name, grid=(nb, nq),
        in_specs=[qspec, qspec, pl.BlockSpec((m, d), lambda b, i: (b, 0)), pl.BlockSpec((m, d), lambda b, i: (b, 1))],
        out_specs=[qspec, pl.BlockSpec((m, d), lambda b, i: (b, 0)), pl.BlockSpec((m, d), lambda b, i: (b, 0))],
        out_shape=[jax.ShapeDtypeStruct((nb * s, d), BF16), jax.ShapeDtypeStruct((nb * m, d), F32),
                   jax.ShapeDtypeStruct((nb * m, d), F32)],
        compiler_params=_params("parallel", "arbitrary"),
    )(do, q, kv, kv)
    return dq, dk, dv


def _all_gather(shards, name):
    n_arr = len(shards)

    def body(*refs):
        x_refs, out_refs = refs[:n_arr], refs[n_arr:2 * n_arr]
        send_sems, recv_sems, local_sems = refs[2 * n_arr:]
        x, y, c = lax.axis_index("x"), lax.axis_index("y"), lax.axis_index("c")
        me, sibling = (x, y, c), (x, y, 1 - c)
        chips = [(1 - x, y), (x, 1 - y), (1 - x, 1 - y)]

        def copy(w, k, block, to, from_input=False):
            px, py, pc = block
            rows = out_refs[w].at[4 * px + 2 * py + pc]
            return pltpu.make_async_remote_copy(
                src_ref=x_refs[w] if from_input else rows, dst_ref=rows,
                send_sem=send_sems.at[7 * w + k], recv_sem=recv_sems.at[7 * w + k], device_id=to, device_id_type=MESH)

        started = []
        for w in range(n_arr):
            mine = pltpu.make_async_copy(x_refs[w], out_refs[w].at[4 * x + 2 * y + c], local_sems.at[w])
            mine.start()
            started.append(mine)
        sends = []
        for w in range(n_arr):
            sends.append(copy(w, 0, me, sibling, from_input=True))
            sends += [copy(w, 1 + j, me, (*chip, c), from_input=True) for j, chip in enumerate(chips)]
        for cp in sends:
            cp.start()
        for j, chip in enumerate(chips):
            for w in range(n_arr):
                copy(w, 1 + j, (*chip, c), me).wait_recv()
                passed = copy(w, 4 + j, (*chip, c), sibling)
                passed.start()
                sends.append(passed)
        for w in range(n_arr):
            copy(w, 0, sibling, me).wait_recv()
            for j, chip in enumerate(chips):
                copy(w, 4 + j, (*chip, 1 - c), me).wait_recv()
        for cp in sends:
            cp.wait_send()
        for mine in started:
            mine.wait()

    hbm = pl.BlockSpec(memory_space=pl.ANY)
    return pl.pallas_call(
        body, name=name, out_shape=[jax.ShapeDtypeStruct((N_DEV,) + s.shape, s.dtype) for s in shards],
        in_specs=[hbm] * n_arr, out_specs=[hbm] * n_arr,
        scratch_shapes=[pltpu.SemaphoreType.DMA((7 * n_arr,)), pltpu.SemaphoreType.DMA((7 * n_arr,)),
                        pltpu.SemaphoreType.DMA((n_arr,))],
    )(*shards)


_HBM = pl.BlockSpec(memory_space=pltpu.HBM)
_SEM = pl.BlockSpec(memory_space=pltpu.SEMAPHORE)
_DATAFLOW = pltpu.SideEffectType.DATAFLOW_SIDE_EFFECTING


def _peer_list(x, y, c):
    return [(1 - x if k & 4 else x, 1 - y if k & 2 else y, 1 - c if k & 1 else c) for k in range(1, N_DEV)]


def _push_copy(src_ref, land_ref, send_sems, recv_sems, w, k, peer, me, per_peer_src, receiving):
    px, py, pc = peer
    peer_slot = 4 * px + 2 * py + pc
    return pltpu.make_async_remote_copy(
        src_ref=src_ref.at[peer_slot] if per_peer_src else src_ref,
        dst_ref=land_ref.at[peer_slot if receiving else me],
        send_sem=send_sems.at[7 * w + k], recv_sem=recv_sems.at[7 * w + k], device_id=peer, device_id_type=MESH)


def _push_start(srcs, per_peer_src, after, name):
    n_arr = len(srcs)
    land_shapes = [s.shape if per_peer_src else (N_DEV,) + s.shape for s in srcs]

    def body(*refs):
        src_refs, land_refs = refs[:n_arr], refs[n_arr:2 * n_arr]
        send_sems, recv_sems = refs[2 * n_arr + 1], refs[2 * n_arr + 2]
        token = refs[-1]
        x, y, c = lax.axis_index("x"), lax.axis_index("y"), lax.axis_index("c")
        me = 4 * x + 2 * y + c
        for w in range(n_arr):
            for k, peer in enumerate(_peer_list(x, y, c)):
                _push_copy(src_refs[w], land_refs[w], send_sems, recv_sems, w, k, peer, me, per_peer_src, False).start()
        token[...] = jnp.zeros_like(token)

    lands = [pltpu.with_memory_space_constraint(lax.empty(ls, s.dtype), pltpu.HBM) for ls, s in zip(land_shapes, srcs)]
    srcs_hbm = [pltpu.with_memory_space_constraint(s, pltpu.HBM) for s in srcs]
    out = pl.pallas_call(
        body, name=name,
        out_shape=(pltpu.SemaphoreType.DMA((7 * n_arr,)), pltpu.SemaphoreType.DMA((7 * n_arr,)),
                   *[pltpu.HBM(s.shape, s.dtype) for s in srcs], *[pltpu.HBM(ls, s.dtype) for ls, s in zip(land_shapes, srcs)],
                   jax.ShapeDtypeStruct((8, LANES), F32)),
        in_specs=[_HBM] * (2 * n_arr) + [pl.BlockSpec(memory_space=pl.ANY)],
        out_specs=(_SEM, _SEM, *([_HBM] * (2 * n_arr)), pl.BlockSpec(memory_space=pltpu.VMEM)),
        input_output_aliases={i: 2 + i for i in range(2 * n_arr)},
        compiler_params=pltpu.CompilerParams(has_side_effects=_DATAFLOW),
    )(*srcs_hbm, *lands, after)
    return dict(send=out[0], recv=out[1], srcs=list(out[2:2 + n_arr]), lands=list(out[2 + n_arr:2 + 2 * n_arr]),
                token=out[-1])


def _push_wait(pending, per_peer_src, after, name):
    n_arr = len(pending["srcs"])

    def body(*refs):
        src_refs, land_refs = refs[:n_arr], refs[n_arr:2 * n_arr]
        send_sems, recv_sems = refs[2 * n_arr], refs[2 * n_arr + 1]
        x, y, c = lax.axis_index("x"), lax.axis_index("y"), lax.axis_index("c")
        me = 4 * x + 2 * y + c
        for w in range(n_arr):
            for k, peer in enumerate(_peer_list(x, y, c)):
                cp = _push_copy(src_refs[w], land_refs[w], send_sems, recv_sems, w, k, peer, me, per_peer_src, True)
                cp.wait_send()
                cp.wait_recv()

    out = pl.pallas_call(
        body, name=name,
        out_shape=tuple(pltpu.HBM(a.shape, a.dtype) for a in pending["srcs"] + pending["lands"]),
        in_specs=[_HBM] * (2 * n_arr) + [_SEM, _SEM, pl.BlockSpec(memory_space=pl.ANY)],
        out_specs=tuple([_HBM] * (2 * n_arr)),
        input_output_aliases={i: i for i in range(2 * n_arr)},
        compiler_params=pltpu.CompilerParams(has_side_effects=_DATAFLOW),
    )(*pending["srcs"], *pending["lands"], pending["send"], pending["recv"], after)
    return list(out[:n_arr]), list(out[n_arr:])


def _adamw_math(w, g, m, v):
    m = ADAM_B1 * m + (1.0 - ADAM_B1) * g
    v = ADAM_B2 * v + (1.0 - ADAM_B2) * (g * g)
    m_hat = m / (1.0 - ADAM_B1 ** ADAM_STEP)
    v_hat = v / (1.0 - ADAM_B2 ** ADAM_STEP)
    delta = -ADAM_LR * (m_hat / (jnp.sqrt(v_hat) + ADAM_EPS) + ADAM_WD * w)
    return delta, m, v


def _sum8(parts, name, tr=512):
    _, r, c_dim = parts.shape
    tr = _tile(r, tr, BF16_SUBLANES)

    def body(p_ref, o_ref):
        acc = p_ref[0].astype(F32)
        for k in range(1, N_DEV):
            acc = acc + p_ref[k].astype(F32)
        o_ref[...] = acc

    return pl.pallas_call(
        body, name=name, grid=(r // tr,), in_specs=[pl.BlockSpec((N_DEV, tr, c_dim), lambda i: (0, i, 0))],
        out_specs=pl.BlockSpec((tr, c_dim), lambda i: (i, 0)),
        out_shape=jax.ShapeDtypeStruct((r, c_dim), F32), compiler_params=_params("parallel"),
    )(parts)


def _sum8_adamw(parts, w, m, v, name, tr=128):
    _, r, c_dim = parts.shape
    tr = _tile(r, tr, BF16_SUBLANES)
    tc = c_dim if tr <= 2 * LANES else _tile(c_dim, LANES, LANES)

    def body(p_ref, w_ref, m_ref, v_ref, g_ref, d_ref, nm_ref, nv_ref):
        g = p_ref[0].astype(F32)
        for k in range(1, N_DEV):
            g = g + p_ref[k].astype(F32)
        g_ref[...] = g
        d_ref[...], nm_ref[...], nv_ref[...] = _adamw_math(w_ref[...], g, m_ref[...], v_ref[...])

    blk = pl.BlockSpec((None, tr, tc), lambda i, j: (0, i, j))
    out = jax.ShapeDtypeStruct((1, r, c_dim), F32)
    return pl.pallas_call(
        body, name=name, grid=(r // tr, c_dim // tc),
        in_specs=[pl.BlockSpec((N_DEV, tr, tc), lambda i, j: (0, i, j)), blk, blk, blk],
        out_specs=[blk] * 4, out_shape=[out] * 4, compiler_params=_params("parallel", "parallel"),
    )(parts, w, m, v)


def _adamw(g, w, m, v, name):
    r, c_dim = g.shape

    def body(g_ref, w_ref, m_ref, v_ref, d_ref, nm_ref, nv_ref):
        d_ref[...], nm_ref[...], nv_ref[...] = _adamw_math(w_ref[...], g_ref[...], m_ref[...], v_ref[...])

    out = jax.ShapeDtypeStruct((r, c_dim), F32)
    return pl.pallas_call(body, name=name, out_shape=[out] * 3)(g, w, m, v)


def _pack_rows(arrays, dtype, row_unit):
    chunks, offs, r0 = [], [], 0
    for a in arrays:
        flat = a.reshape(-1).astype(dtype)
        rows = -(-flat.shape[0] // (LANES * row_unit)) * row_unit
        flat = jnp.pad(flat, (0, rows * LANES - flat.shape[0]))
        chunks.append(flat.reshape(rows, LANES))
        offs.append((r0, rows))
        r0 += rows
    return jnp.concatenate(chunks, axis=0), offs


def _unpack_rows(packed, offs, shapes):
    out = []
    for (r0, rows), shape in zip(offs, shapes):
        n = math.prod(shape)
        blk = packed[..., r0:r0 + rows, :]
        blk = blk.reshape(packed.shape[:-2] + (rows * LANES,))[..., :n]
        out.append(blk.reshape(packed.shape[:-2] + tuple(shape)))
    return out


def _full_from_slots(blk, col_sharded):
    _, r, c = blk.shape
    if col_sharded:
        return blk.transpose(1, 0, 2).reshape(r, N_DEV * c)
    return blk.reshape(N_DEV * r, c)


def _slots_from_full(full, col_sharded):
    r, c = full.shape
    if col_sharded:
        return full.reshape(r, N_DEV, c // N_DEV).transpose(1, 0, 2)
    return full.reshape(N_DEV, r // N_DEV, c)


def _ffn_fwd(h, n, w_gu_t, w_d, tag, next_gain=None):
    gate, up, a = _ffn_up(n, w_gu_t, f"{tag}_up")
    out = _mm(a, w_d, F32, f"{tag}_down", res=h, alpha=FFN_RES_WEIGHT, norm_out=next_gain)
    h_out, n_next = out if next_gain is not None else (out, None)
    return h_out, (h, n, gate, up, a), n_next


def _ffn_bwd(dh_out, saved, g, w_gu_t, w_d, tag, dep, send_grads):
    h, n, gate, up, a = saved
    dw_d = _mm_tn(a, dh_out, f"{tag}_dw_down", alpha=FFN_RES_WEIGHT, dep=dep)
    dgate, dup = _ffn_da(dh_out, w_d, gate, up, f"{tag}_da", FFN_RES_WEIGHT)
    f = dgate.shape[1]
    dw_gu_t = _mm_tn(dgate, n, f"{tag}_dw_gate", out_rows=2 * f)
    dw_gu_t = _mm_tn(dup, n, f"{tag}_dw_up", out_rows=2 * f, row_off=f, into=dw_gu_t)
    dep = send_grads(dw_gu_t, dw_d)
    return _mm([dgate, dup], w_gu_t, F32, f"{tag}_dn", dep=dep, tm=512, res=dh_out, norm_bwd=(h, g))


W_GROUPS = (("ffn1_w_gate_up", "ffn1_w_down"),
            ("w_in",),
            ("w_out_a", "w_out_ssm", "w_mix_out"),
            ("w_q", "w_kv", "w_o_x", "ffn2_w_gate_up", "ffn2_w_down"))
G_GROUPS = (("ffn2_w_gate_up", "ffn2_w_down"),
            ("w_o_x", "w_q", "w_kv", "w_mix_out", "w_out_a", "w_out_ssm", "w_in"),
            ("ffn1_w_gate_up", "ffn1_w_down"))


def _local_step(x3, mem3, target3, small, comm):
    nb, s, d = x3.shape
    m_len = mem3.shape[1]
    t = nb * s
    nc = s // SSM_CHUNK
    di = small["ssm_norm"].shape[1]
    hs = di // SSM_HEAD_DIM
    cc = di + 2 * SSM_GROUPS * SSM_STATE
    x, mem, target = x3.reshape(t, d), mem3.reshape(nb * m_len, d), target3.reshape(t, d)

    sizes = (d, d, d, di, cc, hs, d, d)
    offs = [0]
    for sz in sizes:
        offs.append(offs[-1] + sz)
    z_col0, xbc_col0 = 3 * d, 3 * d + di
    ga_blk, gb_blk = (3 * d + di + cc) // d, (4 * d + di + cc) // d

    pad_vec = lambda v: jnp.pad(v.reshape(1, -1), ((0, 0), (0, LANES - hs)))
    prow = jnp.concatenate([pad_vec(small["ssm_dt_bias"]), pad_vec(small["ssm_a_log"]), pad_vec(small["ssm_d"]),
                            jnp.zeros((5, LANES), F32)], axis=0)
    pcol = prow.T
    e_mat = (lax.broadcasted_iota(jnp.int32, (LANES, di), 0)
             == lax.broadcasted_iota(jnp.int32, (LANES, di), 1) // SSM_HEAD_DIM).astype(BF16)
    conv_a_w8 = jnp.pad(small["conv_a_w"][0], ((0, 8 - CONV_A_K), (0, 0)))
    ssm_conv_w8 = jnp.pad(small["ssm_conv_w"][0], ((0, 8 - SSM_CONV_K), (0, 0)))

    wts, dep = comm.weights(0, None)
    n1 = _rms_fwd(x, small["ffn1_norm"] + dep[0, 0], "ffn1_norm")
    h1, ffn1_saved, u = _ffn_fwd(x, n1, wts["ffn1_w_gate_up"], wts["ffn1_w_down"], "ffn1", small["mix_norm"])
    got, dep = comm.weights(1, h1)
    wts.update(got)
    w_in_t = wts["w_in"]
    w_main_t = jnp.concatenate([w_in_t[offs[i]:offs[i + 1]] for i in (0, 1, 2, 3, 4, 6, 7)], axis=0)
    w_dt_t = jnp.pad(w_in_t[offs[5]:offs[6]], ((0, LANES - hs), (0, 0)))
    proj = _mm(u, w_main_t, BF16, "in_proj", nt=True, dep=dep)
    dtr = _mm(u, w_dt_t, F32, "in_proj_dt", nt=True)
    yap = _conv_a_fwd(proj, conv_a_w8, nb, s, d, "conv_a")
    got, dep = comm.weights(2, yap)
    wts.update(got)
    y_a = _mm(yap, wts["w_out_a"], BF16, "out_a", dep=dep)
    xc, conv_pre = _conv_s_fwd(proj, xbc_col0, ssm_conv_w8, small["ssm_conv_b"] + dep[0, 0], nb, s, cc, "conv_s")
    dtrt = dtr.T
    y_ssd, sprev = _ssd_fwd(xc, dtr, dtrt, prow, pcol, nb, nc, di, "ssd")
    ygn = _gate_norm_fwd(y_ssd, proj, z_col0, small["ssm_norm"], di, "gate_norm")
    y_b = _mm(ygn, wts["w_out_ssm"], BF16, "out_ssm")
    merged = _merge_fwd(y_a, y_b, proj, ga_blk, gb_blk, d, "merge")
    h2, un = _mm(merged, wts["w_mix_out"], F32, "mix_out", res=h1, norm_out=small["xattn_norm"])
    got, _ = comm.weights(3, h2)
    wts.update(got)
    mn = _rms_fwd(mem, small["mem_norm"], "mem_norm")
    q = _mm(un, wts["w_q"], BF16, "q_proj")
    kv = _mm(mn, wts["w_kv"], BF16, "kv_proj", nt=True)
    o = _xattn_fwd(q, kv, nb, s, m_len, d, "xattn")
    h3, n2 = _mm(o, wts["w_o_x"], F32, "o_proj", res=h2, norm_out=small["ffn2_norm"])
    h4, ffn2_saved, _ = _ffn_fwd(h3, n2, wts["ffn2_w_gate_up"], wts["ffn2_w_down"], "ffn2")
    loss_vec, dh4, dg_final = _loss_head(h4, small["final_norm"].reshape(1, d), target, "loss_head")

    grads = {"final_norm": dg_final.reshape(d)}
    big = {}
    dh3, grads["ffn2_norm"] = _ffn_bwd(
        dh4, ffn2_saved, small["ffn2_norm"], wts["ffn2_w_gate_up"], wts["ffn2_w_down"], "ffn2", None,
        lambda dw_gu_t, dw_d: comm.grads(0, {"ffn2_w_gate_up": dw_gu_t, "ffn2_w_down": dw_d}))
    big["w_o_x"] = _mm_tn(o, dh3, "dw_o")
    do = _mm(dh3, wts["w_o_x"], BF16, "d_o", nt=True)
    dq, dk, dv = _xattn_bwd(do, q, kv, nb, s, m_len, d, "xattn_bwd")
    big["w_q"] = _mm_tn(un, dq, "dw_q")
    big["w_kv"] = _mm_tn(dv, mn, "dw_v", out_rows=2 * d, row_off=d, into=_mm_tn(dk, mn, "dw_k", out_rows=2 * d))
    _, grads["mem_norm"] = _mm([dk, dv], wts["w_kv"], F32, "d_mn", norm_bwd=(mem, small["mem_norm"]))
    dh2, grads["xattn_norm"] = _mm(dq, wts["w_q"], F32, "d_un", nt=True, res=dh3, norm_bwd=(h2, small["xattn_norm"]))
    big["w_mix_out"] = _mm_tn(merged, dh2, "dw_mix")
    dmerged = _mm(dh2, wts["w_mix_out"], BF16, "d_merged", nt=True)
    dya, dyb, dga, dgb = _merge_bwd(dmerged, y_a, y_b, proj, ga_blk, gb_blk, d, "merge_bwd")
    big["w_out_a"] = _mm_tn(yap, dya, "dw_out_a")
    big["w_out_ssm"] = _mm_tn(ygn, dyb, "dw_out_ssm")
    dyap = _mm(dya, wts["w_out_a"], BF16, "d_yap", nt=True)
    dygn = _mm(dyb, wts["w_out_ssm"], BF16, "d_ygn", nt=True)
    dab, dac, dav, dconv_a = _conv_a_bwd(dyap, proj, conv_a_w8, nb, s, d, "conv_a_bwd")
    dy_ssd, dz, grads["ssm_norm"] = _gate_norm_bwd(dygn, y_ssd, proj, z_col0, small["ssm_norm"], di, "gate_norm_bwd")
    dxc, ddtr, ssd_sums = _ssd_bwd(dy_ssd, xc, dtr, dtrt, prow, pcol, e_mat, sprev, nb, nc, di, "ssd_bwd")
    dxbc, dconv_s, grads["ssm_conv_b"] = _conv_s_bwd(dxc, conv_pre, proj, xbc_col0, ssm_conv_w8, nb, s, cc, "conv_s_bwd")
    dpieces = [("ab", dab), ("ac", dac), ("av", dav), ("z", dz), ("xbc", dxbc), ("ga", dga), ("gb", dgb)]
    dw = {tag: _mm_tn(piece, u, f"dw_in_{tag}") for tag, piece in dpieces}
    dw_dt = _mm_tn(ddtr, u, "dw_in_dt")[:hs]
    du_main = _mm([piece for _, piece in dpieces], w_main_t, F32, "d_u", tk=1024)
    du = _mm(ddtr, w_dt_t, F32, "d_u_dt", res=du_main)
    dh1, grads["mix_norm"] = _rms_bwd(du, h1, small["mix_norm"], "mix_dnorm", dres=dh2)
    big["w_in"] = jnp.concatenate([dw["ab"], dw["ac"], dw["av"], dw["z"], dw["xbc"], dw_dt, dw["ga"], dw["gb"]], axis=0)
    dep = comm.grads(1, big)
    dx, grads["ffn1_norm"] = _ffn_bwd(
        dh1, ffn1_saved, small["ffn1_norm"], wts["ffn1_w_gate_up"], wts["ffn1_w_down"], "ffn1", dep,
        lambda dw_gu_t, dw_d: comm.grads(2, {"ffn1_w_gate_up": dw_gu_t, "ffn1_w_down": dw_d}))

    grads["conv_a_w"] = dconv_a[:CONV_A_K]
    grads["ssm_conv_w"] = dconv_s[:SSM_CONV_K]
    grads["ssm_dt_bias"] = ssd_sums[0:1, :hs]
    grads["ssm_a_log"] = ssd_sums[1:2, :hs]
    grads["ssm_d"] = ssd_sums[2:3, :hs]
    return loss_vec[0, 0], dx.reshape(nb, s, d), grads


def _step(inputs):
    w = {k: inputs[k] for k in WEIGHT_ORDER}
    mom = {k: inputs["m_" + k] for k in WEIGHT_ORDER}
    vel = {k: inputs["v_" + k] for k in WEIGHT_ORDER}
    me = 4 * lax.axis_index("x") + 2 * lax.axis_index("y") + lax.axis_index("c")

    send = {k: (w[k][0].T if k in COL_SHARDED else w[k][0]).astype(BF16) for k in BIG_WEIGHTS}

    def own_slot(land, mine):
        return lax.dynamic_update_slice(land, mine[None], (me, 0, 0))

    gathers, exchanges = {}, {}

    def weights(i, after):
        if i == 0:
            lands = _all_gather([send[k] for k in W_GROUPS[0]], "gather0")
        else:
            sent, lands = _push_wait(gathers[i], False, after, f"gather{i}_wait")
            lands = [own_slot(land, mine) for land, mine in zip(lands, sent)]
        full = {k: land.reshape(N_DEV * land.shape[1], land.shape[2]) for k, land in zip(W_GROUPS[i], lands)}
        dep = jnp.zeros((8, LANES), F32)
        if i + 1 < len(W_GROUPS):
            gathers[i + 1] = _push_start([send[k] for k in W_GROUPS[i + 1]], False, lands[0], f"gather{i + 1}_start")
            dep = gathers[i + 1]["token"]
        return full, dep

    def send_grads(i, by_name):
        slots = [by_name[k].reshape((N_DEV,) + send[k].shape) for k in G_GROUPS[i]]
        exchanges[i] = _push_start(slots, True, slots[0], f"exchange{i}_start")
        return exchanges[i]["token"]

    comm = types.SimpleNamespace(weights=weights, grads=send_grads)
    small = {k: w[k] for k in SMALL_REPLICATED}
    conv_shapes = [w[k].shape[1:] for k in SMALL_SHARDED]
    packed_c, conv_offs = _pack_rows([w[k][0] for k in SMALL_SHARDED], F32, 8)
    conv_blocks = _unpack_rows(_all_gather([packed_c], "gather_conv_weights")[0], conv_offs, conv_shapes)
    for k, b in zip(SMALL_SHARDED, conv_blocks):
        small[k] = _full_from_slots(b, True)[None]

    loss_local, grad_x, grads = _local_step(inputs["x"], inputs["mem"], inputs["loss_target"], small, comm)
    loss = lax.psum(loss_local, AXES)

    out = {}
    for i, names in enumerate(G_GROUPS):
        sent, lands = _push_wait(exchanges[i], True, grad_x, f"exchange{i}_wait")
        for k, land, slots in zip(names, lands, sent):
            parts = own_slot(land, lax.dynamic_index_in_dim(slots, me, 0, keepdims=False))
            if k in COL_SHARDED and w[k].shape[2] % LANES:
                flip = lambda a: a.transpose(0, 2, 1)
                out[k] = tuple(flip(o) for o in _sum8_adamw(parts, flip(w[k]), flip(mom[k]), flip(vel[k]), f"sum_adamw_{k}"))
                continue
            if k in COL_SHARDED:
                parts = parts.transpose(0, 2, 1)
            out[k] = tuple(_sum8_adamw(parts, w[k], mom[k], vel[k], f"sum_adamw_{k}"))

    small_names = SMALL_REPLICATED + SMALL_SHARDED
    packed_g, small_offs = _pack_rows([grads[k] for k in small_names], F32, 8)
    total = _sum8(_all_gather([packed_g], "gather_small_grads")[0], "sum_small_grads")
    full_grads = _unpack_rows(total, small_offs, [grads[k].shape for k in small_names])
    mine = {}
    for k, g in zip(small_names, full_grads):
        if k in SMALL_SHARDED:
            c_loc = w[k].shape[2]
            g = lax.dynamic_slice_in_dim(g, me * c_loc, c_loc, axis=1)
        mine[k] = g.reshape(w[k].shape)
    sg, s_offs = _pack_rows([mine[k] for k in small_names], F32, 8)
    sw, _ = _pack_rows([w[k] for k in small_names], F32, 8)
    sm, _ = _pack_rows([mom[k] for k in small_names], F32, 8)
    sv, _ = _pack_rows([vel[k] for k in small_names], F32, 8)
    s_shapes = [w[k].shape for k in small_names]
    small_out = [_unpack_rows(a, s_offs, s_shapes) for a in _adamw(sg, sw, sm, sv, "adamw_small")]
    for i, k in enumerate(small_names):
        out[k] = (mine[k],) + tuple(o[i] for o in small_out)

    res = [loss, grad_x]
    for j in range(4):
        res += [out[k][j] for k in WEIGHT_ORDER]
    return tuple(res)


def kernel(x, mem, ffn1_norm, ffn1_w_gate_up, ffn1_w_down, mix_norm, w_in, conv_a_w, w_out_a, ssm_conv_w, ssm_conv_b, ssm_dt_bias, ssm_a_log, ssm_d, ssm_norm, w_out_ssm, w_mix_out, xattn_norm, mem_norm, w_q, w_kv, w_o_x, ffn2_norm, ffn2_w_gate_up, ffn2_w_down, final_norm, loss_target, m_ffn1_norm, m_ffn1_w_gate_up, m_ffn1_w_down, m_mix_norm, m_w_in, m_conv_a_w, m_w_out_a, m_ssm_conv_w, m_ssm_conv_b, m_ssm_dt_bias, m_ssm_a_log, m_ssm_d, m_ssm_norm, m_w_out_ssm, m_w_mix_out, m_xattn_norm, m_mem_norm, m_w_q, m_w_kv, m_w_o_x, m_ffn2_norm, m_ffn2_w_gate_up, m_ffn2_w_down, m_final_norm, v_ffn1_norm, v_ffn1_w_gate_up, v_ffn1_w_down, v_mix_norm, v_w_in, v_conv_a_w, v_w_out_a, v_ssm_conv_w, v_ssm_conv_b, v_ssm_dt_bias, v_ssm_a_log, v_ssm_d, v_ssm_norm, v_w_out_ssm, v_w_mix_out, v_xattn_norm, v_mem_norm, v_w_q, v_w_kv, v_w_o_x, v_ffn2_norm, v_ffn2_w_gate_up, v_ffn2_w_down, v_final_norm):
    return _step(dict(locals()))
```

```python
import functools
import math
import types

import jax
import jax.numpy as jnp
from jax import lax
from jax.experimental import pallas as pl
from jax.experimental.pallas import tpu as pltpu

F32, BF16 = jnp.float32, jnp.bfloat16
HI = lax.Precision.HIGHEST
MESH = pl.DeviceIdType.MESH
AXES = ("x", "y", "c")
N_DEV = 8

EPS = 1e-6
FFN_RES_WEIGHT = 0.5
SSM_HEAD_DIM = 64
SSM_GROUPS = 4
SSM_STATE = 128
SSM_CHUNK = 128
CONV_A_K = 3
SSM_CONV_K = 4
XATTN_HEADS = 4
ADAM_LR, ADAM_B1, ADAM_B2, ADAM_EPS, ADAM_WD, ADAM_STEP = 1e-3, 0.9, 0.999, 1e-8, 0.01, 10

LANES = 128
BF16_SUBLANES = 16
VMEM_LIMIT_BYTES = 56 * 2 ** 20
NEG_BIG = -1e30

BIG_WEIGHTS = ("ffn1_w_gate_up", "ffn1_w_down", "w_in", "w_out_a", "w_out_ssm", "w_mix_out",
               "w_q", "w_kv", "w_o_x", "ffn2_w_gate_up", "ffn2_w_down")
COL_SHARDED = ("ffn1_w_gate_up", "w_in", "w_kv", "ffn2_w_gate_up")
SMALL_REPLICATED = ("ffn1_norm", "mix_norm", "ssm_conv_b", "ssm_dt_bias", "ssm_a_log", "ssm_d", "ssm_norm",
                    "xattn_norm", "mem_norm", "ffn2_norm", "final_norm")
SMALL_SHARDED = ("conv_a_w", "ssm_conv_w")
WEIGHT_ORDER = ("ffn1_norm", "ffn1_w_gate_up", "ffn1_w_down", "mix_norm", "w_in", "conv_a_w", "w_out_a",
                "ssm_conv_w", "ssm_conv_b", "ssm_dt_bias", "ssm_a_log", "ssm_d", "ssm_norm", "w_out_ssm",
                "w_mix_out", "xattn_norm", "mem_norm", "w_q", "w_kv", "w_o_x", "ffn2_norm", "ffn2_w_gate_up",
                "ffn2_w_down", "final_norm")


def _tile(dim, pref, unit):
    best = None
    t = unit
    while t <= min(dim, pref):
        if dim % t == 0:
            best = t
        t += unit
    return best if best is not None else dim


def _params(*sem):
    return pltpu.CompilerParams(dimension_semantics=sem, vmem_limit_bytes=VMEM_LIMIT_BYTES)


def _sigmoid(x):
    return pl.reciprocal(1.0 + jnp.exp(-x), approx=True)


def _silu(x):
    return x * _sigmoid(x)


def _dsilu(x):
    s = _sigmoid(x)
    return s * (1.0 + x * (1.0 - s))


def _softplus(x):
    return jnp.maximum(x, 0.0) + jnp.log(1.0 + jnp.exp(-jnp.abs(x)))


def _dot(a, b, dims=(((1,), (0,)), ((), ())), precision=None):
    return lax.dot_general(a, b, dims, preferred_element_type=F32, precision=precision)


def _stack_rows(rows, width):
    r_idx = lax.broadcasted_iota(jnp.int32, (8, width), 0)
    acc = jnp.zeros((8, width), F32)
    for k, row in enumerate(rows):
        acc = jnp.where(r_idx == k, row, acc)
    return acc


NT = (((1,), (1,)), ((), ()))
TN = (((0,), (0,)), ((), ()))


def _mm(a, b, out_dtype, name, res=None, alpha=1.0, nt=False, dep=None, norm_out=None, norm_bwd=None,
        tm=1024, tn=2048, tk=2816):
    pieces = list(a) if isinstance(a, (list, tuple)) else [a]
    m = pieces[0].shape[0]
    k = sum(p.shape[1] for p in pieces)
    n = b.shape[0] if nt else b.shape[1]
    assert (b.shape[1] if nt else b.shape[0]) == k
    tm, tn = _tile(m, tm, 8), _tile(n, tn, LANES)
    tk = _tile(math.gcd(*[p.shape[1] for p in pieces]), tk, LANES)
    nk = k // tk
    starts, s0 = [], 0
    for p in pieces:
        starts.append((s0, p.shape[1] // tk))
        s0 += p.shape[1] // tk
    n_p = len(pieces)
    assert norm_out is None or norm_bwd is None
    whole_rows = norm_out is not None or norm_bwd is not None
    assert not whole_rows or tn == n
    has_pre = norm_bwd is not None and len(norm_bwd) == 3

    def body(*refs):
        a_refs, b_ref = refs[:n_p], refs[n_p]
        nxt = n_p + 1
        r_ref = refs[nxt] if res is not None else None
        nxt += (res is not None) + (dep is not None)
        g_ref = refs[nxt] if whole_rows else None
        x_ref = refs[nxt + 1] if norm_bwd is not None else None
        pre_ref = refs[nxt + 2] if has_pre else None
        nxt += whole_rows + (norm_bwd is not None) + has_pre
        o_ref = refs[nxt]
        o2_ref = refs[nxt + 1] if whole_rows else None
        scr = refs[nxt + 1 + whole_rows:]
        first_rows = pl.program_id(0) == 0

        def finish(acc):
            acc = alpha * acc if alpha != 1.0 else acc
            if norm_bwd is not None:
                if pre_ref is not None:
                    acc = acc + pre_ref[...]
                xv = x_ref[...]
                r = lax.rsqrt(jnp.mean(xv * xv, axis=-1, keepdims=True) + EPS)
                xh = xv * r
                gy = acc * g_ref[...]
                part = jnp.sum(acc * xh, axis=0, keepdims=True)
                acc = r * (gy - xh * jnp.mean(gy * xh, axis=-1, keepdims=True))

                @pl.when(first_rows)
                def _():
                    o2_ref[...] = part

                @pl.when(jnp.logical_not(first_rows))
                def _():
                    o2_ref[...] += part
            if r_ref is not None:
                acc = r_ref[...] + acc
            o_ref[...] = acc.astype(out_dtype)
            if norm_out is not None:
                r = lax.rsqrt(jnp.mean(acc * acc, axis=-1, keepdims=True) + EPS)
                o2_ref[...] = (acc * r * g_ref[...]).astype(BF16)

        def product(a_ref):
            return _dot(a_ref[...].astype(BF16), b_ref[...].astype(BF16), NT if nt else (((1,), (0,)), ((), ())))

        if nk == 1:
            finish(product(a_refs[0]))
            return
        acc_ref = scr[0]
        kk = pl.program_id(2)
        for (s, cnt), a_ref in zip(starts, a_refs):
            if s == 0:
                @pl.when(kk == 0)
                def _():
                    acc_ref[...] = product(a_ref)

                @pl.when(jnp.logical_and(kk > 0, kk < cnt))
                def _():
                    acc_ref[...] += product(a_ref)
            else:
                @pl.when(jnp.logical_and(kk >= s, kk < s + cnt))
                def _():
                    acc_ref[...] += product(a_ref)

        @pl.when(kk == nk - 1)
        def _():
            finish(acc_ref[...])

    def a_spec(s, cnt):
        return pl.BlockSpec((tm, tk), lambda i, j, kk: (i, jnp.clip(kk - s, 0, cnt - 1)))

    in_specs = [a_spec(s, cnt) for s, cnt in starts]
    in_specs.append(pl.BlockSpec((tn, tk), lambda i, j, kk: (j, kk)) if nt else pl.BlockSpec((tk, tn), lambda i, j, kk: (kk, j)))
    args = pieces + [b]
    if res is not None:
        in_specs.append(pl.BlockSpec((tm, tn), lambda i, j, kk: (i, j)))
        args.append(res)
    if dep is not None:
        in_specs.append(pl.BlockSpec((8, LANES), lambda i, j, kk: (0, 0)))
        args.append(dep)
    tile = pl.BlockSpec((tm, tn), lambda i, j, kk: (i, j))
    vec = pl.BlockSpec((1, n), lambda i, j, kk: (0, 0))
    out_specs, out_shape = tile, jax.ShapeDtypeStruct((m, n), out_dtype)
    if norm_out is not None:
        in_specs.append(vec)
        args.append(norm_out)
        out_specs, out_shape = [tile, tile], [out_shape, jax.ShapeDtypeStruct((m, n), BF16)]
    if norm_bwd is not None:
        in_specs += [vec, tile] + ([tile] if has_pre else [])
        args += [norm_bwd[1], norm_bwd[0]] + ([norm_bwd[2]] if has_pre else [])
        out_specs, out_shape = [tile, vec], [out_shape, jax.ShapeDtypeStruct((1, n), F32)]
    return pl.pallas_call(
        body, name=name, grid=(m // tm, n // tn, nk), in_specs=in_specs, out_specs=out_specs, out_shape=out_shape,
        scratch_shapes=[pltpu.VMEM((tm, tn), F32)] if nk > 1 else [],
        compiler_params=(_params("arbitrary", "arbitrary", "arbitrary") if norm_bwd is not None
                         else _params("parallel", "parallel", "arbitrary")),
    )(*args)


def _mm_tn(x, dy, name, out_dtype=BF16, alpha=1.0, dep=None, out_rows=None, row_off=0, into=None,
           tko=1408, tn=1024, tt=2048):
    t, k = x.shape
    n = dy.shape[1]
    tko, tn, tt = _tile(k, tko, LANES), _tile(n, tn, LANES), _tile(t, tt, 8)
    nt_steps = t // tt

    def body(*refs):
        x_ref, dy_ref = refs[:2]
        o_ref, acc_ref = refs[-2:]
        part = _dot(x_ref[...].astype(BF16), dy_ref[...].astype(BF16), TN)
        step = pl.program_id(2)

        @pl.when(step == 0)
        def _():
            acc_ref[...] = part

        @pl.when(step > 0)
        def _():
            acc_ref[...] += part

        @pl.when(step == nt_steps - 1)
        def _():
            acc = acc_ref[...]
            o_ref[...] = (alpha * acc if alpha != 1.0 else acc).astype(out_dtype)

    in_specs = [pl.BlockSpec((tt, tko), lambda i, j, s: (s, i)), pl.BlockSpec((tt, tn), lambda i, j, s: (s, j))]
    args = [x, dy]
    if dep is not None:
        in_specs.append(pl.BlockSpec((8, LANES), lambda i, j, s: (0, 0)))
        args.append(dep)
    aliases = {}
    if into is not None:
        in_specs.append(pl.BlockSpec(memory_space=pl.ANY))
        args.append(into)
        aliases = {len(args) - 1: 0}
    band = row_off // tko
    assert row_off % tko == 0
    return pl.pallas_call(
        body, name=name, grid=(k // tko, n // tn, nt_steps), in_specs=in_specs,
        out_specs=pl.BlockSpec((tko, tn), lambda i, j, s: (i + band, j)),
        out_shape=jax.ShapeDtypeStruct((out_rows or k, n), out_dtype),
        scratch_shapes=[pltpu.VMEM((tko, tn), F32)], input_output_aliases=aliases,
        compiler_params=_params("parallel", "parallel", "arbitrary"),
    )(*args)


def _rms_fwd(x, g, name, tt=512):
    t, d = x.shape
    tt = _tile(t, tt, 8)

    def body(x_ref, g_ref, o_ref):
        xv = x_ref[...]
        r = lax.rsqrt(jnp.mean(xv * xv, axis=-1, keepdims=True) + EPS)
        o_ref[...] = (xv * r * g_ref[...]).astype(BF16)

    return pl.pallas_call(
        body, name=name, grid=(t // tt,),
        in_specs=[pl.BlockSpec((tt, d), lambda i: (i, 0)), pl.BlockSpec((1, d), lambda i: (0, 0))],
        out_specs=pl.BlockSpec((tt, d), lambda i: (i, 0)),
        out_shape=jax.ShapeDtypeStruct((t, d), BF16), compiler_params=_params("parallel"),
    )(x, g)


def _rms_bwd(dn, x, g, name, dres=None, tt=512):
    t, d = x.shape
    tt = _tile(t, tt, 8)

    def body(*refs):
        if dres is None:
            dn_ref, x_ref, g_ref, dx_ref, dg_ref = refs
            r_ref = None
        else:
            dn_ref, x_ref, g_ref, r_ref, dx_ref, dg_ref = refs
        xv, dnv = x_ref[...], dn_ref[...].astype(F32)
        r = lax.rsqrt(jnp.mean(xv * xv, axis=-1, keepdims=True) + EPS)
        xh = xv * r
        gy = dnv * g_ref[...]
        dx = r * (gy - xh * jnp.mean(gy * xh, axis=-1, keepdims=True))
        if r_ref is not None:
            dx = dx + r_ref[...]
        dx_ref[...] = dx
        part = jnp.sum(dnv * xh, axis=0, keepdims=True)

        @pl.when(pl.program_id(0) == 0)
        def _():
            dg_ref[...] = part

        @pl.when(pl.program_id(0) > 0)
        def _():
            dg_ref[...] += part

    row = pl.BlockSpec((tt, d), lambda i: (i, 0))
    vec = pl.BlockSpec((1, d), lambda i: (0, 0))
    in_specs, args = [row, row, vec], [dn, x, g]
    if dres is not None:
        in_specs.append(row)
        args.append(dres)
    return pl.pallas_call(
        body, name=name, grid=(t // tt,), in_specs=in_specs, out_specs=[row, vec],
        out_shape=[jax.ShapeDtypeStruct((t, d), F32), jax.ShapeDtypeStruct((1, d), F32)],
        compiler_params=_params("arbitrary"),
    )(*args)


def _ffn_up(n, w_gu_t, name, tm=1024, tf=1408):
    t, d = n.shape
    f = w_gu_t.shape[0] // 2
    tm, tf = _tile(t, tm, 8), _tile(f, tf, LANES)
    nf = f // tf

    def body(n_ref, wg_ref, wu_ref, g_ref, u_ref, a_ref):
        nv = n_ref[...]
        gate, up = _dot(nv, wg_ref[...], NT), _dot(nv, wu_ref[...], NT)
        s = _sigmoid(gate)
        sg = gate * s
        g_ref[...] = (up * (s * (1.0 + gate * (1.0 - s)))).astype(BF16)
        u_ref[...] = sg.astype(BF16)
        a_ref[...] = (sg * up).astype(BF16)

    blk = pl.BlockSpec((tm, tf), lambda i, j: (i, j))
    out = jax.ShapeDtypeStruct((t, f), BF16)
    return pl.pallas_call(
        body, name=name, grid=(t // tm, nf),
        in_specs=[pl.BlockSpec((tm, d), lambda i, j: (i, 0)), pl.BlockSpec((tf, d), lambda i, j: (j, 0)),
                  pl.BlockSpec((tf, d), lambda i, j: (j + nf, 0))],
        out_specs=[blk, blk, blk], out_shape=[out, out, out], compiler_params=_params("parallel", "parallel"),
    )(n, w_gu_t, w_gu_t)


def _ffn_da(dh, w_d, gate, up, name, alpha, dep=None, tm=1024, tf=1408):
    t, d = dh.shape
    f = w_d.shape[0]
    tm, tf = _tile(t, tm, 8), _tile(f, tf, LANES)

    def body(*refs):
        dh_ref, w_ref, g_ref, u_ref = refs[:4]
        dg_ref, du_ref = refs[-2:]
        da = alpha * _dot(dh_ref[...].astype(BF16), w_ref[...], NT)
        dg_ref[...] = (da * g_ref[...].astype(F32)).astype(BF16)
        du_ref[...] = (da * u_ref[...].astype(F32)).astype(BF16)

    blk = pl.BlockSpec((tm, tf), lambda i, j: (i, j))
    in_specs = [pl.BlockSpec((tm, d), lambda i, j: (i, 0)), pl.BlockSpec((tf, d), lambda i, j: (j, 0)), blk, blk]
    args = [dh, w_d, gate, up]
    if dep is not None:
        in_specs.append(pl.BlockSpec((8, LANES), lambda i, j: (0, 0)))
        args.append(dep)
    out = jax.ShapeDtypeStruct((t, f), BF16)
    return pl.pallas_call(
        body, name=name, grid=(t // tm, f // tf), in_specs=in_specs, out_specs=[blk, blk], out_shape=[out, out],
        compiler_params=_params("parallel", "parallel"),
    )(*args)


def _merge_fwd(ya, yb, proj, ga_blk, gb_blk, d, name, tt=512):
    t = ya.shape[0]
    tt = _tile(t, tt, 8)

    def body(ya_ref, yb_ref, ga_ref, gb_ref, o_ref):
        o_ref[...] = (_sigmoid(ga_ref[...].astype(F32)) * ya_ref[...].astype(F32)
                      + _sigmoid(gb_ref[...].astype(F32)) * yb_ref[...].astype(F32)).astype(BF16)

    row = pl.BlockSpec((tt, d), lambda i: (i, 0))
    return pl.pallas_call(
        body, name=name, grid=(t // tt,),
        in_specs=[row, row, pl.BlockSpec((tt, d), lambda i: (i, ga_blk)), pl.BlockSpec((tt, d), lambda i: (i, gb_blk))],
        out_specs=row, out_shape=jax.ShapeDtypeStruct((t, d), BF16), compiler_params=_params("parallel"),
    )(ya, yb, proj, proj)


def _merge_bwd(dm, ya, yb, proj, ga_blk, gb_blk, d, name, tt=512):
    t = ya.shape[0]
    tt = _tile(t, tt, 8)

    def body(dm_ref, ya_ref, yb_ref, ga_ref, gb_ref, dya_ref, dyb_ref, dga_ref, dgb_ref):
        dmv = dm_ref[...].astype(F32)
        sa, sb = _sigmoid(ga_ref[...].astype(F32)), _sigmoid(gb_ref[...].astype(F32))
        dya_ref[...] = (dmv * sa).astype(BF16)
        dyb_ref[...] = (dmv * sb).astype(BF16)
        dga_ref[...] = (dmv * ya_ref[...].astype(F32) * sa * (1.0 - sa)).astype(BF16)
        dgb_ref[...] = (dmv * yb_ref[...].astype(F32) * sb * (1.0 - sb)).astype(BF16)

    row = pl.BlockSpec((tt, d), lambda i: (i, 0))
    out = jax.ShapeDtypeStruct((t, d), BF16)
    return pl.pallas_call(
        body, name=name, grid=(t // tt,),
        in_specs=[row, row, row, pl.BlockSpec((tt, d), lambda i: (i, ga_blk)), pl.BlockSpec((tt, d), lambda i: (i, gb_blk))],
        out_specs=[row] * 4, out_shape=[out] * 4, compiler_params=_params("parallel"),
    )(dm, ya, yb, proj, proj)


def _loss_head(h, g, target, name, tt=512):
    t, d = h.shape
    tt = _tile(t, tt, 8)

    def body(h_ref, g_ref, tg_ref, loss_ref, dh_ref, dg_ref):
        xv = h_ref[...]
        r = lax.rsqrt(jnp.mean(xv * xv, axis=-1, keepdims=True) + EPS)
        xh = xv * r
        err = xh * g_ref[...] - tg_ref[...]
        dout = err * (1.0 / d)
        gy = dout * g_ref[...]
        dh_ref[...] = r * (gy - xh * jnp.mean(gy * xh, axis=-1, keepdims=True))
        dg_part = jnp.sum(dout * xh, axis=0, keepdims=True)
        loss_part = jnp.full((1, LANES), 0.5 / d, F32) * jnp.sum(err * err)

        @pl.when(pl.program_id(0) == 0)
        def _():
            dg_ref[...] = dg_part
            loss_ref[...] = loss_part

        @pl.when(pl.program_id(0) > 0)
        def _():
            dg_ref[...] += dg_part
            loss_ref[...] += loss_part

    row = pl.BlockSpec((tt, d), lambda i: (i, 0))
    vec = pl.BlockSpec((1, d), lambda i: (0, 0))
    return pl.pallas_call(
        body, name=name, grid=(t // tt,), in_specs=[row, vec, row],
        out_specs=[pl.BlockSpec((1, LANES), lambda i: (0, 0)), row, vec],
        out_shape=[jax.ShapeDtypeStruct((1, LANES), F32), jax.ShapeDtypeStruct((t, d), F32), jax.ShapeDtypeStruct((1, d), F32)],
        compiler_params=_params("arbitrary"),
    )(h, g, target)


def _shift_down(x, k, t_idx):
    if k == 0:
        return x
    return jnp.where(t_idx >= k, pltpu.roll(x, k, 0), 0.0)


def _shift_up(x, k, t_idx, s):
    if k == 0:
        return x
    return jnp.where(t_idx < s - k, pltpu.roll(x, s - k, 0), 0.0)


def _conv_a_fwd(proj, w, nb, s, d, name, cb=256):
    cb = _tile(d, cb, LANES)
    nd = d // cb

    def body(b_ref, c_ref, v_ref, w_ref, o_ref):
        t_idx = lax.broadcasted_iota(jnp.int32, (s, cb), 0)
        cv = c_ref[...].astype(F32) * v_ref[...].astype(F32)
        cc = sum(w_ref[k:k + 1, :] * _shift_down(cv, CONV_A_K - 1 - k, t_idx) for k in range(CONV_A_K))
        o_ref[...] = (b_ref[...].astype(F32) * cc).astype(BF16)

    def col(off):
        return pl.BlockSpec((s, cb), lambda b, j: (b, j + off * nd))

    return pl.pallas_call(
        body, name=name, grid=(nb, nd), in_specs=[col(0), col(1), col(2), pl.BlockSpec((8, cb), lambda b, j: (0, j))],
        out_specs=pl.BlockSpec((s, cb), lambda b, j: (b, j)),
        out_shape=jax.ShapeDtypeStruct((nb * s, d), BF16), compiler_params=_params("parallel", "parallel"),
    )(proj, proj, proj, w)


def _conv_a_bwd(dy, proj, w, nb, s, d, name, cb=256):
    cb = _tile(d, cb, LANES)
    nd = d // cb

    def body(dy_ref, b_ref, c_ref, v_ref, w_ref, db_ref, dc_ref, dv_ref, dw_ref):
        t_idx = lax.broadcasted_iota(jnp.int32, (s, cb), 0)
        cv_c, cv_v = c_ref[...].astype(F32), v_ref[...].astype(F32)
        cv = cv_c * cv_v
        shifted = [_shift_down(cv, CONV_A_K - 1 - k, t_idx) for k in range(CONV_A_K)]
        cc = sum(w_ref[k:k + 1, :] * shifted[k] for k in range(CONV_A_K))
        dyv = dy_ref[...].astype(F32)
        db_ref[...] = (dyv * cc).astype(BF16)
        dcc = dyv * b_ref[...].astype(F32)
        dcv = sum(w_ref[k:k + 1, :] * _shift_up(dcc, CONV_A_K - 1 - k, t_idx, s) for k in range(CONV_A_K))
        dc_ref[...] = (dcv * cv_v).astype(BF16)
        dv_ref[...] = (dcv * cv_c).astype(BF16)
        rows = [jnp.sum(dcc * shifted[k], axis=0, keepdims=True) for k in range(CONV_A_K)]
        part = _stack_rows(rows, cb)

        @pl.when(pl.program_id(1) == 0)
        def _():
            dw_ref[...] = part

        @pl.when(pl.program_id(1) > 0)
        def _():
            dw_ref[...] += part

    def col(off):
        return pl.BlockSpec((s, cb), lambda j, b: (b, j + off * nd))

    own = pl.BlockSpec((s, cb), lambda j, b: (b, j))
    wspec = pl.BlockSpec((8, cb), lambda j, b: (0, j))
    out = jax.ShapeDtypeStruct((nb * s, d), BF16)
    return pl.pallas_call(
        body, name=name, grid=(nd, nb), in_specs=[own, col(0), col(1), col(2), wspec],
        out_specs=[own, own, own, wspec], out_shape=[out, out, out, jax.ShapeDtypeStruct((8, d), F32)],
        compiler_params=_params("parallel", "arbitrary"),
    )(dy, proj, proj, proj, w)


def _conv_s_fwd(proj, col0, w, bias, nb, s, cc_width, name, cb=256):
    cb = _tile(math.gcd(cc_width, col0) if col0 else cc_width, cb, LANES)
    nd, off = cc_width // cb, col0 // cb

    def body(x_ref, w_ref, b_ref, o_ref, pre_ref):
        t_idx = lax.broadcasted_iota(jnp.int32, (s, cb), 0)
        xv = x_ref[...].astype(F32)
        pre = b_ref[...] + sum(w_ref[k:k + 1, :] * _shift_down(xv, SSM_CONV_K - 1 - k, t_idx) for k in range(SSM_CONV_K))
        o_ref[...] = _silu(pre).astype(BF16)
        pre_ref[...] = pre.astype(BF16)

    vec = pl.BlockSpec((8, cb), lambda b, j: (0, j))
    own = pl.BlockSpec((s, cb), lambda b, j: (b, j))
    out = jax.ShapeDtypeStruct((nb * s, cc_width), BF16)
    return pl.pallas_call(
        body, name=name, grid=(nb, nd),
        in_specs=[pl.BlockSpec((s, cb), lambda b, j: (b, j + off)), vec, pl.BlockSpec((1, cb), lambda b, j: (0, j))],
        out_specs=[own, own], out_shape=[out, out], compiler_params=_params("parallel", "parallel"),
    )(proj, w, bias)


def _conv_s_bwd(dxc, pre, proj, col0, w, nb, s, cc_width, name, cb=256):
    cb = _tile(math.gcd(cc_width, col0) if col0 else cc_width, cb, LANES)
    nd, off = cc_width // cb, col0 // cb

    def body(d_ref, pre_ref, x_ref, w_ref, dx_ref, dw_ref, db_ref):
        t_idx = lax.broadcasted_iota(jnp.int32, (s, cb), 0)
        xv = x_ref[...].astype(F32)
        dpre = d_ref[...].astype(F32) * _dsilu(pre_ref[...].astype(F32))
        ahead = [_shift_up(dpre, j, t_idx, s) for j in range(SSM_CONV_K)]
        dx_ref[...] = sum(w_ref[k:k + 1, :] * ahead[SSM_CONV_K - 1 - k] for k in range(SSM_CONV_K)).astype(BF16)
        rows = [jnp.sum(ahead[SSM_CONV_K - 1 - k] * xv, axis=0, keepdims=True) for k in range(SSM_CONV_K)]
        dw_part = _stack_rows(rows, cb)
        db_part = jnp.sum(dpre, axis=0, keepdims=True)

        @pl.when(pl.program_id(1) == 0)
        def _():
            dw_ref[...] = dw_part
            db_ref[...] = db_part

        @pl.when(pl.program_id(1) > 0)
        def _():
            dw_ref[...] += dw_part
            db_ref[...] += db_part

    own = pl.BlockSpec((s, cb), lambda j, b: (b, j))
    wspec = pl.BlockSpec((8, cb), lambda j, b: (0, j))
    bspec = pl.BlockSpec((1, cb), lambda j, b: (0, j))
    return pl.pallas_call(
        body, name=name, grid=(nd, nb),
        in_specs=[own, own, pl.BlockSpec((s, cb), lambda j, b: (b, j + off)), wspec],
        out_specs=[own, wspec, bspec],
        out_shape=[jax.ShapeDtypeStruct((nb * s, cc_width), BF16), jax.ShapeDtypeStruct((8, cc_width), F32),
                   jax.ShapeDtypeStruct((1, cc_width), F32)],
        compiler_params=_params("parallel", "arbitrary"),
    )(dxc, pre, proj, w)


def _split3(v):
    hi = v.astype(BF16)
    r1 = v - hi.astype(F32)
    mid = r1.astype(BF16)
    return hi, mid, (r1 - mid.astype(F32)).astype(BF16)


def _exact_left(mask_b, v):
    return sum(_dot(mask_b, t) for t in _split3(v))


def _exact_right(v, mask_b):
    return sum(_dot(t, mask_b) for t in _split3(v))


def _head_sums(v, e_b):
    return _dot(v.astype(BF16), e_b, NT)


def _spread(v, out_ref, di):
    lane = lax.broadcasted_iota(jnp.int32, (v.shape[0], LANES), 1)
    for pr in range(di // LANES):
        h0 = pr * (LANES // SSM_HEAD_DIM)
        out_ref[:, pr * LANES:(pr + 1) * LANES] = jnp.where(lane < SSM_HEAD_DIM, v[:, h0:h0 + 1], v[:, h0 + 1:h0 + 2])


def _ssd_common(xc_ref, dtr_ref, dtrt_ref, prow_ref, pcol_ref, dtx_ref, acsx_ref, dx_ref, di):
    l = SSM_CHUNK
    bias_r, a_r = prow_ref[0:1, :], -jnp.exp(prow_ref[1:2, :])
    sp_in = dtr_ref[...] + bias_r
    dt = _softplus(sp_in)
    li = lax.broadcasted_iota(jnp.int32, (l, l), 0)
    si = lax.broadcasted_iota(jnp.int32, (l, l), 1)
    lower_b = (li >= si).astype(BF16)
    upper_b = (li <= si).astype(BF16)
    acs = _exact_left(lower_b, dt * a_r)
    bias_c, a_c = pcol_ref[:, 0:1], -jnp.exp(pcol_ref[:, 1:2])
    dt_t = _softplus(dtrt_ref[...] + bias_c)
    acs_t = _exact_right(dt_t * a_c, upper_b)
    _spread(dt, dtx_ref, di)
    _spread(acs, acsx_ref, di)
    _spread(prow_ref[0:8, :], dx_ref, di)
    acs_exp = acsx_ref[...]
    acs_last = acs_exp[l - 1:l, :]
    x = xc_ref[:, 0:di].astype(F32)
    return dict(dt=dt, a_r=a_r, sp_in=sp_in, acs=acs, acs_t=acs_t, dt_exp=dtx_ref[...], e_exp=jnp.exp(acs_exp),
                el_exp=jnp.exp(acs_last), f_exp=jnp.exp(acs_last - acs_exp), x=x, mask=li >= si, upper_b=upper_b,
                d_exp=dx_ref[2:3, :])


def _decay(q, h):
    seg = q["acs"][:, h:h + 1] - q["acs_t"][h:h + 1, :]
    return jnp.exp(jnp.where(q["mask"], seg, NEG_BIG))


def _ssd_fwd(xc, dtr, dtrt, prow, pcol, nb, nc, di, name):
    l, n, g_n, p = SSM_CHUNK, SSM_STATE, SSM_GROUPS, SSM_HEAD_DIM
    cc = xc.shape[1]
    gw = di // g_n
    assert p * 2 == LANES and gw % LANES == 0

    def body(xc_ref, dtr_ref, dtrt_ref, prow_ref, pcol_ref, y_ref, sprev_ref, st_ref, dtx_ref, acsx_ref, dx_ref):
        @pl.when(pl.program_id(1) == 0)
        def _():
            st_ref[...] = jnp.zeros_like(st_ref)

        q = _ssd_common(xc_ref, dtr_ref, dtrt_ref, prow_ref, pcol_ref, dtx_ref, acsx_ref, dx_ref, di)
        x = q["x"]
        xd = x * q["dt_exp"]
        xdb = xd.astype(BF16)
        xdf = (xd * q["f_exp"]).astype(BF16)
        lane = lax.broadcasted_iota(jnp.int32, (l, LANES), 1)
        for g in range(g_n):
            lo = g * gw
            bg = xc_ref[:, di + g * n: di + (g + 1) * n]
            cg = xc_ref[:, di + g_n * n + g * n: di + g_n * n + (g + 1) * n]
            cb = _dot(cg, bg, NT)
            st_g = st_ref[:, lo:lo + gw]
            y_off = q["e_exp"][:, lo:lo + gw] * _dot(cg, st_g.astype(BF16))
            for pr in range(gw // LANES):
                c0 = lo + pr * LANES
                h0 = c0 // p
                xp = xdb[:, c0:c0 + LANES]
                m0 = (cb * _decay(q, h0)).astype(BF16)
                m1 = (cb * _decay(q, h0 + 1)).astype(BF16)
                yd = _dot(m0, jnp.where(lane < p, xp, 0)) + _dot(m1, jnp.where(lane >= p, xp, 0))
                y_ref[:, c0:c0 + LANES] = (yd + y_off[:, pr * LANES:(pr + 1) * LANES]
                                           + q["d_exp"][:, c0:c0 + LANES] * x[:, c0:c0 + LANES])
            sprev_ref[:, lo:lo + gw] = st_g
            st_ref[:, lo:lo + gw] = q["el_exp"][:, lo:lo + gw] * st_g + _dot(bg, xdf[:, lo:lo + gw], TN)

    tok = lambda w: pl.BlockSpec((l, w), lambda b, c: (b * nc + c, 0))
    const = lambda r, w: pl.BlockSpec((r, w), lambda b, c: (0, 0))
    return pl.pallas_call(
        body, name=name, grid=(nb, nc),
        in_specs=[tok(cc), tok(LANES), pl.BlockSpec((LANES, l), lambda b, c: (0, b * nc + c)),
                  const(8, LANES), const(LANES, 8)],
        out_specs=[tok(di), pl.BlockSpec((None, n, di), lambda b, c: (b * nc + c, 0, 0))],
        out_shape=[jax.ShapeDtypeStruct((nb * nc * l, di), F32), jax.ShapeDtypeStruct((nb * nc, n, di), F32)],
        scratch_shapes=[pltpu.VMEM((n, di), F32), pltpu.VMEM((l, di), F32), pltpu.VMEM((l, di), F32),
                        pltpu.VMEM((8, di), F32)],
        compiler_params=_params("parallel", "arbitrary"),
    )(xc, dtr, dtrt, prow, pcol)


def _ssd_bwd(dy, xc, dtr, dtrt, prow, pcol, e_mat, sprev, nb, nc, di, name):
    l, n, g_n, p = SSM_CHUNK, SSM_STATE, SSM_GROUPS, SSM_HEAD_DIM
    cc = xc.shape[1]
    gw = di // g_n

    def body(dy_ref, xc_ref, dtr_ref, dtrt_ref, prow_ref, pcol_ref, e_ref, sprev_ref,
             dxc_ref, ddtr_ref, sums_ref, dst_ref, off_ref, dxd_ref, last_ref, vst_ref,
             dtx_ref, acsx_ref, dx_ref):
        first = jnp.logical_and(pl.program_id(0) == 0, pl.program_id(1) == 0)

        @pl.when(pl.program_id(1) == 0)
        def _():
            dst_ref[...] = jnp.zeros_like(dst_ref)

        head_row = lax.broadcasted_iota(jnp.int32, (LANES, l), 0)
        row_sums, col_sums = jnp.zeros((l, LANES), F32), jnp.zeros((LANES, l), F32)
        strict_lower = lax.broadcasted_iota(jnp.int32, (l, l), 0) > lax.broadcasted_iota(jnp.int32, (l, l), 1)

        q = _ssd_common(xc_ref, dtr_ref, dtrt_ref, prow_ref, pcol_ref, dtx_ref, acsx_ref, dx_ref, di)
        x = q["x"]
        xd = x * q["dt_exp"]
        xdb = xd.astype(BF16)
        xdf = (xd * q["f_exp"]).astype(BF16)
        dyv = dy_ref[...]
        dyb = dyv.astype(BF16)
        dye = (dyv * q["e_exp"]).astype(BF16)
        upper_b = q["upper_b"]
        lane = lax.broadcasted_iota(jnp.int32, (l, LANES), 1)
        for g in range(g_n):
            lo = g * gw
            bg = xc_ref[:, di + g * n: di + (g + 1) * n]
            cg = xc_ref[:, di + g_n * n + g * n: di + g_n * n + (g + 1) * n]
            cb = _dot(cg, bg, NT)
            st_g = sprev_ref[:, lo:lo + gw]
            st_gb = st_g.astype(BF16)
            dst_g = dst_ref[:, lo:lo + gw]
            dst_gb = dst_g.astype(BF16)
            dye_g = dye[:, lo:lo + gw]
            xdf_g = xdf[:, lo:lo + gw]
            y_off = q["e_exp"][:, lo:lo + gw] * _dot(cg, st_gb)
            dc_g = _dot(dye_g, st_gb, NT)
            db_g = _dot(xdf_g, dst_gb, NT)
            dxd_state = _dot(bg, dst_gb) * q["f_exp"][:, lo:lo + gw]
            last_ref[:, lo:lo + gw] = jnp.sum(dst_g * st_g, axis=0, keepdims=True)
            dst_ref[:, lo:lo + gw] = q["el_exp"][:, lo:lo + gw] * dst_g + _dot(cg, dye_g, TN)
            off_ref[:, lo:lo + gw] = dyv[:, lo:lo + gw] * y_off
            vst_ref[:, lo:lo + gw] = xd[:, lo:lo + gw] * dxd_state
            dcb = jnp.zeros((l, l), F32)
            for pr in range(gw // LANES):
                c0 = lo + pr * LANES
                h0 = c0 // p
                xp = xdb[:, c0:c0 + LANES]
                dyp = dyb[:, c0:c0 + LANES]
                dxd_diag = jnp.zeros((l, LANES), F32)
                for k, keep in enumerate((lane < p, lane >= p)):
                    dec = _decay(q, h0 + k)
                    dy_h = jnp.where(keep, dyp, 0)
                    dm_dec = _dot(dy_h, xp, NT) * dec
                    dcb = dcb + dm_dec
                    dxd_diag = dxd_diag + _dot((cb * dec).astype(BF16), dy_h, TN)
                    qm = dm_dec * cb
                    row_sums = jnp.where(lane == h0 + k, jnp.sum(qm, axis=1, keepdims=True), row_sums)
                    col_sums = jnp.where(head_row == h0 + k, jnp.sum(qm, axis=0, keepdims=True), col_sums)
                dxd_ref[:, c0:c0 + LANES] = dxd_diag + dxd_state[:, pr * LANES:(pr + 1) * LANES]
            dcb_b = dcb.astype(BF16)
            dxc_ref[:, di + g * n: di + (g + 1) * n] = (db_g + _dot(dcb_b, cg, TN)).astype(BF16)
            dxc_ref[:, di + g_n * n + g * n: di + g_n * n + (g + 1) * n] = (dc_g + _dot(dcb_b, bg)).astype(BF16)
        dxd = dxd_ref[...]
        e_b = e_ref[...]
        from_y = _exact_left(upper_b, _head_sums(off_ref[...], e_b) + row_sums - col_sums.T)
        from_s = _exact_left(strict_lower.astype(BF16), _head_sums(vst_ref[...], e_b))
        carried = _head_sums(jnp.broadcast_to(last_ref[...], (8, di)), e_b)[0:1, :] * jnp.exp(q["acs"][l - 1:l, :])
        dla = from_y + from_s + carried
        ddt = dla * q["a_r"] + _head_sums(dxd * x, e_b)
        ddtr = ddt * jax.nn.sigmoid(q["sp_in"])
        ddtr_ref[...] = ddtr
        dxc_ref[:, 0:di] = (dxd * q["dt_exp"] + q["d_exp"] * dyv).astype(BF16)
        dd_exp = jnp.sum(dyv * x, axis=0, keepdims=True)
        dd = _head_sums(jnp.broadcast_to(dd_exp, (8, di)), e_b)[0:1, :]
        part = _stack_rows([jnp.sum(ddtr, axis=0, keepdims=True),
                            jnp.sum(dla * q["dt"], axis=0, keepdims=True) * q["a_r"], dd], LANES)

        @pl.when(first)
        def _():
            sums_ref[...] = part

        @pl.when(jnp.logical_not(first))
        def _():
            sums_ref[...] += part

    rev = lambda b, c: b * nc + (nc - 1 - c)
    tok = lambda w: pl.BlockSpec((l, w), lambda b, c: (rev(b, c), 0))
    const = lambda r, w: pl.BlockSpec((r, w), lambda b, c: (0, 0))
    return pl.pallas_call(
        body, name=name, grid=(nb, nc),
        in_specs=[tok(di), tok(cc), tok(LANES), pl.BlockSpec((LANES, l), lambda b, c: (0, rev(b, c))),
                  const(8, LANES), const(LANES, 8), const(LANES, di),
                  pl.BlockSpec((None, n, di), lambda b, c: (rev(b, c), 0, 0))],
        out_specs=[tok(cc), tok(LANES), const(8, LANES)],
        out_shape=[jax.ShapeDtypeStruct((nb * nc * l, cc), BF16), jax.ShapeDtypeStruct((nb * nc * l, LANES), F32),
                   jax.ShapeDtypeStruct((8, LANES), F32)],
        scratch_shapes=[pltpu.VMEM((n, di), F32), pltpu.VMEM((l, di), F32), pltpu.VMEM((l, di), F32),
                        pltpu.VMEM((1, di), F32), pltpu.VMEM((l, di), F32),
                        pltpu.VMEM((l, di), F32), pltpu.VMEM((l, di), F32), pltpu.VMEM((8, di), F32)],
        compiler_params=_params("arbitrary", "arbitrary"),
    )(dy, xc, dtr, dtrt, prow, pcol, e_mat, sprev)


def _gate_norm_fwd(y, proj, z_col0, norm_g, di, name, tt=256):
    t = y.shape[0]
    tt = _tile(t, tt, 8)
    gw = di // SSM_GROUPS
    zw = _tile(math.gcd(di, z_col0), di, LANES)
    nz, zoff = di // zw, z_col0 // zw

    def body(*refs):
        y_ref, z_refs, g_ref, o_ref = refs[0], refs[1:1 + nz], refs[1 + nz], refs[2 + nz]
        for g in range(SSM_GROUPS):
            lo = g * gw
            zv = z_refs[lo // zw][:, lo % zw:lo % zw + gw].astype(F32)
            yg = y_ref[:, lo:lo + gw] * _silu(zv)
            r = lax.rsqrt(jnp.mean(yg * yg, axis=-1, keepdims=True) + EPS)
            o_ref[:, lo:lo + gw] = (yg * r * g_ref[:, lo:lo + gw]).astype(BF16)

    row = pl.BlockSpec((tt, di), lambda i: (i, 0))
    zspecs = [pl.BlockSpec((tt, zw), functools.partial(lambda i, k: (i, zoff + k), k=k)) for k in range(nz)]
    return pl.pallas_call(
        body, name=name, grid=(t // tt,), in_specs=[row] + zspecs + [pl.BlockSpec((1, di), lambda i: (0, 0))],
        out_specs=row, out_shape=jax.ShapeDtypeStruct((t, di), BF16), compiler_params=_params("parallel"),
    )(y, *([proj] * nz), norm_g)


def _gate_norm_bwd(dn, y, proj, z_col0, norm_g, di, name, tt=256):
    t = y.shape[0]
    tt = _tile(t, tt, 8)
    gw = di // SSM_GROUPS
    zw = _tile(math.gcd(di, z_col0), di, LANES)
    nz, zoff = di // zw, z_col0 // zw

    def body(*refs):
        dn_ref, y_ref, z_refs, g_ref = refs[0], refs[1], refs[2:2 + nz], refs[2 + nz]
        dy_ref, dz_ref, dg_ref = refs[3 + nz:]
        first = pl.program_id(0) == 0
        for g in range(SSM_GROUPS):
            lo = g * gw
            zv = z_refs[lo // zw][:, lo % zw:lo % zw + gw].astype(F32)
            yv = y_ref[:, lo:lo + gw]
            sz = _silu(zv)
            yg = yv * sz
            r = lax.rsqrt(jnp.mean(yg * yg, axis=-1, keepdims=True) + EPS)
            yh = yg * r
            dnv = dn_ref[:, lo:lo + gw].astype(F32)
            gy = dnv * g_ref[:, lo:lo + gw]
            dyg = r * (gy - yh * jnp.mean(gy * yh, axis=-1, keepdims=True))
            dy_ref[:, lo:lo + gw] = dyg * sz
            dz_ref[:, lo:lo + gw] = (dyg * yv * _dsilu(zv)).astype(BF16)
            part = jnp.sum(dnv * yh, axis=0, keepdims=True)

            @pl.when(first)
            def _():
                dg_ref[:, lo:lo + gw] = part

            @pl.when(jnp.logical_not(first))
            def _():
                dg_ref[:, lo:lo + gw] += part

    row = pl.BlockSpec((tt, di), lambda i: (i, 0))
    vec = pl.BlockSpec((1, di), lambda i: (0, 0))
    zspecs = [pl.BlockSpec((tt, zw), functools.partial(lambda i, k: (i, zoff + k), k=k)) for k in range(nz)]
    return pl.pallas_call(
        body, name=name, grid=(t // tt,), in_specs=[row, row] + zspecs + [vec], out_specs=[row, row, vec],
        out_shape=[jax.ShapeDtypeStruct((t, di), F32), jax.ShapeDtypeStruct((t, di), BF16), jax.ShapeDtypeStruct((1, di), F32)],
        compiler_params=_params("arbitrary"),
    )(dn, y, *([proj] * nz), norm_g)


def _softmax_rows(s):
    s = s - jnp.max(s, axis=-1, keepdims=True)
    e = jnp.exp(s)
    return e * (1.0 / jnp.sum(e, axis=-1, keepdims=True))


def _xattn_fwd(q, kv, nb, s, m, d, name, tq=512):
    tq = _tile(s, tq, 8)
    nq = s // tq
    hd = d // XATTN_HEADS
    scale = 1.0 / math.sqrt(hd)

    def body(q_ref, k_ref, v_ref, o_ref):
        for h in range(XATTN_HEADS):
            sl = slice(h * hd, (h + 1) * hd)
            prob = _softmax_rows(_dot(q_ref[:, sl], k_ref[:, sl], NT) * scale)
            o_ref[:, sl] = _dot(prob.astype(BF16), v_ref[:, sl]).astype(BF16)

    return pl.pallas_call(
        body, name=name, grid=(nb, nq),
        in_specs=[pl.BlockSpec((tq, d), lambda b, i: (b * nq + i, 0)), pl.BlockSpec((m, d), lambda b, i: (b, 0)),
                  pl.BlockSpec((m, d), lambda b, i: (b, 1))],
        out_specs=pl.BlockSpec((tq, d), lambda b, i: (b * nq + i, 0)),
        out_shape=jax.ShapeDtypeStruct((nb * s, d), BF16), compiler_params=_params("parallel", "parallel"),
    )(q, kv, kv)


def _xattn_bwd(do, q, kv, nb, s, m, d, name, tq=512):
    tq = _tile(s, tq, 8)
    nq = s // tq
    hd = d // XATTN_HEADS
    scale = 1.0 / math.sqrt(hd)

    def body(do_ref, q_ref, k_ref, v_ref, dq_ref, dk_ref, dv_ref):
        first = pl.program_id(1) == 0
        for h in range(XATTN_HEADS):
            sl = slice(h * hd, (h + 1) * hd)
            qh, kh, vh, doh = q_ref[:, sl], k_ref[:, sl], v_ref[:, sl], do_ref[:, sl]
            prob = _softmax_rows(_dot(qh, kh, NT) * scale)
            dv_h = _dot(prob.astype(BF16), doh, TN)
            dp = _dot(doh, vh, NT)
            ds = (prob * (dp - jnp.sum(dp * prob, axis=-1, keepdims=True)) * scale).astype(BF16)
            dq_ref[:, sl] = _dot(ds, kh).astype(BF16)
            dk_h = _dot(ds, qh, TN)

            @pl.when(first)
            def _():
                dk_ref[:, sl] = dk_h
                dv_ref[:, sl] = dv_h

            @pl.when(jnp.logical_not(first))
            def _():
                dk_ref[:, sl] += dk_h
                dv_ref[:, sl] += dv_h

    qspec = pl.BlockSpec((tq, d), lambda b, i: (b * nq + i, 0))
    dq, dk, dv = pl.pallas_call(
        body, name=name, grid=(nb, nq),
        in_specs=[qspec, qspec, pl.BlockSpec((m, d), lambda b, i: (b, 0)), pl.BlockSpec((m, d), lambda b, i: (b, 1))],
        out_specs=[qspec, pl.BlockSpec((m, d), lambda b, i: (b, 0)), pl.BlockSpec((m, d), lambda b, i: (b, 0))],
        out_shape=[jax.ShapeDtypeStruct((nb * s, d), BF16), jax.ShapeDtypeStruct((nb * m, d), F32),
                   jax.ShapeDtypeStruct((nb * m, d), F32)],
        compiler_params=_params("parallel", "arbitrary"),
    )(do, q, kv, kv)
    return dq, dk, dv


def _all_gather(shards, name):
    n_arr = len(shards)

    def body(*refs):
        x_refs, out_refs = refs[:n_arr], refs[n_arr:2 * n_arr]
        send_sems, recv_sems, local_sems = refs[2 * n_arr:]
        x, y, c = lax.axis_index("x"), lax.axis_index("y"), lax.axis_index("c")
        me, sibling = (x, y, c), (x, y, 1 - c)
        chips = [(1 - x, y), (x, 1 - y), (1 - x, 1 - y)]

        def copy(w, k, block, to, from_input=False):
            px, py, pc = block
            rows = out_refs[w].at[4 * px + 2 * py + pc]
            return pltpu.make_async_remote_copy(
                src_ref=x_refs[w] if from_input else rows, dst_ref=rows,
                send_sem=send_sems.at[7 * w + k], recv_sem=recv_sems.at[7 * w + k], device_id=to, device_id_type=MESH)

        started = []
        for w in range(n_arr):
            mine = pltpu.make_async_copy(x_refs[w], out_refs[w].at[4 * x + 2 * y + c], local_sems.at[w])
            mine.start()
            started.append(mine)
        sends = []
        for w in range(n_arr):
            sends.append(copy(w, 0, me, sibling, from_input=True))
            sends += [copy(w, 1 + j, me, (*chip, c), from_input=True) for j, chip in enumerate(chips)]
        for cp in sends:
            cp.start()
        for j, chip in enumerate(chips):
            for w in range(n_arr):
                copy(w, 1 + j, (*chip, c), me).wait_recv()
                passed = copy(w, 4 + j, (*chip, c), sibling)
                passed.start()
                sends.append(passed)
        for w in range(n_arr):
            copy(w, 0, sibling, me).wait_recv()
            for j, chip in enumerate(chips):
                copy(w, 4 + j, (*chip, 1 - c), me).wait_recv()
        for cp in sends:
            cp.wait_send()
        for mine in started:
            mine.wait()

    hbm = pl.BlockSpec(memory_space=pl.ANY)
    return pl.pallas_call(
        body, name=name, out_shape=[jax.ShapeDtypeStruct((N_DEV,) + s.shape, s.dtype) for s in shards],
        in_specs=[hbm] * n_arr, out_specs=[hbm] * n_arr,
        scratch_shapes=[pltpu.SemaphoreType.DMA((7 * n_arr,)), pltpu.SemaphoreType.DMA((7 * n_arr,)),
                        pltpu.SemaphoreType.DMA((n_arr,))],
    )(*shards)


_HBM = pl.BlockSpec(memory_space=pltpu.HBM)
_SEM = pl.BlockSpec(memory_space=pltpu.SEMAPHORE)
_DATAFLOW = pltpu.SideEffectType.DATAFLOW_SIDE_EFFECTING


def _peer_list(x, y, c):
    return [(1 - x if k & 4 else x, 1 - y if k & 2 else y, 1 - c if k & 1 else c) for k in range(1, N_DEV)]


def _push_copy(src_ref, land_ref, send_sems, recv_sems, w, k, peer, me, per_peer_src, receiving):
    px, py, pc = peer
    peer_slot = 4 * px + 2 * py + pc
    return pltpu.make_async_remote_copy(
        src_ref=src_ref.at[peer_slot] if per_peer_src else src_ref,
        dst_ref=land_ref.at[peer_slot if receiving else me],
        send_sem=send_sems.at[7 * w + k], recv_sem=recv_sems.at[7 * w + k], device_id=peer, device_id_type=MESH)


def _push_start(srcs, per_peer_src, after, name):
    n_arr = len(srcs)
    land_shapes = [s.shape if per_peer_src else (N_DEV,) + s.shape for s in srcs]

    def body(*refs):
        src_refs, land_refs = refs[:n_arr], refs[n_arr:2 * n_arr]
        send_sems, recv_sems = refs[2 * n_arr + 1], refs[2 * n_arr + 2]
        token = refs[-1]
        x, y, c = lax.axis_index("x"), lax.axis_index("y"), lax.axis_index("c")
        me = 4 * x + 2 * y + c
        for w in range(n_arr):
            for k, peer in enumerate(_peer_list(x, y, c)):
                _push_copy(src_refs[w], land_refs[w], send_sems, recv_sems, w, k, peer, me, per_peer_src, False).start()
        token[...] = jnp.zeros_like(token)

    lands = [pltpu.with_memory_space_constraint(lax.empty(ls, s.dtype), pltpu.HBM) for ls, s in zip(land_shapes, srcs)]
    srcs_hbm = [pltpu.with_memory_space_constraint(s, pltpu.HBM) for s in srcs]
    out = pl.pallas_call(
        body, name=name,
        out_shape=(pltpu.SemaphoreType.DMA((7 * n_arr,)), pltpu.SemaphoreType.DMA((7 * n_arr,)),
                   *[pltpu.HBM(s.shape, s.dtype) for s in srcs], *[pltpu.HBM(ls, s.dtype) for ls, s in zip(land_shapes, srcs)],
                   jax.ShapeDtypeStruct((8, LANES), F32)),
        in_specs=[_HBM] * (2 * n_arr) + [pl.BlockSpec(memory_space=pl.ANY)],
        out_specs=(_SEM, _SEM, *([_HBM] * (2 * n_arr)), pl.BlockSpec(memory_space=pltpu.VMEM)),
        input_output_aliases={i: 2 + i for i in range(2 * n_arr)},
        compiler_params=pltpu.CompilerParams(has_side_effects=_DATAFLOW),
    )(*srcs_hbm, *lands, after)
    return dict(send=out[0], recv=out[1], srcs=list(out[2:2 + n_arr]), lands=list(out[2 + n_arr:2 + 2 * n_arr]),
                token=out[-1])


def _push_wait(pending, per_peer_src, after, name):
    n_arr = len(pending["srcs"])

    def body(*refs):
        src_refs, land_refs = refs[:n_arr], refs[n_arr:2 * n_arr]
        send_sems, recv_sems = refs[2 * n_arr], refs[2 * n_arr + 1]
        x, y, c = lax.axis_index("x"), lax.axis_index("y"), lax.axis_index("c")
        me = 4 * x + 2 * y + c
        for w in range(n_arr):
            for k, peer in enumerate(_peer_list(x, y, c)):
                cp = _push_copy(src_refs[w], land_refs[w], send_sems, recv_sems, w, k, peer, me, per_peer_src, True)
                cp.wait_send()
                cp.wait_recv()

    out = pl.pallas_call(
        body, name=name,
        out_shape=tuple(pltpu.HBM(a.shape, a.dtype) for a in pending["srcs"] + pending["lands"]),
        in_specs=[_HBM] * (2 * n_arr) + [_SEM, _SEM, pl.BlockSpec(memory_space=pl.ANY)],
        out_specs=tuple([_HBM] * (2 * n_arr)),
        input_output_aliases={i: i for i in range(2 * n_arr)},
        compiler_params=pltpu.CompilerParams(has_side_effects=_DATAFLOW),
    )(*pending["srcs"], *pending["lands"], pending["send"], pending["recv"], after)
    return list(out[:n_arr]), list(out[n_arr:])


def _adamw_math(w, g, m, v):
    m = ADAM_B1 * m + (1.0 - ADAM_B1) * g
    v = ADAM_B2 * v + (1.0 - ADAM_B2) * (g * g)
    m_hat = m / (1.0 - ADAM_B1 ** ADAM_STEP)
    v_hat = v / (1.0 - ADAM_B2 ** ADAM_STEP)
    delta = -ADAM_LR * (m_hat / (jnp.sqrt(v_hat) + ADAM_EPS) + ADAM_WD * w)
    return delta, m, v


def _sum8(parts, name, tr=512):
    _, r, c_dim = parts.shape
    tr = _tile(r, tr, BF16_SUBLANES)

    def body(p_ref, o_ref):
        acc = p_ref[0].astype(F32)
        for k in range(1, N_DEV):
            acc = acc + p_ref[k].astype(F32)
        o_ref[...] = acc

    return pl.pallas_call(
        body, name=name, grid=(r // tr,), in_specs=[pl.BlockSpec((N_DEV, tr, c_dim), lambda i: (0, i, 0))],
        out_specs=pl.BlockSpec((tr, c_dim), lambda i: (i, 0)),
        out_shape=jax.ShapeDtypeStruct((r, c_dim), F32), compiler_params=_params("parallel"),
    )(parts)


def _sum8_adamw(parts, w, m, v, name, tr=128):
    _, r, c_dim = parts.shape
    tr = _tile(r, tr, BF16_SUBLANES)
    tc = c_dim if tr <= 2 * LANES else _tile(c_dim, LANES, LANES)

    def body(p_ref, w_ref, m_ref, v_ref, g_ref, d_ref, nm_ref, nv_ref):
        g = p_ref[0].astype(F32)
        for k in range(1, N_DEV):
            g = g + p_ref[k].astype(F32)
        g_ref[...] = g
        d_ref[...], nm_ref[...], nv_ref[...] = _adamw_math(w_ref[...], g, m_ref[...], v_ref[...])

    blk = pl.BlockSpec((None, tr, tc), lambda i, j: (0, i, j))
    out = jax.ShapeDtypeStruct((1, r, c_dim), F32)
    return pl.pallas_call(
        body, name=name, grid=(r // tr, c_dim // tc),
        in_specs=[pl.BlockSpec((N_DEV, tr, tc), lambda i, j: (0, i, j)), blk, blk, blk],
        out_specs=[blk] * 4, out_shape=[out] * 4, compiler_params=_params("parallel", "parallel"),
    )(parts, w, m, v)


def _adamw(g, w, m, v, name):
    r, c_dim = g.shape

    def body(g_ref, w_ref, m_ref, v_ref, d_ref, nm_ref, nv_ref):
        d_ref[...], nm_ref[...], nv_ref[...] = _adamw_math(w_ref[...], g_ref[...], m_ref[...], v_ref[...])

    out = jax.ShapeDtypeStruct((r, c_dim), F32)
    return pl.pallas_call(body, name=name, out_shape=[out] * 3)(g, w, m, v)


def _pack_rows(arrays, dtype, row_unit):
    chunks, offs, r0 = [], [], 0
    for a in arrays:
        flat = a.reshape(-1).astype(dtype)
        rows = -(-flat.shape[0] // (LANES * row_unit)) * row_unit
        flat = jnp.pad(flat, (0, rows * LANES - flat.shape[0]))
        chunks.append(flat.reshape(rows, LANES))
        offs.append((r0, rows))
        r0 += rows
    return jnp.concatenate(chunks, axis=0), offs


def _unpack_rows(packed, offs, shapes):
    out = []
    for (r0, rows), shape in zip(offs, shapes):
        n = math.prod(shape)
        blk = packed[..., r0:r0 + rows, :]
        blk = blk.reshape(packed.shape[:-2] + (rows * LANES,))[..., :n]
        out.append(blk.reshape(packed.shape[:-2] + tuple(shape)))
    return out


def _full_from_slots(blk, col_sharded):
    _, r, c = blk.shape
    if col_sharded:
        return blk.transpose(1, 0, 2).reshape(r, N_DEV * c)
    return blk.reshape(N_DEV * r, c)


def _slots_from_full(full, col_sharded):
    r, c = full.shape
    if col_sharded:
        return full.reshape(r, N_DEV, c // N_DEV).transpose(1, 0, 2)
    return full.reshape(N_DEV, r // N_DEV, c)


def _ffn_fwd(h, n, w_gu_t, w_d, tag, next_gain=None):
    gate, up, a = _ffn_up(n, w_gu_t, f"{tag}_up")
    out = _mm(a, w_d, F32, f"{tag}_down", res=h, alpha=FFN_RES_WEIGHT, norm_out=next_gain)
    h_out, n_next = out if next_gain is not None else (out, None)
    return h_out, (h, n, gate, up, a), n_next


def _ffn_bwd(dh_out, saved, g, w_gu_t, w_d, tag, dep, send_grads):
    h, n, gate, up, a = saved
    dw_d = _mm_tn(a, dh_out, f"{tag}_dw_down", alpha=FFN_RES_WEIGHT, dep=dep)
    dgate, dup = _ffn_da(dh_out, w_d, gate, up, f"{tag}_da", FFN_RES_WEIGHT)
    f = dgate.shape[1]
    dw_gu_t = _mm_tn(dgate, n, f"{tag}_dw_gate", out_rows=2 * f)
    dw_gu_t = _mm_tn(dup, n, f"{tag}_dw_up", out_rows=2 * f, row_off=f, into=dw_gu_t)
    dep = send_grads(dw_gu_t, dw_d)
    return _mm([dgate, dup], w_gu_t, F32, f"{tag}_dn", dep=dep, tm=512, res=dh_out, norm_bwd=(h, g))


W_GROUPS = (("ffn1_w_gate_up", "ffn1_w_down"),
            ("w_in",),
            ("w_out_a", "w_out_ssm", "w_mix_out"),
            ("w_q", "w_kv", "w_o_x", "ffn2_w_gate_up", "ffn2_w_down"))
G_GROUPS = (("ffn2_w_gate_up", "ffn2_w_down"),
            ("w_o_x", "w_q", "w_kv", "w_mix_out", "w_out_a", "w_out_ssm", "w_in"),
            ("ffn1_w_gate_up", "ffn1_w_down"))


def _local_step(x3, mem3, target3, small, comm):
    nb, s, d = x3.shape
    m_len = mem3.shape[1]
    t = nb * s
    nc = s // SSM_CHUNK
    di = small["ssm_norm"].shape[1]
    hs = di // SSM_HEAD_DIM
    cc = di + 2 * SSM_GROUPS * SSM_STATE
    x, mem, target = x3.reshape(t, d), mem3.reshape(nb * m_len, d), target3.reshape(t, d)

    sizes = (d, d, d, di, cc, hs, d, d)
    offs = [0]
    for sz in sizes:
        offs.append(offs[-1] + sz)
    z_col0, xbc_col0 = 3 * d, 3 * d + di
    ga_blk, gb_blk = (3 * d + di + cc) // d, (4 * d + di + cc) // d

    pad_vec = lambda v: jnp.pad(v.reshape(1, -1), ((0, 0), (0, LANES - hs)))
    prow = jnp.concatenate([pad_vec(small["ssm_dt_bias"]), pad_vec(small["ssm_a_log"]), pad_vec(small["ssm_d"]),
                            jnp.zeros((5, LANES), F32)], axis=0)
    pcol = prow.T
    e_mat = (lax.broadcasted_iota(jnp.int32, (LANES, di), 0)
             == lax.broadcasted_iota(jnp.int32, (LANES, di), 1) // SSM_HEAD_DIM).astype(BF16)
    conv_a_w8 = jnp.pad(small["conv_a_w"][0], ((0, 8 - CONV_A_K), (0, 0)))
    ssm_conv_w8 = jnp.pad(small["ssm_conv_w"][0], ((0, 8 - SSM_CONV_K), (0, 0)))

    wts, dep = comm.weights(0, None)
    n1 = _rms_fwd(x, small["ffn1_norm"] + dep[0, 0], "ffn1_norm")
    h1, ffn1_saved, u = _ffn_fwd(x, n1, wts["ffn1_w_gate_up"], wts["ffn1_w_down"], "ffn1", small["mix_norm"])
    got, dep = comm.weights(1, h1)
    wts.update(got)
    w_in_t = wts["w_in"]
    w_main_t = jnp.concatenate([w_in_t[offs[i]:offs[i + 1]] for i in (0, 1, 2, 3, 4, 6, 7)], axis=0)
    w_dt_t = jnp.pad(w_in_t[offs[5]:offs[6]], ((0, LANES - hs), (0, 0)))
    proj = _mm(u, w_main_t, BF16, "in_proj", nt=True, dep=dep)
    dtr = _mm(u, w_dt_t, F32, "in_proj_dt", nt=True)
    yap = _conv_a_fwd(proj, conv_a_w8, nb, s, d, "conv_a")
    got, dep = comm.weights(2, yap)
    wts.update(got)
    y_a = _mm(yap, wts["w_out_a"], BF16, "out_a", dep=dep)
    xc, conv_pre = _conv_s_fwd(proj, xbc_col0, ssm_conv_w8, small["ssm_conv_b"] + dep[0, 0], nb, s, cc, "conv_s")
    dtrt = dtr.T
    y_ssd, sprev = _ssd_fwd(xc, dtr, dtrt, prow, pcol, nb, nc, di, "ssd")
    ygn = _gate_norm_fwd(y_ssd, proj, z_col0, small["ssm_norm"], di, "gate_norm")
    y_b = _mm(ygn, wts["w_out_ssm"], BF16, "out_ssm")
    merged = _merge_fwd(y_a, y_b, proj, ga_blk, gb_blk, d, "merge")
    h2, un = _mm(merged, wts["w_mix_out"], F32, "mix_out", res=h1, norm_out=small["xattn_norm"])
    got, _ = comm.weights(3, h2)
    wts.update(got)
    mn = _rms_fwd(mem, small["mem_norm"], "mem_norm")
    q = _mm(un, wts["w_q"], BF16, "q_proj")
    kv = _mm(mn, wts["w_kv"], BF16, "kv_proj", nt=True)
    o = _xattn_fwd(q, kv, nb, s, m_len, d, "xattn")
    h3, n2 = _mm(o, wts["w_o_x"], F32, "o_proj", res=h2, norm_out=small["ffn2_norm"])
    h4, ffn2_saved, _ = _ffn_fwd(h3, n2, wts["ffn2_w_gate_up"], wts["ffn2_w_down"], "ffn2")
    loss_vec, dh4, dg_final = _loss_head(h4, small["final_norm"].reshape(1, d), target, "loss_head")

    grads = {"final_norm": dg_final.reshape(d)}
    big = {}
    dh3, grads["ffn2_norm"] = _ffn_bwd(
        dh4, ffn2_saved, small["ffn2_norm"], wts["ffn2_w_gate_up"], wts["ffn2_w_down"], "ffn2", None,
        lambda dw_gu_t, dw_d: comm.grads(0, {"ffn2_w_gate_up": dw_gu_t, "ffn2_w_down": dw_d}))
    big["w_o_x"] = _mm_tn(o, dh3, "dw_o")
    do = _mm(dh3, wts["w_o_x"], BF16, "d_o", nt=True)
    dq, dk, dv = _xattn_bwd(do, q, kv, nb, s, m_len, d, "xattn_bwd")
    big["w_q"] = _mm_tn(un, dq, "dw_q")
    big["w_kv"] = _mm_tn(dv, mn, "dw_v", out_rows=2 * d, row_off=d, into=_mm_tn(dk, mn, "dw_k", out_rows=2 * d))
    _, grads["mem_norm"] = _mm([dk, dv], wts["w_kv"], F32, "d_mn", norm_bwd=(mem, small["mem_norm"]))
    dh2, grads["xattn_norm"] = _mm(dq, wts["w_q"], F32, "d_un", nt=True, res=dh3, norm_bwd=(h2, small["xattn_norm"]))
    big["w_mix_out"] = _mm_tn(merged, dh2, "dw_mix")
    dmerged = _mm(dh2, wts["w_mix_out"], BF16, "d_merged", nt=True)
    dya, dyb, dga, dgb = _merge_bwd(dmerged, y_a, y_b, proj, ga_blk, gb_blk, d, "merge_bwd")
    big["w_out_a"] = _mm_tn(yap, dya, "dw_out_a")
    big["w_out_ssm"] = _mm_tn(ygn, dyb, "dw_out_ssm")
    dyap = _mm(dya, wts["w_out_a"], BF16, "d_yap", nt=True)
    dygn = _mm(dyb, wts["w_out_ssm"], BF16, "d_ygn", nt=True)
    dab, dac, dav, dconv_a = _conv_a_bwd(dyap, proj, conv_a_w8, nb, s, d, "conv_a_bwd")
    dy_ssd, dz, grads["ssm_norm"] = _gate_norm_bwd(dygn, y_ssd, proj, z_col0, small["ssm_norm"], di, "gate_norm_bwd")
    dxc, ddtr, ssd_sums = _ssd_bwd(dy_ssd, xc, dtr, dtrt, prow, pcol, e_mat, sprev, nb, nc, di, "ssd_bwd")
    dxbc, dconv_s, grads["ssm_conv_b"] = _conv_s_bwd(dxc, conv_pre, proj, xbc_col0, ssm_conv_w8, nb, s, cc, "conv_s_bwd")
    dpieces = [("ab", dab), ("ac", dac), ("av", dav), ("z", dz), ("xbc", dxbc), ("ga", dga), ("gb", dgb)]
    dw = {tag: _mm_tn(piece, u, f"dw_in_{tag}") for tag, piece in dpieces}
    dw_dt = _mm_tn(ddtr, u, "dw_in_dt")[:hs]
    du_main = _mm([piece for _, piece in dpieces], w_main_t, F32, "d_u", tk=1024)
    dh1, grads["mix_norm"] = _mm(ddtr, w_dt_t, F32, "d_u_dt", res=dh2, norm_bwd=(h1, small["mix_norm"], du_main))
    big["w_in"] = jnp.concatenate([dw["ab"], dw["ac"], dw["av"], dw["z"], dw["xbc"], dw_dt, dw["ga"], dw["gb"]], axis=0)
    dep = comm.grads(1, big)
    dx, grads["ffn1_norm"] = _ffn_bwd(
        dh1, ffn1_saved, small["ffn1_norm"], wts["ffn1_w_gate_up"], wts["ffn1_w_down"], "ffn1", dep,
        lambda dw_gu_t, dw_d: comm.grads(2, {"ffn1_w_gate_up": dw_gu_t, "ffn1_w_down": dw_d}))

    grads["conv_a_w"] = dconv_a[:CONV_A_K]
    grads["ssm_conv_w"] = dconv_s[:SSM_CONV_K]
    grads["ssm_dt_bias"] = ssd_sums[0:1, :hs]
    grads["ssm_a_log"] = ssd_sums[1:2, :hs]
    grads["ssm_d"] = ssd_sums[2:3, :hs]
    return loss_vec[0, 0], dx.reshape(nb, s, d), grads


def _step(inputs):
    w = {k: inputs[k] for k in WEIGHT_ORDER}
    mom = {k: inputs["m_" + k] for k in WEIGHT_ORDER}
    vel = {k: inputs["v_" + k] for k in WEIGHT_ORDER}
    me = 4 * lax.axis_index("x") + 2 * lax.axis_index("y") + lax.axis_index("c")

    send = {k: (w[k][0].T if k in COL_SHARDED else w[k][0]).astype(BF16) for k in BIG_WEIGHTS}

    def own_slot(land, mine):
        return lax.dynamic_update_slice(land, mine[None], (me, 0, 0))

    gathers, exchanges = {}, {}

    def weights(i, after):
        if i == 0:
            lands = _all_gather([send[k] for k in W_GROUPS[0]], "gather0")
        else:
            sent, lands = _push_wait(gathers[i], False, after, f"gather{i}_wait")
            lands = [own_slot(land, mine) for land, mine in zip(lands, sent)]
        full = {k: land.reshape(N_DEV * land.shape[1], land.shape[2]) for k, land in zip(W_GROUPS[i], lands)}
        dep = jnp.zeros((8, LANES), F32)
        if i + 1 < len(W_GROUPS):
            gathers[i + 1] = _push_start([send[k] for k in W_GROUPS[i + 1]], False, lands[0], f"gather{i + 1}_start")
            dep = gathers[i + 1]["token"]
        return full, dep

    def slot_shape(k):
        rows, cols = send[k].shape
        return (rows * cols // LANES, LANES) if rows % BF16_SUBLANES else (rows, cols)

    def send_grads(i, by_name):
        slots = [by_name[k].reshape((N_DEV,) + slot_shape(k)) for k in G_GROUPS[i]]
        exchanges[i] = _push_start(slots, True, slots[0], f"exchange{i}_start")
        return exchanges[i]["token"]

    comm = types.SimpleNamespace(weights=weights, grads=send_grads)
    small = {k: w[k] for k in SMALL_REPLICATED}
    conv_shapes = [w[k].shape[1:] for k in SMALL_SHARDED]
    packed_c, conv_offs = _pack_rows([w[k][0] for k in SMALL_SHARDED], F32, 8)
    conv_blocks = _unpack_rows(_all_gather([packed_c], "gather_conv_weights")[0], conv_offs, conv_shapes)
    for k, b in zip(SMALL_SHARDED, conv_blocks):
        small[k] = _full_from_slots(b, True)[None]

    loss_local, grad_x, grads = _local_step(inputs["x"], inputs["mem"], inputs["loss_target"], small, comm)
    loss = lax.psum(loss_local, AXES)

    out = {}
    for i, names in enumerate(G_GROUPS):
        sent, lands = _push_wait(exchanges[i], True, grad_x, f"exchange{i}_wait")
        for k, land, slots in zip(names, lands, sent):
            parts = own_slot(land, lax.dynamic_index_in_dim(slots, me, 0, keepdims=False))
            if k in COL_SHARDED and w[k].shape[2] % LANES:
                flip = lambda a: a.transpose(0, 2, 1).reshape((1,) + slot_shape(k))
                unflip = lambda a: a.reshape((1,) + send[k].shape).transpose(0, 2, 1)
                out[k] = tuple(unflip(o) for o in _sum8_adamw(parts, flip(w[k]), flip(mom[k]), flip(vel[k]),
                                                             f"sum_adamw_{k}", tr=2048 if slot_shape(k)[1] == LANES else 128))
                continue
            if k in COL_SHARDED:
                parts = parts.transpose(0, 2, 1)
            out[k] = tuple(_sum8_adamw(parts, w[k], mom[k], vel[k], f"sum_adamw_{k}"))

    small_names = SMALL_REPLICATED + SMALL_SHARDED
    packed_g, small_offs = _pack_rows([grads[k] for k in small_names], F32, 8)
    total = _sum8(_all_gather([packed_g], "gather_small_grads")[0], "sum_small_grads")
    full_grads = _unpack_rows(total, small_offs, [grads[k].shape for k in small_names])
    mine = {}
    for k, g in zip(small_names, full_grads):
        if k in SMALL_SHARDED:
            c_loc = w[k].shape[2]
            g = lax.dynamic_slice_in_dim(g, me * c_loc, c_loc, axis=1)
        mine[k] = g.reshape(w[k].shape)
    sg, s_offs = _pack_rows([mine[k] for k in small_names], F32, 8)
    sw, _ = _pack_rows([w[k] for k in small_names], F32, 8)
    sm, _ = _pack_rows([mom[k] for k in small_names], F32, 8)
    sv, _ = _pack_rows([vel[k] for k in small_names], F32, 8)
    s_shapes = [w[k].shape for k in small_names]
    small_out = [_unpack_rows(a, s_offs, s_shapes) for a in _adamw(sg, sw, sm, sv, "adamw_small")]
    for i, k in enumerate(small_names):
        out[k] = (mine[k],) + tuple(o[i] for o in small_out)

    res = [loss, grad_x]
    for j in range(4):
        res += [out[k][j] for k in WEIGHT_ORDER]
    return tuple(res)


def kernel(x, mem, ffn1_norm, ffn1_w_gate_up, ffn1_w_down, mix_norm, w_in, conv_a_w, w_out_a, ssm_conv_w, ssm_conv_b, ssm_dt_bias, ssm_a_log, ssm_d, ssm_norm, w_out_ssm, w_mix_out, xattn_norm, mem_norm, w_q, w_kv, w_o_x, ffn2_norm, ffn2_w_gate_up, ffn2_w_down, final_norm, loss_target, m_ffn1_norm, m_ffn1_w_gate_up, m_ffn1_w_down, m_mix_norm, m_w_in, m_conv_a_w, m_w_out_a, m_ssm_conv_w, m_ssm_conv_b, m_ssm_dt_bias, m_ssm_a_log, m_ssm_d, m_ssm_norm, m_w_out_ssm, m_w_mix_out, m_xattn_norm, m_mem_norm, m_w_q, m_w_kv, m_w_o_x, m_ffn2_norm, m_ffn2_w_gate_up, m_ffn2_w_down, m_final_norm, v_ffn1_norm, v_ffn1_w_gate_up, v_ffn1_w_down, v_mix_norm, v_w_in, v_conv_a_w, v_w_out_a, v_ssm_conv_w, v_ssm_conv_b, v_ssm_dt_bias, v_ssm_a_log, v_ssm_d, v_ssm_norm, v_w_out_ssm, v_w_mix_out, v_xattn_norm, v_mem_norm, v_w_q, v_w_kv, v_w_o_x, v_ffn2_norm, v_ffn2_w_gate_up, v_ffn2_w_down, v_final_norm):
    return _step(dict(locals()))
```

```python
import functools
import math
import types

import jax
import jax.numpy as jnp
from jax import lax
from jax.experimental import pallas as pl
from jax.experimental.pallas import tpu as pltpu

F32, BF16 = jnp.float32, jnp.bfloat16
HI = lax.Precision.HIGHEST
MESH = pl.DeviceIdType.MESH
AXES = ("x", "y", "c")
N_DEV = 8

EPS = 1e-6
FFN_RES_WEIGHT = 0.5
SSM_HEAD_DIM = 64
SSM_GROUPS = 4
SSM_STATE = 128
SSM_CHUNK = 128
CONV_A_K = 3
SSM_CONV_K = 4
XATTN_HEADS = 4
ADAM_LR, ADAM_B1, ADAM_B2, ADAM_EPS, ADAM_WD, ADAM_STEP = 1e-3, 0.9, 0.999, 1e-8, 0.01, 10

LANES = 128
BF16_SUBLANES = 16
VMEM_LIMIT_BYTES = 56 * 2 ** 20
NEG_BIG = -1e30

BIG_WEIGHTS = ("ffn1_w_gate_up", "ffn1_w_down", "w_in", "w_out_a", "w_out_ssm", "w_mix_out",
               "w_q", "w_kv", "w_o_x", "ffn2_w_gate_up", "ffn2_w_down")
COL_SHARDED = ("ffn1_w_gate_up", "w_in", "w_kv", "ffn2_w_gate_up")
SMALL_REPLICATED = ("ffn1_norm", "mix_norm", "ssm_conv_b", "ssm_dt_bias", "ssm_a_log", "ssm_d", "ssm_norm",
                    "xattn_norm", "mem_norm", "ffn2_norm", "final_norm")
SMALL_SHARDED = ("conv_a_w", "ssm_conv_w")
WEIGHT_ORDER = ("ffn1_norm", "ffn1_w_gate_up", "ffn1_w_down", "mix_norm", "w_in", "conv_a_w", "w_out_a",
                "ssm_conv_w", "ssm_conv_b", "ssm_dt_bias", "ssm_a_log", "ssm_d", "ssm_norm", "w_out_ssm",
                "w_mix_out", "xattn_norm", "mem_norm", "w_q", "w_kv", "w_o_x", "ffn2_norm", "ffn2_w_gate_up",
                "ffn2_w_down", "final_norm")


def _tile(dim, pref, unit):
    best = None
    t = unit
    while t <= min(dim, pref):
        if dim % t == 0:
            best = t
        t += unit
    return best if best is not None else dim


def _params(*sem):
    return pltpu.CompilerParams(dimension_semantics=sem, vmem_limit_bytes=VMEM_LIMIT_BYTES)


def _sigmoid(x):
    return pl.reciprocal(1.0 + jnp.exp(-x), approx=True)


def _silu(x):
    return x * _sigmoid(x)


def _dsilu(x):
    s = _sigmoid(x)
    return s * (1.0 + x * (1.0 - s))


def _softplus(x):
    return jnp.maximum(x, 0.0) + jnp.log(1.0 + jnp.exp(-jnp.abs(x)))


def _dot(a, b, dims=(((1,), (0,)), ((), ())), precision=None):
    return lax.dot_general(a, b, dims, preferred_element_type=F32, precision=precision)


def _stack_rows(rows, width):
    r_idx = lax.broadcasted_iota(jnp.int32, (8, width), 0)
    acc = jnp.zeros((8, width), F32)
    for k, row in enumerate(rows):
        acc = jnp.where(r_idx == k, row, acc)
    return acc


NT = (((1,), (1,)), ((), ()))
TN = (((0,), (0,)), ((), ()))


def _mm(a, b, out_dtype, name, res=None, alpha=1.0, nt=False, dep=None, norm_out=None, norm_bwd=None,
        tm=1024, tn=2048, tk=2816):
    pieces = list(a) if isinstance(a, (list, tuple)) else [a]
    m = pieces[0].shape[0]
    k = sum(p.shape[1] for p in pieces)
    n = b.shape[0] if nt else b.shape[1]
    assert (b.shape[1] if nt else b.shape[0]) == k
    tm, tn = _tile(m, tm, 8), _tile(n, tn, LANES)
    tk = _tile(math.gcd(*[p.shape[1] for p in pieces]), tk, LANES)
    nk = k // tk
    starts, s0 = [], 0
    for p in pieces:
        starts.append((s0, p.shape[1] // tk))
        s0 += p.shape[1] // tk
    n_p = len(pieces)
    assert norm_out is None or norm_bwd is None
    whole_rows = norm_out is not None or norm_bwd is not None
    assert not whole_rows or tn == n
    has_pre = norm_bwd is not None and len(norm_bwd) == 3

    def body(*refs):
        a_refs, b_ref = refs[:n_p], refs[n_p]
        nxt = n_p + 1
        r_ref = refs[nxt] if res is not None else None
        nxt += (res is not None) + (dep is not None)
        g_ref = refs[nxt] if whole_rows else None
        x_ref = refs[nxt + 1] if norm_bwd is not None else None
        pre_ref = refs[nxt + 2] if has_pre else None
        nxt += whole_rows + (norm_bwd is not None) + has_pre
        o_ref = refs[nxt]
        o2_ref = refs[nxt + 1] if whole_rows else None
        scr = refs[nxt + 1 + whole_rows:]
        first_rows = pl.program_id(0) == 0

        def finish(acc):
            acc = alpha * acc if alpha != 1.0 else acc
            if norm_bwd is not None:
                if pre_ref is not None:
                    acc = acc + pre_ref[...]
                xv = x_ref[...]
                r = lax.rsqrt(jnp.mean(xv * xv, axis=-1, keepdims=True) + EPS)
                xh = xv * r
                gy = acc * g_ref[...]
                part = jnp.sum(acc * xh, axis=0, keepdims=True)
                acc = r * (gy - xh * jnp.mean(gy * xh, axis=-1, keepdims=True))

                @pl.when(first_rows)
                def _():
                    o2_ref[...] = part

                @pl.when(jnp.logical_not(first_rows))
                def _():
                    o2_ref[...] += part
            if r_ref is not None:
                acc = r_ref[...] + acc
            o_ref[...] = acc.astype(out_dtype)
            if norm_out is not None:
                r = lax.rsqrt(jnp.mean(acc * acc, axis=-1, keepdims=True) + EPS)
                o2_ref[...] = (acc * r * g_ref[...]).astype(BF16)

        def product(a_ref):
            return _dot(a_ref[...].astype(BF16), b_ref[...].astype(BF16), NT if nt else (((1,), (0,)), ((), ())))

        if nk == 1:
            finish(product(a_refs[0]))
            return
        acc_ref = scr[0]
        kk = pl.program_id(2)
        for (s, cnt), a_ref in zip(starts, a_refs):
            if s == 0:
                @pl.when(kk == 0)
                def _():
                    acc_ref[...] = product(a_ref)

                @pl.when(jnp.logical_and(kk > 0, kk < cnt))
                def _():
                    acc_ref[...] += product(a_ref)
            else:
                @pl.when(jnp.logical_and(kk >= s, kk < s + cnt))
                def _():
                    acc_ref[...] += product(a_ref)

        @pl.when(kk == nk - 1)
        def _():
            finish(acc_ref[...])

    def a_spec(s, cnt):
        return pl.BlockSpec((tm, tk), lambda i, j, kk: (i, jnp.clip(kk - s, 0, cnt - 1)))

    in_specs = [a_spec(s, cnt) for s, cnt in starts]
    in_specs.append(pl.BlockSpec((tn, tk), lambda i, j, kk: (j, kk)) if nt else pl.BlockSpec((tk, tn), lambda i, j, kk: (kk, j)))
    args = pieces + [b]
    if res is not None:
        in_specs.append(pl.BlockSpec((tm, tn), lambda i, j, kk: (i, j)))
        args.append(res)
    if dep is not None:
        in_specs.append(pl.BlockSpec((8, LANES), lambda i, j, kk: (0, 0)))
        args.append(dep)
    tile = pl.BlockSpec((tm, tn), lambda i, j, kk: (i, j))
    vec = pl.BlockSpec((1, n), lambda i, j, kk: (0, 0))
    out_specs, out_shape = tile, jax.ShapeDtypeStruct((m, n), out_dtype)
    if norm_out is not None:
        in_specs.append(vec)
        args.append(norm_out)
        out_specs, out_shape = [tile, tile], [out_shape, jax.ShapeDtypeStruct((m, n), BF16)]
    if norm_bwd is not None:
        in_specs += [vec, tile] + ([tile] if has_pre else [])
        args += [norm_bwd[1], norm_bwd[0]] + ([norm_bwd[2]] if has_pre else [])
        out_specs, out_shape = [tile, vec], [out_shape, jax.ShapeDtypeStruct((1, n), F32)]
    return pl.pallas_call(
        body, name=name, grid=(m // tm, n // tn, nk), in_specs=in_specs, out_specs=out_specs, out_shape=out_shape,
        scratch_shapes=[pltpu.VMEM((tm, tn), F32)] if nk > 1 else [],
        compiler_params=(_params("arbitrary", "arbitrary", "arbitrary") if norm_bwd is not None
                         else _params("parallel", "parallel", "arbitrary")),
    )(*args)


def _mm_tn(x, dy, name, out_dtype=BF16, alpha=1.0, dep=None, out_rows=None, row_off=0, into=None,
           tko=1408, tn=1024, tt=2048):
    t, k = x.shape
    n = dy.shape[1]
    tko, tn, tt = _tile(k, tko, LANES), _tile(n, tn, LANES), _tile(t, tt, 8)
    nt_steps = t // tt

    def body(*refs):
        x_ref, dy_ref = refs[:2]
        o_ref, acc_ref = refs[-2:]
        part = _dot(x_ref[...].astype(BF16), dy_ref[...].astype(BF16), TN)
        step = pl.program_id(2)

        @pl.when(step == 0)
        def _():
            acc_ref[...] = part

        @pl.when(step > 0)
        def _():
            acc_ref[...] += part

        @pl.when(step == nt_steps - 1)
        def _():
            acc = acc_ref[...]
            o_ref[...] = (alpha * acc if alpha != 1.0 else acc).astype(out_dtype)

    in_specs = [pl.BlockSpec((tt, tko), lambda i, j, s: (s, i)), pl.BlockSpec((tt, tn), lambda i, j, s: (s, j))]
    args = [x, dy]
    if dep is not None:
        in_specs.append(pl.BlockSpec((8, LANES), lambda i, j, s: (0, 0)))
        args.append(dep)
    aliases = {}
    if into is not None:
        in_specs.append(pl.BlockSpec(memory_space=pl.ANY))
        args.append(into)
        aliases = {len(args) - 1: 0}
    band = row_off // tko
    assert row_off % tko == 0
    return pl.pallas_call(
        body, name=name, grid=(k // tko, n // tn, nt_steps), in_specs=in_specs,
        out_specs=pl.BlockSpec((tko, tn), lambda i, j, s: (i + band, j)),
        out_shape=jax.ShapeDtypeStruct((out_rows or k, n), out_dtype),
        scratch_shapes=[pltpu.VMEM((tko, tn), F32)], input_output_aliases=aliases,
        compiler_params=_params("parallel", "parallel", "arbitrary"),
    )(*args)


def _rms_fwd(x, g, name, tt=512):
    t, d = x.shape
    tt = _tile(t, tt, 8)

    def body(x_ref, g_ref, o_ref):
        xv = x_ref[...]
        r = lax.rsqrt(jnp.mean(xv * xv, axis=-1, keepdims=True) + EPS)
        o_ref[...] = (xv * r * g_ref[...]).astype(BF16)

    return pl.pallas_call(
        body, name=name, grid=(t // tt,),
        in_specs=[pl.BlockSpec((tt, d), lambda i: (i, 0)), pl.BlockSpec((1, d), lambda i: (0, 0))],
        out_specs=pl.BlockSpec((tt, d), lambda i: (i, 0)),
        out_shape=jax.ShapeDtypeStruct((t, d), BF16), compiler_params=_params("parallel"),
    )(x, g)


def _ffn_up(n, w_gu_t, name, tm=1024, tf=1408):
    t, d = n.shape
    f = w_gu_t.shape[0] // 2
    tm, tf = _tile(t, tm, 8), _tile(f, tf, LANES)
    nf = f // tf

    def body(n_ref, wg_ref, wu_ref, g_ref, u_ref, a_ref):
        nv = n_ref[...]
        gate, up = _dot(nv, wg_ref[...], NT), _dot(nv, wu_ref[...], NT)
        s = _sigmoid(gate)
        sg = gate * s
        g_ref[...] = (up * (s * (1.0 + gate * (1.0 - s)))).astype(BF16)
        u_ref[...] = sg.astype(BF16)
        a_ref[...] = (sg * up).astype(BF16)

    blk = pl.BlockSpec((tm, tf), lambda i, j: (i, j))
    out = jax.ShapeDtypeStruct((t, f), BF16)
    return pl.pallas_call(
        body, name=name, grid=(t // tm, nf),
        in_specs=[pl.BlockSpec((tm, d), lambda i, j: (i, 0)), pl.BlockSpec((tf, d), lambda i, j: (j, 0)),
                  pl.BlockSpec((tf, d), lambda i, j: (j + nf, 0))],
        out_specs=[blk, blk, blk], out_shape=[out, out, out], compiler_params=_params("parallel", "parallel"),
    )(n, w_gu_t, w_gu_t)


def _ffn_da(dh, w_d, gate, up, name, alpha, dep=None, tm=1024, tf=1408):
    t, d = dh.shape
    f = w_d.shape[0]
    tm, tf = _tile(t, tm, 8), _tile(f, tf, LANES)

    def body(*refs):
        dh_ref, w_ref, g_ref, u_ref = refs[:4]
        dg_ref, du_ref = refs[-2:]
        da = alpha * _dot(dh_ref[...].astype(BF16), w_ref[...], NT)
        dg_ref[...] = (da * g_ref[...].astype(F32)).astype(BF16)
        du_ref[...] = (da * u_ref[...].astype(F32)).astype(BF16)

    blk = pl.BlockSpec((tm, tf), lambda i, j: (i, j))
    in_specs = [pl.BlockSpec((tm, d), lambda i, j: (i, 0)), pl.BlockSpec((tf, d), lambda i, j: (j, 0)), blk, blk]
    args = [dh, w_d, gate, up]
    if dep is not None:
        in_specs.append(pl.BlockSpec((8, LANES), lambda i, j: (0, 0)))
        args.append(dep)
    out = jax.ShapeDtypeStruct((t, f), BF16)
    return pl.pallas_call(
        body, name=name, grid=(t // tm, f // tf), in_specs=in_specs, out_specs=[blk, blk], out_shape=[out, out],
        compiler_params=_params("parallel", "parallel"),
    )(*args)


def _merge_fwd(ya, yb, proj, ga_blk, gb_blk, d, name, tt=512):
    t = ya.shape[0]
    tt = _tile(t, tt, 8)

    def body(ya_ref, yb_ref, ga_ref, gb_ref, o_ref):
        o_ref[...] = (_sigmoid(ga_ref[...].astype(F32)) * ya_ref[...].astype(F32)
                      + _sigmoid(gb_ref[...].astype(F32)) * yb_ref[...].astype(F32)).astype(BF16)

    row = pl.BlockSpec((tt, d), lambda i: (i, 0))
    return pl.pallas_call(
        body, name=name, grid=(t // tt,),
        in_specs=[row, row, pl.BlockSpec((tt, d), lambda i: (i, ga_blk)), pl.BlockSpec((tt, d), lambda i: (i, gb_blk))],
        out_specs=row, out_shape=jax.ShapeDtypeStruct((t, d), BF16), compiler_params=_params("parallel"),
    )(ya, yb, proj, proj)


def _merge_bwd(dm, ya, yb, proj, ga_blk, gb_blk, d, name, tt=512):
    t = ya.shape[0]
    tt = _tile(t, tt, 8)

    def body(dm_ref, ya_ref, yb_ref, ga_ref, gb_ref, dya_ref, dyb_ref, dga_ref, dgb_ref):
        dmv = dm_ref[...].astype(F32)
        sa, sb = _sigmoid(ga_ref[...].astype(F32)), _sigmoid(gb_ref[...].astype(F32))
        dya_ref[...] = (dmv * sa).astype(BF16)
        dyb_ref[...] = (dmv * sb).astype(BF16)
        dga_ref[...] = (dmv * ya_ref[...].astype(F32) * sa * (1.0 - sa)).astype(BF16)
        dgb_ref[...] = (dmv * yb_ref[...].astype(F32) * sb * (1.0 - sb)).astype(BF16)

    row = pl.BlockSpec((tt, d), lambda i: (i, 0))
    out = jax.ShapeDtypeStruct((t, d), BF16)
    return pl.pallas_call(
        body, name=name, grid=(t // tt,),
        in_specs=[row, row, row, pl.BlockSpec((tt, d), lambda i: (i, ga_blk)), pl.BlockSpec((tt, d), lambda i: (i, gb_blk))],
        out_specs=[row] * 4, out_shape=[out] * 4, compiler_params=_params("parallel"),
    )(dm, ya, yb, proj, proj)


def _loss_head(h, g, target, name, tt=512):
    t, d = h.shape
    tt = _tile(t, tt, 8)

    def body(h_ref, g_ref, tg_ref, loss_ref, dh_ref, dg_ref):
        xv = h_ref[...]
        r = lax.rsqrt(jnp.mean(xv * xv, axis=-1, keepdims=True) + EPS)
        xh = xv * r
        err = xh * g_ref[...] - tg_ref[...]
        dout = err * (1.0 / d)
        gy = dout * g_ref[...]
        dh_ref[...] = r * (gy - xh * jnp.mean(gy * xh, axis=-1, keepdims=True))
        dg_part = jnp.sum(dout * xh, axis=0, keepdims=True)
        loss_part = jnp.full((1, LANES), 0.5 / d, F32) * jnp.sum(err * err)

        @pl.when(pl.program_id(0) == 0)
        def _():
            dg_ref[...] = dg_part
            loss_ref[...] = loss_part

        @pl.when(pl.program_id(0) > 0)
        def _():
            dg_ref[...] += dg_part
            loss_ref[...] += loss_part

    row = pl.BlockSpec((tt, d), lambda i: (i, 0))
    vec = pl.BlockSpec((1, d), lambda i: (0, 0))
    return pl.pallas_call(
        body, name=name, grid=(t // tt,), in_specs=[row, vec, row],
        out_specs=[pl.BlockSpec((1, LANES), lambda i: (0, 0)), row, vec],
        out_shape=[jax.ShapeDtypeStruct((1, LANES), F32), jax.ShapeDtypeStruct((t, d), F32), jax.ShapeDtypeStruct((1, d), F32)],
        compiler_params=_params("arbitrary"),
    )(h, g, target)


def _shift_down(x, k, t_idx):
    if k == 0:
        return x
    return jnp.where(t_idx >= k, pltpu.roll(x, k, 0), 0.0)


def _shift_up(x, k, t_idx, s):
    if k == 0:
        return x
    return jnp.where(t_idx < s - k, pltpu.roll(x, s - k, 0), 0.0)


def _conv_a_fwd(proj, w, nb, s, d, name, cb=256):
    cb = _tile(d, cb, LANES)
    nd = d // cb

    def body(b_ref, c_ref, v_ref, w_ref, o_ref):
        t_idx = lax.broadcasted_iota(jnp.int32, (s, cb), 0)
        cv = c_ref[...].astype(F32) * v_ref[...].astype(F32)
        cc = sum(w_ref[k:k + 1, :] * _shift_down(cv, CONV_A_K - 1 - k, t_idx) for k in range(CONV_A_K))
        o_ref[...] = (b_ref[...].astype(F32) * cc).astype(BF16)

    def col(off):
        return pl.BlockSpec((s, cb), lambda b, j: (b, j + off * nd))

    return pl.pallas_call(
        body, name=name, grid=(nb, nd), in_specs=[col(0), col(1), col(2), pl.BlockSpec((8, cb), lambda b, j: (0, j))],
        out_specs=pl.BlockSpec((s, cb), lambda b, j: (b, j)),
        out_shape=jax.ShapeDtypeStruct((nb * s, d), BF16), compiler_params=_params("parallel", "parallel"),
    )(proj, proj, proj, w)


def _conv_a_bwd(dy, proj, w, nb, s, d, name, cb=256):
    cb = _tile(d, cb, LANES)
    nd = d // cb

    def body(dy_ref, b_ref, c_ref, v_ref, w_ref, db_ref, dc_ref, dv_ref, dw_ref):
        t_idx = lax.broadcasted_iota(jnp.int32, (s, cb), 0)
        cv_c, cv_v = c_ref[...].astype(F32), v_ref[...].astype(F32)
        cv = cv_c * cv_v
        shifted = [_shift_down(cv, CONV_A_K - 1 - k, t_idx) for k in range(CONV_A_K)]
        cc = sum(w_ref[k:k + 1, :] * shifted[k] for k in range(CONV_A_K))
        dyv = dy_ref[...].astype(F32)
        db_ref[...] = (dyv * cc).astype(BF16)
        dcc = dyv * b_ref[...].astype(F32)
        dcv = sum(w_ref[k:k + 1, :] * _shift_up(dcc, CONV_A_K - 1 - k, t_idx, s) for k in range(CONV_A_K))
        dc_ref[...] = (dcv * cv_v).astype(BF16)
        dv_ref[...] = (dcv * cv_c).astype(BF16)
        rows = [jnp.sum(dcc * shifted[k], axis=0, keepdims=True) for k in range(CONV_A_K)]
        part = _stack_rows(rows, cb)

        @pl.when(pl.program_id(1) == 0)
        def _():
            dw_ref[...] = part

        @pl.when(pl.program_id(1) > 0)
        def _():
            dw_ref[...] += part

    def col(off):
        return pl.BlockSpec((s, cb), lambda j, b: (b, j + off * nd))

    own = pl.BlockSpec((s, cb), lambda j, b: (b, j))
    wspec = pl.BlockSpec((8, cb), lambda j, b: (0, j))
    out = jax.ShapeDtypeStruct((nb * s, d), BF16)
    return pl.pallas_call(
        body, name=name, grid=(nd, nb), in_specs=[own, col(0), col(1), col(2), wspec],
        out_specs=[own, own, own, wspec], out_shape=[out, out, out, jax.ShapeDtypeStruct((8, d), F32)],
        compiler_params=_params("parallel", "arbitrary"),
    )(dy, proj, proj, proj, w)


def _conv_s_fwd(proj, col0, w, bias, nb, s, cc_width, name, cb=256):
    cb = _tile(math.gcd(cc_width, col0) if col0 else cc_width, cb, LANES)
    nd, off = cc_width // cb, col0 // cb

    def body(x_ref, w_ref, b_ref, o_ref, pre_ref):
        t_idx = lax.broadcasted_iota(jnp.int32, (s, cb), 0)
        xv = x_ref[...].astype(F32)
        pre = b_ref[...] + sum(w_ref[k:k + 1, :] * _shift_down(xv, SSM_CONV_K - 1 - k, t_idx) for k in range(SSM_CONV_K))
        o_ref[...] = _silu(pre).astype(BF16)
        pre_ref[...] = pre.astype(BF16)

    vec = pl.BlockSpec((8, cb), lambda b, j: (0, j))
    own = pl.BlockSpec((s, cb), lambda b, j: (b, j))
    out = jax.ShapeDtypeStruct((nb * s, cc_width), BF16)
    return pl.pallas_call(
        body, name=name, grid=(nb, nd),
        in_specs=[pl.BlockSpec((s, cb), lambda b, j: (b, j + off)), vec, pl.BlockSpec((1, cb), lambda b, j: (0, j))],
        out_specs=[own, own], out_shape=[out, out], compiler_params=_params("parallel", "parallel"),
    )(proj, w, bias)


def _conv_s_bwd(dxc, pre, proj, col0, w, nb, s, cc_width, name, cb=256):
    cb = _tile(math.gcd(cc_width, col0) if col0 else cc_width, cb, LANES)
    nd, off = cc_width // cb, col0 // cb

    def body(d_ref, pre_ref, x_ref, w_ref, dx_ref, dw_ref, db_ref):
        t_idx = lax.broadcasted_iota(jnp.int32, (s, cb), 0)
        xv = x_ref[...].astype(F32)
        dpre = d_ref[...].astype(F32) * _dsilu(pre_ref[...].astype(F32))
        ahead = [_shift_up(dpre, j, t_idx, s) for j in range(SSM_CONV_K)]
        dx_ref[...] = sum(w_ref[k:k + 1, :] * ahead[SSM_CONV_K - 1 - k] for k in range(SSM_CONV_K)).astype(BF16)
        rows = [jnp.sum(ahead[SSM_CONV_K - 1 - k] * xv, axis=0, keepdims=True) for k in range(SSM_CONV_K)]
        dw_part = _stack_rows(rows, cb)
        db_part = jnp.sum(dpre, axis=0, keepdims=True)

        @pl.when(pl.program_id(1) == 0)
        def _():
            dw_ref[...] = dw_part
            db_ref[...] = db_part

        @pl.when(pl.program_id(1) > 0)
        def _():
            dw_ref[...] += dw_part
            db_ref[...] += db_part

    own = pl.BlockSpec((s, cb), lambda j, b: (b, j))
    wspec = pl.BlockSpec((8, cb), lambda j, b: (0, j))
    bspec = pl.BlockSpec((1, cb), lambda j, b: (0, j))
    return pl.pallas_call(
        body, name=name, grid=(nd, nb),
        in_specs=[own, own, pl.BlockSpec((s, cb), lambda j, b: (b, j + off)), wspec],
        out_specs=[own, wspec, bspec],
        out_shape=[jax.ShapeDtypeStruct((nb * s, cc_width), BF16), jax.ShapeDtypeStruct((8, cc_width), F32),
                   jax.ShapeDtypeStruct((1, cc_width), F32)],
        compiler_params=_params("parallel", "arbitrary"),
    )(dxc, pre, proj, w)


def _split3(v):
    hi = v.astype(BF16)
    r1 = v - hi.astype(F32)
    mid = r1.astype(BF16)
    return hi, mid, (r1 - mid.astype(F32)).astype(BF16)


def _exact_left(mask_b, v):
    return sum(_dot(mask_b, t) for t in _split3(v))


def _exact_right(v, mask_b):
    return sum(_dot(t, mask_b) for t in _split3(v))


def _head_sums(v, e_b):
    return _dot(v.astype(BF16), e_b, NT)


def _spread(v, out_ref, di):
    lane = lax.broadcasted_iota(jnp.int32, (v.shape[0], LANES), 1)
    for pr in range(di // LANES):
        h0 = pr * (LANES // SSM_HEAD_DIM)
        out_ref[:, pr * LANES:(pr + 1) * LANES] = jnp.where(lane < SSM_HEAD_DIM, v[:, h0:h0 + 1], v[:, h0 + 1:h0 + 2])


def _ssd_common(xc_ref, dtr_ref, dtrt_ref, prow_ref, pcol_ref, dtx_ref, acsx_ref, dx_ref, di):
    l = SSM_CHUNK
    bias_r, a_r = prow_ref[0:1, :], -jnp.exp(prow_ref[1:2, :])
    sp_in = dtr_ref[...] + bias_r
    dt = _softplus(sp_in)
    li = lax.broadcasted_iota(jnp.int32, (l, l), 0)
    si = lax.broadcasted_iota(jnp.int32, (l, l), 1)
    lower_b = (li >= si).astype(BF16)
    upper_b = (li <= si).astype(BF16)
    acs = _exact_left(lower_b, dt * a_r)
    bias_c, a_c = pcol_ref[:, 0:1], -jnp.exp(pcol_ref[:, 1:2])
    dt_t = _softplus(dtrt_ref[...] + bias_c)
    acs_t = _exact_right(dt_t * a_c, upper_b)
    _spread(dt, dtx_ref, di)
    _spread(acs, acsx_ref, di)
    _spread(prow_ref[0:8, :], dx_ref, di)
    acs_exp = acsx_ref[...]
    acs_last = acs_exp[l - 1:l, :]
    x = xc_ref[:, 0:di].astype(F32)
    return dict(dt=dt, a_r=a_r, sp_in=sp_in, acs=acs, acs_t=acs_t, dt_exp=dtx_ref[...], e_exp=jnp.exp(acs_exp),
                el_exp=jnp.exp(acs_last), f_exp=jnp.exp(acs_last - acs_exp), x=x, mask=li >= si, upper_b=upper_b,
                d_exp=dx_ref[2:3, :])


def _decay(q, h):
    seg = q["acs"][:, h:h + 1] - q["acs_t"][h:h + 1, :]
    return jnp.exp(jnp.where(q["mask"], seg, NEG_BIG))


def _ssd_fwd(xc, dtr, dtrt, prow, pcol, nb, nc, di, name):
    l, n, g_n, p = SSM_CHUNK, SSM_STATE, SSM_GROUPS, SSM_HEAD_DIM
    cc = xc.shape[1]
    gw = di // g_n
    assert p * 2 == LANES and gw % LANES == 0

    def body(xc_ref, dtr_ref, dtrt_ref, prow_ref, pcol_ref, y_ref, sprev_ref, st_ref, dtx_ref, acsx_ref, dx_ref):
        @pl.when(pl.program_id(1) == 0)
        def _():
            st_ref[...] = jnp.zeros_like(st_ref)

        q = _ssd_common(xc_ref, dtr_ref, dtrt_ref, prow_ref, pcol_ref, dtx_ref, acsx_ref, dx_ref, di)
        x = q["x"]
        xd = x * q["dt_exp"]
        xdb = xd.astype(BF16)
        xdf = (xd * q["f_exp"]).astype(BF16)
        lane = lax.broadcasted_iota(jnp.int32, (l, LANES), 1)
        for g in range(g_n):
            lo = g * gw
            bg = xc_ref[:, di + g * n: di + (g + 1) * n]
            cg = xc_ref[:, di + g_n * n + g * n: di + g_n * n + (g + 1) * n]
            cb = _dot(cg, bg, NT)
            st_g = st_ref[:, lo:lo + gw]
            y_off = q["e_exp"][:, lo:lo + gw] * _dot(cg, st_g.astype(BF16))
            for pr in range(gw // LANES):
                c0 = lo + pr * LANES
                h0 = c0 // p
                xp = xdb[:, c0:c0 + LANES]
                m0 = (cb * _decay(q, h0)).astype(BF16)
                m1 = (cb * _decay(q, h0 + 1)).astype(BF16)
                yd = _dot(m0, jnp.where(lane < p, xp, 0)) + _dot(m1, jnp.where(lane >= p, xp, 0))
                y_ref[:, c0:c0 + LANES] = (yd + y_off[:, pr * LANES:(pr + 1) * LANES]
                                           + q["d_exp"][:, c0:c0 + LANES] * x[:, c0:c0 + LANES])
            sprev_ref[:, lo:lo + gw] = st_g
            st_ref[:, lo:lo + gw] = q["el_exp"][:, lo:lo + gw] * st_g + _dot(bg, xdf[:, lo:lo + gw], TN)

    tok = lambda w: pl.BlockSpec((l, w), lambda b, c: (b * nc + c, 0))
    const = lambda r, w: pl.BlockSpec((r, w), lambda b, c: (0, 0))
    return pl.pallas_call(
        body, name=name, grid=(nb, nc),
        in_specs=[tok(cc), tok(LANES), pl.BlockSpec((LANES, l), lambda b, c: (0, b * nc + c)),
                  const(8, LANES), const(LANES, 8)],
        out_specs=[tok(di), pl.BlockSpec((None, n, di), lambda b, c: (b * nc + c, 0, 0))],
        out_shape=[jax.ShapeDtypeStruct((nb * nc * l, di), F32), jax.ShapeDtypeStruct((nb * nc, n, di), F32)],
        scratch_shapes=[pltpu.VMEM((n, di), F32), pltpu.VMEM((l, di), F32), pltpu.VMEM((l, di), F32),
                        pltpu.VMEM((8, di), F32)],
        compiler_params=_params("parallel", "arbitrary"),
    )(xc, dtr, dtrt, prow, pcol)


def _ssd_bwd(dy, xc, dtr, dtrt, prow, pcol, e_mat, sprev, nb, nc, di, name):
    l, n, g_n, p = SSM_CHUNK, SSM_STATE, SSM_GROUPS, SSM_HEAD_DIM
    cc = xc.shape[1]
    gw = di // g_n

    def body(dy_ref, xc_ref, dtr_ref, dtrt_ref, prow_ref, pcol_ref, e_ref, sprev_ref,
             dxc_ref, ddtr_ref, sums_ref, dst_ref, off_ref, dxd_ref, last_ref, vst_ref,
             dtx_ref, acsx_ref, dx_ref):
        first = jnp.logical_and(pl.program_id(0) == 0, pl.program_id(1) == 0)

        @pl.when(pl.program_id(1) == 0)
        def _():
            dst_ref[...] = jnp.zeros_like(dst_ref)

        head_row = lax.broadcasted_iota(jnp.int32, (LANES, l), 0)
        row_sums, col_sums = jnp.zeros((l, LANES), F32), jnp.zeros((LANES, l), F32)
        strict_lower = lax.broadcasted_iota(jnp.int32, (l, l), 0) > lax.broadcasted_iota(jnp.int32, (l, l), 1)

        q = _ssd_common(xc_ref, dtr_ref, dtrt_ref, prow_ref, pcol_ref, dtx_ref, acsx_ref, dx_ref, di)
        x = q["x"]
        xd = x * q["dt_exp"]
        xdb = xd.astype(BF16)
        xdf = (xd * q["f_exp"]).astype(BF16)
        dyv = dy_ref[...]
        dyb = dyv.astype(BF16)
        dye = (dyv * q["e_exp"]).astype(BF16)
        upper_b = q["upper_b"]
        lane = lax.broadcasted_iota(jnp.int32, (l, LANES), 1)
        for g in range(g_n):
            lo = g * gw
            bg = xc_ref[:, di + g * n: di + (g + 1) * n]
            cg = xc_ref[:, di + g_n * n + g * n: di + g_n * n + (g + 1) * n]
            cb = _dot(cg, bg, NT)
            st_g = sprev_ref[:, lo:lo + gw]
            st_gb = st_g.astype(BF16)
            dst_g = dst_ref[:, lo:lo + gw]
            dst_gb = dst_g.astype(BF16)
            dye_g = dye[:, lo:lo + gw]
            xdf_g = xdf[:, lo:lo + gw]
            y_off = q["e_exp"][:, lo:lo + gw] * _dot(cg, st_gb)
            dc_g = _dot(dye_g, st_gb, NT)
            db_g = _dot(xdf_g, dst_gb, NT)
            dxd_state = _dot(bg, dst_gb) * q["f_exp"][:, lo:lo + gw]
            last_ref[:, lo:lo + gw] = jnp.sum(dst_g * st_g, axis=0, keepdims=True)
            dst_ref[:, lo:lo + gw] = q["el_exp"][:, lo:lo + gw] * dst_g + _dot(cg, dye_g, TN)
            off_ref[:, lo:lo + gw] = dyv[:, lo:lo + gw] * y_off
            vst_ref[:, lo:lo + gw] = xd[:, lo:lo + gw] * dxd_state
            dcb = jnp.zeros((l, l), F32)
            for pr in range(gw // LANES):
                c0 = lo + pr * LANES
                h0 = c0 // p
                xp = xdb[:, c0:c0 + LANES]
                dyp = dyb[:, c0:c0 + LANES]
                dxd_diag = jnp.zeros((l, LANES), F32)
                for k, keep in enumerate((lane < p, lane >= p)):
                    dec = _decay(q, h0 + k)
                    dy_h = jnp.where(keep, dyp, 0)
                    dm_dec = _dot(dy_h, xp, NT) * dec
                    dcb = dcb + dm_dec
                    dxd_diag = dxd_diag + _dot((cb * dec).astype(BF16), dy_h, TN)
                    qm = dm_dec * cb
                    row_sums = jnp.where(lane == h0 + k, jnp.sum(qm, axis=1, keepdims=True), row_sums)
                    col_sums = jnp.where(head_row == h0 + k, jnp.sum(qm, axis=0, keepdims=True), col_sums)
                dxd_ref[:, c0:c0 + LANES] = dxd_diag + dxd_state[:, pr * LANES:(pr + 1) * LANES]
            dcb_b = dcb.astype(BF16)
            dxc_ref[:, di + g * n: di + (g + 1) * n] = (db_g + _dot(dcb_b, cg, TN)).astype(BF16)
            dxc_ref[:, di + g_n * n + g * n: di + g_n * n + (g + 1) * n] = (dc_g + _dot(dcb_b, bg)).astype(BF16)
        dxd = dxd_ref[...]
        e_b = e_ref[...]
        from_y = _exact_left(upper_b, _head_sums(off_ref[...], e_b) + row_sums - col_sums.T)
        from_s = _exact_left(strict_lower.astype(BF16), _head_sums(vst_ref[...], e_b))
        carried = _head_sums(jnp.broadcast_to(last_ref[...], (8, di)), e_b)[0:1, :] * jnp.exp(q["acs"][l - 1:l, :])
        dla = from_y + from_s + carried
        ddt = dla * q["a_r"] + _head_sums(dxd * x, e_b)
        ddtr = ddt * jax.nn.sigmoid(q["sp_in"])
        ddtr_ref[...] = ddtr
        dxc_ref[:, 0:di] = (dxd * q["dt_exp"] + q["d_exp"] * dyv).astype(BF16)
        dd_exp = jnp.sum(dyv * x, axis=0, keepdims=True)
        dd = _head_sums(jnp.broadcast_to(dd_exp, (8, di)), e_b)[0:1, :]
        part = _stack_rows([jnp.sum(ddtr, axis=0, keepdims=True),
                            jnp.sum(dla * q["dt"], axis=0, keepdims=True) * q["a_r"], dd], LANES)

        @pl.when(first)
        def _():
            sums_ref[...] = part

        @pl.when(jnp.logical_not(first))
        def _():
            sums_ref[...] += part

    rev = lambda b, c: b * nc + (nc - 1 - c)
    tok = lambda w: pl.BlockSpec((l, w), lambda b, c: (rev(b, c), 0))
    const = lambda r, w: pl.BlockSpec((r, w), lambda b, c: (0, 0))
    return pl.pallas_call(
        body, name=name, grid=(nb, nc),
        in_specs=[tok(di), tok(cc), tok(LANES), pl.BlockSpec((LANES, l), lambda b, c: (0, rev(b, c))),
                  const(8, LANES), const(LANES, 8), const(LANES, di),
                  pl.BlockSpec((None, n, di), lambda b, c: (rev(b, c), 0, 0))],
        out_specs=[tok(cc), tok(LANES), const(8, LANES)],
        out_shape=[jax.ShapeDtypeStruct((nb * nc * l, cc), BF16), jax.ShapeDtypeStruct((nb * nc * l, LANES), F32),
                   jax.ShapeDtypeStruct((8, LANES), F32)],
        scratch_shapes=[pltpu.VMEM((n, di), F32), pltpu.VMEM((l, di), F32), pltpu.VMEM((l, di), F32),
                        pltpu.VMEM((1, di), F32), pltpu.VMEM((l, di), F32),
                        pltpu.VMEM((l, di), F32), pltpu.VMEM((l, di), F32), pltpu.VMEM((8, di), F32)],
        compiler_params=_params("arbitrary", "arbitrary"),
    )(dy, xc, dtr, dtrt, prow, pcol, e_mat, sprev)


def _gate_norm_fwd(y, proj, z_col0, norm_g, di, name, tt=256):
    t = y.shape[0]
    tt = _tile(t, tt, 8)
    gw = di // SSM_GROUPS
    zw = _tile(math.gcd(di, z_col0), di, LANES)
    nz, zoff = di // zw, z_col0 // zw

    def body(*refs):
        y_ref, z_refs, g_ref, o_ref = refs[0], refs[1:1 + nz], refs[1 + nz], refs[2 + nz]
        for g in range(SSM_GROUPS):
            lo = g * gw
            zv = z_refs[lo // zw][:, lo % zw:lo % zw + gw].astype(F32)
            yg = y_ref[:, lo:lo + gw] * _silu(zv)
            r = lax.rsqrt(jnp.mean(yg * yg, axis=-1, keepdims=True) + EPS)
            o_ref[:, lo:lo + gw] = (yg * r * g_ref[:, lo:lo + gw]).astype(BF16)

    row = pl.BlockSpec((tt, di), lambda i: (i, 0))
    zspecs = [pl.BlockSpec((tt, zw), functools.partial(lambda i, k: (i, zoff + k), k=k)) for k in range(nz)]
    return pl.pallas_call(
        body, name=name, grid=(t // tt,), in_specs=[row] + zspecs + [pl.BlockSpec((1, di), lambda i: (0, 0))],
        out_specs=row, out_shape=jax.ShapeDtypeStruct((t, di), BF16), compiler_params=_params("parallel"),
    )(y, *([proj] * nz), norm_g)


def _gate_norm_bwd(dn, y, proj, z_col0, norm_g, di, name, tt=256):
    t = y.shape[0]
    tt = _tile(t, tt, 8)
    gw = di // SSM_GROUPS
    zw = _tile(math.gcd(di, z_col0), di, LANES)
    nz, zoff = di // zw, z_col0 // zw

    def body(*refs):
        dn_ref, y_ref, z_refs, g_ref = refs[0], refs[1], refs[2:2 + nz], refs[2 + nz]
        dy_ref, dz_ref, dg_ref = refs[3 + nz:]
        first = pl.program_id(0) == 0
        for g in range(SSM_GROUPS):
            lo = g * gw
            zv = z_refs[lo // zw][:, lo % zw:lo % zw + gw].astype(F32)
            yv = y_ref[:, lo:lo + gw]
            sz = _silu(zv)
            yg = yv * sz
            r = lax.rsqrt(jnp.mean(yg * yg, axis=-1, keepdims=True) + EPS)
            yh = yg * r
            dnv = dn_ref[:, lo:lo + gw].astype(F32)
            gy = dnv * g_ref[:, lo:lo + gw]
            dyg = r * (gy - yh * jnp.mean(gy * yh, axis=-1, keepdims=True))
            dy_ref[:, lo:lo + gw] = dyg * sz
            dz_ref[:, lo:lo + gw] = (dyg * yv * _dsilu(zv)).astype(BF16)
            part = jnp.sum(dnv * yh, axis=0, keepdims=True)

            @pl.when(first)
            def _():
                dg_ref[:, lo:lo + gw] = part

            @pl.when(jnp.logical_not(first))
            def _():
                dg_ref[:, lo:lo + gw] += part

    row = pl.BlockSpec((tt, di), lambda i: (i, 0))
    vec = pl.BlockSpec((1, di), lambda i: (0, 0))
    zspecs = [pl.BlockSpec((tt, zw), functools.partial(lambda i, k: (i, zoff + k), k=k)) for k in range(nz)]
    return pl.pallas_call(
        body, name=name, grid=(t // tt,), in_specs=[row, row] + zspecs + [vec], out_specs=[row, row, vec],
        out_shape=[jax.ShapeDtypeStruct((t, di), F32), jax.ShapeDtypeStruct((t, di), BF16), jax.ShapeDtypeStruct((1, di), F32)],
        compiler_params=_params("arbitrary"),
    )(dn, y, *([proj] * nz), norm_g)


def _softmax_rows(s):
    s = s - jnp.max(s, axis=-1, keepdims=True)
    e = jnp.exp(s)
    return e * (1.0 / jnp.sum(e, axis=-1, keepdims=True))


def _xattn_fwd(q, kv, nb, s, m, d, name, tq=1024):
    tq = _tile(s, tq, 8)
    nq = s // tq
    hd = d // XATTN_HEADS
    scale = 1.0 / math.sqrt(hd)

    def body(q_ref, k_ref, v_ref, o_ref):
        for h in range(XATTN_HEADS):
            sl = slice(h * hd, (h + 1) * hd)
            prob = _softmax_rows(_dot(q_ref[:, sl], k_ref[:, sl], NT) * scale)
            o_ref[:, sl] = _dot(prob.astype(BF16), v_ref[:, sl]).astype(BF16)

    return pl.pallas_call(
        body, name=name, grid=(nb, nq),
        in_specs=[pl.BlockSpec((tq, d), lambda b, i: (b * nq + i, 0)), pl.BlockSpec((m, d), lambda b, i: (b, 0)),
                  pl.BlockSpec((m, d), lambda b, i: (b, 1))],
        out_specs=pl.BlockSpec((tq, d), lambda b, i: (b * nq + i, 0)),
        out_shape=jax.ShapeDtypeStruct((nb * s, d), BF16), compiler_params=_params("parallel", "parallel"),
    )(q, kv, kv)


def _xattn_bwd(do, q, kv, nb, s, m, d, name, tq=1024):
    tq = _tile(s, tq, 8)
    nq = s // tq
    hd = d // XATTN_HEADS
    scale = 1.0 / math.sqrt(hd)

    def body(do_ref, q_ref, k_ref, v_ref, dq_ref, dk_ref, dv_ref):
        first = pl.program_id(1) == 0
        for h in range(XATTN_HEADS):
            sl = slice(h * hd, (h + 1) * hd)
            qh, kh, vh, doh = q_ref[:, sl], k_ref[:, sl], v_ref[:, sl], do_ref[:, sl]
            prob = _softmax_rows(_dot(qh, kh, NT) * scale)
            dv_h = _dot(prob.astype(BF16), doh, TN)
            dp = _dot(doh, vh, NT)
            ds = (prob * (dp - jnp.sum(dp * prob, axis=-1, keepdims=True)) * scale).astype(BF16)
            dq_ref[:, sl] = _dot(ds, kh).astype(BF16)
            dk_h = _dot(ds, qh, TN)

            @pl.when(first)
            def _():
                dk_ref[:, sl] = dk_h
                dv_ref[:, sl] = dv_h

            @pl.when(jnp.logical_not(first))
            def _():
                dk_ref[:, sl] += dk_h
                dv_ref[:, sl] += dv_h

    qspec = pl.BlockSpec((tq, d), lambda b, i: (b * nq + i, 0))
    dq, dk, dv = pl.pallas_call(
        body, name=name, grid=(nb, nq),
        in_specs=[qspec, qspec, pl.BlockSpec((m, d), lambda b, i: (b, 0)), pl.BlockSpec((m, d), lambda b, i: (b, 1))],
        out_specs=[qspec, pl.BlockSpec((m, d), lambda b, i: (b, 0)), pl.BlockSpec((m, d), lambda b, i: (b, 0))],
        out_shape=[jax.ShapeDtypeStruct((nb * s, d), BF16), jax.ShapeDtypeStruct((nb * m, d), F32),
                   jax.ShapeDtypeStruct((nb * m, d), F32)],
        compiler_params=_params("parallel", "arbitrary"),
    )(do, q, kv, kv)
    return dq, dk, dv


def _all_gather(shards, name):
    n_arr = len(shards)

    def body(*refs):
        x_refs, out_refs = refs[:n_arr], refs[n_arr:2 * n_arr]
        send_sems, recv_sems, local_sems = refs[2 * n_arr:]
        x, y, c = lax.axis_index("x"), lax.axis_index("y"), lax.axis_index("c")
        me, sibling = (x, y, c), (x, y, 1 - c)
        chips = [(1 - x, y), (x, 1 - y), (1 - x, 1 - y)]

        def copy(w, k, block, to, from_input=False):
            px, py, pc = block
            rows = out_refs[w].at[4 * px + 2 * py + pc]
            return pltpu.make_async_remote_copy(
                src_ref=x_refs[w] if from_input else rows, dst_ref=rows,
                send_sem=send_sems.at[7 * w + k], recv_sem=recv_sems.at[7 * w + k], device_id=to, device_id_type=MESH)

        started = []
        for w in range(n_arr):
            mine = pltpu.make_async_copy(x_refs[w], out_refs[w].at[4 * x + 2 * y + c], local_sems.at[w])
            mine.start()
            started.append(mine)
        sends = []
        for w in range(n_arr):
            sends.append(copy(w, 0, me, sibling, from_input=True))
            sends += [copy(w, 1 + j, me, (*chip, c), from_input=True) for j, chip in enumerate(chips)]
        for cp in sends:
            cp.start()
        for j, chip in enumerate(chips):
            for w in range(n_arr):
                copy(w, 1 + j, (*chip, c), me).wait_recv()
                passed = copy(w, 4 + j, (*chip, c), sibling)
                passed.start()
                sends.append(passed)
        for w in range(n_arr):
            copy(w, 0, sibling, me).wait_recv()
            for j, chip in enumerate(chips):
                copy(w, 4 + j, (*chip, 1 - c), me).wait_recv()
        for cp in sends:
            cp.wait_send()
        for mine in started:
            mine.wait()

    hbm = pl.BlockSpec(memory_space=pl.ANY)
    return pl.pallas_call(
        body, name=name, out_shape=[jax.ShapeDtypeStruct((N_DEV,) + s.shape, s.dtype) for s in shards],
        in_specs=[hbm] * n_arr, out_specs=[hbm] * n_arr,
        scratch_shapes=[pltpu.SemaphoreType.DMA((7 * n_arr,)), pltpu.SemaphoreType.DMA((7 * n_arr,)),
                        pltpu.SemaphoreType.DMA((n_arr,))],
    )(*shards)


_HBM = pl.BlockSpec(memory_space=pltpu.HBM)
_SEM = pl.BlockSpec(memory_space=pltpu.SEMAPHORE)
_DATAFLOW = pltpu.SideEffectType.DATAFLOW_SIDE_EFFECTING


def _peer_list(x, y, c):
    return [(1 - x if k & 4 else x, 1 - y if k & 2 else y, 1 - c if k & 1 else c) for k in range(1, N_DEV)]


def _push_copy(src_ref, land_ref, send_sems, recv_sems, w, k, peer, me, per_peer_src, receiving):
    px, py, pc = peer
    peer_slot = 4 * px + 2 * py + pc
    return pltpu.make_async_remote_copy(
        src_ref=src_ref.at[peer_slot] if per_peer_src else src_ref,
        dst_ref=land_ref.at[peer_slot if receiving else me],
        send_sem=send_sems.at[7 * w + k], recv_sem=recv_sems.at[7 * w + k], device_id=peer, device_id_type=MESH)


def _push_start(srcs, per_peer_src, after, name):
    n_arr = len(srcs)
    land_shapes = [s.shape if per_peer_src else (N_DEV,) + s.shape for s in srcs]

    def body(*refs):
        src_refs, land_refs = refs[:n_arr], refs[n_arr:2 * n_arr]
        send_sems, recv_sems = refs[2 * n_arr + 1], refs[2 * n_arr + 2]
        token = refs[-1]
        x, y, c = lax.axis_index("x"), lax.axis_index("y"), lax.axis_index("c")
        me = 4 * x + 2 * y + c
        for w in range(n_arr):
            for k, peer in enumerate(_peer_list(x, y, c)):
                _push_copy(src_refs[w], land_refs[w], send_sems, recv_sems, w, k, peer, me, per_peer_src, False).start()
        token[...] = jnp.zeros_like(token)

    lands = [pltpu.with_memory_space_constraint(lax.empty(ls, s.dtype), pltpu.HBM) for ls, s in zip(land_shapes, srcs)]
    srcs_hbm = [pltpu.with_memory_space_constraint(s, pltpu.HBM) for s in srcs]
    out = pl.pallas_call(
        body, name=name,
        out_shape=(pltpu.SemaphoreType.DMA((7 * n_arr,)), pltpu.SemaphoreType.DMA((7 * n_arr,)),
                   *[pltpu.HBM(s.shape, s.dtype) for s in srcs], *[pltpu.HBM(ls, s.dtype) for ls, s in zip(land_shapes, srcs)],
                   jax.ShapeDtypeStruct((8, LANES), F32)),
        in_specs=[_HBM] * (2 * n_arr) + [pl.BlockSpec(memory_space=pl.ANY)],
        out_specs=(_SEM, _SEM, *([_HBM] * (2 * n_arr)), pl.BlockSpec(memory_space=pltpu.VMEM)),
        input_output_aliases={i: 2 + i for i in range(2 * n_arr)},
        compiler_params=pltpu.CompilerParams(has_side_effects=_DATAFLOW),
    )(*srcs_hbm, *lands, after)
    return dict(send=out[0], recv=out[1], srcs=list(out[2:2 + n_arr]), lands=list(out[2 + n_arr:2 + 2 * n_arr]),
                token=out[-1])


def _push_wait(pending, per_peer_src, after, name):
    n_arr = len(pending["srcs"])

    def body(*refs):
        src_refs, land_refs = refs[:n_arr], refs[n_arr:2 * n_arr]
        send_sems, recv_sems = refs[2 * n_arr], refs[2 * n_arr + 1]
        x, y, c = lax.axis_index("x"), lax.axis_index("y"), lax.axis_index("c")
        me = 4 * x + 2 * y + c
        for w in range(n_arr):
            for k, peer in enumerate(_peer_list(x, y, c)):
                cp = _push_copy(src_refs[w], land_refs[w], send_sems, recv_sems, w, k, peer, me, per_peer_src, True)
                cp.wait_send()
                cp.wait_recv()

    out = pl.pallas_call(
        body, name=name,
        out_shape=tuple(pltpu.HBM(a.shape, a.dtype) for a in pending["srcs"] + pending["lands"]),
        in_specs=[_HBM] * (2 * n_arr) + [_SEM, _SEM, pl.BlockSpec(memory_space=pl.ANY)],
        out_specs=tuple([_HBM] * (2 * n_arr)),
        input_output_aliases={i: i for i in range(2 * n_arr)},
        compiler_params=pltpu.CompilerParams(has_side_effects=_DATAFLOW),
    )(*pending["srcs"], *pending["lands"], pending["send"], pending["recv"], after)
    return list(out[:n_arr]), list(out[n_arr:])


def _adamw_math(w, g, m, v):
    m = ADAM_B1 * m + (1.0 - ADAM_B1) * g
    v = ADAM_B2 * v + (1.0 - ADAM_B2) * (g * g)
    m_hat = m / (1.0 - ADAM_B1 ** ADAM_STEP)
    v_hat = v / (1.0 - ADAM_B2 ** ADAM_STEP)
    delta = -ADAM_LR * (m_hat / (jnp.sqrt(v_hat) + ADAM_EPS) + ADAM_WD * w)
    return delta, m, v


def _sum8(parts, name, tr=512):
    _, r, c_dim = parts.shape
    tr = _tile(r, tr, BF16_SUBLANES)

    def body(p_ref, o_ref):
        acc = p_ref[0].astype(F32)
        for k in range(1, N_DEV):
            acc = acc + p_ref[k].astype(F32)
        o_ref[...] = acc

    return pl.pallas_call(
        body, name=name, grid=(r // tr,), in_specs=[pl.BlockSpec((N_DEV, tr, c_dim), lambda i: (0, i, 0))],
        out_specs=pl.BlockSpec((tr, c_dim), lambda i: (i, 0)),
        out_shape=jax.ShapeDtypeStruct((r, c_dim), F32), compiler_params=_params("parallel"),
    )(parts)


def _sum8_adamw(parts, w, m, v, name, tr=128):
    _, r, c_dim = parts.shape
    tr = _tile(r, tr, BF16_SUBLANES)
    tc = c_dim if tr <= 2 * LANES else _tile(c_dim, LANES, LANES)

    def body(p_ref, w_ref, m_ref, v_ref, g_ref, d_ref, nm_ref, nv_ref):
        g = p_ref[0].astype(F32)
        for k in range(1, N_DEV):
            g = g + p_ref[k].astype(F32)
        g_ref[...] = g
        d_ref[...], nm_ref[...], nv_ref[...] = _adamw_math(w_ref[...], g, m_ref[...], v_ref[...])

    blk = pl.BlockSpec((None, tr, tc), lambda i, j: (0, i, j))
    out = jax.ShapeDtypeStruct((1, r, c_dim), F32)
    return pl.pallas_call(
        body, name=name, grid=(r // tr, c_dim // tc),
        in_specs=[pl.BlockSpec((N_DEV, tr, tc), lambda i, j: (0, i, j)), blk, blk, blk],
        out_specs=[blk] * 4, out_shape=[out] * 4, compiler_params=_params("parallel", "parallel"),
    )(parts, w, m, v)


def _adamw(g, w, m, v, name):
    r, c_dim = g.shape

    def body(g_ref, w_ref, m_ref, v_ref, d_ref, nm_ref, nv_ref):
        d_ref[...], nm_ref[...], nv_ref[...] = _adamw_math(w_ref[...], g_ref[...], m_ref[...], v_ref[...])

    out = jax.ShapeDtypeStruct((r, c_dim), F32)
    return pl.pallas_call(body, name=name, out_shape=[out] * 3)(g, w, m, v)


def _pack_rows(arrays, dtype, row_unit):
    chunks, offs, r0 = [], [], 0
    for a in arrays:
        flat = a.reshape(-1).astype(dtype)
        rows = -(-flat.shape[0] // (LANES * row_unit)) * row_unit
        flat = jnp.pad(flat, (0, rows * LANES - flat.shape[0]))
        chunks.append(flat.reshape(rows, LANES))
        offs.append((r0, rows))
        r0 += rows
    return jnp.concatenate(chunks, axis=0), offs


def _unpack_rows(packed, offs, shapes):
    out = []
    for (r0, rows), shape in zip(offs, shapes):
        n = math.prod(shape)
        blk = packed[..., r0:r0 + rows, :]
        blk = blk.reshape(packed.shape[:-2] + (rows * LANES,))[..., :n]
        out.append(blk.reshape(packed.shape[:-2] + tuple(shape)))
    return out


def _full_from_slots(blk, col_sharded):
    _, r, c = blk.shape
    if col_sharded:
        return blk.transpose(1, 0, 2).reshape(r, N_DEV * c)
    return blk.reshape(N_DEV * r, c)


def _ffn_fwd(h, n, w_gu_t, w_d, tag, next_gain=None):
    gate, up, a = _ffn_up(n, w_gu_t, f"{tag}_up")
    out = _mm(a, w_d, F32, f"{tag}_down", res=h, alpha=FFN_RES_WEIGHT, norm_out=next_gain)
    h_out, n_next = out if next_gain is not None else (out, None)
    return h_out, (h, n, gate, up, a), n_next


def _ffn_bwd(dh_out, saved, g, w_gu_t, w_d, tag, dep, send_grads):
    h, n, gate, up, a = saved
    dw_d = _mm_tn(a, dh_out, f"{tag}_dw_down", alpha=FFN_RES_WEIGHT, dep=dep)
    dgate, dup = _ffn_da(dh_out, w_d, gate, up, f"{tag}_da", FFN_RES_WEIGHT)
    f = dgate.shape[1]
    dw_gu_t = _mm_tn(dgate, n, f"{tag}_dw_gate", out_rows=2 * f)
    dw_gu_t = _mm_tn(dup, n, f"{tag}_dw_up", out_rows=2 * f, row_off=f, into=dw_gu_t)
    dep = send_grads(dw_gu_t, dw_d)
    return _mm([dgate, dup], w_gu_t, F32, f"{tag}_dn", dep=dep, tk=1408, res=dh_out, norm_bwd=(h, g))


W_GROUPS = (("ffn1_w_gate_up", "ffn1_w_down"),
            ("w_in",),
            ("w_out_a", "w_out_ssm", "w_mix_out"),
            ("w_q", "w_kv", "w_o_x", "ffn2_w_gate_up", "ffn2_w_down"))
G_GROUPS = (("ffn2_w_gate_up", "ffn2_w_down"),
            ("w_o_x", "w_q", "w_kv", "w_mix_out", "w_out_a", "w_out_ssm", "w_in"),
            ("ffn1_w_gate_up", "ffn1_w_down"))


def _local_step(x3, mem3, target3, small, comm):
    nb, s, d = x3.shape
    m_len = mem3.shape[1]
    t = nb * s
    nc = s // SSM_CHUNK
    di = small["ssm_norm"].shape[1]
    hs = di // SSM_HEAD_DIM
    cc = di + 2 * SSM_GROUPS * SSM_STATE
    x, mem, target = x3.reshape(t, d), mem3.reshape(nb * m_len, d), target3.reshape(t, d)

    sizes = (d, d, d, di, cc, hs, d, d)
    offs = [0]
    for sz in sizes:
        offs.append(offs[-1] + sz)
    z_col0, xbc_col0 = 3 * d, 3 * d + di
    ga_blk, gb_blk = (3 * d + di + cc) // d, (4 * d + di + cc) // d

    pad_vec = lambda v: jnp.pad(v.reshape(1, -1), ((0, 0), (0, LANES - hs)))
    prow = jnp.concatenate([pad_vec(small["ssm_dt_bias"]), pad_vec(small["ssm_a_log"]), pad_vec(small["ssm_d"]),
                            jnp.zeros((5, LANES), F32)], axis=0)
    pcol = prow.T
    e_mat = (lax.broadcasted_iota(jnp.int32, (LANES, di), 0)
             == lax.broadcasted_iota(jnp.int32, (LANES, di), 1) // SSM_HEAD_DIM).astype(BF16)
    conv_a_w8 = jnp.pad(small["conv_a_w"][0], ((0, 8 - CONV_A_K), (0, 0)))
    ssm_conv_w8 = jnp.pad(small["ssm_conv_w"][0], ((0, 8 - SSM_CONV_K), (0, 0)))

    wts, dep = comm.weights(0, None)
    n1 = _rms_fwd(x, small["ffn1_norm"] + dep[0, 0], "ffn1_norm")
    h1, ffn1_saved, u = _ffn_fwd(x, n1, wts["ffn1_w_gate_up"], wts["ffn1_w_down"], "ffn1", small["mix_norm"])
    got, dep = comm.weights(1, h1)
    wts.update(got)
    w_in_t = wts["w_in"]
    w_main_t = jnp.concatenate([w_in_t[offs[i]:offs[i + 1]] for i in (0, 1, 2, 3, 4, 6, 7)], axis=0)
    w_dt_t = jnp.pad(w_in_t[offs[5]:offs[6]], ((0, LANES - hs), (0, 0)))
    proj = _mm(u, w_main_t, BF16, "in_proj", nt=True, dep=dep)
    dtr = _mm(u, w_dt_t, F32, "in_proj_dt", nt=True)
    yap = _conv_a_fwd(proj, conv_a_w8, nb, s, d, "conv_a")
    got, dep = comm.weights(2, yap)
    wts.update(got)
    y_a = _mm(yap, wts["w_out_a"], BF16, "out_a", dep=dep)
    xc, conv_pre = _conv_s_fwd(proj, xbc_col0, ssm_conv_w8, small["ssm_conv_b"] + dep[0, 0], nb, s, cc, "conv_s")
    dtrt = dtr.T
    y_ssd, sprev = _ssd_fwd(xc, dtr, dtrt, prow, pcol, nb, nc, di, "ssd")
    ygn = _gate_norm_fwd(y_ssd, proj, z_col0, small["ssm_norm"], di, "gate_norm")
    y_b = _mm(ygn, wts["w_out_ssm"], BF16, "out_ssm")
    merged = _merge_fwd(y_a, y_b, proj, ga_blk, gb_blk, d, "merge")
    h2, un = _mm(merged, wts["w_mix_out"], F32, "mix_out", res=h1, norm_out=small["xattn_norm"])
    got, _ = comm.weights(3, h2)
    wts.update(got)
    mn = _rms_fwd(mem, small["mem_norm"], "mem_norm")
    q = _mm(un, wts["w_q"], BF16, "q_proj")
    kv = _mm(mn, wts["w_kv"], BF16, "kv_proj", nt=True)
    o = _xattn_fwd(q, kv, nb, s, m_len, d, "xattn")
    h3, n2 = _mm(o, wts["w_o_x"], F32, "o_proj", res=h2, norm_out=small["ffn2_norm"])
    h4, ffn2_saved, _ = _ffn_fwd(h3, n2, wts["ffn2_w_gate_up"], wts["ffn2_w_down"], "ffn2")
    loss_vec, dh4, dg_final = _loss_head(h4, small["final_norm"].reshape(1, d), target, "loss_head")

    grads = {"final_norm": dg_final.reshape(d)}
    big = {}
    dh3, grads["ffn2_norm"] = _ffn_bwd(
        dh4, ffn2_saved, small["ffn2_norm"], wts["ffn2_w_gate_up"], wts["ffn2_w_down"], "ffn2", None,
        lambda dw_gu_t, dw_d: comm.grads(0, {"ffn2_w_gate_up": dw_gu_t, "ffn2_w_down": dw_d}))
    big["w_o_x"] = _mm_tn(o, dh3, "dw_o")
    do = _mm(dh3, wts["w_o_x"], BF16, "d_o", nt=True)
    dq, dk, dv = _xattn_bwd(do, q, kv, nb, s, m_len, d, "xattn_bwd")
    big["w_q"] = _mm_tn(un, dq, "dw_q")
    big["w_kv"] = _mm_tn(dv, mn, "dw_v", out_rows=2 * d, row_off=d, into=_mm_tn(dk, mn, "dw_k", out_rows=2 * d))
    _, grads["mem_norm"] = _mm([dk, dv], wts["w_kv"], F32, "d_mn", norm_bwd=(mem, small["mem_norm"]))
    dh2, grads["xattn_norm"] = _mm(dq, wts["w_q"], F32, "d_un", nt=True, res=dh3, norm_bwd=(h2, small["xattn_norm"]))
    big["w_mix_out"] = _mm_tn(merged, dh2, "dw_mix")
    dmerged = _mm(dh2, wts["w_mix_out"], BF16, "d_merged", nt=True)
    dya, dyb, dga, dgb = _merge_bwd(dmerged, y_a, y_b, proj, ga_blk, gb_blk, d, "merge_bwd")
    big["w_out_a"] = _mm_tn(yap, dya, "dw_out_a")
    big["w_out_ssm"] = _mm_tn(ygn, dyb, "dw_out_ssm")
    dyap = _mm(dya, wts["w_out_a"], BF16, "d_yap", nt=True)
    dygn = _mm(dyb, wts["w_out_ssm"], BF16, "d_ygn", nt=True)
    dab, dac, dav, dconv_a = _conv_a_bwd(dyap, proj, conv_a_w8, nb, s, d, "conv_a_bwd")
    dy_ssd, dz, grads["ssm_norm"] = _gate_norm_bwd(dygn, y_ssd, proj, z_col0, small["ssm_norm"], di, "gate_norm_bwd")
    dxc, ddtr, ssd_sums = _ssd_bwd(dy_ssd, xc, dtr, dtrt, prow, pcol, e_mat, sprev, nb, nc, di, "ssd_bwd")
    dxbc, dconv_s, grads["ssm_conv_b"] = _conv_s_bwd(dxc, conv_pre, proj, xbc_col0, ssm_conv_w8, nb, s, cc, "conv_s_bwd")
    dpieces = [("ab", dab), ("ac", dac), ("av", dav), ("z", dz), ("xbc", dxbc), ("ga", dga), ("gb", dgb)]
    dw = {tag: _mm_tn(piece, u, f"dw_in_{tag}") for tag, piece in dpieces}
    dw_dt = _mm_tn(ddtr, u, "dw_in_dt")[:hs]
    du_main = _mm([piece for _, piece in dpieces], w_main_t, F32, "d_u", tk=1024)
    dh1, grads["mix_norm"] = _mm(ddtr, w_dt_t, F32, "d_u_dt", res=dh2, norm_bwd=(h1, small["mix_norm"], du_main))
    big["w_in"] = jnp.concatenate([dw["ab"], dw["ac"], dw["av"], dw["z"], dw["xbc"], dw_dt, dw["ga"], dw["gb"]], axis=0)
    dep = comm.grads(1, big)
    dx, grads["ffn1_norm"] = _ffn_bwd(
        dh1, ffn1_saved, small["ffn1_norm"], wts["ffn1_w_gate_up"], wts["ffn1_w_down"], "ffn1", dep,
        lambda dw_gu_t, dw_d: comm.grads(2, {"ffn1_w_gate_up": dw_gu_t, "ffn1_w_down": dw_d}))

    grads["conv_a_w"] = dconv_a[:CONV_A_K]
    grads["ssm_conv_w"] = dconv_s[:SSM_CONV_K]
    grads["ssm_dt_bias"] = ssd_sums[0:1, :hs]
    grads["ssm_a_log"] = ssd_sums[1:2, :hs]
    grads["ssm_d"] = ssd_sums[2:3, :hs]
    return loss_vec[0, 0], dx.reshape(nb, s, d), grads


def _step(inputs):
    w = {k: inputs[k] for k in WEIGHT_ORDER}
    mom = {k: inputs["m_" + k] for k in WEIGHT_ORDER}
    vel = {k: inputs["v_" + k] for k in WEIGHT_ORDER}
    me = 4 * lax.axis_index("x") + 2 * lax.axis_index("y") + lax.axis_index("c")

    send = {k: (w[k][0].T if k in COL_SHARDED else w[k][0]).astype(BF16) for k in BIG_WEIGHTS}

    def own_slot(land, mine):
        return lax.dynamic_update_slice(land, mine[None], (me, 0, 0))

    gathers, exchanges = {}, {}

    def weights(i, after):
        if i == 0:
            lands = _all_gather([send[k] for k in W_GROUPS[0]], "gather0")
        else:
            sent, lands = _push_wait(gathers[i], False, after, f"gather{i}_wait")
            lands = [own_slot(land, mine) for land, mine in zip(lands, sent)]
        full = {k: land.reshape(N_DEV * land.shape[1], land.shape[2]) for k, land in zip(W_GROUPS[i], lands)}
        dep = jnp.zeros((8, LANES), F32)
        if i + 1 < len(W_GROUPS):
            gathers[i + 1] = _push_start([send[k] for k in W_GROUPS[i + 1]], False, lands[0], f"gather{i + 1}_start")
            dep = gathers[i + 1]["token"]
        return full, dep

    def slot_shape(k):
        rows, cols = send[k].shape
        return (rows * cols // LANES, LANES) if rows % BF16_SUBLANES else (rows, cols)

    def send_grads(i, by_name):
        slots = [by_name[k].reshape((N_DEV,) + slot_shape(k)) for k in G_GROUPS[i]]
        exchanges[i] = _push_start(slots, True, slots[0], f"exchange{i}_start")
        return exchanges[i]["token"]

    comm = types.SimpleNamespace(weights=weights, grads=send_grads)
    small = {k: w[k] for k in SMALL_REPLICATED}
    conv_shapes = [w[k].shape[1:] for k in SMALL_SHARDED]
    packed_c, conv_offs = _pack_rows([w[k][0] for k in SMALL_SHARDED], F32, 8)
    conv_blocks = _unpack_rows(_all_gather([packed_c], "gather_conv_weights")[0], conv_offs, conv_shapes)
    for k, b in zip(SMALL_SHARDED, conv_blocks):
        small[k] = _full_from_slots(b, True)[None]

    loss_local, grad_x, grads = _local_step(inputs["x"], inputs["mem"], inputs["loss_target"], small, comm)
    loss = lax.psum(loss_local, AXES)

    out = {}
    for i, names in enumerate(G_GROUPS):
        sent, lands = _push_wait(exchanges[i], True, grad_x, f"exchange{i}_wait")
        for k, land, slots in zip(names, lands, sent):
            parts = own_slot(land, lax.dynamic_index_in_dim(slots, me, 0, keepdims=False))
            if k in COL_SHARDED and w[k].shape[2] % LANES:
                flip = lambda a: a.transpose(0, 2, 1).reshape((1,) + slot_shape(k))
                unflip = lambda a: a.reshape((1,) + send[k].shape).transpose(0, 2, 1)
                out[k] = tuple(unflip(o) for o in _sum8_adamw(parts, flip(w[k]), flip(mom[k]), flip(vel[k]),
                                                             f"sum_adamw_{k}", tr=2048 if slot_shape(k)[1] == LANES else 128))
                continue
            if k in COL_SHARDED:
                parts = parts.transpose(0, 2, 1)
            out[k] = tuple(_sum8_adamw(parts, w[k], mom[k], vel[k], f"sum_adamw_{k}"))

    small_names = SMALL_REPLICATED + SMALL_SHARDED
    packed_g, small_offs = _pack_rows([grads[k] for k in small_names], F32, 8)
    total = _sum8(_all_gather([packed_g], "gather_small_grads")[0], "sum_small_grads")
    full_grads = _unpack_rows(total, small_offs, [grads[k].shape for k in small_names])
    mine = {}
    for k, g in zip(small_names, full_grads):
        if k in SMALL_SHARDED:
            c_loc = w[k].shape[2]
            g = lax.dynamic_slice_in_dim(g, me * c_loc, c_loc, axis=1)
        mine[k] = g.reshape(w[k].shape)
    sg, s_offs = _pack_rows([mine[k] for k in small_names], F32, 8)
    sw, _ = _pack_rows([w[k] for k in small_names], F32, 8)
    sm, _ = _pack_rows([mom[k] for k in small_names], F32, 8)
    sv, _ = _pack_rows([vel[k] for k in small_names], F32, 8)
    s_shapes = [w[k].shape for k in small_names]
    small_out = [_unpack_rows(a, s_offs, s_shapes) for a in _adamw(sg, sw, sm, sv, "adamw_small")]
    for i, k in enumerate(small_names):
        out[k] = (mine[k],) + tuple(o[i] for o in small_out)

    res = [loss, grad_x]
    for j in range(4):
        res += [out[k][j] for k in WEIGHT_ORDER]
    return tuple(res)


def kernel(x, mem, ffn1_norm, ffn1_w_gate_up, ffn1_w_down, mix_norm, w_in, conv_a_w, w_out_a, ssm_conv_w, ssm_conv_b, ssm_dt_bias, ssm_a_log, ssm_d, ssm_norm, w_out_ssm, w_mix_out, xattn_norm, mem_norm, w_q, w_kv, w_o_x, ffn2_norm, ffn2_w_gate_up, ffn2_w_down, final_norm, loss_target, m_ffn1_norm, m_ffn1_w_gate_up, m_ffn1_w_down, m_mix_norm, m_w_in, m_conv_a_w, m_w_out_a, m_ssm_conv_w, m_ssm_conv_b, m_ssm_dt_bias, m_ssm_a_log, m_ssm_d, m_ssm_norm, m_w_out_ssm, m_w_mix_out, m_xattn_norm, m_mem_norm, m_w_q, m_w_kv, m_w_o_x, m_ffn2_norm, m_ffn2_w_gate_up, m_ffn2_w_down, m_final_norm, v_ffn1_norm, v_ffn1_w_gate_up, v_ffn1_w_down, v_mix_norm, v_w_in, v_conv_a_w, v_w_out_a, v_ssm_conv_w, v_ssm_conv_b, v_ssm_dt_bias, v_ssm_a_log, v_ssm_d, v_ssm_norm, v_w_out_ssm, v_w_mix_out, v_xattn_norm, v_mem_norm, v_w_q, v_w_kv, v_w_o_x, v_ffn2_norm, v_ffn2_w_gate_up, v_ffn2_w_down, v_final_norm):
    return _step(dict(locals()))
```

```python
import functools
import math
import types

import jax
import jax.numpy as jnp
from jax import lax
from jax.experimental import pallas as pl
from jax.experimental.pallas import tpu as pltpu

F32, BF16 = jnp.float32, jnp.bfloat16
HI = lax.Precision.HIGHEST
MESH = pl.DeviceIdType.MESH
AXES = ("x", "y", "c")
N_DEV = 8

EPS = 1e-6
FFN_RES_WEIGHT = 0.5
SSM_HEAD_DIM = 64
SSM_GROUPS = 4
SSM_STATE = 128
SSM_CHUNK = 128
CONV_A_K = 3
SSM_CONV_K = 4
XATTN_HEADS = 4
ADAM_LR, ADAM_B1, ADAM_B2, ADAM_EPS, ADAM_WD, ADAM_STEP = 1e-3, 0.9, 0.999, 1e-8, 0.01, 10

LANES = 128
BF16_SUBLANES = 16
VMEM_LIMIT_BYTES = 56 * 2 ** 20
NEG_BIG = -1e30

BIG_WEIGHTS = ("ffn1_w_gate_up", "ffn1_w_down", "w_in", "w_out_a", "w_out_ssm", "w_mix_out",
               "w_q", "w_kv", "w_o_x", "ffn2_w_gate_up", "ffn2_w_down")
COL_SHARDED = ("ffn1_w_gate_up", "w_in", "w_kv", "ffn2_w_gate_up")
SMALL_REPLICATED = ("ffn1_norm", "mix_norm", "ssm_conv_b", "ssm_dt_bias", "ssm_a_log", "ssm_d", "ssm_norm",
                    "xattn_norm", "mem_norm", "ffn2_norm", "final_norm")
SMALL_SHARDED = ("conv_a_w", "ssm_conv_w")
WEIGHT_ORDER = ("ffn1_norm", "ffn1_w_gate_up", "ffn1_w_down", "mix_norm", "w_in", "conv_a_w", "w_out_a",
                "ssm_conv_w", "ssm_conv_b", "ssm_dt_bias", "ssm_a_log", "ssm_d", "ssm_norm", "w_out_ssm",
                "w_mix_out", "xattn_norm", "mem_norm", "w_q", "w_kv", "w_o_x", "ffn2_norm", "ffn2_w_gate_up",
                "ffn2_w_down", "final_norm")


def _tile(dim, pref, unit):
    best = None
    t = unit
    while t <= min(dim, pref):
        if dim % t == 0:
            best = t
        t += unit
    return best if best is not None else dim


def _params(*sem):
    return pltpu.CompilerParams(dimension_semantics=sem, vmem_limit_bytes=VMEM_LIMIT_BYTES)


def _sigmoid(x):
    return pl.reciprocal(1.0 + jnp.exp(-x), approx=True)


def _silu(x):
    return x * _sigmoid(x)


def _dsilu(x):
    s = _sigmoid(x)
    return s * (1.0 + x * (1.0 - s))


def _softplus(x):
    return jnp.maximum(x, 0.0) + jnp.log(1.0 + jnp.exp(-jnp.abs(x)))


def _dot(a, b, dims=(((1,), (0,)), ((), ())), precision=None):
    return lax.dot_general(a, b, dims, preferred_element_type=F32, precision=precision)


def _stack_rows(rows, width):
    r_idx = lax.broadcasted_iota(jnp.int32, (8, width), 0)
    acc = jnp.zeros((8, width), F32)
    for k, row in enumerate(rows):
        acc = jnp.where(r_idx == k, row, acc)
    return acc


NT = (((1,), (1,)), ((), ()))
TN = (((0,), (0,)), ((), ()))


def _mm(a, b, out_dtype, name, res=None, alpha=1.0, nt=False, dep=None, norm_out=None, norm_bwd=None,
        tm=1024, tn=2048, tk=2816):
    pieces = list(a) if isinstance(a, (list, tuple)) else [a]
    m = pieces[0].shape[0]
    k = sum(p.shape[1] for p in pieces)
    n = b.shape[0] if nt else b.shape[1]
    assert (b.shape[1] if nt else b.shape[0]) == k
    tm, tn = _tile(m, tm, 8), _tile(n, tn, LANES)
    tk = _tile(math.gcd(*[p.shape[1] for p in pieces]), tk, LANES)
    nk = k // tk
    starts, s0 = [], 0
    for p in pieces:
        starts.append((s0, p.shape[1] // tk))
        s0 += p.shape[1] // tk
    n_p = len(pieces)
    assert norm_out is None or norm_bwd is None
    whole_rows = norm_out is not None or norm_bwd is not None
    assert not whole_rows or tn == n
    has_pre = norm_bwd is not None and len(norm_bwd) == 3

    def body(*refs):
        a_refs, b_ref = refs[:n_p], refs[n_p]
        nxt = n_p + 1
        r_ref = refs[nxt] if res is not None else None
        nxt += (res is not None) + (dep is not None)
        g_ref = refs[nxt] if whole_rows else None
        x_ref = refs[nxt + 1] if norm_bwd is not None else None
        pre_ref = refs[nxt + 2] if has_pre else None
        nxt += whole_rows + (norm_bwd is not None) + has_pre
        o_ref = refs[nxt]
        o2_ref = refs[nxt + 1] if whole_rows else None
        scr = refs[nxt + 1 + whole_rows:]
        first_rows = pl.program_id(0) == 0

        def finish(acc):
            acc = alpha * acc if alpha != 1.0 else acc
            if norm_bwd is not None:
                if pre_ref is not None:
                    acc = acc + pre_ref[...]
                xv = x_ref[...]
                r = lax.rsqrt(jnp.mean(xv * xv, axis=-1, keepdims=True) + EPS)
                xh = xv * r
                gy = acc * g_ref[...]
                part = jnp.sum(acc * xh, axis=0, keepdims=True)
                acc = r * (gy - xh * jnp.mean(gy * xh, axis=-1, keepdims=True))

                @pl.when(first_rows)
                def _():
                    o2_ref[...] = part

                @pl.when(jnp.logical_not(first_rows))
                def _():
                    o2_ref[...] += part
            if r_ref is not None:
                acc = r_ref[...] + acc
            o_ref[...] = acc.astype(out_dtype)
            if norm_out is not None:
                r = lax.rsqrt(jnp.mean(acc * acc, axis=-1, keepdims=True) + EPS)
                o2_ref[...] = (acc * r * g_ref[...]).astype(BF16)

        def product(a_ref):
            return _dot(a_ref[...].astype(BF16), b_ref[...].astype(BF16), NT if nt else (((1,), (0,)), ((), ())))

        if nk == 1:
            finish(product(a_refs[0]))
            return
        acc_ref = scr[0]
        kk = pl.program_id(2)
        for (s, cnt), a_ref in zip(starts, a_refs):
            if s == 0:
                @pl.when(kk == 0)
                def _():
                    acc_ref[...] = product(a_ref)

                @pl.when(jnp.logical_and(kk > 0, kk < cnt))
                def _():
                    acc_ref[...] += product(a_ref)
            else:
                @pl.when(jnp.logical_and(kk >= s, kk < s + cnt))
                def _():
                    acc_ref[...] += product(a_ref)

        @pl.when(kk == nk - 1)
        def _():
            finish(acc_ref[...])

    def a_spec(s, cnt):
        return pl.BlockSpec((tm, tk), lambda i, j, kk: (i, jnp.clip(kk - s, 0, cnt - 1)))

    in_specs = [a_spec(s, cnt) for s, cnt in starts]
    in_specs.append(pl.BlockSpec((tn, tk), lambda i, j, kk: (j, kk)) if nt else pl.BlockSpec((tk, tn), lambda i, j, kk: (kk, j)))
    args = pieces + [b]
    if res is not None:
        in_specs.append(pl.BlockSpec((tm, tn), lambda i, j, kk: (i, j)))
        args.append(res)
    if dep is not None:
        in_specs.append(pl.BlockSpec((8, LANES), lambda i, j, kk: (0, 0)))
        args.append(dep)
    tile = pl.BlockSpec((tm, tn), lambda i, j, kk: (i, j))
    vec = pl.BlockSpec((1, n), lambda i, j, kk: (0, 0))
    out_specs, out_shape = tile, jax.ShapeDtypeStruct((m, n), out_dtype)
    if norm_out is not None:
        in_specs.append(vec)
        args.append(norm_out)
        out_specs, out_shape = [tile, tile], [out_shape, jax.ShapeDtypeStruct((m, n), BF16)]
    if norm_bwd is not None:
        in_specs += [vec, tile] + ([tile] if has_pre else [])
        args += [norm_bwd[1], norm_bwd[0]] + ([norm_bwd[2]] if has_pre else [])
        out_specs, out_shape = [tile, vec], [out_shape, jax.ShapeDtypeStruct((1, n), F32)]
    return pl.pallas_call(
        body, name=name, grid=(m // tm, n // tn, nk), in_specs=in_specs, out_specs=out_specs, out_shape=out_shape,
        scratch_shapes=[pltpu.VMEM((tm, tn), F32)] if nk > 1 else [],
        compiler_params=(_params("arbitrary", "arbitrary", "arbitrary") if norm_bwd is not None
                         else _params("parallel", "parallel", "arbitrary")),
    )(*args)


def _mm_tn(x, dy, name, out_dtype=BF16, alpha=1.0, dep=None, out_rows=None, row_off=0, into=None,
           tko=1408, tn=1024, tt=2048):
    t, k = x.shape
    n = dy.shape[1]
    tko, tn, tt = _tile(k, tko, LANES), _tile(n, tn, LANES), _tile(t, tt, 8)
    nt_steps = t // tt

    def body(*refs):
        x_ref, dy_ref = refs[:2]
        o_ref, acc_ref = refs[-2:]
        part = _dot(x_ref[...].astype(BF16), dy_ref[...].astype(BF16), TN)
        step = pl.program_id(2)

        @pl.when(step == 0)
        def _():
            acc_ref[...] = part

        @pl.when(step > 0)
        def _():
            acc_ref[...] += part

        @pl.when(step == nt_steps - 1)
        def _():
            acc = acc_ref[...]
            o_ref[...] = (alpha * acc if alpha != 1.0 else acc).astype(out_dtype)

    in_specs = [pl.BlockSpec((tt, tko), lambda i, j, s: (s, i)), pl.BlockSpec((tt, tn), lambda i, j, s: (s, j))]
    args = [x, dy]
    if dep is not None:
        in_specs.append(pl.BlockSpec((8, LANES), lambda i, j, s: (0, 0)))
        args.append(dep)
    aliases = {}
    if into is not None:
        in_specs.append(pl.BlockSpec(memory_space=pl.ANY))
        args.append(into)
        aliases = {len(args) - 1: 0}
    band = row_off // tko
    assert row_off % tko == 0
    return pl.pallas_call(
        body, name=name, grid=(k // tko, n // tn, nt_steps), in_specs=in_specs,
        out_specs=pl.BlockSpec((tko, tn), lambda i, j, s: (i + band, j)),
        out_shape=jax.ShapeDtypeStruct((out_rows or k, n), out_dtype),
        scratch_shapes=[pltpu.VMEM((tko, tn), F32)], input_output_aliases=aliases,
        compiler_params=_params("parallel", "parallel", "arbitrary"),
    )(*args)


def _rms_fwd(x, g, name, tt=512):
    t, d = x.shape
    tt = _tile(t, tt, 8)

    def body(x_ref, g_ref, o_ref):
        xv = x_ref[...]
        r = lax.rsqrt(jnp.mean(xv * xv, axis=-1, keepdims=True) + EPS)
        o_ref[...] = (xv * r * g_ref[...]).astype(BF16)

    return pl.pallas_call(
        body, name=name, grid=(t // tt,),
        in_specs=[pl.BlockSpec((tt, d), lambda i: (i, 0)), pl.BlockSpec((1, d), lambda i: (0, 0))],
        out_specs=pl.BlockSpec((tt, d), lambda i: (i, 0)),
        out_shape=jax.ShapeDtypeStruct((t, d), BF16), compiler_params=_params("parallel"),
    )(x, g)


def _ffn_up(n, w_gu_t, name, tm=512, tf=2816):
    t, d = n.shape
    f = w_gu_t.shape[0] // 2
    tm, tf = _tile(t, tm, 8), _tile(f, tf, LANES)
    nf = f // tf

    def body(n_ref, wg_ref, wu_ref, g_ref, u_ref, a_ref):
        nv = n_ref[...]
        gate, up = _dot(nv, wg_ref[...], NT), _dot(nv, wu_ref[...], NT)
        s = _sigmoid(gate)
        sg = gate * s
        g_ref[...] = (up * (s * (1.0 + gate * (1.0 - s)))).astype(BF16)
        u_ref[...] = sg.astype(BF16)
        a_ref[...] = (sg * up).astype(BF16)

    blk = pl.BlockSpec((tm, tf), lambda i, j: (i, j))
    out = jax.ShapeDtypeStruct((t, f), BF16)
    return pl.pallas_call(
        body, name=name, grid=(t // tm, nf),
        in_specs=[pl.BlockSpec((tm, d), lambda i, j: (i, 0)), pl.BlockSpec((tf, d), lambda i, j: (j, 0)),
                  pl.BlockSpec((tf, d), lambda i, j: (j + nf, 0))],
        out_specs=[blk, blk, blk], out_shape=[out, out, out], compiler_params=_params("parallel", "parallel"),
    )(n, w_gu_t, w_gu_t)


def _ffn_da(dh, w_d, gate, up, name, alpha, dep=None, tm=512, tf=2816):
    t, d = dh.shape
    f = w_d.shape[0]
    tm, tf = _tile(t, tm, 8), _tile(f, tf, LANES)

    def body(*refs):
        dh_ref, w_ref, g_ref, u_ref = refs[:4]
        dg_ref, du_ref = refs[-2:]
        da = alpha * _dot(dh_ref[...].astype(BF16), w_ref[...], NT)
        dg_ref[...] = (da * g_ref[...].astype(F32)).astype(BF16)
        du_ref[...] = (da * u_ref[...].astype(F32)).astype(BF16)

    blk = pl.BlockSpec((tm, tf), lambda i, j: (i, j))
    in_specs = [pl.BlockSpec((tm, d), lambda i, j: (i, 0)), pl.BlockSpec((tf, d), lambda i, j: (j, 0)), blk, blk]
    args = [dh, w_d, gate, up]
    if dep is not None:
        in_specs.append(pl.BlockSpec((8, LANES), lambda i, j: (0, 0)))
        args.append(dep)
    out = jax.ShapeDtypeStruct((t, f), BF16)
    return pl.pallas_call(
        body, name=name, grid=(t // tm, f // tf), in_specs=in_specs, out_specs=[blk, blk], out_shape=[out, out],
        compiler_params=_params("parallel", "parallel"),
    )(*args)


def _merge_fwd(ya, yb, proj, ga_blk, gb_blk, d, name, tt=512):
    t = ya.shape[0]
    tt = _tile(t, tt, 8)

    def body(ya_ref, yb_ref, ga_ref, gb_ref, o_ref):
        o_ref[...] = (_sigmoid(ga_ref[...].astype(F32)) * ya_ref[...].astype(F32)
                      + _sigmoid(gb_ref[...].astype(F32)) * yb_ref[...].astype(F32)).astype(BF16)

    row = pl.BlockSpec((tt, d), lambda i: (i, 0))
    return pl.pallas_call(
        body, name=name, grid=(t // tt,),
        in_specs=[row, row, pl.BlockSpec((tt, d), lambda i: (i, ga_blk)), pl.BlockSpec((tt, d), lambda i: (i, gb_blk))],
        out_specs=row, out_shape=jax.ShapeDtypeStruct((t, d), BF16), compiler_params=_params("parallel"),
    )(ya, yb, proj, proj)


def _merge_bwd(dm, ya, yb, proj, ga_blk, gb_blk, d, name, tt=512):
    t = ya.shape[0]
    tt = _tile(t, tt, 8)

    def body(dm_ref, ya_ref, yb_ref, ga_ref, gb_ref, dya_ref, dyb_ref, dga_ref, dgb_ref):
        dmv = dm_ref[...].astype(F32)
        sa, sb = _sigmoid(ga_ref[...].astype(F32)), _sigmoid(gb_ref[...].astype(F32))
        dya_ref[...] = (dmv * sa).astype(BF16)
        dyb_ref[...] = (dmv * sb).astype(BF16)
        dga_ref[...] = (dmv * ya_ref[...].astype(F32) * sa * (1.0 - sa)).astype(BF16)
        dgb_ref[...] = (dmv * yb_ref[...].astype(F32) * sb * (1.0 - sb)).astype(BF16)

    row = pl.BlockSpec((tt, d), lambda i: (i, 0))
    out = jax.ShapeDtypeStruct((t, d), BF16)
    return pl.pallas_call(
        body, name=name, grid=(t // tt,),
        in_specs=[row, row, row, pl.BlockSpec((tt, d), lambda i: (i, ga_blk)), pl.BlockSpec((tt, d), lambda i: (i, gb_blk))],
        out_specs=[row] * 4, out_shape=[out] * 4, compiler_params=_params("parallel"),
    )(dm, ya, yb, proj, proj)


def _loss_head(h, g, target, name, tt=512):
    t, d = h.shape
    tt = _tile(t, tt, 8)

    def body(h_ref, g_ref, tg_ref, loss_ref, dh_ref, dg_ref):
        xv = h_ref[...]
        r = lax.rsqrt(jnp.mean(xv * xv, axis=-1, keepdims=True) + EPS)
        xh = xv * r
        err = xh * g_ref[...] - tg_ref[...]
        dout = err * (1.0 / d)
        gy = dout * g_ref[...]
        dh_ref[...] = r * (gy - xh * jnp.mean(gy * xh, axis=-1, keepdims=True))
        dg_part = jnp.sum(dout * xh, axis=0, keepdims=True)
        loss_part = jnp.full((1, LANES), 0.5 / d, F32) * jnp.sum(err * err)

        @pl.when(pl.program_id(0) == 0)
        def _():
            dg_ref[...] = dg_part
            loss_ref[...] = loss_part

        @pl.when(pl.program_id(0) > 0)
        def _():
            dg_ref[...] += dg_part
            loss_ref[...] += loss_part

    row = pl.BlockSpec((tt, d), lambda i: (i, 0))
    vec = pl.BlockSpec((1, d), lambda i: (0, 0))
    return pl.pallas_call(
        body, name=name, grid=(t // tt,), in_specs=[row, vec, row],
        out_specs=[pl.BlockSpec((1, LANES), lambda i: (0, 0)), row, vec],
        out_shape=[jax.ShapeDtypeStruct((1, LANES), F32), jax.ShapeDtypeStruct((t, d), F32), jax.ShapeDtypeStruct((1, d), F32)],
        compiler_params=_params("arbitrary"),
    )(h, g, target)


def _shift_down(x, k, t_idx):
    if k == 0:
        return x
    return jnp.where(t_idx >= k, pltpu.roll(x, k, 0), 0.0)


def _shift_up(x, k, t_idx, s):
    if k == 0:
        return x
    return jnp.where(t_idx < s - k, pltpu.roll(x, s - k, 0), 0.0)


def _conv_a_fwd(proj, w, nb, s, d, name, cb=256):
    cb = _tile(d, cb, LANES)
    nd = d // cb

    def body(b_ref, c_ref, v_ref, w_ref, o_ref):
        t_idx = lax.broadcasted_iota(jnp.int32, (s, cb), 0)
        cv = c_ref[...].astype(F32) * v_ref[...].astype(F32)
        cc = sum(w_ref[k:k + 1, :] * _shift_down(cv, CONV_A_K - 1 - k, t_idx) for k in range(CONV_A_K))
        o_ref[...] = (b_ref[...].astype(F32) * cc).astype(BF16)

    def col(off):
        return pl.BlockSpec((s, cb), lambda b, j: (b, j + off * nd))

    return pl.pallas_call(
        body, name=name, grid=(nb, nd), in_specs=[col(0), col(1), col(2), pl.BlockSpec((8, cb), lambda b, j: (0, j))],
        out_specs=pl.BlockSpec((s, cb), lambda b, j: (b, j)),
        out_shape=jax.ShapeDtypeStruct((nb * s, d), BF16), compiler_params=_params("parallel", "parallel"),
    )(proj, proj, proj, w)


def _conv_a_bwd(dy, proj, w, nb, s, d, name, cb=256):
    cb = _tile(d, cb, LANES)
    nd = d // cb

    def body(dy_ref, b_ref, c_ref, v_ref, w_ref, db_ref, dc_ref, dv_ref, dw_ref):
        t_idx = lax.broadcasted_iota(jnp.int32, (s, cb), 0)
        cv_c, cv_v = c_ref[...].astype(F32), v_ref[...].astype(F32)
        cv = cv_c * cv_v
        shifted = [_shift_down(cv, CONV_A_K - 1 - k, t_idx) for k in range(CONV_A_K)]
        cc = sum(w_ref[k:k + 1, :] * shifted[k] for k in range(CONV_A_K))
        dyv = dy_ref[...].astype(F32)
        db_ref[...] = (dyv * cc).astype(BF16)
        dcc = dyv * b_ref[...].astype(F32)
        dcv = sum(w_ref[k:k + 1, :] * _shift_up(dcc, CONV_A_K - 1 - k, t_idx, s) for k in range(CONV_A_K))
        dc_ref[...] = (dcv * cv_v).astype(BF16)
        dv_ref[...] = (dcv * cv_c).astype(BF16)
        rows = [jnp.sum(dcc * shifted[k], axis=0, keepdims=True) for k in range(CONV_A_K)]
        part = _stack_rows(rows, cb)

        @pl.when(pl.program_id(1) == 0)
        def _():
            dw_ref[...] = part

        @pl.when(pl.program_id(1) > 0)
        def _():
            dw_ref[...] += part

    def col(off):
        return pl.BlockSpec((s, cb), lambda j, b: (b, j + off * nd))

    own = pl.BlockSpec((s, cb), lambda j, b: (b, j))
    wspec = pl.BlockSpec((8, cb), lambda j, b: (0, j))
    out = jax.ShapeDtypeStruct((nb * s, d), BF16)
    return pl.pallas_call(
        body, name=name, grid=(nd, nb), in_specs=[own, col(0), col(1), col(2), wspec],
        out_specs=[own, own, own, wspec], out_shape=[out, out, out, jax.ShapeDtypeStruct((8, d), F32)],
        compiler_params=_params("parallel", "arbitrary"),
    )(dy, proj, proj, proj, w)


def _conv_s_fwd(proj, col0, w, bias, nb, s, cc_width, name, cb=256):
    cb = _tile(math.gcd(cc_width, col0) if col0 else cc_width, cb, LANES)
    nd, off = cc_width // cb, col0 // cb

    def body(x_ref, w_ref, b_ref, o_ref, pre_ref):
        t_idx = lax.broadcasted_iota(jnp.int32, (s, cb), 0)
        xv = x_ref[...].astype(F32)
        pre = b_ref[...] + sum(w_ref[k:k + 1, :] * _shift_down(xv, SSM_CONV_K - 1 - k, t_idx) for k in range(SSM_CONV_K))
        o_ref[...] = _silu(pre).astype(BF16)
        pre_ref[...] = pre.astype(BF16)

    vec = pl.BlockSpec((8, cb), lambda b, j: (0, j))
    own = pl.BlockSpec((s, cb), lambda b, j: (b, j))
    out = jax.ShapeDtypeStruct((nb * s, cc_width), BF16)
    return pl.pallas_call(
        body, name=name, grid=(nb, nd),
        in_specs=[pl.BlockSpec((s, cb), lambda b, j: (b, j + off)), vec, pl.BlockSpec((1, cb), lambda b, j: (0, j))],
        out_specs=[own, own], out_shape=[out, out], compiler_params=_params("parallel", "parallel"),
    )(proj, w, bias)


def _conv_s_bwd(dxc, pre, proj, col0, w, nb, s, cc_width, name, cb=256):
    cb = _tile(math.gcd(cc_width, col0) if col0 else cc_width, cb, LANES)
    nd, off = cc_width // cb, col0 // cb

    def body(d_ref, pre_ref, x_ref, w_ref, dx_ref, dw_ref, db_ref):
        t_idx = lax.broadcasted_iota(jnp.int32, (s, cb), 0)
        xv = x_ref[...].astype(F32)
        dpre = d_ref[...].astype(F32) * _dsilu(pre_ref[...].astype(F32))
        ahead = [_shift_up(dpre, j, t_idx, s) for j in range(SSM_CONV_K)]
        dx_ref[...] = sum(w_ref[k:k + 1, :] * ahead[SSM_CONV_K - 1 - k] for k in range(SSM_CONV_K)).astype(BF16)
        rows = [jnp.sum(ahead[SSM_CONV_K - 1 - k] * xv, axis=0, keepdims=True) for k in range(SSM_CONV_K)]
        dw_part = _stack_rows(rows, cb)
        db_part = jnp.sum(dpre, axis=0, keepdims=True)

        @pl.when(pl.program_id(1) == 0)
        def _():
            dw_ref[...] = dw_part
            db_ref[...] = db_part

        @pl.when(pl.program_id(1) > 0)
        def _():
            dw_ref[...] += dw_part
            db_ref[...] += db_part

    own = pl.BlockSpec((s, cb), lambda j, b: (b, j))
    wspec = pl.BlockSpec((8, cb), lambda j, b: (0, j))
    bspec = pl.BlockSpec((1, cb), lambda j, b: (0, j))
    return pl.pallas_call(
        body, name=name, grid=(nd, nb),
        in_specs=[own, own, pl.BlockSpec((s, cb), lambda j, b: (b, j + off)), wspec],
        out_specs=[own, wspec, bspec],
        out_shape=[jax.ShapeDtypeStruct((nb * s, cc_width), BF16), jax.ShapeDtypeStruct((8, cc_width), F32),
                   jax.ShapeDtypeStruct((1, cc_width), F32)],
        compiler_params=_params("parallel", "arbitrary"),
    )(dxc, pre, proj, w)


def _split3(v):
    hi = v.astype(BF16)
    r1 = v - hi.astype(F32)
    mid = r1.astype(BF16)
    return hi, mid, (r1 - mid.astype(F32)).astype(BF16)


def _exact_left(mask_b, v):
    return sum(_dot(mask_b, t) for t in _split3(v))


def _exact_right(v, mask_b):
    return sum(_dot(t, mask_b) for t in _split3(v))


def _head_sums(v, e_b):
    return _dot(v.astype(BF16), e_b, NT)


def _spread(v, out_ref, di):
    lane = lax.broadcasted_iota(jnp.int32, (v.shape[0], LANES), 1)
    for pr in range(di // LANES):
        h0 = pr * (LANES // SSM_HEAD_DIM)
        out_ref[:, pr * LANES:(pr + 1) * LANES] = jnp.where(lane < SSM_HEAD_DIM, v[:, h0:h0 + 1], v[:, h0 + 1:h0 + 2])


def _ssd_common(xc_ref, dtr_ref, dtrt_ref, prow_ref, pcol_ref, dtx_ref, acsx_ref, dx_ref, di):
    l = SSM_CHUNK
    bias_r, a_r = prow_ref[0:1, :], -jnp.exp(prow_ref[1:2, :])
    sp_in = dtr_ref[...] + bias_r
    dt = _softplus(sp_in)
    li = lax.broadcasted_iota(jnp.int32, (l, l), 0)
    si = lax.broadcasted_iota(jnp.int32, (l, l), 1)
    lower_b = (li >= si).astype(BF16)
    upper_b = (li <= si).astype(BF16)
    acs = _exact_left(lower_b, dt * a_r)
    bias_c, a_c = pcol_ref[:, 0:1], -jnp.exp(pcol_ref[:, 1:2])
    dt_t = _softplus(dtrt_ref[...] + bias_c)
    acs_t = _exact_right(dt_t * a_c, upper_b)
    _spread(dt, dtx_ref, di)
    _spread(acs, acsx_ref, di)
    _spread(prow_ref[0:8, :], dx_ref, di)
    acs_exp = acsx_ref[...]
    acs_last = acs_exp[l - 1:l, :]
    x = xc_ref[:, 0:di].astype(F32)
    return dict(dt=dt, a_r=a_r, sp_in=sp_in, acs=acs, acs_t=acs_t, dt_exp=dtx_ref[...], e_exp=jnp.exp(acs_exp),
                el_exp=jnp.exp(acs_last), f_exp=jnp.exp(acs_last - acs_exp), x=x, mask=li >= si, upper_b=upper_b,
                d_exp=dx_ref[2:3, :])


def _decay(q, h):
    seg = q["acs"][:, h:h + 1] - q["acs_t"][h:h + 1, :]
    return jnp.exp(jnp.where(q["mask"], seg, NEG_BIG))


def _ssd_fwd(xc, dtr, dtrt, prow, pcol, nb, nc, di, name):
    l, n, g_n, p = SSM_CHUNK, SSM_STATE, SSM_GROUPS, SSM_HEAD_DIM
    cc = xc.shape[1]
    gw = di // g_n
    assert p * 2 == LANES and gw % LANES == 0

    def body(xc_ref, dtr_ref, dtrt_ref, prow_ref, pcol_ref, y_ref, sprev_ref, st_ref, dtx_ref, acsx_ref, dx_ref):
        @pl.when(pl.program_id(1) == 0)
        def _():
            st_ref[...] = jnp.zeros_like(st_ref)

        q = _ssd_common(xc_ref, dtr_ref, dtrt_ref, prow_ref, pcol_ref, dtx_ref, acsx_ref, dx_ref, di)
        x = q["x"]
        xd = x * q["dt_exp"]
        xdb = xd.astype(BF16)
        xdf = (xd * q["f_exp"]).astype(BF16)
        lane = lax.broadcasted_iota(jnp.int32, (l, LANES), 1)
        for g in range(g_n):
            lo = g * gw
            bg = xc_ref[:, di + g * n: di + (g + 1) * n]
            cg = xc_ref[:, di + g_n * n + g * n: di + g_n * n + (g + 1) * n]
            cb = _dot(cg, bg, NT)
            st_g = st_ref[:, lo:lo + gw]
            y_off = q["e_exp"][:, lo:lo + gw] * _dot(cg, st_g.astype(BF16))
            for pr in range(gw // LANES):
                c0 = lo + pr * LANES
                h0 = c0 // p
                xp = xdb[:, c0:c0 + LANES]
                m0 = (cb * _decay(q, h0)).astype(BF16)
                m1 = (cb * _decay(q, h0 + 1)).astype(BF16)
                yd = _dot(m0, jnp.where(lane < p, xp, 0)) + _dot(m1, jnp.where(lane >= p, xp, 0))
                y_ref[:, c0:c0 + LANES] = (yd + y_off[:, pr * LANES:(pr + 1) * LANES]
                                           + q["d_exp"][:, c0:c0 + LANES] * x[:, c0:c0 + LANES])
            sprev_ref[:, lo:lo + gw] = st_g
            st_ref[:, lo:lo + gw] = q["el_exp"][:, lo:lo + gw] * st_g + _dot(bg, xdf[:, lo:lo + gw], TN)

    tok = lambda w: pl.BlockSpec((l, w), lambda b, c: (b * nc + c, 0))
    const = lambda r, w: pl.BlockSpec((r, w), lambda b, c: (0, 0))
    return pl.pallas_call(
        body, name=name, grid=(nb, nc),
        in_specs=[tok(cc), tok(LANES), pl.BlockSpec((LANES, l), lambda b, c: (0, b * nc + c)),
                  const(8, LANES), const(LANES, 8)],
        out_specs=[tok(di), pl.BlockSpec((None, n, di), lambda b, c: (b * nc + c, 0, 0))],
        out_shape=[jax.ShapeDtypeStruct((nb * nc * l, di), F32), jax.ShapeDtypeStruct((nb * nc, n, di), F32)],
        scratch_shapes=[pltpu.VMEM((n, di), F32), pltpu.VMEM((l, di), F32), pltpu.VMEM((l, di), F32),
                        pltpu.VMEM((8, di), F32)],
        compiler_params=_params("parallel", "arbitrary"),
    )(xc, dtr, dtrt, prow, pcol)


def _ssd_bwd(dy, xc, dtr, dtrt, prow, pcol, e_mat, sprev, nb, nc, di, name):
    l, n, g_n, p = SSM_CHUNK, SSM_STATE, SSM_GROUPS, SSM_HEAD_DIM
    cc = xc.shape[1]
    gw = di // g_n

    def body(dy_ref, xc_ref, dtr_ref, dtrt_ref, prow_ref, pcol_ref, e_ref, sprev_ref,
             dxc_ref, ddtr_ref, sums_ref, dst_ref, off_ref, dxd_ref, last_ref, vst_ref,
             dtx_ref, acsx_ref, dx_ref):
        first = jnp.logical_and(pl.program_id(0) == 0, pl.program_id(1) == 0)

        @pl.when(pl.program_id(1) == 0)
        def _():
            dst_ref[...] = jnp.zeros_like(dst_ref)

        head_row = lax.broadcasted_iota(jnp.int32, (LANES, l), 0)
        row_sums, col_sums = jnp.zeros((l, LANES), F32), jnp.zeros((LANES, l), F32)
        strict_lower = lax.broadcasted_iota(jnp.int32, (l, l), 0) > lax.broadcasted_iota(jnp.int32, (l, l), 1)

        q = _ssd_common(xc_ref, dtr_ref, dtrt_ref, prow_ref, pcol_ref, dtx_ref, acsx_ref, dx_ref, di)
        x = q["x"]
        xd = x * q["dt_exp"]
        xdb = xd.astype(BF16)
        xdf = (xd * q["f_exp"]).astype(BF16)
        dyv = dy_ref[...]
        dyb = dyv.astype(BF16)
        dye = (dyv * q["e_exp"]).astype(BF16)
        upper_b = q["upper_b"]
        lane = lax.broadcasted_iota(jnp.int32, (l, LANES), 1)
        for g in range(g_n):
            lo = g * gw
            bg = xc_ref[:, di + g * n: di + (g + 1) * n]
            cg = xc_ref[:, di + g_n * n + g * n: di + g_n * n + (g + 1) * n]
            cb = _dot(cg, bg, NT)
            st_g = sprev_ref[:, lo:lo + gw]
            st_gb = st_g.astype(BF16)
            dst_g = dst_ref[:, lo:lo + gw]
            dst_gb = dst_g.astype(BF16)
            dye_g = dye[:, lo:lo + gw]
            xdf_g = xdf[:, lo:lo + gw]
            y_off = q["e_exp"][:, lo:lo + gw] * _dot(cg, st_gb)
            dc_g = _dot(dye_g, st_gb, NT)
            db_g = _dot(xdf_g, dst_gb, NT)
            dxd_state = _dot(bg, dst_gb) * q["f_exp"][:, lo:lo + gw]
            last_ref[:, lo:lo + gw] = jnp.sum(dst_g * st_g, axis=0, keepdims=True)
            dst_ref[:, lo:lo + gw] = q["el_exp"][:, lo:lo + gw] * dst_g + _dot(cg, dye_g, TN)
            off_ref[:, lo:lo + gw] = dyv[:, lo:lo + gw] * y_off
            vst_ref[:, lo:lo + gw] = xd[:, lo:lo + gw] * dxd_state
            dcb = jnp.zeros((l, l), F32)
            for pr in range(gw // LANES):
                c0 = lo + pr * LANES
                h0 = c0 // p
                xp = xdb[:, c0:c0 + LANES]
                dyp = dyb[:, c0:c0 + LANES]
                dxd_diag = jnp.zeros((l, LANES), F32)
                for k, keep in enumerate((lane < p, lane >= p)):
                    dec = _decay(q, h0 + k)
                    dy_h = jnp.where(keep, dyp, 0)
                    dm_dec = _dot(dy_h, xp, NT) * dec
                    dcb = dcb + dm_dec
                    dxd_diag = dxd_diag + _dot((cb * dec).astype(BF16), dy_h, TN)
                    qm = dm_dec * cb
                    row_sums = jnp.where(lane == h0 + k, jnp.sum(qm, axis=1, keepdims=True), row_sums)
                    col_sums = jnp.where(head_row == h0 + k, jnp.sum(qm, axis=0, keepdims=True), col_sums)
                dxd_ref[:, c0:c0 + LANES] = dxd_diag + dxd_state[:, pr * LANES:(pr + 1) * LANES]
            dcb_b = dcb.astype(BF16)
            dxc_ref[:, di + g * n: di + (g + 1) * n] = (db_g + _dot(dcb_b, cg, TN)).astype(BF16)
            dxc_ref[:, di + g_n * n + g * n: di + g_n * n + (g + 1) * n] = (dc_g + _dot(dcb_b, bg)).astype(BF16)
        dxd = dxd_ref[...]
        e_b = e_ref[...]
        from_y = _exact_left(upper_b, _head_sums(off_ref[...], e_b) + row_sums - col_sums.T)
        from_s = _exact_left(strict_lower.astype(BF16), _head_sums(vst_ref[...], e_b))
        carried = _head_sums(jnp.broadcast_to(last_ref[...], (8, di)), e_b)[0:1, :] * jnp.exp(q["acs"][l - 1:l, :])
        dla = from_y + from_s + carried
        ddt = dla * q["a_r"] + _head_sums(dxd * x, e_b)
        ddtr = ddt * jax.nn.sigmoid(q["sp_in"])
        ddtr_ref[...] = ddtr
        dxc_ref[:, 0:di] = (dxd * q["dt_exp"] + q["d_exp"] * dyv).astype(BF16)
        dd_exp = jnp.sum(dyv * x, axis=0, keepdims=True)
        dd = _head_sums(jnp.broadcast_to(dd_exp, (8, di)), e_b)[0:1, :]
        part = _stack_rows([jnp.sum(ddtr, axis=0, keepdims=True),
                            jnp.sum(dla * q["dt"], axis=0, keepdims=True) * q["a_r"], dd], LANES)

        @pl.when(first)
        def _():
            sums_ref[...] = part

        @pl.when(jnp.logical_not(first))
        def _():
            sums_ref[...] += part

    rev = lambda b, c: b * nc + (nc - 1 - c)
    tok = lambda w: pl.BlockSpec((l, w), lambda b, c: (rev(b, c), 0))
    const = lambda r, w: pl.BlockSpec((r, w), lambda b, c: (0, 0))
    return pl.pallas_call(
        body, name=name, grid=(nb, nc),
        in_specs=[tok(di), tok(cc), tok(LANES), pl.BlockSpec((LANES, l), lambda b, c: (0, rev(b, c))),
                  const(8, LANES), const(LANES, 8), const(LANES, di),
                  pl.BlockSpec((None, n, di), lambda b, c: (rev(b, c), 0, 0))],
        out_specs=[tok(cc), tok(LANES), const(8, LANES)],
        out_shape=[jax.ShapeDtypeStruct((nb * nc * l, cc), BF16), jax.ShapeDtypeStruct((nb * nc * l, LANES), F32),
                   jax.ShapeDtypeStruct((8, LANES), F32)],
        scratch_shapes=[pltpu.VMEM((n, di), F32), pltpu.VMEM((l, di), F32), pltpu.VMEM((l, di), F32),
                        pltpu.VMEM((1, di), F32), pltpu.VMEM((l, di), F32),
                        pltpu.VMEM((l, di), F32), pltpu.VMEM((l, di), F32), pltpu.VMEM((8, di), F32)],
        compiler_params=_params("arbitrary", "arbitrary"),
    )(dy, xc, dtr, dtrt, prow, pcol, e_mat, sprev)


def _gate_norm_fwd(y, proj, z_col0, norm_g, di, name, tt=256):
    t = y.shape[0]
    tt = _tile(t, tt, 8)
    gw = di // SSM_GROUPS
    zw = _tile(math.gcd(di, z_col0), di, LANES)
    nz, zoff = di // zw, z_col0 // zw

    def body(*refs):
        y_ref, z_refs, g_ref, o_ref = refs[0], refs[1:1 + nz], refs[1 + nz], refs[2 + nz]
        for g in range(SSM_GROUPS):
            lo = g * gw
            zv = z_refs[lo // zw][:, lo % zw:lo % zw + gw].astype(F32)
            yg = y_ref[:, lo:lo + gw] * _silu(zv)
            r = lax.rsqrt(jnp.mean(yg * yg, axis=-1, keepdims=True) + EPS)
            o_ref[:, lo:lo + gw] = (yg * r * g_ref[:, lo:lo + gw]).astype(BF16)

    row = pl.BlockSpec((tt, di), lambda i: (i, 0))
    zspecs = [pl.BlockSpec((tt, zw), functools.partial(lambda i, k: (i, zoff + k), k=k)) for k in range(nz)]
    return pl.pallas_call(
        body, name=name, grid=(t // tt,), in_specs=[row] + zspecs + [pl.BlockSpec((1, di), lambda i: (0, 0))],
        out_specs=row, out_shape=jax.ShapeDtypeStruct((t, di), BF16), compiler_params=_params("parallel"),
    )(y, *([proj] * nz), norm_g)


def _gate_norm_bwd(dn, y, proj, z_col0, norm_g, di, name, tt=256):
    t = y.shape[0]
    tt = _tile(t, tt, 8)
    gw = di // SSM_GROUPS
    zw = _tile(math.gcd(di, z_col0), di, LANES)
    nz, zoff = di // zw, z_col0 // zw

    def body(*refs):
        dn_ref, y_ref, z_refs, g_ref = refs[0], refs[1], refs[2:2 + nz], refs[2 + nz]
        dy_ref, dz_ref, dg_ref = refs[3 + nz:]
        first = pl.program_id(0) == 0
        for g in range(SSM_GROUPS):
            lo = g * gw
            zv = z_refs[lo // zw][:, lo % zw:lo % zw + gw].astype(F32)
            yv = y_ref[:, lo:lo + gw]
            sz = _silu(zv)
            yg = yv * sz
            r = lax.rsqrt(jnp.mean(yg * yg, axis=-1, keepdims=True) + EPS)
            yh = yg * r
            dnv = dn_ref[:, lo:lo + gw].astype(F32)
            gy = dnv * g_ref[:, lo:lo + gw]
            dyg = r * (gy - yh * jnp.mean(gy * yh, axis=-1, keepdims=True))
            dy_ref[:, lo:lo + gw] = dyg * sz
            dz_ref[:, lo:lo + gw] = (dyg * yv * _dsilu(zv)).astype(BF16)
            part = jnp.sum(dnv * yh, axis=0, keepdims=True)

            @pl.when(first)
            def _():
                dg_ref[:, lo:lo + gw] = part

            @pl.when(jnp.logical_not(first))
            def _():
                dg_ref[:, lo:lo + gw] += part

    row = pl.BlockSpec((tt, di), lambda i: (i, 0))
    vec = pl.BlockSpec((1, di), lambda i: (0, 0))
    zspecs = [pl.BlockSpec((tt, zw), functools.partial(lambda i, k: (i, zoff + k), k=k)) for k in range(nz)]
    return pl.pallas_call(
        body, name=name, grid=(t // tt,), in_specs=[row, row] + zspecs + [vec], out_specs=[row, row, vec],
        out_shape=[jax.ShapeDtypeStruct((t, di), F32), jax.ShapeDtypeStruct((t, di), BF16), jax.ShapeDtypeStruct((1, di), F32)],
        compiler_params=_params("arbitrary"),
    )(dn, y, *([proj] * nz), norm_g)


def _softmax_rows(s):
    s = s - jnp.max(s, axis=-1, keepdims=True)
    e = jnp.exp(s)
    return e * (1.0 / jnp.sum(e, axis=-1, keepdims=True))


def _xattn_fwd(q, kv, nb, s, m, d, name, tq=1024):
    tq = _tile(s, tq, 8)
    nq = s // tq
    hd = d // XATTN_HEADS
    scale = 1.0 / math.sqrt(hd)

    def body(q_ref, k_ref, v_ref, o_ref):
        for h in range(XATTN_HEADS):
            sl = slice(h * hd, (h + 1) * hd)
            prob = _softmax_rows(_dot(q_ref[:, sl], k_ref[:, sl], NT) * scale)
            o_ref[:, sl] = _dot(prob.astype(BF16), v_ref[:, sl]).astype(BF16)

    return pl.pallas_call(
        body, name=name, grid=(nb, nq),
        in_specs=[pl.BlockSpec((tq, d), lambda b, i: (b * nq + i, 0)), pl.BlockSpec((m, d), lambda b, i: (b, 0)),
                  pl.BlockSpec((m, d), lambda b, i: (b, 1))],
        out_specs=pl.BlockSpec((tq, d), lambda b, i: (b * nq + i, 0)),
        out_shape=jax.ShapeDtypeStruct((nb * s, d), BF16), compiler_params=_params("parallel", "parallel"),
    )(q, kv, kv)


def _xattn_bwd(do, q, kv, nb, s, m, d, name, tq=1024):
    tq = _tile(s, tq, 8)
    nq = s // tq
    hd = d // XATTN_HEADS
    scale = 1.0 / math.sqrt(hd)

    def body(do_ref, q_ref, k_ref, v_ref, dq_ref, dk_ref, dv_ref):
        first = pl.program_id(1) == 0
        for h in range(XATTN_HEADS):
            sl = slice(h * hd, (h + 1) * hd)
            qh, kh, vh, doh = q_ref[:, sl], k_ref[:, sl], v_ref[:, sl], do_ref[:, sl]
            prob = _softmax_rows(_dot(qh, kh, NT) * scale)
            dv_h = _dot(prob.astype(BF16), doh, TN)
            dp = _dot(doh, vh, NT)
            ds = (prob * (dp - jnp.sum(dp * prob, axis=-1, keepdims=True)) * scale).astype(BF16)
            dq_ref[:, sl] = _dot(ds, kh).astype(BF16)
            dk_h = _dot(ds, qh, TN)

            @pl.when(first)
            def _():
                dk_ref[:, sl] = dk_h
                dv_ref[:, sl] = dv_h

            @pl.when(jnp.logical_not(first))
            def _():
                dk_ref[:, sl] += dk_h
                dv_ref[:, sl] += dv_h

    qspec = pl.BlockSpec((tq, d), lambda b, i: (b * nq + i, 0))
    dq, dk, dv = pl.pallas_call(
        body, name=name, grid=(nb, nq),
        in_specs=[qspec, qspec, pl.BlockSpec((m, d), lambda b, i: (b, 0)), pl.BlockSpec((m, d), lambda b, i: (b, 1))],
        out_specs=[qspec, pl.BlockSpec((m, d), lambda b, i: (b, 0)), pl.BlockSpec((m, d), lambda b, i: (b, 0))],
        out_shape=[jax.ShapeDtypeStruct((nb * s, d), BF16), jax.ShapeDtypeStruct((nb * m, d), F32),
                   jax.ShapeDtypeStruct((nb * m, d), F32)],
        compiler_params=_params("parallel", "arbitrary"),
    )(do, q, kv, kv)
    return dq, dk, dv


def _all_gather(shards, name):
    n_arr = len(shards)

    def body(*refs):
        x_refs, out_refs = refs[:n_arr], refs[n_arr:2 * n_arr]
        send_sems, recv_sems, local_sems = refs[2 * n_arr:]
        x, y, c = lax.axis_index("x"), lax.axis_index("y"), lax.axis_index("c")
        me, sibling = (x, y, c), (x, y, 1 - c)
        chips = [(1 - x, y), (x, 1 - y), (1 - x, 1 - y)]

        def copy(w, k, block, to, from_input=False):
            px, py, pc = block
            rows = out_refs[w].at[4 * px + 2 * py + pc]
            return pltpu.make_async_remote_copy(
                src_ref=x_refs[w] if from_input else rows, dst_ref=rows,
                send_sem=send_sems.at[7 * w + k], recv_sem=recv_sems.at[7 * w + k], device_id=to, device_id_type=MESH)

        started = []
        for w in range(n_arr):
            mine = pltpu.make_async_copy(x_refs[w], out_refs[w].at[4 * x + 2 * y + c], local_sems.at[w])
            mine.start()
            started.append(mine)
        sends = []
        for w in range(n_arr):
            sends.append(copy(w, 0, me, sibling, from_input=True))
            sends += [copy(w, 1 + j, me, (*chip, c), from_input=True) for j, chip in enumerate(chips)]
        for cp in sends:
            cp.start()
        for j, chip in enumerate(chips):
            for w in range(n_arr):
                copy(w, 1 + j, (*chip, c), me).wait_recv()
                passed = copy(w, 4 + j, (*chip, c), sibling)
                passed.start()
                sends.append(passed)
        for w in range(n_arr):
            copy(w, 0, sibling, me).wait_recv()
            for j, chip in enumerate(chips):
                copy(w, 4 + j, (*chip, 1 - c), me).wait_recv()
        for cp in sends:
            cp.wait_send()
        for mine in started:
            mine.wait()

    hbm = pl.BlockSpec(memory_space=pl.ANY)
    return pl.pallas_call(
        body, name=name, out_shape=[jax.ShapeDtypeStruct((N_DEV,) + s.shape, s.dtype) for s in shards],
        in_specs=[hbm] * n_arr, out_specs=[hbm] * n_arr,
        scratch_shapes=[pltpu.SemaphoreType.DMA((7 * n_arr,)), pltpu.SemaphoreType.DMA((7 * n_arr,)),
                        pltpu.SemaphoreType.DMA((n_arr,))],
    )(*shards)


_HBM = pl.BlockSpec(memory_space=pltpu.HBM)
_SEM = pl.BlockSpec(memory_space=pltpu.SEMAPHORE)
_DATAFLOW = pltpu.SideEffectType.DATAFLOW_SIDE_EFFECTING


def _peer_list(x, y, c):
    return [(1 - x if k & 4 else x, 1 - y if k & 2 else y, 1 - c if k & 1 else c) for k in range(1, N_DEV)]


def _push_copy(src_ref, land_ref, send_sems, recv_sems, w, k, peer, me, per_peer_src, receiving):
    px, py, pc = peer
    peer_slot = 4 * px + 2 * py + pc
    return pltpu.make_async_remote_copy(
        src_ref=src_ref.at[peer_slot] if per_peer_src else src_ref,
        dst_ref=land_ref.at[peer_slot if receiving else me],
        send_sem=send_sems.at[7 * w + k], recv_sem=recv_sems.at[7 * w + k], device_id=peer, device_id_type=MESH)


def _push_start(srcs, per_peer_src, after, name):
    n_arr = len(srcs)
    land_shapes = [s.shape if per_peer_src else (N_DEV,) + s.shape for s in srcs]

    def body(*refs):
        src_refs, land_refs = refs[:n_arr], refs[n_arr:2 * n_arr]
        send_sems, recv_sems = refs[2 * n_arr + 1], refs[2 * n_arr + 2]
        token = refs[-1]
        x, y, c = lax.axis_index("x"), lax.axis_index("y"), lax.axis_index("c")
        me = 4 * x + 2 * y + c
        for w in range(n_arr):
            for k, peer in enumerate(_peer_list(x, y, c)):
                _push_copy(src_refs[w], land_refs[w], send_sems, recv_sems, w, k, peer, me, per_peer_src, False).start()
        token[...] = jnp.zeros_like(token)

    lands = [pltpu.with_memory_space_constraint(lax.empty(ls, s.dtype), pltpu.HBM) for ls, s in zip(land_shapes, srcs)]
    srcs_hbm = [pltpu.with_memory_space_constraint(s, pltpu.HBM) for s in srcs]
    out = pl.pallas_call(
        body, name=name,
        out_shape=(pltpu.SemaphoreType.DMA((7 * n_arr,)), pltpu.SemaphoreType.DMA((7 * n_arr,)),
                   *[pltpu.HBM(s.shape, s.dtype) for s in srcs], *[pltpu.HBM(ls, s.dtype) for ls, s in zip(land_shapes, srcs)],
                   jax.ShapeDtypeStruct((8, LANES), F32)),
        in_specs=[_HBM] * (2 * n_arr) + [pl.BlockSpec(memory_space=pl.ANY)],
        out_specs=(_SEM, _SEM, *([_HBM] * (2 * n_arr)), pl.BlockSpec(memory_space=pltpu.VMEM)),
        input_output_aliases={i: 2 + i for i in range(2 * n_arr)},
        compiler_params=pltpu.CompilerParams(has_side_effects=_DATAFLOW),
    )(*srcs_hbm, *lands, after)
    return dict(send=out[0], recv=out[1], srcs=list(out[2:2 + n_arr]), lands=list(out[2 + n_arr:2 + 2 * n_arr]),
                token=out[-1])


def _push_wait(pending, per_peer_src, after, name):
    n_arr = len(pending["srcs"])

    def body(*refs):
        src_refs, land_refs = refs[:n_arr], refs[n_arr:2 * n_arr]
        send_sems, recv_sems = refs[2 * n_arr], refs[2 * n_arr + 1]
        x, y, c = lax.axis_index("x"), lax.axis_index("y"), lax.axis_index("c")
        me = 4 * x + 2 * y + c
        for w in range(n_arr):
            for k, peer in enumerate(_peer_list(x, y, c)):
                cp = _push_copy(src_refs[w], land_refs[w], send_sems, recv_sems, w, k, peer, me, per_peer_src, True)
                cp.wait_send()
                cp.wait_recv()

    out = pl.pallas_call(
        body, name=name,
        out_shape=tuple(pltpu.HBM(a.shape, a.dtype) for a in pending["srcs"] + pending["lands"]),
        in_specs=[_HBM] * (2 * n_arr) + [_SEM, _SEM, pl.BlockSpec(memory_space=pl.ANY)],
        out_specs=tuple([_HBM] * (2 * n_arr)),
        input_output_aliases={i: i for i in range(2 * n_arr)},
        compiler_params=pltpu.CompilerParams(has_side_effects=_DATAFLOW),
    )(*pending["srcs"], *pending["lands"], pending["send"], pending["recv"], after)
    return list(out[:n_arr]), list(out[n_arr:])


def _adamw_math(w, g, m, v):
    m = ADAM_B1 * m + (1.0 - ADAM_B1) * g
    v = ADAM_B2 * v + (1.0 - ADAM_B2) * (g * g)
    m_hat = m / (1.0 - ADAM_B1 ** ADAM_STEP)
    v_hat = v / (1.0 - ADAM_B2 ** ADAM_STEP)
    delta = -ADAM_LR * (m_hat / (jnp.sqrt(v_hat) + ADAM_EPS) + ADAM_WD * w)
    return delta, m, v


def _sum8(parts, name, tr=512):
    _, r, c_dim = parts.shape
    tr = _tile(r, tr, BF16_SUBLANES)

    def body(p_ref, o_ref):
        acc = p_ref[0].astype(F32)
        for k in range(1, N_DEV):
            acc = acc + p_ref[k].astype(F32)
        o_ref[...] = acc

    return pl.pallas_call(
        body, name=name, grid=(r // tr,), in_specs=[pl.BlockSpec((N_DEV, tr, c_dim), lambda i: (0, i, 0))],
        out_specs=pl.BlockSpec((tr, c_dim), lambda i: (i, 0)),
        out_shape=jax.ShapeDtypeStruct((r, c_dim), F32), compiler_params=_params("parallel"),
    )(parts)


def _sum8_adamw(parts, w, m, v, name, tr=128):
    _, r, c_dim = parts.shape
    tr = _tile(r, tr, BF16_SUBLANES)
    tc = c_dim if tr <= 2 * LANES else _tile(c_dim, LANES, LANES)

    def body(p_ref, w_ref, m_ref, v_ref, g_ref, d_ref, nm_ref, nv_ref):
        g = p_ref[0].astype(F32)
        for k in range(1, N_DEV):
            g = g + p_ref[k].astype(F32)
        g_ref[...] = g
        d_ref[...], nm_ref[...], nv_ref[...] = _adamw_math(w_ref[...], g, m_ref[...], v_ref[...])

    blk = pl.BlockSpec((None, tr, tc), lambda i, j: (0, i, j))
    out = jax.ShapeDtypeStruct((1, r, c_dim), F32)
    return pl.pallas_call(
        body, name=name, grid=(r // tr, c_dim // tc),
        in_specs=[pl.BlockSpec((N_DEV, tr, tc), lambda i, j: (0, i, j)), blk, blk, blk],
        out_specs=[blk] * 4, out_shape=[out] * 4, compiler_params=_params("parallel", "parallel"),
    )(parts, w, m, v)


def _adamw(g, w, m, v, name):
    r, c_dim = g.shape

    def body(g_ref, w_ref, m_ref, v_ref, d_ref, nm_ref, nv_ref):
        d_ref[...], nm_ref[...], nv_ref[...] = _adamw_math(w_ref[...], g_ref[...], m_ref[...], v_ref[...])

    out = jax.ShapeDtypeStruct((r, c_dim), F32)
    return pl.pallas_call(body, name=name, out_shape=[out] * 3)(g, w, m, v)


def _pack_rows(arrays, dtype, row_unit):
    chunks, offs, r0 = [], [], 0
    for a in arrays:
        flat = a.reshape(-1).astype(dtype)
        rows = -(-flat.shape[0] // (LANES * row_unit)) * row_unit
        flat = jnp.pad(flat, (0, rows * LANES - flat.shape[0]))
        chunks.append(flat.reshape(rows, LANES))
        offs.append((r0, rows))
        r0 += rows
    return jnp.concatenate(chunks, axis=0), offs


def _unpack_rows(packed, offs, shapes):
    out = []
    for (r0, rows), shape in zip(offs, shapes):
        n = math.prod(shape)
        blk = packed[..., r0:r0 + rows, :]
        blk = blk.reshape(packed.shape[:-2] + (rows * LANES,))[..., :n]
        out.append(blk.reshape(packed.shape[:-2] + tuple(shape)))
    return out


def _full_from_slots(blk, col_sharded):
    _, r, c = blk.shape
    if col_sharded:
        return blk.transpose(1, 0, 2).reshape(r, N_DEV * c)
    return blk.reshape(N_DEV * r, c)


def _ffn_fwd(h, n, w_gu_t, w_d, tag, next_gain=None):
    gate, up, a = _ffn_up(n, w_gu_t, f"{tag}_up")
    out = _mm(a, w_d, F32, f"{tag}_down", res=h, alpha=FFN_RES_WEIGHT, norm_out=next_gain)
    h_out, n_next = out if next_gain is not None else (out, None)
    return h_out, (h, n, gate, up, a), n_next


def _ffn_bwd(dh_out, saved, g, w_gu_t, w_d, tag, dep, send_grads):
    h, n, gate, up, a = saved
    dw_d = _mm_tn(a, dh_out, f"{tag}_dw_down", alpha=FFN_RES_WEIGHT, dep=dep)
    dgate, dup = _ffn_da(dh_out, w_d, gate, up, f"{tag}_da", FFN_RES_WEIGHT)
    f = dgate.shape[1]
    dw_gu_t = _mm_tn(dgate, n, f"{tag}_dw_gate", out_rows=2 * f)
    dw_gu_t = _mm_tn(dup, n, f"{tag}_dw_up", out_rows=2 * f, row_off=f, into=dw_gu_t)
    dep = send_grads(dw_gu_t, dw_d)
    return _mm([dgate, dup], w_gu_t, F32, f"{tag}_dn", dep=dep, tk=1408, res=dh_out, norm_bwd=(h, g))


W_GROUPS = (("ffn1_w_gate_up", "ffn1_w_down"),
            ("w_in",),
            ("w_out_a", "w_out_ssm", "w_mix_out"),
            ("w_q", "w_kv", "w_o_x", "ffn2_w_gate_up", "ffn2_w_down"))
G_GROUPS = (("ffn2_w_gate_up", "ffn2_w_down"),
            ("w_o_x", "w_q", "w_kv", "w_mix_out", "w_out_a", "w_out_ssm", "w_in"),
            ("ffn1_w_gate_up", "ffn1_w_down"))


def _local_step(x3, mem3, target3, small, comm):
    nb, s, d = x3.shape
    m_len = mem3.shape[1]
    t = nb * s
    nc = s // SSM_CHUNK
    di = small["ssm_norm"].shape[1]
    hs = di // SSM_HEAD_DIM
    cc = di + 2 * SSM_GROUPS * SSM_STATE
    x, mem, target = x3.reshape(t, d), mem3.reshape(nb * m_len, d), target3.reshape(t, d)

    sizes = (d, d, d, di, cc, hs, d, d)
    offs = [0]
    for sz in sizes:
        offs.append(offs[-1] + sz)
    z_col0, xbc_col0 = 3 * d, 3 * d + di
    ga_blk, gb_blk = (3 * d + di + cc) // d, (4 * d + di + cc) // d

    pad_vec = lambda v: jnp.pad(v.reshape(1, -1), ((0, 0), (0, LANES - hs)))
    prow = jnp.concatenate([pad_vec(small["ssm_dt_bias"]), pad_vec(small["ssm_a_log"]), pad_vec(small["ssm_d"]),
                            jnp.zeros((5, LANES), F32)], axis=0)
    pcol = prow.T
    e_mat = (lax.broadcasted_iota(jnp.int32, (LANES, di), 0)
             == lax.broadcasted_iota(jnp.int32, (LANES, di), 1) // SSM_HEAD_DIM).astype(BF16)
    conv_a_w8 = jnp.pad(small["conv_a_w"][0], ((0, 8 - CONV_A_K), (0, 0)))
    ssm_conv_w8 = jnp.pad(small["ssm_conv_w"][0], ((0, 8 - SSM_CONV_K), (0, 0)))

    wts, dep = comm.weights(0, None)
    n1 = _rms_fwd(x, small["ffn1_norm"] + dep[0, 0], "ffn1_norm")
    h1, ffn1_saved, u = _ffn_fwd(x, n1, wts["ffn1_w_gate_up"], wts["ffn1_w_down"], "ffn1", small["mix_norm"])
    got, dep = comm.weights(1, h1)
    wts.update(got)
    w_in_t = wts["w_in"]
    w_main_t = jnp.concatenate([w_in_t[offs[i]:offs[i + 1]] for i in (0, 1, 2, 3, 4, 6, 7)], axis=0)
    w_dt_t = jnp.pad(w_in_t[offs[5]:offs[6]], ((0, LANES - hs), (0, 0)))
    proj = _mm(u, w_main_t, BF16, "in_proj", nt=True, dep=dep)
    dtr = _mm(u, w_dt_t, F32, "in_proj_dt", nt=True)
    yap = _conv_a_fwd(proj, conv_a_w8, nb, s, d, "conv_a")
    got, dep = comm.weights(2, yap)
    wts.update(got)
    y_a = _mm(yap, wts["w_out_a"], BF16, "out_a", dep=dep)
    xc, conv_pre = _conv_s_fwd(proj, xbc_col0, ssm_conv_w8, small["ssm_conv_b"] + dep[0, 0], nb, s, cc, "conv_s")
    dtrt = dtr.T
    y_ssd, sprev = _ssd_fwd(xc, dtr, dtrt, prow, pcol, nb, nc, di, "ssd")
    ygn = _gate_norm_fwd(y_ssd, proj, z_col0, small["ssm_norm"], di, "gate_norm")
    y_b = _mm(ygn, wts["w_out_ssm"], BF16, "out_ssm")
    merged = _merge_fwd(y_a, y_b, proj, ga_blk, gb_blk, d, "merge")
    h2, un = _mm(merged, wts["w_mix_out"], F32, "mix_out", res=h1, norm_out=small["xattn_norm"])
    got, _ = comm.weights(3, h2)
    wts.update(got)
    mn = _rms_fwd(mem, small["mem_norm"], "mem_norm")
    q = _mm(un, wts["w_q"], BF16, "q_proj")
    kv = _mm(mn, wts["w_kv"], BF16, "kv_proj", nt=True)
    o = _xattn_fwd(q, kv, nb, s, m_len, d, "xattn")
    h3, n2 = _mm(o, wts["w_o_x"], F32, "o_proj", res=h2, norm_out=small["ffn2_norm"])
    h4, ffn2_saved, _ = _ffn_fwd(h3, n2, wts["ffn2_w_gate_up"], wts["ffn2_w_down"], "ffn2")
    loss_vec, dh4, dg_final = _loss_head(h4, small["final_norm"].reshape(1, d), target, "loss_head")

    grads = {"final_norm": dg_final.reshape(d)}
    big = {}
    dh3, grads["ffn2_norm"] = _ffn_bwd(
        dh4, ffn2_saved, small["ffn2_norm"], wts["ffn2_w_gate_up"], wts["ffn2_w_down"], "ffn2", None,
        lambda dw_gu_t, dw_d: comm.grads(0, {"ffn2_w_gate_up": dw_gu_t, "ffn2_w_down": dw_d}))
    big["w_o_x"] = _mm_tn(o, dh3, "dw_o")
    do = _mm(dh3, wts["w_o_x"], BF16, "d_o", nt=True)
    dq, dk, dv = _xattn_bwd(do, q, kv, nb, s, m_len, d, "xattn_bwd")
    big["w_q"] = _mm_tn(un, dq, "dw_q")
    big["w_kv"] = _mm_tn(dv, mn, "dw_v", out_rows=2 * d, row_off=d, into=_mm_tn(dk, mn, "dw_k", out_rows=2 * d))
    _, grads["mem_norm"] = _mm([dk, dv], wts["w_kv"], F32, "d_mn", norm_bwd=(mem, small["mem_norm"]))
    dh2, grads["xattn_norm"] = _mm(dq, wts["w_q"], F32, "d_un", nt=True, res=dh3, norm_bwd=(h2, small["xattn_norm"]))
    big["w_mix_out"] = _mm_tn(merged, dh2, "dw_mix")
    dmerged = _mm(dh2, wts["w_mix_out"], BF16, "d_merged", nt=True)
    dya, dyb, dga, dgb = _merge_bwd(dmerged, y_a, y_b, proj, ga_blk, gb_blk, d, "merge_bwd")
    big["w_out_a"] = _mm_tn(yap, dya, "dw_out_a")
    big["w_out_ssm"] = _mm_tn(ygn, dyb, "dw_out_ssm")
    dyap = _mm(dya, wts["w_out_a"], BF16, "d_yap", nt=True)
    dygn = _mm(dyb, wts["w_out_ssm"], BF16, "d_ygn", nt=True)
    dab, dac, dav, dconv_a = _conv_a_bwd(dyap, proj, conv_a_w8, nb, s, d, "conv_a_bwd")
    dy_ssd, dz, grads["ssm_norm"] = _gate_norm_bwd(dygn, y_ssd, proj, z_col0, small["ssm_norm"], di, "gate_norm_bwd")
    dxc, ddtr, ssd_sums = _ssd_bwd(dy_ssd, xc, dtr, dtrt, prow, pcol, e_mat, sprev, nb, nc, di, "ssd_bwd")
    dxbc, dconv_s, grads["ssm_conv_b"] = _conv_s_bwd(dxc, conv_pre, proj, xbc_col0, ssm_conv_w8, nb, s, cc, "conv_s_bwd")
    dpieces = [("ab", dab), ("ac", dac), ("av", dav), ("z", dz), ("xbc", dxbc), ("ga", dga), ("gb", dgb)]
    dw = {tag: _mm_tn(piece, u, f"dw_in_{tag}") for tag, piece in dpieces}
    dw_dt = _mm_tn(ddtr, u, "dw_in_dt")[:hs]
    du_main = _mm([piece for _, piece in dpieces], w_main_t, F32, "d_u", tk=1024)
    dh1, grads["mix_norm"] = _mm(ddtr, w_dt_t, F32, "d_u_dt", res=dh2, norm_bwd=(h1, small["mix_norm"], du_main))
    big["w_in"] = jnp.concatenate([dw["ab"], dw["ac"], dw["av"], dw["z"], dw["xbc"], dw_dt, dw["ga"], dw["gb"]], axis=0)
    dep = comm.grads(1, big)
    dx, grads["ffn1_norm"] = _ffn_bwd(
        dh1, ffn1_saved, small["ffn1_norm"], wts["ffn1_w_gate_up"], wts["ffn1_w_down"], "ffn1", dep,
        lambda dw_gu_t, dw_d: comm.grads(2, {"ffn1_w_gate_up": dw_gu_t, "ffn1_w_down": dw_d}))

    grads["conv_a_w"] = dconv_a[:CONV_A_K]
    grads["ssm_conv_w"] = dconv_s[:SSM_CONV_K]
    grads["ssm_dt_bias"] = ssd_sums[0:1, :hs]
    grads["ssm_a_log"] = ssd_sums[1:2, :hs]
    grads["ssm_d"] = ssd_sums[2:3, :hs]
    return loss_vec[0, 0], dx.reshape(nb, s, d), grads


def _step(inputs):
    w = {k: inputs[k] for k in WEIGHT_ORDER}
    mom = {k: inputs["m_" + k] for k in WEIGHT_ORDER}
    vel = {k: inputs["v_" + k] for k in WEIGHT_ORDER}
    me = 4 * lax.axis_index("x") + 2 * lax.axis_index("y") + lax.axis_index("c")

    send = {k: (w[k][0].T if k in COL_SHARDED else w[k][0]).astype(BF16) for k in BIG_WEIGHTS}

    def own_slot(land, mine):
        return lax.dynamic_update_slice(land, mine[None], (me, 0, 0))

    gathers, exchanges = {}, {}

    def weights(i, after):
        if i == 0:
            lands = _all_gather([send[k] for k in W_GROUPS[0]], "gather0")
        else:
            sent, lands = _push_wait(gathers[i], False, after, f"gather{i}_wait")
            lands = [own_slot(land, mine) for land, mine in zip(lands, sent)]
        full = {k: land.reshape(N_DEV * land.shape[1], land.shape[2]) for k, land in zip(W_GROUPS[i], lands)}
        dep = jnp.zeros((8, LANES), F32)
        if i + 1 < len(W_GROUPS):
            gathers[i + 1] = _push_start([send[k] for k in W_GROUPS[i + 1]], False, lands[0], f"gather{i + 1}_start")
            dep = gathers[i + 1]["token"]
        return full, dep

    def slot_shape(k):
        rows, cols = send[k].shape
        return (rows * cols // LANES, LANES) if rows % BF16_SUBLANES else (rows, cols)

    def send_grads(i, by_name):
        slots = [by_name[k].reshape((N_DEV,) + slot_shape(k)) for k in G_GROUPS[i]]
        exchanges[i] = _push_start(slots, True, slots[0], f"exchange{i}_start")
        return exchanges[i]["token"]

    comm = types.SimpleNamespace(weights=weights, grads=send_grads)
    small = {k: w[k] for k in SMALL_REPLICATED}
    conv_shapes = [w[k].shape[1:] for k in SMALL_SHARDED]
    packed_c, conv_offs = _pack_rows([w[k][0] for k in SMALL_SHARDED], F32, 8)
    conv_blocks = _unpack_rows(_all_gather([packed_c], "gather_conv_weights")[0], conv_offs, conv_shapes)
    for k, b in zip(SMALL_SHARDED, conv_blocks):
        small[k] = _full_from_slots(b, True)[None]

    loss_local, grad_x, grads = _local_step(inputs["x"], inputs["mem"], inputs["loss_target"], small, comm)
    loss = lax.psum(loss_local, AXES)

    out = {}
    for i, names in enumerate(G_GROUPS):
        sent, lands = _push_wait(exchanges[i], True, grad_x, f"exchange{i}_wait")
        for k, land, slots in zip(names, lands, sent):
            parts = own_slot(land, lax.dynamic_index_in_dim(slots, me, 0, keepdims=False))
            if k in COL_SHARDED and w[k].shape[2] % LANES:
                flip = lambda a: a.transpose(0, 2, 1).reshape((1,) + slot_shape(k))
                unflip = lambda a: a.reshape((1,) + send[k].shape).transpose(0, 2, 1)
                out[k] = tuple(unflip(o) for o in _sum8_adamw(parts, flip(w[k]), flip(mom[k]), flip(vel[k]),
                                                             f"sum_adamw_{k}", tr=2048 if slot_shape(k)[1] == LANES else 128))
                continue
            if k in COL_SHARDED:
                parts = parts.transpose(0, 2, 1)
            out[k] = tuple(_sum8_adamw(parts, w[k], mom[k], vel[k], f"sum_adamw_{k}"))

    small_names = SMALL_REPLICATED + SMALL_SHARDED
    packed_g, small_offs = _pack_rows([grads[k] for k in small_names], F32, 8)
    total = _sum8(_all_gather([packed_g], "gather_small_grads")[0], "sum_small_grads")
    full_grads = _unpack_rows(total, small_offs, [grads[k].shape for k in small_names])
    mine = {}
    for k, g in zip(small_names, full_grads):
        if k in SMALL_SHARDED:
            c_loc = w[k].shape[2]
            g = lax.dynamic_slice_in_dim(g, me * c_loc, c_loc, axis=1)
        mine[k] = g.reshape(w[k].shape)
    sg, s_offs = _pack_rows([mine[k] for k in small_names], F32, 8)
    sw, _ = _pack_rows([w[k] for k in small_names], F32, 8)
    sm, _ = _pack_rows([mom[k] for k in small_names], F32, 8)
    sv, _ = _pack_rows([vel[k] for k in small_names], F32, 8)
    s_shapes = [w[k].shape for k in small_names]
    small_out = [_unpack_rows(a, s_offs, s_shapes) for a in _adamw(sg, sw, sm, sv, "adamw_small")]
    for i, k in enumerate(small_names):
        out[k] = (mine[k],) + tuple(o[i] for o in small_out)

    res = [loss, grad_x]
    for j in range(4):
        res += [out[k][j] for k in WEIGHT_ORDER]
    return tuple(res)


def kernel(x, mem, ffn1_norm, ffn1_w_gate_up, ffn1_w_down, mix_norm, w_in, conv_a_w, w_out_a, ssm_conv_w, ssm_conv_b, ssm_dt_bias, ssm_a_log, ssm_d, ssm_norm, w_out_ssm, w_mix_out, xattn_norm, mem_norm, w_q, w_kv, w_o_x, ffn2_norm, ffn2_w_gate_up, ffn2_w_down, final_norm, loss_target, m_ffn1_norm, m_ffn1_w_gate_up, m_ffn1_w_down, m_mix_norm, m_w_in, m_conv_a_w, m_w_out_a, m_ssm_conv_w, m_ssm_conv_b, m_ssm_dt_bias, m_ssm_a_log, m_ssm_d, m_ssm_norm, m_w_out_ssm, m_w_mix_out, m_xattn_norm, m_mem_norm, m_w_q, m_w_kv, m_w_o_x, m_ffn2_norm, m_ffn2_w_gate_up, m_ffn2_w_down, m_final_norm, v_ffn1_norm, v_ffn1_w_gate_up, v_ffn1_w_down, v_mix_norm, v_w_in, v_conv_a_w, v_w_out_a, v_ssm_conv_w, v_ssm_conv_b, v_ssm_dt_bias, v_ssm_a_log, v_ssm_d, v_ssm_norm, v_w_out_ssm, v_w_mix_out, v_xattn_norm, v_mem_norm, v_w_q, v_w_kv, v_w_o_x, v_ffn2_norm, v_ffn2_w_gate_up, v_ffn2_w_down, v_final_norm):
    return _step(dict(locals()))
```

```python
import functools
import math
import types

import jax
import jax.numpy as jnp
from jax import lax
from jax.experimental import pallas as pl
from jax.experimental.pallas import tpu as pltpu

F32, BF16 = jnp.float32, jnp.bfloat16
HI = lax.Precision.HIGHEST
MESH = pl.DeviceIdType.MESH
AXES = ("x", "y", "c")
N_DEV = 8

EPS = 1e-6
FFN_RES_WEIGHT = 0.5
SSM_HEAD_DIM = 64
SSM_GROUPS = 4
SSM_STATE = 128
SSM_CHUNK = 128
CONV_A_K = 3
SSM_CONV_K = 4
XATTN_HEADS = 4
ADAM_LR, ADAM_B1, ADAM_B2, ADAM_EPS, ADAM_WD, ADAM_STEP = 1e-3, 0.9, 0.999, 1e-8, 0.01, 10

LANES = 128
BF16_SUBLANES = 16
VMEM_LIMIT_BYTES = 56 * 2 ** 20
NEG_BIG = -1e30

BIG_WEIGHTS = ("ffn1_w_gate_up", "ffn1_w_down", "w_in", "w_out_a", "w_out_ssm", "w_mix_out",
               "w_q", "w_kv", "w_o_x", "ffn2_w_gate_up", "ffn2_w_down")
COL_SHARDED = ("ffn1_w_gate_up", "w_in", "w_kv", "ffn2_w_gate_up")
SMALL_REPLICATED = ("ffn1_norm", "mix_norm", "ssm_conv_b", "ssm_dt_bias", "ssm_a_log", "ssm_d", "ssm_norm",
                    "xattn_norm", "mem_norm", "ffn2_norm", "final_norm")
SMALL_SHARDED = ("conv_a_w", "ssm_conv_w")
WEIGHT_ORDER = ("ffn1_norm", "ffn1_w_gate_up", "ffn1_w_down", "mix_norm", "w_in", "conv_a_w", "w_out_a",
                "ssm_conv_w", "ssm_conv_b", "ssm_dt_bias", "ssm_a_log", "ssm_d", "ssm_norm", "w_out_ssm",
                "w_mix_out", "xattn_norm", "mem_norm", "w_q", "w_kv", "w_o_x", "ffn2_norm", "ffn2_w_gate_up",
                "ffn2_w_down", "final_norm")


def _tile(dim, pref, unit):
    best = None
    t = unit
    while t <= min(dim, pref):
        if dim % t == 0:
            best = t
        t += unit
    return best if best is not None else dim


def _params(*sem):
    return pltpu.CompilerParams(dimension_semantics=sem, vmem_limit_bytes=VMEM_LIMIT_BYTES)


def _sigmoid(x):
    return pl.reciprocal(1.0 + jnp.exp(-x), approx=True)


def _silu(x):
    return x * _sigmoid(x)


def _dsilu(x):
    s = _sigmoid(x)
    return s * (1.0 + x * (1.0 - s))


def _softplus(x):
    return jnp.maximum(x, 0.0) + jnp.log(1.0 + jnp.exp(-jnp.abs(x)))


def _dot(a, b, dims=(((1,), (0,)), ((), ())), precision=None):
    return lax.dot_general(a, b, dims, preferred_element_type=F32, precision=precision)


def _stack_rows(rows, width):
    r_idx = lax.broadcasted_iota(jnp.int32, (8, width), 0)
    acc = jnp.zeros((8, width), F32)
    for k, row in enumerate(rows):
        acc = jnp.where(r_idx == k, row, acc)
    return acc


NT = (((1,), (1,)), ((), ()))
TN = (((0,), (0,)), ((), ()))


def _mm(a, b, out_dtype, name, res=None, alpha=1.0, nt=False, dep=None, norm_out=None, norm_bwd=None,
        tm=1024, tn=2048, tk=2816):
    pieces = list(a) if isinstance(a, (list, tuple)) else [a]
    m = pieces[0].shape[0]
    k = sum(p.shape[1] for p in pieces)
    n = b.shape[0] if nt else b.shape[1]
    assert (b.shape[1] if nt else b.shape[0]) == k
    tm, tn = _tile(m, tm, 8), _tile(n, tn, LANES)
    tk = _tile(math.gcd(*[p.shape[1] for p in pieces]), tk, LANES)
    nk = k // tk
    starts, s0 = [], 0
    for p in pieces:
        starts.append((s0, p.shape[1] // tk))
        s0 += p.shape[1] // tk
    n_p = len(pieces)
    assert norm_out is None or norm_bwd is None
    whole_rows = norm_out is not None or norm_bwd is not None
    assert not whole_rows or tn == n
    has_pre = norm_bwd is not None and len(norm_bwd) == 3

    def body(*refs):
        a_refs, b_ref = refs[:n_p], refs[n_p]
        nxt = n_p + 1
        r_ref = refs[nxt] if res is not None else None
        nxt += (res is not None) + (dep is not None)
        g_ref = refs[nxt] if whole_rows else None
        x_ref = refs[nxt + 1] if norm_bwd is not None else None
        pre_ref = refs[nxt + 2] if has_pre else None
        nxt += whole_rows + (norm_bwd is not None) + has_pre
        o_ref = refs[nxt]
        o2_ref = refs[nxt + 1] if whole_rows else None
        scr = refs[nxt + 1 + whole_rows:]
        first_rows = pl.program_id(0) == 0

        def finish(acc):
            acc = alpha * acc if alpha != 1.0 else acc
            if norm_bwd is not None:
                if pre_ref is not None:
                    acc = acc + pre_ref[...]
                xv = x_ref[...]
                r = lax.rsqrt(jnp.mean(xv * xv, axis=-1, keepdims=True) + EPS)
                xh = xv * r
                gy = acc * g_ref[...]
                part = jnp.sum(acc * xh, axis=0, keepdims=True)
                acc = r * (gy - xh * jnp.mean(gy * xh, axis=-1, keepdims=True))

                @pl.when(first_rows)
                def _():
                    o2_ref[...] = part

                @pl.when(jnp.logical_not(first_rows))
                def _():
                    o2_ref[...] += part
            if r_ref is not None:
                acc = r_ref[...] + acc
            o_ref[...] = acc.astype(out_dtype)
            if norm_out is not None:
                r = lax.rsqrt(jnp.mean(acc * acc, axis=-1, keepdims=True) + EPS)
                o2_ref[...] = (acc * r * g_ref[...]).astype(BF16)

        def product(a_ref):
            return _dot(a_ref[...].astype(BF16), b_ref[...].astype(BF16), NT if nt else (((1,), (0,)), ((), ())))

        if nk == 1:
            finish(product(a_refs[0]))
            return
        acc_ref = scr[0]
        kk = pl.program_id(2)
        for (s, cnt), a_ref in zip(starts, a_refs):
            if s == 0:
                @pl.when(kk == 0)
                def _():
                    acc_ref[...] = product(a_ref)

                @pl.when(jnp.logical_and(kk > 0, kk < cnt))
                def _():
                    acc_ref[...] += product(a_ref)
            else:
                @pl.when(jnp.logical_and(kk >= s, kk < s + cnt))
                def _():
                    acc_ref[...] += product(a_ref)

        @pl.when(kk == nk - 1)
        def _():
            finish(acc_ref[...])

    def a_spec(s, cnt):
        return pl.BlockSpec((tm, tk), lambda i, j, kk: (i, jnp.clip(kk - s, 0, cnt - 1)))

    in_specs = [a_spec(s, cnt) for s, cnt in starts]
    in_specs.append(pl.BlockSpec((tn, tk), lambda i, j, kk: (j, kk)) if nt else pl.BlockSpec((tk, tn), lambda i, j, kk: (kk, j)))
    args = pieces + [b]
    if res is not None:
        in_specs.append(pl.BlockSpec((tm, tn), lambda i, j, kk: (i, j)))
        args.append(res)
    if dep is not None:
        in_specs.append(pl.BlockSpec((8, LANES), lambda i, j, kk: (0, 0)))
        args.append(dep)
    tile = pl.BlockSpec((tm, tn), lambda i, j, kk: (i, j))
    vec = pl.BlockSpec((1, n), lambda i, j, kk: (0, 0))
    out_specs, out_shape = tile, jax.ShapeDtypeStruct((m, n), out_dtype)
    if norm_out is not None:
        in_specs.append(vec)
        args.append(norm_out)
        out_specs, out_shape = [tile, tile], [out_shape, jax.ShapeDtypeStruct((m, n), BF16)]
    if norm_bwd is not None:
        in_specs += [vec, tile] + ([tile] if has_pre else [])
        args += [norm_bwd[1], norm_bwd[0]] + ([norm_bwd[2]] if has_pre else [])
        out_specs, out_shape = [tile, vec], [out_shape, jax.ShapeDtypeStruct((1, n), F32)]
    return pl.pallas_call(
        body, name=name, grid=(m // tm, n // tn, nk), in_specs=in_specs, out_specs=out_specs, out_shape=out_shape,
        scratch_shapes=[pltpu.VMEM((tm, tn), F32)] if nk > 1 else [],
        compiler_params=(_params("arbitrary", "arbitrary", "arbitrary") if norm_bwd is not None
                         else _params("parallel", "parallel", "arbitrary")),
    )(*args)


def _mm_tn(x, dy, name, out_dtype=BF16, alpha=1.0, dep=None, out_rows=None, row_off=0, into=None,
           tko=1408, tn=1024, tt=2048):
    t, k = x.shape
    n = dy.shape[1]
    tko, tn, tt = _tile(k, tko, LANES), _tile(n, tn, LANES), _tile(t, tt, 8)
    nt_steps = t // tt

    def body(*refs):
        x_ref, dy_ref = refs[:2]
        o_ref, acc_ref = refs[-2:]
        part = _dot(x_ref[...].astype(BF16), dy_ref[...].astype(BF16), TN)
        step = pl.program_id(2)

        @pl.when(step == 0)
        def _():
            acc_ref[...] = part

        @pl.when(step > 0)
        def _():
            acc_ref[...] += part

        @pl.when(step == nt_steps - 1)
        def _():
            acc = acc_ref[...]
            o_ref[...] = (alpha * acc if alpha != 1.0 else acc).astype(out_dtype)

    in_specs = [pl.BlockSpec((tt, tko), lambda i, j, s: (s, i)), pl.BlockSpec((tt, tn), lambda i, j, s: (s, j))]
    args = [x, dy]
    if dep is not None:
        in_specs.append(pl.BlockSpec((8, LANES), lambda i, j, s: (0, 0)))
        args.append(dep)
    aliases = {}
    if into is not None:
        in_specs.append(pl.BlockSpec(memory_space=pl.ANY))
        args.append(into)
        aliases = {len(args) - 1: 0}
    band = row_off // tko
    assert row_off % tko == 0
    return pl.pallas_call(
        body, name=name, grid=(k // tko, n // tn, nt_steps), in_specs=in_specs,
        out_specs=pl.BlockSpec((tko, tn), lambda i, j, s: (i + band, j)),
        out_shape=jax.ShapeDtypeStruct((out_rows or k, n), out_dtype),
        scratch_shapes=[pltpu.VMEM((tko, tn), F32)], input_output_aliases=aliases,
        compiler_params=_params("parallel", "parallel", "arbitrary"),
    )(*args)


def _rms_fwd(x, g, name, tt=512):
    t, d = x.shape
    tt = _tile(t, tt, 8)

    def body(x_ref, g_ref, o_ref):
        xv = x_ref[...]
        r = lax.rsqrt(jnp.mean(xv * xv, axis=-1, keepdims=True) + EPS)
        o_ref[...] = (xv * r * g_ref[...]).astype(BF16)

    return pl.pallas_call(
        body, name=name, grid=(t // tt,),
        in_specs=[pl.BlockSpec((tt, d), lambda i: (i, 0)), pl.BlockSpec((1, d), lambda i: (0, 0))],
        out_specs=pl.BlockSpec((tt, d), lambda i: (i, 0)),
        out_shape=jax.ShapeDtypeStruct((t, d), BF16), compiler_params=_params("parallel"),
    )(x, g)


def _ffn_up(n, w_gu_t, name, tm=512, tf=2816):
    t, d = n.shape
    f = w_gu_t.shape[0] // 2
    tm, tf = _tile(t, tm, 8), _tile(f, tf, LANES)
    nf = f // tf

    def body(n_ref, wg_ref, wu_ref, g_ref, u_ref, a_ref):
        nv = n_ref[...]
        gate, up = _dot(nv, wg_ref[...], NT), _dot(nv, wu_ref[...], NT)
        s = _sigmoid(gate)
        sg = gate * s
        g_ref[...] = (up * (s * (1.0 + gate * (1.0 - s)))).astype(BF16)
        u_ref[...] = sg.astype(BF16)
        a_ref[...] = (sg * up).astype(BF16)

    blk = pl.BlockSpec((tm, tf), lambda i, j: (i, j))
    out = jax.ShapeDtypeStruct((t, f), BF16)
    return pl.pallas_call(
        body, name=name, grid=(t // tm, nf),
        in_specs=[pl.BlockSpec((tm, d), lambda i, j: (i, 0)), pl.BlockSpec((tf, d), lambda i, j: (j, 0)),
                  pl.BlockSpec((tf, d), lambda i, j: (j + nf, 0))],
        out_specs=[blk, blk, blk], out_shape=[out, out, out], compiler_params=_params("parallel", "parallel"),
    )(n, w_gu_t, w_gu_t)


def _ffn_da(dh, w_d, gate, up, name, alpha, dep=None, tm=512, tf=2816):
    t, d = dh.shape
    f = w_d.shape[0]
    tm, tf = _tile(t, tm, 8), _tile(f, tf, LANES)

    def body(*refs):
        dh_ref, w_ref, g_ref, u_ref = refs[:4]
        dg_ref, du_ref = refs[-2:]
        da = alpha * _dot(dh_ref[...].astype(BF16), w_ref[...], NT)
        dg_ref[...] = (da * g_ref[...].astype(F32)).astype(BF16)
        du_ref[...] = (da * u_ref[...].astype(F32)).astype(BF16)

    blk = pl.BlockSpec((tm, tf), lambda i, j: (i, j))
    in_specs = [pl.BlockSpec((tm, d), lambda i, j: (i, 0)), pl.BlockSpec((tf, d), lambda i, j: (j, 0)), blk, blk]
    args = [dh, w_d, gate, up]
    if dep is not None:
        in_specs.append(pl.BlockSpec((8, LANES), lambda i, j: (0, 0)))
        args.append(dep)
    out = jax.ShapeDtypeStruct((t, f), BF16)
    return pl.pallas_call(
        body, name=name, grid=(t // tm, f // tf), in_specs=in_specs, out_specs=[blk, blk], out_shape=[out, out],
        compiler_params=_params("parallel", "parallel"),
    )(*args)


def _merge_fwd(ya, yb, proj, ga_blk, gb_blk, d, name, tt=512):
    t = ya.shape[0]
    tt = _tile(t, tt, 8)

    def body(ya_ref, yb_ref, ga_ref, gb_ref, o_ref):
        o_ref[...] = (_sigmoid(ga_ref[...].astype(F32)) * ya_ref[...].astype(F32)
                      + _sigmoid(gb_ref[...].astype(F32)) * yb_ref[...].astype(F32)).astype(BF16)

    row = pl.BlockSpec((tt, d), lambda i: (i, 0))
    return pl.pallas_call(
        body, name=name, grid=(t // tt,),
        in_specs=[row, row, pl.BlockSpec((tt, d), lambda i: (i, ga_blk)), pl.BlockSpec((tt, d), lambda i: (i, gb_blk))],
        out_specs=row, out_shape=jax.ShapeDtypeStruct((t, d), BF16), compiler_params=_params("parallel"),
    )(ya, yb, proj, proj)


def _merge_bwd(dm, ya, yb, proj, ga_blk, gb_blk, d, name, tt=512):
    t = ya.shape[0]
    tt = _tile(t, tt, 8)

    def body(dm_ref, ya_ref, yb_ref, ga_ref, gb_ref, dya_ref, dyb_ref, dg_ref):
        dmv = dm_ref[...].astype(F32)
        sa, sb = _sigmoid(ga_ref[...].astype(F32)), _sigmoid(gb_ref[...].astype(F32))
        dya_ref[...] = (dmv * sa).astype(BF16)
        dyb_ref[...] = (dmv * sb).astype(BF16)
        dg_ref[:, 0:d] = (dmv * ya_ref[...].astype(F32) * sa * (1.0 - sa)).astype(BF16)
        dg_ref[:, d:2 * d] = (dmv * yb_ref[...].astype(F32) * sb * (1.0 - sb)).astype(BF16)

    row = pl.BlockSpec((tt, d), lambda i: (i, 0))
    out = jax.ShapeDtypeStruct((t, d), BF16)
    return pl.pallas_call(
        body, name=name, grid=(t // tt,),
        in_specs=[row, row, row, pl.BlockSpec((tt, d), lambda i: (i, ga_blk)), pl.BlockSpec((tt, d), lambda i: (i, gb_blk))],
        out_specs=[row, row, pl.BlockSpec((tt, 2 * d), lambda i: (i, 0))],
        out_shape=[out, out, jax.ShapeDtypeStruct((t, 2 * d), BF16)], compiler_params=_params("parallel"),
    )(dm, ya, yb, proj, proj)


def _loss_head(h, g, target, name, tt=512):
    t, d = h.shape
    tt = _tile(t, tt, 8)

    def body(h_ref, g_ref, tg_ref, loss_ref, dh_ref, dg_ref):
        xv = h_ref[...]
        r = lax.rsqrt(jnp.mean(xv * xv, axis=-1, keepdims=True) + EPS)
        xh = xv * r
        err = xh * g_ref[...] - tg_ref[...]
        dout = err * (1.0 / d)
        gy = dout * g_ref[...]
        dh_ref[...] = r * (gy - xh * jnp.mean(gy * xh, axis=-1, keepdims=True))
        dg_part = jnp.sum(dout * xh, axis=0, keepdims=True)
        loss_part = jnp.full((1, LANES), 0.5 / d, F32) * jnp.sum(err * err)

        @pl.when(pl.program_id(0) == 0)
        def _():
            dg_ref[...] = dg_part
            loss_ref[...] = loss_part

        @pl.when(pl.program_id(0) > 0)
        def _():
            dg_ref[...] += dg_part
            loss_ref[...] += loss_part

    row = pl.BlockSpec((tt, d), lambda i: (i, 0))
    vec = pl.BlockSpec((1, d), lambda i: (0, 0))
    return pl.pallas_call(
        body, name=name, grid=(t // tt,), in_specs=[row, vec, row],
        out_specs=[pl.BlockSpec((1, LANES), lambda i: (0, 0)), row, vec],
        out_shape=[jax.ShapeDtypeStruct((1, LANES), F32), jax.ShapeDtypeStruct((t, d), F32), jax.ShapeDtypeStruct((1, d), F32)],
        compiler_params=_params("arbitrary"),
    )(h, g, target)


def _shift_down(x, k, t_idx):
    if k == 0:
        return x
    return jnp.where(t_idx >= k, pltpu.roll(x, k, 0), 0.0)


def _shift_up(x, k, t_idx, s):
    if k == 0:
        return x
    return jnp.where(t_idx < s - k, pltpu.roll(x, s - k, 0), 0.0)


CONV_A_BLOCK = 256


def _conv_a_fwd(proj, w, nb, s, d, name):
    cb = _tile(d, CONV_A_BLOCK, LANES)

    def body(x_ref, w_ref, o_ref):
        t_idx = lax.broadcasted_iota(jnp.int32, (s, cb), 0)
        cv = x_ref[:, cb:2 * cb].astype(F32) * x_ref[:, 2 * cb:3 * cb].astype(F32)
        cc = sum(w_ref[k:k + 1, :] * _shift_down(cv, CONV_A_K - 1 - k, t_idx) for k in range(CONV_A_K))
        o_ref[...] = (x_ref[:, 0:cb].astype(F32) * cc).astype(BF16)

    return pl.pallas_call(
        body, name=name, grid=(nb, d // cb),
        in_specs=[pl.BlockSpec((s, 3 * cb), lambda b, j: (b, j)), pl.BlockSpec((8, cb), lambda b, j: (0, j))],
        out_specs=pl.BlockSpec((s, cb), lambda b, j: (b, j)),
        out_shape=jax.ShapeDtypeStruct((nb * s, d), BF16), compiler_params=_params("parallel", "parallel"),
    )(proj, w)


def _conv_a_bwd(dy, proj, w, nb, s, d, out_cols, name):
    cb = _tile(d, CONV_A_BLOCK, LANES)

    def body(dy_ref, x_ref, w_ref, o_ref, dw_ref):
        t_idx = lax.broadcasted_iota(jnp.int32, (s, cb), 0)
        cv_c, cv_v = x_ref[:, cb:2 * cb].astype(F32), x_ref[:, 2 * cb:3 * cb].astype(F32)
        cv = cv_c * cv_v
        shifted = [_shift_down(cv, CONV_A_K - 1 - k, t_idx) for k in range(CONV_A_K)]
        cc = sum(w_ref[k:k + 1, :] * shifted[k] for k in range(CONV_A_K))
        dyv = dy_ref[...].astype(F32)
        o_ref[:, 0:cb] = (dyv * cc).astype(BF16)
        dcc = dyv * x_ref[:, 0:cb].astype(F32)
        dcv = sum(w_ref[k:k + 1, :] * _shift_up(dcc, CONV_A_K - 1 - k, t_idx, s) for k in range(CONV_A_K))
        o_ref[:, cb:2 * cb] = (dcv * cv_v).astype(BF16)
        o_ref[:, 2 * cb:3 * cb] = (dcv * cv_c).astype(BF16)
        rows = [jnp.sum(dcc * shifted[k], axis=0, keepdims=True) for k in range(CONV_A_K)]
        part = _stack_rows(rows, cb)

        @pl.when(pl.program_id(1) == 0)
        def _():
            dw_ref[...] = part

        @pl.when(pl.program_id(1) > 0)
        def _():
            dw_ref[...] += part

    wspec = pl.BlockSpec((8, cb), lambda j, b: (0, j))
    wide = pl.BlockSpec((s, 3 * cb), lambda j, b: (b, j))
    return pl.pallas_call(
        body, name=name, grid=(d // cb, nb), in_specs=[pl.BlockSpec((s, cb), lambda j, b: (b, j)), wide, wspec],
        out_specs=[wide, wspec],
        out_shape=[jax.ShapeDtypeStruct((nb * s, out_cols), BF16), jax.ShapeDtypeStruct((8, d), F32)],
        compiler_params=_params("parallel", "arbitrary"),
    )(dy, proj, w)


def _conv_s_fwd(proj, col0, w, bias, nb, s, cc_width, name, cb=256):
    cb = _tile(math.gcd(cc_width, col0) if col0 else cc_width, cb, LANES)
    nd, off = cc_width // cb, col0 // cb

    def body(x_ref, w_ref, b_ref, o_ref, pre_ref):
        t_idx = lax.broadcasted_iota(jnp.int32, (s, cb), 0)
        xv = x_ref[...].astype(F32)
        pre = b_ref[...] + sum(w_ref[k:k + 1, :] * _shift_down(xv, SSM_CONV_K - 1 - k, t_idx) for k in range(SSM_CONV_K))
        o_ref[...] = _silu(pre).astype(BF16)
        pre_ref[...] = pre.astype(BF16)

    vec = pl.BlockSpec((8, cb), lambda b, j: (0, j))
    own = pl.BlockSpec((s, cb), lambda b, j: (b, j))
    out = jax.ShapeDtypeStruct((nb * s, cc_width), BF16)
    return pl.pallas_call(
        body, name=name, grid=(nb, nd),
        in_specs=[pl.BlockSpec((s, cb), lambda b, j: (b, j + off)), vec, pl.BlockSpec((1, cb), lambda b, j: (0, j))],
        out_specs=[own, own], out_shape=[out, out], compiler_params=_params("parallel", "parallel"),
    )(proj, w, bias)


def _conv_s_bwd(dxc, pre, proj, col0, w, nb, s, cc_width, into, name, cb=256):
    cb = _tile(math.gcd(cc_width, col0) if col0 else cc_width, cb, LANES)
    nd, off = cc_width // cb, col0 // cb

    def body(d_ref, pre_ref, x_ref, w_ref, into_ref, dx_ref, dw_ref, db_ref):
        t_idx = lax.broadcasted_iota(jnp.int32, (s, cb), 0)
        xv = x_ref[...].astype(F32)
        dpre = d_ref[...].astype(F32) * _dsilu(pre_ref[...].astype(F32))
        ahead = [_shift_up(dpre, j, t_idx, s) for j in range(SSM_CONV_K)]
        dx_ref[...] = sum(w_ref[k:k + 1, :] * ahead[SSM_CONV_K - 1 - k] for k in range(SSM_CONV_K)).astype(BF16)
        rows = [jnp.sum(ahead[SSM_CONV_K - 1 - k] * xv, axis=0, keepdims=True) for k in range(SSM_CONV_K)]
        dw_part = _stack_rows(rows, cb)
        db_part = jnp.sum(dpre, axis=0, keepdims=True)

        @pl.when(pl.program_id(1) == 0)
        def _():
            dw_ref[...] = dw_part
            db_ref[...] = db_part

        @pl.when(pl.program_id(1) > 0)
        def _():
            dw_ref[...] += dw_part
            db_ref[...] += db_part

    own = pl.BlockSpec((s, cb), lambda j, b: (b, j))
    shifted = pl.BlockSpec((s, cb), lambda j, b: (b, j + off))
    wspec = pl.BlockSpec((8, cb), lambda j, b: (0, j))
    bspec = pl.BlockSpec((1, cb), lambda j, b: (0, j))
    return pl.pallas_call(
        body, name=name, grid=(nd, nb),
        in_specs=[own, own, shifted, wspec, pl.BlockSpec(memory_space=pl.ANY)],
        out_specs=[shifted, wspec, bspec],
        out_shape=[jax.ShapeDtypeStruct(into.shape, into.dtype), jax.ShapeDtypeStruct((8, cc_width), F32),
                   jax.ShapeDtypeStruct((1, cc_width), F32)],
        input_output_aliases={4: 0},
        compiler_params=_params("parallel", "arbitrary"),
    )(dxc, pre, proj, w, into)


def _split3(v):
    hi = v.astype(BF16)
    r1 = v - hi.astype(F32)
    mid = r1.astype(BF16)
    return hi, mid, (r1 - mid.astype(F32)).astype(BF16)


def _exact_left(mask_b, v):
    return sum(_dot(mask_b, t) for t in _split3(v))


def _exact_right(v, mask_b):
    return sum(_dot(t, mask_b) for t in _split3(v))


def _head_sums(v, e_b):
    return _dot(v.astype(BF16), e_b, NT)


def _spread(v, out_ref, di):
    lane = lax.broadcasted_iota(jnp.int32, (v.shape[0], LANES), 1)
    for pr in range(di // LANES):
        h0 = pr * (LANES // SSM_HEAD_DIM)
        out_ref[:, pr * LANES:(pr + 1) * LANES] = jnp.where(lane < SSM_HEAD_DIM, v[:, h0:h0 + 1], v[:, h0 + 1:h0 + 2])


def _ssd_common(xc_ref, dtr_ref, dtrt_ref, prow_ref, pcol_ref, dtx_ref, acsx_ref, dx_ref, di):
    l = SSM_CHUNK
    bias_r, a_r = prow_ref[0:1, :], -jnp.exp(prow_ref[1:2, :])
    sp_in = dtr_ref[...] + bias_r
    dt = _softplus(sp_in)
    li = lax.broadcasted_iota(jnp.int32, (l, l), 0)
    si = lax.broadcasted_iota(jnp.int32, (l, l), 1)
    lower_b = (li >= si).astype(BF16)
    upper_b = (li <= si).astype(BF16)
    acs = _exact_left(lower_b, dt * a_r)
    bias_c, a_c = pcol_ref[:, 0:1], -jnp.exp(pcol_ref[:, 1:2])
    dt_t = _softplus(dtrt_ref[...] + bias_c)
    acs_t = _exact_right(dt_t * a_c, upper_b)
    _spread(dt, dtx_ref, di)
    _spread(acs, acsx_ref, di)
    _spread(prow_ref[0:8, :], dx_ref, di)
    acs_exp = acsx_ref[...]
    acs_last = acs_exp[l - 1:l, :]
    x = xc_ref[:, 0:di].astype(F32)
    return dict(dt=dt, a_r=a_r, sp_in=sp_in, acs=acs, acs_t=acs_t, dt_exp=dtx_ref[...], e_exp=jnp.exp(acs_exp),
                el_exp=jnp.exp(acs_last), f_exp=jnp.exp(acs_last - acs_exp), x=x, mask=li >= si, upper_b=upper_b,
                d_exp=dx_ref[2:3, :])


def _decay(q, h):
    seg = q["acs"][:, h:h + 1] - q["acs_t"][h:h + 1, :]
    return jnp.exp(jnp.where(q["mask"], seg, NEG_BIG))


def _ssd_fwd(xc, dtr, dtrt, prow, pcol, nb, nc, di, name):
    l, n, g_n, p = SSM_CHUNK, SSM_STATE, SSM_GROUPS, SSM_HEAD_DIM
    cc = xc.shape[1]
    gw = di // g_n
    assert p * 2 == LANES and gw % LANES == 0

    def body(xc_ref, dtr_ref, dtrt_ref, prow_ref, pcol_ref, y_ref, sprev_ref, st_ref, dtx_ref, acsx_ref, dx_ref):
        @pl.when(pl.program_id(1) == 0)
        def _():
            st_ref[...] = jnp.zeros_like(st_ref)

        q = _ssd_common(xc_ref, dtr_ref, dtrt_ref, prow_ref, pcol_ref, dtx_ref, acsx_ref, dx_ref, di)
        x = q["x"]
        xd = x * q["dt_exp"]
        xdb = xd.astype(BF16)
        xdf = (xd * q["f_exp"]).astype(BF16)
        lane = lax.broadcasted_iota(jnp.int32, (l, LANES), 1)
        for g in range(g_n):
            lo = g * gw
            bg = xc_ref[:, di + g * n: di + (g + 1) * n]
            cg = xc_ref[:, di + g_n * n + g * n: di + g_n * n + (g + 1) * n]
            cb = _dot(cg, bg, NT)
            st_g = st_ref[:, lo:lo + gw]
            y_off = q["e_exp"][:, lo:lo + gw] * _dot(cg, st_g.astype(BF16))
            for pr in range(gw // LANES):
                c0 = lo + pr * LANES
                h0 = c0 // p
                xp = xdb[:, c0:c0 + LANES]
                m0 = (cb * _decay(q, h0)).astype(BF16)
                m1 = (cb * _decay(q, h0 + 1)).astype(BF16)
                yd = _dot(m0, jnp.where(lane < p, xp, 0)) + _dot(m1, jnp.where(lane >= p, xp, 0))
                y_ref[:, c0:c0 + LANES] = (yd + y_off[:, pr * LANES:(pr + 1) * LANES]
                                           + q["d_exp"][:, c0:c0 + LANES] * x[:, c0:c0 + LANES])
            sprev_ref[:, lo:lo + gw] = st_g
            st_ref[:, lo:lo + gw] = q["el_exp"][:, lo:lo + gw] * st_g + _dot(bg, xdf[:, lo:lo + gw], TN)

    tok = lambda w: pl.BlockSpec((l, w), lambda b, c: (b * nc + c, 0))
    const = lambda r, w: pl.BlockSpec((r, w), lambda b, c: (0, 0))
    return pl.pallas_call(
        body, name=name, grid=(nb, nc),
        in_specs=[tok(cc), tok(LANES), pl.BlockSpec((LANES, l), lambda b, c: (0, b * nc + c)),
                  const(8, LANES), const(LANES, 8)],
        out_specs=[tok(di), pl.BlockSpec((None, n, di), lambda b, c: (b * nc + c, 0, 0))],
        out_shape=[jax.ShapeDtypeStruct((nb * nc * l, di), F32), jax.ShapeDtypeStruct((nb * nc, n, di), F32)],
        scratch_shapes=[pltpu.VMEM((n, di), F32), pltpu.VMEM((l, di), F32), pltpu.VMEM((l, di), F32),
                        pltpu.VMEM((8, di), F32)],
        compiler_params=_params("parallel", "arbitrary"),
    )(xc, dtr, dtrt, prow, pcol)


def _ssd_bwd(dy, xc, dtr, dtrt, prow, pcol, e_mat, sprev, nb, nc, di, name):
    l, n, g_n, p = SSM_CHUNK, SSM_STATE, SSM_GROUPS, SSM_HEAD_DIM
    cc = xc.shape[1]
    gw = di // g_n

    def body(dy_ref, xc_ref, dtr_ref, dtrt_ref, prow_ref, pcol_ref, e_ref, sprev_ref,
             dxc_ref, ddtr_ref, sums_ref, dst_ref, off_ref, dxd_ref, last_ref, vst_ref,
             dtx_ref, acsx_ref, dx_ref):
        first = jnp.logical_and(pl.program_id(0) == 0, pl.program_id(1) == 0)

        @pl.when(pl.program_id(1) == 0)
        def _():
            dst_ref[...] = jnp.zeros_like(dst_ref)

        head_row = lax.broadcasted_iota(jnp.int32, (LANES, l), 0)
        row_sums, col_sums = jnp.zeros((l, LANES), F32), jnp.zeros((LANES, l), F32)
        strict_lower = lax.broadcasted_iota(jnp.int32, (l, l), 0) > lax.broadcasted_iota(jnp.int32, (l, l), 1)

        q = _ssd_common(xc_ref, dtr_ref, dtrt_ref, prow_ref, pcol_ref, dtx_ref, acsx_ref, dx_ref, di)
        x = q["x"]
        xd = x * q["dt_exp"]
        xdb = xd.astype(BF16)
        xdf = (xd * q["f_exp"]).astype(BF16)
        dyv = dy_ref[...]
        dyb = dyv.astype(BF16)
        dye = (dyv * q["e_exp"]).astype(BF16)
        upper_b = q["upper_b"]
        lane = lax.broadcasted_iota(jnp.int32, (l, LANES), 1)
        for g in range(g_n):
            lo = g * gw
            bg = xc_ref[:, di + g * n: di + (g + 1) * n]
            cg = xc_ref[:, di + g_n * n + g * n: di + g_n * n + (g + 1) * n]
            cb = _dot(cg, bg, NT)
            st_g = sprev_ref[:, lo:lo + gw]
            st_gb = st_g.astype(BF16)
            dst_g = dst_ref[:, lo:lo + gw]
            dst_gb = dst_g.astype(BF16)
            dye_g = dye[:, lo:lo + gw]
            xdf_g = xdf[:, lo:lo + gw]
            y_off = q["e_exp"][:, lo:lo + gw] * _dot(cg, st_gb)
            dc_g = _dot(dye_g, st_gb, NT)
            db_g = _dot(xdf_g, dst_gb, NT)
            dxd_state = _dot(bg, dst_gb) * q["f_exp"][:, lo:lo + gw]
            last_ref[:, lo:lo + gw] = jnp.sum(dst_g * st_g, axis=0, keepdims=True)
            dst_ref[:, lo:lo + gw] = q["el_exp"][:, lo:lo + gw] * dst_g + _dot(cg, dye_g, TN)
            off_ref[:, lo:lo + gw] = dyv[:, lo:lo + gw] * y_off
            vst_ref[:, lo:lo + gw] = xd[:, lo:lo + gw] * dxd_state
            dcb = jnp.zeros((l, l), F32)
            for pr in range(gw // LANES):
                c0 = lo + pr * LANES
                h0 = c0 // p
                xp = xdb[:, c0:c0 + LANES]
                dyp = dyb[:, c0:c0 + LANES]
                dxd_diag = jnp.zeros((l, LANES), F32)
                for k, keep in enumerate((lane < p, lane >= p)):
                    dec = _decay(q, h0 + k)
                    dy_h = jnp.where(keep, dyp, 0)
                    dm_dec = _dot(dy_h, xp, NT) * dec
                    dcb = dcb + dm_dec
                    dxd_diag = dxd_diag + _dot((cb * dec).astype(BF16), dy_h, TN)
                    qm = dm_dec * cb
                    row_sums = jnp.where(lane == h0 + k, jnp.sum(qm, axis=1, keepdims=True), row_sums)
                    col_sums = jnp.where(head_row == h0 + k, jnp.sum(qm, axis=0, keepdims=True), col_sums)
                dxd_ref[:, c0:c0 + LANES] = dxd_diag + dxd_state[:, pr * LANES:(pr + 1) * LANES]
            dcb_b = dcb.astype(BF16)
            dxc_ref[:, di + g * n: di + (g + 1) * n] = (db_g + _dot(dcb_b, cg, TN)).astype(BF16)
            dxc_ref[:, di + g_n * n + g * n: di + g_n * n + (g + 1) * n] = (dc_g + _dot(dcb_b, bg)).astype(BF16)
        dxd = dxd_ref[...]
        e_b = e_ref[...]
        from_y = _exact_left(upper_b, _head_sums(off_ref[...], e_b) + row_sums - col_sums.T)
        from_s = _exact_left(strict_lower.astype(BF16), _head_sums(vst_ref[...], e_b))
        carried = _head_sums(jnp.broadcast_to(last_ref[...], (8, di)), e_b)[0:1, :] * jnp.exp(q["acs"][l - 1:l, :])
        dla = from_y + from_s + carried
        ddt = dla * q["a_r"] + _head_sums(dxd * x, e_b)
        ddtr = ddt * jax.nn.sigmoid(q["sp_in"])
        ddtr_ref[...] = ddtr
        dxc_ref[:, 0:di] = (dxd * q["dt_exp"] + q["d_exp"] * dyv).astype(BF16)
        dd_exp = jnp.sum(dyv * x, axis=0, keepdims=True)
        dd = _head_sums(jnp.broadcast_to(dd_exp, (8, di)), e_b)[0:1, :]
        part = _stack_rows([jnp.sum(ddtr, axis=0, keepdims=True),
                            jnp.sum(dla * q["dt"], axis=0, keepdims=True) * q["a_r"], dd], LANES)

        @pl.when(first)
        def _():
            sums_ref[...] = part

        @pl.when(jnp.logical_not(first))
        def _():
            sums_ref[...] += part

    rev = lambda b, c: b * nc + (nc - 1 - c)
    tok = lambda w: pl.BlockSpec((l, w), lambda b, c: (rev(b, c), 0))
    const = lambda r, w: pl.BlockSpec((r, w), lambda b, c: (0, 0))
    return pl.pallas_call(
        body, name=name, grid=(nb, nc),
        in_specs=[tok(di), tok(cc), tok(LANES), pl.BlockSpec((LANES, l), lambda b, c: (0, rev(b, c))),
                  const(8, LANES), const(LANES, 8), const(LANES, di),
                  pl.BlockSpec((None, n, di), lambda b, c: (rev(b, c), 0, 0))],
        out_specs=[tok(cc), tok(LANES), const(8, LANES)],
        out_shape=[jax.ShapeDtypeStruct((nb * nc * l, cc), BF16), jax.ShapeDtypeStruct((nb * nc * l, LANES), F32),
                   jax.ShapeDtypeStruct((8, LANES), F32)],
        scratch_shapes=[pltpu.VMEM((n, di), F32), pltpu.VMEM((l, di), F32), pltpu.VMEM((l, di), F32),
                        pltpu.VMEM((1, di), F32), pltpu.VMEM((l, di), F32),
                        pltpu.VMEM((l, di), F32), pltpu.VMEM((l, di), F32), pltpu.VMEM((8, di), F32)],
        compiler_params=_params("arbitrary", "arbitrary"),
    )(dy, xc, dtr, dtrt, prow, pcol, e_mat, sprev)


def _gate_norm_fwd(y, proj, z_col0, norm_g, di, name, tt=256):
    t = y.shape[0]
    tt = _tile(t, tt, 8)
    gw = di // SSM_GROUPS
    zw = _tile(math.gcd(di, z_col0), di, LANES)
    nz, zoff = di // zw, z_col0 // zw

    def body(*refs):
        y_ref, z_refs, g_ref, o_ref = refs[0], refs[1:1 + nz], refs[1 + nz], refs[2 + nz]
        for g in range(SSM_GROUPS):
            lo = g * gw
            zv = z_refs[lo // zw][:, lo % zw:lo % zw + gw].astype(F32)
            yg = y_ref[:, lo:lo + gw] * _silu(zv)
            r = lax.rsqrt(jnp.mean(yg * yg, axis=-1, keepdims=True) + EPS)
            o_ref[:, lo:lo + gw] = (yg * r * g_ref[:, lo:lo + gw]).astype(BF16)

    row = pl.BlockSpec((tt, di), lambda i: (i, 0))
    zspecs = [pl.BlockSpec((tt, zw), functools.partial(lambda i, k: (i, zoff + k), k=k)) for k in range(nz)]
    return pl.pallas_call(
        body, name=name, grid=(t // tt,), in_specs=[row] + zspecs + [pl.BlockSpec((1, di), lambda i: (0, 0))],
        out_specs=row, out_shape=jax.ShapeDtypeStruct((t, di), BF16), compiler_params=_params("parallel"),
    )(y, *([proj] * nz), norm_g)


def _gate_norm_bwd(dn, y, proj, z_col0, norm_g, di, name, tt=256):
    t = y.shape[0]
    tt = _tile(t, tt, 8)
    gw = di // SSM_GROUPS
    zw = _tile(math.gcd(di, z_col0), di, LANES)
    nz, zoff = di // zw, z_col0 // zw

    def body(*refs):
        dn_ref, y_ref, z_refs, g_ref = refs[0], refs[1], refs[2:2 + nz], refs[2 + nz]
        dy_ref, dz_ref, dg_ref = refs[3 + nz:]
        first = pl.program_id(0) == 0
        for g in range(SSM_GROUPS):
            lo = g * gw
            zv = z_refs[lo // zw][:, lo % zw:lo % zw + gw].astype(F32)
            yv = y_ref[:, lo:lo + gw]
            sz = _silu(zv)
            yg = yv * sz
            r = lax.rsqrt(jnp.mean(yg * yg, axis=-1, keepdims=True) + EPS)
            yh = yg * r
            dnv = dn_ref[:, lo:lo + gw].astype(F32)
            gy = dnv * g_ref[:, lo:lo + gw]
            dyg = r * (gy - yh * jnp.mean(gy * yh, axis=-1, keepdims=True))
            dy_ref[:, lo:lo + gw] = dyg * sz
            dz_ref[:, lo:lo + gw] = (dyg * yv * _dsilu(zv)).astype(BF16)
            part = jnp.sum(dnv * yh, axis=0, keepdims=True)

            @pl.when(first)
            def _():
                dg_ref[:, lo:lo + gw] = part

            @pl.when(jnp.logical_not(first))
            def _():
                dg_ref[:, lo:lo + gw] += part

    row = pl.BlockSpec((tt, di), lambda i: (i, 0))
    vec = pl.BlockSpec((1, di), lambda i: (0, 0))
    zspecs = [pl.BlockSpec((tt, zw), functools.partial(lambda i, k: (i, zoff + k), k=k)) for k in range(nz)]
    return pl.pallas_call(
        body, name=name, grid=(t // tt,), in_specs=[row, row] + zspecs + [vec], out_specs=[row, row, vec],
        out_shape=[jax.ShapeDtypeStruct((t, di), F32), jax.ShapeDtypeStruct((t, di), BF16), jax.ShapeDtypeStruct((1, di), F32)],
        compiler_params=_params("arbitrary"),
    )(dn, y, *([proj] * nz), norm_g)


def _softmax_rows(s):
    s = s - jnp.max(s, axis=-1, keepdims=True)
    e = jnp.exp(s)
    return e * (1.0 / jnp.sum(e, axis=-1, keepdims=True))


def _xattn_fwd(q, kv, nb, s, m, d, name, tq=1024):
    tq = _tile(s, tq, 8)
    nq = s // tq
    hd = d // XATTN_HEADS
    scale = 1.0 / math.sqrt(hd)

    def body(q_ref, k_ref, v_ref, o_ref):
        for h in range(XATTN_HEADS):
            sl = slice(h * hd, (h + 1) * hd)
            prob = _softmax_rows(_dot(q_ref[:, sl], k_ref[:, sl], NT) * scale)
            o_ref[:, sl] = _dot(prob.astype(BF16), v_ref[:, sl]).astype(BF16)

    return pl.pallas_call(
        body, name=name, grid=(nb, nq),
        in_specs=[pl.BlockSpec((tq, d), lambda b, i: (b * nq + i, 0)), pl.BlockSpec((m, d), lambda b, i: (b, 0)),
                  pl.BlockSpec((m, d), lambda b, i: (b, 1))],
        out_specs=pl.BlockSpec((tq, d), lambda b, i: (b * nq + i, 0)),
        out_shape=jax.ShapeDtypeStruct((nb * s, d), BF16), compiler_params=_params("parallel", "parallel"),
    )(q, kv, kv)


def _xattn_bwd(do, q, kv, nb, s, m, d, name, tq=1024):
    tq = _tile(s, tq, 8)
    nq = s // tq
    hd = d // XATTN_HEADS
    scale = 1.0 / math.sqrt(hd)

    def body(do_ref, q_ref, k_ref, v_ref, dq_ref, dk_ref, dv_ref):
        first = pl.program_id(1) == 0
        for h in range(XATTN_HEADS):
            sl = slice(h * hd, (h + 1) * hd)
            qh, kh, vh, doh = q_ref[:, sl], k_ref[:, sl], v_ref[:, sl], do_ref[:, sl]
            prob = _softmax_rows(_dot(qh, kh, NT) * scale)
            dv_h = _dot(prob.astype(BF16), doh, TN)
            dp = _dot(doh, vh, NT)
            ds = (prob * (dp - jnp.sum(dp * prob, axis=-1, keepdims=True)) * scale).astype(BF16)
            dq_ref[:, sl] = _dot(ds, kh).astype(BF16)
            dk_h = _dot(ds, qh, TN)

            @pl.when(first)
            def _():
                dk_ref[:, sl] = dk_h
                dv_ref[:, sl] = dv_h

            @pl.when(jnp.logical_not(first))
            def _():
                dk_ref[:, sl] += dk_h
                dv_ref[:, sl] += dv_h

    qspec = pl.BlockSpec((tq, d), lambda b, i: (b * nq + i, 0))
    dq, dk, dv = pl.pallas_call(
        body, name=name, grid=(nb, nq),
        in_specs=[qspec, qspec, pl.BlockSpec((m, d), lambda b, i: (b, 0)), pl.BlockSpec((m, d), lambda b, i: (b, 1))],
        out_specs=[qspec, pl.BlockSpec((m, d), lambda b, i: (b, 0)), pl.BlockSpec((m, d), lambda b, i: (b, 0))],
        out_shape=[jax.ShapeDtypeStruct((nb * s, d), BF16), jax.ShapeDtypeStruct((nb * m, d), F32),
                   jax.ShapeDtypeStruct((nb * m, d), F32)],
        compiler_params=_params("parallel", "arbitrary"),
    )(do, q, kv, kv)
    return dq, dk, dv


def _all_gather(shards, name):
    n_arr = len(shards)

    def body(*refs):
        x_refs, out_refs = refs[:n_arr], refs[n_arr:2 * n_arr]
        send_sems, recv_sems, local_sems = refs[2 * n_arr:]
        x, y, c = lax.axis_index("x"), lax.axis_index("y"), lax.axis_index("c")
        me, sibling = (x, y, c), (x, y, 1 - c)
        chips = [(1 - x, y), (x, 1 - y), (1 - x, 1 - y)]

        def copy(w, k, block, to, from_input=False):
            px, py, pc = block
            rows = out_refs[w].at[4 * px + 2 * py + pc]
            return pltpu.make_async_remote_copy(
                src_ref=x_refs[w] if from_input else rows, dst_ref=rows,
                send_sem=send_sems.at[7 * w + k], recv_sem=recv_sems.at[7 * w + k], device_id=to, device_id_type=MESH)

        started = []
        for w in range(n_arr):
            mine = pltpu.make_async_copy(x_refs[w], out_refs[w].at[4 * x + 2 * y + c], local_sems.at[w])
            mine.start()
            started.append(mine)
        sends = []
        for w in range(n_arr):
            sends.append(copy(w, 0, me, sibling, from_input=True))
            sends += [copy(w, 1 + j, me, (*chip, c), from_input=True) for j, chip in enumerate(chips)]
        for cp in sends:
            cp.start()
        for j, chip in enumerate(chips):
            for w in range(n_arr):
                copy(w, 1 + j, (*chip, c), me).wait_recv()
                passed = copy(w, 4 + j, (*chip, c), sibling)
                passed.start()
                sends.append(passed)
        for w in range(n_arr):
            copy(w, 0, sibling, me).wait_recv()
            for j, chip in enumerate(chips):
                copy(w, 4 + j, (*chip, 1 - c), me).wait_recv()
        for cp in sends:
            cp.wait_send()
        for mine in started:
            mine.wait()

    hbm = pl.BlockSpec(memory_space=pl.ANY)
    return pl.pallas_call(
        body, name=name, out_shape=[jax.ShapeDtypeStruct((N_DEV,) + s.shape, s.dtype) for s in shards],
        in_specs=[hbm] * n_arr, out_specs=[hbm] * n_arr,
        scratch_shapes=[pltpu.SemaphoreType.DMA((7 * n_arr,)), pltpu.SemaphoreType.DMA((7 * n_arr,)),
                        pltpu.SemaphoreType.DMA((n_arr,))],
    )(*shards)


_HBM = pl.BlockSpec(memory_space=pltpu.HBM)
_SEM = pl.BlockSpec(memory_space=pltpu.SEMAPHORE)
_DATAFLOW = pltpu.SideEffectType.DATAFLOW_SIDE_EFFECTING


def _peer_list(x, y, c):
    return [(1 - x if k & 4 else x, 1 - y if k & 2 else y, 1 - c if k & 1 else c) for k in range(1, N_DEV)]


def _push_copy(src_ref, land_ref, send_sems, recv_sems, w, k, peer, me, per_peer_src, receiving):
    px, py, pc = peer
    peer_slot = 4 * px + 2 * py + pc
    return pltpu.make_async_remote_copy(
        src_ref=src_ref.at[peer_slot] if per_peer_src else src_ref,
        dst_ref=land_ref.at[peer_slot if receiving else me],
        send_sem=send_sems.at[7 * w + k], recv_sem=recv_sems.at[7 * w + k], device_id=peer, device_id_type=MESH)


def _push_start(srcs, per_peer_src, after, name):
    n_arr = len(srcs)
    land_shapes = [s.shape if per_peer_src else (N_DEV,) + s.shape for s in srcs]

    def body(*refs):
        src_refs, land_refs = refs[:n_arr], refs[n_arr:2 * n_arr]
        send_sems, recv_sems = refs[2 * n_arr + 1], refs[2 * n_arr + 2]
        token = refs[-1]
        x, y, c = lax.axis_index("x"), lax.axis_index("y"), lax.axis_index("c")
        me = 4 * x + 2 * y + c
        for w in range(n_arr):
            for k, peer in enumerate(_peer_list(x, y, c)):
                _push_copy(src_refs[w], land_refs[w], send_sems, recv_sems, w, k, peer, me, per_peer_src, False).start()
        token[...] = jnp.zeros_like(token)

    lands = [pltpu.with_memory_space_constraint(lax.empty(ls, s.dtype), pltpu.HBM) for ls, s in zip(land_shapes, srcs)]
    srcs_hbm = [pltpu.with_memory_space_constraint(s, pltpu.HBM) for s in srcs]
    out = pl.pallas_call(
        body, name=name,
        out_shape=(pltpu.SemaphoreType.DMA((7 * n_arr,)), pltpu.SemaphoreType.DMA((7 * n_arr,)),
                   *[pltpu.HBM(s.shape, s.dtype) for s in srcs], *[pltpu.HBM(ls, s.dtype) for ls, s in zip(land_shapes, srcs)],
                   jax.ShapeDtypeStruct((8, LANES), F32)),
        in_specs=[_HBM] * (2 * n_arr) + [pl.BlockSpec(memory_space=pl.ANY)],
        out_specs=(_SEM, _SEM, *([_HBM] * (2 * n_arr)), pl.BlockSpec(memory_space=pltpu.VMEM)),
        input_output_aliases={i: 2 + i for i in range(2 * n_arr)},
        compiler_params=pltpu.CompilerParams(has_side_effects=_DATAFLOW),
    )(*srcs_hbm, *lands, after)
    return dict(send=out[0], recv=out[1], srcs=list(out[2:2 + n_arr]), lands=list(out[2 + n_arr:2 + 2 * n_arr]),
                token=out[-1])


def _push_wait(pending, per_peer_src, after, name):
    n_arr = len(pending["srcs"])

    def body(*refs):
        src_refs, land_refs = refs[:n_arr], refs[n_arr:2 * n_arr]
        send_sems, recv_sems = refs[2 * n_arr], refs[2 * n_arr + 1]
        x, y, c = lax.axis_index("x"), lax.axis_index("y"), lax.axis_index("c")
        me = 4 * x + 2 * y + c
        for w in range(n_arr):
            for k, peer in enumerate(_peer_list(x, y, c)):
                cp = _push_copy(src_refs[w], land_refs[w], send_sems, recv_sems, w, k, peer, me, per_peer_src, True)
                cp.wait_send()
                cp.wait_recv()

    out = pl.pallas_call(
        body, name=name,
        out_shape=tuple(pltpu.HBM(a.shape, a.dtype) for a in pending["srcs"] + pending["lands"]),
        in_specs=[_HBM] * (2 * n_arr) + [_SEM, _SEM, pl.BlockSpec(memory_space=pl.ANY)],
        out_specs=tuple([_HBM] * (2 * n_arr)),
        input_output_aliases={i: i for i in range(2 * n_arr)},
        compiler_params=pltpu.CompilerParams(has_side_effects=_DATAFLOW),
    )(*pending["srcs"], *pending["lands"], pending["send"], pending["recv"], after)
    return list(out[:n_arr]), list(out[n_arr:])


def _adamw_math(w, g, m, v):
    m = ADAM_B1 * m + (1.0 - ADAM_B1) * g
    v = ADAM_B2 * v + (1.0 - ADAM_B2) * (g * g)
    m_hat = m / (1.0 - ADAM_B1 ** ADAM_STEP)
    v_hat = v / (1.0 - ADAM_B2 ** ADAM_STEP)
    delta = -ADAM_LR * (m_hat / (jnp.sqrt(v_hat) + ADAM_EPS) + ADAM_WD * w)
    return delta, m, v


def _sum8(parts, name, tr=512):
    _, r, c_dim = parts.shape
    tr = _tile(r, tr, BF16_SUBLANES)

    def body(p_ref, o_ref):
        acc = p_ref[0].astype(F32)
        for k in range(1, N_DEV):
            acc = acc + p_ref[k].astype(F32)
        o_ref[...] = acc

    return pl.pallas_call(
        body, name=name, grid=(r // tr,), in_specs=[pl.BlockSpec((N_DEV, tr, c_dim), lambda i: (0, i, 0))],
        out_specs=pl.BlockSpec((tr, c_dim), lambda i: (i, 0)),
        out_shape=jax.ShapeDtypeStruct((r, c_dim), F32), compiler_params=_params("parallel"),
    )(parts)


def _sum8_adamw(parts, w, m, v, name, tr=128):
    _, r, c_dim = parts.shape
    tr = _tile(r, tr, BF16_SUBLANES)
    tc = c_dim if tr <= 2 * LANES else _tile(c_dim, LANES, LANES)

    def body(p_ref, w_ref, m_ref, v_ref, g_ref, d_ref, nm_ref, nv_ref):
        g = p_ref[0].astype(F32)
        for k in range(1, N_DEV):
            g = g + p_ref[k].astype(F32)
        g_ref[...] = g
        d_ref[...], nm_ref[...], nv_ref[...] = _adamw_math(w_ref[...], g, m_ref[...], v_ref[...])

    blk = pl.BlockSpec((None, tr, tc), lambda i, j: (0, i, j))
    out = jax.ShapeDtypeStruct((1, r, c_dim), F32)
    return pl.pallas_call(
        body, name=name, grid=(r // tr, c_dim // tc),
        in_specs=[pl.BlockSpec((N_DEV, tr, tc), lambda i, j: (0, i, j)), blk, blk, blk],
        out_specs=[blk] * 4, out_shape=[out] * 4, compiler_params=_params("parallel", "parallel"),
    )(parts, w, m, v)


def _adamw(g, w, m, v, name):
    r, c_dim = g.shape

    def body(g_ref, w_ref, m_ref, v_ref, d_ref, nm_ref, nv_ref):
        d_ref[...], nm_ref[...], nv_ref[...] = _adamw_math(w_ref[...], g_ref[...], m_ref[...], v_ref[...])

    out = jax.ShapeDtypeStruct((r, c_dim), F32)
    return pl.pallas_call(body, name=name, out_shape=[out] * 3)(g, w, m, v)


def _pack_rows(arrays, dtype, row_unit):
    chunks, offs, r0 = [], [], 0
    for a in arrays:
        flat = a.reshape(-1).astype(dtype)
        rows = -(-flat.shape[0] // (LANES * row_unit)) * row_unit
        flat = jnp.pad(flat, (0, rows * LANES - flat.shape[0]))
        chunks.append(flat.reshape(rows, LANES))
        offs.append((r0, rows))
        r0 += rows
    return jnp.concatenate(chunks, axis=0), offs


def _unpack_rows(packed, offs, shapes):
    out = []
    for (r0, rows), shape in zip(offs, shapes):
        n = math.prod(shape)
        blk = packed[..., r0:r0 + rows, :]
        blk = blk.reshape(packed.shape[:-2] + (rows * LANES,))[..., :n]
        out.append(blk.reshape(packed.shape[:-2] + tuple(shape)))
    return out


def _full_from_slots(blk, col_sharded):
    _, r, c = blk.shape
    if col_sharded:
        return blk.transpose(1, 0, 2).reshape(r, N_DEV * c)
    return blk.reshape(N_DEV * r, c)


def _ffn_fwd(h, n, w_gu_t, w_d, tag, next_gain=None):
    gate, up, a = _ffn_up(n, w_gu_t, f"{tag}_up")
    out = _mm(a, w_d, F32, f"{tag}_down", res=h, alpha=FFN_RES_WEIGHT, norm_out=next_gain)
    h_out, n_next = out if next_gain is not None else (out, None)
    return h_out, (h, n, gate, up, a), n_next


def _ffn_bwd(dh_out, saved, g, w_gu_t, w_d, tag, dep, send_grads):
    h, n, gate, up, a = saved
    dw_d = _mm_tn(a, dh_out, f"{tag}_dw_down", alpha=FFN_RES_WEIGHT, dep=dep)
    dgate, dup = _ffn_da(dh_out, w_d, gate, up, f"{tag}_da", FFN_RES_WEIGHT)
    f = dgate.shape[1]
    dw_gu_t = _mm_tn(dgate, n, f"{tag}_dw_gate", out_rows=2 * f)
    dw_gu_t = _mm_tn(dup, n, f"{tag}_dw_up", out_rows=2 * f, row_off=f, into=dw_gu_t)
    dep = send_grads(dw_gu_t, dw_d)
    return _mm([dgate, dup], w_gu_t, F32, f"{tag}_dn", dep=dep, tk=1408, res=dh_out, norm_bwd=(h, g))


W_GROUPS = (("ffn1_w_gate_up", "ffn1_w_down"),
            ("w_in",),
            ("w_out_a", "w_out_ssm", "w_mix_out"),
            ("w_q", "w_kv", "w_o_x", "ffn2_w_gate_up", "ffn2_w_down"))
G_GROUPS = (("ffn2_w_gate_up", "ffn2_w_down"),
            ("w_o_x", "w_q", "w_kv", "w_mix_out", "w_out_a", "w_out_ssm", "w_in"),
            ("ffn1_w_gate_up", "ffn1_w_down"))


def _local_step(x3, mem3, target3, small, comm):
    nb, s, d = x3.shape
    m_len = mem3.shape[1]
    t = nb * s
    nc = s // SSM_CHUNK
    di = small["ssm_norm"].shape[1]
    hs = di // SSM_HEAD_DIM
    cc = di + 2 * SSM_GROUPS * SSM_STATE
    x, mem, target = x3.reshape(t, d), mem3.reshape(nb * m_len, d), target3.reshape(t, d)

    sizes = (d, d, d, di, cc, hs, d, d)
    offs = [0]
    for sz in sizes:
        offs.append(offs[-1] + sz)
    cb_a = _tile(d, CONV_A_BLOCK, LANES)
    xbc_col0, z_col0 = 3 * d, 3 * d + cc
    ga_blk, gb_blk = (3 * d + di + cc) // d, (4 * d + di + cc) // d

    pad_vec = lambda v: jnp.pad(v.reshape(1, -1), ((0, 0), (0, LANES - hs)))
    prow = jnp.concatenate([pad_vec(small["ssm_dt_bias"]), pad_vec(small["ssm_a_log"]), pad_vec(small["ssm_d"]),
                            jnp.zeros((5, LANES), F32)], axis=0)
    pcol = prow.T
    e_mat = (lax.broadcasted_iota(jnp.int32, (LANES, di), 0)
             == lax.broadcasted_iota(jnp.int32, (LANES, di), 1) // SSM_HEAD_DIM).astype(BF16)
    conv_a_w8 = jnp.pad(small["conv_a_w"][0], ((0, 8 - CONV_A_K), (0, 0)))
    ssm_conv_w8 = jnp.pad(small["ssm_conv_w"][0], ((0, 8 - SSM_CONV_K), (0, 0)))

    wts, dep = comm.weights(0, None)
    n1 = _rms_fwd(x, small["ffn1_norm"] + dep[0, 0], "ffn1_norm")
    h1, ffn1_saved, u = _ffn_fwd(x, n1, wts["ffn1_w_gate_up"], wts["ffn1_w_down"], "ffn1", small["mix_norm"])
    got, dep = comm.weights(1, h1)
    wts.update(got)
    w_in_t = wts["w_in"]
    conv_rows = [w_in_t[offs[i] + j * cb_a:offs[i] + (j + 1) * cb_a] for j in range(d // cb_a) for i in (0, 1, 2)]
    w_main_t = jnp.concatenate(conv_rows + [w_in_t[offs[i]:offs[i + 1]] for i in (4, 3, 6, 7)], axis=0)
    w_dt_t = jnp.pad(w_in_t[offs[5]:offs[6]], ((0, LANES - hs), (0, 0)))
    proj = _mm(u, w_main_t, BF16, "in_proj", nt=True, dep=dep)
    dtr = _mm(u, w_dt_t, F32, "in_proj_dt", nt=True)
    yap = _conv_a_fwd(proj, conv_a_w8, nb, s, d, "conv_a")
    got, dep = comm.weights(2, yap)
    wts.update(got)
    y_a = _mm(yap, wts["w_out_a"], BF16, "out_a", dep=dep)
    xc, conv_pre = _conv_s_fwd(proj, xbc_col0, ssm_conv_w8, small["ssm_conv_b"] + dep[0, 0], nb, s, cc, "conv_s")
    dtrt = dtr.T
    y_ssd, sprev = _ssd_fwd(xc, dtr, dtrt, prow, pcol, nb, nc, di, "ssd")
    ygn = _gate_norm_fwd(y_ssd, proj, z_col0, small["ssm_norm"], di, "gate_norm")
    y_b = _mm(ygn, wts["w_out_ssm"], BF16, "out_ssm")
    merged = _merge_fwd(y_a, y_b, proj, ga_blk, gb_blk, d, "merge")
    h2, un = _mm(merged, wts["w_mix_out"], F32, "mix_out", res=h1, norm_out=small["xattn_norm"])
    got, _ = comm.weights(3, h2)
    wts.update(got)
    mn = _rms_fwd(mem, small["mem_norm"], "mem_norm")
    q = _mm(un, wts["w_q"], BF16, "q_proj")
    kv = _mm(mn, wts["w_kv"], BF16, "kv_proj", nt=True)
    o = _xattn_fwd(q, kv, nb, s, m_len, d, "xattn")
    h3, n2 = _mm(o, wts["w_o_x"], F32, "o_proj", res=h2, norm_out=small["ffn2_norm"])
    h4, ffn2_saved, _ = _ffn_fwd(h3, n2, wts["ffn2_w_gate_up"], wts["ffn2_w_down"], "ffn2")
    loss_vec, dh4, dg_final = _loss_head(h4, small["final_norm"].reshape(1, d), target, "loss_head")

    grads = {"final_norm": dg_final.reshape(d)}
    big = {}
    dh3, grads["ffn2_norm"] = _ffn_bwd(
        dh4, ffn2_saved, small["ffn2_norm"], wts["ffn2_w_gate_up"], wts["ffn2_w_down"], "ffn2", None,
        lambda dw_gu_t, dw_d: comm.grads(0, {"ffn2_w_gate_up": dw_gu_t, "ffn2_w_down": dw_d}))
    big["w_o_x"] = _mm_tn(o, dh3, "dw_o")
    do = _mm(dh3, wts["w_o_x"], BF16, "d_o", nt=True)
    dq, dk, dv = _xattn_bwd(do, q, kv, nb, s, m_len, d, "xattn_bwd")
    big["w_q"] = _mm_tn(un, dq, "dw_q")
    big["w_kv"] = _mm_tn(dv, mn, "dw_v", out_rows=2 * d, row_off=d, into=_mm_tn(dk, mn, "dw_k", out_rows=2 * d))
    _, grads["mem_norm"] = _mm([dk, dv], wts["w_kv"], F32, "d_mn", norm_bwd=(mem, small["mem_norm"]))
    dh2, grads["xattn_norm"] = _mm(dq, wts["w_q"], F32, "d_un", nt=True, res=dh3, norm_bwd=(h2, small["xattn_norm"]))
    big["w_mix_out"] = _mm_tn(merged, dh2, "dw_mix")
    dmerged = _mm(dh2, wts["w_mix_out"], BF16, "d_merged", nt=True)
    dya, dyb, dg = _merge_bwd(dmerged, y_a, y_b, proj, ga_blk, gb_blk, d, "merge_bwd")
    big["w_out_a"] = _mm_tn(yap, dya, "dw_out_a")
    big["w_out_ssm"] = _mm_tn(ygn, dyb, "dw_out_ssm")
    dyap = _mm(dya, wts["w_out_a"], BF16, "d_yap", nt=True)
    dygn = _mm(dyb, wts["w_out_ssm"], BF16, "d_ygn", nt=True)
    dcx, dconv_a = _conv_a_bwd(dyap, proj, conv_a_w8, nb, s, d, 3 * d + cc, "conv_a_bwd")
    dy_ssd, dz, grads["ssm_norm"] = _gate_norm_bwd(dygn, y_ssd, proj, z_col0, small["ssm_norm"], di, "gate_norm_bwd")
    dxc, ddtr, ssd_sums = _ssd_bwd(dy_ssd, xc, dtr, dtrt, prow, pcol, e_mat, sprev, nb, nc, di, "ssd_bwd")
    dcx, dconv_s, grads["ssm_conv_b"] = _conv_s_bwd(dxc, conv_pre, proj, xbc_col0, ssm_conv_w8, nb, s, cc, dcx, "conv_s_bwd")
    dw_cx, dw_z, dw_g = _mm_tn(dcx, u, "dw_in_cx"), _mm_tn(dz, u, "dw_in_z"), _mm_tn(dg, u, "dw_in_g")
    dw_dt = _mm_tn(ddtr, u, "dw_in_dt")[:hs]
    du_main = _mm([dcx, dz, dg], w_main_t, F32, "d_u")
    dh1, grads["mix_norm"] = _mm(ddtr, w_dt_t, F32, "d_u_dt", res=dh2, norm_bwd=(h1, small["mix_norm"], du_main))
    bcv = [[dw_cx[(3 * j + i) * cb_a:(3 * j + i + 1) * cb_a] for j in range(d // cb_a)] for i in range(3)]
    big["w_in"] = jnp.concatenate(bcv[0] + bcv[1] + bcv[2] + [dw_z, dw_cx[3 * d:], dw_dt, dw_g], axis=0)
    dep = comm.grads(1, big)
    dx, grads["ffn1_norm"] = _ffn_bwd(
        dh1, ffn1_saved, small["ffn1_norm"], wts["ffn1_w_gate_up"], wts["ffn1_w_down"], "ffn1", dep,
        lambda dw_gu_t, dw_d: comm.grads(2, {"ffn1_w_gate_up": dw_gu_t, "ffn1_w_down": dw_d}))

    grads["conv_a_w"] = dconv_a[:CONV_A_K]
    grads["ssm_conv_w"] = dconv_s[:SSM_CONV_K]
    grads["ssm_dt_bias"] = ssd_sums[0:1, :hs]
    grads["ssm_a_log"] = ssd_sums[1:2, :hs]
    grads["ssm_d"] = ssd_sums[2:3, :hs]
    return loss_vec[0, 0], dx.reshape(nb, s, d), grads


def _step(inputs):
    w = {k: inputs[k] for k in WEIGHT_ORDER}
    mom = {k: inputs["m_" + k] for k in WEIGHT_ORDER}
    vel = {k: inputs["v_" + k] for k in WEIGHT_ORDER}
    me = 4 * lax.axis_index("x") + 2 * lax.axis_index("y") + lax.axis_index("c")

    send = {k: (w[k][0].T if k in COL_SHARDED else w[k][0]).astype(BF16) for k in BIG_WEIGHTS}

    def own_slot(land, mine):
        return lax.dynamic_update_slice(land, mine[None], (me, 0, 0))

    gathers, exchanges = {}, {}

    def weights(i, after):
        if i == 0:
            lands = _all_gather([send[k] for k in W_GROUPS[0]], "gather0")
        else:
            sent, lands = _push_wait(gathers[i], False, after, f"gather{i}_wait")
            lands = [own_slot(land, mine) for land, mine in zip(lands, sent)]
        full = {k: land.reshape(N_DEV * land.shape[1], land.shape[2]) for k, land in zip(W_GROUPS[i], lands)}
        dep = jnp.zeros((8, LANES), F32)
        if i + 1 < len(W_GROUPS):
            gathers[i + 1] = _push_start([send[k] for k in W_GROUPS[i + 1]], False, lands[0], f"gather{i + 1}_start")
            dep = gathers[i + 1]["token"]
        return full, dep

    def slot_shape(k):
        rows, cols = send[k].shape
        return (rows * cols // LANES, LANES) if rows % BF16_SUBLANES else (rows, cols)

    def send_grads(i, by_name):
        slots = [by_name[k].reshape((N_DEV,) + slot_shape(k)) for k in G_GROUPS[i]]
        exchanges[i] = _push_start(slots, True, slots[0], f"exchange{i}_start")
        return exchanges[i]["token"]

    comm = types.SimpleNamespace(weights=weights, grads=send_grads)
    small = {k: w[k] for k in SMALL_REPLICATED}
    conv_shapes = [w[k].shape[1:] for k in SMALL_SHARDED]
    packed_c, conv_offs = _pack_rows([w[k][0] for k in SMALL_SHARDED], F32, 8)
    conv_blocks = _unpack_rows(_all_gather([packed_c], "gather_conv_weights")[0], conv_offs, conv_shapes)
    for k, b in zip(SMALL_SHARDED, conv_blocks):
        small[k] = _full_from_slots(b, True)[None]

    loss_local, grad_x, grads = _local_step(inputs["x"], inputs["mem"], inputs["loss_target"], small, comm)
    loss = lax.psum(loss_local, AXES)

    out = {}
    for i, names in enumerate(G_GROUPS):
        sent, lands = _push_wait(exchanges[i], True, grad_x, f"exchange{i}_wait")
        for k, land, slots in zip(names, lands, sent):
            parts = own_slot(land, lax.dynamic_index_in_dim(slots, me, 0, keepdims=False))
            if k in COL_SHARDED and w[k].shape[2] % LANES:
                flip = lambda a: a.transpose(0, 2, 1).reshape((1,) + slot_shape(k))
                unflip = lambda a: a.reshape((1,) + send[k].shape).transpose(0, 2, 1)
                out[k] = tuple(unflip(o) for o in _sum8_adamw(parts, flip(w[k]), flip(mom[k]), flip(vel[k]),
                                                             f"sum_adamw_{k}", tr=2048 if slot_shape(k)[1] == LANES else 128))
                continue
            if k in COL_SHARDED:
                parts = parts.transpose(0, 2, 1)
            out[k] = tuple(_sum8_adamw(parts, w[k], mom[k], vel[k], f"sum_adamw_{k}"))

    small_names = SMALL_REPLICATED + SMALL_SHARDED
    packed_g, small_offs = _pack_rows([grads[k] for k in small_names], F32, 8)
    total = _sum8(_all_gather([packed_g], "gather_small_grads")[0], "sum_small_grads")
    full_grads = _unpack_rows(total, small_offs, [grads[k].shape for k in small_names])
    mine = {}
    for k, g in zip(small_names, full_grads):
        if k in SMALL_SHARDED:
            c_loc = w[k].shape[2]
            g = lax.dynamic_slice_in_dim(g, me * c_loc, c_loc, axis=1)
        mine[k] = g.reshape(w[k].shape)
    sg, s_offs = _pack_rows([mine[k] for k in small_names], F32, 8)
    sw, _ = _pack_rows([w[k] for k in small_names], F32, 8)
    sm, _ = _pack_rows([mom[k] for k in small_names], F32, 8)
    sv, _ = _pack_rows([vel[k] for k in small_names], F32, 8)
    s_shapes = [w[k].shape for k in small_names]
    small_out = [_unpack_rows(a, s_offs, s_shapes) for a in _adamw(sg, sw, sm, sv, "adamw_small")]
    for i, k in enumerate(small_names):
        out[k] = (mine[k],) + tuple(o[i] for o in small_out)

    res = [loss, grad_x]
    for j in range(4):
        res += [out[k][j] for k in WEIGHT_ORDER]
    return tuple(res)


def kernel(x, mem, ffn1_norm, ffn1_w_gate_up, ffn1_w_down, mix_norm, w_in, conv_a_w, w_out_a, ssm_conv_w, ssm_conv_b, ssm_dt_bias, ssm_a_log, ssm_d, ssm_norm, w_out_ssm, w_mix_out, xattn_norm, mem_norm, w_q, w_kv, w_o_x, ffn2_norm, ffn2_w_gate_up, ffn2_w_down, final_norm, loss_target, m_ffn1_norm, m_ffn1_w_gate_up, m_ffn1_w_down, m_mix_norm, m_w_in, m_conv_a_w, m_w_out_a, m_ssm_conv_w, m_ssm_conv_b, m_ssm_dt_bias, m_ssm_a_log, m_ssm_d, m_ssm_norm, m_w_out_ssm, m_w_mix_out, m_xattn_norm, m_mem_norm, m_w_q, m_w_kv, m_w_o_x, m_ffn2_norm, m_ffn2_w_gate_up, m_ffn2_w_down, m_final_norm, v_ffn1_norm, v_ffn1_w_gate_up, v_ffn1_w_down, v_mix_norm, v_w_in, v_conv_a_w, v_w_out_a, v_ssm_conv_w, v_ssm_conv_b, v_ssm_dt_bias, v_ssm_a_log, v_ssm_d, v_ssm_norm, v_w_out_ssm, v_w_mix_out, v_xattn_norm, v_mem_norm, v_w_q, v_w_kv, v_w_o_x, v_ffn2_norm, v_ffn2_w_gate_up, v_ffn2_w_down, v_final_norm):
    return _step(dict(locals()))
```

```python
import functools
import math
import types

import jax
import jax.numpy as jnp
from jax import lax
from jax.experimental import pallas as pl
from jax.experimental.pallas import tpu as pltpu

F32, BF16 = jnp.float32, jnp.bfloat16
HI = lax.Precision.HIGHEST
MESH = pl.DeviceIdType.MESH
AXES = ("x", "y", "c")
N_DEV = 8

EPS = 1e-6
FFN_RES_WEIGHT = 0.5
SSM_HEAD_DIM = 64
SSM_GROUPS = 4
SSM_STATE = 128
SSM_CHUNK = 128
CONV_A_K = 3
SSM_CONV_K = 4
XATTN_HEADS = 4
ADAM_LR, ADAM_B1, ADAM_B2, ADAM_EPS, ADAM_WD, ADAM_STEP = 1e-3, 0.9, 0.999, 1e-8, 0.01, 10

LANES = 128
BF16_SUBLANES = 16
VMEM_LIMIT_BYTES = 56 * 2 ** 20
NEG_BIG = -1e30

BIG_WEIGHTS = ("ffn1_w_gate_up", "ffn1_w_down", "w_in", "w_out_a", "w_out_ssm", "w_mix_out",
               "w_q", "w_kv", "w_o_x", "ffn2_w_gate_up", "ffn2_w_down")
COL_SHARDED = ("ffn1_w_gate_up", "w_in", "w_kv", "ffn2_w_gate_up")
SMALL_REPLICATED = ("ffn1_norm", "mix_norm", "ssm_conv_b", "ssm_dt_bias", "ssm_a_log", "ssm_d", "ssm_norm",
                    "xattn_norm", "mem_norm", "ffn2_norm", "final_norm")
SMALL_SHARDED = ("conv_a_w", "ssm_conv_w")
WEIGHT_ORDER = ("ffn1_norm", "ffn1_w_gate_up", "ffn1_w_down", "mix_norm", "w_in", "conv_a_w", "w_out_a",
                "ssm_conv_w", "ssm_conv_b", "ssm_dt_bias", "ssm_a_log", "ssm_d", "ssm_norm", "w_out_ssm",
                "w_mix_out", "xattn_norm", "mem_norm", "w_q", "w_kv", "w_o_x", "ffn2_norm", "ffn2_w_gate_up",
                "ffn2_w_down", "final_norm")


def _tile(dim, pref, unit):
    best = None
    t = unit
    while t <= min(dim, pref):
        if dim % t == 0:
            best = t
        t += unit
    return best if best is not None else dim


def _params(*sem):
    return pltpu.CompilerParams(dimension_semantics=sem, vmem_limit_bytes=VMEM_LIMIT_BYTES)


def _sigmoid(x):
    return pl.reciprocal(1.0 + jnp.exp(-x), approx=True)


def _silu(x):
    return x * _sigmoid(x)


def _dsilu(x):
    s = _sigmoid(x)
    return s * (1.0 + x * (1.0 - s))


def _softplus(x):
    return jnp.maximum(x, 0.0) + jnp.log(1.0 + jnp.exp(-jnp.abs(x)))


def _dot(a, b, dims=(((1,), (0,)), ((), ())), precision=None):
    return lax.dot_general(a, b, dims, preferred_element_type=F32, precision=precision)


def _stack_rows(rows, width):
    r_idx = lax.broadcasted_iota(jnp.int32, (8, width), 0)
    acc = jnp.zeros((8, width), F32)
    for k, row in enumerate(rows):
        acc = jnp.where(r_idx == k, row, acc)
    return acc


NT = (((1,), (1,)), ((), ()))
TN = (((0,), (0,)), ((), ()))


def _mm(a, b, out_dtype, name, res=None, alpha=1.0, nt=False, dep=None, norm_out=None, norm_bwd=None,
        tm=1024, tn=2048, tk=2816):
    pieces = list(a) if isinstance(a, (list, tuple)) else [a]
    m = pieces[0].shape[0]
    k = sum(p.shape[1] for p in pieces)
    n = b.shape[0] if nt else b.shape[1]
    assert (b.shape[1] if nt else b.shape[0]) == k
    tm, tn = _tile(m, tm, 8), _tile(n, tn, LANES)
    tk = _tile(math.gcd(*[p.shape[1] for p in pieces]), tk, LANES)
    nk = k // tk
    starts, s0 = [], 0
    for p in pieces:
        starts.append((s0, p.shape[1] // tk))
        s0 += p.shape[1] // tk
    n_p = len(pieces)
    assert norm_out is None or norm_bwd is None
    whole_rows = norm_out is not None or norm_bwd is not None
    assert not whole_rows or tn == n
    has_pre = norm_bwd is not None and len(norm_bwd) == 3

    def body(*refs):
        a_refs, b_ref = refs[:n_p], refs[n_p]
        nxt = n_p + 1
        r_ref = refs[nxt] if res is not None else None
        nxt += (res is not None) + (dep is not None)
        g_ref = refs[nxt] if whole_rows else None
        x_ref = refs[nxt + 1] if norm_bwd is not None else None
        pre_ref = refs[nxt + 2] if has_pre else None
        nxt += whole_rows + (norm_bwd is not None) + has_pre
        o_ref = refs[nxt]
        o2_ref = refs[nxt + 1] if whole_rows else None
        scr = refs[nxt + 1 + whole_rows:]
        first_rows = pl.program_id(0) == 0

        def finish(acc):
            acc = alpha * acc if alpha != 1.0 else acc
            if norm_bwd is not None:
                if pre_ref is not None:
                    acc = acc + pre_ref[...]
                xv = x_ref[...]
                r = lax.rsqrt(jnp.mean(xv * xv, axis=-1, keepdims=True) + EPS)
                xh = xv * r
                gy = acc * g_ref[...]
                part = jnp.sum(acc * xh, axis=0, keepdims=True)
                acc = r * (gy - xh * jnp.mean(gy * xh, axis=-1, keepdims=True))

                @pl.when(first_rows)
                def _():
                    o2_ref[...] = part

                @pl.when(jnp.logical_not(first_rows))
                def _():
                    o2_ref[...] += part
            if r_ref is not None:
                acc = r_ref[...] + acc
            o_ref[...] = acc.astype(out_dtype)
            if norm_out is not None:
                r = lax.rsqrt(jnp.mean(acc * acc, axis=-1, keepdims=True) + EPS)
                o2_ref[...] = (acc * r * g_ref[...]).astype(BF16)

        def product(a_ref):
            return _dot(a_ref[...].astype(BF16), b_ref[...].astype(BF16), NT if nt else (((1,), (0,)), ((), ())))

        if nk == 1:
            finish(product(a_refs[0]))
            return
        acc_ref = scr[0]
        kk = pl.program_id(2)
        for (s, cnt), a_ref in zip(starts, a_refs):
            if s == 0:
                @pl.when(kk == 0)
                def _():
                    acc_ref[...] = product(a_ref)

                @pl.when(jnp.logical_and(kk > 0, kk < cnt))
                def _():
                    acc_ref[...] += product(a_ref)
            else:
                @pl.when(jnp.logical_and(kk >= s, kk < s + cnt))
                def _():
                    acc_ref[...] += product(a_ref)

        @pl.when(kk == nk - 1)
        def _():
            finish(acc_ref[...])

    def a_spec(s, cnt):
        return pl.BlockSpec((tm, tk), lambda i, j, kk: (i, jnp.clip(kk - s, 0, cnt - 1)))

    in_specs = [a_spec(s, cnt) for s, cnt in starts]
    in_specs.append(pl.BlockSpec((tn, tk), lambda i, j, kk: (j, kk)) if nt else pl.BlockSpec((tk, tn), lambda i, j, kk: (kk, j)))
    args = pieces + [b]
    if res is not None:
        in_specs.append(pl.BlockSpec((tm, tn), lambda i, j, kk: (i, j)))
        args.append(res)
    if dep is not None:
        in_specs.append(pl.BlockSpec((8, LANES), lambda i, j, kk: (0, 0)))
        args.append(dep)
    tile = pl.BlockSpec((tm, tn), lambda i, j, kk: (i, j))
    vec = pl.BlockSpec((1, n), lambda i, j, kk: (0, 0))
    out_specs, out_shape = tile, jax.ShapeDtypeStruct((m, n), out_dtype)
    if norm_out is not None:
        in_specs.append(vec)
        args.append(norm_out)
        out_specs, out_shape = [tile, tile], [out_shape, jax.ShapeDtypeStruct((m, n), BF16)]
    if norm_bwd is not None:
        in_specs += [vec, tile] + ([tile] if has_pre else [])
        args += [norm_bwd[1], norm_bwd[0]] + ([norm_bwd[2]] if has_pre else [])
        out_specs, out_shape = [tile, vec], [out_shape, jax.ShapeDtypeStruct((1, n), F32)]
    return pl.pallas_call(
        body, name=name, grid=(m // tm, n // tn, nk), in_specs=in_specs, out_specs=out_specs, out_shape=out_shape,
        scratch_shapes=[pltpu.VMEM((tm, tn), F32)] if nk > 1 else [],
        compiler_params=(_params("arbitrary", "arbitrary", "arbitrary") if norm_bwd is not None
                         else _params("parallel", "parallel", "arbitrary")),
    )(*args)


def _mm_tn(x, dy, name, out_dtype=BF16, alpha=1.0, dep=None, out_rows=None, row_off=0, into=None,
           tko=1408, tn=1024, tt=2048):
    t, k = x.shape
    n = dy.shape[1]
    tko, tn, tt = _tile(k, tko, LANES), _tile(n, tn, LANES), _tile(t, tt, 8)
    nt_steps = t // tt

    def body(*refs):
        x_ref, dy_ref = refs[:2]
        o_ref, acc_ref = refs[-2:]
        part = _dot(x_ref[...].astype(BF16), dy_ref[...].astype(BF16), TN)
        step = pl.program_id(2)

        @pl.when(step == 0)
        def _():
            acc_ref[...] = part

        @pl.when(step > 0)
        def _():
            acc_ref[...] += part

        @pl.when(step == nt_steps - 1)
        def _():
            acc = acc_ref[...]
            o_ref[...] = (alpha * acc if alpha != 1.0 else acc).astype(out_dtype)

    in_specs = [pl.BlockSpec((tt, tko), lambda i, j, s: (s, i)), pl.BlockSpec((tt, tn), lambda i, j, s: (s, j))]
    args = [x, dy]
    if dep is not None:
        in_specs.append(pl.BlockSpec((8, LANES), lambda i, j, s: (0, 0)))
        args.append(dep)
    aliases = {}
    if into is not None:
        in_specs.append(pl.BlockSpec(memory_space=pl.ANY))
        args.append(into)
        aliases = {len(args) - 1: 0}
    band = row_off // tko
    assert row_off % tko == 0
    return pl.pallas_call(
        body, name=name, grid=(k // tko, n // tn, nt_steps), in_specs=in_specs,
        out_specs=pl.BlockSpec((tko, tn), lambda i, j, s: (i + band, j)),
        out_shape=jax.ShapeDtypeStruct((out_rows or k, n), out_dtype),
        scratch_shapes=[pltpu.VMEM((tko, tn), F32)], input_output_aliases=aliases,
        compiler_params=_params("parallel", "parallel", "arbitrary"),
    )(*args)


def _rms_fwd(x, g, name, tt=512):
    t, d = x.shape
    tt = _tile(t, tt, 8)

    def body(x_ref, g_ref, o_ref):
        xv = x_ref[...]
        r = lax.rsqrt(jnp.mean(xv * xv, axis=-1, keepdims=True) + EPS)
        o_ref[...] = (xv * r * g_ref[...]).astype(BF16)

    return pl.pallas_call(
        body, name=name, grid=(t // tt,),
        in_specs=[pl.BlockSpec((tt, d), lambda i: (i, 0)), pl.BlockSpec((1, d), lambda i: (0, 0))],
        out_specs=pl.BlockSpec((tt, d), lambda i: (i, 0)),
        out_shape=jax.ShapeDtypeStruct((t, d), BF16), compiler_params=_params("parallel"),
    )(x, g)


def _ffn_up(n, w_gu_t, name, tm=512, tf=2816):
    t, d = n.shape
    f = w_gu_t.shape[0] // 2
    tm, tf = _tile(t, tm, 8), _tile(f, tf, LANES)
    nf = f // tf

    def body(n_ref, wg_ref, wu_ref, g_ref, u_ref, a_ref):
        nv = n_ref[...]
        gate, up = _dot(nv, wg_ref[...], NT), _dot(nv, wu_ref[...], NT)
        s = _sigmoid(gate)
        sg = gate * s
        g_ref[...] = (up * (s * (1.0 + gate * (1.0 - s)))).astype(BF16)
        u_ref[...] = sg.astype(BF16)
        a_ref[...] = (sg * up).astype(BF16)

    blk = pl.BlockSpec((tm, tf), lambda i, j: (i, j))
    out = jax.ShapeDtypeStruct((t, f), BF16)
    return pl.pallas_call(
        body, name=name, grid=(t // tm, nf),
        in_specs=[pl.BlockSpec((tm, d), lambda i, j: (i, 0)), pl.BlockSpec((tf, d), lambda i, j: (j, 0)),
                  pl.BlockSpec((tf, d), lambda i, j: (j + nf, 0))],
        out_specs=[blk, blk, blk], out_shape=[out, out, out], compiler_params=_params("parallel", "parallel"),
    )(n, w_gu_t, w_gu_t)


def _ffn_da(dh, w_d, gate, up, name, alpha, dep=None, tm=512, tf=2816):
    t, d = dh.shape
    f = w_d.shape[0]
    tm, tf = _tile(t, tm, 8), _tile(f, tf, LANES)

    def body(*refs):
        dh_ref, w_ref, g_ref, u_ref = refs[:4]
        dg_ref, du_ref = refs[-2:]
        da = alpha * _dot(dh_ref[...].astype(BF16), w_ref[...], NT)
        dg_ref[...] = (da * g_ref[...].astype(F32)).astype(BF16)
        du_ref[...] = (da * u_ref[...].astype(F32)).astype(BF16)

    blk = pl.BlockSpec((tm, tf), lambda i, j: (i, j))
    in_specs = [pl.BlockSpec((tm, d), lambda i, j: (i, 0)), pl.BlockSpec((tf, d), lambda i, j: (j, 0)), blk, blk]
    args = [dh, w_d, gate, up]
    if dep is not None:
        in_specs.append(pl.BlockSpec((8, LANES), lambda i, j: (0, 0)))
        args.append(dep)
    out = jax.ShapeDtypeStruct((t, f), BF16)
    return pl.pallas_call(
        body, name=name, grid=(t // tm, f // tf), in_specs=in_specs, out_specs=[blk, blk], out_shape=[out, out],
        compiler_params=_params("parallel", "parallel"),
    )(*args)


def _merge_fwd(ya, yb, proj, ga_blk, gb_blk, d, name, tt=512):
    t = ya.shape[0]
    tt = _tile(t, tt, 8)

    def body(ya_ref, yb_ref, ga_ref, gb_ref, o_ref):
        o_ref[...] = (_sigmoid(ga_ref[...].astype(F32)) * ya_ref[...].astype(F32)
                      + _sigmoid(gb_ref[...].astype(F32)) * yb_ref[...].astype(F32)).astype(BF16)

    row = pl.BlockSpec((tt, d), lambda i: (i, 0))
    return pl.pallas_call(
        body, name=name, grid=(t // tt,),
        in_specs=[row, row, pl.BlockSpec((tt, d), lambda i: (i, ga_blk)), pl.BlockSpec((tt, d), lambda i: (i, gb_blk))],
        out_specs=row, out_shape=jax.ShapeDtypeStruct((t, d), BF16), compiler_params=_params("parallel"),
    )(ya, yb, proj, proj)


def _merge_bwd(dm, ya, yb, proj, ga_blk, gb_blk, d, name, tt=512):
    t = ya.shape[0]
    tt = _tile(t, tt, 8)

    def body(dm_ref, ya_ref, yb_ref, ga_ref, gb_ref, dya_ref, dyb_ref, dg_ref):
        dmv = dm_ref[...].astype(F32)
        sa, sb = _sigmoid(ga_ref[...].astype(F32)), _sigmoid(gb_ref[...].astype(F32))
        dya_ref[...] = (dmv * sa).astype(BF16)
        dyb_ref[...] = (dmv * sb).astype(BF16)
        dg_ref[:, 0:d] = (dmv * ya_ref[...].astype(F32) * sa * (1.0 - sa)).astype(BF16)
        dg_ref[:, d:2 * d] = (dmv * yb_ref[...].astype(F32) * sb * (1.0 - sb)).astype(BF16)

    row = pl.BlockSpec((tt, d), lambda i: (i, 0))
    out = jax.ShapeDtypeStruct((t, d), BF16)
    return pl.pallas_call(
        body, name=name, grid=(t // tt,),
        in_specs=[row, row, row, pl.BlockSpec((tt, d), lambda i: (i, ga_blk)), pl.BlockSpec((tt, d), lambda i: (i, gb_blk))],
        out_specs=[row, row, pl.BlockSpec((tt, 2 * d), lambda i: (i, 0))],
        out_shape=[out, out, jax.ShapeDtypeStruct((t, 2 * d), BF16)], compiler_params=_params("parallel"),
    )(dm, ya, yb, proj, proj)


def _loss_head(h, g, target, name, tt=512):
    t, d = h.shape
    tt = _tile(t, tt, 8)

    def body(h_ref, g_ref, tg_ref, loss_ref, dh_ref, dg_ref):
        xv = h_ref[...]
        r = lax.rsqrt(jnp.mean(xv * xv, axis=-1, keepdims=True) + EPS)
        xh = xv * r
        err = xh * g_ref[...] - tg_ref[...]
        dout = err * (1.0 / d)
        gy = dout * g_ref[...]
        dh_ref[...] = r * (gy - xh * jnp.mean(gy * xh, axis=-1, keepdims=True))
        dg_part = jnp.sum(dout * xh, axis=0, keepdims=True)
        loss_part = jnp.full((1, LANES), 0.5 / d, F32) * jnp.sum(err * err)

        @pl.when(pl.program_id(0) == 0)
        def _():
            dg_ref[...] = dg_part
            loss_ref[...] = loss_part

        @pl.when(pl.program_id(0) > 0)
        def _():
            dg_ref[...] += dg_part
            loss_ref[...] += loss_part

    row = pl.BlockSpec((tt, d), lambda i: (i, 0))
    vec = pl.BlockSpec((1, d), lambda i: (0, 0))
    return pl.pallas_call(
        body, name=name, grid=(t // tt,), in_specs=[row, vec, row],
        out_specs=[pl.BlockSpec((1, LANES), lambda i: (0, 0)), row, vec],
        out_shape=[jax.ShapeDtypeStruct((1, LANES), F32), jax.ShapeDtypeStruct((t, d), F32), jax.ShapeDtypeStruct((1, d), F32)],
        compiler_params=_params("arbitrary"),
    )(h, g, target)


def _shift_down(x, k, t_idx):
    if k == 0:
        return x
    return jnp.where(t_idx >= k, pltpu.roll(x, k, 0), 0.0)


def _shift_up(x, k, t_idx, s):
    if k == 0:
        return x
    return jnp.where(t_idx < s - k, pltpu.roll(x, s - k, 0), 0.0)


CONV_A_BLOCK = 256


def _conv_a_fwd(proj, w, nb, s, d, name):
    cb = _tile(d, CONV_A_BLOCK, LANES)

    def body(x_ref, w_ref, o_ref):
        t_idx = lax.broadcasted_iota(jnp.int32, (s, cb), 0)
        cv = x_ref[:, cb:2 * cb].astype(F32) * x_ref[:, 2 * cb:3 * cb].astype(F32)
        cc = sum(w_ref[k:k + 1, :] * _shift_down(cv, CONV_A_K - 1 - k, t_idx) for k in range(CONV_A_K))
        o_ref[...] = (x_ref[:, 0:cb].astype(F32) * cc).astype(BF16)

    return pl.pallas_call(
        body, name=name, grid=(nb, d // cb),
        in_specs=[pl.BlockSpec((s, 3 * cb), lambda b, j: (b, j)), pl.BlockSpec((8, cb), lambda b, j: (0, j))],
        out_specs=pl.BlockSpec((s, cb), lambda b, j: (b, j)),
        out_shape=jax.ShapeDtypeStruct((nb * s, d), BF16), compiler_params=_params("parallel", "parallel"),
    )(proj, w)


def _conv_a_bwd(dy, proj, w, nb, s, d, out_cols, name):
    cb = _tile(d, CONV_A_BLOCK, LANES)

    def body(dy_ref, x_ref, w_ref, o_ref, dw_ref):
        t_idx = lax.broadcasted_iota(jnp.int32, (s, cb), 0)
        cv_c, cv_v = x_ref[:, cb:2 * cb].astype(F32), x_ref[:, 2 * cb:3 * cb].astype(F32)
        cv = cv_c * cv_v
        shifted = [_shift_down(cv, CONV_A_K - 1 - k, t_idx) for k in range(CONV_A_K)]
        cc = sum(w_ref[k:k + 1, :] * shifted[k] for k in range(CONV_A_K))
        dyv = dy_ref[...].astype(F32)
        o_ref[:, 0:cb] = (dyv * cc).astype(BF16)
        dcc = dyv * x_ref[:, 0:cb].astype(F32)
        dcv = sum(w_ref[k:k + 1, :] * _shift_up(dcc, CONV_A_K - 1 - k, t_idx, s) for k in range(CONV_A_K))
        o_ref[:, cb:2 * cb] = (dcv * cv_v).astype(BF16)
        o_ref[:, 2 * cb:3 * cb] = (dcv * cv_c).astype(BF16)
        rows = [jnp.sum(dcc * shifted[k], axis=0, keepdims=True) for k in range(CONV_A_K)]
        part = _stack_rows(rows, cb)

        @pl.when(pl.program_id(1) == 0)
        def _():
            dw_ref[...] = part

        @pl.when(pl.program_id(1) > 0)
        def _():
            dw_ref[...] += part

    wspec = pl.BlockSpec((8, cb), lambda j, b: (0, j))
    wide = pl.BlockSpec((s, 3 * cb), lambda j, b: (b, j))
    return pl.pallas_call(
        body, name=name, grid=(d // cb, nb), in_specs=[pl.BlockSpec((s, cb), lambda j, b: (b, j)), wide, wspec],
        out_specs=[wide, wspec],
        out_shape=[jax.ShapeDtypeStruct((nb * s, out_cols), BF16), jax.ShapeDtypeStruct((8, d), F32)],
        compiler_params=_params("parallel", "arbitrary"),
    )(dy, proj, w)


def _conv_s_fwd(proj, col0, w, bias, nb, s, cc_width, name, cb=256):
    cb = _tile(math.gcd(cc_width, col0) if col0 else cc_width, cb, LANES)
    nd, off = cc_width // cb, col0 // cb

    def body(x_ref, w_ref, b_ref, o_ref, pre_ref):
        t_idx = lax.broadcasted_iota(jnp.int32, (s, cb), 0)
        xv = x_ref[...].astype(F32)
        pre = b_ref[...] + sum(w_ref[k:k + 1, :] * _shift_down(xv, SSM_CONV_K - 1 - k, t_idx) for k in range(SSM_CONV_K))
        o_ref[...] = _silu(pre).astype(BF16)
        pre_ref[...] = pre.astype(BF16)

    vec = pl.BlockSpec((8, cb), lambda b, j: (0, j))
    own = pl.BlockSpec((s, cb), lambda b, j: (b, j))
    out = jax.ShapeDtypeStruct((nb * s, cc_width), BF16)
    return pl.pallas_call(
        body, name=name, grid=(nb, nd),
        in_specs=[pl.BlockSpec((s, cb), lambda b, j: (b, j + off)), vec, pl.BlockSpec((1, cb), lambda b, j: (0, j))],
        out_specs=[own, own], out_shape=[out, out], compiler_params=_params("parallel", "parallel"),
    )(proj, w, bias)


def _conv_s_bwd(dxc, pre, proj, col0, w, nb, s, cc_width, into, name, cb=256):
    cb = _tile(math.gcd(cc_width, col0) if col0 else cc_width, cb, LANES)
    nd, off = cc_width // cb, col0 // cb

    def body(d_ref, pre_ref, x_ref, w_ref, into_ref, dx_ref, dw_ref, db_ref):
        t_idx = lax.broadcasted_iota(jnp.int32, (s, cb), 0)
        xv = x_ref[...].astype(F32)
        dpre = d_ref[...].astype(F32) * _dsilu(pre_ref[...].astype(F32))
        ahead = [_shift_up(dpre, j, t_idx, s) for j in range(SSM_CONV_K)]
        dx_ref[...] = sum(w_ref[k:k + 1, :] * ahead[SSM_CONV_K - 1 - k] for k in range(SSM_CONV_K)).astype(BF16)
        rows = [jnp.sum(ahead[SSM_CONV_K - 1 - k] * xv, axis=0, keepdims=True) for k in range(SSM_CONV_K)]
        dw_part = _stack_rows(rows, cb)
        db_part = jnp.sum(dpre, axis=0, keepdims=True)

        @pl.when(pl.program_id(1) == 0)
        def _():
            dw_ref[...] = dw_part
            db_ref[...] = db_part

        @pl.when(pl.program_id(1) > 0)
        def _():
            dw_ref[...] += dw_part
            db_ref[...] += db_part

    own = pl.BlockSpec((s, cb), lambda j, b: (b, j))
    shifted = pl.BlockSpec((s, cb), lambda j, b: (b, j + off))
    wspec = pl.BlockSpec((8, cb), lambda j, b: (0, j))
    bspec = pl.BlockSpec((1, cb), lambda j, b: (0, j))
    return pl.pallas_call(
        body, name=name, grid=(nd, nb),
        in_specs=[own, own, shifted, wspec, pl.BlockSpec(memory_space=pl.ANY)],
        out_specs=[shifted, wspec, bspec],
        out_shape=[jax.ShapeDtypeStruct(into.shape, into.dtype), jax.ShapeDtypeStruct((8, cc_width), F32),
                   jax.ShapeDtypeStruct((1, cc_width), F32)],
        input_output_aliases={4: 0},
        compiler_params=_params("parallel", "arbitrary"),
    )(dxc, pre, proj, w, into)


def _split3(v):
    hi = v.astype(BF16)
    r1 = v - hi.astype(F32)
    mid = r1.astype(BF16)
    return hi, mid, (r1 - mid.astype(F32)).astype(BF16)


def _exact_left(mask_b, v):
    return sum(_dot(mask_b, t) for t in _split3(v))


def _exact_right(v, mask_b):
    return sum(_dot(t, mask_b) for t in _split3(v))


def _head_sums(v, e_b):
    return _dot(v.astype(BF16), e_b, NT)


def _spread(v, out_ref, di):
    lane = lax.broadcasted_iota(jnp.int32, (v.shape[0], LANES), 1)
    for pr in range(di // LANES):
        h0 = pr * (LANES // SSM_HEAD_DIM)
        out_ref[:, pr * LANES:(pr + 1) * LANES] = jnp.where(lane < SSM_HEAD_DIM, v[:, h0:h0 + 1], v[:, h0 + 1:h0 + 2])


def _ssd_common(xc_ref, dtr_ref, dtrt_ref, prow_ref, pcol_ref, dtx_ref, acsx_ref, dx_ref, di):
    l = SSM_CHUNK
    bias_r, a_r = prow_ref[0:1, :], -jnp.exp(prow_ref[1:2, :])
    sp_in = dtr_ref[...] + bias_r
    dt = _softplus(sp_in)
    li = lax.broadcasted_iota(jnp.int32, (l, l), 0)
    si = lax.broadcasted_iota(jnp.int32, (l, l), 1)
    lower_b = (li >= si).astype(BF16)
    upper_b = (li <= si).astype(BF16)
    acs = _exact_left(lower_b, dt * a_r)
    bias_c, a_c = pcol_ref[:, 0:1], -jnp.exp(pcol_ref[:, 1:2])
    dt_t = _softplus(dtrt_ref[...] + bias_c)
    acs_t = _exact_right(dt_t * a_c, upper_b)
    _spread(dt, dtx_ref, di)
    _spread(acs, acsx_ref, di)
    _spread(prow_ref[0:8, :], dx_ref, di)
    acs_exp = acsx_ref[...]
    acs_last = acs_exp[l - 1:l, :]
    x = xc_ref[:, 0:di].astype(F32)
    return dict(dt=dt, a_r=a_r, sp_in=sp_in, acs=acs, acs_t=acs_t, dt_exp=dtx_ref[...], e_exp=jnp.exp(acs_exp),
                el_exp=jnp.exp(acs_last), f_exp=jnp.exp(acs_last - acs_exp), x=x, mask=li >= si, upper_b=upper_b,
                d_exp=dx_ref[2:3, :])


def _decay(q, h):
    seg = q["acs"][:, h:h + 1] - q["acs_t"][h:h + 1, :]
    return jnp.exp(jnp.where(q["mask"], seg, NEG_BIG))


def _ssd_fwd(xc, dtr, dtrt, prow, pcol, nb, nc, di, name):
    l, n, g_n, p = SSM_CHUNK, SSM_STATE, SSM_GROUPS, SSM_HEAD_DIM
    cc = xc.shape[1]
    gw = di // g_n
    assert p * 2 == LANES and gw % LANES == 0

    def body(xc_ref, dtr_ref, dtrt_ref, prow_ref, pcol_ref, y_ref, sprev_ref, st_ref, dtx_ref, acsx_ref, dx_ref):
        @pl.when(pl.program_id(1) == 0)
        def _():
            st_ref[...] = jnp.zeros_like(st_ref)

        q = _ssd_common(xc_ref, dtr_ref, dtrt_ref, prow_ref, pcol_ref, dtx_ref, acsx_ref, dx_ref, di)
        x = q["x"]
        xd = x * q["dt_exp"]
        xdb = xd.astype(BF16)
        xdf = (xd * q["f_exp"]).astype(BF16)
        lane = lax.broadcasted_iota(jnp.int32, (l, LANES), 1)
        for g in range(g_n):
            lo = g * gw
            bg = xc_ref[:, di + g * n: di + (g + 1) * n]
            cg = xc_ref[:, di + g_n * n + g * n: di + g_n * n + (g + 1) * n]
            cb = _dot(cg, bg, NT)
            st_g = st_ref[:, lo:lo + gw]
            y_off = q["e_exp"][:, lo:lo + gw] * _dot(cg, st_g.astype(BF16))
            for pr in range(gw // LANES):
                c0 = lo + pr * LANES
                h0 = c0 // p
                xp = xdb[:, c0:c0 + LANES]
                m0 = (cb * _decay(q, h0)).astype(BF16)
                m1 = (cb * _decay(q, h0 + 1)).astype(BF16)
                yd = _dot(m0, jnp.where(lane < p, xp, 0)) + _dot(m1, jnp.where(lane >= p, xp, 0))
                y_ref[:, c0:c0 + LANES] = (yd + y_off[:, pr * LANES:(pr + 1) * LANES]
                                           + q["d_exp"][:, c0:c0 + LANES] * x[:, c0:c0 + LANES]).astype(BF16)
            sprev_ref[:, lo:lo + gw] = st_g
            st_ref[:, lo:lo + gw] = q["el_exp"][:, lo:lo + gw] * st_g + _dot(bg, xdf[:, lo:lo + gw], TN)

    tok = lambda w: pl.BlockSpec((l, w), lambda b, c: (b * nc + c, 0))
    const = lambda r, w: pl.BlockSpec((r, w), lambda b, c: (0, 0))
    return pl.pallas_call(
        body, name=name, grid=(nb, nc),
        in_specs=[tok(cc), tok(LANES), pl.BlockSpec((LANES, l), lambda b, c: (0, b * nc + c)),
                  const(8, LANES), const(LANES, 8)],
        out_specs=[tok(di), pl.BlockSpec((None, n, di), lambda b, c: (b * nc + c, 0, 0))],
        out_shape=[jax.ShapeDtypeStruct((nb * nc * l, di), BF16), jax.ShapeDtypeStruct((nb * nc, n, di), F32)],
        scratch_shapes=[pltpu.VMEM((n, di), F32), pltpu.VMEM((l, di), F32), pltpu.VMEM((l, di), F32),
                        pltpu.VMEM((8, di), F32)],
        compiler_params=_params("parallel", "arbitrary"),
    )(xc, dtr, dtrt, prow, pcol)


def _ssd_bwd(dy, xc, dtr, dtrt, prow, pcol, e_mat, sprev, nb, nc, di, name):
    l, n, g_n, p = SSM_CHUNK, SSM_STATE, SSM_GROUPS, SSM_HEAD_DIM
    cc = xc.shape[1]
    gw = di // g_n

    def body(dy_ref, xc_ref, dtr_ref, dtrt_ref, prow_ref, pcol_ref, e_ref, sprev_ref,
             dxc_ref, ddtr_ref, sums_ref, dst_ref, off_ref, dxd_ref, last_ref, vst_ref,
             dtx_ref, acsx_ref, dx_ref):
        first = jnp.logical_and(pl.program_id(0) == 0, pl.program_id(1) == 0)

        @pl.when(pl.program_id(1) == 0)
        def _():
            dst_ref[...] = jnp.zeros_like(dst_ref)

        head_row = lax.broadcasted_iota(jnp.int32, (LANES, l), 0)
        row_sums, col_sums = jnp.zeros((l, LANES), F32), jnp.zeros((LANES, l), F32)
        strict_lower = lax.broadcasted_iota(jnp.int32, (l, l), 0) > lax.broadcasted_iota(jnp.int32, (l, l), 1)

        q = _ssd_common(xc_ref, dtr_ref, dtrt_ref, prow_ref, pcol_ref, dtx_ref, acsx_ref, dx_ref, di)
        x = q["x"]
        xd = x * q["dt_exp"]
        xdb = xd.astype(BF16)
        xdf = (xd * q["f_exp"]).astype(BF16)
        dyv = dy_ref[...].astype(F32)
        dyb = dy_ref[...]
        dye = (dyv * q["e_exp"]).astype(BF16)
        upper_b = q["upper_b"]
        lane = lax.broadcasted_iota(jnp.int32, (l, LANES), 1)
        for g in range(g_n):
            lo = g * gw
            bg = xc_ref[:, di + g * n: di + (g + 1) * n]
            cg = xc_ref[:, di + g_n * n + g * n: di + g_n * n + (g + 1) * n]
            cb = _dot(cg, bg, NT)
            st_g = sprev_ref[:, lo:lo + gw]
            st_gb = st_g.astype(BF16)
            dst_g = dst_ref[:, lo:lo + gw]
            dst_gb = dst_g.astype(BF16)
            dye_g = dye[:, lo:lo + gw]
            xdf_g = xdf[:, lo:lo + gw]
            y_off = q["e_exp"][:, lo:lo + gw] * _dot(cg, st_gb)
            dc_g = _dot(dye_g, st_gb, NT)
            db_g = _dot(xdf_g, dst_gb, NT)
            dxd_state = _dot(bg, dst_gb) * q["f_exp"][:, lo:lo + gw]
            last_ref[:, lo:lo + gw] = jnp.sum(dst_g * st_g, axis=0, keepdims=True)
            dst_ref[:, lo:lo + gw] = q["el_exp"][:, lo:lo + gw] * dst_g + _dot(cg, dye_g, TN)
            off_ref[:, lo:lo + gw] = dyv[:, lo:lo + gw] * y_off
            vst_ref[:, lo:lo + gw] = xd[:, lo:lo + gw] * dxd_state
            dcb = jnp.zeros((l, l), F32)
            for pr in range(gw // LANES):
                c0 = lo + pr * LANES
                h0 = c0 // p
                xp = xdb[:, c0:c0 + LANES]
                dyp = dyb[:, c0:c0 + LANES]
                dxd_diag = jnp.zeros((l, LANES), F32)
                for k, keep in enumerate((lane < p, lane >= p)):
                    dec = _decay(q, h0 + k)
                    dy_h = jnp.where(keep, dyp, 0)
                    dm_dec = _dot(dy_h, xp, NT) * dec
                    dcb = dcb + dm_dec
                    dxd_diag = dxd_diag + _dot((cb * dec).astype(BF16), dy_h, TN)
                    qm = dm_dec * cb
                    row_sums = jnp.where(lane == h0 + k, jnp.sum(qm, axis=1, keepdims=True), row_sums)
                    col_sums = jnp.where(head_row == h0 + k, jnp.sum(qm, axis=0, keepdims=True), col_sums)
                dxd_ref[:, c0:c0 + LANES] = dxd_diag + dxd_state[:, pr * LANES:(pr + 1) * LANES]
            dcb_b = dcb.astype(BF16)
            dxc_ref[:, di + g * n: di + (g + 1) * n] = (db_g + _dot(dcb_b, cg, TN)).astype(BF16)
            dxc_ref[:, di + g_n * n + g * n: di + g_n * n + (g + 1) * n] = (dc_g + _dot(dcb_b, bg)).astype(BF16)
        dxd = dxd_ref[...]
        e_b = e_ref[...]
        from_y = _exact_left(upper_b, _head_sums(off_ref[...], e_b) + row_sums - col_sums.T)
        from_s = _exact_left(strict_lower.astype(BF16), _head_sums(vst_ref[...], e_b))
        carried = _head_sums(jnp.broadcast_to(last_ref[...], (8, di)), e_b)[0:1, :] * jnp.exp(q["acs"][l - 1:l, :])
        dla = from_y + from_s + carried
        ddt = dla * q["a_r"] + _head_sums(dxd * x, e_b)
        ddtr = ddt * jax.nn.sigmoid(q["sp_in"])
        ddtr_ref[...] = ddtr
        dxc_ref[:, 0:di] = (dxd * q["dt_exp"] + q["d_exp"] * dyv).astype(BF16)
        dd_exp = jnp.sum(dyv * x, axis=0, keepdims=True)
        dd = _head_sums(jnp.broadcast_to(dd_exp, (8, di)), e_b)[0:1, :]
        part = _stack_rows([jnp.sum(ddtr, axis=0, keepdims=True),
                            jnp.sum(dla * q["dt"], axis=0, keepdims=True) * q["a_r"], dd], LANES)

        @pl.when(first)
        def _():
            sums_ref[...] = part

        @pl.when(jnp.logical_not(first))
        def _():
            sums_ref[...] += part

    rev = lambda b, c: b * nc + (nc - 1 - c)
    tok = lambda w: pl.BlockSpec((l, w), lambda b, c: (rev(b, c), 0))
    const = lambda r, w: pl.BlockSpec((r, w), lambda b, c: (0, 0))
    return pl.pallas_call(
        body, name=name, grid=(nb, nc),
        in_specs=[tok(di), tok(cc), tok(LANES), pl.BlockSpec((LANES, l), lambda b, c: (0, rev(b, c))),
                  const(8, LANES), const(LANES, 8), const(LANES, di),
                  pl.BlockSpec((None, n, di), lambda b, c: (rev(b, c), 0, 0))],
        out_specs=[tok(cc), tok(LANES), const(8, LANES)],
        out_shape=[jax.ShapeDtypeStruct((nb * nc * l, cc), BF16), jax.ShapeDtypeStruct((nb * nc * l, LANES), F32),
                   jax.ShapeDtypeStruct((8, LANES), F32)],
        scratch_shapes=[pltpu.VMEM((n, di), F32), pltpu.VMEM((l, di), F32), pltpu.VMEM((l, di), F32),
                        pltpu.VMEM((1, di), F32), pltpu.VMEM((l, di), F32),
                        pltpu.VMEM((l, di), F32), pltpu.VMEM((l, di), F32), pltpu.VMEM((8, di), F32)],
        compiler_params=_params("arbitrary", "arbitrary"),
    )(dy, xc, dtr, dtrt, prow, pcol, e_mat, sprev)


def _gate_norm_fwd(y, proj, z_col0, norm_g, di, name, tt=256):
    t = y.shape[0]
    tt = _tile(t, tt, 8)
    gw = di // SSM_GROUPS
    zw = _tile(math.gcd(di, z_col0), di, LANES)
    nz, zoff = di // zw, z_col0 // zw

    def body(*refs):
        y_ref, z_refs, g_ref, o_ref = refs[0], refs[1:1 + nz], refs[1 + nz], refs[2 + nz]
        for g in range(SSM_GROUPS):
            lo = g * gw
            zv = z_refs[lo // zw][:, lo % zw:lo % zw + gw].astype(F32)
            yg = y_ref[:, lo:lo + gw].astype(F32) * _silu(zv)
            r = lax.rsqrt(jnp.mean(yg * yg, axis=-1, keepdims=True) + EPS)
            o_ref[:, lo:lo + gw] = (yg * r * g_ref[:, lo:lo + gw]).astype(BF16)

    row = pl.BlockSpec((tt, di), lambda i: (i, 0))
    zspecs = [pl.BlockSpec((tt, zw), functools.partial(lambda i, k: (i, zoff + k), k=k)) for k in range(nz)]
    return pl.pallas_call(
        body, name=name, grid=(t // tt,), in_specs=[row] + zspecs + [pl.BlockSpec((1, di), lambda i: (0, 0))],
        out_specs=row, out_shape=jax.ShapeDtypeStruct((t, di), BF16), compiler_params=_params("parallel"),
    )(y, *([proj] * nz), norm_g)


def _gate_norm_bwd(dn, y, proj, z_col0, norm_g, di, name, tt=256):
    t = y.shape[0]
    tt = _tile(t, tt, 8)
    gw = di // SSM_GROUPS
    zw = _tile(math.gcd(di, z_col0), di, LANES)
    nz, zoff = di // zw, z_col0 // zw

    def body(*refs):
        dn_ref, y_ref, z_refs, g_ref = refs[0], refs[1], refs[2:2 + nz], refs[2 + nz]
        dy_ref, dz_ref, dg_ref = refs[3 + nz:]
        first = pl.program_id(0) == 0
        for g in range(SSM_GROUPS):
            lo = g * gw
            zv = z_refs[lo // zw][:, lo % zw:lo % zw + gw].astype(F32)
            yv = y_ref[:, lo:lo + gw].astype(F32)
            sz = _silu(zv)
            yg = yv * sz
            r = lax.rsqrt(jnp.mean(yg * yg, axis=-1, keepdims=True) + EPS)
            yh = yg * r
            dnv = dn_ref[:, lo:lo + gw].astype(F32)
            gy = dnv * g_ref[:, lo:lo + gw]
            dyg = r * (gy - yh * jnp.mean(gy * yh, axis=-1, keepdims=True))
            dy_ref[:, lo:lo + gw] = (dyg * sz).astype(BF16)
            dz_ref[:, lo:lo + gw] = (dyg * yv * _dsilu(zv)).astype(BF16)
            part = jnp.sum(dnv * yh, axis=0, keepdims=True)

            @pl.when(first)
            def _():
                dg_ref[:, lo:lo + gw] = part

            @pl.when(jnp.logical_not(first))
            def _():
                dg_ref[:, lo:lo + gw] += part

    row = pl.BlockSpec((tt, di), lambda i: (i, 0))
    vec = pl.BlockSpec((1, di), lambda i: (0, 0))
    zspecs = [pl.BlockSpec((tt, zw), functools.partial(lambda i, k: (i, zoff + k), k=k)) for k in range(nz)]
    return pl.pallas_call(
        body, name=name, grid=(t // tt,), in_specs=[row, row] + zspecs + [vec], out_specs=[row, row, vec],
        out_shape=[jax.ShapeDtypeStruct((t, di), BF16), jax.ShapeDtypeStruct((t, di), BF16), jax.ShapeDtypeStruct((1, di), F32)],
        compiler_params=_params("arbitrary"),
    )(dn, y, *([proj] * nz), norm_g)


def _softmax_rows(s):
    s = s - jnp.max(s, axis=-1, keepdims=True)
    e = jnp.exp(s)
    return e * (1.0 / jnp.sum(e, axis=-1, keepdims=True))


def _xattn_fwd(q, kv, nb, s, m, d, name, tq=1024):
    tq = _tile(s, tq, 8)
    nq = s // tq
    hd = d // XATTN_HEADS
    scale = 1.0 / math.sqrt(hd)

    def body(q_ref, k_ref, v_ref, o_ref):
        for h in range(XATTN_HEADS):
            sl = slice(h * hd, (h + 1) * hd)
            prob = _softmax_rows(_dot(q_ref[:, sl], k_ref[:, sl], NT) * scale)
            o_ref[:, sl] = _dot(prob.astype(BF16), v_ref[:, sl]).astype(BF16)

    return pl.pallas_call(
        body, name=name, grid=(nb, nq),
        in_specs=[pl.BlockSpec((tq, d), lambda b, i: (b * nq + i, 0)), pl.BlockSpec((m, d), lambda b, i: (b, 0)),
                  pl.BlockSpec((m, d), lambda b, i: (b, 1))],
        out_specs=pl.BlockSpec((tq, d), lambda b, i: (b * nq + i, 0)),
        out_shape=jax.ShapeDtypeStruct((nb * s, d), BF16), compiler_params=_params("parallel", "parallel"),
    )(q, kv, kv)


def _xattn_bwd(do, q, kv, nb, s, m, d, name, tq=1024):
    tq = _tile(s, tq, 8)
    nq = s // tq
    hd = d // XATTN_HEADS
    scale = 1.0 / math.sqrt(hd)

    def body(do_ref, q_ref, k_ref, v_ref, dq_ref, dk_ref, dv_ref):
        first = pl.program_id(1) == 0
        for h in range(XATTN_HEADS):
            sl = slice(h * hd, (h + 1) * hd)
            qh, kh, vh, doh = q_ref[:, sl], k_ref[:, sl], v_ref[:, sl], do_ref[:, sl]
            prob = _softmax_rows(_dot(qh, kh, NT) * scale)
            dv_h = _dot(prob.astype(BF16), doh, TN)
            dp = _dot(doh, vh, NT)
            ds = (prob * (dp - jnp.sum(dp * prob, axis=-1, keepdims=True)) * scale).astype(BF16)
            dq_ref[:, sl] = _dot(ds, kh).astype(BF16)
            dk_h = _dot(ds, qh, TN)

            @pl.when(first)
            def _():
                dk_ref[:, sl] = dk_h
                dv_ref[:, sl] = dv_h

            @pl.when(jnp.logical_not(first))
            def _():
                dk_ref[:, sl] += dk_h
                dv_ref[:, sl] += dv_h

    qspec = pl.BlockSpec((tq, d), lambda b, i: (b * nq + i, 0))
    dq, dk, dv = pl.pallas_call(
        body, name=name, grid=(nb, nq),
        in_specs=[qspec, qspec, pl.BlockSpec((m, d), lambda b, i: (b, 0)), pl.BlockSpec((m, d), lambda b, i: (b, 1))],
        out_specs=[qspec, pl.BlockSpec((m, d), lambda b, i: (b, 0)), pl.BlockSpec((m, d), lambda b, i: (b, 0))],
        out_shape=[jax.ShapeDtypeStruct((nb * s, d), BF16), jax.ShapeDtypeStruct((nb * m, d), F32),
                   jax.ShapeDtypeStruct((nb * m, d), F32)],
        compiler_params=_params("parallel", "arbitrary"),
    )(do, q, kv, kv)
    return dq, dk, dv


def _all_gather(shards, name):
    n_arr = len(shards)

    def body(*refs):
        x_refs, out_refs = refs[:n_arr], refs[n_arr:2 * n_arr]
        send_sems, recv_sems, local_sems = refs[2 * n_arr:]
        x, y, c = lax.axis_index("x"), lax.axis_index("y"), lax.axis_index("c")
        me, sibling = (x, y, c), (x, y, 1 - c)
        chips = [(1 - x, y), (x, 1 - y), (1 - x, 1 - y)]

        def copy(w, k, block, to, from_input=False):
            px, py, pc = block
            rows = out_refs[w].at[4 * px + 2 * py + pc]
            return pltpu.make_async_remote_copy(
                src_ref=x_refs[w] if from_input else rows, dst_ref=rows,
                send_sem=send_sems.at[7 * w + k], recv_sem=recv_sems.at[7 * w + k], device_id=to, device_id_type=MESH)

        started = []
        for w in range(n_arr):
            mine = pltpu.make_async_copy(x_refs[w], out_refs[w].at[4 * x + 2 * y + c], local_sems.at[w])
            mine.start()
            started.append(mine)
        sends = []
        for w in range(n_arr):
            sends.append(copy(w, 0, me, sibling, from_input=True))
            sends += [copy(w, 1 + j, me, (*chip, c), from_input=True) for j, chip in enumerate(chips)]
        for cp in sends:
            cp.start()
        for j, chip in enumerate(chips):
            for w in range(n_arr):
                copy(w, 1 + j, (*chip, c), me).wait_recv()
                passed = copy(w, 4 + j, (*chip, c), sibling)
                passed.start()
                sends.append(passed)
        for w in range(n_arr):
            copy(w, 0, sibling, me).wait_recv()
            for j, chip in enumerate(chips):
                copy(w, 4 + j, (*chip, 1 - c), me).wait_recv()
        for cp in sends:
            cp.wait_send()
        for mine in started:
            mine.wait()

    hbm = pl.BlockSpec(memory_space=pl.ANY)
    return pl.pallas_call(
        body, name=name, out_shape=[jax.ShapeDtypeStruct((N_DEV,) + s.shape, s.dtype) for s in shards],
        in_specs=[hbm] * n_arr, out_specs=[hbm] * n_arr,
        scratch_shapes=[pltpu.SemaphoreType.DMA((7 * n_arr,)), pltpu.SemaphoreType.DMA((7 * n_arr,)),
                        pltpu.SemaphoreType.DMA((n_arr,))],
    )(*shards)


_HBM = pl.BlockSpec(memory_space=pltpu.HBM)
_SEM = pl.BlockSpec(memory_space=pltpu.SEMAPHORE)
_DATAFLOW = pltpu.SideEffectType.DATAFLOW_SIDE_EFFECTING


def _peer_list(x, y, c):
    return [(1 - x if k & 4 else x, 1 - y if k & 2 else y, 1 - c if k & 1 else c) for k in range(1, N_DEV)]


def _push_copy(src_ref, land_ref, send_sems, recv_sems, w, k, peer, me, per_peer_src, receiving):
    px, py, pc = peer
    peer_slot = 4 * px + 2 * py + pc
    return pltpu.make_async_remote_copy(
        src_ref=src_ref.at[peer_slot] if per_peer_src else src_ref,
        dst_ref=land_ref.at[peer_slot if receiving else me],
        send_sem=send_sems.at[7 * w + k], recv_sem=recv_sems.at[7 * w + k], device_id=peer, device_id_type=MESH)


def _push_start(srcs, per_peer_src, after, name):
    n_arr = len(srcs)
    land_shapes = [s.shape if per_peer_src else (N_DEV,) + s.shape for s in srcs]

    def body(*refs):
        src_refs, land_refs = refs[:n_arr], refs[n_arr:2 * n_arr]
        send_sems, recv_sems = refs[2 * n_arr + 1], refs[2 * n_arr + 2]
        token = refs[-1]
        x, y, c = lax.axis_index("x"), lax.axis_index("y"), lax.axis_index("c")
        me = 4 * x + 2 * y + c
        for w in range(n_arr):
            for k, peer in enumerate(_peer_list(x, y, c)):
                _push_copy(src_refs[w], land_refs[w], send_sems, recv_sems, w, k, peer, me, per_peer_src, False).start()
        token[...] = jnp.zeros_like(token)

    lands = [pltpu.with_memory_space_constraint(lax.empty(ls, s.dtype), pltpu.HBM) for ls, s in zip(land_shapes, srcs)]
    srcs_hbm = [pltpu.with_memory_space_constraint(s, pltpu.HBM) for s in srcs]
    out = pl.pallas_call(
        body, name=name,
        out_shape=(pltpu.SemaphoreType.DMA((7 * n_arr,)), pltpu.SemaphoreType.DMA((7 * n_arr,)),
                   *[pltpu.HBM(s.shape, s.dtype) for s in srcs], *[pltpu.HBM(ls, s.dtype) for ls, s in zip(land_shapes, srcs)],
                   jax.ShapeDtypeStruct((8, LANES), F32)),
        in_specs=[_HBM] * (2 * n_arr) + [pl.BlockSpec(memory_space=pl.ANY)],
        out_specs=(_SEM, _SEM, *([_HBM] * (2 * n_arr)), pl.BlockSpec(memory_space=pltpu.VMEM)),
        input_output_aliases={i: 2 + i for i in range(2 * n_arr)},
        compiler_params=pltpu.CompilerParams(has_side_effects=_DATAFLOW),
    )(*srcs_hbm, *lands, after)
    return dict(send=out[0], recv=out[1], srcs=list(out[2:2 + n_arr]), lands=list(out[2 + n_arr:2 + 2 * n_arr]),
                token=out[-1])


def _push_wait(pending, per_peer_src, after, name):
    n_arr = len(pending["srcs"])

    def body(*refs):
        src_refs, land_refs = refs[:n_arr], refs[n_arr:2 * n_arr]
        send_sems, recv_sems = refs[2 * n_arr], refs[2 * n_arr + 1]
        x, y, c = lax.axis_index("x"), lax.axis_index("y"), lax.axis_index("c")
        me = 4 * x + 2 * y + c
        for w in range(n_arr):
            for k, peer in enumerate(_peer_list(x, y, c)):
                cp = _push_copy(src_refs[w], land_refs[w], send_sems, recv_sems, w, k, peer, me, per_peer_src, True)
                cp.wait_send()
                cp.wait_recv()

    out = pl.pallas_call(
        body, name=name,
        out_shape=tuple(pltpu.HBM(a.shape, a.dtype) for a in pending["srcs"] + pending["lands"]),
        in_specs=[_HBM] * (2 * n_arr) + [_SEM, _SEM, pl.BlockSpec(memory_space=pl.ANY)],
        out_specs=tuple([_HBM] * (2 * n_arr)),
        input_output_aliases={i: i for i in range(2 * n_arr)},
        compiler_params=pltpu.CompilerParams(has_side_effects=_DATAFLOW),
    )(*pending["srcs"], *pending["lands"], pending["send"], pending["recv"], after)
    return list(out[:n_arr]), list(out[n_arr:])


def _adamw_math(w, g, m, v):
    m = ADAM_B1 * m + (1.0 - ADAM_B1) * g
    v = ADAM_B2 * v + (1.0 - ADAM_B2) * (g * g)
    m_hat = m / (1.0 - ADAM_B1 ** ADAM_STEP)
    v_hat = v / (1.0 - ADAM_B2 ** ADAM_STEP)
    delta = -ADAM_LR * (m_hat / (jnp.sqrt(v_hat) + ADAM_EPS) + ADAM_WD * w)
    return delta, m, v


def _sum8(parts, name, tr=512):
    _, r, c_dim = parts.shape
    tr = _tile(r, tr, BF16_SUBLANES)

    def body(p_ref, o_ref):
        acc = p_ref[0].astype(F32)
        for k in range(1, N_DEV):
            acc = acc + p_ref[k].astype(F32)
        o_ref[...] = acc

    return pl.pallas_call(
        body, name=name, grid=(r // tr,), in_specs=[pl.BlockSpec((N_DEV, tr, c_dim), lambda i: (0, i, 0))],
        out_specs=pl.BlockSpec((tr, c_dim), lambda i: (i, 0)),
        out_shape=jax.ShapeDtypeStruct((r, c_dim), F32), compiler_params=_params("parallel"),
    )(parts)


def _sum8_adamw(parts, w, m, v, name, tr=128):
    _, r, c_dim = parts.shape
    tr = _tile(r, tr, BF16_SUBLANES)
    tc = c_dim if tr <= 2 * LANES else _tile(c_dim, LANES, LANES)

    def body(p_ref, w_ref, m_ref, v_ref, g_ref, d_ref, nm_ref, nv_ref):
        g = p_ref[0].astype(F32)
        for k in range(1, N_DEV):
            g = g + p_ref[k].astype(F32)
        g_ref[...] = g
        d_ref[...], nm_ref[...], nv_ref[...] = _adamw_math(w_ref[...], g, m_ref[...], v_ref[...])

    blk = pl.BlockSpec((None, tr, tc), lambda i, j: (0, i, j))
    out = jax.ShapeDtypeStruct((1, r, c_dim), F32)
    return pl.pallas_call(
        body, name=name, grid=(r // tr, c_dim // tc),
        in_specs=[pl.BlockSpec((N_DEV, tr, tc), lambda i, j: (0, i, j)), blk, blk, blk],
        out_specs=[blk] * 4, out_shape=[out] * 4, compiler_params=_params("parallel", "parallel"),
    )(parts, w, m, v)


def _adamw(g, w, m, v, name):
    r, c_dim = g.shape

    def body(g_ref, w_ref, m_ref, v_ref, d_ref, nm_ref, nv_ref):
        d_ref[...], nm_ref[...], nv_ref[...] = _adamw_math(w_ref[...], g_ref[...], m_ref[...], v_ref[...])

    out = jax.ShapeDtypeStruct((r, c_dim), F32)
    return pl.pallas_call(body, name=name, out_shape=[out] * 3)(g, w, m, v)


def _pack_rows(arrays, dtype, row_unit):
    chunks, offs, r0 = [], [], 0
    for a in arrays:
        flat = a.reshape(-1).astype(dtype)
        rows = -(-flat.shape[0] // (LANES * row_unit)) * row_unit
        flat = jnp.pad(flat, (0, rows * LANES - flat.shape[0]))
        chunks.append(flat.reshape(rows, LANES))
        offs.append((r0, rows))
        r0 += rows
    return jnp.concatenate(chunks, axis=0), offs


def _unpack_rows(packed, offs, shapes):
    out = []
    for (r0, rows), shape in zip(offs, shapes):
        n = math.prod(shape)
        blk = packed[..., r0:r0 + rows, :]
        blk = blk.reshape(packed.shape[:-2] + (rows * LANES,))[..., :n]
        out.append(blk.reshape(packed.shape[:-2] + tuple(shape)))
    return out


def _full_from_slots(blk, col_sharded):
    _, r, c = blk.shape
    if col_sharded:
        return blk.transpose(1, 0, 2).reshape(r, N_DEV * c)
    return blk.reshape(N_DEV * r, c)


def _ffn_fwd(h, n, w_gu_t, w_d, tag, next_gain=None):
    gate, up, a = _ffn_up(n, w_gu_t, f"{tag}_up")
    out = _mm(a, w_d, F32, f"{tag}_down", res=h, alpha=FFN_RES_WEIGHT, norm_out=next_gain)
    h_out, n_next = out if next_gain is not None else (out, None)
    return h_out, (h, n, gate, up, a), n_next


def _ffn_bwd(dh_out, saved, g, w_gu_t, w_d, tag, dep, send_grads):
    h, n, gate, up, a = saved
    dw_d = _mm_tn(a, dh_out, f"{tag}_dw_down", alpha=FFN_RES_WEIGHT, dep=dep)
    dgate, dup = _ffn_da(dh_out, w_d, gate, up, f"{tag}_da", FFN_RES_WEIGHT)
    f = dgate.shape[1]
    dw_gu_t = _mm_tn(dgate, n, f"{tag}_dw_gate", out_rows=2 * f)
    dw_gu_t = _mm_tn(dup, n, f"{tag}_dw_up", out_rows=2 * f, row_off=f, into=dw_gu_t)
    dep = send_grads(dw_gu_t, dw_d)
    return _mm([dgate, dup], w_gu_t, F32, f"{tag}_dn", dep=dep, tk=1408, res=dh_out, norm_bwd=(h, g))


W_GROUPS = (("ffn1_w_gate_up", "ffn1_w_down"),
            ("w_in",),
            ("w_out_a", "w_out_ssm", "w_mix_out"),
            ("w_q", "w_kv", "w_o_x", "ffn2_w_gate_up", "ffn2_w_down"))
G_GROUPS = (("ffn2_w_gate_up", "ffn2_w_down"),
            ("w_o_x", "w_q", "w_kv", "w_mix_out", "w_out_a", "w_out_ssm", "w_in"),
            ("ffn1_w_gate_up", "ffn1_w_down"))


def _local_step(x3, mem3, target3, small, comm):
    nb, s, d = x3.shape
    m_len = mem3.shape[1]
    t = nb * s
    nc = s // SSM_CHUNK
    di = small["ssm_norm"].shape[1]
    hs = di // SSM_HEAD_DIM
    cc = di + 2 * SSM_GROUPS * SSM_STATE
    x, mem, target = x3.reshape(t, d), mem3.reshape(nb * m_len, d), target3.reshape(t, d)

    sizes = (d, d, d, di, cc, hs, d, d)
    offs = [0]
    for sz in sizes:
        offs.append(offs[-1] + sz)
    cb_a = _tile(d, CONV_A_BLOCK, LANES)
    xbc_col0, z_col0 = 3 * d, 3 * d + cc
    ga_blk, gb_blk = (3 * d + di + cc) // d, (4 * d + di + cc) // d

    pad_vec = lambda v: jnp.pad(v.reshape(1, -1), ((0, 0), (0, LANES - hs)))
    prow = jnp.concatenate([pad_vec(small["ssm_dt_bias"]), pad_vec(small["ssm_a_log"]), pad_vec(small["ssm_d"]),
                            jnp.zeros((5, LANES), F32)], axis=0)
    pcol = prow.T
    e_mat = (lax.broadcasted_iota(jnp.int32, (LANES, di), 0)
             == lax.broadcasted_iota(jnp.int32, (LANES, di), 1) // SSM_HEAD_DIM).astype(BF16)
    conv_a_w8 = jnp.pad(small["conv_a_w"][0], ((0, 8 - CONV_A_K), (0, 0)))
    ssm_conv_w8 = jnp.pad(small["ssm_conv_w"][0], ((0, 8 - SSM_CONV_K), (0, 0)))

    wts, dep = comm.weights(0, None)
    n1 = _rms_fwd(x, small["ffn1_norm"] + dep[0, 0], "ffn1_norm")
    h1, ffn1_saved, u = _ffn_fwd(x, n1, wts["ffn1_w_gate_up"], wts["ffn1_w_down"], "ffn1", small["mix_norm"])
    got, dep = comm.weights(1, h1)
    wts.update(got)
    w_in_t = wts["w_in"]
    conv_rows = [w_in_t[offs[i] + j * cb_a:offs[i] + (j + 1) * cb_a] for j in range(d // cb_a) for i in (0, 1, 2)]
    w_main_t = jnp.concatenate(conv_rows + [w_in_t[offs[i]:offs[i + 1]] for i in (4, 3, 6, 7)], axis=0)
    w_dt_t = jnp.pad(w_in_t[offs[5]:offs[6]], ((0, LANES - hs), (0, 0)))
    proj = _mm(u, w_main_t, BF16, "in_proj", nt=True, dep=dep)
    dtr = _mm(u, w_dt_t, F32, "in_proj_dt", nt=True)
    yap = _conv_a_fwd(proj, conv_a_w8, nb, s, d, "conv_a")
    got, dep = comm.weights(2, yap)
    wts.update(got)
    y_a = _mm(yap, wts["w_out_a"], BF16, "out_a", dep=dep)
    xc, conv_pre = _conv_s_fwd(proj, xbc_col0, ssm_conv_w8, small["ssm_conv_b"] + dep[0, 0], nb, s, cc, "conv_s")
    dtrt = dtr.T
    y_ssd, sprev = _ssd_fwd(xc, dtr, dtrt, prow, pcol, nb, nc, di, "ssd")
    ygn = _gate_norm_fwd(y_ssd, proj, z_col0, small["ssm_norm"], di, "gate_norm")
    y_b = _mm(ygn, wts["w_out_ssm"], BF16, "out_ssm")
    merged = _merge_fwd(y_a, y_b, proj, ga_blk, gb_blk, d, "merge")
    h2, un = _mm(merged, wts["w_mix_out"], F32, "mix_out", res=h1, norm_out=small["xattn_norm"])
    got, _ = comm.weights(3, h2)
    wts.update(got)
    mn = _rms_fwd(mem, small["mem_norm"], "mem_norm")
    q = _mm(un, wts["w_q"], BF16, "q_proj")
    kv = _mm(mn, wts["w_kv"], BF16, "kv_proj", nt=True)
    o = _xattn_fwd(q, kv, nb, s, m_len, d, "xattn")
    h3, n2 = _mm(o, wts["w_o_x"], F32, "o_proj", res=h2, norm_out=small["ffn2_norm"])
    h4, ffn2_saved, _ = _ffn_fwd(h3, n2, wts["ffn2_w_gate_up"], wts["ffn2_w_down"], "ffn2")
    loss_vec, dh4, dg_final = _loss_head(h4, small["final_norm"].reshape(1, d), target, "loss_head")

    grads = {"final_norm": dg_final.reshape(d)}
    big = {}
    dh3, grads["ffn2_norm"] = _ffn_bwd(
        dh4, ffn2_saved, small["ffn2_norm"], wts["ffn2_w_gate_up"], wts["ffn2_w_down"], "ffn2", None,
        lambda dw_gu_t, dw_d: comm.grads(0, {"ffn2_w_gate_up": dw_gu_t, "ffn2_w_down": dw_d}))
    big["w_o_x"] = _mm_tn(o, dh3, "dw_o")
    do = _mm(dh3, wts["w_o_x"], BF16, "d_o", nt=True)
    dq, dk, dv = _xattn_bwd(do, q, kv, nb, s, m_len, d, "xattn_bwd")
    big["w_q"] = _mm_tn(un, dq, "dw_q")
    big["w_kv"] = _mm_tn(dv, mn, "dw_v", out_rows=2 * d, row_off=d, into=_mm_tn(dk, mn, "dw_k", out_rows=2 * d))
    _, grads["mem_norm"] = _mm([dk, dv], wts["w_kv"], F32, "d_mn", norm_bwd=(mem, small["mem_norm"]))
    dh2, grads["xattn_norm"] = _mm(dq, wts["w_q"], F32, "d_un", nt=True, res=dh3, norm_bwd=(h2, small["xattn_norm"]))
    big["w_mix_out"] = _mm_tn(merged, dh2, "dw_mix")
    dmerged = _mm(dh2, wts["w_mix_out"], BF16, "d_merged", nt=True)
    dya, dyb, dg = _merge_bwd(dmerged, y_a, y_b, proj, ga_blk, gb_blk, d, "merge_bwd")
    big["w_out_a"] = _mm_tn(yap, dya, "dw_out_a")
    big["w_out_ssm"] = _mm_tn(ygn, dyb, "dw_out_ssm")
    dyap = _mm(dya, wts["w_out_a"], BF16, "d_yap", nt=True)
    dygn = _mm(dyb, wts["w_out_ssm"], BF16, "d_ygn", nt=True)
    dcx, dconv_a = _conv_a_bwd(dyap, proj, conv_a_w8, nb, s, d, 3 * d + cc, "conv_a_bwd")
    dy_ssd, dz, grads["ssm_norm"] = _gate_norm_bwd(dygn, y_ssd, proj, z_col0, small["ssm_norm"], di, "gate_norm_bwd")
    dxc, ddtr, ssd_sums = _ssd_bwd(dy_ssd, xc, dtr, dtrt, prow, pcol, e_mat, sprev, nb, nc, di, "ssd_bwd")
    dcx, dconv_s, grads["ssm_conv_b"] = _conv_s_bwd(dxc, conv_pre, proj, xbc_col0, ssm_conv_w8, nb, s, cc, dcx, "conv_s_bwd")
    dw_cx, dw_z, dw_g = _mm_tn(dcx, u, "dw_in_cx"), _mm_tn(dz, u, "dw_in_z"), _mm_tn(dg, u, "dw_in_g")
    dw_dt = _mm_tn(ddtr, u, "dw_in_dt")[:hs]
    du_main = _mm([dcx, dz, dg], w_main_t, F32, "d_u")
    dh1, grads["mix_norm"] = _mm(ddtr, w_dt_t, F32, "d_u_dt", res=dh2, norm_bwd=(h1, small["mix_norm"], du_main))
    bcv = [[dw_cx[(3 * j + i) * cb_a:(3 * j + i + 1) * cb_a] for j in range(d // cb_a)] for i in range(3)]
    big["w_in"] = jnp.concatenate(bcv[0] + bcv[1] + bcv[2] + [dw_z, dw_cx[3 * d:], dw_dt, dw_g], axis=0)
    dep = comm.grads(1, big)
    dx, grads["ffn1_norm"] = _ffn_bwd(
        dh1, ffn1_saved, small["ffn1_norm"], wts["ffn1_w_gate_up"], wts["ffn1_w_down"], "ffn1", dep,
        lambda dw_gu_t, dw_d: comm.grads(2, {"ffn1_w_gate_up": dw_gu_t, "ffn1_w_down": dw_d}))

    grads["conv_a_w"] = dconv_a[:CONV_A_K]
    grads["ssm_conv_w"] = dconv_s[:SSM_CONV_K]
    grads["ssm_dt_bias"] = ssd_sums[0:1, :hs]
    grads["ssm_a_log"] = ssd_sums[1:2, :hs]
    grads["ssm_d"] = ssd_sums[2:3, :hs]
    return loss_vec[0, 0], dx.reshape(nb, s, d), grads


def _step(inputs):
    w = {k: inputs[k] for k in WEIGHT_ORDER}
    mom = {k: inputs["m_" + k] for k in WEIGHT_ORDER}
    vel = {k: inputs["v_" + k] for k in WEIGHT_ORDER}
    me = 4 * lax.axis_index("x") + 2 * lax.axis_index("y") + lax.axis_index("c")

    send = {k: (w[k][0].T if k in COL_SHARDED else w[k][0]).astype(BF16) for k in BIG_WEIGHTS}

    def own_slot(land, mine):
        return lax.dynamic_update_slice(land, mine[None], (me, 0, 0))

    gathers, exchanges = {}, {}

    def weights(i, after):
        if i == 0:
            lands = _all_gather([send[k] for k in W_GROUPS[0]], "gather0")
        else:
            sent, lands = _push_wait(gathers[i], False, after, f"gather{i}_wait")
            lands = [own_slot(land, mine) for land, mine in zip(lands, sent)]
        full = {k: land.reshape(N_DEV * land.shape[1], land.shape[2]) for k, land in zip(W_GROUPS[i], lands)}
        dep = jnp.zeros((8, LANES), F32)
        if i + 1 < len(W_GROUPS):
            gathers[i + 1] = _push_start([send[k] for k in W_GROUPS[i + 1]], False, lands[0], f"gather{i + 1}_start")
            dep = gathers[i + 1]["token"]
        return full, dep

    def slot_shape(k):
        rows, cols = send[k].shape
        return (rows * cols // LANES, LANES) if rows % BF16_SUBLANES else (rows, cols)

    def send_grads(i, by_name):
        slots = [by_name[k].reshape((N_DEV,) + slot_shape(k)) for k in G_GROUPS[i]]
        exchanges[i] = _push_start(slots, True, slots[0], f"exchange{i}_start")
        return exchanges[i]["token"]

    comm = types.SimpleNamespace(weights=weights, grads=send_grads)
    small = {k: w[k] for k in SMALL_REPLICATED}
    conv_shapes = [w[k].shape[1:] for k in SMALL_SHARDED]
    packed_c, conv_offs = _pack_rows([w[k][0] for k in SMALL_SHARDED], F32, 8)
    conv_blocks = _unpack_rows(_all_gather([packed_c], "gather_conv_weights")[0], conv_offs, conv_shapes)
    for k, b in zip(SMALL_SHARDED, conv_blocks):
        small[k] = _full_from_slots(b, True)[None]

    loss_local, grad_x, grads = _local_step(inputs["x"], inputs["mem"], inputs["loss_target"], small, comm)
    loss = lax.psum(loss_local, AXES)

    out = {}
    for i, names in enumerate(G_GROUPS):
        sent, lands = _push_wait(exchanges[i], True, grad_x, f"exchange{i}_wait")
        for k, land, slots in zip(names, lands, sent):
            parts = own_slot(land, lax.dynamic_index_in_dim(slots, me, 0, keepdims=False))
            if k in COL_SHARDED and w[k].shape[2] % LANES:
                flip = lambda a: a.transpose(0, 2, 1).reshape((1,) + slot_shape(k))
                unflip = lambda a: a.reshape((1,) + send[k].shape).transpose(0, 2, 1)
                out[k] = tuple(unflip(o) for o in _sum8_adamw(parts, flip(w[k]), flip(mom[k]), flip(vel[k]),
                                                             f"sum_adamw_{k}", tr=2048 if slot_shape(k)[1] == LANES else 128))
                continue
            if k in COL_SHARDED:
                parts = parts.transpose(0, 2, 1)
            out[k] = tuple(_sum8_adamw(parts, w[k], mom[k], vel[k], f"sum_adamw_{k}"))

    small_names = SMALL_REPLICATED + SMALL_SHARDED
    packed_g, small_offs = _pack_rows([grads[k] for k in small_names], F32, 8)
    total = _sum8(_all_gather([packed_g], "gather_small_grads")[0], "sum_small_grads")
    full_grads = _unpack_rows(total, small_offs, [grads[k].shape for k in small_names])
    mine = {}
    for k, g in zip(small_names, full_grads):
        if k in SMALL_SHARDED:
            c_loc = w[k].shape[2]
            g = lax.dynamic_slice_in_dim(g, me * c_loc, c_loc, axis=1)
        mine[k] = g.reshape(w[k].shape)
    sg, s_offs = _pack_rows([mine[k] for k in small_names], F32, 8)
    sw, _ = _pack_rows([w[k] for k in small_names], F32, 8)
    sm, _ = _pack_rows([mom[k] for k in small_names], F32, 8)
    sv, _ = _pack_rows([vel[k] for k in small_names], F32, 8)
    s_shapes = [w[k].shape for k in small_names]
    small_out = [_unpack_rows(a, s_offs, s_shapes) for a in _adamw(sg, sw, sm, sv, "adamw_small")]
    for i, k in enumerate(small_names):
        out[k] = (mine[k],) + tuple(o[i] for o in small_out)

    res = [loss, grad_x]
    for j in range(4):
        res += [out[k][j] for k in WEIGHT_ORDER]
    return tuple(res)


def kernel(x, mem, ffn1_norm, ffn1_w_gate_up, ffn1_w_down, mix_norm, w_in, conv_a_w, w_out_a, ssm_conv_w, ssm_conv_b, ssm_dt_bias, ssm_a_log, ssm_d, ssm_norm, w_out_ssm, w_mix_out, xattn_norm, mem_norm, w_q, w_kv, w_o_x, ffn2_norm, ffn2_w_gate_up, ffn2_w_down, final_norm, loss_target, m_ffn1_norm, m_ffn1_w_gate_up, m_ffn1_w_down, m_mix_norm, m_w_in, m_conv_a_w, m_w_out_a, m_ssm_conv_w, m_ssm_conv_b, m_ssm_dt_bias, m_ssm_a_log, m_ssm_d, m_ssm_norm, m_w_out_ssm, m_w_mix_out, m_xattn_norm, m_mem_norm, m_w_q, m_w_kv, m_w_o_x, m_ffn2_norm, m_ffn2_w_gate_up, m_ffn2_w_down, m_final_norm, v_ffn1_norm, v_ffn1_w_gate_up, v_ffn1_w_down, v_mix_norm, v_w_in, v_conv_a_w, v_w_out_a, v_ssm_conv_w, v_ssm_conv_b, v_ssm_dt_bias, v_ssm_a_log, v_ssm_d, v_ssm_norm, v_w_out_ssm, v_w_mix_out, v_xattn_norm, v_mem_norm, v_w_q, v_w_kv, v_w_o_x, v_ffn2_norm, v_ffn2_w_gate_up, v_ffn2_w_down, v_final_norm):
    return _step(dict(locals()))
```

```python
import functools
import math
import types

import jax
import jax.numpy as jnp
from jax import lax
from jax.experimental import pallas as pl
from jax.experimental.pallas import tpu as pltpu

F32, BF16 = jnp.float32, jnp.bfloat16
HI = lax.Precision.HIGHEST
MESH = pl.DeviceIdType.MESH
AXES = ("x", "y", "c")
N_DEV = 8

EPS = 1e-6
FFN_RES_WEIGHT = 0.5
SSM_HEAD_DIM = 64
SSM_GROUPS = 4
SSM_STATE = 128
SSM_CHUNK = 128
CONV_A_K = 3
SSM_CONV_K = 4
XATTN_HEADS = 4
ADAM_LR, ADAM_B1, ADAM_B2, ADAM_EPS, ADAM_WD, ADAM_STEP = 1e-3, 0.9, 0.999, 1e-8, 0.01, 10

LANES = 128
BF16_SUBLANES = 16
VMEM_LIMIT_BYTES = 56 * 2 ** 20
NEG_BIG = -1e30

BIG_WEIGHTS = ("ffn1_w_gate_up", "ffn1_w_down", "w_in", "w_out_a", "w_out_ssm", "w_mix_out",
               "w_q", "w_kv", "w_o_x", "ffn2_w_gate_up", "ffn2_w_down")
COL_SHARDED = ("ffn1_w_gate_up", "w_in", "w_kv", "ffn2_w_gate_up")
SMALL_REPLICATED = ("ffn1_norm", "mix_norm", "ssm_conv_b", "ssm_dt_bias", "ssm_a_log", "ssm_d", "ssm_norm",
                    "xattn_norm", "mem_norm", "ffn2_norm", "final_norm")
SMALL_SHARDED = ("conv_a_w", "ssm_conv_w")
WEIGHT_ORDER = ("ffn1_norm", "ffn1_w_gate_up", "ffn1_w_down", "mix_norm", "w_in", "conv_a_w", "w_out_a",
                "ssm_conv_w", "ssm_conv_b", "ssm_dt_bias", "ssm_a_log", "ssm_d", "ssm_norm", "w_out_ssm",
                "w_mix_out", "xattn_norm", "mem_norm", "w_q", "w_kv", "w_o_x", "ffn2_norm", "ffn2_w_gate_up",
                "ffn2_w_down", "final_norm")


def _tile(dim, pref, unit):
    best = None
    t = unit
    while t <= min(dim, pref):
        if dim % t == 0:
            best = t
        t += unit
    return best if best is not None else dim


def _params(*sem):
    return pltpu.CompilerParams(dimension_semantics=sem, vmem_limit_bytes=VMEM_LIMIT_BYTES)


def _sigmoid(x):
    return pl.reciprocal(1.0 + jnp.exp(-x), approx=True)


def _silu(x):
    return x * _sigmoid(x)


def _dsilu(x):
    s = _sigmoid(x)
    return s * (1.0 + x * (1.0 - s))


def _softplus(x):
    return jnp.maximum(x, 0.0) + jnp.log(1.0 + jnp.exp(-jnp.abs(x)))


def _dot(a, b, dims=(((1,), (0,)), ((), ())), precision=None):
    return lax.dot_general(a, b, dims, preferred_element_type=F32, precision=precision)


def _stack_rows(rows, width):
    r_idx = lax.broadcasted_iota(jnp.int32, (8, width), 0)
    acc = jnp.zeros((8, width), F32)
    for k, row in enumerate(rows):
        acc = jnp.where(r_idx == k, row, acc)
    return acc


NT = (((1,), (1,)), ((), ()))
TN = (((0,), (0,)), ((), ()))


def _mm(a, b, out_dtype, name, res=None, alpha=1.0, nt=False, dep=None, norm_out=None, norm_bwd=None,
        tm=1024, tn=2048, tk=2816):
    pieces = list(a) if isinstance(a, (list, tuple)) else [a]
    m = pieces[0].shape[0]
    k = sum(p.shape[1] for p in pieces)
    n = b.shape[0] if nt else b.shape[1]
    assert (b.shape[1] if nt else b.shape[0]) == k
    tm, tn = _tile(m, tm, 8), _tile(n, tn, LANES)
    tk = _tile(math.gcd(*[p.shape[1] for p in pieces]), tk, LANES)
    nk = k // tk
    starts, s0 = [], 0
    for p in pieces:
        starts.append((s0, p.shape[1] // tk))
        s0 += p.shape[1] // tk
    n_p = len(pieces)
    assert norm_out is None or norm_bwd is None
    whole_rows = norm_out is not None or norm_bwd is not None
    assert not whole_rows or tn == n
    has_pre = norm_bwd is not None and len(norm_bwd) == 3

    def body(*refs):
        a_refs, b_ref = refs[:n_p], refs[n_p]
        nxt = n_p + 1
        r_ref = refs[nxt] if res is not None else None
        nxt += (res is not None) + (dep is not None)
        g_ref = refs[nxt] if whole_rows else None
        x_ref = refs[nxt + 1] if norm_bwd is not None else None
        pre_ref = refs[nxt + 2] if has_pre else None
        nxt += whole_rows + (norm_bwd is not None) + has_pre
        o_ref = refs[nxt]
        o2_ref = refs[nxt + 1] if whole_rows else None
        scr = refs[nxt + 1 + whole_rows:]
        first_rows = pl.program_id(0) == 0

        def finish(acc):
            acc = alpha * acc if alpha != 1.0 else acc
            if norm_bwd is not None:
                if pre_ref is not None:
                    acc = acc + pre_ref[...]
                xv = x_ref[...]
                r = lax.rsqrt(jnp.mean(xv * xv, axis=-1, keepdims=True) + EPS)
                xh = xv * r
                gy = acc * g_ref[...]
                part = jnp.sum(acc * xh, axis=0, keepdims=True)
                acc = r * (gy - xh * jnp.mean(gy * xh, axis=-1, keepdims=True))

                @pl.when(first_rows)
                def _():
                    o2_ref[...] = part

                @pl.when(jnp.logical_not(first_rows))
                def _():
                    o2_ref[...] += part
            if r_ref is not None:
                acc = r_ref[...] + acc
            o_ref[...] = acc.astype(out_dtype)
            if norm_out is not None:
                r = lax.rsqrt(jnp.mean(acc * acc, axis=-1, keepdims=True) + EPS)
                o2_ref[...] = (acc * r * g_ref[...]).astype(BF16)

        def product(a_ref):
            return _dot(a_ref[...].astype(BF16), b_ref[...].astype(BF16), NT if nt else (((1,), (0,)), ((), ())))

        if nk == 1:
            finish(product(a_refs[0]))
            return
        acc_ref = scr[0]
        kk = pl.program_id(2)
        for (s, cnt), a_ref in zip(starts, a_refs):
            if s == 0:
                @pl.when(kk == 0)
                def _():
                    acc_ref[...] = product(a_ref)

                @pl.when(jnp.logical_and(kk > 0, kk < cnt))
                def _():
                    acc_ref[...] += product(a_ref)
            else:
                @pl.when(jnp.logical_and(kk >= s, kk < s + cnt))
                def _():
                    acc_ref[...] += product(a_ref)

        @pl.when(kk == nk - 1)
        def _():
            finish(acc_ref[...])

    def a_spec(s, cnt):
        return pl.BlockSpec((tm, tk), lambda i, j, kk: (i, jnp.clip(kk - s, 0, cnt - 1)))

    in_specs = [a_spec(s, cnt) for s, cnt in starts]
    in_specs.append(pl.BlockSpec((tn, tk), lambda i, j, kk: (j, kk)) if nt else pl.BlockSpec((tk, tn), lambda i, j, kk: (kk, j)))
    args = pieces + [b]
    if res is not None:
        in_specs.append(pl.BlockSpec((tm, tn), lambda i, j, kk: (i, j)))
        args.append(res)
    if dep is not None:
        in_specs.append(pl.BlockSpec((8, LANES), lambda i, j, kk: (0, 0)))
        args.append(dep)
    tile = pl.BlockSpec((tm, tn), lambda i, j, kk: (i, j))
    vec = pl.BlockSpec((1, n), lambda i, j, kk: (0, 0))
    out_specs, out_shape = tile, jax.ShapeDtypeStruct((m, n), out_dtype)
    if norm_out is not None:
        in_specs.append(vec)
        args.append(norm_out)
        out_specs, out_shape = [tile, tile], [out_shape, jax.ShapeDtypeStruct((m, n), BF16)]
    if norm_bwd is not None:
        in_specs += [vec, tile] + ([tile] if has_pre else [])
        args += [norm_bwd[1], norm_bwd[0]] + ([norm_bwd[2]] if has_pre else [])
        out_specs, out_shape = [tile, vec], [out_shape, jax.ShapeDtypeStruct((1, n), F32)]
    return pl.pallas_call(
        body, name=name, grid=(m // tm, n // tn, nk), in_specs=in_specs, out_specs=out_specs, out_shape=out_shape,
        scratch_shapes=[pltpu.VMEM((tm, tn), F32)] if nk > 1 else [],
        compiler_params=(_params("arbitrary", "arbitrary", "arbitrary") if norm_bwd is not None
                         else _params("parallel", "parallel", "arbitrary")),
    )(*args)


def _mm_tn(x, dy, name, out_dtype=BF16, alpha=1.0, dep=None, out_rows=None, row_off=0, into=None,
           tko=1408, tn=1024, tt=2048):
    t, k = x.shape
    n = dy.shape[1]
    tko, tn, tt = _tile(k, tko, LANES), _tile(n, tn, LANES), _tile(t, tt, 8)
    nt_steps = t // tt

    def body(*refs):
        x_ref, dy_ref = refs[:2]
        o_ref, acc_ref = refs[-2:]
        part = _dot(x_ref[...].astype(BF16), dy_ref[...].astype(BF16), TN)
        step = pl.program_id(2)

        @pl.when(step == 0)
        def _():
            acc_ref[...] = part

        @pl.when(step > 0)
        def _():
            acc_ref[...] += part

        @pl.when(step == nt_steps - 1)
        def _():
            acc = acc_ref[...]
            o_ref[...] = (alpha * acc if alpha != 1.0 else acc).astype(out_dtype)

    in_specs = [pl.BlockSpec((tt, tko), lambda i, j, s: (s, i)), pl.BlockSpec((tt, tn), lambda i, j, s: (s, j))]
    args = [x, dy]
    if dep is not None:
        in_specs.append(pl.BlockSpec((8, LANES), lambda i, j, s: (0, 0)))
        args.append(dep)
    aliases = {}
    if into is not None:
        in_specs.append(pl.BlockSpec(memory_space=pl.ANY))
        args.append(into)
        aliases = {len(args) - 1: 0}
    band = row_off // tko
    assert row_off % tko == 0
    return pl.pallas_call(
        body, name=name, grid=(k // tko, n // tn, nt_steps), in_specs=in_specs,
        out_specs=pl.BlockSpec((tko, tn), lambda i, j, s: (i + band, j)),
        out_shape=jax.ShapeDtypeStruct((out_rows or k, n), out_dtype),
        scratch_shapes=[pltpu.VMEM((tko, tn), F32)], input_output_aliases=aliases,
        compiler_params=_params("parallel", "parallel", "arbitrary"),
    )(*args)


def _rms_fwd(x, g, name, tt=512):
    t, d = x.shape
    tt = _tile(t, tt, 8)

    def body(x_ref, g_ref, o_ref):
        xv = x_ref[...]
        r = lax.rsqrt(jnp.mean(xv * xv, axis=-1, keepdims=True) + EPS)
        o_ref[...] = (xv * r * g_ref[...]).astype(BF16)

    return pl.pallas_call(
        body, name=name, grid=(t // tt,),
        in_specs=[pl.BlockSpec((tt, d), lambda i: (i, 0)), pl.BlockSpec((1, d), lambda i: (0, 0))],
        out_specs=pl.BlockSpec((tt, d), lambda i: (i, 0)),
        out_shape=jax.ShapeDtypeStruct((t, d), BF16), compiler_params=_params("parallel"),
    )(x, g)


def _ffn_up(n, w_gu_t, name, tm=512, tf=2816):
    t, d = n.shape
    f = w_gu_t.shape[0] // 2
    tm, tf = _tile(t, tm, 8), _tile(f, tf, LANES)
    nf = f // tf

    def body(n_ref, wg_ref, wu_ref, g_ref, u_ref, a_ref):
        nv = n_ref[...]
        gate, up = _dot(nv, wg_ref[...], NT), _dot(nv, wu_ref[...], NT)
        s = _sigmoid(gate)
        sg = gate * s
        g_ref[...] = (up * (s * (1.0 + gate * (1.0 - s)))).astype(BF16)
        u_ref[...] = sg.astype(BF16)
        a_ref[...] = (sg * up).astype(BF16)

    blk = pl.BlockSpec((tm, tf), lambda i, j: (i, j))
    out = jax.ShapeDtypeStruct((t, f), BF16)
    return pl.pallas_call(
        body, name=name, grid=(t // tm, nf),
        in_specs=[pl.BlockSpec((tm, d), lambda i, j: (i, 0)), pl.BlockSpec((tf, d), lambda i, j: (j, 0)),
                  pl.BlockSpec((tf, d), lambda i, j: (j + nf, 0))],
        out_specs=[blk, blk, blk], out_shape=[out, out, out], compiler_params=_params("parallel", "parallel"),
    )(n, w_gu_t, w_gu_t)


def _ffn_da(dh, w_d, gate, up, name, alpha, dep=None, tm=512, tf=2816):
    t, d = dh.shape
    f = w_d.shape[0]
    tm, tf = _tile(t, tm, 8), _tile(f, tf, LANES)

    def body(*refs):
        dh_ref, w_ref, g_ref, u_ref = refs[:4]
        dg_ref, du_ref = refs[-2:]
        da = alpha * _dot(dh_ref[...].astype(BF16), w_ref[...], NT)
        dg_ref[...] = (da * g_ref[...].astype(F32)).astype(BF16)
        du_ref[...] = (da * u_ref[...].astype(F32)).astype(BF16)

    blk = pl.BlockSpec((tm, tf), lambda i, j: (i, j))
    in_specs = [pl.BlockSpec((tm, d), lambda i, j: (i, 0)), pl.BlockSpec((tf, d), lambda i, j: (j, 0)), blk, blk]
    args = [dh, w_d, gate, up]
    if dep is not None:
        in_specs.append(pl.BlockSpec((8, LANES), lambda i, j: (0, 0)))
        args.append(dep)
    out = jax.ShapeDtypeStruct((t, f), BF16)
    return pl.pallas_call(
        body, name=name, grid=(t // tm, f // tf), in_specs=in_specs, out_specs=[blk, blk], out_shape=[out, out],
        compiler_params=_params("parallel", "parallel"),
    )(*args)


def _merge_fwd(ya, yb, proj, ga_blk, gb_blk, d, name, tt=512):
    t = ya.shape[0]
    tt = _tile(t, tt, 8)

    def body(ya_ref, yb_ref, ga_ref, gb_ref, o_ref):
        o_ref[...] = (_sigmoid(ga_ref[...].astype(F32)) * ya_ref[...].astype(F32)
                      + _sigmoid(gb_ref[...].astype(F32)) * yb_ref[...].astype(F32)).astype(BF16)

    row = pl.BlockSpec((tt, d), lambda i: (i, 0))
    return pl.pallas_call(
        body, name=name, grid=(t // tt,),
        in_specs=[row, row, pl.BlockSpec((tt, d), lambda i: (i, ga_blk)), pl.BlockSpec((tt, d), lambda i: (i, gb_blk))],
        out_specs=row, out_shape=jax.ShapeDtypeStruct((t, d), BF16), compiler_params=_params("parallel"),
    )(ya, yb, proj, proj)


def _merge_bwd(dm, ya, yb, proj, ga_blk, gb_blk, d, name, tt=512):
    t = ya.shape[0]
    tt = _tile(t, tt, 8)

    def body(dm_ref, ya_ref, yb_ref, ga_ref, gb_ref, dya_ref, dyb_ref, dg_ref):
        dmv = dm_ref[...].astype(F32)
        sa, sb = _sigmoid(ga_ref[...].astype(F32)), _sigmoid(gb_ref[...].astype(F32))
        dya_ref[...] = (dmv * sa).astype(BF16)
        dyb_ref[...] = (dmv * sb).astype(BF16)
        dg_ref[:, 0:d] = (dmv * ya_ref[...].astype(F32) * sa * (1.0 - sa)).astype(BF16)
        dg_ref[:, d:2 * d] = (dmv * yb_ref[...].astype(F32) * sb * (1.0 - sb)).astype(BF16)

    row = pl.BlockSpec((tt, d), lambda i: (i, 0))
    out = jax.ShapeDtypeStruct((t, d), BF16)
    return pl.pallas_call(
        body, name=name, grid=(t // tt,),
        in_specs=[row, row, row, pl.BlockSpec((tt, d), lambda i: (i, ga_blk)), pl.BlockSpec((tt, d), lambda i: (i, gb_blk))],
        out_specs=[row, row, pl.BlockSpec((tt, 2 * d), lambda i: (i, 0))],
        out_shape=[out, out, jax.ShapeDtypeStruct((t, 2 * d), BF16)], compiler_params=_params("parallel"),
    )(dm, ya, yb, proj, proj)


def _loss_head(h, g, target, name, tt=512):
    t, d = h.shape
    tt = _tile(t, tt, 8)

    def body(h_ref, g_ref, tg_ref, loss_ref, dh_ref, dg_ref):
        xv = h_ref[...]
        r = lax.rsqrt(jnp.mean(xv * xv, axis=-1, keepdims=True) + EPS)
        xh = xv * r
        err = xh * g_ref[...] - tg_ref[...]
        dout = err * (1.0 / d)
        gy = dout * g_ref[...]
        dh_ref[...] = r * (gy - xh * jnp.mean(gy * xh, axis=-1, keepdims=True))
        dg_part = jnp.sum(dout * xh, axis=0, keepdims=True)
        loss_part = jnp.full((1, LANES), 0.5 / d, F32) * jnp.sum(err * err)

        @pl.when(pl.program_id(0) == 0)
        def _():
            dg_ref[...] = dg_part
            loss_ref[...] = loss_part

        @pl.when(pl.program_id(0) > 0)
        def _():
            dg_ref[...] += dg_part
            loss_ref[...] += loss_part

    row = pl.BlockSpec((tt, d), lambda i: (i, 0))
    vec = pl.BlockSpec((1, d), lambda i: (0, 0))
    return pl.pallas_call(
        body, name=name, grid=(t // tt,), in_specs=[row, vec, row],
        out_specs=[pl.BlockSpec((1, LANES), lambda i: (0, 0)), row, vec],
        out_shape=[jax.ShapeDtypeStruct((1, LANES), F32), jax.ShapeDtypeStruct((t, d), F32), jax.ShapeDtypeStruct((1, d), F32)],
        compiler_params=_params("arbitrary"),
    )(h, g, target)


def _shift_down(x, k, t_idx):
    if k == 0:
        return x
    return jnp.where(t_idx >= k, pltpu.roll(x, k, 0), 0.0)


def _shift_up(x, k, t_idx, s):
    if k == 0:
        return x
    return jnp.where(t_idx < s - k, pltpu.roll(x, s - k, 0), 0.0)


CONV_A_BLOCK = 256


def _conv_a_fwd(proj, w, nb, s, d, name):
    cb = _tile(d, CONV_A_BLOCK, LANES)

    def body(x_ref, w_ref, o_ref):
        t_idx = lax.broadcasted_iota(jnp.int32, (s, cb), 0)
        cv = x_ref[:, cb:2 * cb].astype(F32) * x_ref[:, 2 * cb:3 * cb].astype(F32)
        cc = sum(w_ref[k:k + 1, :] * _shift_down(cv, CONV_A_K - 1 - k, t_idx) for k in range(CONV_A_K))
        o_ref[...] = (x_ref[:, 0:cb].astype(F32) * cc).astype(BF16)

    return pl.pallas_call(
        body, name=name, grid=(nb, d // cb),
        in_specs=[pl.BlockSpec((s, 3 * cb), lambda b, j: (b, j)), pl.BlockSpec((8, cb), lambda b, j: (0, j))],
        out_specs=pl.BlockSpec((s, cb), lambda b, j: (b, j)),
        out_shape=jax.ShapeDtypeStruct((nb * s, d), BF16), compiler_params=_params("parallel", "parallel"),
    )(proj, w)


def _conv_a_bwd(dy, proj, w, nb, s, d, out_cols, name):
    cb = _tile(d, CONV_A_BLOCK, LANES)

    def body(dy_ref, x_ref, w_ref, o_ref, dw_ref):
        t_idx = lax.broadcasted_iota(jnp.int32, (s, cb), 0)
        cv_c, cv_v = x_ref[:, cb:2 * cb].astype(F32), x_ref[:, 2 * cb:3 * cb].astype(F32)
        cv = cv_c * cv_v
        shifted = [_shift_down(cv, CONV_A_K - 1 - k, t_idx) for k in range(CONV_A_K)]
        cc = sum(w_ref[k:k + 1, :] * shifted[k] for k in range(CONV_A_K))
        dyv = dy_ref[...].astype(F32)
        o_ref[:, 0:cb] = (dyv * cc).astype(BF16)
        dcc = dyv * x_ref[:, 0:cb].astype(F32)
        dcv = sum(w_ref[k:k + 1, :] * _shift_up(dcc, CONV_A_K - 1 - k, t_idx, s) for k in range(CONV_A_K))
        o_ref[:, cb:2 * cb] = (dcv * cv_v).astype(BF16)
        o_ref[:, 2 * cb:3 * cb] = (dcv * cv_c).astype(BF16)
        rows = [jnp.sum(dcc * shifted[k], axis=0, keepdims=True) for k in range(CONV_A_K)]
        part = _stack_rows(rows, cb)

        @pl.when(pl.program_id(1) == 0)
        def _():
            dw_ref[...] = part

        @pl.when(pl.program_id(1) > 0)
        def _():
            dw_ref[...] += part

    wspec = pl.BlockSpec((8, cb), lambda j, b: (0, j))
    wide = pl.BlockSpec((s, 3 * cb), lambda j, b: (b, j))
    return pl.pallas_call(
        body, name=name, grid=(d // cb, nb), in_specs=[pl.BlockSpec((s, cb), lambda j, b: (b, j)), wide, wspec],
        out_specs=[wide, wspec],
        out_shape=[jax.ShapeDtypeStruct((nb * s, out_cols), BF16), jax.ShapeDtypeStruct((8, d), F32)],
        compiler_params=_params("parallel", "arbitrary"),
    )(dy, proj, w)


def _conv_s_fwd(proj, col0, w, bias, nb, s, cc_width, name, cb=256):
    cb = _tile(math.gcd(cc_width, col0) if col0 else cc_width, cb, LANES)
    nd, off = cc_width // cb, col0 // cb

    def body(x_ref, w_ref, b_ref, o_ref, pre_ref):
        t_idx = lax.broadcasted_iota(jnp.int32, (s, cb), 0)
        xv = x_ref[...].astype(F32)
        pre = b_ref[...] + sum(w_ref[k:k + 1, :] * _shift_down(xv, SSM_CONV_K - 1 - k, t_idx) for k in range(SSM_CONV_K))
        o_ref[...] = _silu(pre).astype(BF16)
        pre_ref[...] = pre.astype(BF16)

    vec = pl.BlockSpec((8, cb), lambda b, j: (0, j))
    own = pl.BlockSpec((s, cb), lambda b, j: (b, j))
    out = jax.ShapeDtypeStruct((nb * s, cc_width), BF16)
    return pl.pallas_call(
        body, name=name, grid=(nb, nd),
        in_specs=[pl.BlockSpec((s, cb), lambda b, j: (b, j + off)), vec, pl.BlockSpec((1, cb), lambda b, j: (0, j))],
        out_specs=[own, own], out_shape=[out, out], compiler_params=_params("parallel", "parallel"),
    )(proj, w, bias)


def _conv_s_bwd(dxc, pre, proj, col0, w, nb, s, cc_width, into, name, cb=256):
    cb = _tile(math.gcd(cc_width, col0) if col0 else cc_width, cb, LANES)
    nd, off = cc_width // cb, col0 // cb

    def body(d_ref, pre_ref, x_ref, w_ref, into_ref, dx_ref, dw_ref, db_ref):
        t_idx = lax.broadcasted_iota(jnp.int32, (s, cb), 0)
        xv = x_ref[...].astype(F32)
        dpre = d_ref[...].astype(F32) * _dsilu(pre_ref[...].astype(F32))
        ahead = [_shift_up(dpre, j, t_idx, s) for j in range(SSM_CONV_K)]
        dx_ref[...] = sum(w_ref[k:k + 1, :] * ahead[SSM_CONV_K - 1 - k] for k in range(SSM_CONV_K)).astype(BF16)
        rows = [jnp.sum(ahead[SSM_CONV_K - 1 - k] * xv, axis=0, keepdims=True) for k in range(SSM_CONV_K)]
        dw_part = _stack_rows(rows, cb)
        db_part = jnp.sum(dpre, axis=0, keepdims=True)

        @pl.when(pl.program_id(1) == 0)
        def _():
            dw_ref[...] = dw_part
            db_ref[...] = db_part

        @pl.when(pl.program_id(1) > 0)
        def _():
            dw_ref[...] += dw_part
            db_ref[...] += db_part

    own = pl.BlockSpec((s, cb), lambda j, b: (b, j))
    shifted = pl.BlockSpec((s, cb), lambda j, b: (b, j + off))
    wspec = pl.BlockSpec((8, cb), lambda j, b: (0, j))
    bspec = pl.BlockSpec((1, cb), lambda j, b: (0, j))
    return pl.pallas_call(
        body, name=name, grid=(nd, nb),
        in_specs=[own, own, shifted, wspec, pl.BlockSpec(memory_space=pl.ANY)],
        out_specs=[shifted, wspec, bspec],
        out_shape=[jax.ShapeDtypeStruct(into.shape, into.dtype), jax.ShapeDtypeStruct((8, cc_width), F32),
                   jax.ShapeDtypeStruct((1, cc_width), F32)],
        input_output_aliases={4: 0},
        compiler_params=_params("parallel", "arbitrary"),
    )(dxc, pre, proj, w, into)


def _split3(v):
    hi = v.astype(BF16)
    r1 = v - hi.astype(F32)
    mid = r1.astype(BF16)
    return hi, mid, (r1 - mid.astype(F32)).astype(BF16)


def _exact_left(mask_b, v):
    return sum(_dot(mask_b, t) for t in _split3(v))


def _exact_right(v, mask_b):
    return sum(_dot(t, mask_b) for t in _split3(v))


def _head_sums(v, e_b):
    return _dot(v.astype(BF16), e_b, NT)


def _spread(v, out_ref, di):
    lane = lax.broadcasted_iota(jnp.int32, (v.shape[0], LANES), 1)
    for pr in range(di // LANES):
        h0 = pr * (LANES // SSM_HEAD_DIM)
        out_ref[:, pr * LANES:(pr + 1) * LANES] = jnp.where(lane < SSM_HEAD_DIM, v[:, h0:h0 + 1], v[:, h0 + 1:h0 + 2])


def _ssd_common(xc_ref, dtr_ref, dtrt_ref, prow_ref, pcol_ref, dtx_ref, acsx_ref, dx_ref, di):
    l = SSM_CHUNK
    bias_r, a_r = prow_ref[0:1, :], -jnp.exp(prow_ref[1:2, :])
    sp_in = dtr_ref[...] + bias_r
    dt = _softplus(sp_in)
    li = lax.broadcasted_iota(jnp.int32, (l, l), 0)
    si = lax.broadcasted_iota(jnp.int32, (l, l), 1)
    lower_b = (li >= si).astype(BF16)
    upper_b = (li <= si).astype(BF16)
    acs = _exact_left(lower_b, dt * a_r)
    bias_c, a_c = pcol_ref[:, 0:1], -jnp.exp(pcol_ref[:, 1:2])
    dt_t = _softplus(dtrt_ref[...] + bias_c)
    acs_t = _exact_right(dt_t * a_c, upper_b)
    _spread(dt, dtx_ref, di)
    _spread(acs, acsx_ref, di)
    _spread(prow_ref[0:8, :], dx_ref, di)
    acs_exp = acsx_ref[...]
    acs_last = acs_exp[l - 1:l, :]
    x = xc_ref[:, 0:di].astype(F32)
    return dict(dt=dt, a_r=a_r, sp_in=sp_in, acs=acs, acs_t=acs_t, dt_exp=dtx_ref[...], e_exp=jnp.exp(acs_exp),
                el_exp=jnp.exp(acs_last), f_exp=jnp.exp(acs_last - acs_exp), x=x, mask=li >= si, upper_b=upper_b,
                d_exp=dx_ref[2:3, :])


def _decay(q, h):
    seg = q["acs"][:, h:h + 1] - q["acs_t"][h:h + 1, :]
    return jnp.exp(jnp.where(q["mask"], seg, NEG_BIG))


def _ssd_fwd(xc, dtr, dtrt, prow, pcol, nb, nc, di, name):
    l, n, g_n, p = SSM_CHUNK, SSM_STATE, SSM_GROUPS, SSM_HEAD_DIM
    cc = xc.shape[1]
    gw = di // g_n
    assert p * 2 == LANES and gw % LANES == 0

    def body(xc_ref, dtr_ref, dtrt_ref, prow_ref, pcol_ref, y_ref, sprev_ref, st_ref, dtx_ref, acsx_ref, dx_ref):
        @pl.when(pl.program_id(1) == 0)
        def _():
            st_ref[...] = jnp.zeros_like(st_ref)

        q = _ssd_common(xc_ref, dtr_ref, dtrt_ref, prow_ref, pcol_ref, dtx_ref, acsx_ref, dx_ref, di)
        x = q["x"]
        xd = x * q["dt_exp"]
        xdb = xd.astype(BF16)
        xdf = (xd * q["f_exp"]).astype(BF16)
        lane = lax.broadcasted_iota(jnp.int32, (l, LANES), 1)
        for g in range(g_n):
            lo = g * gw
            bg = xc_ref[:, di + g * n: di + (g + 1) * n]
            cg = xc_ref[:, di + g_n * n + g * n: di + g_n * n + (g + 1) * n]
            cb = _dot(cg, bg, NT)
            st_g = st_ref[:, lo:lo + gw]
            y_off = q["e_exp"][:, lo:lo + gw] * _dot(cg, st_g.astype(BF16))
            for pr in range(gw // LANES):
                c0 = lo + pr * LANES
                h0 = c0 // p
                xp = xdb[:, c0:c0 + LANES]
                m0 = (cb * _decay(q, h0)).astype(BF16)
                m1 = (cb * _decay(q, h0 + 1)).astype(BF16)
                yd = _dot(m0, jnp.where(lane < p, xp, 0)) + _dot(m1, jnp.where(lane >= p, xp, 0))
                y_ref[:, c0:c0 + LANES] = (yd + y_off[:, pr * LANES:(pr + 1) * LANES]
                                           + q["d_exp"][:, c0:c0 + LANES] * x[:, c0:c0 + LANES]).astype(BF16)
            sprev_ref[:, lo:lo + gw] = st_g
            st_ref[:, lo:lo + gw] = q["el_exp"][:, lo:lo + gw] * st_g + _dot(bg, xdf[:, lo:lo + gw], TN)

    tok = lambda w: pl.BlockSpec((l, w), lambda b, c: (b * nc + c, 0))
    const = lambda r, w: pl.BlockSpec((r, w), lambda b, c: (0, 0))
    return pl.pallas_call(
        body, name=name, grid=(nb, nc),
        in_specs=[tok(cc), tok(LANES), pl.BlockSpec((LANES, l), lambda b, c: (0, b * nc + c)),
                  const(8, LANES), const(LANES, 8)],
        out_specs=[tok(di), pl.BlockSpec((None, n, di), lambda b, c: (b * nc + c, 0, 0))],
        out_shape=[jax.ShapeDtypeStruct((nb * nc * l, di), BF16), jax.ShapeDtypeStruct((nb * nc, n, di), F32)],
        scratch_shapes=[pltpu.VMEM((n, di), F32), pltpu.VMEM((l, di), F32), pltpu.VMEM((l, di), F32),
                        pltpu.VMEM((8, di), F32)],
        compiler_params=_params("parallel", "arbitrary"),
    )(xc, dtr, dtrt, prow, pcol)


def _ssd_bwd(dy, xc, dtr, dtrt, prow, pcol, e_mat, sprev, nb, nc, di, name):
    l, n, g_n, p = SSM_CHUNK, SSM_STATE, SSM_GROUPS, SSM_HEAD_DIM
    cc = xc.shape[1]
    gw = di // g_n

    def body(dy_ref, xc_ref, dtr_ref, dtrt_ref, prow_ref, pcol_ref, e_ref, sprev_ref,
             dxc_ref, ddtr_ref, sums_ref, dst_ref, off_ref, dxd_ref, last_ref, vst_ref,
             dtx_ref, acsx_ref, dx_ref):
        first = jnp.logical_and(pl.program_id(0) == 0, pl.program_id(1) == 0)

        @pl.when(pl.program_id(1) == 0)
        def _():
            dst_ref[...] = jnp.zeros_like(dst_ref)

        head_row = lax.broadcasted_iota(jnp.int32, (LANES, l), 0)
        row_sums, col_sums = jnp.zeros((l, LANES), F32), jnp.zeros((LANES, l), F32)
        strict_lower = lax.broadcasted_iota(jnp.int32, (l, l), 0) > lax.broadcasted_iota(jnp.int32, (l, l), 1)

        q = _ssd_common(xc_ref, dtr_ref, dtrt_ref, prow_ref, pcol_ref, dtx_ref, acsx_ref, dx_ref, di)
        x = q["x"]
        xd = x * q["dt_exp"]
        xdb = xd.astype(BF16)
        xdf = (xd * q["f_exp"]).astype(BF16)
        dyv = dy_ref[...].astype(F32)
        dyb = dy_ref[...]
        dye = (dyv * q["e_exp"]).astype(BF16)
        upper_b = q["upper_b"]
        lane = lax.broadcasted_iota(jnp.int32, (l, LANES), 1)
        for g in range(g_n):
            lo = g * gw
            bg = xc_ref[:, di + g * n: di + (g + 1) * n]
            cg = xc_ref[:, di + g_n * n + g * n: di + g_n * n + (g + 1) * n]
            cb = _dot(cg, bg, NT)
            st_g = sprev_ref[:, lo:lo + gw]
            st_gb = st_g.astype(BF16)
            dst_g = dst_ref[:, lo:lo + gw]
            dst_gb = dst_g.astype(BF16)
            dye_g = dye[:, lo:lo + gw]
            xdf_g = xdf[:, lo:lo + gw]
            y_off = q["e_exp"][:, lo:lo + gw] * _dot(cg, st_gb)
            dc_g = _dot(dye_g, st_gb, NT)
            db_g = _dot(xdf_g, dst_gb, NT)
            dxd_state = _dot(bg, dst_gb) * q["f_exp"][:, lo:lo + gw]
            last_ref[:, lo:lo + gw] = jnp.sum(dst_g * st_g, axis=0, keepdims=True)
            dst_ref[:, lo:lo + gw] = q["el_exp"][:, lo:lo + gw] * dst_g + _dot(cg, dye_g, TN)
            off_ref[:, lo:lo + gw] = dyv[:, lo:lo + gw] * y_off
            vst_ref[:, lo:lo + gw] = xd[:, lo:lo + gw] * dxd_state
            dcb = jnp.zeros((l, l), F32)
            for pr in range(gw // LANES):
                c0 = lo + pr * LANES
                h0 = c0 // p
                xp = xdb[:, c0:c0 + LANES]
                dyp = dyb[:, c0:c0 + LANES]
                dxd_diag = jnp.zeros((l, LANES), F32)
                for k, keep in enumerate((lane < p, lane >= p)):
                    dec = _decay(q, h0 + k)
                    dy_h = jnp.where(keep, dyp, 0)
                    dm_dec = _dot(dy_h, xp, NT) * dec
                    dcb = dcb + dm_dec
                    dxd_diag = dxd_diag + _dot((cb * dec).astype(BF16), dy_h, TN)
                    qm = dm_dec * cb
                    row_sums = jnp.where(lane == h0 + k, jnp.sum(qm, axis=1, keepdims=True), row_sums)
                    col_sums = jnp.where(head_row == h0 + k, jnp.sum(qm, axis=0, keepdims=True), col_sums)
                dxd_ref[:, c0:c0 + LANES] = dxd_diag + dxd_state[:, pr * LANES:(pr + 1) * LANES]
            dcb_b = dcb.astype(BF16)
            dxc_ref[:, di + g * n: di + (g + 1) * n] = (db_g + _dot(dcb_b, cg, TN)).astype(BF16)
            dxc_ref[:, di + g_n * n + g * n: di + g_n * n + (g + 1) * n] = (dc_g + _dot(dcb_b, bg)).astype(BF16)
        dxd = dxd_ref[...]
        e_b = e_ref[...]
        from_y = _exact_left(upper_b, _head_sums(off_ref[...], e_b) + row_sums - col_sums.T)
        from_s = _exact_left(strict_lower.astype(BF16), _head_sums(vst_ref[...], e_b))
        carried = _head_sums(jnp.broadcast_to(last_ref[...], (8, di)), e_b)[0:1, :] * jnp.exp(q["acs"][l - 1:l, :])
        dla = from_y + from_s + carried
        ddt = dla * q["a_r"] + _head_sums(dxd * x, e_b)
        ddtr = ddt * jax.nn.sigmoid(q["sp_in"])
        ddtr_ref[...] = ddtr
        dxc_ref[:, 0:di] = (dxd * q["dt_exp"] + q["d_exp"] * dyv).astype(BF16)
        dd_exp = jnp.sum(dyv * x, axis=0, keepdims=True)
        dd = _head_sums(jnp.broadcast_to(dd_exp, (8, di)), e_b)[0:1, :]
        part = _stack_rows([jnp.sum(ddtr, axis=0, keepdims=True),
                            jnp.sum(dla * q["dt"], axis=0, keepdims=True) * q["a_r"], dd], LANES)

        @pl.when(first)
        def _():
            sums_ref[...] = part

        @pl.when(jnp.logical_not(first))
        def _():
            sums_ref[...] += part

    rev = lambda b, c: b * nc + (nc - 1 - c)
    tok = lambda w: pl.BlockSpec((l, w), lambda b, c: (rev(b, c), 0))
    const = lambda r, w: pl.BlockSpec((r, w), lambda b, c: (0, 0))
    return pl.pallas_call(
        body, name=name, grid=(nb, nc),
        in_specs=[tok(di), tok(cc), tok(LANES), pl.BlockSpec((LANES, l), lambda b, c: (0, rev(b, c))),
                  const(8, LANES), const(LANES, 8), const(LANES, di),
                  pl.BlockSpec((None, n, di), lambda b, c: (rev(b, c), 0, 0))],
        out_specs=[tok(cc), tok(LANES), const(8, LANES)],
        out_shape=[jax.ShapeDtypeStruct((nb * nc * l, cc), BF16), jax.ShapeDtypeStruct((nb * nc * l, LANES), F32),
                   jax.ShapeDtypeStruct((8, LANES), F32)],
        scratch_shapes=[pltpu.VMEM((n, di), F32), pltpu.VMEM((l, di), F32), pltpu.VMEM((l, di), F32),
                        pltpu.VMEM((1, di), F32), pltpu.VMEM((l, di), F32),
                        pltpu.VMEM((l, di), F32), pltpu.VMEM((l, di), F32), pltpu.VMEM((8, di), F32)],
        compiler_params=_params("arbitrary", "arbitrary"),
    )(dy, xc, dtr, dtrt, prow, pcol, e_mat, sprev)


def _gate_norm_fwd(y, proj, z_col0, norm_g, di, name, tt=256):
    t = y.shape[0]
    tt = _tile(t, tt, 8)
    gw = di // SSM_GROUPS
    zw = _tile(math.gcd(di, z_col0), di, LANES)
    nz, zoff = di // zw, z_col0 // zw

    def body(*refs):
        y_ref, z_refs, g_ref, o_ref = refs[0], refs[1:1 + nz], refs[1 + nz], refs[2 + nz]
        for g in range(SSM_GROUPS):
            lo = g * gw
            zv = z_refs[lo // zw][:, lo % zw:lo % zw + gw].astype(F32)
            yg = y_ref[:, lo:lo + gw].astype(F32) * _silu(zv)
            r = lax.rsqrt(jnp.mean(yg * yg, axis=-1, keepdims=True) + EPS)
            o_ref[:, lo:lo + gw] = (yg * r * g_ref[:, lo:lo + gw]).astype(BF16)

    row = pl.BlockSpec((tt, di), lambda i: (i, 0))
    zspecs = [pl.BlockSpec((tt, zw), functools.partial(lambda i, k: (i, zoff + k), k=k)) for k in range(nz)]
    return pl.pallas_call(
        body, name=name, grid=(t // tt,), in_specs=[row] + zspecs + [pl.BlockSpec((1, di), lambda i: (0, 0))],
        out_specs=row, out_shape=jax.ShapeDtypeStruct((t, di), BF16), compiler_params=_params("parallel"),
    )(y, *([proj] * nz), norm_g)


def _gate_norm_bwd(dn, y, proj, z_col0, norm_g, di, name, tt=256):
    t = y.shape[0]
    tt = _tile(t, tt, 8)
    gw = di // SSM_GROUPS
    zw = _tile(math.gcd(di, z_col0), di, LANES)
    nz, zoff = di // zw, z_col0 // zw

    def body(*refs):
        dn_ref, y_ref, z_refs, g_ref = refs[0], refs[1], refs[2:2 + nz], refs[2 + nz]
        dy_ref, dz_ref, dg_ref = refs[3 + nz:]
        first = pl.program_id(0) == 0
        for g in range(SSM_GROUPS):
            lo = g * gw
            zv = z_refs[lo // zw][:, lo % zw:lo % zw + gw].astype(F32)
            yv = y_ref[:, lo:lo + gw].astype(F32)
            sz = _silu(zv)
            yg = yv * sz
            r = lax.rsqrt(jnp.mean(yg * yg, axis=-1, keepdims=True) + EPS)
            yh = yg * r
            dnv = dn_ref[:, lo:lo + gw].astype(F32)
            gy = dnv * g_ref[:, lo:lo + gw]
            dyg = r * (gy - yh * jnp.mean(gy * yh, axis=-1, keepdims=True))
            dy_ref[:, lo:lo + gw] = (dyg * sz).astype(BF16)
            dz_ref[:, lo:lo + gw] = (dyg * yv * _dsilu(zv)).astype(BF16)
            part = jnp.sum(dnv * yh, axis=0, keepdims=True)

            @pl.when(first)
            def _():
                dg_ref[:, lo:lo + gw] = part

            @pl.when(jnp.logical_not(first))
            def _():
                dg_ref[:, lo:lo + gw] += part

    row = pl.BlockSpec((tt, di), lambda i: (i, 0))
    vec = pl.BlockSpec((1, di), lambda i: (0, 0))
    zspecs = [pl.BlockSpec((tt, zw), functools.partial(lambda i, k: (i, zoff + k), k=k)) for k in range(nz)]
    return pl.pallas_call(
        body, name=name, grid=(t // tt,), in_specs=[row, row] + zspecs + [vec], out_specs=[row, row, vec],
        out_shape=[jax.ShapeDtypeStruct((t, di), BF16), jax.ShapeDtypeStruct((t, di), BF16), jax.ShapeDtypeStruct((1, di), F32)],
        compiler_params=_params("arbitrary"),
    )(dn, y, *([proj] * nz), norm_g)


def _softmax_rows(s):
    s = s - jnp.max(s, axis=-1, keepdims=True)
    e = jnp.exp(s)
    return e * (1.0 / jnp.sum(e, axis=-1, keepdims=True))


def _xattn_fwd(q, kv, nb, s, m, d, name, tq=1024):
    tq = _tile(s, tq, 8)
    nq = s // tq
    hd = d // XATTN_HEADS
    scale = 1.0 / math.sqrt(hd)

    def body(q_ref, k_ref, v_ref, o_ref):
        for h in range(XATTN_HEADS):
            sl = slice(h * hd, (h + 1) * hd)
            prob = _softmax_rows(_dot(q_ref[:, sl], k_ref[:, sl], NT) * scale)
            o_ref[:, sl] = _dot(prob.astype(BF16), v_ref[:, sl]).astype(BF16)

    return pl.pallas_call(
        body, name=name, grid=(nb, nq),
        in_specs=[pl.BlockSpec((tq, d), lambda b, i: (b * nq + i, 0)), pl.BlockSpec((m, d), lambda b, i: (b, 0)),
                  pl.BlockSpec((m, d), lambda b, i: (b, 1))],
        out_specs=pl.BlockSpec((tq, d), lambda b, i: (b * nq + i, 0)),
        out_shape=jax.ShapeDtypeStruct((nb * s, d), BF16), compiler_params=_params("parallel", "parallel"),
    )(q, kv, kv)


def _xattn_bwd(do, q, kv, nb, s, m, d, name, tq=1024):
    tq = _tile(s, tq, 8)
    nq = s // tq
    hd = d // XATTN_HEADS
    scale = 1.0 / math.sqrt(hd)

    def body(do_ref, q_ref, k_ref, v_ref, dq_ref, dk_ref, dv_ref):
        first = pl.program_id(1) == 0
        for h in range(XATTN_HEADS):
            sl = slice(h * hd, (h + 1) * hd)
            qh, kh, vh, doh = q_ref[:, sl], k_ref[:, sl], v_ref[:, sl], do_ref[:, sl]
            prob = _softmax_rows(_dot(qh, kh, NT) * scale)
            dv_h = _dot(prob.astype(BF16), doh, TN)
            dp = _dot(doh, vh, NT)
            ds = (prob * (dp - jnp.sum(dp * prob, axis=-1, keepdims=True)) * scale).astype(BF16)
            dq_ref[:, sl] = _dot(ds, kh).astype(BF16)
            dk_h = _dot(ds, qh, TN)

            @pl.when(first)
            def _():
                dk_ref[:, sl] = dk_h
                dv_ref[:, sl] = dv_h

            @pl.when(jnp.logical_not(first))
            def _():
                dk_ref[:, sl] += dk_h
                dv_ref[:, sl] += dv_h

    qspec = pl.BlockSpec((tq, d), lambda b, i: (b * nq + i, 0))
    dq, dk, dv = pl.pallas_call(
        body, name=name, grid=(nb, nq),
        in_specs=[qspec, qspec, pl.BlockSpec((m, d), lambda b, i: (b, 0)), pl.BlockSpec((m, d), lambda b, i: (b, 1))],
        out_specs=[qspec, pl.BlockSpec((m, d), lambda b, i: (b, 0)), pl.BlockSpec((m, d), lambda b, i: (b, 0))],
        out_shape=[jax.ShapeDtypeStruct((nb * s, d), BF16), jax.ShapeDtypeStruct((nb * m, d), F32),
                   jax.ShapeDtypeStruct((nb * m, d), F32)],
        compiler_params=_params("parallel", "arbitrary"),
    )(do, q, kv, kv)
    return dq, dk, dv


def _all_gather(shards, name):
    n_arr = len(shards)

    def body(*refs):
        x_refs, out_refs = refs[:n_arr], refs[n_arr:2 * n_arr]
        send_sems, recv_sems, local_sems = refs[2 * n_arr:]
        x, y, c = lax.axis_index("x"), lax.axis_index("y"), lax.axis_index("c")
        me, sibling = (x, y, c), (x, y, 1 - c)
        chips = [(1 - x, y), (x, 1 - y), (1 - x, 1 - y)]

        def copy(w, k, block, to, from_input=False):
            px, py, pc = block
            rows = out_refs[w].at[4 * px + 2 * py + pc]
            return pltpu.make_async_remote_copy(
                src_ref=x_refs[w] if from_input else rows, dst_ref=rows,
                send_sem=send_sems.at[7 * w + k], recv_sem=recv_sems.at[7 * w + k], device_id=to, device_id_type=MESH)

        started = []
        for w in range(n_arr):
            mine = pltpu.make_async_copy(x_refs[w], out_refs[w].at[4 * x + 2 * y + c], local_sems.at[w])
            mine.start()
            started.append(mine)
        sends = []
        for w in range(n_arr):
            sends.append(copy(w, 0, me, sibling, from_input=True))
            sends += [copy(w, 1 + j, me, (*chip, c), from_input=True) for j, chip in enumerate(chips)]
        for cp in sends:
            cp.start()
        for j, chip in enumerate(chips):
            for w in range(n_arr):
                copy(w, 1 + j, (*chip, c), me).wait_recv()
                passed = copy(w, 4 + j, (*chip, c), sibling)
                passed.start()
                sends.append(passed)
        for w in range(n_arr):
            copy(w, 0, sibling, me).wait_recv()
            for j, chip in enumerate(chips):
                copy(w, 4 + j, (*chip, 1 - c), me).wait_recv()
        for cp in sends:
            cp.wait_send()
        for mine in started:
            mine.wait()

    hbm = pl.BlockSpec(memory_space=pl.ANY)
    return pl.pallas_call(
        body, name=name, out_shape=[jax.ShapeDtypeStruct((N_DEV,) + s.shape, s.dtype) for s in shards],
        in_specs=[hbm] * n_arr, out_specs=[hbm] * n_arr,
        scratch_shapes=[pltpu.SemaphoreType.DMA((7 * n_arr,)), pltpu.SemaphoreType.DMA((7 * n_arr,)),
                        pltpu.SemaphoreType.DMA((n_arr,))],
    )(*shards)


_HBM = pl.BlockSpec(memory_space=pltpu.HBM)
_SEM = pl.BlockSpec(memory_space=pltpu.SEMAPHORE)
_DATAFLOW = pltpu.SideEffectType.DATAFLOW_SIDE_EFFECTING


def _peer_list(x, y, c):
    return [(1 - x if k & 4 else x, 1 - y if k & 2 else y, 1 - c if k & 1 else c) for k in range(1, N_DEV)]


def _push_copy(src_ref, land_ref, send_sems, recv_sems, w, k, peer, me, per_peer_src, receiving):
    px, py, pc = peer
    peer_slot = 4 * px + 2 * py + pc
    return pltpu.make_async_remote_copy(
        src_ref=src_ref.at[peer_slot] if per_peer_src else src_ref,
        dst_ref=land_ref.at[peer_slot if receiving else me],
        send_sem=send_sems.at[7 * w + k], recv_sem=recv_sems.at[7 * w + k], device_id=peer, device_id_type=MESH)


def _push_start(srcs, per_peer_src, after, name):
    n_arr = len(srcs)
    land_shapes = [s.shape if per_peer_src else (N_DEV,) + s.shape for s in srcs]

    def body(*refs):
        src_refs, land_refs = refs[:n_arr], refs[n_arr:2 * n_arr]
        send_sems, recv_sems = refs[2 * n_arr + 1], refs[2 * n_arr + 2]
        token = refs[-1]
        x, y, c = lax.axis_index("x"), lax.axis_index("y"), lax.axis_index("c")
        me = 4 * x + 2 * y + c
        for w in range(n_arr):
            for k, peer in enumerate(_peer_list(x, y, c)):
                _push_copy(src_refs[w], land_refs[w], send_sems, recv_sems, w, k, peer, me, per_peer_src, False).start()
        token[...] = jnp.zeros_like(token)

    lands = [pltpu.with_memory_space_constraint(lax.empty(ls, s.dtype), pltpu.HBM) for ls, s in zip(land_shapes, srcs)]
    srcs_hbm = [pltpu.with_memory_space_constraint(s, pltpu.HBM) for s in srcs]
    out = pl.pallas_call(
        body, name=name,
        out_shape=(pltpu.SemaphoreType.DMA((7 * n_arr,)), pltpu.SemaphoreType.DMA((7 * n_arr,)),
                   *[pltpu.HBM(s.shape, s.dtype) for s in srcs], *[pltpu.HBM(ls, s.dtype) for ls, s in zip(land_shapes, srcs)],
                   jax.ShapeDtypeStruct((8, LANES), F32)),
        in_specs=[_HBM] * (2 * n_arr) + [pl.BlockSpec(memory_space=pl.ANY)],
        out_specs=(_SEM, _SEM, *([_HBM] * (2 * n_arr)), pl.BlockSpec(memory_space=pltpu.VMEM)),
        input_output_aliases={i: 2 + i for i in range(2 * n_arr)},
        compiler_params=pltpu.CompilerParams(has_side_effects=_DATAFLOW),
    )(*srcs_hbm, *lands, after)
    return dict(send=out[0], recv=out[1], srcs=list(out[2:2 + n_arr]), lands=list(out[2 + n_arr:2 + 2 * n_arr]),
                token=out[-1])


def _push_wait(pending, per_peer_src, after, name):
    n_arr = len(pending["srcs"])

    def body(*refs):
        src_refs, land_refs = refs[:n_arr], refs[n_arr:2 * n_arr]
        send_sems, recv_sems = refs[2 * n_arr], refs[2 * n_arr + 1]
        x, y, c = lax.axis_index("x"), lax.axis_index("y"), lax.axis_index("c")
        me = 4 * x + 2 * y + c
        for w in range(n_arr):
            for k, peer in enumerate(_peer_list(x, y, c)):
                cp = _push_copy(src_refs[w], land_refs[w], send_sems, recv_sems, w, k, peer, me, per_peer_src, True)
                cp.wait_send()
                cp.wait_recv()

    out = pl.pallas_call(
        body, name=name,
        out_shape=tuple(pltpu.HBM(a.shape, a.dtype) for a in pending["srcs"] + pending["lands"]),
        in_specs=[_HBM] * (2 * n_arr) + [_SEM, _SEM, pl.BlockSpec(memory_space=pl.ANY)],
        out_specs=tuple([_HBM] * (2 * n_arr)),
        input_output_aliases={i: i for i in range(2 * n_arr)},
        compiler_params=pltpu.CompilerParams(has_side_effects=_DATAFLOW),
    )(*pending["srcs"], *pending["lands"], pending["send"], pending["recv"], after)
    return list(out[:n_arr]), list(out[n_arr:])


def _adamw_math(w, g, m, v):
    m = ADAM_B1 * m + (1.0 - ADAM_B1) * g
    v = ADAM_B2 * v + (1.0 - ADAM_B2) * (g * g)
    m_hat = m / (1.0 - ADAM_B1 ** ADAM_STEP)
    v_hat = v / (1.0 - ADAM_B2 ** ADAM_STEP)
    delta = -ADAM_LR * (m_hat / (jnp.sqrt(v_hat) + ADAM_EPS) + ADAM_WD * w)
    return delta, m, v


def _sum8(parts, name, tr=512):
    _, r, c_dim = parts.shape
    tr = _tile(r, tr, BF16_SUBLANES)

    def body(p_ref, o_ref):
        acc = p_ref[0].astype(F32)
        for k in range(1, N_DEV):
            acc = acc + p_ref[k].astype(F32)
        o_ref[...] = acc

    return pl.pallas_call(
        body, name=name, grid=(r // tr,), in_specs=[pl.BlockSpec((N_DEV, tr, c_dim), lambda i: (0, i, 0))],
        out_specs=pl.BlockSpec((tr, c_dim), lambda i: (i, 0)),
        out_shape=jax.ShapeDtypeStruct((r, c_dim), F32), compiler_params=_params("parallel"),
    )(parts)


def _sum8_adamw(parts, w, m, v, name, tr=128):
    _, r, c_dim = parts.shape
    tr = _tile(r, tr, BF16_SUBLANES)
    tc = c_dim if tr <= 2 * LANES else _tile(c_dim, LANES, LANES)

    def body(p_ref, w_ref, m_ref, v_ref, g_ref, d_ref, nm_ref, nv_ref):
        g = p_ref[0].astype(F32)
        for k in range(1, N_DEV):
            g = g + p_ref[k].astype(F32)
        g_ref[...] = g
        d_ref[...], nm_ref[...], nv_ref[...] = _adamw_math(w_ref[...], g, m_ref[...], v_ref[...])

    blk = pl.BlockSpec((None, tr, tc), lambda i, j: (0, i, j))
    out = jax.ShapeDtypeStruct((1, r, c_dim), F32)
    return pl.pallas_call(
        body, name=name, grid=(r // tr, c_dim // tc),
        in_specs=[pl.BlockSpec((N_DEV, tr, tc), lambda i, j: (0, i, j)), blk, blk, blk],
        out_specs=[blk] * 4, out_shape=[out] * 4, compiler_params=_params("parallel", "parallel"),
    )(parts, w, m, v)


def _adamw(g, w, m, v, name):
    r, c_dim = g.shape

    def body(g_ref, w_ref, m_ref, v_ref, d_ref, nm_ref, nv_ref):
        d_ref[...], nm_ref[...], nv_ref[...] = _adamw_math(w_ref[...], g_ref[...], m_ref[...], v_ref[...])

    out = jax.ShapeDtypeStruct((r, c_dim), F32)
    return pl.pallas_call(body, name=name, out_shape=[out] * 3)(g, w, m, v)


def _pack_rows(arrays, dtype, row_unit):
    chunks, offs, r0 = [], [], 0
    for a in arrays:
        flat = a.reshape(-1).astype(dtype)
        rows = -(-flat.shape[0] // (LANES * row_unit)) * row_unit
        flat = jnp.pad(flat, (0, rows * LANES - flat.shape[0]))
        chunks.append(flat.reshape(rows, LANES))
        offs.append((r0, rows))
        r0 += rows
    return jnp.concatenate(chunks, axis=0), offs


def _unpack_rows(packed, offs, shapes):
    out = []
    for (r0, rows), shape in zip(offs, shapes):
        n = math.prod(shape)
        blk = packed[..., r0:r0 + rows, :]
        blk = blk.reshape(packed.shape[:-2] + (rows * LANES,))[..., :n]
        out.append(blk.reshape(packed.shape[:-2] + tuple(shape)))
    return out


def _full_from_slots(blk, col_sharded):
    _, r, c = blk.shape
    if col_sharded:
        return blk.transpose(1, 0, 2).reshape(r, N_DEV * c)
    return blk.reshape(N_DEV * r, c)


def _ffn_fwd(h, n, w_gu_t, w_d, tag, next_gain=None):
    gate, up, a = _ffn_up(n, w_gu_t, f"{tag}_up")
    out = _mm(a, w_d, F32, f"{tag}_down", res=h, alpha=FFN_RES_WEIGHT, norm_out=next_gain)
    h_out, n_next = out if next_gain is not None else (out, None)
    return h_out, (h, n, gate, up, a), n_next


def _ffn_bwd(dh_out, saved, g, w_gu_t, w_d, tag, dep, send_grads):
    h, n, gate, up, a = saved
    dw_d = _mm_tn(a, dh_out, f"{tag}_dw_down", alpha=FFN_RES_WEIGHT, dep=dep)
    dgate, dup = _ffn_da(dh_out, w_d, gate, up, f"{tag}_da", FFN_RES_WEIGHT)
    f = dgate.shape[1]
    dw_gu_t = _mm_tn(dgate, n, f"{tag}_dw_gate", out_rows=2 * f)
    dw_gu_t = _mm_tn(dup, n, f"{tag}_dw_up", out_rows=2 * f, row_off=f, into=dw_gu_t)
    dep = send_grads(dw_gu_t, dw_d)
    return _mm([dgate, dup], w_gu_t, F32, f"{tag}_dn", dep=dep, tk=1408, res=dh_out, norm_bwd=(h, g))


W_GROUPS = (("ffn1_w_gate_up", "ffn1_w_down"),
            ("w_in",),
            ("w_out_a", "w_out_ssm", "w_mix_out"),
            ("w_q", "w_kv", "w_o_x", "ffn2_w_gate_up", "ffn2_w_down"))
G_GROUPS = (("ffn2_w_gate_up", "ffn2_w_down"),
            ("w_o_x", "w_q", "w_kv", "w_mix_out", "w_out_a", "w_out_ssm", "w_in"),
            ("ffn1_w_gate_up", "ffn1_w_down"))


def _local_step(x3, mem3, target3, small, comm):
    nb, s, d = x3.shape
    m_len = mem3.shape[1]
    t = nb * s
    nc = s // SSM_CHUNK
    di = small["ssm_norm"].shape[1]
    hs = di // SSM_HEAD_DIM
    cc = di + 2 * SSM_GROUPS * SSM_STATE
    x, mem, target = x3.reshape(t, d), mem3.reshape(nb * m_len, d), target3.reshape(t, d)

    sizes = (d, d, d, di, cc, hs, d, d)
    offs = [0]
    for sz in sizes:
        offs.append(offs[-1] + sz)
    cb_a = _tile(d, CONV_A_BLOCK, LANES)
    xbc_col0, z_col0 = 3 * d, 3 * d + cc
    ga_blk, gb_blk = (3 * d + di + cc) // d, (4 * d + di + cc) // d

    pad_vec = lambda v: jnp.pad(v.reshape(1, -1), ((0, 0), (0, LANES - hs)))
    prow = jnp.concatenate([pad_vec(small["ssm_dt_bias"]), pad_vec(small["ssm_a_log"]), pad_vec(small["ssm_d"]),
                            jnp.zeros((5, LANES), F32)], axis=0)
    pcol = prow.T
    e_mat = (lax.broadcasted_iota(jnp.int32, (LANES, di), 0)
             == lax.broadcasted_iota(jnp.int32, (LANES, di), 1) // SSM_HEAD_DIM).astype(BF16)

    wts, dep = comm.weights(0, None)
    conv_a_w8 = jnp.pad(wts["conv_a_w"][0], ((0, 8 - CONV_A_K), (0, 0)))
    ssm_conv_w8 = jnp.pad(wts["ssm_conv_w"][0], ((0, 8 - SSM_CONV_K), (0, 0)))
    n1 = _rms_fwd(x, small["ffn1_norm"] + dep[0, 0], "ffn1_norm")
    h1, ffn1_saved, u = _ffn_fwd(x, n1, wts["ffn1_w_gate_up"], wts["ffn1_w_down"], "ffn1", small["mix_norm"])
    got, dep = comm.weights(1, h1)
    wts.update(got)
    w_in_t = wts["w_in"]
    conv_rows = [w_in_t[offs[i] + j * cb_a:offs[i] + (j + 1) * cb_a] for j in range(d // cb_a) for i in (0, 1, 2)]
    w_main_t = jnp.concatenate(conv_rows + [w_in_t[offs[i]:offs[i + 1]] for i in (4, 3, 6, 7)], axis=0)
    w_dt_t = jnp.pad(w_in_t[offs[5]:offs[6]], ((0, LANES - hs), (0, 0)))
    proj = _mm(u, w_main_t, BF16, "in_proj", nt=True, dep=dep)
    dtr = _mm(u, w_dt_t, F32, "in_proj_dt", nt=True)
    yap = _conv_a_fwd(proj, conv_a_w8, nb, s, d, "conv_a")
    got, dep = comm.weights(2, yap)
    wts.update(got)
    y_a = _mm(yap, wts["w_out_a"], BF16, "out_a", dep=dep)
    xc, conv_pre = _conv_s_fwd(proj, xbc_col0, ssm_conv_w8, small["ssm_conv_b"] + dep[0, 0], nb, s, cc, "conv_s")
    dtrt = dtr.T
    y_ssd, sprev = _ssd_fwd(xc, dtr, dtrt, prow, pcol, nb, nc, di, "ssd")
    ygn = _gate_norm_fwd(y_ssd, proj, z_col0, small["ssm_norm"], di, "gate_norm")
    y_b = _mm(ygn, wts["w_out_ssm"], BF16, "out_ssm")
    merged = _merge_fwd(y_a, y_b, proj, ga_blk, gb_blk, d, "merge")
    h2, un = _mm(merged, wts["w_mix_out"], F32, "mix_out", res=h1, norm_out=small["xattn_norm"])
    got, _ = comm.weights(3, h2)
    wts.update(got)
    mn = _rms_fwd(mem, small["mem_norm"], "mem_norm")
    q = _mm(un, wts["w_q"], BF16, "q_proj")
    kv = _mm(mn, wts["w_kv"], BF16, "kv_proj", nt=True)
    o = _xattn_fwd(q, kv, nb, s, m_len, d, "xattn")
    h3, n2 = _mm(o, wts["w_o_x"], F32, "o_proj", res=h2, norm_out=small["ffn2_norm"])
    h4, ffn2_saved, _ = _ffn_fwd(h3, n2, wts["ffn2_w_gate_up"], wts["ffn2_w_down"], "ffn2")
    loss_vec, dh4, dg_final = _loss_head(h4, small["final_norm"].reshape(1, d), target, "loss_head")

    grads = {"final_norm": dg_final.reshape(d)}
    big = {}
    dh3, grads["ffn2_norm"] = _ffn_bwd(
        dh4, ffn2_saved, small["ffn2_norm"], wts["ffn2_w_gate_up"], wts["ffn2_w_down"], "ffn2", None,
        lambda dw_gu_t, dw_d: comm.grads(0, {"ffn2_w_gate_up": dw_gu_t, "ffn2_w_down": dw_d}))
    big["w_o_x"] = _mm_tn(o, dh3, "dw_o")
    do = _mm(dh3, wts["w_o_x"], BF16, "d_o", nt=True)
    dq, dk, dv = _xattn_bwd(do, q, kv, nb, s, m_len, d, "xattn_bwd")
    big["w_q"] = _mm_tn(un, dq, "dw_q")
    big["w_kv"] = _mm_tn(dv, mn, "dw_v", out_rows=2 * d, row_off=d, into=_mm_tn(dk, mn, "dw_k", out_rows=2 * d))
    _, grads["mem_norm"] = _mm([dk, dv], wts["w_kv"], F32, "d_mn", norm_bwd=(mem, small["mem_norm"]))
    dh2, grads["xattn_norm"] = _mm(dq, wts["w_q"], F32, "d_un", nt=True, res=dh3, norm_bwd=(h2, small["xattn_norm"]))
    big["w_mix_out"] = _mm_tn(merged, dh2, "dw_mix")
    dmerged = _mm(dh2, wts["w_mix_out"], BF16, "d_merged", nt=True)
    dya, dyb, dg = _merge_bwd(dmerged, y_a, y_b, proj, ga_blk, gb_blk, d, "merge_bwd")
    big["w_out_a"] = _mm_tn(yap, dya, "dw_out_a")
    big["w_out_ssm"] = _mm_tn(ygn, dyb, "dw_out_ssm")
    dyap = _mm(dya, wts["w_out_a"], BF16, "d_yap", nt=True)
    dygn = _mm(dyb, wts["w_out_ssm"], BF16, "d_ygn", nt=True)
    dcx, dconv_a = _conv_a_bwd(dyap, proj, conv_a_w8, nb, s, d, 3 * d + cc, "conv_a_bwd")
    dy_ssd, dz, grads["ssm_norm"] = _gate_norm_bwd(dygn, y_ssd, proj, z_col0, small["ssm_norm"], di, "gate_norm_bwd")
    dxc, ddtr, ssd_sums = _ssd_bwd(dy_ssd, xc, dtr, dtrt, prow, pcol, e_mat, sprev, nb, nc, di, "ssd_bwd")
    dcx, dconv_s, grads["ssm_conv_b"] = _conv_s_bwd(dxc, conv_pre, proj, xbc_col0, ssm_conv_w8, nb, s, cc, dcx, "conv_s_bwd")
    dw_cx, dw_z, dw_g = _mm_tn(dcx, u, "dw_in_cx"), _mm_tn(dz, u, "dw_in_z"), _mm_tn(dg, u, "dw_in_g")
    dw_dt = _mm_tn(ddtr, u, "dw_in_dt")[:hs]
    du_main = _mm([dcx, dz, dg], w_main_t, F32, "d_u")
    dh1, grads["mix_norm"] = _mm(ddtr, w_dt_t, F32, "d_u_dt", res=dh2, norm_bwd=(h1, small["mix_norm"], du_main))
    bcv = [[dw_cx[(3 * j + i) * cb_a:(3 * j + i + 1) * cb_a] for j in range(d // cb_a)] for i in range(3)]
    big["w_in"] = jnp.concatenate(bcv[0] + bcv[1] + bcv[2] + [dw_z, dw_cx[3 * d:], dw_dt, dw_g], axis=0)
    dep = comm.grads(1, big)
    dx, grads["ffn1_norm"] = _ffn_bwd(
        dh1, ffn1_saved, small["ffn1_norm"], wts["ffn1_w_gate_up"], wts["ffn1_w_down"], "ffn1", dep,
        lambda dw_gu_t, dw_d: comm.grads(2, {"ffn1_w_gate_up": dw_gu_t, "ffn1_w_down": dw_d}))

    grads["conv_a_w"] = dconv_a[:CONV_A_K]
    grads["ssm_conv_w"] = dconv_s[:SSM_CONV_K]
    grads["ssm_dt_bias"] = ssd_sums[0:1, :hs]
    grads["ssm_a_log"] = ssd_sums[1:2, :hs]
    grads["ssm_d"] = ssd_sums[2:3, :hs]
    return loss_vec[0, 0], dx.reshape(nb, s, d), grads


def _step(inputs):
    w = {k: inputs[k] for k in WEIGHT_ORDER}
    mom = {k: inputs["m_" + k] for k in WEIGHT_ORDER}
    vel = {k: inputs["v_" + k] for k in WEIGHT_ORDER}
    me = 4 * lax.axis_index("x") + 2 * lax.axis_index("y") + lax.axis_index("c")

    send = {k: (w[k][0].T if k in COL_SHARDED else w[k][0]).astype(BF16) for k in BIG_WEIGHTS}

    def own_slot(land, mine):
        return lax.dynamic_update_slice(land, mine[None], (me, 0, 0))

    gathers, exchanges = {}, {}

    def weights(i, after):
        conv = {}
        if i == 0:
            packed_c, conv_offs = _pack_rows([w[k][0] for k in SMALL_SHARDED], F32, 8)
            lands = list(_all_gather([send[k] for k in W_GROUPS[0]] + [packed_c], "gather0"))
            conv_blocks = _unpack_rows(lands.pop(), conv_offs, [w[k].shape[1:] for k in SMALL_SHARDED])
            conv = {k: _full_from_slots(b, True)[None] for k, b in zip(SMALL_SHARDED, conv_blocks)}
        else:
            sent, lands = _push_wait(gathers[i], False, after, f"gather{i}_wait")
            lands = [own_slot(land, mine) for land, mine in zip(lands, sent)]
        full = {k: land.reshape(N_DEV * land.shape[1], land.shape[2]) for k, land in zip(W_GROUPS[i], lands)}
        full.update(conv)
        dep = jnp.zeros((8, LANES), F32)
        if i + 1 < len(W_GROUPS):
            gathers[i + 1] = _push_start([send[k] for k in W_GROUPS[i + 1]], False, lands[0], f"gather{i + 1}_start")
            dep = gathers[i + 1]["token"]
        return full, dep

    def slot_shape(k):
        rows, cols = send[k].shape
        return (rows * cols // LANES, LANES) if rows % BF16_SUBLANES else (rows, cols)

    def send_grads(i, by_name):
        slots = [by_name[k].reshape((N_DEV,) + slot_shape(k)) for k in G_GROUPS[i]]
        exchanges[i] = _push_start(slots, True, slots[0], f"exchange{i}_start")
        return exchanges[i]["token"]

    comm = types.SimpleNamespace(weights=weights, grads=send_grads)
    small = {k: w[k] for k in SMALL_REPLICATED}

    loss_local, grad_x, grads = _local_step(inputs["x"], inputs["mem"], inputs["loss_target"], small, comm)
    loss = lax.psum(loss_local, AXES)

    out = {}
    for i, names in enumerate(G_GROUPS):
        sent, lands = _push_wait(exchanges[i], True, grad_x, f"exchange{i}_wait")
        for k, land, slots in zip(names, lands, sent):
            parts = own_slot(land, lax.dynamic_index_in_dim(slots, me, 0, keepdims=False))
            if k in COL_SHARDED and w[k].shape[2] % LANES:
                flip = lambda a: a.transpose(0, 2, 1).reshape((1,) + slot_shape(k))
                unflip = lambda a: a.reshape((1,) + send[k].shape).transpose(0, 2, 1)
                out[k] = tuple(unflip(o) for o in _sum8_adamw(parts, flip(w[k]), flip(mom[k]), flip(vel[k]),
                                                             f"sum_adamw_{k}", tr=2048 if slot_shape(k)[1] == LANES else 128))
                continue
            if k in COL_SHARDED:
                parts = parts.transpose(0, 2, 1)
            out[k] = tuple(_sum8_adamw(parts, w[k], mom[k], vel[k], f"sum_adamw_{k}"))

    small_names = SMALL_REPLICATED + SMALL_SHARDED
    packed_g, small_offs = _pack_rows([grads[k] for k in small_names], F32, 8)
    total = _sum8(_all_gather([packed_g], "gather_small_grads")[0], "sum_small_grads")
    full_grads = _unpack_rows(total, small_offs, [grads[k].shape for k in small_names])
    mine = {}
    for k, g in zip(small_names, full_grads):
        if k in SMALL_SHARDED:
            c_loc = w[k].shape[2]
            g = lax.dynamic_slice_in_dim(g, me * c_loc, c_loc, axis=1)
        mine[k] = g.reshape(w[k].shape)
    sg, s_offs = _pack_rows([mine[k] for k in small_names], F32, 8)
    sw, _ = _pack_rows([w[k] for k in small_names], F32, 8)
    sm, _ = _pack_rows([mom[k] for k in small_names], F32, 8)
    sv, _ = _pack_rows([vel[k] for k in small_names], F32, 8)
    s_shapes = [w[k].shape for k in small_names]
    small_out = [_unpack_rows(a, s_offs, s_shapes) for a in _adamw(sg, sw, sm, sv, "adamw_small")]
    for i, k in enumerate(small_names):
        out[k] = (mine[k],) + tuple(o[i] for o in small_out)

    res = [loss, grad_x]
    for j in range(4):
        res += [out[k][j] for k in WEIGHT_ORDER]
    return tuple(res)


def kernel(x, mem, ffn1_norm, ffn1_w_gate_up, ffn1_w_down, mix_norm, w_in, conv_a_w, w_out_a, ssm_conv_w, ssm_conv_b, ssm_dt_bias, ssm_a_log, ssm_d, ssm_norm, w_out_ssm, w_mix_out, xattn_norm, mem_norm, w_q, w_kv, w_o_x, ffn2_norm, ffn2_w_gate_up, ffn2_w_down, final_norm, loss_target, m_ffn1_norm, m_ffn1_w_gate_up, m_ffn1_w_down, m_mix_norm, m_w_in, m_conv_a_w, m_w_out_a, m_ssm_conv_w, m_ssm_conv_b, m_ssm_dt_bias, m_ssm_a_log, m_ssm_d, m_ssm_norm, m_w_out_ssm, m_w_mix_out, m_xattn_norm, m_mem_norm, m_w_q, m_w_kv, m_w_o_x, m_ffn2_norm, m_ffn2_w_gate_up, m_ffn2_w_down, m_final_norm, v_ffn1_norm, v_ffn1_w_gate_up, v_ffn1_w_down, v_mix_norm, v_w_in, v_conv_a_w, v_w_out_a, v_ssm_conv_w, v_ssm_conv_b, v_ssm_dt_bias, v_ssm_a_log, v_ssm_d, v_ssm_norm, v_w_out_ssm, v_w_mix_out, v_xattn_norm, v_mem_norm, v_w_q, v_w_kv, v_w_o_x, v_ffn2_norm, v_ffn2_w_gate_up, v_ffn2_w_down, v_final_norm):
    return _step(dict(locals()))
```

```python
import functools
import math
import types

import jax
import jax.numpy as jnp
from jax import lax
from jax.experimental import pallas as pl
from jax.experimental.pallas import tpu as pltpu

F32, BF16 = jnp.float32, jnp.bfloat16
HI = lax.Precision.HIGHEST
MESH = pl.DeviceIdType.MESH
AXES = ("x", "y", "c")
N_DEV = 8

EPS = 1e-6
FFN_RES_WEIGHT = 0.5
SSM_HEAD_DIM = 64
SSM_GROUPS = 4
SSM_STATE = 128
SSM_CHUNK = 128
CONV_A_K = 3
SSM_CONV_K = 4
XATTN_HEADS = 4
ADAM_LR, ADAM_B1, ADAM_B2, ADAM_EPS, ADAM_WD, ADAM_STEP = 1e-3, 0.9, 0.999, 1e-8, 0.01, 10

LANES = 128
BF16_SUBLANES = 16
VMEM_LIMIT_BYTES = 56 * 2 ** 20
NEG_BIG = -1e30

BIG_WEIGHTS = ("ffn1_w_gate_up", "ffn1_w_down", "w_in", "w_out_a", "w_out_ssm", "w_mix_out",
               "w_q", "w_kv", "w_o_x", "ffn2_w_gate_up", "ffn2_w_down")
COL_SHARDED = ("ffn1_w_gate_up", "w_in", "w_kv", "ffn2_w_gate_up")
SMALL_REPLICATED = ("ffn1_norm", "mix_norm", "ssm_conv_b", "ssm_dt_bias", "ssm_a_log", "ssm_d", "ssm_norm",
                    "xattn_norm", "mem_norm", "ffn2_norm", "final_norm")
SMALL_SHARDED = ("conv_a_w", "ssm_conv_w")
WEIGHT_ORDER = ("ffn1_norm", "ffn1_w_gate_up", "ffn1_w_down", "mix_norm", "w_in", "conv_a_w", "w_out_a",
                "ssm_conv_w", "ssm_conv_b", "ssm_dt_bias", "ssm_a_log", "ssm_d", "ssm_norm", "w_out_ssm",
                "w_mix_out", "xattn_norm", "mem_norm", "w_q", "w_kv", "w_o_x", "ffn2_norm", "ffn2_w_gate_up",
                "ffn2_w_down", "final_norm")


def _tile(dim, pref, unit):
    best = None
    t = unit
    while t <= min(dim, pref):
        if dim % t == 0:
            best = t
        t += unit
    return best if best is not None else dim


def _params(*sem):
    return pltpu.CompilerParams(dimension_semantics=sem, vmem_limit_bytes=VMEM_LIMIT_BYTES)


def _sigmoid(x):
    return pl.reciprocal(1.0 + jnp.exp(-x), approx=True)


def _silu(x):
    return x * _sigmoid(x)


def _dsilu(x):
    s = _sigmoid(x)
    return s * (1.0 + x * (1.0 - s))


def _softplus(x):
    return jnp.maximum(x, 0.0) + jnp.log(1.0 + jnp.exp(-jnp.abs(x)))


def _dot(a, b, dims=(((1,), (0,)), ((), ())), precision=None):
    return lax.dot_general(a, b, dims, preferred_element_type=F32, precision=precision)


def _stack_rows(rows, width):
    r_idx = lax.broadcasted_iota(jnp.int32, (8, width), 0)
    acc = jnp.zeros((8, width), F32)
    for k, row in enumerate(rows):
        acc = jnp.where(r_idx == k, row, acc)
    return acc


NT = (((1,), (1,)), ((), ()))
TN = (((0,), (0,)), ((), ()))


def _mm(a, b, out_dtype, name, res=None, alpha=1.0, nt=False, dep=None, norm_out=None, norm_bwd=None,
        loss_head=None, tm=1024, tn=2048, tk=2816):
    pieces = list(a) if isinstance(a, (list, tuple)) else [a]
    m = pieces[0].shape[0]
    k = sum(p.shape[1] for p in pieces)
    n = b.shape[0] if nt else b.shape[1]
    assert (b.shape[1] if nt else b.shape[0]) == k
    tm, tn = _tile(m, tm, 8), _tile(n, tn, LANES)
    tk = _tile(math.gcd(*[p.shape[1] for p in pieces]), tk, LANES)
    nk = k // tk
    starts, s0 = [], 0
    for p in pieces:
        starts.append((s0, p.shape[1] // tk))
        s0 += p.shape[1] // tk
    n_p = len(pieces)
    assert (norm_out is not None) + (norm_bwd is not None) + (loss_head is not None) <= 1
    whole_rows = norm_out is not None or norm_bwd is not None or loss_head is not None
    assert not whole_rows or tn == n
    has_pre = norm_bwd is not None and len(norm_bwd) == 3
    has_x = norm_bwd is not None or loss_head is not None

    def body(*refs):
        a_refs, b_ref = refs[:n_p], refs[n_p]
        nxt = n_p + 1
        r_ref = refs[nxt] if res is not None else None
        nxt += (res is not None) + (dep is not None)
        g_ref = refs[nxt] if whole_rows else None
        x_ref = refs[nxt + 1] if has_x else None
        pre_ref = refs[nxt + 2] if has_pre else None
        nxt += whole_rows + has_x + has_pre
        o_ref = refs[nxt]
        o2_ref = refs[nxt + 1] if whole_rows else None
        o3_ref = refs[nxt + 2] if loss_head is not None else None
        scr = refs[nxt + 1 + whole_rows + (loss_head is not None):]
        first_rows = pl.program_id(0) == 0

        def finish(acc):
            acc = alpha * acc if alpha != 1.0 else acc
            if loss_head is not None:
                hv = r_ref[...] + acc if r_ref is not None else acc
                r = lax.rsqrt(jnp.mean(hv * hv, axis=-1, keepdims=True) + EPS)
                xh = hv * r
                err = xh * g_ref[...] - x_ref[...]
                dout = err * (1.0 / n)
                gy = dout * g_ref[...]
                o_ref[...] = r * (gy - xh * jnp.mean(gy * xh, axis=-1, keepdims=True))
                dg_part = jnp.sum(dout * xh, axis=0, keepdims=True)
                loss_part = jnp.full((1, LANES), 0.5 / n, F32) * jnp.sum(err * err)

                @pl.when(first_rows)
                def _():
                    o2_ref[...] = dg_part
                    o3_ref[...] = loss_part

                @pl.when(jnp.logical_not(first_rows))
                def _():
                    o2_ref[...] += dg_part
                    o3_ref[...] += loss_part
                return
            if norm_bwd is not None:
                if pre_ref is not None:
                    acc = acc + pre_ref[...]
                xv = x_ref[...]
                r = lax.rsqrt(jnp.mean(xv * xv, axis=-1, keepdims=True) + EPS)
                xh = xv * r
                gy = acc * g_ref[...]
                part = jnp.sum(acc * xh, axis=0, keepdims=True)
                acc = r * (gy - xh * jnp.mean(gy * xh, axis=-1, keepdims=True))

                @pl.when(first_rows)
                def _():
                    o2_ref[...] = part

                @pl.when(jnp.logical_not(first_rows))
                def _():
                    o2_ref[...] += part
            if r_ref is not None:
                acc = r_ref[...] + acc
            o_ref[...] = acc.astype(out_dtype)
            if norm_out is not None:
                r = lax.rsqrt(jnp.mean(acc * acc, axis=-1, keepdims=True) + EPS)
                o2_ref[...] = (acc * r * g_ref[...]).astype(BF16)

        def product(a_ref):
            return _dot(a_ref[...].astype(BF16), b_ref[...].astype(BF16), NT if nt else (((1,), (0,)), ((), ())))

        if nk == 1:
            finish(product(a_refs[0]))
            return
        acc_ref = scr[0]
        kk = pl.program_id(2)
        for (s, cnt), a_ref in zip(starts, a_refs):
            if s == 0:
                @pl.when(kk == 0)
                def _():
                    acc_ref[...] = product(a_ref)

                @pl.when(jnp.logical_and(kk > 0, kk < cnt))
                def _():
                    acc_ref[...] += product(a_ref)
            else:
                @pl.when(jnp.logical_and(kk >= s, kk < s + cnt))
                def _():
                    acc_ref[...] += product(a_ref)

        @pl.when(kk == nk - 1)
        def _():
            finish(acc_ref[...])

    def a_spec(s, cnt):
        return pl.BlockSpec((tm, tk), lambda i, j, kk: (i, jnp.clip(kk - s, 0, cnt - 1)))

    in_specs = [a_spec(s, cnt) for s, cnt in starts]
    in_specs.append(pl.BlockSpec((tn, tk), lambda i, j, kk: (j, kk)) if nt else pl.BlockSpec((tk, tn), lambda i, j, kk: (kk, j)))
    args = pieces + [b]
    if res is not None:
        in_specs.append(pl.BlockSpec((tm, tn), lambda i, j, kk: (i, j)))
        args.append(res)
    if dep is not None:
        in_specs.append(pl.BlockSpec((8, LANES), lambda i, j, kk: (0, 0)))
        args.append(dep)
    tile = pl.BlockSpec((tm, tn), lambda i, j, kk: (i, j))
    vec = pl.BlockSpec((1, n), lambda i, j, kk: (0, 0))
    out_specs, out_shape = tile, jax.ShapeDtypeStruct((m, n), out_dtype)
    if norm_out is not None:
        in_specs.append(vec)
        args.append(norm_out)
        out_specs, out_shape = [tile, tile], [out_shape, jax.ShapeDtypeStruct((m, n), BF16)]
    if norm_bwd is not None:
        in_specs += [vec, tile] + ([tile] if has_pre else [])
        args += [norm_bwd[1], norm_bwd[0]] + ([norm_bwd[2]] if has_pre else [])
        out_specs, out_shape = [tile, vec], [out_shape, jax.ShapeDtypeStruct((1, n), F32)]
    if loss_head is not None:
        in_specs += [vec, tile]
        args += [loss_head[0], loss_head[1]]
        out_specs = [tile, vec, pl.BlockSpec((1, LANES), lambda i, j, kk: (0, 0))]
        out_shape = [out_shape, jax.ShapeDtypeStruct((1, n), F32), jax.ShapeDtypeStruct((1, LANES), F32)]
    sums_over_rows = norm_bwd is not None or loss_head is not None
    return pl.pallas_call(
        body, name=name, grid=(m // tm, n // tn, nk), in_specs=in_specs, out_specs=out_specs, out_shape=out_shape,
        scratch_shapes=[pltpu.VMEM((tm, tn), F32)] if nk > 1 else [],
        compiler_params=(_params("arbitrary", "arbitrary", "arbitrary") if sums_over_rows
                         else _params("parallel", "parallel", "arbitrary")),
    )(*args)


def _mm_tn(x, dy, name, out_dtype=BF16, alpha=1.0, dep=None, out_rows=None, row_off=0, into=None,
           tko=1408, tn=1024, tt=2048):
    t, k = x.shape
    n = dy.shape[1]
    tko, tn, tt = _tile(k, tko, LANES), _tile(n, tn, LANES), _tile(t, tt, 8)
    nt_steps = t // tt

    def body(*refs):
        x_ref, dy_ref = refs[:2]
        o_ref, acc_ref = refs[-2:]
        part = _dot(x_ref[...].astype(BF16), dy_ref[...].astype(BF16), TN)
        step = pl.program_id(2)

        @pl.when(step == 0)
        def _():
            acc_ref[...] = part

        @pl.when(step > 0)
        def _():
            acc_ref[...] += part

        @pl.when(step == nt_steps - 1)
        def _():
            acc = acc_ref[...]
            o_ref[...] = (alpha * acc if alpha != 1.0 else acc).astype(out_dtype)

    in_specs = [pl.BlockSpec((tt, tko), lambda i, j, s: (s, i)), pl.BlockSpec((tt, tn), lambda i, j, s: (s, j))]
    args = [x, dy]
    if dep is not None:
        in_specs.append(pl.BlockSpec((8, LANES), lambda i, j, s: (0, 0)))
        args.append(dep)
    aliases = {}
    if into is not None:
        in_specs.append(pl.BlockSpec(memory_space=pl.ANY))
        args.append(into)
        aliases = {len(args) - 1: 0}
    band = row_off // tko
    assert row_off % tko == 0
    return pl.pallas_call(
        body, name=name, grid=(k // tko, n // tn, nt_steps), in_specs=in_specs,
        out_specs=pl.BlockSpec((tko, tn), lambda i, j, s: (i + band, j)),
        out_shape=jax.ShapeDtypeStruct((out_rows or k, n), out_dtype),
        scratch_shapes=[pltpu.VMEM((tko, tn), F32)], input_output_aliases=aliases,
        compiler_params=_params("parallel", "parallel", "arbitrary"),
    )(*args)


def _rms_fwd(x, g, name, tt=512):
    t, d = x.shape
    tt = _tile(t, tt, 8)

    def body(x_ref, g_ref, o_ref):
        xv = x_ref[...]
        r = lax.rsqrt(jnp.mean(xv * xv, axis=-1, keepdims=True) + EPS)
        o_ref[...] = (xv * r * g_ref[...]).astype(BF16)

    return pl.pallas_call(
        body, name=name, grid=(t // tt,),
        in_specs=[pl.BlockSpec((tt, d), lambda i: (i, 0)), pl.BlockSpec((1, d), lambda i: (0, 0))],
        out_specs=pl.BlockSpec((tt, d), lambda i: (i, 0)),
        out_shape=jax.ShapeDtypeStruct((t, d), BF16), compiler_params=_params("parallel"),
    )(x, g)


def _ffn_up(n, w_gu_t, name, tm=512, tf=2816):
    t, d = n.shape
    f = w_gu_t.shape[0] // 2
    tm, tf = _tile(t, tm, 8), _tile(f, tf, LANES)
    nf = f // tf

    def body(n_ref, wg_ref, wu_ref, g_ref, u_ref, a_ref):
        nv = n_ref[...]
        gate, up = _dot(nv, wg_ref[...], NT), _dot(nv, wu_ref[...], NT)
        s = _sigmoid(gate)
        sg = gate * s
        g_ref[...] = (up * (s * (1.0 + gate * (1.0 - s)))).astype(BF16)
        u_ref[...] = sg.astype(BF16)
        a_ref[...] = (sg * up).astype(BF16)

    blk = pl.BlockSpec((tm, tf), lambda i, j: (i, j))
    out = jax.ShapeDtypeStruct((t, f), BF16)
    return pl.pallas_call(
        body, name=name, grid=(t // tm, nf),
        in_specs=[pl.BlockSpec((tm, d), lambda i, j: (i, 0)), pl.BlockSpec((tf, d), lambda i, j: (j, 0)),
                  pl.BlockSpec((tf, d), lambda i, j: (j + nf, 0))],
        out_specs=[blk, blk, blk], out_shape=[out, out, out], compiler_params=_params("parallel", "parallel"),
    )(n, w_gu_t, w_gu_t)


def _ffn_da(dh, w_d, gate, up, name, alpha, dep=None, tm=512, tf=2816):
    t, d = dh.shape
    f = w_d.shape[0]
    tm, tf = _tile(t, tm, 8), _tile(f, tf, LANES)

    def body(*refs):
        dh_ref, w_ref, g_ref, u_ref = refs[:4]
        dg_ref, du_ref = refs[-2:]
        da = alpha * _dot(dh_ref[...].astype(BF16), w_ref[...], NT)
        dg_ref[...] = (da * g_ref[...].astype(F32)).astype(BF16)
        du_ref[...] = (da * u_ref[...].astype(F32)).astype(BF16)

    blk = pl.BlockSpec((tm, tf), lambda i, j: (i, j))
    in_specs = [pl.BlockSpec((tm, d), lambda i, j: (i, 0)), pl.BlockSpec((tf, d), lambda i, j: (j, 0)), blk, blk]
    args = [dh, w_d, gate, up]
    if dep is not None:
        in_specs.append(pl.BlockSpec((8, LANES), lambda i, j: (0, 0)))
        args.append(dep)
    out = jax.ShapeDtypeStruct((t, f), BF16)
    return pl.pallas_call(
        body, name=name, grid=(t // tm, f // tf), in_specs=in_specs, out_specs=[blk, blk], out_shape=[out, out],
        compiler_params=_params("parallel", "parallel"),
    )(*args)


def _merge_fwd(ya, yb, proj, ga_blk, gb_blk, d, name, tt=512):
    t = ya.shape[0]
    tt = _tile(t, tt, 8)

    def body(ya_ref, yb_ref, ga_ref, gb_ref, o_ref):
        o_ref[...] = (_sigmoid(ga_ref[...].astype(F32)) * ya_ref[...].astype(F32)
                      + _sigmoid(gb_ref[...].astype(F32)) * yb_ref[...].astype(F32)).astype(BF16)

    row = pl.BlockSpec((tt, d), lambda i: (i, 0))
    return pl.pallas_call(
        body, name=name, grid=(t // tt,),
        in_specs=[row, row, pl.BlockSpec((tt, d), lambda i: (i, ga_blk)), pl.BlockSpec((tt, d), lambda i: (i, gb_blk))],
        out_specs=row, out_shape=jax.ShapeDtypeStruct((t, d), BF16), compiler_params=_params("parallel"),
    )(ya, yb, proj, proj)


def _merge_bwd(dm, ya, yb, proj, ga_blk, gb_blk, d, name, tt=512):
    t = ya.shape[0]
    tt = _tile(t, tt, 8)

    def body(dm_ref, ya_ref, yb_ref, ga_ref, gb_ref, dya_ref, dyb_ref, dg_ref):
        dmv = dm_ref[...].astype(F32)
        sa, sb = _sigmoid(ga_ref[...].astype(F32)), _sigmoid(gb_ref[...].astype(F32))
        dya_ref[...] = (dmv * sa).astype(BF16)
        dyb_ref[...] = (dmv * sb).astype(BF16)
        dg_ref[:, 0:d] = (dmv * ya_ref[...].astype(F32) * sa * (1.0 - sa)).astype(BF16)
        dg_ref[:, d:2 * d] = (dmv * yb_ref[...].astype(F32) * sb * (1.0 - sb)).astype(BF16)

    row = pl.BlockSpec((tt, d), lambda i: (i, 0))
    out = jax.ShapeDtypeStruct((t, d), BF16)
    return pl.pallas_call(
        body, name=name, grid=(t // tt,),
        in_specs=[row, row, row, pl.BlockSpec((tt, d), lambda i: (i, ga_blk)), pl.BlockSpec((tt, d), lambda i: (i, gb_blk))],
        out_specs=[row, row, pl.BlockSpec((tt, 2 * d), lambda i: (i, 0))],
        out_shape=[out, out, jax.ShapeDtypeStruct((t, 2 * d), BF16)], compiler_params=_params("parallel"),
    )(dm, ya, yb, proj, proj)


def _shift_down(x, k, t_idx):
    if k == 0:
        return x
    return jnp.where(t_idx >= k, pltpu.roll(x, k, 0), 0.0)


def _shift_up(x, k, t_idx, s):
    if k == 0:
        return x
    return jnp.where(t_idx < s - k, pltpu.roll(x, s - k, 0), 0.0)


CONV_A_BLOCK = 256


def _conv_a_fwd(proj, w, nb, s, d, name):
    cb = _tile(d, CONV_A_BLOCK, LANES)

    def body(x_ref, w_ref, o_ref):
        t_idx = lax.broadcasted_iota(jnp.int32, (s, cb), 0)
        cv = x_ref[:, cb:2 * cb].astype(F32) * x_ref[:, 2 * cb:3 * cb].astype(F32)
        cc = sum(w_ref[k:k + 1, :] * _shift_down(cv, CONV_A_K - 1 - k, t_idx) for k in range(CONV_A_K))
        o_ref[...] = (x_ref[:, 0:cb].astype(F32) * cc).astype(BF16)

    return pl.pallas_call(
        body, name=name, grid=(nb, d // cb),
        in_specs=[pl.BlockSpec((s, 3 * cb), lambda b, j: (b, j)), pl.BlockSpec((8, cb), lambda b, j: (0, j))],
        out_specs=pl.BlockSpec((s, cb), lambda b, j: (b, j)),
        out_shape=jax.ShapeDtypeStruct((nb * s, d), BF16), compiler_params=_params("parallel", "parallel"),
    )(proj, w)


def _conv_a_bwd(dy, proj, w, nb, s, d, out_cols, name):
    cb = _tile(d, CONV_A_BLOCK, LANES)

    def body(dy_ref, x_ref, w_ref, o_ref, dw_ref):
        t_idx = lax.broadcasted_iota(jnp.int32, (s, cb), 0)
        cv_c, cv_v = x_ref[:, cb:2 * cb].astype(F32), x_ref[:, 2 * cb:3 * cb].astype(F32)
        cv = cv_c * cv_v
        shifted = [_shift_down(cv, CONV_A_K - 1 - k, t_idx) for k in range(CONV_A_K)]
        cc = sum(w_ref[k:k + 1, :] * shifted[k] for k in range(CONV_A_K))
        dyv = dy_ref[...].astype(F32)
        o_ref[:, 0:cb] = (dyv * cc).astype(BF16)
        dcc = dyv * x_ref[:, 0:cb].astype(F32)
        dcv = sum(w_ref[k:k + 1, :] * _shift_up(dcc, CONV_A_K - 1 - k, t_idx, s) for k in range(CONV_A_K))
        o_ref[:, cb:2 * cb] = (dcv * cv_v).astype(BF16)
        o_ref[:, 2 * cb:3 * cb] = (dcv * cv_c).astype(BF16)
        rows = [jnp.sum(dcc * shifted[k], axis=0, keepdims=True) for k in range(CONV_A_K)]
        part = _stack_rows(rows, cb)

        @pl.when(pl.program_id(1) == 0)
        def _():
            dw_ref[...] = part

        @pl.when(pl.program_id(1) > 0)
        def _():
            dw_ref[...] += part

    wspec = pl.BlockSpec((8, cb), lambda j, b: (0, j))
    wide = pl.BlockSpec((s, 3 * cb), lambda j, b: (b, j))
    return pl.pallas_call(
        body, name=name, grid=(d // cb, nb), in_specs=[pl.BlockSpec((s, cb), lambda j, b: (b, j)), wide, wspec],
        out_specs=[wide, wspec],
        out_shape=[jax.ShapeDtypeStruct((nb * s, out_cols), BF16), jax.ShapeDtypeStruct((8, d), F32)],
        compiler_params=_params("parallel", "arbitrary"),
    )(dy, proj, w)


def _conv_s_fwd(proj, col0, w, bias, nb, s, cc_width, name, cb=256):
    cb = _tile(math.gcd(cc_width, col0) if col0 else cc_width, cb, LANES)
    nd, off = cc_width // cb, col0 // cb

    def body(x_ref, w_ref, b_ref, o_ref, pre_ref):
        t_idx = lax.broadcasted_iota(jnp.int32, (s, cb), 0)
        xv = x_ref[...].astype(F32)
        pre = b_ref[...] + sum(w_ref[k:k + 1, :] * _shift_down(xv, SSM_CONV_K - 1 - k, t_idx) for k in range(SSM_CONV_K))
        o_ref[...] = _silu(pre).astype(BF16)
        pre_ref[...] = pre.astype(BF16)

    vec = pl.BlockSpec((8, cb), lambda b, j: (0, j))
    own = pl.BlockSpec((s, cb), lambda b, j: (b, j))
    out = jax.ShapeDtypeStruct((nb * s, cc_width), BF16)
    return pl.pallas_call(
        body, name=name, grid=(nb, nd),
        in_specs=[pl.BlockSpec((s, cb), lambda b, j: (b, j + off)), vec, pl.BlockSpec((1, cb), lambda b, j: (0, j))],
        out_specs=[own, own], out_shape=[out, out], compiler_params=_params("parallel", "parallel"),
    )(proj, w, bias)


def _conv_s_bwd(dxc, pre, proj, col0, w, nb, s, cc_width, into, name, cb=256):
    cb = _tile(math.gcd(cc_width, col0) if col0 else cc_width, cb, LANES)
    nd, off = cc_width // cb, col0 // cb

    def body(d_ref, pre_ref, x_ref, w_ref, into_ref, dx_ref, dw_ref, db_ref):
        t_idx = lax.broadcasted_iota(jnp.int32, (s, cb), 0)
        xv = x_ref[...].astype(F32)
        dpre = d_ref[...].astype(F32) * _dsilu(pre_ref[...].astype(F32))
        ahead = [_shift_up(dpre, j, t_idx, s) for j in range(SSM_CONV_K)]
        dx_ref[...] = sum(w_ref[k:k + 1, :] * ahead[SSM_CONV_K - 1 - k] for k in range(SSM_CONV_K)).astype(BF16)
        rows = [jnp.sum(ahead[SSM_CONV_K - 1 - k] * xv, axis=0, keepdims=True) for k in range(SSM_CONV_K)]
        dw_part = _stack_rows(rows, cb)
        db_part = jnp.sum(dpre, axis=0, keepdims=True)

        @pl.when(pl.program_id(1) == 0)
        def _():
            dw_ref[...] = dw_part
            db_ref[...] = db_part

        @pl.when(pl.program_id(1) > 0)
        def _():
            dw_ref[...] += dw_part
            db_ref[...] += db_part

    own = pl.BlockSpec((s, cb), lambda j, b: (b, j))
    shifted = pl.BlockSpec((s, cb), lambda j, b: (b, j + off))
    wspec = pl.BlockSpec((8, cb), lambda j, b: (0, j))
    bspec = pl.BlockSpec((1, cb), lambda j, b: (0, j))
    return pl.pallas_call(
        body, name=name, grid=(nd, nb),
        in_specs=[own, own, shifted, wspec, pl.BlockSpec(memory_space=pl.ANY)],
        out_specs=[shifted, wspec, bspec],
        out_shape=[jax.ShapeDtypeStruct(into.shape, into.dtype), jax.ShapeDtypeStruct((8, cc_width), F32),
                   jax.ShapeDtypeStruct((1, cc_width), F32)],
        input_output_aliases={4: 0},
        compiler_params=_params("parallel", "arbitrary"),
    )(dxc, pre, proj, w, into)


def _split3(v):
    hi = v.astype(BF16)
    r1 = v - hi.astype(F32)
    mid = r1.astype(BF16)
    return hi, mid, (r1 - mid.astype(F32)).astype(BF16)


def _exact_left(mask_b, v):
    return sum(_dot(mask_b, t) for t in _split3(v))


def _exact_right(v, mask_b):
    return sum(_dot(t, mask_b) for t in _split3(v))


def _head_sums(v, e_b):
    return _dot(v.astype(BF16), e_b, NT)


def _spread(v, out_ref, di):
    lane = lax.broadcasted_iota(jnp.int32, (v.shape[0], LANES), 1)
    for pr in range(di // LANES):
        h0 = pr * (LANES // SSM_HEAD_DIM)
        out_ref[:, pr * LANES:(pr + 1) * LANES] = jnp.where(lane < SSM_HEAD_DIM, v[:, h0:h0 + 1], v[:, h0 + 1:h0 + 2])


def _ssd_common(xc_ref, dtr_ref, dtrt_ref, prow_ref, pcol_ref, dtx_ref, acsx_ref, dx_ref, di):
    l = SSM_CHUNK
    bias_r, a_r = prow_ref[0:1, :], -jnp.exp(prow_ref[1:2, :])
    sp_in = dtr_ref[...] + bias_r
    dt = _softplus(sp_in)
    li = lax.broadcasted_iota(jnp.int32, (l, l), 0)
    si = lax.broadcasted_iota(jnp.int32, (l, l), 1)
    lower_b = (li >= si).astype(BF16)
    upper_b = (li <= si).astype(BF16)
    acs = _exact_left(lower_b, dt * a_r)
    bias_c, a_c = pcol_ref[:, 0:1], -jnp.exp(pcol_ref[:, 1:2])
    dt_t = _softplus(dtrt_ref[...] + bias_c)
    acs_t = _exact_right(dt_t * a_c, upper_b)
    _spread(dt, dtx_ref, di)
    _spread(acs, acsx_ref, di)
    _spread(prow_ref[0:8, :], dx_ref, di)
    acs_exp = acsx_ref[...]
    acs_last = acs_exp[l - 1:l, :]
    x = xc_ref[:, 0:di].astype(F32)
    return dict(dt=dt, a_r=a_r, sp_in=sp_in, acs=acs, acs_t=acs_t, dt_exp=dtx_ref[...], e_exp=jnp.exp(acs_exp),
                el_exp=jnp.exp(acs_last), f_exp=jnp.exp(acs_last - acs_exp), x=x, mask=li >= si, upper_b=upper_b,
                d_exp=dx_ref[2:3, :])


def _decay(q, h):
    seg = q["acs"][:, h:h + 1] - q["acs_t"][h:h + 1, :]
    return jnp.exp(jnp.where(q["mask"], seg, NEG_BIG))


def _ssd_fwd(xc, dtr, dtrt, prow, pcol, nb, nc, di, name):
    l, n, g_n, p = SSM_CHUNK, SSM_STATE, SSM_GROUPS, SSM_HEAD_DIM
    cc = xc.shape[1]
    gw = di // g_n
    assert p * 2 == LANES and gw % LANES == 0

    def body(xc_ref, dtr_ref, dtrt_ref, prow_ref, pcol_ref, y_ref, sprev_ref, st_ref, dtx_ref, acsx_ref, dx_ref):
        @pl.when(pl.program_id(1) == 0)
        def _():
            st_ref[...] = jnp.zeros_like(st_ref)

        q = _ssd_common(xc_ref, dtr_ref, dtrt_ref, prow_ref, pcol_ref, dtx_ref, acsx_ref, dx_ref, di)
        x = q["x"]
        xd = x * q["dt_exp"]
        xdb = xd.astype(BF16)
        xdf = (xd * q["f_exp"]).astype(BF16)
        lane = lax.broadcasted_iota(jnp.int32, (l, LANES), 1)
        for g in range(g_n):
            lo = g * gw
            bg = xc_ref[:, di + g * n: di + (g + 1) * n]
            cg = xc_ref[:, di + g_n * n + g * n: di + g_n * n + (g + 1) * n]
            cb = _dot(cg, bg, NT)
            st_g = st_ref[:, lo:lo + gw]
            y_off = q["e_exp"][:, lo:lo + gw] * _dot(cg, st_g.astype(BF16))
            for pr in range(gw // LANES):
                c0 = lo + pr * LANES
                h0 = c0 // p
                xp = xdb[:, c0:c0 + LANES]
                m0 = (cb * _decay(q, h0)).astype(BF16)
                m1 = (cb * _decay(q, h0 + 1)).astype(BF16)
                yd = _dot(m0, jnp.where(lane < p, xp, 0)) + _dot(m1, jnp.where(lane >= p, xp, 0))
                y_ref[:, c0:c0 + LANES] = (yd + y_off[:, pr * LANES:(pr + 1) * LANES]
                                           + q["d_exp"][:, c0:c0 + LANES] * x[:, c0:c0 + LANES]).astype(BF16)
            sprev_ref[:, lo:lo + gw] = st_g
            st_ref[:, lo:lo + gw] = q["el_exp"][:, lo:lo + gw] * st_g + _dot(bg, xdf[:, lo:lo + gw], TN)

    tok = lambda w: pl.BlockSpec((l, w), lambda b, c: (b * nc + c, 0))
    const = lambda r, w: pl.BlockSpec((r, w), lambda b, c: (0, 0))
    return pl.pallas_call(
        body, name=name, grid=(nb, nc),
        in_specs=[tok(cc), tok(LANES), pl.BlockSpec((LANES, l), lambda b, c: (0, b * nc + c)),
                  const(8, LANES), const(LANES, 8)],
        out_specs=[tok(di), pl.BlockSpec((None, n, di), lambda b, c: (b * nc + c, 0, 0))],
        out_shape=[jax.ShapeDtypeStruct((nb * nc * l, di), BF16), jax.ShapeDtypeStruct((nb * nc, n, di), F32)],
        scratch_shapes=[pltpu.VMEM((n, di), F32), pltpu.VMEM((l, di), F32), pltpu.VMEM((l, di), F32),
                        pltpu.VMEM((8, di), F32)],
        compiler_params=_params("parallel", "arbitrary"),
    )(xc, dtr, dtrt, prow, pcol)


def _ssd_bwd(dy, xc, dtr, dtrt, prow, pcol, e_mat, sprev, nb, nc, di, name):
    l, n, g_n, p = SSM_CHUNK, SSM_STATE, SSM_GROUPS, SSM_HEAD_DIM
    cc = xc.shape[1]
    gw = di // g_n

    def body(dy_ref, xc_ref, dtr_ref, dtrt_ref, prow_ref, pcol_ref, e_ref, sprev_ref,
             dxc_ref, ddtr_ref, sums_ref, dst_ref, off_ref, dxd_ref, last_ref, vst_ref,
             dtx_ref, acsx_ref, dx_ref):
        first = jnp.logical_and(pl.program_id(0) == 0, pl.program_id(1) == 0)

        @pl.when(pl.program_id(1) == 0)
        def _():
            dst_ref[...] = jnp.zeros_like(dst_ref)

        head_row = lax.broadcasted_iota(jnp.int32, (LANES, l), 0)
        row_sums, col_sums = jnp.zeros((l, LANES), F32), jnp.zeros((LANES, l), F32)
        strict_lower = lax.broadcasted_iota(jnp.int32, (l, l), 0) > lax.broadcasted_iota(jnp.int32, (l, l), 1)

        q = _ssd_common(xc_ref, dtr_ref, dtrt_ref, prow_ref, pcol_ref, dtx_ref, acsx_ref, dx_ref, di)
        x = q["x"]
        xd = x * q["dt_exp"]
        xdb = xd.astype(BF16)
        xdf = (xd * q["f_exp"]).astype(BF16)
        dyv = dy_ref[...].astype(F32)
        dyb = dy_ref[...]
        dye = (dyv * q["e_exp"]).astype(BF16)
        upper_b = q["upper_b"]
        lane = lax.broadcasted_iota(jnp.int32, (l, LANES), 1)
        for g in range(g_n):
            lo = g * gw
            bg = xc_ref[:, di + g * n: di + (g + 1) * n]
            cg = xc_ref[:, di + g_n * n + g * n: di + g_n * n + (g + 1) * n]
            cb = _dot(cg, bg, NT)
            st_g = sprev_ref[:, lo:lo + gw]
            st_gb = st_g.astype(BF16)
            dst_g = dst_ref[:, lo:lo + gw]
            dst_gb = dst_g.astype(BF16)
            dye_g = dye[:, lo:lo + gw]
            xdf_g = xdf[:, lo:lo + gw]
            y_off = q["e_exp"][:, lo:lo + gw] * _dot(cg, st_gb)
            dc_g = _dot(dye_g, st_gb, NT)
            db_g = _dot(xdf_g, dst_gb, NT)
            dxd_state = _dot(bg, dst_gb) * q["f_exp"][:, lo:lo + gw]
            last_ref[:, lo:lo + gw] = jnp.sum(dst_g * st_g, axis=0, keepdims=True)
            dst_ref[:, lo:lo + gw] = q["el_exp"][:, lo:lo + gw] * dst_g + _dot(cg, dye_g, TN)
            off_ref[:, lo:lo + gw] = dyv[:, lo:lo + gw] * y_off
            vst_ref[:, lo:lo + gw] = xd[:, lo:lo + gw] * dxd_state
            dcb = jnp.zeros((l, l), F32)
            for pr in range(gw // LANES):
                c0 = lo + pr * LANES
                h0 = c0 // p
                xp = xdb[:, c0:c0 + LANES]
                dyp = dyb[:, c0:c0 + LANES]
                dxd_diag = jnp.zeros((l, LANES), F32)
                for k, keep in enumerate((lane < p, lane >= p)):
                    dec = _decay(q, h0 + k)
                    dy_h = jnp.where(keep, dyp, 0)
                    dm_dec = _dot(dy_h, xp, NT) * dec
                    dcb = dcb + dm_dec
                    dxd_diag = dxd_diag + _dot((cb * dec).astype(BF16), dy_h, TN)
                    qm = dm_dec * cb
                    row_sums = jnp.where(lane == h0 + k, jnp.sum(qm, axis=1, keepdims=True), row_sums)
                    col_sums = jnp.where(head_row == h0 + k, jnp.sum(qm, axis=0, keepdims=True), col_sums)
                dxd_ref[:, c0:c0 + LANES] = dxd_diag + dxd_state[:, pr * LANES:(pr + 1) * LANES]
            dcb_b = dcb.astype(BF16)
            dxc_ref[:, di + g * n: di + (g + 1) * n] = (db_g + _dot(dcb_b, cg, TN)).astype(BF16)
            dxc_ref[:, di + g_n * n + g * n: di + g_n * n + (g + 1) * n] = (dc_g + _dot(dcb_b, bg)).astype(BF16)
        dxd = dxd_ref[...]
        e_b = e_ref[...]
        from_y = _exact_left(upper_b, _head_sums(off_ref[...], e_b) + row_sums - col_sums.T)
        from_s = _exact_left(strict_lower.astype(BF16), _head_sums(vst_ref[...], e_b))
        carried = _head_sums(jnp.broadcast_to(last_ref[...], (8, di)), e_b)[0:1, :] * jnp.exp(q["acs"][l - 1:l, :])
        dla = from_y + from_s + carried
        ddt = dla * q["a_r"] + _head_sums(dxd * x, e_b)
        ddtr = ddt * jax.nn.sigmoid(q["sp_in"])
        ddtr_ref[...] = ddtr
        dxc_ref[:, 0:di] = (dxd * q["dt_exp"] + q["d_exp"] * dyv).astype(BF16)
        dd_exp = jnp.sum(dyv * x, axis=0, keepdims=True)
        dd = _head_sums(jnp.broadcast_to(dd_exp, (8, di)), e_b)[0:1, :]
        part = _stack_rows([jnp.sum(ddtr, axis=0, keepdims=True),
                            jnp.sum(dla * q["dt"], axis=0, keepdims=True) * q["a_r"], dd], LANES)

        @pl.when(first)
        def _():
            sums_ref[...] = part

        @pl.when(jnp.logical_not(first))
        def _():
            sums_ref[...] += part

    rev = lambda b, c: b * nc + (nc - 1 - c)
    tok = lambda w: pl.BlockSpec((l, w), lambda b, c: (rev(b, c), 0))
    const = lambda r, w: pl.BlockSpec((r, w), lambda b, c: (0, 0))
    return pl.pallas_call(
        body, name=name, grid=(nb, nc),
        in_specs=[tok(di), tok(cc), tok(LANES), pl.BlockSpec((LANES, l), lambda b, c: (0, rev(b, c))),
                  const(8, LANES), const(LANES, 8), const(LANES, di),
                  pl.BlockSpec((None, n, di), lambda b, c: (rev(b, c), 0, 0))],
        out_specs=[tok(cc), tok(LANES), const(8, LANES)],
        out_shape=[jax.ShapeDtypeStruct((nb * nc * l, cc), BF16), jax.ShapeDtypeStruct((nb * nc * l, LANES), F32),
                   jax.ShapeDtypeStruct((8, LANES), F32)],
        scratch_shapes=[pltpu.VMEM((n, di), F32), pltpu.VMEM((l, di), F32), pltpu.VMEM((l, di), F32),
                        pltpu.VMEM((1, di), F32), pltpu.VMEM((l, di), F32),
                        pltpu.VMEM((l, di), F32), pltpu.VMEM((l, di), F32), pltpu.VMEM((8, di), F32)],
        compiler_params=_params("arbitrary", "arbitrary"),
    )(dy, xc, dtr, dtrt, prow, pcol, e_mat, sprev)


def _gate_norm_fwd(y, proj, z_col0, norm_g, di, name, tt=256):
    t = y.shape[0]
    tt = _tile(t, tt, 8)
    gw = di // SSM_GROUPS
    zw = _tile(math.gcd(di, z_col0), di, LANES)
    nz, zoff = di // zw, z_col0 // zw

    def body(*refs):
        y_ref, z_refs, g_ref, o_ref = refs[0], refs[1:1 + nz], refs[1 + nz], refs[2 + nz]
        for g in range(SSM_GROUPS):
            lo = g * gw
            zv = z_refs[lo // zw][:, lo % zw:lo % zw + gw].astype(F32)
            yg = y_ref[:, lo:lo + gw].astype(F32) * _silu(zv)
            r = lax.rsqrt(jnp.mean(yg * yg, axis=-1, keepdims=True) + EPS)
            o_ref[:, lo:lo + gw] = (yg * r * g_ref[:, lo:lo + gw]).astype(BF16)

    row = pl.BlockSpec((tt, di), lambda i: (i, 0))
    zspecs = [pl.BlockSpec((tt, zw), functools.partial(lambda i, k: (i, zoff + k), k=k)) for k in range(nz)]
    return pl.pallas_call(
        body, name=name, grid=(t // tt,), in_specs=[row] + zspecs + [pl.BlockSpec((1, di), lambda i: (0, 0))],
        out_specs=row, out_shape=jax.ShapeDtypeStruct((t, di), BF16), compiler_params=_params("parallel"),
    )(y, *([proj] * nz), norm_g)


def _gate_norm_bwd(dn, y, proj, z_col0, norm_g, di, name, tt=256):
    t = y.shape[0]
    tt = _tile(t, tt, 8)
    gw = di // SSM_GROUPS
    zw = _tile(math.gcd(di, z_col0), di, LANES)
    nz, zoff = di // zw, z_col0 // zw

    def body(*refs):
        dn_ref, y_ref, z_refs, g_ref = refs[0], refs[1], refs[2:2 + nz], refs[2 + nz]
        dy_ref, dz_ref, dg_ref = refs[3 + nz:]
        first = pl.program_id(0) == 0
        for g in range(SSM_GROUPS):
            lo = g * gw
            zv = z_refs[lo // zw][:, lo % zw:lo % zw + gw].astype(F32)
            yv = y_ref[:, lo:lo + gw].astype(F32)
            sz = _silu(zv)
            yg = yv * sz
            r = lax.rsqrt(jnp.mean(yg * yg, axis=-1, keepdims=True) + EPS)
            yh = yg * r
            dnv = dn_ref[:, lo:lo + gw].astype(F32)
            gy = dnv * g_ref[:, lo:lo + gw]
            dyg = r * (gy - yh * jnp.mean(gy * yh, axis=-1, keepdims=True))
            dy_ref[:, lo:lo + gw] = (dyg * sz).astype(BF16)
            dz_ref[:, lo:lo + gw] = (dyg * yv * _dsilu(zv)).astype(BF16)
            part = jnp.sum(dnv * yh, axis=0, keepdims=True)

            @pl.when(first)
            def _():
                dg_ref[:, lo:lo + gw] = part

            @pl.when(jnp.logical_not(first))
            def _():
                dg_ref[:, lo:lo + gw] += part

    row = pl.BlockSpec((tt, di), lambda i: (i, 0))
    vec = pl.BlockSpec((1, di), lambda i: (0, 0))
    zspecs = [pl.BlockSpec((tt, zw), functools.partial(lambda i, k: (i, zoff + k), k=k)) for k in range(nz)]
    return pl.pallas_call(
        body, name=name, grid=(t // tt,), in_specs=[row, row] + zspecs + [vec], out_specs=[row, row, vec],
        out_shape=[jax.ShapeDtypeStruct((t, di), BF16), jax.ShapeDtypeStruct((t, di), BF16), jax.ShapeDtypeStruct((1, di), F32)],
        compiler_params=_params("arbitrary"),
    )(dn, y, *([proj] * nz), norm_g)


def _softmax_rows(s):
    s = s - jnp.max(s, axis=-1, keepdims=True)
    e = jnp.exp(s)
    return e * (1.0 / jnp.sum(e, axis=-1, keepdims=True))


def _xattn_fwd(q, kv, nb, s, m, d, name, tq=1024):
    tq = _tile(s, tq, 8)
    nq = s // tq
    hd = d // XATTN_HEADS
    scale = 1.0 / math.sqrt(hd)

    def body(q_ref, k_ref, v_ref, o_ref):
        for h in range(XATTN_HEADS):
            sl = slice(h * hd, (h + 1) * hd)
            prob = _softmax_rows(_dot(q_ref[:, sl], k_ref[:, sl], NT) * scale)
            o_ref[:, sl] = _dot(prob.astype(BF16), v_ref[:, sl]).astype(BF16)

    return pl.pallas_call(
        body, name=name, grid=(nb, nq),
        in_specs=[pl.BlockSpec((tq, d), lambda b, i: (b * nq + i, 0)), pl.BlockSpec((m, d), lambda b, i: (b, 0)),
                  pl.BlockSpec((m, d), lambda b, i: (b, 1))],
        out_specs=pl.BlockSpec((tq, d), lambda b, i: (b * nq + i, 0)),
        out_shape=jax.ShapeDtypeStruct((nb * s, d), BF16), compiler_params=_params("parallel", "parallel"),
    )(q, kv, kv)


def _xattn_bwd(do, q, kv, nb, s, m, d, name, tq=1024):
    tq = _tile(s, tq, 8)
    nq = s // tq
    hd = d // XATTN_HEADS
    scale = 1.0 / math.sqrt(hd)

    def body(do_ref, q_ref, k_ref, v_ref, dq_ref, dk_ref, dv_ref):
        first = pl.program_id(1) == 0
        for h in range(XATTN_HEADS):
            sl = slice(h * hd, (h + 1) * hd)
            qh, kh, vh, doh = q_ref[:, sl], k_ref[:, sl], v_ref[:, sl], do_ref[:, sl]
            prob = _softmax_rows(_dot(qh, kh, NT) * scale)
            dv_h = _dot(prob.astype(BF16), doh, TN)
            dp = _dot(doh, vh, NT)
            ds = (prob * (dp - jnp.sum(dp * prob, axis=-1, keepdims=True)) * scale).astype(BF16)
            dq_ref[:, sl] = _dot(ds, kh).astype(BF16)
            dk_h = _dot(ds, qh, TN)

            @pl.when(first)
            def _():
                dk_ref[:, sl] = dk_h
                dv_ref[:, sl] = dv_h

            @pl.when(jnp.logical_not(first))
            def _():
                dk_ref[:, sl] += dk_h
                dv_ref[:, sl] += dv_h

    qspec = pl.BlockSpec((tq, d), lambda b, i: (b * nq + i, 0))
    dq, dk, dv = pl.pallas_call(
        body, name=name, grid=(nb, nq),
        in_specs=[qspec, qspec, pl.BlockSpec((m, d), lambda b, i: (b, 0)), pl.BlockSpec((m, d), lambda b, i: (b, 1))],
        out_specs=[qspec, pl.BlockSpec((m, d), lambda b, i: (b, 0)), pl.BlockSpec((m, d), lambda b, i: (b, 0))],
        out_shape=[jax.ShapeDtypeStruct((nb * s, d), BF16), jax.ShapeDtypeStruct((nb * m, d), F32),
                   jax.ShapeDtypeStruct((nb * m, d), F32)],
        compiler_params=_params("parallel", "arbitrary"),
    )(do, q, kv, kv)
    return dq, dk, dv


def _all_gather(shards, name):
    n_arr = len(shards)

    def body(*refs):
        x_refs, out_refs = refs[:n_arr], refs[n_arr:2 * n_arr]
        send_sems, recv_sems, local_sems = refs[2 * n_arr:]
        x, y, c = lax.axis_index("x"), lax.axis_index("y"), lax.axis_index("c")
        me, sibling = (x, y, c), (x, y, 1 - c)
        chips = [(1 - x, y), (x, 1 - y), (1 - x, 1 - y)]

        def copy(w, k, block, to, from_input=False):
            px, py, pc = block
            rows = out_refs[w].at[4 * px + 2 * py + pc]
            return pltpu.make_async_remote_copy(
                src_ref=x_refs[w] if from_input else rows, dst_ref=rows,
                send_sem=send_sems.at[7 * w + k], recv_sem=recv_sems.at[7 * w + k], device_id=to, device_id_type=MESH)

        started = []
        for w in range(n_arr):
            mine = pltpu.make_async_copy(x_refs[w], out_refs[w].at[4 * x + 2 * y + c], local_sems.at[w])
            mine.start()
            started.append(mine)
        sends = []
        for w in range(n_arr):
            sends.append(copy(w, 0, me, sibling, from_input=True))
            sends += [copy(w, 1 + j, me, (*chip, c), from_input=True) for j, chip in enumerate(chips)]
        for cp in sends:
            cp.start()
        for j, chip in enumerate(chips):
            for w in range(n_arr):
                copy(w, 1 + j, (*chip, c), me).wait_recv()
                passed = copy(w, 4 + j, (*chip, c), sibling)
                passed.start()
                sends.append(passed)
        for w in range(n_arr):
            copy(w, 0, sibling, me).wait_recv()
            for j, chip in enumerate(chips):
                copy(w, 4 + j, (*chip, 1 - c), me).wait_recv()
        for cp in sends:
            cp.wait_send()
        for mine in started:
            mine.wait()

    hbm = pl.BlockSpec(memory_space=pl.ANY)
    return pl.pallas_call(
        body, name=name, out_shape=[jax.ShapeDtypeStruct((N_DEV,) + s.shape, s.dtype) for s in shards],
        in_specs=[hbm] * n_arr, out_specs=[hbm] * n_arr,
        scratch_shapes=[pltpu.SemaphoreType.DMA((7 * n_arr,)), pltpu.SemaphoreType.DMA((7 * n_arr,)),
                        pltpu.SemaphoreType.DMA((n_arr,))],
    )(*shards)


_HBM = pl.BlockSpec(memory_space=pltpu.HBM)
_SEM = pl.BlockSpec(memory_space=pltpu.SEMAPHORE)
_DATAFLOW = pltpu.SideEffectType.DATAFLOW_SIDE_EFFECTING


def _peer_list(x, y, c):
    return [(1 - x if k & 4 else x, 1 - y if k & 2 else y, 1 - c if k & 1 else c) for k in range(1, N_DEV)]


def _push_copy(src_ref, land_ref, send_sems, recv_sems, w, k, peer, me, per_peer_src, receiving):
    px, py, pc = peer
    peer_slot = 4 * px + 2 * py + pc
    return pltpu.make_async_remote_copy(
        src_ref=src_ref.at[peer_slot] if per_peer_src else src_ref,
        dst_ref=land_ref.at[peer_slot if receiving else me],
        send_sem=send_sems.at[7 * w + k], recv_sem=recv_sems.at[7 * w + k], device_id=peer, device_id_type=MESH)


def _push_start(srcs, per_peer_src, after, name):
    n_arr = len(srcs)
    land_shapes = [s.shape if per_peer_src else (N_DEV,) + s.shape for s in srcs]

    def body(*refs):
        src_refs, land_refs = refs[:n_arr], refs[n_arr:2 * n_arr]
        send_sems, recv_sems = refs[2 * n_arr + 1], refs[2 * n_arr + 2]
        token = refs[-1]
        x, y, c = lax.axis_index("x"), lax.axis_index("y"), lax.axis_index("c")
        me = 4 * x + 2 * y + c
        for w in range(n_arr):
            for k, peer in enumerate(_peer_list(x, y, c)):
                _push_copy(src_refs[w], land_refs[w], send_sems, recv_sems, w, k, peer, me, per_peer_src, False).start()
        token[...] = jnp.zeros_like(token)

    lands = [pltpu.with_memory_space_constraint(lax.empty(ls, s.dtype), pltpu.HBM) for ls, s in zip(land_shapes, srcs)]
    srcs_hbm = [pltpu.with_memory_space_constraint(s, pltpu.HBM) for s in srcs]
    out = pl.pallas_call(
        body, name=name,
        out_shape=(pltpu.SemaphoreType.DMA((7 * n_arr,)), pltpu.SemaphoreType.DMA((7 * n_arr,)),
                   *[pltpu.HBM(s.shape, s.dtype) for s in srcs], *[pltpu.HBM(ls, s.dtype) for ls, s in zip(land_shapes, srcs)],
                   jax.ShapeDtypeStruct((8, LANES), F32)),
        in_specs=[_HBM] * (2 * n_arr) + [pl.BlockSpec(memory_space=pl.ANY)],
        out_specs=(_SEM, _SEM, *([_HBM] * (2 * n_arr)), pl.BlockSpec(memory_space=pltpu.VMEM)),
        input_output_aliases={i: 2 + i for i in range(2 * n_arr)},
        compiler_params=pltpu.CompilerParams(has_side_effects=_DATAFLOW),
    )(*srcs_hbm, *lands, after)
    return dict(send=out[0], recv=out[1], srcs=list(out[2:2 + n_arr]), lands=list(out[2 + n_arr:2 + 2 * n_arr]),
                token=out[-1])


def _push_wait(pending, per_peer_src, after, name):
    n_arr = len(pending["srcs"])

    def body(*refs):
        src_refs, land_refs = refs[:n_arr], refs[n_arr:2 * n_arr]
        send_sems, recv_sems = refs[2 * n_arr], refs[2 * n_arr + 1]
        x, y, c = lax.axis_index("x"), lax.axis_index("y"), lax.axis_index("c")
        me = 4 * x + 2 * y + c
        for w in range(n_arr):
            for k, peer in enumerate(_peer_list(x, y, c)):
                cp = _push_copy(src_refs[w], land_refs[w], send_sems, recv_sems, w, k, peer, me, per_peer_src, True)
                cp.wait_send()
                cp.wait_recv()

    out = pl.pallas_call(
        body, name=name,
        out_shape=tuple(pltpu.HBM(a.shape, a.dtype) for a in pending["srcs"] + pending["lands"]),
        in_specs=[_HBM] * (2 * n_arr) + [_SEM, _SEM, pl.BlockSpec(memory_space=pl.ANY)],
        out_specs=tuple([_HBM] * (2 * n_arr)),
        input_output_aliases={i: i for i in range(2 * n_arr)},
        compiler_params=pltpu.CompilerParams(has_side_effects=_DATAFLOW),
    )(*pending["srcs"], *pending["lands"], pending["send"], pending["recv"], after)
    return list(out[:n_arr]), list(out[n_arr:])


def _adamw_math(w, g, m, v):
    m = ADAM_B1 * m + (1.0 - ADAM_B1) * g
    v = ADAM_B2 * v + (1.0 - ADAM_B2) * (g * g)
    m_hat = m / (1.0 - ADAM_B1 ** ADAM_STEP)
    v_hat = v / (1.0 - ADAM_B2 ** ADAM_STEP)
    delta = -ADAM_LR * (m_hat / (jnp.sqrt(v_hat) + ADAM_EPS) + ADAM_WD * w)
    return delta, m, v


def _sum8(parts, name, tr=512):
    _, r, c_dim = parts.shape
    tr = _tile(r, tr, BF16_SUBLANES)

    def body(p_ref, o_ref):
        acc = p_ref[0].astype(F32)
        for k in range(1, N_DEV):
            acc = acc + p_ref[k].astype(F32)
        o_ref[...] = acc

    return pl.pallas_call(
        body, name=name, grid=(r // tr,), in_specs=[pl.BlockSpec((N_DEV, tr, c_dim), lambda i: (0, i, 0))],
        out_specs=pl.BlockSpec((tr, c_dim), lambda i: (i, 0)),
        out_shape=jax.ShapeDtypeStruct((r, c_dim), F32), compiler_params=_params("parallel"),
    )(parts)


def _sum8_adamw(parts, w, m, v, name, tr=128):
    _, r, c_dim = parts.shape
    tr = _tile(r, tr, BF16_SUBLANES)
    tc = c_dim if tr <= 2 * LANES else _tile(c_dim, LANES, LANES)

    def body(p_ref, w_ref, m_ref, v_ref, g_ref, d_ref, nm_ref, nv_ref):
        g = p_ref[0].astype(F32)
        for k in range(1, N_DEV):
            g = g + p_ref[k].astype(F32)
        g_ref[...] = g
        d_ref[...], nm_ref[...], nv_ref[...] = _adamw_math(w_ref[...], g, m_ref[...], v_ref[...])

    blk = pl.BlockSpec((None, tr, tc), lambda i, j: (0, i, j))
    out = jax.ShapeDtypeStruct((1, r, c_dim), F32)
    return pl.pallas_call(
        body, name=name, grid=(r // tr, c_dim // tc),
        in_specs=[pl.BlockSpec((N_DEV, tr, tc), lambda i, j: (0, i, j)), blk, blk, blk],
        out_specs=[blk] * 4, out_shape=[out] * 4, compiler_params=_params("parallel", "parallel"),
    )(parts, w, m, v)


def _adamw(g, w, m, v, name):
    r, c_dim = g.shape

    def body(g_ref, w_ref, m_ref, v_ref, d_ref, nm_ref, nv_ref):
        d_ref[...], nm_ref[...], nv_ref[...] = _adamw_math(w_ref[...], g_ref[...], m_ref[...], v_ref[...])

    out = jax.ShapeDtypeStruct((r, c_dim), F32)
    return pl.pallas_call(body, name=name, out_shape=[out] * 3)(g, w, m, v)


def _pack_rows(arrays, dtype, row_unit):
    chunks, offs, r0 = [], [], 0
    for a in arrays:
        flat = a.reshape(-1).astype(dtype)
        rows = -(-flat.shape[0] // (LANES * row_unit)) * row_unit
        flat = jnp.pad(flat, (0, rows * LANES - flat.shape[0]))
        chunks.append(flat.reshape(rows, LANES))
        offs.append((r0, rows))
        r0 += rows
    return jnp.concatenate(chunks, axis=0), offs


def _unpack_rows(packed, offs, shapes):
    out = []
    for (r0, rows), shape in zip(offs, shapes):
        n = math.prod(shape)
        blk = packed[..., r0:r0 + rows, :]
        blk = blk.reshape(packed.shape[:-2] + (rows * LANES,))[..., :n]
        out.append(blk.reshape(packed.shape[:-2] + tuple(shape)))
    return out


def _full_from_slots(blk, col_sharded):
    _, r, c = blk.shape
    if col_sharded:
        return blk.transpose(1, 0, 2).reshape(r, N_DEV * c)
    return blk.reshape(N_DEV * r, c)


def _ffn_fwd(h, n, w_gu_t, w_d, tag, **tail):
    gate, up, a = _ffn_up(n, w_gu_t, f"{tag}_up")
    return _mm(a, w_d, F32, f"{tag}_down", res=h, alpha=FFN_RES_WEIGHT, **tail), (h, n, gate, up, a)


def _ffn_bwd(dh_out, saved, g, w_gu_t, w_d, tag, dep, send_grads):
    h, n, gate, up, a = saved
    dw_d = _mm_tn(a, dh_out, f"{tag}_dw_down", alpha=FFN_RES_WEIGHT, dep=dep)
    dgate, dup = _ffn_da(dh_out, w_d, gate, up, f"{tag}_da", FFN_RES_WEIGHT)
    f = dgate.shape[1]
    dw_gu_t = _mm_tn(dgate, n, f"{tag}_dw_gate", out_rows=2 * f)
    dw_gu_t = _mm_tn(dup, n, f"{tag}_dw_up", out_rows=2 * f, row_off=f, into=dw_gu_t)
    dep = send_grads(dw_gu_t, dw_d)
    return _mm([dgate, dup], w_gu_t, F32, f"{tag}_dn", dep=dep, tk=1408, res=dh_out, norm_bwd=(h, g))


W_GROUPS = (("ffn1_w_gate_up", "ffn1_w_down"),
            ("w_in",),
            ("w_out_a", "w_out_ssm", "w_mix_out"),
            ("w_q", "w_kv", "w_o_x", "ffn2_w_gate_up", "ffn2_w_down"))
G_GROUPS = (("ffn2_w_gate_up", "ffn2_w_down"),
            ("w_o_x", "w_q", "w_kv", "w_mix_out", "w_out_a", "w_out_ssm", "w_in"),
            ("ffn1_w_gate_up", "ffn1_w_down"))


def _local_step(x3, mem3, target3, small, comm):
    nb, s, d = x3.shape
    m_len = mem3.shape[1]
    t = nb * s
    nc = s // SSM_CHUNK
    di = small["ssm_norm"].shape[1]
    hs = di // SSM_HEAD_DIM
    cc = di + 2 * SSM_GROUPS * SSM_STATE
    x, mem, target = x3.reshape(t, d), mem3.reshape(nb * m_len, d), target3.reshape(t, d)

    sizes = (d, d, d, di, cc, hs, d, d)
    offs = [0]
    for sz in sizes:
        offs.append(offs[-1] + sz)
    cb_a = _tile(d, CONV_A_BLOCK, LANES)
    xbc_col0, z_col0 = 3 * d, 3 * d + cc
    ga_blk, gb_blk = (3 * d + di + cc) // d, (4 * d + di + cc) // d

    pad_vec = lambda v: jnp.pad(v.reshape(1, -1), ((0, 0), (0, LANES - hs)))
    prow = jnp.concatenate([pad_vec(small["ssm_dt_bias"]), pad_vec(small["ssm_a_log"]), pad_vec(small["ssm_d"]),
                            jnp.zeros((5, LANES), F32)], axis=0)
    pcol = prow.T
    e_mat = (lax.broadcasted_iota(jnp.int32, (LANES, di), 0)
             == lax.broadcasted_iota(jnp.int32, (LANES, di), 1) // SSM_HEAD_DIM).astype(BF16)

    wts, dep = comm.weights(0, None)
    conv_a_w8 = jnp.pad(wts["conv_a_w"][0], ((0, 8 - CONV_A_K), (0, 0)))
    ssm_conv_w8 = jnp.pad(wts["ssm_conv_w"][0], ((0, 8 - SSM_CONV_K), (0, 0)))
    n1 = _rms_fwd(x, small["ffn1_norm"] + dep[0, 0], "ffn1_norm")
    (h1, u), ffn1_saved = _ffn_fwd(x, n1, wts["ffn1_w_gate_up"], wts["ffn1_w_down"], "ffn1", norm_out=small["mix_norm"])
    got, dep = comm.weights(1, h1)
    wts.update(got)
    w_in_t = wts["w_in"]
    conv_rows = [w_in_t[offs[i] + j * cb_a:offs[i] + (j + 1) * cb_a] for j in range(d // cb_a) for i in (0, 1, 2)]
    w_main_t = jnp.concatenate(conv_rows + [w_in_t[offs[i]:offs[i + 1]] for i in (4, 3, 6, 7)], axis=0)
    w_dt_t = jnp.pad(w_in_t[offs[5]:offs[6]], ((0, LANES - hs), (0, 0)))
    proj = _mm(u, w_main_t, BF16, "in_proj", nt=True, dep=dep)
    dtr = _mm(u, w_dt_t, F32, "in_proj_dt", nt=True)
    yap = _conv_a_fwd(proj, conv_a_w8, nb, s, d, "conv_a")
    got, dep = comm.weights(2, yap)
    wts.update(got)
    y_a = _mm(yap, wts["w_out_a"], BF16, "out_a", dep=dep)
    xc, conv_pre = _conv_s_fwd(proj, xbc_col0, ssm_conv_w8, small["ssm_conv_b"] + dep[0, 0], nb, s, cc, "conv_s")
    dtrt = dtr.T
    y_ssd, sprev = _ssd_fwd(xc, dtr, dtrt, prow, pcol, nb, nc, di, "ssd")
    ygn = _gate_norm_fwd(y_ssd, proj, z_col0, small["ssm_norm"], di, "gate_norm")
    y_b = _mm(ygn, wts["w_out_ssm"], BF16, "out_ssm")
    merged = _merge_fwd(y_a, y_b, proj, ga_blk, gb_blk, d, "merge")
    h2, un = _mm(merged, wts["w_mix_out"], F32, "mix_out", res=h1, norm_out=small["xattn_norm"])
    got, _ = comm.weights(3, h2)
    wts.update(got)
    mn = _rms_fwd(mem, small["mem_norm"], "mem_norm")
    q = _mm(un, wts["w_q"], BF16, "q_proj")
    kv = _mm(mn, wts["w_kv"], BF16, "kv_proj", nt=True)
    o = _xattn_fwd(q, kv, nb, s, m_len, d, "xattn")
    h3, n2 = _mm(o, wts["w_o_x"], F32, "o_proj", res=h2, norm_out=small["ffn2_norm"])
    (dh4, dg_final, loss_vec), ffn2_saved = _ffn_fwd(h3, n2, wts["ffn2_w_gate_up"], wts["ffn2_w_down"], "ffn2",
                                                     loss_head=(small["final_norm"].reshape(1, d), target))

    grads = {"final_norm": dg_final.reshape(d)}
    big = {}
    dh3, grads["ffn2_norm"] = _ffn_bwd(
        dh4, ffn2_saved, small["ffn2_norm"], wts["ffn2_w_gate_up"], wts["ffn2_w_down"], "ffn2", None,
        lambda dw_gu_t, dw_d: comm.grads(0, {"ffn2_w_gate_up": dw_gu_t, "ffn2_w_down": dw_d}))
    big["w_o_x"] = _mm_tn(o, dh3, "dw_o")
    do = _mm(dh3, wts["w_o_x"], BF16, "d_o", nt=True)
    dq, dk, dv = _xattn_bwd(do, q, kv, nb, s, m_len, d, "xattn_bwd")
    big["w_q"] = _mm_tn(un, dq, "dw_q")
    big["w_kv"] = _mm_tn(dv, mn, "dw_v", out_rows=2 * d, row_off=d, into=_mm_tn(dk, mn, "dw_k", out_rows=2 * d))
    _, grads["mem_norm"] = _mm([dk, dv], wts["w_kv"], F32, "d_mn", norm_bwd=(mem, small["mem_norm"]))
    dh2, grads["xattn_norm"] = _mm(dq, wts["w_q"], F32, "d_un", nt=True, res=dh3, norm_bwd=(h2, small["xattn_norm"]))
    big["w_mix_out"] = _mm_tn(merged, dh2, "dw_mix")
    dmerged = _mm(dh2, wts["w_mix_out"], BF16, "d_merged", nt=True)
    dya, dyb, dg = _merge_bwd(dmerged, y_a, y_b, proj, ga_blk, gb_blk, d, "merge_bwd")
    big["w_out_a"] = _mm_tn(yap, dya, "dw_out_a")
    big["w_out_ssm"] = _mm_tn(ygn, dyb, "dw_out_ssm")
    dyap = _mm(dya, wts["w_out_a"], BF16, "d_yap", nt=True)
    dygn = _mm(dyb, wts["w_out_ssm"], BF16, "d_ygn", nt=True)
    dcx, dconv_a = _conv_a_bwd(dyap, proj, conv_a_w8, nb, s, d, 3 * d + cc, "conv_a_bwd")
    dy_ssd, dz, grads["ssm_norm"] = _gate_norm_bwd(dygn, y_ssd, proj, z_col0, small["ssm_norm"], di, "gate_norm_bwd")
    dxc, ddtr, ssd_sums = _ssd_bwd(dy_ssd, xc, dtr, dtrt, prow, pcol, e_mat, sprev, nb, nc, di, "ssd_bwd")
    dcx, dconv_s, grads["ssm_conv_b"] = _conv_s_bwd(dxc, conv_pre, proj, xbc_col0, ssm_conv_w8, nb, s, cc, dcx, "conv_s_bwd")
    dw_cx, dw_z, dw_g = _mm_tn(dcx, u, "dw_in_cx"), _mm_tn(dz, u, "dw_in_z"), _mm_tn(dg, u, "dw_in_g")
    dw_dt = _mm_tn(ddtr, u, "dw_in_dt")[:hs]
    du_main = _mm([dcx, dz, dg], w_main_t, F32, "d_u")
    dh1, grads["mix_norm"] = _mm(ddtr, w_dt_t, F32, "d_u_dt", res=dh2, norm_bwd=(h1, small["mix_norm"], du_main))
    bcv = [[dw_cx[(3 * j + i) * cb_a:(3 * j + i + 1) * cb_a] for j in range(d // cb_a)] for i in range(3)]
    big["w_in"] = jnp.concatenate(bcv[0] + bcv[1] + bcv[2] + [dw_z, dw_cx[3 * d:], dw_dt, dw_g], axis=0)
    dep = comm.grads(1, big)
    dx, grads["ffn1_norm"] = _ffn_bwd(
        dh1, ffn1_saved, small["ffn1_norm"], wts["ffn1_w_gate_up"], wts["ffn1_w_down"], "ffn1", dep,
        lambda dw_gu_t, dw_d: comm.grads(2, {"ffn1_w_gate_up": dw_gu_t, "ffn1_w_down": dw_d}))

    grads["conv_a_w"] = dconv_a[:CONV_A_K]
    grads["ssm_conv_w"] = dconv_s[:SSM_CONV_K]
    grads["ssm_dt_bias"] = ssd_sums[0:1, :hs]
    grads["ssm_a_log"] = ssd_sums[1:2, :hs]
    grads["ssm_d"] = ssd_sums[2:3, :hs]
    return loss_vec[0, 0], dx.reshape(nb, s, d), grads


def _step(inputs):
    w = {k: inputs[k] for k in WEIGHT_ORDER}
    mom = {k: inputs["m_" + k] for k in WEIGHT_ORDER}
    vel = {k: inputs["v_" + k] for k in WEIGHT_ORDER}
    me = 4 * lax.axis_index("x") + 2 * lax.axis_index("y") + lax.axis_index("c")

    send = {k: (w[k][0].T if k in COL_SHARDED else w[k][0]).astype(BF16) for k in BIG_WEIGHTS}

    def own_slot(land, mine):
        return lax.dynamic_update_slice(land, mine[None], (me, 0, 0))

    gathers, exchanges = {}, {}

    def weights(i, after):
        conv = {}
        if i == 0:
            packed_c, conv_offs = _pack_rows([w[k][0] for k in SMALL_SHARDED], F32, 8)
            lands = list(_all_gather([send[k] for k in W_GROUPS[0]] + [packed_c], "gather0"))
            conv_blocks = _unpack_rows(lands.pop(), conv_offs, [w[k].shape[1:] for k in SMALL_SHARDED])
            conv = {k: _full_from_slots(b, True)[None] for k, b in zip(SMALL_SHARDED, conv_blocks)}
        else:
            sent, lands = _push_wait(gathers[i], False, after, f"gather{i}_wait")
            lands = [own_slot(land, mine) for land, mine in zip(lands, sent)]
        full = {k: land.reshape(N_DEV * land.shape[1], land.shape[2]) for k, land in zip(W_GROUPS[i], lands)}
        full.update(conv)
        dep = jnp.zeros((8, LANES), F32)
        if i + 1 < len(W_GROUPS):
            gathers[i + 1] = _push_start([send[k] for k in W_GROUPS[i + 1]], False, lands[0], f"gather{i + 1}_start")
            dep = gathers[i + 1]["token"]
        return full, dep

    def slot_shape(k):
        rows, cols = send[k].shape
        return (rows * cols // LANES, LANES) if rows % BF16_SUBLANES else (rows, cols)

    def send_grads(i, by_name):
        slots = [by_name[k].reshape((N_DEV,) + slot_shape(k)) for k in G_GROUPS[i]]
        exchanges[i] = _push_start(slots, True, slots[0], f"exchange{i}_start")
        return exchanges[i]["token"]

    comm = types.SimpleNamespace(weights=weights, grads=send_grads)
    small = {k: w[k] for k in SMALL_REPLICATED}

    loss_local, grad_x, grads = _local_step(inputs["x"], inputs["mem"], inputs["loss_target"], small, comm)
    loss = lax.psum(loss_local, AXES)

    out = {}
    for i, names in enumerate(G_GROUPS):
        sent, lands = _push_wait(exchanges[i], True, grad_x, f"exchange{i}_wait")
        for k, land, slots in zip(names, lands, sent):
            parts = own_slot(land, lax.dynamic_index_in_dim(slots, me, 0, keepdims=False))
            if k in COL_SHARDED and w[k].shape[2] % LANES:
                flip = lambda a: a.transpose(0, 2, 1).reshape((1,) + slot_shape(k))
                unflip = lambda a: a.reshape((1,) + send[k].shape).transpose(0, 2, 1)
                out[k] = tuple(unflip(o) for o in _sum8_adamw(parts, flip(w[k]), flip(mom[k]), flip(vel[k]),
                                                             f"sum_adamw_{k}", tr=2048 if slot_shape(k)[1] == LANES else 128))
                continue
            if k in COL_SHARDED:
                parts = parts.transpose(0, 2, 1)
            out[k] = tuple(_sum8_adamw(parts, w[k], mom[k], vel[k], f"sum_adamw_{k}"))

    small_names = SMALL_REPLICATED + SMALL_SHARDED
    packed_g, small_offs = _pack_rows([grads[k] for k in small_names], F32, 8)
    total = _sum8(_all_gather([packed_g], "gather_small_grads")[0], "sum_small_grads")
    full_grads = _unpack_rows(total, small_offs, [grads[k].shape for k in small_names])
    mine = {}
    for k, g in zip(small_names, full_grads):
        if k in SMALL_SHARDED:
            c_loc = w[k].shape[2]
            g = lax.dynamic_slice_in_dim(g, me * c_loc, c_loc, axis=1)
        mine[k] = g.reshape(w[k].shape)
    sg, s_offs = _pack_rows([mine[k] for k in small_names], F32, 8)
    sw, _ = _pack_rows([w[k] for k in small_names], F32, 8)
    sm, _ = _pack_rows([mom[k] for k in small_names], F32, 8)
    sv, _ = _pack_rows([vel[k] for k in small_names], F32, 8)
    s_shapes = [w[k].shape for k in small_names]
    small_out = [_unpack_rows(a, s_offs, s_shapes) for a in _adamw(sg, sw, sm, sv, "adamw_small")]
    for i, k in enumerate(small_names):
        out[k] = (mine[k],) + tuple(o[i] for o in small_out)

    res = [loss, grad_x]
    for j in range(4):
        res += [out[k][j] for k in WEIGHT_ORDER]
    return tuple(res)


def kernel(x, mem, ffn1_norm, ffn1_w_gate_up, ffn1_w_down, mix_norm, w_in, conv_a_w, w_out_a, ssm_conv_w, ssm_conv_b, ssm_dt_bias, ssm_a_log, ssm_d, ssm_norm, w_out_ssm, w_mix_out, xattn_norm, mem_norm, w_q, w_kv, w_o_x, ffn2_norm, ffn2_w_gate_up, ffn2_w_down, final_norm, loss_target, m_ffn1_norm, m_ffn1_w_gate_up, m_ffn1_w_down, m_mix_norm, m_w_in, m_conv_a_w, m_w_out_a, m_ssm_conv_w, m_ssm_conv_b, m_ssm_dt_bias, m_ssm_a_log, m_ssm_d, m_ssm_norm, m_w_out_ssm, m_w_mix_out, m_xattn_norm, m_mem_norm, m_w_q, m_w_kv, m_w_o_x, m_ffn2_norm, m_ffn2_w_gate_up, m_ffn2_w_down, m_final_norm, v_ffn1_norm, v_ffn1_w_gate_up, v_ffn1_w_down, v_mix_norm, v_w_in, v_conv_a_w, v_w_out_a, v_ssm_conv_w, v_ssm_conv_b, v_ssm_dt_bias, v_ssm_a_log, v_ssm_d, v_ssm_norm, v_w_out_ssm, v_w_mix_out, v_xattn_norm, v_mem_norm, v_w_q, v_w_kv, v_w_o_x, v_ffn2_norm, v_ffn2_w_gate_up, v_ffn2_w_down, v_final_norm):
    return _step(dict(locals()))
```

```python
import functools
import math
import types

import jax
import jax.numpy as jnp
from jax import lax
from jax.experimental import pallas as pl
from jax.experimental.pallas import tpu as pltpu

F32, BF16 = jnp.float32, jnp.bfloat16
HI = lax.Precision.HIGHEST
MESH = pl.DeviceIdType.MESH
AXES = ("x", "y", "c")
N_DEV = 8

EPS = 1e-6
FFN_RES_WEIGHT = 0.5
SSM_HEAD_DIM = 64
SSM_GROUPS = 4
SSM_STATE = 128
SSM_CHUNK = 128
CONV_A_K = 3
SSM_CONV_K = 4
XATTN_HEADS = 4
ADAM_LR, ADAM_B1, ADAM_B2, ADAM_EPS, ADAM_WD, ADAM_STEP = 1e-3, 0.9, 0.999, 1e-8, 0.01, 10

LANES = 128
BF16_SUBLANES = 16
VMEM_LIMIT_BYTES = 56 * 2 ** 20
NEG_BIG = -1e30

BIG_WEIGHTS = ("ffn1_w_gate_up", "ffn1_w_down", "w_in", "w_out_a", "w_out_ssm", "w_mix_out",
               "w_q", "w_kv", "w_o_x", "ffn2_w_gate_up", "ffn2_w_down")
COL_SHARDED = ("ffn1_w_gate_up", "w_in", "w_kv", "ffn2_w_gate_up")
SMALL_REPLICATED = ("ffn1_norm", "mix_norm", "ssm_conv_b", "ssm_dt_bias", "ssm_a_log", "ssm_d", "ssm_norm",
                    "xattn_norm", "mem_norm", "ffn2_norm", "final_norm")
SMALL_SHARDED = ("conv_a_w", "ssm_conv_w")
WEIGHT_ORDER = ("ffn1_norm", "ffn1_w_gate_up", "ffn1_w_down", "mix_norm", "w_in", "conv_a_w", "w_out_a",
                "ssm_conv_w", "ssm_conv_b", "ssm_dt_bias", "ssm_a_log", "ssm_d", "ssm_norm", "w_out_ssm",
                "w_mix_out", "xattn_norm", "mem_norm", "w_q", "w_kv", "w_o_x", "ffn2_norm", "ffn2_w_gate_up",
                "ffn2_w_down", "final_norm")


def _tile(dim, pref, unit):
    best = None
    t = unit
    while t <= min(dim, pref):
        if dim % t == 0:
            best = t
        t += unit
    return best if best is not None else dim


def _params(*sem):
    return pltpu.CompilerParams(dimension_semantics=sem, vmem_limit_bytes=VMEM_LIMIT_BYTES)


def _sigmoid(x):
    return pl.reciprocal(1.0 + jnp.exp(-x), approx=True)


def _silu(x):
    return x * _sigmoid(x)


def _dsilu(x):
    s = _sigmoid(x)
    return s * (1.0 + x * (1.0 - s))


def _softplus(x):
    return jnp.maximum(x, 0.0) + jnp.log(1.0 + jnp.exp(-jnp.abs(x)))


def _dot(a, b, dims=(((1,), (0,)), ((), ())), precision=None):
    return lax.dot_general(a, b, dims, preferred_element_type=F32, precision=precision)


def _stack_rows(rows, width):
    r_idx = lax.broadcasted_iota(jnp.int32, (8, width), 0)
    acc = jnp.zeros((8, width), F32)
    for k, row in enumerate(rows):
        acc = jnp.where(r_idx == k, row, acc)
    return acc


NT = (((1,), (1,)), ((), ()))
TN = (((0,), (0,)), ((), ()))


def _mm(a, b, out_dtype, name, res=None, alpha=1.0, nt=False, dep=None, norm_out=None, norm_bwd=None,
        loss_head=None, tm=1024, tn=2048, tk=2816):
    pieces = list(a) if isinstance(a, (list, tuple)) else [a]
    m = pieces[0].shape[0]
    k = sum(p.shape[1] for p in pieces)
    n = b.shape[0] if nt else b.shape[1]
    assert (b.shape[1] if nt else b.shape[0]) == k
    tm, tn = _tile(m, tm, 8), _tile(n, tn, LANES)
    tk = _tile(math.gcd(*[p.shape[1] for p in pieces]), tk, LANES)
    nk = k // tk
    starts, s0 = [], 0
    for p in pieces:
        starts.append((s0, p.shape[1] // tk))
        s0 += p.shape[1] // tk
    n_p = len(pieces)
    assert (norm_out is not None) + (norm_bwd is not None) + (loss_head is not None) <= 1
    whole_rows = norm_out is not None or norm_bwd is not None or loss_head is not None
    assert not whole_rows or tn == n
    has_pre = norm_bwd is not None and len(norm_bwd) == 3
    has_x = norm_bwd is not None or loss_head is not None

    def body(*refs):
        a_refs, b_ref = refs[:n_p], refs[n_p]
        nxt = n_p + 1
        r_ref = refs[nxt] if res is not None else None
        nxt += (res is not None) + (dep is not None)
        g_ref = refs[nxt] if whole_rows else None
        x_ref = refs[nxt + 1] if has_x else None
        pre_ref = refs[nxt + 2] if has_pre else None
        nxt += whole_rows + has_x + has_pre
        o_ref = refs[nxt]
        o2_ref = refs[nxt + 1] if whole_rows else None
        o3_ref = refs[nxt + 2] if loss_head is not None else None
        scr = refs[nxt + 1 + whole_rows + (loss_head is not None):]
        first_rows = pl.program_id(0) == 0

        def finish(acc):
            acc = alpha * acc if alpha != 1.0 else acc
            if loss_head is not None:
                hv = r_ref[...] + acc if r_ref is not None else acc
                r = lax.rsqrt(jnp.mean(hv * hv, axis=-1, keepdims=True) + EPS)
                xh = hv * r
                err = xh * g_ref[...] - x_ref[...]
                dout = err * (1.0 / n)
                gy = dout * g_ref[...]
                o_ref[...] = r * (gy - xh * jnp.mean(gy * xh, axis=-1, keepdims=True))
                dg_part = jnp.sum(dout * xh, axis=0, keepdims=True)
                loss_part = jnp.full((1, LANES), 0.5 / n, F32) * jnp.sum(err * err)

                @pl.when(first_rows)
                def _():
                    o2_ref[...] = dg_part
                    o3_ref[...] = loss_part

                @pl.when(jnp.logical_not(first_rows))
                def _():
                    o2_ref[...] += dg_part
                    o3_ref[...] += loss_part
                return
            if norm_bwd is not None:
                if pre_ref is not None:
                    acc = acc + pre_ref[...]
                xv = x_ref[...]
                r = lax.rsqrt(jnp.mean(xv * xv, axis=-1, keepdims=True) + EPS)
                xh = xv * r
                gy = acc * g_ref[...]
                part = jnp.sum(acc * xh, axis=0, keepdims=True)
                acc = r * (gy - xh * jnp.mean(gy * xh, axis=-1, keepdims=True))

                @pl.when(first_rows)
                def _():
                    o2_ref[...] = part

                @pl.when(jnp.logical_not(first_rows))
                def _():
                    o2_ref[...] += part
            if r_ref is not None:
                acc = r_ref[...] + acc
            o_ref[...] = acc.astype(out_dtype)
            if norm_out is not None:
                r = lax.rsqrt(jnp.mean(acc * acc, axis=-1, keepdims=True) + EPS)
                o2_ref[...] = (acc * r * g_ref[...]).astype(BF16)

        def product(a_ref):
            return _dot(a_ref[...].astype(BF16), b_ref[...].astype(BF16), NT if nt else (((1,), (0,)), ((), ())))

        if nk == 1:
            finish(product(a_refs[0]))
            return
        acc_ref = scr[0]
        kk = pl.program_id(2)
        for (s, cnt), a_ref in zip(starts, a_refs):
            if s == 0:
                @pl.when(kk == 0)
                def _():
                    acc_ref[...] = product(a_ref)

                @pl.when(jnp.logical_and(kk > 0, kk < cnt))
                def _():
                    acc_ref[...] += product(a_ref)
            else:
                @pl.when(jnp.logical_and(kk >= s, kk < s + cnt))
                def _():
                    acc_ref[...] += product(a_ref)

        @pl.when(kk == nk - 1)
        def _():
            finish(acc_ref[...])

    def a_spec(s, cnt):
        return pl.BlockSpec((tm, tk), lambda i, j, kk: (i, jnp.clip(kk - s, 0, cnt - 1)))

    in_specs = [a_spec(s, cnt) for s, cnt in starts]
    in_specs.append(pl.BlockSpec((tn, tk), lambda i, j, kk: (j, kk)) if nt else pl.BlockSpec((tk, tn), lambda i, j, kk: (kk, j)))
    args = pieces + [b]
    if res is not None:
        in_specs.append(pl.BlockSpec((tm, tn), lambda i, j, kk: (i, j)))
        args.append(res)
    if dep is not None:
        in_specs.append(pl.BlockSpec((8, LANES), lambda i, j, kk: (0, 0)))
        args.append(dep)
    tile = pl.BlockSpec((tm, tn), lambda i, j, kk: (i, j))
    vec = pl.BlockSpec((1, n), lambda i, j, kk: (0, 0))
    out_specs, out_shape = tile, jax.ShapeDtypeStruct((m, n), out_dtype)
    if norm_out is not None:
        in_specs.append(vec)
        args.append(norm_out)
        out_specs, out_shape = [tile, tile], [out_shape, jax.ShapeDtypeStruct((m, n), BF16)]
    if norm_bwd is not None:
        in_specs += [vec, tile] + ([tile] if has_pre else [])
        args += [norm_bwd[1], norm_bwd[0]] + ([norm_bwd[2]] if has_pre else [])
        out_specs, out_shape = [tile, vec], [out_shape, jax.ShapeDtypeStruct((1, n), F32)]
    if loss_head is not None:
        in_specs += [vec, tile]
        args += [loss_head[0], loss_head[1]]
        out_specs = [tile, vec, pl.BlockSpec((1, LANES), lambda i, j, kk: (0, 0))]
        out_shape = [out_shape, jax.ShapeDtypeStruct((1, n), F32), jax.ShapeDtypeStruct((1, LANES), F32)]
    sums_over_rows = norm_bwd is not None or loss_head is not None
    return pl.pallas_call(
        body, name=name, grid=(m // tm, n // tn, nk), in_specs=in_specs, out_specs=out_specs, out_shape=out_shape,
        scratch_shapes=[pltpu.VMEM((tm, tn), F32)] if nk > 1 else [],
        compiler_params=(_params("arbitrary", "arbitrary", "arbitrary") if sums_over_rows
                         else _params("parallel", "parallel", "arbitrary")),
    )(*args)


def _mm_tn(x, dy, name, out_dtype=BF16, alpha=1.0, dep=None, out_rows=None, row_off=0, into=None,
           tko=1408, tn=1024, tt=2048):
    t, k = x.shape
    n = dy.shape[1]
    tko, tn, tt = _tile(k, tko, LANES), _tile(n, tn, LANES), _tile(t, tt, 8)
    nt_steps = t // tt

    def body(*refs):
        x_ref, dy_ref = refs[:2]
        o_ref, acc_ref = refs[-2:]
        part = _dot(x_ref[...].astype(BF16), dy_ref[...].astype(BF16), TN)
        step = pl.program_id(2)

        @pl.when(step == 0)
        def _():
            acc_ref[...] = part

        @pl.when(step > 0)
        def _():
            acc_ref[...] += part

        @pl.when(step == nt_steps - 1)
        def _():
            acc = acc_ref[...]
            o_ref[...] = (alpha * acc if alpha != 1.0 else acc).astype(out_dtype)

    in_specs = [pl.BlockSpec((tt, tko), lambda i, j, s: (s, i)), pl.BlockSpec((tt, tn), lambda i, j, s: (s, j))]
    args = [x, dy]
    if dep is not None:
        in_specs.append(pl.BlockSpec((8, LANES), lambda i, j, s: (0, 0)))
        args.append(dep)
    aliases = {}
    if into is not None:
        in_specs.append(pl.BlockSpec(memory_space=pl.ANY))
        args.append(into)
        aliases = {len(args) - 1: 0}
    band = row_off // tko
    assert row_off % tko == 0
    return pl.pallas_call(
        body, name=name, grid=(k // tko, n // tn, nt_steps), in_specs=in_specs,
        out_specs=pl.BlockSpec((tko, tn), lambda i, j, s: (i + band, j)),
        out_shape=jax.ShapeDtypeStruct((out_rows or k, n), out_dtype),
        scratch_shapes=[pltpu.VMEM((tko, tn), F32)], input_output_aliases=aliases,
        compiler_params=_params("parallel", "parallel", "arbitrary"),
    )(*args)


def _rms_fwd(x, g, name, tt=512):
    t, d = x.shape
    tt = _tile(t, tt, 8)

    def body(x_ref, g_ref, o_ref):
        xv = x_ref[...]
        r = lax.rsqrt(jnp.mean(xv * xv, axis=-1, keepdims=True) + EPS)
        o_ref[...] = (xv * r * g_ref[...]).astype(BF16)

    return pl.pallas_call(
        body, name=name, grid=(t // tt,),
        in_specs=[pl.BlockSpec((tt, d), lambda i: (i, 0)), pl.BlockSpec((1, d), lambda i: (0, 0))],
        out_specs=pl.BlockSpec((tt, d), lambda i: (i, 0)),
        out_shape=jax.ShapeDtypeStruct((t, d), BF16), compiler_params=_params("parallel"),
    )(x, g)


def _ffn_up(n, w_gu_t, name, tm=512, tf=2816):
    t, d = n.shape
    f = w_gu_t.shape[0] // 2
    tm, tf = _tile(t, tm, 8), _tile(f, tf, LANES)
    nf = f // tf

    def body(n_ref, wg_ref, wu_ref, g_ref, u_ref, a_ref):
        nv = n_ref[...]
        gate, up = _dot(nv, wg_ref[...], NT), _dot(nv, wu_ref[...], NT)
        s = _sigmoid(gate)
        sg = gate * s
        g_ref[...] = (up * (s * (1.0 + gate * (1.0 - s)))).astype(BF16)
        u_ref[...] = sg.astype(BF16)
        a_ref[...] = (sg * up).astype(BF16)

    blk = pl.BlockSpec((tm, tf), lambda i, j: (i, j))
    out = jax.ShapeDtypeStruct((t, f), BF16)
    return pl.pallas_call(
        body, name=name, grid=(t // tm, nf),
        in_specs=[pl.BlockSpec((tm, d), lambda i, j: (i, 0)), pl.BlockSpec((tf, d), lambda i, j: (j, 0)),
                  pl.BlockSpec((tf, d), lambda i, j: (j + nf, 0))],
        out_specs=[blk, blk, blk], out_shape=[out, out, out], compiler_params=_params("parallel", "parallel"),
    )(n, w_gu_t, w_gu_t)


def _ffn_da(dh, w_d, gate, up, name, alpha, dep=None, tm=512, tf=2816):
    t, d = dh.shape
    f = w_d.shape[0]
    tm, tf = _tile(t, tm, 8), _tile(f, tf, LANES)

    def body(*refs):
        dh_ref, w_ref, g_ref, u_ref = refs[:4]
        dg_ref, du_ref = refs[-2:]
        da = alpha * _dot(dh_ref[...].astype(BF16), w_ref[...], NT)
        dg_ref[...] = (da * g_ref[...].astype(F32)).astype(BF16)
        du_ref[...] = (da * u_ref[...].astype(F32)).astype(BF16)

    blk = pl.BlockSpec((tm, tf), lambda i, j: (i, j))
    in_specs = [pl.BlockSpec((tm, d), lambda i, j: (i, 0)), pl.BlockSpec((tf, d), lambda i, j: (j, 0)), blk, blk]
    args = [dh, w_d, gate, up]
    if dep is not None:
        in_specs.append(pl.BlockSpec((8, LANES), lambda i, j: (0, 0)))
        args.append(dep)
    out = jax.ShapeDtypeStruct((t, f), BF16)
    return pl.pallas_call(
        body, name=name, grid=(t // tm, f // tf), in_specs=in_specs, out_specs=[blk, blk], out_shape=[out, out],
        compiler_params=_params("parallel", "parallel"),
    )(*args)


def _merge_mix_out(ya, yb, proj, ga_blk, gb_blk, w_mix, h, gain, name, tt=512):
    t, d = ya.shape
    tt = _tile(t, tt, 8)

    def body(ya_ref, yb_ref, ga_ref, gb_ref, w_ref, h_ref, g_ref, m_ref, ho_ref, n_ref):
        merged = (_sigmoid(ga_ref[...].astype(F32)) * ya_ref[...].astype(F32)
                  + _sigmoid(gb_ref[...].astype(F32)) * yb_ref[...].astype(F32)).astype(BF16)
        m_ref[...] = merged
        hv = h_ref[...] + _dot(merged, w_ref[...])
        ho_ref[...] = hv
        r = lax.rsqrt(jnp.mean(hv * hv, axis=-1, keepdims=True) + EPS)
        n_ref[...] = (hv * r * g_ref[...]).astype(BF16)

    row = pl.BlockSpec((tt, d), lambda i: (i, 0))
    half = jax.ShapeDtypeStruct((t, d), BF16)
    return pl.pallas_call(
        body, name=name, grid=(t // tt,),
        in_specs=[row, row, pl.BlockSpec((tt, d), lambda i: (i, ga_blk)), pl.BlockSpec((tt, d), lambda i: (i, gb_blk)),
                  pl.BlockSpec((d, d), lambda i: (0, 0)), row, pl.BlockSpec((1, d), lambda i: (0, 0))],
        out_specs=[row, row, row], out_shape=[half, jax.ShapeDtypeStruct((t, d), F32), half],
        compiler_params=_params("parallel"),
    )(ya, yb, proj, proj, w_mix, h, gain)


def _merge_bwd(dm, ya, yb, proj, ga_blk, gb_blk, d, name, tt=512):
    t = ya.shape[0]
    tt = _tile(t, tt, 8)

    def body(dm_ref, ya_ref, yb_ref, ga_ref, gb_ref, dya_ref, dyb_ref, dg_ref):
        dmv = dm_ref[...].astype(F32)
        sa, sb = _sigmoid(ga_ref[...].astype(F32)), _sigmoid(gb_ref[...].astype(F32))
        dya_ref[...] = (dmv * sa).astype(BF16)
        dyb_ref[...] = (dmv * sb).astype(BF16)
        dg_ref[:, 0:d] = (dmv * ya_ref[...].astype(F32) * sa * (1.0 - sa)).astype(BF16)
        dg_ref[:, d:2 * d] = (dmv * yb_ref[...].astype(F32) * sb * (1.0 - sb)).astype(BF16)

    row = pl.BlockSpec((tt, d), lambda i: (i, 0))
    out = jax.ShapeDtypeStruct((t, d), BF16)
    return pl.pallas_call(
        body, name=name, grid=(t // tt,),
        in_specs=[row, row, row, pl.BlockSpec((tt, d), lambda i: (i, ga_blk)), pl.BlockSpec((tt, d), lambda i: (i, gb_blk))],
        out_specs=[row, row, pl.BlockSpec((tt, 2 * d), lambda i: (i, 0))],
        out_shape=[out, out, jax.ShapeDtypeStruct((t, 2 * d), BF16)], compiler_params=_params("parallel"),
    )(dm, ya, yb, proj, proj)


def _shift_down(x, k, t_idx):
    if k == 0:
        return x
    return jnp.where(t_idx >= k, pltpu.roll(x, k, 0), 0.0)


def _shift_up(x, k, t_idx, s):
    if k == 0:
        return x
    return jnp.where(t_idx < s - k, pltpu.roll(x, s - k, 0), 0.0)


CONV_A_BLOCK = 256


def _conv_a_fwd(proj, w, nb, s, d, name):
    cb = _tile(d, CONV_A_BLOCK, LANES)

    def body(x_ref, w_ref, o_ref):
        t_idx = lax.broadcasted_iota(jnp.int32, (s, cb), 0)
        cv = x_ref[:, cb:2 * cb].astype(F32) * x_ref[:, 2 * cb:3 * cb].astype(F32)
        cc = sum(w_ref[k:k + 1, :] * _shift_down(cv, CONV_A_K - 1 - k, t_idx) for k in range(CONV_A_K))
        o_ref[...] = (x_ref[:, 0:cb].astype(F32) * cc).astype(BF16)

    return pl.pallas_call(
        body, name=name, grid=(nb, d // cb),
        in_specs=[pl.BlockSpec((s, 3 * cb), lambda b, j: (b, j)), pl.BlockSpec((8, cb), lambda b, j: (0, j))],
        out_specs=pl.BlockSpec((s, cb), lambda b, j: (b, j)),
        out_shape=jax.ShapeDtypeStruct((nb * s, d), BF16), compiler_params=_params("parallel", "parallel"),
    )(proj, w)


def _conv_a_bwd(dy, proj, w, nb, s, d, out_cols, name):
    cb = _tile(d, CONV_A_BLOCK, LANES)

    def body(dy_ref, x_ref, w_ref, o_ref, dw_ref):
        t_idx = lax.broadcasted_iota(jnp.int32, (s, cb), 0)
        cv_c, cv_v = x_ref[:, cb:2 * cb].astype(F32), x_ref[:, 2 * cb:3 * cb].astype(F32)
        cv = cv_c * cv_v
        shifted = [_shift_down(cv, CONV_A_K - 1 - k, t_idx) for k in range(CONV_A_K)]
        cc = sum(w_ref[k:k + 1, :] * shifted[k] for k in range(CONV_A_K))
        dyv = dy_ref[...].astype(F32)
        o_ref[:, 0:cb] = (dyv * cc).astype(BF16)
        dcc = dyv * x_ref[:, 0:cb].astype(F32)
        dcv = sum(w_ref[k:k + 1, :] * _shift_up(dcc, CONV_A_K - 1 - k, t_idx, s) for k in range(CONV_A_K))
        o_ref[:, cb:2 * cb] = (dcv * cv_v).astype(BF16)
        o_ref[:, 2 * cb:3 * cb] = (dcv * cv_c).astype(BF16)
        rows = [jnp.sum(dcc * shifted[k], axis=0, keepdims=True) for k in range(CONV_A_K)]
        part = _stack_rows(rows, cb)

        @pl.when(pl.program_id(1) == 0)
        def _():
            dw_ref[...] = part

        @pl.when(pl.program_id(1) > 0)
        def _():
            dw_ref[...] += part

    wspec = pl.BlockSpec((8, cb), lambda j, b: (0, j))
    wide = pl.BlockSpec((s, 3 * cb), lambda j, b: (b, j))
    return pl.pallas_call(
        body, name=name, grid=(d // cb, nb), in_specs=[pl.BlockSpec((s, cb), lambda j, b: (b, j)), wide, wspec],
        out_specs=[wide, wspec],
        out_shape=[jax.ShapeDtypeStruct((nb * s, out_cols), BF16), jax.ShapeDtypeStruct((8, d), F32)],
        compiler_params=_params("parallel", "arbitrary"),
    )(dy, proj, w)


def _conv_s_fwd(proj, col0, w, bias, nb, s, cc_width, name, cb=256):
    cb = _tile(math.gcd(cc_width, col0) if col0 else cc_width, cb, LANES)
    nd, off = cc_width // cb, col0 // cb

    def body(x_ref, w_ref, b_ref, o_ref, pre_ref):
        t_idx = lax.broadcasted_iota(jnp.int32, (s, cb), 0)
        xv = x_ref[...].astype(F32)
        pre = b_ref[...] + sum(w_ref[k:k + 1, :] * _shift_down(xv, SSM_CONV_K - 1 - k, t_idx) for k in range(SSM_CONV_K))
        o_ref[...] = _silu(pre).astype(BF16)
        pre_ref[...] = pre.astype(BF16)

    vec = pl.BlockSpec((8, cb), lambda b, j: (0, j))
    own = pl.BlockSpec((s, cb), lambda b, j: (b, j))
    out = jax.ShapeDtypeStruct((nb * s, cc_width), BF16)
    return pl.pallas_call(
        body, name=name, grid=(nb, nd),
        in_specs=[pl.BlockSpec((s, cb), lambda b, j: (b, j + off)), vec, pl.BlockSpec((1, cb), lambda b, j: (0, j))],
        out_specs=[own, own], out_shape=[out, out], compiler_params=_params("parallel", "parallel"),
    )(proj, w, bias)


def _conv_s_bwd(dxc, pre, proj, col0, w, nb, s, cc_width, into, name, cb=256):
    cb = _tile(math.gcd(cc_width, col0) if col0 else cc_width, cb, LANES)
    nd, off = cc_width // cb, col0 // cb

    def body(d_ref, pre_ref, x_ref, w_ref, into_ref, dx_ref, dw_ref, db_ref):
        t_idx = lax.broadcasted_iota(jnp.int32, (s, cb), 0)
        xv = x_ref[...].astype(F32)
        dpre = d_ref[...].astype(F32) * _dsilu(pre_ref[...].astype(F32))
        ahead = [_shift_up(dpre, j, t_idx, s) for j in range(SSM_CONV_K)]
        dx_ref[...] = sum(w_ref[k:k + 1, :] * ahead[SSM_CONV_K - 1 - k] for k in range(SSM_CONV_K)).astype(BF16)
        rows = [jnp.sum(ahead[SSM_CONV_K - 1 - k] * xv, axis=0, keepdims=True) for k in range(SSM_CONV_K)]
        dw_part = _stack_rows(rows, cb)
        db_part = jnp.sum(dpre, axis=0, keepdims=True)

        @pl.when(pl.program_id(1) == 0)
        def _():
            dw_ref[...] = dw_part
            db_ref[...] = db_part

        @pl.when(pl.program_id(1) > 0)
        def _():
            dw_ref[...] += dw_part
            db_ref[...] += db_part

    own = pl.BlockSpec((s, cb), lambda j, b: (b, j))
    shifted = pl.BlockSpec((s, cb), lambda j, b: (b, j + off))
    wspec = pl.BlockSpec((8, cb), lambda j, b: (0, j))
    bspec = pl.BlockSpec((1, cb), lambda j, b: (0, j))
    return pl.pallas_call(
        body, name=name, grid=(nd, nb),
        in_specs=[own, own, shifted, wspec, pl.BlockSpec(memory_space=pl.ANY)],
        out_specs=[shifted, wspec, bspec],
        out_shape=[jax.ShapeDtypeStruct(into.shape, into.dtype), jax.ShapeDtypeStruct((8, cc_width), F32),
                   jax.ShapeDtypeStruct((1, cc_width), F32)],
        input_output_aliases={4: 0},
        compiler_params=_params("parallel", "arbitrary"),
    )(dxc, pre, proj, w, into)


def _split3(v):
    hi = v.astype(BF16)
    r1 = v - hi.astype(F32)
    mid = r1.astype(BF16)
    return hi, mid, (r1 - mid.astype(F32)).astype(BF16)


def _exact_left(mask_b, v):
    return sum(_dot(mask_b, t) for t in _split3(v))


def _exact_right(v, mask_b):
    return sum(_dot(t, mask_b) for t in _split3(v))


def _head_sums(v, e_b):
    return _dot(v.astype(BF16), e_b, NT)


def _spread(v, out_ref, di):
    lane = lax.broadcasted_iota(jnp.int32, (v.shape[0], LANES), 1)
    for pr in range(di // LANES):
        h0 = pr * (LANES // SSM_HEAD_DIM)
        out_ref[:, pr * LANES:(pr + 1) * LANES] = jnp.where(lane < SSM_HEAD_DIM, v[:, h0:h0 + 1], v[:, h0 + 1:h0 + 2])


def _ssd_common(xc_ref, dtr_ref, dtrt_ref, prow_ref, pcol_ref, dtx_ref, acsx_ref, dx_ref, di):
    l = SSM_CHUNK
    bias_r, a_r = prow_ref[0:1, :], -jnp.exp(prow_ref[1:2, :])
    sp_in = dtr_ref[...] + bias_r
    dt = _softplus(sp_in)
    li = lax.broadcasted_iota(jnp.int32, (l, l), 0)
    si = lax.broadcasted_iota(jnp.int32, (l, l), 1)
    lower_b = (li >= si).astype(BF16)
    upper_b = (li <= si).astype(BF16)
    acs = _exact_left(lower_b, dt * a_r)
    bias_c, a_c = pcol_ref[:, 0:1], -jnp.exp(pcol_ref[:, 1:2])
    dt_t = _softplus(dtrt_ref[...] + bias_c)
    acs_t = _exact_right(dt_t * a_c, upper_b)
    _spread(dt, dtx_ref, di)
    _spread(acs, acsx_ref, di)
    _spread(prow_ref[0:8, :], dx_ref, di)
    acs_exp = acsx_ref[...]
    acs_last = acs_exp[l - 1:l, :]
    x = xc_ref[:, 0:di].astype(F32)
    return dict(dt=dt, a_r=a_r, sp_in=sp_in, acs=acs, acs_t=acs_t, dt_exp=dtx_ref[...], e_exp=jnp.exp(acs_exp),
                el_exp=jnp.exp(acs_last), f_exp=jnp.exp(acs_last - acs_exp), x=x, mask=li >= si, upper_b=upper_b,
                d_exp=dx_ref[2:3, :])


def _decay(q, h):
    seg = q["acs"][:, h:h + 1] - q["acs_t"][h:h + 1, :]
    return jnp.exp(jnp.where(q["mask"], seg, NEG_BIG))


def _ssd_fwd(xc, dtr, dtrt, prow, pcol, nb, nc, di, name):
    l, n, g_n, p = SSM_CHUNK, SSM_STATE, SSM_GROUPS, SSM_HEAD_DIM
    cc = xc.shape[1]
    gw = di // g_n
    assert p * 2 == LANES and gw % LANES == 0

    def body(xc_ref, dtr_ref, dtrt_ref, prow_ref, pcol_ref, y_ref, sprev_ref, st_ref, dtx_ref, acsx_ref, dx_ref):
        @pl.when(pl.program_id(1) == 0)
        def _():
            st_ref[...] = jnp.zeros_like(st_ref)

        q = _ssd_common(xc_ref, dtr_ref, dtrt_ref, prow_ref, pcol_ref, dtx_ref, acsx_ref, dx_ref, di)
        x = q["x"]
        xd = x * q["dt_exp"]
        xdb = xd.astype(BF16)
        xdf = (xd * q["f_exp"]).astype(BF16)
        lane = lax.broadcasted_iota(jnp.int32, (l, LANES), 1)
        for g in range(g_n):
            lo = g * gw
            bg = xc_ref[:, di + g * n: di + (g + 1) * n]
            cg = xc_ref[:, di + g_n * n + g * n: di + g_n * n + (g + 1) * n]
            cb = _dot(cg, bg, NT)
            st_g = st_ref[:, lo:lo + gw]
            y_off = q["e_exp"][:, lo:lo + gw] * _dot(cg, st_g.astype(BF16))
            for pr in range(gw // LANES):
                c0 = lo + pr * LANES
                h0 = c0 // p
                xp = xdb[:, c0:c0 + LANES]
                m0 = (cb * _decay(q, h0)).astype(BF16)
                m1 = (cb * _decay(q, h0 + 1)).astype(BF16)
                yd = _dot(m0, jnp.where(lane < p, xp, 0)) + _dot(m1, jnp.where(lane >= p, xp, 0))
                y_ref[:, c0:c0 + LANES] = (yd + y_off[:, pr * LANES:(pr + 1) * LANES]
                                           + q["d_exp"][:, c0:c0 + LANES] * x[:, c0:c0 + LANES]).astype(BF16)
            sprev_ref[:, lo:lo + gw] = st_g
            st_ref[:, lo:lo + gw] = q["el_exp"][:, lo:lo + gw] * st_g + _dot(bg, xdf[:, lo:lo + gw], TN)

    tok = lambda w: pl.BlockSpec((l, w), lambda b, c: (b * nc + c, 0))
    const = lambda r, w: pl.BlockSpec((r, w), lambda b, c: (0, 0))
    return pl.pallas_call(
        body, name=name, grid=(nb, nc),
        in_specs=[tok(cc), tok(LANES), pl.BlockSpec((LANES, l), lambda b, c: (0, b * nc + c)),
                  const(8, LANES), const(LANES, 8)],
        out_specs=[tok(di), pl.BlockSpec((None, n, di), lambda b, c: (b * nc + c, 0, 0))],
        out_shape=[jax.ShapeDtypeStruct((nb * nc * l, di), BF16), jax.ShapeDtypeStruct((nb * nc, n, di), F32)],
        scratch_shapes=[pltpu.VMEM((n, di), F32), pltpu.VMEM((l, di), F32), pltpu.VMEM((l, di), F32),
                        pltpu.VMEM((8, di), F32)],
        compiler_params=_params("parallel", "arbitrary"),
    )(xc, dtr, dtrt, prow, pcol)


def _ssd_bwd(dy, xc, dtr, dtrt, prow, pcol, e_mat, sprev, nb, nc, di, name):
    l, n, g_n, p = SSM_CHUNK, SSM_STATE, SSM_GROUPS, SSM_HEAD_DIM
    cc = xc.shape[1]
    gw = di // g_n

    def body(dy_ref, xc_ref, dtr_ref, dtrt_ref, prow_ref, pcol_ref, e_ref, sprev_ref,
             dxc_ref, ddtr_ref, sums_ref, dst_ref, off_ref, dxd_ref, last_ref, vst_ref,
             dtx_ref, acsx_ref, dx_ref):
        first = jnp.logical_and(pl.program_id(0) == 0, pl.program_id(1) == 0)

        @pl.when(pl.program_id(1) == 0)
        def _():
            dst_ref[...] = jnp.zeros_like(dst_ref)

        head_row = lax.broadcasted_iota(jnp.int32, (LANES, l), 0)
        row_sums, col_sums = jnp.zeros((l, LANES), F32), jnp.zeros((LANES, l), F32)
        strict_lower = lax.broadcasted_iota(jnp.int32, (l, l), 0) > lax.broadcasted_iota(jnp.int32, (l, l), 1)

        q = _ssd_common(xc_ref, dtr_ref, dtrt_ref, prow_ref, pcol_ref, dtx_ref, acsx_ref, dx_ref, di)
        x = q["x"]
        xd = x * q["dt_exp"]
        xdb = xd.astype(BF16)
        xdf = (xd * q["f_exp"]).astype(BF16)
        dyv = dy_ref[...].astype(F32)
        dyb = dy_ref[...]
        dye = (dyv * q["e_exp"]).astype(BF16)
        upper_b = q["upper_b"]
        lane = lax.broadcasted_iota(jnp.int32, (l, LANES), 1)
        for g in range(g_n):
            lo = g * gw
            bg = xc_ref[:, di + g * n: di + (g + 1) * n]
            cg = xc_ref[:, di + g_n * n + g * n: di + g_n * n + (g + 1) * n]
            cb = _dot(cg, bg, NT)
            st_g = sprev_ref[:, lo:lo + gw]
            st_gb = st_g.astype(BF16)
            dst_g = dst_ref[:, lo:lo + gw]
            dst_gb = dst_g.astype(BF16)
            dye_g = dye[:, lo:lo + gw]
            xdf_g = xdf[:, lo:lo + gw]
            y_off = q["e_exp"][:, lo:lo + gw] * _dot(cg, st_gb)
            dc_g = _dot(dye_g, st_gb, NT)
            db_g = _dot(xdf_g, dst_gb, NT)
            dxd_state = _dot(bg, dst_gb) * q["f_exp"][:, lo:lo + gw]
            last_ref[:, lo:lo + gw] = jnp.sum(dst_g * st_g, axis=0, keepdims=True)
            dst_ref[:, lo:lo + gw] = q["el_exp"][:, lo:lo + gw] * dst_g + _dot(cg, dye_g, TN)
            off_ref[:, lo:lo + gw] = dyv[:, lo:lo + gw] * y_off
            vst_ref[:, lo:lo + gw] = xd[:, lo:lo + gw] * dxd_state
            dcb = jnp.zeros((l, l), F32)
            for pr in range(gw // LANES):
                c0 = lo + pr * LANES
                h0 = c0 // p
                xp = xdb[:, c0:c0 + LANES]
                dyp = dyb[:, c0:c0 + LANES]
                dxd_diag = jnp.zeros((l, LANES), F32)
                for k, keep in enumerate((lane < p, lane >= p)):
                    dec = _decay(q, h0 + k)
                    dy_h = jnp.where(keep, dyp, 0)
                    dm_dec = _dot(dy_h, xp, NT) * dec
                    dcb = dcb + dm_dec
                    dxd_diag = dxd_diag + _dot((cb * dec).astype(BF16), dy_h, TN)
                    qm = dm_dec * cb
                    row_sums = jnp.where(lane == h0 + k, jnp.sum(qm, axis=1, keepdims=True), row_sums)
                    col_sums = jnp.where(head_row == h0 + k, jnp.sum(qm, axis=0, keepdims=True), col_sums)
                dxd_ref[:, c0:c0 + LANES] = dxd_diag + dxd_state[:, pr * LANES:(pr + 1) * LANES]
            dcb_b = dcb.astype(BF16)
            dxc_ref[:, di + g * n: di + (g + 1) * n] = (db_g + _dot(dcb_b, cg, TN)).astype(BF16)
            dxc_ref[:, di + g_n * n + g * n: di + g_n * n + (g + 1) * n] = (dc_g + _dot(dcb_b, bg)).astype(BF16)
        dxd = dxd_ref[...]
        e_b = e_ref[...]
        from_y = _exact_left(upper_b, _head_sums(off_ref[...], e_b) + row_sums - col_sums.T)
        from_s = _exact_left(strict_lower.astype(BF16), _head_sums(vst_ref[...], e_b))
        carried = _head_sums(jnp.broadcast_to(last_ref[...], (8, di)), e_b)[0:1, :] * jnp.exp(q["acs"][l - 1:l, :])
        dla = from_y + from_s + carried
        ddt = dla * q["a_r"] + _head_sums(dxd * x, e_b)
        ddtr = ddt * jax.nn.sigmoid(q["sp_in"])
        ddtr_ref[...] = ddtr
        dxc_ref[:, 0:di] = (dxd * q["dt_exp"] + q["d_exp"] * dyv).astype(BF16)
        dd_exp = jnp.sum(dyv * x, axis=0, keepdims=True)
        dd = _head_sums(jnp.broadcast_to(dd_exp, (8, di)), e_b)[0:1, :]
        part = _stack_rows([jnp.sum(ddtr, axis=0, keepdims=True),
                            jnp.sum(dla * q["dt"], axis=0, keepdims=True) * q["a_r"], dd], LANES)

        @pl.when(first)
        def _():
            sums_ref[...] = part

        @pl.when(jnp.logical_not(first))
        def _():
            sums_ref[...] += part

    rev = lambda b, c: b * nc + (nc - 1 - c)
    tok = lambda w: pl.BlockSpec((l, w), lambda b, c: (rev(b, c), 0))
    const = lambda r, w: pl.BlockSpec((r, w), lambda b, c: (0, 0))
    return pl.pallas_call(
        body, name=name, grid=(nb, nc),
        in_specs=[tok(di), tok(cc), tok(LANES), pl.BlockSpec((LANES, l), lambda b, c: (0, rev(b, c))),
                  const(8, LANES), const(LANES, 8), const(LANES, di),
                  pl.BlockSpec((None, n, di), lambda b, c: (rev(b, c), 0, 0))],
        out_specs=[tok(cc), tok(LANES), const(8, LANES)],
        out_shape=[jax.ShapeDtypeStruct((nb * nc * l, cc), BF16), jax.ShapeDtypeStruct((nb * nc * l, LANES), F32),
                   jax.ShapeDtypeStruct((8, LANES), F32)],
        scratch_shapes=[pltpu.VMEM((n, di), F32), pltpu.VMEM((l, di), F32), pltpu.VMEM((l, di), F32),
                        pltpu.VMEM((1, di), F32), pltpu.VMEM((l, di), F32),
                        pltpu.VMEM((l, di), F32), pltpu.VMEM((l, di), F32), pltpu.VMEM((8, di), F32)],
        compiler_params=_params("arbitrary", "arbitrary"),
    )(dy, xc, dtr, dtrt, prow, pcol, e_mat, sprev)


def _gate_norm_out(y, proj, z_col0, norm_g, w_out, di, name, tt=512):
    t = y.shape[0]
    d_out = w_out.shape[1]
    tt = _tile(t, tt, 8)
    gw = di // SSM_GROUPS
    zw = _tile(math.gcd(di, z_col0), di, LANES)
    nz, zoff = di // zw, z_col0 // zw

    def body(*refs):
        y_ref, z_refs, g_ref, w_ref, o_ref, yb_ref = refs[0], refs[1:1 + nz], refs[1 + nz], refs[2 + nz], refs[3 + nz], refs[4 + nz]
        for g in range(SSM_GROUPS):
            lo = g * gw
            zv = z_refs[lo // zw][:, lo % zw:lo % zw + gw].astype(F32)
            yg = y_ref[:, lo:lo + gw].astype(F32) * _silu(zv)
            r = lax.rsqrt(jnp.mean(yg * yg, axis=-1, keepdims=True) + EPS)
            o_ref[:, lo:lo + gw] = (yg * r * g_ref[:, lo:lo + gw]).astype(BF16)
        yb_ref[...] = _dot(o_ref[...], w_ref[...]).astype(BF16)

    row = pl.BlockSpec((tt, di), lambda i: (i, 0))
    zspecs = [pl.BlockSpec((tt, zw), functools.partial(lambda i, k: (i, zoff + k), k=k)) for k in range(nz)]
    return pl.pallas_call(
        body, name=name, grid=(t // tt,),
        in_specs=[row] + zspecs + [pl.BlockSpec((1, di), lambda i: (0, 0)), pl.BlockSpec((di, d_out), lambda i: (0, 0))],
        out_specs=[row, pl.BlockSpec((tt, d_out), lambda i: (i, 0))],
        out_shape=[jax.ShapeDtypeStruct((t, di), BF16), jax.ShapeDtypeStruct((t, d_out), BF16)],
        compiler_params=_params("parallel"),
    )(y, *([proj] * nz), norm_g, w_out)


def _gate_norm_bwd(dn, y, proj, z_col0, norm_g, di, name, tt=256):
    t = y.shape[0]
    tt = _tile(t, tt, 8)
    gw = di // SSM_GROUPS
    zw = _tile(math.gcd(di, z_col0), di, LANES)
    nz, zoff = di // zw, z_col0 // zw

    def body(*refs):
        dn_ref, y_ref, z_refs, g_ref = refs[0], refs[1], refs[2:2 + nz], refs[2 + nz]
        dy_ref, dz_ref, dg_ref = refs[3 + nz:]
        first = pl.program_id(0) == 0
        for g in range(SSM_GROUPS):
            lo = g * gw
            zv = z_refs[lo // zw][:, lo % zw:lo % zw + gw].astype(F32)
            yv = y_ref[:, lo:lo + gw].astype(F32)
            sz = _silu(zv)
            yg = yv * sz
            r = lax.rsqrt(jnp.mean(yg * yg, axis=-1, keepdims=True) + EPS)
            yh = yg * r
            dnv = dn_ref[:, lo:lo + gw].astype(F32)
            gy = dnv * g_ref[:, lo:lo + gw]
            dyg = r * (gy - yh * jnp.mean(gy * yh, axis=-1, keepdims=True))
            dy_ref[:, lo:lo + gw] = (dyg * sz).astype(BF16)
            dz_ref[:, lo:lo + gw] = (dyg * yv * _dsilu(zv)).astype(BF16)
            part = jnp.sum(dnv * yh, axis=0, keepdims=True)

            @pl.when(first)
            def _():
                dg_ref[:, lo:lo + gw] = part

            @pl.when(jnp.logical_not(first))
            def _():
                dg_ref[:, lo:lo + gw] += part

    row = pl.BlockSpec((tt, di), lambda i: (i, 0))
    vec = pl.BlockSpec((1, di), lambda i: (0, 0))
    zspecs = [pl.BlockSpec((tt, zw), functools.partial(lambda i, k: (i, zoff + k), k=k)) for k in range(nz)]
    return pl.pallas_call(
        body, name=name, grid=(t // tt,), in_specs=[row, row] + zspecs + [vec], out_specs=[row, row, vec],
        out_shape=[jax.ShapeDtypeStruct((t, di), BF16), jax.ShapeDtypeStruct((t, di), BF16), jax.ShapeDtypeStruct((1, di), F32)],
        compiler_params=_params("arbitrary"),
    )(dn, y, *([proj] * nz), norm_g)


def _softmax_rows(s):
    s = s - jnp.max(s, axis=-1, keepdims=True)
    e = jnp.exp(s)
    return e * (1.0 / jnp.sum(e, axis=-1, keepdims=True))


def _xattn_fwd(q, kv, nb, s, m, d, name, tq=1024):
    tq = _tile(s, tq, 8)
    nq = s // tq
    hd = d // XATTN_HEADS
    scale = 1.0 / math.sqrt(hd)

    def body(q_ref, k_ref, v_ref, o_ref):
        for h in range(XATTN_HEADS):
            sl = slice(h * hd, (h + 1) * hd)
            prob = _softmax_rows(_dot(q_ref[:, sl], k_ref[:, sl], NT) * scale)
            o_ref[:, sl] = _dot(prob.astype(BF16), v_ref[:, sl]).astype(BF16)

    return pl.pallas_call(
        body, name=name, grid=(nb, nq),
        in_specs=[pl.BlockSpec((tq, d), lambda b, i: (b * nq + i, 0)), pl.BlockSpec((m, d), lambda b, i: (b, 0)),
                  pl.BlockSpec((m, d), lambda b, i: (b, 1))],
        out_specs=pl.BlockSpec((tq, d), lambda b, i: (b * nq + i, 0)),
        out_shape=jax.ShapeDtypeStruct((nb * s, d), BF16), compiler_params=_params("parallel", "parallel"),
    )(q, kv, kv)


def _xattn_bwd(do, q, kv, nb, s, m, d, name, tq=1024):
    tq = _tile(s, tq, 8)
    nq = s // tq
    hd = d // XATTN_HEADS
    scale = 1.0 / math.sqrt(hd)

    def body(do_ref, q_ref, k_ref, v_ref, dq_ref, dk_ref, dv_ref):
        first = pl.program_id(1) == 0
        for h in range(XATTN_HEADS):
            sl = slice(h * hd, (h + 1) * hd)
            qh, kh, vh, doh = q_ref[:, sl], k_ref[:, sl], v_ref[:, sl], do_ref[:, sl]
            prob = _softmax_rows(_dot(qh, kh, NT) * scale)
            dv_h = _dot(prob.astype(BF16), doh, TN)
            dp = _dot(doh, vh, NT)
            ds = (prob * (dp - jnp.sum(dp * prob, axis=-1, keepdims=True)) * scale).astype(BF16)
            dq_ref[:, sl] = _dot(ds, kh).astype(BF16)
            dk_h = _dot(ds, qh, TN)

            @pl.when(first)
            def _():
                dk_ref[:, sl] = dk_h
                dv_ref[:, sl] = dv_h

            @pl.when(jnp.logical_not(first))
            def _():
                dk_ref[:, sl] += dk_h
                dv_ref[:, sl] += dv_h

    qspec = pl.BlockSpec((tq, d), lambda b, i: (b * nq + i, 0))
    dq, dk, dv = pl.pallas_call(
        body, name=name, grid=(nb, nq),
        in_specs=[qspec, qspec, pl.BlockSpec((m, d), lambda b, i: (b, 0)), pl.BlockSpec((m, d), lambda b, i: (b, 1))],
        out_specs=[qspec, pl.BlockSpec((m, d), lambda b, i: (b, 0)), pl.BlockSpec((m, d), lambda b, i: (b, 0))],
        out_shape=[jax.ShapeDtypeStruct((nb * s, d), BF16), jax.ShapeDtypeStruct((nb * m, d), F32),
                   jax.ShapeDtypeStruct((nb * m, d), F32)],
        compiler_params=_params("parallel", "arbitrary"),
    )(do, q, kv, kv)
    return dq, dk, dv


def _all_gather(shards, name):
    n_arr = len(shards)

    def body(*refs):
        x_refs, out_refs = refs[:n_arr], refs[n_arr:2 * n_arr]
        send_sems, recv_sems, local_sems = refs[2 * n_arr:]
        x, y, c = lax.axis_index("x"), lax.axis_index("y"), lax.axis_index("c")
        me, sibling = (x, y, c), (x, y, 1 - c)
        chips = [(1 - x, y), (x, 1 - y), (1 - x, 1 - y)]

        def copy(w, k, block, to, from_input=False):
            px, py, pc = block
            rows = out_refs[w].at[4 * px + 2 * py + pc]
            return pltpu.make_async_remote_copy(
                src_ref=x_refs[w] if from_input else rows, dst_ref=rows,
                send_sem=send_sems.at[7 * w + k], recv_sem=recv_sems.at[7 * w + k], device_id=to, device_id_type=MESH)

        started = []
        for w in range(n_arr):
            mine = pltpu.make_async_copy(x_refs[w], out_refs[w].at[4 * x + 2 * y + c], local_sems.at[w])
            mine.start()
            started.append(mine)
        sends = []
        for w in range(n_arr):
            sends.append(copy(w, 0, me, sibling, from_input=True))
            sends += [copy(w, 1 + j, me, (*chip, c), from_input=True) for j, chip in enumerate(chips)]
        for cp in sends:
            cp.start()
        for j, chip in enumerate(chips):
            for w in range(n_arr):
                copy(w, 1 + j, (*chip, c), me).wait_recv()
                passed = copy(w, 4 + j, (*chip, c), sibling)
                passed.start()
                sends.append(passed)
        for w in range(n_arr):
            copy(w, 0, sibling, me).wait_recv()
            for j, chip in enumerate(chips):
                copy(w, 4 + j, (*chip, 1 - c), me).wait_recv()
        for cp in sends:
            cp.wait_send()
        for mine in started:
            mine.wait()

    hbm = pl.BlockSpec(memory_space=pl.ANY)
    return pl.pallas_call(
        body, name=name, out_shape=[jax.ShapeDtypeStruct((N_DEV,) + s.shape, s.dtype) for s in shards],
        in_specs=[hbm] * n_arr, out_specs=[hbm] * n_arr,
        scratch_shapes=[pltpu.SemaphoreType.DMA((7 * n_arr,)), pltpu.SemaphoreType.DMA((7 * n_arr,)),
                        pltpu.SemaphoreType.DMA((n_arr,))],
    )(*shards)


_HBM = pl.BlockSpec(memory_space=pltpu.HBM)
_SEM = pl.BlockSpec(memory_space=pltpu.SEMAPHORE)
_DATAFLOW = pltpu.SideEffectType.DATAFLOW_SIDE_EFFECTING


def _peer_list(x, y, c):
    return [(1 - x if k & 4 else x, 1 - y if k & 2 else y, 1 - c if k & 1 else c) for k in range(1, N_DEV)]


def _push_copy(src_ref, land_ref, send_sems, recv_sems, w, k, peer, me, per_peer_src, receiving):
    px, py, pc = peer
    peer_slot = 4 * px + 2 * py + pc
    return pltpu.make_async_remote_copy(
        src_ref=src_ref.at[peer_slot] if per_peer_src else src_ref,
        dst_ref=land_ref.at[peer_slot if receiving else me],
        send_sem=send_sems.at[7 * w + k], recv_sem=recv_sems.at[7 * w + k], device_id=peer, device_id_type=MESH)


def _push_start(srcs, per_peer_src, after, name):
    n_arr = len(srcs)
    land_shapes = [s.shape if per_peer_src else (N_DEV,) + s.shape for s in srcs]

    def body(*refs):
        src_refs, land_refs = refs[:n_arr], refs[n_arr:2 * n_arr]
        send_sems, recv_sems = refs[2 * n_arr + 1], refs[2 * n_arr + 2]
        token = refs[-1]
        x, y, c = lax.axis_index("x"), lax.axis_index("y"), lax.axis_index("c")
        me = 4 * x + 2 * y + c
        for w in range(n_arr):
            for k, peer in enumerate(_peer_list(x, y, c)):
                _push_copy(src_refs[w], land_refs[w], send_sems, recv_sems, w, k, peer, me, per_peer_src, False).start()
        token[...] = jnp.zeros_like(token)

    lands = [pltpu.with_memory_space_constraint(lax.empty(ls, s.dtype), pltpu.HBM) for ls, s in zip(land_shapes, srcs)]
    srcs_hbm = [pltpu.with_memory_space_constraint(s, pltpu.HBM) for s in srcs]
    out = pl.pallas_call(
        body, name=name,
        out_shape=(pltpu.SemaphoreType.DMA((7 * n_arr,)), pltpu.SemaphoreType.DMA((7 * n_arr,)),
                   *[pltpu.HBM(s.shape, s.dtype) for s in srcs], *[pltpu.HBM(ls, s.dtype) for ls, s in zip(land_shapes, srcs)],
                   jax.ShapeDtypeStruct((8, LANES), F32)),
        in_specs=[_HBM] * (2 * n_arr) + [pl.BlockSpec(memory_space=pl.ANY)],
        out_specs=(_SEM, _SEM, *([_HBM] * (2 * n_arr)), pl.BlockSpec(memory_space=pltpu.VMEM)),
        input_output_aliases={i: 2 + i for i in range(2 * n_arr)},
        compiler_params=pltpu.CompilerParams(has_side_effects=_DATAFLOW),
    )(*srcs_hbm, *lands, after)
    return dict(send=out[0], recv=out[1], srcs=list(out[2:2 + n_arr]), lands=list(out[2 + n_arr:2 + 2 * n_arr]),
                token=out[-1])


def _push_wait(pending, per_peer_src, after, name):
    n_arr = len(pending["srcs"])

    def body(*refs):
        src_refs, land_refs = refs[:n_arr], refs[n_arr:2 * n_arr]
        send_sems, recv_sems = refs[2 * n_arr], refs[2 * n_arr + 1]
        x, y, c = lax.axis_index("x"), lax.axis_index("y"), lax.axis_index("c")
        me = 4 * x + 2 * y + c
        for w in range(n_arr):
            for k, peer in enumerate(_peer_list(x, y, c)):
                cp = _push_copy(src_refs[w], land_refs[w], send_sems, recv_sems, w, k, peer, me, per_peer_src, True)
                cp.wait_send()
                cp.wait_recv()

    out = pl.pallas_call(
        body, name=name,
        out_shape=tuple(pltpu.HBM(a.shape, a.dtype) for a in pending["srcs"] + pending["lands"]),
        in_specs=[_HBM] * (2 * n_arr) + [_SEM, _SEM, pl.BlockSpec(memory_space=pl.ANY)],
        out_specs=tuple([_HBM] * (2 * n_arr)),
        input_output_aliases={i: i for i in range(2 * n_arr)},
        compiler_params=pltpu.CompilerParams(has_side_effects=_DATAFLOW),
    )(*pending["srcs"], *pending["lands"], pending["send"], pending["recv"], after)
    return list(out[:n_arr]), list(out[n_arr:])


def _adamw_math(w, g, m, v):
    m = ADAM_B1 * m + (1.0 - ADAM_B1) * g
    v = ADAM_B2 * v + (1.0 - ADAM_B2) * (g * g)
    m_hat = m / (1.0 - ADAM_B1 ** ADAM_STEP)
    v_hat = v / (1.0 - ADAM_B2 ** ADAM_STEP)
    delta = -ADAM_LR * (m_hat / (jnp.sqrt(v_hat) + ADAM_EPS) + ADAM_WD * w)
    return delta, m, v


def _sum8(parts, name, tr=512):
    _, r, c_dim = parts.shape
    tr = _tile(r, tr, BF16_SUBLANES)

    def body(p_ref, o_ref):
        acc = p_ref[0].astype(F32)
        for k in range(1, N_DEV):
            acc = acc + p_ref[k].astype(F32)
        o_ref[...] = acc

    return pl.pallas_call(
        body, name=name, grid=(r // tr,), in_specs=[pl.BlockSpec((N_DEV, tr, c_dim), lambda i: (0, i, 0))],
        out_specs=pl.BlockSpec((tr, c_dim), lambda i: (i, 0)),
        out_shape=jax.ShapeDtypeStruct((r, c_dim), F32), compiler_params=_params("parallel"),
    )(parts)


def _sum8_adamw(parts, w, m, v, name, tr=128):
    _, r, c_dim = parts.shape
    tr = _tile(r, tr, BF16_SUBLANES)
    tc = c_dim if tr <= 2 * LANES else _tile(c_dim, LANES, LANES)

    def body(p_ref, w_ref, m_ref, v_ref, g_ref, d_ref, nm_ref, nv_ref):
        g = p_ref[0].astype(F32)
        for k in range(1, N_DEV):
            g = g + p_ref[k].astype(F32)
        g_ref[...] = g
        d_ref[...], nm_ref[...], nv_ref[...] = _adamw_math(w_ref[...], g, m_ref[...], v_ref[...])

    blk = pl.BlockSpec((None, tr, tc), lambda i, j: (0, i, j))
    out = jax.ShapeDtypeStruct((1, r, c_dim), F32)
    return pl.pallas_call(
        body, name=name, grid=(r // tr, c_dim // tc),
        in_specs=[pl.BlockSpec((N_DEV, tr, tc), lambda i, j: (0, i, j)), blk, blk, blk],
        out_specs=[blk] * 4, out_shape=[out] * 4, compiler_params=_params("parallel", "parallel"),
    )(parts, w, m, v)


def _adamw(g, w, m, v, name):
    r, c_dim = g.shape

    def body(g_ref, w_ref, m_ref, v_ref, d_ref, nm_ref, nv_ref):
        d_ref[...], nm_ref[...], nv_ref[...] = _adamw_math(w_ref[...], g_ref[...], m_ref[...], v_ref[...])

    out = jax.ShapeDtypeStruct((r, c_dim), F32)
    return pl.pallas_call(body, name=name, out_shape=[out] * 3)(g, w, m, v)


def _pack_rows(arrays, dtype, row_unit):
    chunks, offs, r0 = [], [], 0
    for a in arrays:
        flat = a.reshape(-1).astype(dtype)
        rows = -(-flat.shape[0] // (LANES * row_unit)) * row_unit
        flat = jnp.pad(flat, (0, rows * LANES - flat.shape[0]))
        chunks.append(flat.reshape(rows, LANES))
        offs.append((r0, rows))
        r0 += rows
    return jnp.concatenate(chunks, axis=0), offs


def _unpack_rows(packed, offs, shapes):
    out = []
    for (r0, rows), shape in zip(offs, shapes):
        n = math.prod(shape)
        blk = packed[..., r0:r0 + rows, :]
        blk = blk.reshape(packed.shape[:-2] + (rows * LANES,))[..., :n]
        out.append(blk.reshape(packed.shape[:-2] + tuple(shape)))
    return out


def _full_from_slots(blk, col_sharded):
    _, r, c = blk.shape
    if col_sharded:
        return blk.transpose(1, 0, 2).reshape(r, N_DEV * c)
    return blk.reshape(N_DEV * r, c)


def _ffn_fwd(h, n, w_gu_t, w_d, tag, **tail):
    gate, up, a = _ffn_up(n, w_gu_t, f"{tag}_up")
    return _mm(a, w_d, F32, f"{tag}_down", res=h, alpha=FFN_RES_WEIGHT, **tail), (h, n, gate, up, a)


def _ffn_bwd(dh_out, saved, g, w_gu_t, w_d, tag, dep, send_grads):
    h, n, gate, up, a = saved
    dw_d = _mm_tn(a, dh_out, f"{tag}_dw_down", alpha=FFN_RES_WEIGHT, dep=dep)
    dgate, dup = _ffn_da(dh_out, w_d, gate, up, f"{tag}_da", FFN_RES_WEIGHT)
    f = dgate.shape[1]
    dw_gu_t = _mm_tn(dgate, n, f"{tag}_dw_gate", out_rows=2 * f)
    dw_gu_t = _mm_tn(dup, n, f"{tag}_dw_up", out_rows=2 * f, row_off=f, into=dw_gu_t)
    dep = send_grads(dw_gu_t, dw_d)
    return _mm([dgate, dup], w_gu_t, F32, f"{tag}_dn", dep=dep, tk=1408, res=dh_out, norm_bwd=(h, g))


W_GROUPS = (("ffn1_w_gate_up", "ffn1_w_down"),
            ("w_in",),
            ("w_out_a", "w_out_ssm", "w_mix_out"),
            ("w_q", "w_kv", "w_o_x", "ffn2_w_gate_up", "ffn2_w_down"))
G_GROUPS = (("ffn2_w_gate_up", "ffn2_w_down"),
            ("w_o_x", "w_q", "w_kv", "w_mix_out", "w_out_a", "w_out_ssm", "w_in"),
            ("ffn1_w_gate_up", "ffn1_w_down"))


def _local_step(x3, mem3, target3, small, comm):
    nb, s, d = x3.shape
    m_len = mem3.shape[1]
    t = nb * s
    nc = s // SSM_CHUNK
    di = small["ssm_norm"].shape[1]
    hs = di // SSM_HEAD_DIM
    cc = di + 2 * SSM_GROUPS * SSM_STATE
    x, mem, target = x3.reshape(t, d), mem3.reshape(nb * m_len, d), target3.reshape(t, d)

    sizes = (d, d, d, di, cc, hs, d, d)
    offs = [0]
    for sz in sizes:
        offs.append(offs[-1] + sz)
    cb_a = _tile(d, CONV_A_BLOCK, LANES)
    xbc_col0, z_col0 = 3 * d, 3 * d + cc
    ga_blk, gb_blk = (3 * d + di + cc) // d, (4 * d + di + cc) // d

    pad_vec = lambda v: jnp.pad(v.reshape(1, -1), ((0, 0), (0, LANES - hs)))
    prow = jnp.concatenate([pad_vec(small["ssm_dt_bias"]), pad_vec(small["ssm_a_log"]), pad_vec(small["ssm_d"]),
                            jnp.zeros((5, LANES), F32)], axis=0)
    pcol = prow.T
    e_mat = (lax.broadcasted_iota(jnp.int32, (LANES, di), 0)
             == lax.broadcasted_iota(jnp.int32, (LANES, di), 1) // SSM_HEAD_DIM).astype(BF16)

    wts, dep = comm.weights(0, None)
    conv_a_w8 = jnp.pad(wts["conv_a_w"][0], ((0, 8 - CONV_A_K), (0, 0)))
    ssm_conv_w8 = jnp.pad(wts["ssm_conv_w"][0], ((0, 8 - SSM_CONV_K), (0, 0)))
    n1 = _rms_fwd(x, small["ffn1_norm"] + dep[0, 0], "ffn1_norm")
    (h1, u), ffn1_saved = _ffn_fwd(x, n1, wts["ffn1_w_gate_up"], wts["ffn1_w_down"], "ffn1", norm_out=small["mix_norm"])
    got, dep = comm.weights(1, h1)
    wts.update(got)
    w_in_t = wts["w_in"]
    conv_rows = [w_in_t[offs[i] + j * cb_a:offs[i] + (j + 1) * cb_a] for j in range(d // cb_a) for i in (0, 1, 2)]
    w_main_t = jnp.concatenate(conv_rows + [w_in_t[offs[i]:offs[i + 1]] for i in (4, 3, 6, 7)], axis=0)
    w_dt_t = jnp.pad(w_in_t[offs[5]:offs[6]], ((0, LANES - hs), (0, 0)))
    proj = _mm(u, w_main_t, BF16, "in_proj", nt=True, dep=dep)
    dtr = _mm(u, w_dt_t, F32, "in_proj_dt", nt=True)
    yap = _conv_a_fwd(proj, conv_a_w8, nb, s, d, "conv_a")
    got, dep = comm.weights(2, yap)
    wts.update(got)
    y_a = _mm(yap, wts["w_out_a"], BF16, "out_a", dep=dep)
    xc, conv_pre = _conv_s_fwd(proj, xbc_col0, ssm_conv_w8, small["ssm_conv_b"] + dep[0, 0], nb, s, cc, "conv_s")
    dtrt = dtr.T
    y_ssd, sprev = _ssd_fwd(xc, dtr, dtrt, prow, pcol, nb, nc, di, "ssd")
    ygn, y_b = _gate_norm_out(y_ssd, proj, z_col0, small["ssm_norm"], wts["w_out_ssm"], di, "gate_norm_out")
    merged, h2, un = _merge_mix_out(y_a, y_b, proj, ga_blk, gb_blk, wts["w_mix_out"], h1, small["xattn_norm"], "merge_mix_out")
    got, _ = comm.weights(3, h2)
    wts.update(got)
    mn = _rms_fwd(mem, small["mem_norm"], "mem_norm")
    q = _mm(un, wts["w_q"], BF16, "q_proj")
    kv = _mm(mn, wts["w_kv"], BF16, "kv_proj", nt=True)
    o = _xattn_fwd(q, kv, nb, s, m_len, d, "xattn")
    h3, n2 = _mm(o, wts["w_o_x"], F32, "o_proj", res=h2, norm_out=small["ffn2_norm"])
    (dh4, dg_final, loss_vec), ffn2_saved = _ffn_fwd(h3, n2, wts["ffn2_w_gate_up"], wts["ffn2_w_down"], "ffn2",
                                                     loss_head=(small["final_norm"].reshape(1, d), target))

    grads = {"final_norm": dg_final.reshape(d)}
    big = {}
    dh3, grads["ffn2_norm"] = _ffn_bwd(
        dh4, ffn2_saved, small["ffn2_norm"], wts["ffn2_w_gate_up"], wts["ffn2_w_down"], "ffn2", None,
        lambda dw_gu_t, dw_d: comm.grads(0, {"ffn2_w_gate_up": dw_gu_t, "ffn2_w_down": dw_d}))
    big["w_o_x"] = _mm_tn(o, dh3, "dw_o")
    do = _mm(dh3, wts["w_o_x"], BF16, "d_o", nt=True)
    dq, dk, dv = _xattn_bwd(do, q, kv, nb, s, m_len, d, "xattn_bwd")
    big["w_q"] = _mm_tn(un, dq, "dw_q")
    big["w_kv"] = _mm_tn(dv, mn, "dw_v", out_rows=2 * d, row_off=d, into=_mm_tn(dk, mn, "dw_k", out_rows=2 * d))
    _, grads["mem_norm"] = _mm([dk, dv], wts["w_kv"], F32, "d_mn", norm_bwd=(mem, small["mem_norm"]))
    dh2, grads["xattn_norm"] = _mm(dq, wts["w_q"], F32, "d_un", nt=True, res=dh3, norm_bwd=(h2, small["xattn_norm"]))
    big["w_mix_out"] = _mm_tn(merged, dh2, "dw_mix")
    dmerged = _mm(dh2, wts["w_mix_out"], BF16, "d_merged", nt=True)
    dya, dyb, dg = _merge_bwd(dmerged, y_a, y_b, proj, ga_blk, gb_blk, d, "merge_bwd")
    big["w_out_a"] = _mm_tn(yap, dya, "dw_out_a")
    big["w_out_ssm"] = _mm_tn(ygn, dyb, "dw_out_ssm")
    dyap = _mm(dya, wts["w_out_a"], BF16, "d_yap", nt=True)
    dygn = _mm(dyb, wts["w_out_ssm"], BF16, "d_ygn", nt=True)
    dcx, dconv_a = _conv_a_bwd(dyap, proj, conv_a_w8, nb, s, d, 3 * d + cc, "conv_a_bwd")
    dy_ssd, dz, grads["ssm_norm"] = _gate_norm_bwd(dygn, y_ssd, proj, z_col0, small["ssm_norm"], di, "gate_norm_bwd")
    dxc, ddtr, ssd_sums = _ssd_bwd(dy_ssd, xc, dtr, dtrt, prow, pcol, e_mat, sprev, nb, nc, di, "ssd_bwd")
    dcx, dconv_s, grads["ssm_conv_b"] = _conv_s_bwd(dxc, conv_pre, proj, xbc_col0, ssm_conv_w8, nb, s, cc, dcx, "conv_s_bwd")
    dw_cx, dw_z, dw_g = _mm_tn(dcx, u, "dw_in_cx"), _mm_tn(dz, u, "dw_in_z"), _mm_tn(dg, u, "dw_in_g")
    dw_dt = _mm_tn(ddtr, u, "dw_in_dt")[:hs]
    du_main = _mm([dcx, dz, dg], w_main_t, F32, "d_u")
    dh1, grads["mix_norm"] = _mm(ddtr, w_dt_t, F32, "d_u_dt", res=dh2, norm_bwd=(h1, small["mix_norm"], du_main))
    bcv = [[dw_cx[(3 * j + i) * cb_a:(3 * j + i + 1) * cb_a] for j in range(d // cb_a)] for i in range(3)]
    big["w_in"] = jnp.concatenate(bcv[0] + bcv[1] + bcv[2] + [dw_z, dw_cx[3 * d:], dw_dt, dw_g], axis=0)
    dep = comm.grads(1, big)
    dx, grads["ffn1_norm"] = _ffn_bwd(
        dh1, ffn1_saved, small["ffn1_norm"], wts["ffn1_w_gate_up"], wts["ffn1_w_down"], "ffn1", dep,
        lambda dw_gu_t, dw_d: comm.grads(2, {"ffn1_w_gate_up": dw_gu_t, "ffn1_w_down": dw_d}))

    grads["conv_a_w"] = dconv_a[:CONV_A_K]
    grads["ssm_conv_w"] = dconv_s[:SSM_CONV_K]
    grads["ssm_dt_bias"] = ssd_sums[0:1, :hs]
    grads["ssm_a_log"] = ssd_sums[1:2, :hs]
    grads["ssm_d"] = ssd_sums[2:3, :hs]
    return loss_vec[0, 0], dx.reshape(nb, s, d), grads


def _step(inputs):
    w = {k: inputs[k] for k in WEIGHT_ORDER}
    mom = {k: inputs["m_" + k] for k in WEIGHT_ORDER}
    vel = {k: inputs["v_" + k] for k in WEIGHT_ORDER}
    me = 4 * lax.axis_index("x") + 2 * lax.axis_index("y") + lax.axis_index("c")

    send = {k: (w[k][0].T if k in COL_SHARDED else w[k][0]).astype(BF16) for k in BIG_WEIGHTS}

    def own_slot(land, mine):
        return lax.dynamic_update_slice(land, mine[None], (me, 0, 0))

    gathers, exchanges = {}, {}

    def weights(i, after):
        conv = {}
        if i == 0:
            packed_c, conv_offs = _pack_rows([w[k][0] for k in SMALL_SHARDED], F32, 8)
            lands = list(_all_gather([send[k] for k in W_GROUPS[0]] + [packed_c], "gather0"))
            conv_blocks = _unpack_rows(lands.pop(), conv_offs, [w[k].shape[1:] for k in SMALL_SHARDED])
            conv = {k: _full_from_slots(b, True)[None] for k, b in zip(SMALL_SHARDED, conv_blocks)}
        else:
            sent, lands = _push_wait(gathers[i], False, after, f"gather{i}_wait")
            lands = [own_slot(land, mine) for land, mine in zip(lands, sent)]
        full = {k: land.reshape(N_DEV * land.shape[1], land.shape[2]) for k, land in zip(W_GROUPS[i], lands)}
        full.update(conv)
        dep = jnp.zeros((8, LANES), F32)
        if i + 1 < len(W_GROUPS):
            gathers[i + 1] = _push_start([send[k] for k in W_GROUPS[i + 1]], False, lands[0], f"gather{i + 1}_start")
            dep = gathers[i + 1]["token"]
        return full, dep

    def slot_shape(k):
        rows, cols = send[k].shape
        return (rows * cols // LANES, LANES) if rows % BF16_SUBLANES else (rows, cols)

    def send_grads(i, by_name):
        slots = [by_name[k].reshape((N_DEV,) + slot_shape(k)) for k in G_GROUPS[i]]
        exchanges[i] = _push_start(slots, True, slots[0], f"exchange{i}_start")
        return exchanges[i]["token"]

    comm = types.SimpleNamespace(weights=weights, grads=send_grads)
    small = {k: w[k] for k in SMALL_REPLICATED}

    loss_local, grad_x, grads = _local_step(inputs["x"], inputs["mem"], inputs["loss_target"], small, comm)
    loss = lax.psum(loss_local, AXES)

    out = {}
    for i, names in enumerate(G_GROUPS):
        sent, lands = _push_wait(exchanges[i], True, grad_x, f"exchange{i}_wait")
        for k, land, slots in zip(names, lands, sent):
            parts = own_slot(land, lax.dynamic_index_in_dim(slots, me, 0, keepdims=False))
            if k in COL_SHARDED and w[k].shape[2] % LANES:
                flip = lambda a: a.transpose(0, 2, 1).reshape((1,) + slot_shape(k))
                unflip = lambda a: a.reshape((1,) + send[k].shape).transpose(0, 2, 1)
                out[k] = tuple(unflip(o) for o in _sum8_adamw(parts, flip(w[k]), flip(mom[k]), flip(vel[k]),
                                                             f"sum_adamw_{k}", tr=2048 if slot_shape(k)[1] == LANES else 128))
                continue
            if k in COL_SHARDED:
                parts = parts.transpose(0, 2, 1)
            out[k] = tuple(_sum8_adamw(parts, w[k], mom[k], vel[k], f"sum_adamw_{k}"))

    small_names = SMALL_REPLICATED + SMALL_SHARDED
    packed_g, small_offs = _pack_rows([grads[k] for k in small_names], F32, 8)
    total = _sum8(_all_gather([packed_g], "gather_small_grads")[0], "sum_small_grads")
    full_grads = _unpack_rows(total, small_offs, [grads[k].shape for k in small_names])
    mine = {}
    for k, g in zip(small_names, full_grads):
        if k in SMALL_SHARDED:
            c_loc = w[k].shape[2]
            g = lax.dynamic_slice_in_dim(g, me * c_loc, c_loc, axis=1)
        mine[k] = g.reshape(w[k].shape)
    sg, s_offs = _pack_rows([mine[k] for k in small_names], F32, 8)
    sw, _ = _pack_rows([w[k] for k in small_names], F32, 8)
    sm, _ = _pack_rows([mom[k] for k in small_names], F32, 8)
    sv, _ = _pack_rows([vel[k] for k in small_names], F32, 8)
    s_shapes = [w[k].shape for k in small_names]
    small_out = [_unpack_rows(a, s_offs, s_shapes) for a in _adamw(sg, sw, sm, sv, "adamw_small")]
    for i, k in enumerate(small_names):
        out[k] = (mine[k],) + tuple(o[i] for o in small_out)

    res = [loss, grad_x]
    for j in range(4):
        res += [out[k][j] for k in WEIGHT_ORDER]
    return tuple(res)


def kernel(x, mem, ffn1_norm, ffn1_w_gate_up, ffn1_w_down, mix_norm, w_in, conv_a_w, w_out_a, ssm_conv_w, ssm_conv_b, ssm_dt_bias, ssm_a_log, ssm_d, ssm_norm, w_out_ssm, w_mix_out, xattn_norm, mem_norm, w_q, w_kv, w_o_x, ffn2_norm, ffn2_w_gate_up, ffn2_w_down, final_norm, loss_target, m_ffn1_norm, m_ffn1_w_gate_up, m_ffn1_w_down, m_mix_norm, m_w_in, m_conv_a_w, m_w_out_a, m_ssm_conv_w, m_ssm_conv_b, m_ssm_dt_bias, m_ssm_a_log, m_ssm_d, m_ssm_norm, m_w_out_ssm, m_w_mix_out, m_xattn_norm, m_mem_norm, m_w_q, m_w_kv, m_w_o_x, m_ffn2_norm, m_ffn2_w_gate_up, m_ffn2_w_down, m_final_norm, v_ffn1_norm, v_ffn1_w_gate_up, v_ffn1_w_down, v_mix_norm, v_w_in, v_conv_a_w, v_w_out_a, v_ssm_conv_w, v_ssm_conv_b, v_ssm_dt_bias, v_ssm_a_log, v_ssm_d, v_ssm_norm, v_w_out_ssm, v_w_mix_out, v_xattn_norm, v_mem_norm, v_w_q, v_w_kv, v_w_o_x, v_ffn2_norm, v_ffn2_w_gate_up, v_ffn2_w_down, v_final_norm):
    return _step(dict(locals()))
```

```python
import functools
import math
import types

import jax
import jax.numpy as jnp
from jax import lax
from jax.experimental import pallas as pl
from jax.experimental.pallas import tpu as pltpu

F32, BF16 = jnp.float32, jnp.bfloat16
HI = lax.Precision.HIGHEST
MESH = pl.DeviceIdType.MESH
AXES = ("x", "y", "c")
N_DEV = 8

EPS = 1e-6
FFN_RES_WEIGHT = 0.5
SSM_HEAD_DIM = 64
SSM_GROUPS = 4
SSM_STATE = 128
SSM_CHUNK = 128
CONV_A_K = 3
SSM_CONV_K = 4
XATTN_HEADS = 4
ADAM_LR, ADAM_B1, ADAM_B2, ADAM_EPS, ADAM_WD, ADAM_STEP = 1e-3, 0.9, 0.999, 1e-8, 0.01, 10

LANES = 128
BF16_SUBLANES = 16
VMEM_LIMIT_BYTES = 56 * 2 ** 20
NEG_BIG = -1e30

BIG_WEIGHTS = ("ffn1_w_gate_up", "ffn1_w_down", "w_in", "w_out_a", "w_out_ssm", "w_mix_out",
               "w_q", "w_kv", "w_o_x", "ffn2_w_gate_up", "ffn2_w_down")
COL_SHARDED = ("ffn1_w_gate_up", "w_in", "w_kv", "ffn2_w_gate_up")
SMALL_REPLICATED = ("ffn1_norm", "mix_norm", "ssm_conv_b", "ssm_dt_bias", "ssm_a_log", "ssm_d", "ssm_norm",
                    "xattn_norm", "mem_norm", "ffn2_norm", "final_norm")
SMALL_SHARDED = ("conv_a_w", "ssm_conv_w")
WEIGHT_ORDER = ("ffn1_norm", "ffn1_w_gate_up", "ffn1_w_down", "mix_norm", "w_in", "conv_a_w", "w_out_a",
                "ssm_conv_w", "ssm_conv_b", "ssm_dt_bias", "ssm_a_log", "ssm_d", "ssm_norm", "w_out_ssm",
                "w_mix_out", "xattn_norm", "mem_norm", "w_q", "w_kv", "w_o_x", "ffn2_norm", "ffn2_w_gate_up",
                "ffn2_w_down", "final_norm")


def _tile(dim, pref, unit):
    best = None
    t = unit
    while t <= min(dim, pref):
        if dim % t == 0:
            best = t
        t += unit
    return best if best is not None else dim


def _params(*sem):
    return pltpu.CompilerParams(dimension_semantics=sem, vmem_limit_bytes=VMEM_LIMIT_BYTES)


def _sigmoid(x):
    return pl.reciprocal(1.0 + jnp.exp(-x), approx=True)


def _silu(x):
    return x * _sigmoid(x)


def _dsilu(x):
    s = _sigmoid(x)
    return s * (1.0 + x * (1.0 - s))


def _softplus(x):
    return jnp.maximum(x, 0.0) + jnp.log(1.0 + jnp.exp(-jnp.abs(x)))


def _dot(a, b, dims=(((1,), (0,)), ((), ())), precision=None):
    return lax.dot_general(a, b, dims, preferred_element_type=F32, precision=precision)


def _stack_rows(rows, width):
    r_idx = lax.broadcasted_iota(jnp.int32, (8, width), 0)
    acc = jnp.zeros((8, width), F32)
    for k, row in enumerate(rows):
        acc = jnp.where(r_idx == k, row, acc)
    return acc


NT = (((1,), (1,)), ((), ()))
TN = (((0,), (0,)), ((), ()))


def _mm(a, b, out_dtype, name, res=None, alpha=1.0, nt=False, dep=None, norm_out=None, norm_bwd=None,
        loss_head=None, tm=1024, tn=2048, tk=2816):
    pieces = list(a) if isinstance(a, (list, tuple)) else [a]
    m = pieces[0].shape[0]
    k = sum(p.shape[1] for p in pieces)
    n = b.shape[0] if nt else b.shape[1]
    assert (b.shape[1] if nt else b.shape[0]) == k
    tm, tn = _tile(m, tm, 8), _tile(n, tn, LANES)
    tk = _tile(math.gcd(*[p.shape[1] for p in pieces]), tk, LANES)
    nk = k // tk
    starts, s0 = [], 0
    for p in pieces:
        starts.append((s0, p.shape[1] // tk))
        s0 += p.shape[1] // tk
    n_p = len(pieces)
    assert (norm_out is not None) + (norm_bwd is not None) + (loss_head is not None) <= 1
    whole_rows = norm_out is not None or norm_bwd is not None or loss_head is not None
    assert not whole_rows or tn == n
    has_pre = norm_bwd is not None and len(norm_bwd) == 3
    has_x = norm_bwd is not None or loss_head is not None

    def body(*refs):
        a_refs, b_ref = refs[:n_p], refs[n_p]
        nxt = n_p + 1
        r_ref = refs[nxt] if res is not None else None
        nxt += (res is not None) + (dep is not None)
        g_ref = refs[nxt] if whole_rows else None
        x_ref = refs[nxt + 1] if has_x else None
        pre_ref = refs[nxt + 2] if has_pre else None
        nxt += whole_rows + has_x + has_pre
        o_ref = refs[nxt]
        o2_ref = refs[nxt + 1] if whole_rows else None
        o3_ref = refs[nxt + 2] if loss_head is not None else None
        scr = refs[nxt + 1 + whole_rows + (loss_head is not None):]
        first_rows = pl.program_id(0) == 0

        def finish(acc):
            acc = alpha * acc if alpha != 1.0 else acc
            if loss_head is not None:
                hv = r_ref[...] + acc if r_ref is not None else acc
                r = lax.rsqrt(jnp.mean(hv * hv, axis=-1, keepdims=True) + EPS)
                xh = hv * r
                err = xh * g_ref[...] - x_ref[...]
                dout = err * (1.0 / n)
                gy = dout * g_ref[...]
                o_ref[...] = r * (gy - xh * jnp.mean(gy * xh, axis=-1, keepdims=True))
                dg_part = jnp.sum(dout * xh, axis=0, keepdims=True)
                loss_part = jnp.full((1, LANES), 0.5 / n, F32) * jnp.sum(err * err)

                @pl.when(first_rows)
                def _():
                    o2_ref[...] = dg_part
                    o3_ref[...] = loss_part

                @pl.when(jnp.logical_not(first_rows))
                def _():
                    o2_ref[...] += dg_part
                    o3_ref[...] += loss_part
                return
            if norm_bwd is not None:
                if pre_ref is not None:
                    acc = acc + pre_ref[...]
                xv = x_ref[...]
                r = lax.rsqrt(jnp.mean(xv * xv, axis=-1, keepdims=True) + EPS)
                xh = xv * r
                gy = acc * g_ref[...]
                part = jnp.sum(acc * xh, axis=0, keepdims=True)
                acc = r * (gy - xh * jnp.mean(gy * xh, axis=-1, keepdims=True))

                @pl.when(first_rows)
                def _():
                    o2_ref[...] = part

                @pl.when(jnp.logical_not(first_rows))
                def _():
                    o2_ref[...] += part
            if r_ref is not None:
                acc = r_ref[...] + acc
            o_ref[...] = acc.astype(out_dtype)
            if norm_out is not None:
                r = lax.rsqrt(jnp.mean(acc * acc, axis=-1, keepdims=True) + EPS)
                o2_ref[...] = (acc * r * g_ref[...]).astype(BF16)

        def product(a_ref):
            return _dot(a_ref[...].astype(BF16), b_ref[...].astype(BF16), NT if nt else (((1,), (0,)), ((), ())))

        if nk == 1:
            finish(product(a_refs[0]))
            return
        acc_ref = scr[0]
        kk = pl.program_id(2)
        for (s, cnt), a_ref in zip(starts, a_refs):
            if s == 0:
                @pl.when(kk == 0)
                def _():
                    acc_ref[...] = product(a_ref)

                @pl.when(jnp.logical_and(kk > 0, kk < cnt))
                def _():
                    acc_ref[...] += product(a_ref)
            else:
                @pl.when(jnp.logical_and(kk >= s, kk < s + cnt))
                def _():
                    acc_ref[...] += product(a_ref)

        @pl.when(kk == nk - 1)
        def _():
            finish(acc_ref[...])

    def a_spec(s, cnt):
        return pl.BlockSpec((tm, tk), lambda i, j, kk: (i, jnp.clip(kk - s, 0, cnt - 1)))

    in_specs = [a_spec(s, cnt) for s, cnt in starts]
    in_specs.append(pl.BlockSpec((tn, tk), lambda i, j, kk: (j, kk)) if nt else pl.BlockSpec((tk, tn), lambda i, j, kk: (kk, j)))
    args = pieces + [b]
    if res is not None:
        in_specs.append(pl.BlockSpec((tm, tn), lambda i, j, kk: (i, j)))
        args.append(res)
    if dep is not None:
        in_specs.append(pl.BlockSpec((8, LANES), lambda i, j, kk: (0, 0)))
        args.append(dep)
    tile = pl.BlockSpec((tm, tn), lambda i, j, kk: (i, j))
    vec = pl.BlockSpec((1, n), lambda i, j, kk: (0, 0))
    out_specs, out_shape = tile, jax.ShapeDtypeStruct((m, n), out_dtype)
    if norm_out is not None:
        in_specs.append(vec)
        args.append(norm_out)
        out_specs, out_shape = [tile, tile], [out_shape, jax.ShapeDtypeStruct((m, n), BF16)]
    if norm_bwd is not None:
        in_specs += [vec, tile] + ([tile] if has_pre else [])
        args += [norm_bwd[1], norm_bwd[0]] + ([norm_bwd[2]] if has_pre else [])
        out_specs, out_shape = [tile, vec], [out_shape, jax.ShapeDtypeStruct((1, n), F32)]
    if loss_head is not None:
        in_specs += [vec, tile]
        args += [loss_head[0], loss_head[1]]
        out_specs = [tile, vec, pl.BlockSpec((1, LANES), lambda i, j, kk: (0, 0))]
        out_shape = [out_shape, jax.ShapeDtypeStruct((1, n), F32), jax.ShapeDtypeStruct((1, LANES), F32)]
    sums_over_rows = norm_bwd is not None or loss_head is not None
    return pl.pallas_call(
        body, name=name, grid=(m // tm, n // tn, nk), in_specs=in_specs, out_specs=out_specs, out_shape=out_shape,
        scratch_shapes=[pltpu.VMEM((tm, tn), F32)] if nk > 1 else [],
        compiler_params=(_params("arbitrary", "arbitrary", "arbitrary") if sums_over_rows
                         else _params("parallel", "parallel", "arbitrary")),
    )(*args)


def _mm_tn(x, dy, name, out_dtype=BF16, alpha=1.0, dep=None, out_rows=None, row_off=0, into=None,
           tko=1408, tn=1024, tt=2048):
    t, k = x.shape
    n = dy.shape[1]
    tko, tn, tt = _tile(k, tko, LANES), _tile(n, tn, LANES), _tile(t, tt, 8)
    nt_steps = t // tt

    def body(*refs):
        x_ref, dy_ref = refs[:2]
        o_ref, acc_ref = refs[-2:]
        part = _dot(x_ref[...].astype(BF16), dy_ref[...].astype(BF16), TN)
        step = pl.program_id(2)

        @pl.when(step == 0)
        def _():
            acc_ref[...] = part

        @pl.when(step > 0)
        def _():
            acc_ref[...] += part

        @pl.when(step == nt_steps - 1)
        def _():
            acc = acc_ref[...]
            o_ref[...] = (alpha * acc if alpha != 1.0 else acc).astype(out_dtype)

    in_specs = [pl.BlockSpec((tt, tko), lambda i, j, s: (s, i)), pl.BlockSpec((tt, tn), lambda i, j, s: (s, j))]
    args = [x, dy]
    if dep is not None:
        in_specs.append(pl.BlockSpec((8, LANES), lambda i, j, s: (0, 0)))
        args.append(dep)
    aliases = {}
    if into is not None:
        in_specs.append(pl.BlockSpec(memory_space=pl.ANY))
        args.append(into)
        aliases = {len(args) - 1: 0}
    band = row_off // tko
    assert row_off % tko == 0
    return pl.pallas_call(
        body, name=name, grid=(k // tko, n // tn, nt_steps), in_specs=in_specs,
        out_specs=pl.BlockSpec((tko, tn), lambda i, j, s: (i + band, j)),
        out_shape=jax.ShapeDtypeStruct((out_rows or k, n), out_dtype),
        scratch_shapes=[pltpu.VMEM((tko, tn), F32)], input_output_aliases=aliases,
        compiler_params=_params("parallel", "parallel", "arbitrary"),
    )(*args)


def _rms_fwd(x, g, name, tt=512):
    t, d = x.shape
    tt = _tile(t, tt, 8)

    def body(x_ref, g_ref, o_ref):
        xv = x_ref[...]
        r = lax.rsqrt(jnp.mean(xv * xv, axis=-1, keepdims=True) + EPS)
        o_ref[...] = (xv * r * g_ref[...]).astype(BF16)

    return pl.pallas_call(
        body, name=name, grid=(t // tt,),
        in_specs=[pl.BlockSpec((tt, d), lambda i: (i, 0)), pl.BlockSpec((1, d), lambda i: (0, 0))],
        out_specs=pl.BlockSpec((tt, d), lambda i: (i, 0)),
        out_shape=jax.ShapeDtypeStruct((t, d), BF16), compiler_params=_params("parallel"),
    )(x, g)


def _ffn_up(n, w_gu_t, name, tm=512, tf=2816):
    t, d = n.shape
    f = w_gu_t.shape[0] // 2
    tm, tf = _tile(t, tm, 8), _tile(f, tf, LANES)
    nf = f // tf

    def body(n_ref, wg_ref, wu_ref, g_ref, u_ref, a_ref):
        nv = n_ref[...]
        gate, up = _dot(nv, wg_ref[...], NT), _dot(nv, wu_ref[...], NT)
        s = _sigmoid(gate)
        sg = gate * s
        g_ref[...] = (up * (s * (1.0 + gate * (1.0 - s)))).astype(BF16)
        u_ref[...] = sg.astype(BF16)
        a_ref[...] = (sg * up).astype(BF16)

    blk = pl.BlockSpec((tm, tf), lambda i, j: (i, j))
    out = jax.ShapeDtypeStruct((t, f), BF16)
    return pl.pallas_call(
        body, name=name, grid=(t // tm, nf),
        in_specs=[pl.BlockSpec((tm, d), lambda i, j: (i, 0)), pl.BlockSpec((tf, d), lambda i, j: (j, 0)),
                  pl.BlockSpec((tf, d), lambda i, j: (j + nf, 0))],
        out_specs=[blk, blk, blk], out_shape=[out, out, out], compiler_params=_params("parallel", "parallel"),
    )(n, w_gu_t, w_gu_t)


def _ffn_da(dh, w_d, gate, up, name, alpha, dep=None, tm=512, tf=2816):
    t, d = dh.shape
    f = w_d.shape[0]
    tm, tf = _tile(t, tm, 8), _tile(f, tf, LANES)

    def body(*refs):
        dh_ref, w_ref, g_ref, u_ref = refs[:4]
        dg_ref, du_ref = refs[-2:]
        da = alpha * _dot(dh_ref[...].astype(BF16), w_ref[...], NT)
        dg_ref[...] = (da * g_ref[...].astype(F32)).astype(BF16)
        du_ref[...] = (da * u_ref[...].astype(F32)).astype(BF16)

    blk = pl.BlockSpec((tm, tf), lambda i, j: (i, j))
    in_specs = [pl.BlockSpec((tm, d), lambda i, j: (i, 0)), pl.BlockSpec((tf, d), lambda i, j: (j, 0)), blk, blk]
    args = [dh, w_d, gate, up]
    if dep is not None:
        in_specs.append(pl.BlockSpec((8, LANES), lambda i, j: (0, 0)))
        args.append(dep)
    out = jax.ShapeDtypeStruct((t, f), BF16)
    return pl.pallas_call(
        body, name=name, grid=(t // tm, f // tf), in_specs=in_specs, out_specs=[blk, blk], out_shape=[out, out],
        compiler_params=_params("parallel", "parallel"),
    )(*args)


def _merge_mix_out(ya, yb, proj, ga_blk, gb_blk, w_mix, h, gain, name, tt=512):
    t, d = ya.shape
    tt = _tile(t, tt, 8)

    def body(ya_ref, yb_ref, ga_ref, gb_ref, w_ref, h_ref, g_ref, m_ref, ho_ref, n_ref):
        merged = (_sigmoid(ga_ref[...].astype(F32)) * ya_ref[...].astype(F32)
                  + _sigmoid(gb_ref[...].astype(F32)) * yb_ref[...].astype(F32)).astype(BF16)
        m_ref[...] = merged
        hv = h_ref[...] + _dot(merged, w_ref[...])
        ho_ref[...] = hv
        r = lax.rsqrt(jnp.mean(hv * hv, axis=-1, keepdims=True) + EPS)
        n_ref[...] = (hv * r * g_ref[...]).astype(BF16)

    row = pl.BlockSpec((tt, d), lambda i: (i, 0))
    half = jax.ShapeDtypeStruct((t, d), BF16)
    return pl.pallas_call(
        body, name=name, grid=(t // tt,),
        in_specs=[row, row, pl.BlockSpec((tt, d), lambda i: (i, ga_blk)), pl.BlockSpec((tt, d), lambda i: (i, gb_blk)),
                  pl.BlockSpec((d, d), lambda i: (0, 0)), row, pl.BlockSpec((1, d), lambda i: (0, 0))],
        out_specs=[row, row, row], out_shape=[half, jax.ShapeDtypeStruct((t, d), F32), half],
        compiler_params=_params("parallel"),
    )(ya, yb, proj, proj, w_mix, h, gain)


def _merge_bwd(dh, w_mix, ya, yb, proj, ga_blk, gb_blk, d, name, tt=512):
    t = ya.shape[0]
    tt = _tile(t, tt, 8)

    def body(dh_ref, w_ref, ya_ref, yb_ref, ga_ref, gb_ref, dya_ref, dyb_ref, dg_ref):
        dmv = _dot(dh_ref[...].astype(BF16), w_ref[...], NT)
        sa, sb = _sigmoid(ga_ref[...].astype(F32)), _sigmoid(gb_ref[...].astype(F32))
        dya_ref[...] = (dmv * sa).astype(BF16)
        dyb_ref[...] = (dmv * sb).astype(BF16)
        dg_ref[:, 0:d] = (dmv * ya_ref[...].astype(F32) * sa * (1.0 - sa)).astype(BF16)
        dg_ref[:, d:2 * d] = (dmv * yb_ref[...].astype(F32) * sb * (1.0 - sb)).astype(BF16)

    row = pl.BlockSpec((tt, d), lambda i: (i, 0))
    out = jax.ShapeDtypeStruct((t, d), BF16)
    return pl.pallas_call(
        body, name=name, grid=(t // tt,),
        in_specs=[row, pl.BlockSpec((d, d), lambda i: (0, 0)), row, row,
                  pl.BlockSpec((tt, d), lambda i: (i, ga_blk)), pl.BlockSpec((tt, d), lambda i: (i, gb_blk))],
        out_specs=[row, row, pl.BlockSpec((tt, 2 * d), lambda i: (i, 0))],
        out_shape=[out, out, jax.ShapeDtypeStruct((t, 2 * d), BF16)], compiler_params=_params("parallel"),
    )(dh, w_mix, ya, yb, proj, proj)


def _shift_down(x, k, t_idx):
    if k == 0:
        return x
    return jnp.where(t_idx >= k, pltpu.roll(x, k, 0), 0.0)


def _shift_up(x, k, t_idx, s):
    if k == 0:
        return x
    return jnp.where(t_idx < s - k, pltpu.roll(x, s - k, 0), 0.0)


CONV_A_BLOCK = 256


def _conv_a_fwd(proj, w, nb, s, d, name):
    cb = _tile(d, CONV_A_BLOCK, LANES)

    def body(x_ref, w_ref, o_ref):
        t_idx = lax.broadcasted_iota(jnp.int32, (s, cb), 0)
        cv = x_ref[:, cb:2 * cb].astype(F32) * x_ref[:, 2 * cb:3 * cb].astype(F32)
        cc = sum(w_ref[k:k + 1, :] * _shift_down(cv, CONV_A_K - 1 - k, t_idx) for k in range(CONV_A_K))
        o_ref[...] = (x_ref[:, 0:cb].astype(F32) * cc).astype(BF16)

    return pl.pallas_call(
        body, name=name, grid=(nb, d // cb),
        in_specs=[pl.BlockSpec((s, 3 * cb), lambda b, j: (b, j)), pl.BlockSpec((8, cb), lambda b, j: (0, j))],
        out_specs=pl.BlockSpec((s, cb), lambda b, j: (b, j)),
        out_shape=jax.ShapeDtypeStruct((nb * s, d), BF16), compiler_params=_params("parallel", "parallel"),
    )(proj, w)


def _conv_a_bwd(dy, proj, w, nb, s, d, out_cols, name):
    cb = _tile(d, CONV_A_BLOCK, LANES)

    def body(dy_ref, x_ref, w_ref, o_ref, dw_ref):
        t_idx = lax.broadcasted_iota(jnp.int32, (s, cb), 0)
        cv_c, cv_v = x_ref[:, cb:2 * cb].astype(F32), x_ref[:, 2 * cb:3 * cb].astype(F32)
        cv = cv_c * cv_v
        shifted = [_shift_down(cv, CONV_A_K - 1 - k, t_idx) for k in range(CONV_A_K)]
        cc = sum(w_ref[k:k + 1, :] * shifted[k] for k in range(CONV_A_K))
        dyv = dy_ref[...].astype(F32)
        o_ref[:, 0:cb] = (dyv * cc).astype(BF16)
        dcc = dyv * x_ref[:, 0:cb].astype(F32)
        dcv = sum(w_ref[k:k + 1, :] * _shift_up(dcc, CONV_A_K - 1 - k, t_idx, s) for k in range(CONV_A_K))
        o_ref[:, cb:2 * cb] = (dcv * cv_v).astype(BF16)
        o_ref[:, 2 * cb:3 * cb] = (dcv * cv_c).astype(BF16)
        rows = [jnp.sum(dcc * shifted[k], axis=0, keepdims=True) for k in range(CONV_A_K)]
        part = _stack_rows(rows, cb)

        @pl.when(pl.program_id(1) == 0)
        def _():
            dw_ref[...] = part

        @pl.when(pl.program_id(1) > 0)
        def _():
            dw_ref[...] += part

    wspec = pl.BlockSpec((8, cb), lambda j, b: (0, j))
    wide = pl.BlockSpec((s, 3 * cb), lambda j, b: (b, j))
    return pl.pallas_call(
        body, name=name, grid=(d // cb, nb), in_specs=[pl.BlockSpec((s, cb), lambda j, b: (b, j)), wide, wspec],
        out_specs=[wide, wspec],
        out_shape=[jax.ShapeDtypeStruct((nb * s, out_cols), BF16), jax.ShapeDtypeStruct((8, d), F32)],
        compiler_params=_params("parallel", "arbitrary"),
    )(dy, proj, w)


def _conv_s_fwd(proj, col0, w, bias, nb, s, cc_width, name, cb=256):
    cb = _tile(math.gcd(cc_width, col0) if col0 else cc_width, cb, LANES)
    nd, off = cc_width // cb, col0 // cb

    def body(x_ref, w_ref, b_ref, o_ref, pre_ref):
        t_idx = lax.broadcasted_iota(jnp.int32, (s, cb), 0)
        xv = x_ref[...].astype(F32)
        pre = b_ref[...] + sum(w_ref[k:k + 1, :] * _shift_down(xv, SSM_CONV_K - 1 - k, t_idx) for k in range(SSM_CONV_K))
        o_ref[...] = _silu(pre).astype(BF16)
        pre_ref[...] = pre.astype(BF16)

    vec = pl.BlockSpec((8, cb), lambda b, j: (0, j))
    own = pl.BlockSpec((s, cb), lambda b, j: (b, j))
    out = jax.ShapeDtypeStruct((nb * s, cc_width), BF16)
    return pl.pallas_call(
        body, name=name, grid=(nb, nd),
        in_specs=[pl.BlockSpec((s, cb), lambda b, j: (b, j + off)), vec, pl.BlockSpec((1, cb), lambda b, j: (0, j))],
        out_specs=[own, own], out_shape=[out, out], compiler_params=_params("parallel", "parallel"),
    )(proj, w, bias)


def _conv_s_bwd(dxc, pre, proj, col0, w, nb, s, cc_width, into, name, cb=256):
    cb = _tile(math.gcd(cc_width, col0) if col0 else cc_width, cb, LANES)
    nd, off = cc_width // cb, col0 // cb

    def body(d_ref, pre_ref, x_ref, w_ref, into_ref, dx_ref, dw_ref, db_ref):
        t_idx = lax.broadcasted_iota(jnp.int32, (s, cb), 0)
        xv = x_ref[...].astype(F32)
        dpre = d_ref[...].astype(F32) * _dsilu(pre_ref[...].astype(F32))
        ahead = [_shift_up(dpre, j, t_idx, s) for j in range(SSM_CONV_K)]
        dx_ref[...] = sum(w_ref[k:k + 1, :] * ahead[SSM_CONV_K - 1 - k] for k in range(SSM_CONV_K)).astype(BF16)
        rows = [jnp.sum(ahead[SSM_CONV_K - 1 - k] * xv, axis=0, keepdims=True) for k in range(SSM_CONV_K)]
        dw_part = _stack_rows(rows, cb)
        db_part = jnp.sum(dpre, axis=0, keepdims=True)

        @pl.when(pl.program_id(1) == 0)
        def _():
            dw_ref[...] = dw_part
            db_ref[...] = db_part

        @pl.when(pl.program_id(1) > 0)
        def _():
            dw_ref[...] += dw_part
            db_ref[...] += db_part

    own = pl.BlockSpec((s, cb), lambda j, b: (b, j))
    shifted = pl.BlockSpec((s, cb), lambda j, b: (b, j + off))
    wspec = pl.BlockSpec((8, cb), lambda j, b: (0, j))
    bspec = pl.BlockSpec((1, cb), lambda j, b: (0, j))
    return pl.pallas_call(
        body, name=name, grid=(nd, nb),
        in_specs=[own, own, shifted, wspec, pl.BlockSpec(memory_space=pl.ANY)],
        out_specs=[shifted, wspec, bspec],
        out_shape=[jax.ShapeDtypeStruct(into.shape, into.dtype), jax.ShapeDtypeStruct((8, cc_width), F32),
                   jax.ShapeDtypeStruct((1, cc_width), F32)],
        input_output_aliases={4: 0},
        compiler_params=_params("parallel", "arbitrary"),
    )(dxc, pre, proj, w, into)


def _split3(v):
    hi = v.astype(BF16)
    r1 = v - hi.astype(F32)
    mid = r1.astype(BF16)
    return hi, mid, (r1 - mid.astype(F32)).astype(BF16)


def _exact_left(mask_b, v):
    return sum(_dot(mask_b, t) for t in _split3(v))


def _exact_right(v, mask_b):
    return sum(_dot(t, mask_b) for t in _split3(v))


def _head_sums(v, e_b):
    return _dot(v.astype(BF16), e_b, NT)


def _spread(v, out_ref, di):
    lane = lax.broadcasted_iota(jnp.int32, (v.shape[0], LANES), 1)
    for pr in range(di // LANES):
        h0 = pr * (LANES // SSM_HEAD_DIM)
        out_ref[:, pr * LANES:(pr + 1) * LANES] = jnp.where(lane < SSM_HEAD_DIM, v[:, h0:h0 + 1], v[:, h0 + 1:h0 + 2])


def _ssd_common(xc_ref, dtr_ref, dtrt_ref, prow_ref, pcol_ref, dtx_ref, acsx_ref, dx_ref, di):
    l = SSM_CHUNK
    bias_r, a_r = prow_ref[0:1, :], -jnp.exp(prow_ref[1:2, :])
    sp_in = dtr_ref[...] + bias_r
    dt = _softplus(sp_in)
    li = lax.broadcasted_iota(jnp.int32, (l, l), 0)
    si = lax.broadcasted_iota(jnp.int32, (l, l), 1)
    lower_b = (li >= si).astype(BF16)
    upper_b = (li <= si).astype(BF16)
    acs = _exact_left(lower_b, dt * a_r)
    bias_c, a_c = pcol_ref[:, 0:1], -jnp.exp(pcol_ref[:, 1:2])
    dt_t = _softplus(dtrt_ref[...] + bias_c)
    acs_t = _exact_right(dt_t * a_c, upper_b)
    _spread(dt, dtx_ref, di)
    _spread(acs, acsx_ref, di)
    _spread(prow_ref[0:8, :], dx_ref, di)
    acs_exp = acsx_ref[...]
    acs_last = acs_exp[l - 1:l, :]
    x = xc_ref[:, 0:di].astype(F32)
    return dict(dt=dt, a_r=a_r, sp_in=sp_in, acs=acs, acs_t=acs_t, dt_exp=dtx_ref[...], e_exp=jnp.exp(acs_exp),
                el_exp=jnp.exp(acs_last), f_exp=jnp.exp(acs_last - acs_exp), x=x, mask=li >= si, upper_b=upper_b,
                d_exp=dx_ref[2:3, :])


def _decay(q, h):
    seg = q["acs"][:, h:h + 1] - q["acs_t"][h:h + 1, :]
    return jnp.exp(jnp.where(q["mask"], seg, NEG_BIG))


def _ssd_fwd(xc, dtr, dtrt, prow, pcol, nb, nc, di, name):
    l, n, g_n, p = SSM_CHUNK, SSM_STATE, SSM_GROUPS, SSM_HEAD_DIM
    cc = xc.shape[1]
    gw = di // g_n
    assert p * 2 == LANES and gw % LANES == 0

    def body(xc_ref, dtr_ref, dtrt_ref, prow_ref, pcol_ref, y_ref, sprev_ref, st_ref, dtx_ref, acsx_ref, dx_ref):
        @pl.when(pl.program_id(1) == 0)
        def _():
            st_ref[...] = jnp.zeros_like(st_ref)

        q = _ssd_common(xc_ref, dtr_ref, dtrt_ref, prow_ref, pcol_ref, dtx_ref, acsx_ref, dx_ref, di)
        x = q["x"]
        xd = x * q["dt_exp"]
        xdb = xd.astype(BF16)
        xdf = (xd * q["f_exp"]).astype(BF16)
        lane = lax.broadcasted_iota(jnp.int32, (l, LANES), 1)
        for g in range(g_n):
            lo = g * gw
            bg = xc_ref[:, di + g * n: di + (g + 1) * n]
            cg = xc_ref[:, di + g_n * n + g * n: di + g_n * n + (g + 1) * n]
            cb = _dot(cg, bg, NT)
            st_g = st_ref[:, lo:lo + gw]
            y_off = q["e_exp"][:, lo:lo + gw] * _dot(cg, st_g.astype(BF16))
            for pr in range(gw // LANES):
                c0 = lo + pr * LANES
                h0 = c0 // p
                xp = xdb[:, c0:c0 + LANES]
                m0 = (cb * _decay(q, h0)).astype(BF16)
                m1 = (cb * _decay(q, h0 + 1)).astype(BF16)
                yd = _dot(m0, jnp.where(lane < p, xp, 0)) + _dot(m1, jnp.where(lane >= p, xp, 0))
                y_ref[:, c0:c0 + LANES] = (yd + y_off[:, pr * LANES:(pr + 1) * LANES]
                                           + q["d_exp"][:, c0:c0 + LANES] * x[:, c0:c0 + LANES]).astype(BF16)
            sprev_ref[:, lo:lo + gw] = st_g
            st_ref[:, lo:lo + gw] = q["el_exp"][:, lo:lo + gw] * st_g + _dot(bg, xdf[:, lo:lo + gw], TN)

    tok = lambda w: pl.BlockSpec((l, w), lambda b, c: (b * nc + c, 0))
    const = lambda r, w: pl.BlockSpec((r, w), lambda b, c: (0, 0))
    return pl.pallas_call(
        body, name=name, grid=(nb, nc),
        in_specs=[tok(cc), tok(LANES), pl.BlockSpec((LANES, l), lambda b, c: (0, b * nc + c)),
                  const(8, LANES), const(LANES, 8)],
        out_specs=[tok(di), pl.BlockSpec((None, n, di), lambda b, c: (b * nc + c, 0, 0))],
        out_shape=[jax.ShapeDtypeStruct((nb * nc * l, di), BF16), jax.ShapeDtypeStruct((nb * nc, n, di), F32)],
        scratch_shapes=[pltpu.VMEM((n, di), F32), pltpu.VMEM((l, di), F32), pltpu.VMEM((l, di), F32),
                        pltpu.VMEM((8, di), F32)],
        compiler_params=_params("parallel", "arbitrary"),
    )(xc, dtr, dtrt, prow, pcol)


def _ssd_bwd(dy, xc, dtr, dtrt, prow, pcol, e_mat, sprev, nb, nc, di, name):
    l, n, g_n, p = SSM_CHUNK, SSM_STATE, SSM_GROUPS, SSM_HEAD_DIM
    cc = xc.shape[1]
    gw = di // g_n

    def body(dy_ref, xc_ref, dtr_ref, dtrt_ref, prow_ref, pcol_ref, e_ref, sprev_ref,
             dxc_ref, ddtr_ref, sums_ref, dst_ref, off_ref, dxd_ref, last_ref, vst_ref,
             dtx_ref, acsx_ref, dx_ref):
        first = jnp.logical_and(pl.program_id(0) == 0, pl.program_id(1) == 0)

        @pl.when(pl.program_id(1) == 0)
        def _():
            dst_ref[...] = jnp.zeros_like(dst_ref)

        head_row = lax.broadcasted_iota(jnp.int32, (LANES, l), 0)
        row_sums, col_sums = jnp.zeros((l, LANES), F32), jnp.zeros((LANES, l), F32)
        strict_lower = lax.broadcasted_iota(jnp.int32, (l, l), 0) > lax.broadcasted_iota(jnp.int32, (l, l), 1)

        q = _ssd_common(xc_ref, dtr_ref, dtrt_ref, prow_ref, pcol_ref, dtx_ref, acsx_ref, dx_ref, di)
        x = q["x"]
        xd = x * q["dt_exp"]
        xdb = xd.astype(BF16)
        xdf = (xd * q["f_exp"]).astype(BF16)
        dyv = dy_ref[...].astype(F32)
        dyb = dy_ref[...]
        dye = (dyv * q["e_exp"]).astype(BF16)
        upper_b = q["upper_b"]
        lane = lax.broadcasted_iota(jnp.int32, (l, LANES), 1)
        for g in range(g_n):
            lo = g * gw
            bg = xc_ref[:, di + g * n: di + (g + 1) * n]
            cg = xc_ref[:, di + g_n * n + g * n: di + g_n * n + (g + 1) * n]
            cb = _dot(cg, bg, NT)
            st_g = sprev_ref[:, lo:lo + gw]
            st_gb = st_g.astype(BF16)
            dst_g = dst_ref[:, lo:lo + gw]
            dst_gb = dst_g.astype(BF16)
            dye_g = dye[:, lo:lo + gw]
            xdf_g = xdf[:, lo:lo + gw]
            y_off = q["e_exp"][:, lo:lo + gw] * _dot(cg, st_gb)
            dc_g = _dot(dye_g, st_gb, NT)
            db_g = _dot(xdf_g, dst_gb, NT)
            dxd_state = _dot(bg, dst_gb) * q["f_exp"][:, lo:lo + gw]
            last_ref[:, lo:lo + gw] = jnp.sum(dst_g * st_g, axis=0, keepdims=True)
            dst_ref[:, lo:lo + gw] = q["el_exp"][:, lo:lo + gw] * dst_g + _dot(cg, dye_g, TN)
            off_ref[:, lo:lo + gw] = dyv[:, lo:lo + gw] * y_off
            vst_ref[:, lo:lo + gw] = xd[:, lo:lo + gw] * dxd_state
            dcb = jnp.zeros((l, l), F32)
            for pr in range(gw // LANES):
                c0 = lo + pr * LANES
                h0 = c0 // p
                xp = xdb[:, c0:c0 + LANES]
                dyp = dyb[:, c0:c0 + LANES]
                dxd_diag = jnp.zeros((l, LANES), F32)
                for k, keep in enumerate((lane < p, lane >= p)):
                    dec = _decay(q, h0 + k)
                    dy_h = jnp.where(keep, dyp, 0)
                    dm_dec = _dot(dy_h, xp, NT) * dec
                    dcb = dcb + dm_dec
                    dxd_diag = dxd_diag + _dot((cb * dec).astype(BF16), dy_h, TN)
                    qm = dm_dec * cb
                    row_sums = jnp.where(lane == h0 + k, jnp.sum(qm, axis=1, keepdims=True), row_sums)
                    col_sums = jnp.where(head_row == h0 + k, jnp.sum(qm, axis=0, keepdims=True), col_sums)
                dxd_ref[:, c0:c0 + LANES] = dxd_diag + dxd_state[:, pr * LANES:(pr + 1) * LANES]
            dcb_b = dcb.astype(BF16)
            dxc_ref[:, di + g * n: di + (g + 1) * n] = (db_g + _dot(dcb_b, cg, TN)).astype(BF16)
            dxc_ref[:, di + g_n * n + g * n: di + g_n * n + (g + 1) * n] = (dc_g + _dot(dcb_b, bg)).astype(BF16)
        dxd = dxd_ref[...]
        e_b = e_ref[...]
        from_y = _exact_left(upper_b, _head_sums(off_ref[...], e_b) + row_sums - col_sums.T)
        from_s = _exact_left(strict_lower.astype(BF16), _head_sums(vst_ref[...], e_b))
        carried = _head_sums(jnp.broadcast_to(last_ref[...], (8, di)), e_b)[0:1, :] * jnp.exp(q["acs"][l - 1:l, :])
        dla = from_y + from_s + carried
        ddt = dla * q["a_r"] + _head_sums(dxd * x, e_b)
        ddtr = ddt * jax.nn.sigmoid(q["sp_in"])
        ddtr_ref[...] = ddtr
        dxc_ref[:, 0:di] = (dxd * q["dt_exp"] + q["d_exp"] * dyv).astype(BF16)
        dd_exp = jnp.sum(dyv * x, axis=0, keepdims=True)
        dd = _head_sums(jnp.broadcast_to(dd_exp, (8, di)), e_b)[0:1, :]
        part = _stack_rows([jnp.sum(ddtr, axis=0, keepdims=True),
                            jnp.sum(dla * q["dt"], axis=0, keepdims=True) * q["a_r"], dd], LANES)

        @pl.when(first)
        def _():
            sums_ref[...] = part

        @pl.when(jnp.logical_not(first))
        def _():
            sums_ref[...] += part

    rev = lambda b, c: b * nc + (nc - 1 - c)
    tok = lambda w: pl.BlockSpec((l, w), lambda b, c: (rev(b, c), 0))
    const = lambda r, w: pl.BlockSpec((r, w), lambda b, c: (0, 0))
    return pl.pallas_call(
        body, name=name, grid=(nb, nc),
        in_specs=[tok(di), tok(cc), tok(LANES), pl.BlockSpec((LANES, l), lambda b, c: (0, rev(b, c))),
                  const(8, LANES), const(LANES, 8), const(LANES, di),
                  pl.BlockSpec((None, n, di), lambda b, c: (rev(b, c), 0, 0))],
        out_specs=[tok(cc), tok(LANES), const(8, LANES)],
        out_shape=[jax.ShapeDtypeStruct((nb * nc * l, cc), BF16), jax.ShapeDtypeStruct((nb * nc * l, LANES), F32),
                   jax.ShapeDtypeStruct((8, LANES), F32)],
        scratch_shapes=[pltpu.VMEM((n, di), F32), pltpu.VMEM((l, di), F32), pltpu.VMEM((l, di), F32),
                        pltpu.VMEM((1, di), F32), pltpu.VMEM((l, di), F32),
                        pltpu.VMEM((l, di), F32), pltpu.VMEM((l, di), F32), pltpu.VMEM((8, di), F32)],
        compiler_params=_params("arbitrary", "arbitrary"),
    )(dy, xc, dtr, dtrt, prow, pcol, e_mat, sprev)


def _gate_norm_out(y, proj, z_col0, norm_g, w_out, di, name, tt=512):
    t = y.shape[0]
    d_out = w_out.shape[1]
    tt = _tile(t, tt, 8)
    gw = di // SSM_GROUPS
    zw = _tile(math.gcd(di, z_col0), di, LANES)
    nz, zoff = di // zw, z_col0 // zw

    def body(*refs):
        y_ref, z_refs, g_ref, w_ref, o_ref, yb_ref = refs[0], refs[1:1 + nz], refs[1 + nz], refs[2 + nz], refs[3 + nz], refs[4 + nz]
        for g in range(SSM_GROUPS):
            lo = g * gw
            zv = z_refs[lo // zw][:, lo % zw:lo % zw + gw].astype(F32)
            yg = y_ref[:, lo:lo + gw].astype(F32) * _silu(zv)
            r = lax.rsqrt(jnp.mean(yg * yg, axis=-1, keepdims=True) + EPS)
            o_ref[:, lo:lo + gw] = (yg * r * g_ref[:, lo:lo + gw]).astype(BF16)
        yb_ref[...] = _dot(o_ref[...], w_ref[...]).astype(BF16)

    row = pl.BlockSpec((tt, di), lambda i: (i, 0))
    zspecs = [pl.BlockSpec((tt, zw), functools.partial(lambda i, k: (i, zoff + k), k=k)) for k in range(nz)]
    return pl.pallas_call(
        body, name=name, grid=(t // tt,),
        in_specs=[row] + zspecs + [pl.BlockSpec((1, di), lambda i: (0, 0)), pl.BlockSpec((di, d_out), lambda i: (0, 0))],
        out_specs=[row, pl.BlockSpec((tt, d_out), lambda i: (i, 0))],
        out_shape=[jax.ShapeDtypeStruct((t, di), BF16), jax.ShapeDtypeStruct((t, d_out), BF16)],
        compiler_params=_params("parallel"),
    )(y, *([proj] * nz), norm_g, w_out)


def _gate_norm_bwd(dyb, w_out, y, proj, z_col0, norm_g, di, name, tt=256):
    t = y.shape[0]
    d_out = w_out.shape[1]
    tt = _tile(t, tt, 8)
    gw = di // SSM_GROUPS
    zw = _tile(math.gcd(di, z_col0), di, LANES)
    nz, zoff = di // zw, z_col0 // zw

    def body(*refs):
        dyb_ref, w_ref, y_ref, z_refs, g_ref = refs[0], refs[1], refs[2], refs[3:3 + nz], refs[3 + nz]
        dy_ref, dz_ref, dg_ref, dn_ref = refs[4 + nz:]
        dn_ref[...] = _dot(dyb_ref[...], w_ref[...], NT)
        first = pl.program_id(0) == 0
        for g in range(SSM_GROUPS):
            lo = g * gw
            zv = z_refs[lo // zw][:, lo % zw:lo % zw + gw].astype(F32)
            yv = y_ref[:, lo:lo + gw].astype(F32)
            sz = _silu(zv)
            yg = yv * sz
            r = lax.rsqrt(jnp.mean(yg * yg, axis=-1, keepdims=True) + EPS)
            yh = yg * r
            dnv = dn_ref[:, lo:lo + gw].astype(F32)
            gy = dnv * g_ref[:, lo:lo + gw]
            dyg = r * (gy - yh * jnp.mean(gy * yh, axis=-1, keepdims=True))
            dy_ref[:, lo:lo + gw] = (dyg * sz).astype(BF16)
            dz_ref[:, lo:lo + gw] = (dyg * yv * _dsilu(zv)).astype(BF16)
            part = jnp.sum(dnv * yh, axis=0, keepdims=True)

            @pl.when(first)
            def _():
                dg_ref[:, lo:lo + gw] = part

            @pl.when(jnp.logical_not(first))
            def _():
                dg_ref[:, lo:lo + gw] += part

    row = pl.BlockSpec((tt, di), lambda i: (i, 0))
    vec = pl.BlockSpec((1, di), lambda i: (0, 0))
    zspecs = [pl.BlockSpec((tt, zw), functools.partial(lambda i, k: (i, zoff + k), k=k)) for k in range(nz)]
    return pl.pallas_call(
        body, name=name, grid=(t // tt,),
        in_specs=[pl.BlockSpec((tt, d_out), lambda i: (i, 0)), pl.BlockSpec((di, d_out), lambda i: (0, 0)), row] + zspecs + [vec],
        out_specs=[row, row, vec],
        out_shape=[jax.ShapeDtypeStruct((t, di), BF16), jax.ShapeDtypeStruct((t, di), BF16), jax.ShapeDtypeStruct((1, di), F32)],
        scratch_shapes=[pltpu.VMEM((tt, di), F32)],
        compiler_params=_params("arbitrary"),
    )(dyb, w_out, y, *([proj] * nz), norm_g)


def _softmax_rows(s):
    s = s - jnp.max(s, axis=-1, keepdims=True)
    e = jnp.exp(s)
    return e * (1.0 / jnp.sum(e, axis=-1, keepdims=True))


def _xattn_fwd(q, kv, nb, s, m, d, name, tq=1024):
    tq = _tile(s, tq, 8)
    nq = s // tq
    hd = d // XATTN_HEADS
    scale = 1.0 / math.sqrt(hd)

    def body(q_ref, k_ref, v_ref, o_ref):
        for h in range(XATTN_HEADS):
            sl = slice(h * hd, (h + 1) * hd)
            prob = _softmax_rows(_dot(q_ref[:, sl], k_ref[:, sl], NT) * scale)
            o_ref[:, sl] = _dot(prob.astype(BF16), v_ref[:, sl]).astype(BF16)

    return pl.pallas_call(
        body, name=name, grid=(nb, nq),
        in_specs=[pl.BlockSpec((tq, d), lambda b, i: (b * nq + i, 0)), pl.BlockSpec((m, d), lambda b, i: (b, 0)),
                  pl.BlockSpec((m, d), lambda b, i: (b, 1))],
        out_specs=pl.BlockSpec((tq, d), lambda b, i: (b * nq + i, 0)),
        out_shape=jax.ShapeDtypeStruct((nb * s, d), BF16), compiler_params=_params("parallel", "parallel"),
    )(q, kv, kv)


def _xattn_bwd(do, q, kv, nb, s, m, d, name, tq=1024):
    tq = _tile(s, tq, 8)
    nq = s // tq
    hd = d // XATTN_HEADS
    scale = 1.0 / math.sqrt(hd)

    def body(do_ref, q_ref, k_ref, v_ref, dq_ref, dk_ref, dv_ref):
        first = pl.program_id(1) == 0
        for h in range(XATTN_HEADS):
            sl = slice(h * hd, (h + 1) * hd)
            qh, kh, vh, doh = q_ref[:, sl], k_ref[:, sl], v_ref[:, sl], do_ref[:, sl]
            prob = _softmax_rows(_dot(qh, kh, NT) * scale)
            dv_h = _dot(prob.astype(BF16), doh, TN)
            dp = _dot(doh, vh, NT)
            ds = (prob * (dp - jnp.sum(dp * prob, axis=-1, keepdims=True)) * scale).astype(BF16)
            dq_ref[:, sl] = _dot(ds, kh).astype(BF16)
            dk_h = _dot(ds, qh, TN)

            @pl.when(first)
            def _():
                dk_ref[:, sl] = dk_h
                dv_ref[:, sl] = dv_h

            @pl.when(jnp.logical_not(first))
            def _():
                dk_ref[:, sl] += dk_h
                dv_ref[:, sl] += dv_h

    qspec = pl.BlockSpec((tq, d), lambda b, i: (b * nq + i, 0))
    dq, dk, dv = pl.pallas_call(
        body, name=name, grid=(nb, nq),
        in_specs=[qspec, qspec, pl.BlockSpec((m, d), lambda b, i: (b, 0)), pl.BlockSpec((m, d), lambda b, i: (b, 1))],
        out_specs=[qspec, pl.BlockSpec((m, d), lambda b, i: (b, 0)), pl.BlockSpec((m, d), lambda b, i: (b, 0))],
        out_shape=[jax.ShapeDtypeStruct((nb * s, d), BF16), jax.ShapeDtypeStruct((nb * m, d), F32),
                   jax.ShapeDtypeStruct((nb * m, d), F32)],
        compiler_params=_params("parallel", "arbitrary"),
    )(do, q, kv, kv)
    return dq, dk, dv


def _all_gather(shards, name):
    n_arr = len(shards)

    def body(*refs):
        x_refs, out_refs = refs[:n_arr], refs[n_arr:2 * n_arr]
        send_sems, recv_sems, local_sems = refs[2 * n_arr:]
        x, y, c = lax.axis_index("x"), lax.axis_index("y"), lax.axis_index("c")
        me, sibling = (x, y, c), (x, y, 1 - c)
        chips = [(1 - x, y), (x, 1 - y), (1 - x, 1 - y)]

        def copy(w, k, block, to, from_input=False):
            px, py, pc = block
            rows = out_refs[w].at[4 * px + 2 * py + pc]
            return pltpu.make_async_remote_copy(
                src_ref=x_refs[w] if from_input else rows, dst_ref=rows,
                send_sem=send_sems.at[7 * w + k], recv_sem=recv_sems.at[7 * w + k], device_id=to, device_id_type=MESH)

        started = []
        for w in range(n_arr):
            mine = pltpu.make_async_copy(x_refs[w], out_refs[w].at[4 * x + 2 * y + c], local_sems.at[w])
            mine.start()
            started.append(mine)
        sends = []
        for w in range(n_arr):
            sends.append(copy(w, 0, me, sibling, from_input=True))
            sends += [copy(w, 1 + j, me, (*chip, c), from_input=True) for j, chip in enumerate(chips)]
        for cp in sends:
            cp.start()
        for j, chip in enumerate(chips):
            for w in range(n_arr):
                copy(w, 1 + j, (*chip, c), me).wait_recv()
                passed = copy(w, 4 + j, (*chip, c), sibling)
                passed.start()
                sends.append(passed)
        for w in range(n_arr):
            copy(w, 0, sibling, me).wait_recv()
            for j, chip in enumerate(chips):
                copy(w, 4 + j, (*chip, 1 - c), me).wait_recv()
        for cp in sends:
            cp.wait_send()
        for mine in started:
            mine.wait()

    hbm = pl.BlockSpec(memory_space=pl.ANY)
    return pl.pallas_call(
        body, name=name, out_shape=[jax.ShapeDtypeStruct((N_DEV,) + s.shape, s.dtype) for s in shards],
        in_specs=[hbm] * n_arr, out_specs=[hbm] * n_arr,
        scratch_shapes=[pltpu.SemaphoreType.DMA((7 * n_arr,)), pltpu.SemaphoreType.DMA((7 * n_arr,)),
                        pltpu.SemaphoreType.DMA((n_arr,))],
    )(*shards)


_HBM = pl.BlockSpec(memory_space=pltpu.HBM)
_SEM = pl.BlockSpec(memory_space=pltpu.SEMAPHORE)
_DATAFLOW = pltpu.SideEffectType.DATAFLOW_SIDE_EFFECTING


def _peer_list(x, y, c):
    return [(1 - x if k & 4 else x, 1 - y if k & 2 else y, 1 - c if k & 1 else c) for k in range(1, N_DEV)]


def _push_copy(src_ref, land_ref, send_sems, recv_sems, w, k, peer, me, per_peer_src, receiving):
    px, py, pc = peer
    peer_slot = 4 * px + 2 * py + pc
    return pltpu.make_async_remote_copy(
        src_ref=src_ref.at[peer_slot] if per_peer_src else src_ref,
        dst_ref=land_ref.at[peer_slot if receiving else me],
        send_sem=send_sems.at[7 * w + k], recv_sem=recv_sems.at[7 * w + k], device_id=peer, device_id_type=MESH)


def _push_start(srcs, per_peer_src, after, name):
    n_arr = len(srcs)
    land_shapes = [s.shape if per_peer_src else (N_DEV,) + s.shape for s in srcs]

    def body(*refs):
        src_refs, land_refs = refs[:n_arr], refs[n_arr:2 * n_arr]
        send_sems, recv_sems = refs[2 * n_arr + 1], refs[2 * n_arr + 2]
        token = refs[-1]
        x, y, c = lax.axis_index("x"), lax.axis_index("y"), lax.axis_index("c")
        me = 4 * x + 2 * y + c
        for w in range(n_arr):
            for k, peer in enumerate(_peer_list(x, y, c)):
                _push_copy(src_refs[w], land_refs[w], send_sems, recv_sems, w, k, peer, me, per_peer_src, False).start()
        token[...] = jnp.zeros_like(token)

    lands = [pltpu.with_memory_space_constraint(lax.empty(ls, s.dtype), pltpu.HBM) for ls, s in zip(land_shapes, srcs)]
    srcs_hbm = [pltpu.with_memory_space_constraint(s, pltpu.HBM) for s in srcs]
    out = pl.pallas_call(
        body, name=name,
        out_shape=(pltpu.SemaphoreType.DMA((7 * n_arr,)), pltpu.SemaphoreType.DMA((7 * n_arr,)),
                   *[pltpu.HBM(s.shape, s.dtype) for s in srcs], *[pltpu.HBM(ls, s.dtype) for ls, s in zip(land_shapes, srcs)],
                   jax.ShapeDtypeStruct((8, LANES), F32)),
        in_specs=[_HBM] * (2 * n_arr) + [pl.BlockSpec(memory_space=pl.ANY)],
        out_specs=(_SEM, _SEM, *([_HBM] * (2 * n_arr)), pl.BlockSpec(memory_space=pltpu.VMEM)),
        input_output_aliases={i: 2 + i for i in range(2 * n_arr)},
        compiler_params=pltpu.CompilerParams(has_side_effects=_DATAFLOW),
    )(*srcs_hbm, *lands, after)
    return dict(send=out[0], recv=out[1], srcs=list(out[2:2 + n_arr]), lands=list(out[2 + n_arr:2 + 2 * n_arr]),
                token=out[-1])


def _push_wait(pending, per_peer_src, after, name):
    n_arr = len(pending["srcs"])

    def body(*refs):
        src_refs, land_refs = refs[:n_arr], refs[n_arr:2 * n_arr]
        send_sems, recv_sems = refs[2 * n_arr], refs[2 * n_arr + 1]
        x, y, c = lax.axis_index("x"), lax.axis_index("y"), lax.axis_index("c")
        me = 4 * x + 2 * y + c
        for w in range(n_arr):
            for k, peer in enumerate(_peer_list(x, y, c)):
                cp = _push_copy(src_refs[w], land_refs[w], send_sems, recv_sems, w, k, peer, me, per_peer_src, True)
                cp.wait_send()
                cp.wait_recv()

    out = pl.pallas_call(
        body, name=name,
        out_shape=tuple(pltpu.HBM(a.shape, a.dtype) for a in pending["srcs"] + pending["lands"]),
        in_specs=[_HBM] * (2 * n_arr) + [_SEM, _SEM, pl.BlockSpec(memory_space=pl.ANY)],
        out_specs=tuple([_HBM] * (2 * n_arr)),
        input_output_aliases={i: i for i in range(2 * n_arr)},
        compiler_params=pltpu.CompilerParams(has_side_effects=_DATAFLOW),
    )(*pending["srcs"], *pending["lands"], pending["send"], pending["recv"], after)
    return list(out[:n_arr]), list(out[n_arr:])


def _adamw_math(w, g, m, v):
    m = ADAM_B1 * m + (1.0 - ADAM_B1) * g
    v = ADAM_B2 * v + (1.0 - ADAM_B2) * (g * g)
    m_hat = m / (1.0 - ADAM_B1 ** ADAM_STEP)
    v_hat = v / (1.0 - ADAM_B2 ** ADAM_STEP)
    delta = -ADAM_LR * (m_hat / (jnp.sqrt(v_hat) + ADAM_EPS) + ADAM_WD * w)
    return delta, m, v


def _sum8(parts, name, tr=512):
    _, r, c_dim = parts.shape
    tr = _tile(r, tr, BF16_SUBLANES)

    def body(p_ref, o_ref):
        acc = p_ref[0].astype(F32)
        for k in range(1, N_DEV):
            acc = acc + p_ref[k].astype(F32)
        o_ref[...] = acc

    return pl.pallas_call(
        body, name=name, grid=(r // tr,), in_specs=[pl.BlockSpec((N_DEV, tr, c_dim), lambda i: (0, i, 0))],
        out_specs=pl.BlockSpec((tr, c_dim), lambda i: (i, 0)),
        out_shape=jax.ShapeDtypeStruct((r, c_dim), F32), compiler_params=_params("parallel"),
    )(parts)


def _sum8_adamw(parts, w, m, v, name, tr=128):
    _, r, c_dim = parts.shape
    tr = _tile(r, tr, BF16_SUBLANES)
    tc = c_dim if tr <= 2 * LANES else _tile(c_dim, LANES, LANES)

    def body(p_ref, w_ref, m_ref, v_ref, g_ref, d_ref, nm_ref, nv_ref):
        g = p_ref[0].astype(F32)
        for k in range(1, N_DEV):
            g = g + p_ref[k].astype(F32)
        g_ref[...] = g
        d_ref[...], nm_ref[...], nv_ref[...] = _adamw_math(w_ref[...], g, m_ref[...], v_ref[...])

    blk = pl.BlockSpec((None, tr, tc), lambda i, j: (0, i, j))
    out = jax.ShapeDtypeStruct((1, r, c_dim), F32)
    return pl.pallas_call(
        body, name=name, grid=(r // tr, c_dim // tc),
        in_specs=[pl.BlockSpec((N_DEV, tr, tc), lambda i, j: (0, i, j)), blk, blk, blk],
        out_specs=[blk] * 4, out_shape=[out] * 4, compiler_params=_params("parallel", "parallel"),
    )(parts, w, m, v)


def _adamw(g, w, m, v, name):
    r, c_dim = g.shape

    def body(g_ref, w_ref, m_ref, v_ref, d_ref, nm_ref, nv_ref):
        d_ref[...], nm_ref[...], nv_ref[...] = _adamw_math(w_ref[...], g_ref[...], m_ref[...], v_ref[...])

    out = jax.ShapeDtypeStruct((r, c_dim), F32)
    return pl.pallas_call(body, name=name, out_shape=[out] * 3)(g, w, m, v)


def _pack_rows(arrays, dtype, row_unit):
    chunks, offs, r0 = [], [], 0
    for a in arrays:
        flat = a.reshape(-1).astype(dtype)
        rows = -(-flat.shape[0] // (LANES * row_unit)) * row_unit
        flat = jnp.pad(flat, (0, rows * LANES - flat.shape[0]))
        chunks.append(flat.reshape(rows, LANES))
        offs.append((r0, rows))
        r0 += rows
    return jnp.concatenate(chunks, axis=0), offs


def _unpack_rows(packed, offs, shapes):
    out = []
    for (r0, rows), shape in zip(offs, shapes):
        n = math.prod(shape)
        blk = packed[..., r0:r0 + rows, :]
        blk = blk.reshape(packed.shape[:-2] + (rows * LANES,))[..., :n]
        out.append(blk.reshape(packed.shape[:-2] + tuple(shape)))
    return out


def _full_from_slots(blk, col_sharded):
    _, r, c = blk.shape
    if col_sharded:
        return blk.transpose(1, 0, 2).reshape(r, N_DEV * c)
    return blk.reshape(N_DEV * r, c)


def _ffn_fwd(h, n, w_gu_t, w_d, tag, **tail):
    gate, up, a = _ffn_up(n, w_gu_t, f"{tag}_up")
    return _mm(a, w_d, F32, f"{tag}_down", res=h, alpha=FFN_RES_WEIGHT, **tail), (h, n, gate, up, a)


def _ffn_bwd(dh_out, saved, g, w_gu_t, w_d, tag, dep, send_grads):
    h, n, gate, up, a = saved
    dw_d = _mm_tn(a, dh_out, f"{tag}_dw_down", alpha=FFN_RES_WEIGHT, dep=dep)
    dgate, dup = _ffn_da(dh_out, w_d, gate, up, f"{tag}_da", FFN_RES_WEIGHT)
    f = dgate.shape[1]
    dw_gu_t = _mm_tn(dgate, n, f"{tag}_dw_gate", out_rows=2 * f)
    dw_gu_t = _mm_tn(dup, n, f"{tag}_dw_up", out_rows=2 * f, row_off=f, into=dw_gu_t)
    dep = send_grads(dw_gu_t, dw_d)
    return _mm([dgate, dup], w_gu_t, F32, f"{tag}_dn", dep=dep, tk=1408, res=dh_out, norm_bwd=(h, g))


W_GROUPS = (("ffn1_w_gate_up", "ffn1_w_down"),
            ("w_in",),
            ("w_out_a", "w_out_ssm", "w_mix_out"),
            ("w_q", "w_kv", "w_o_x", "ffn2_w_gate_up", "ffn2_w_down"))
G_GROUPS = (("ffn2_w_gate_up", "ffn2_w_down"),
            ("w_o_x", "w_q", "w_kv", "w_mix_out", "w_out_a", "w_out_ssm", "w_in"),
            ("ffn1_w_gate_up", "ffn1_w_down"))


def _local_step(x3, mem3, target3, small, comm):
    nb, s, d = x3.shape
    m_len = mem3.shape[1]
    t = nb * s
    nc = s // SSM_CHUNK
    di = small["ssm_norm"].shape[1]
    hs = di // SSM_HEAD_DIM
    cc = di + 2 * SSM_GROUPS * SSM_STATE
    x, mem, target = x3.reshape(t, d), mem3.reshape(nb * m_len, d), target3.reshape(t, d)

    sizes = (d, d, d, di, cc, hs, d, d)
    offs = [0]
    for sz in sizes:
        offs.append(offs[-1] + sz)
    cb_a = _tile(d, CONV_A_BLOCK, LANES)
    xbc_col0, z_col0 = 3 * d, 3 * d + cc
    ga_blk, gb_blk = (3 * d + di + cc) // d, (4 * d + di + cc) // d

    pad_vec = lambda v: jnp.pad(v.reshape(1, -1), ((0, 0), (0, LANES - hs)))
    prow = jnp.concatenate([pad_vec(small["ssm_dt_bias"]), pad_vec(small["ssm_a_log"]), pad_vec(small["ssm_d"]),
                            jnp.zeros((5, LANES), F32)], axis=0)
    pcol = prow.T
    e_mat = (lax.broadcasted_iota(jnp.int32, (LANES, di), 0)
             == lax.broadcasted_iota(jnp.int32, (LANES, di), 1) // SSM_HEAD_DIM).astype(BF16)

    wts, dep = comm.weights(0, None)
    conv_a_w8 = jnp.pad(wts["conv_a_w"][0], ((0, 8 - CONV_A_K), (0, 0)))
    ssm_conv_w8 = jnp.pad(wts["ssm_conv_w"][0], ((0, 8 - SSM_CONV_K), (0, 0)))
    n1 = _rms_fwd(x, small["ffn1_norm"] + dep[0, 0], "ffn1_norm")
    (h1, u), ffn1_saved = _ffn_fwd(x, n1, wts["ffn1_w_gate_up"], wts["ffn1_w_down"], "ffn1", norm_out=small["mix_norm"])
    got, dep = comm.weights(1, h1)
    wts.update(got)
    w_in_t = wts["w_in"]
    conv_rows = [w_in_t[offs[i] + j * cb_a:offs[i] + (j + 1) * cb_a] for j in range(d // cb_a) for i in (0, 1, 2)]
    w_main_t = jnp.concatenate(conv_rows + [w_in_t[offs[i]:offs[i + 1]] for i in (4, 3, 6, 7)], axis=0)
    w_dt_t = jnp.pad(w_in_t[offs[5]:offs[6]], ((0, LANES - hs), (0, 0)))
    proj = _mm(u, w_main_t, BF16, "in_proj", nt=True, dep=dep)
    dtr = _mm(u, w_dt_t, F32, "in_proj_dt", nt=True)
    yap = _conv_a_fwd(proj, conv_a_w8, nb, s, d, "conv_a")
    got, dep = comm.weights(2, yap)
    wts.update(got)
    y_a = _mm(yap, wts["w_out_a"], BF16, "out_a", dep=dep)
    xc, conv_pre = _conv_s_fwd(proj, xbc_col0, ssm_conv_w8, small["ssm_conv_b"] + dep[0, 0], nb, s, cc, "conv_s")
    dtrt = dtr.T
    y_ssd, sprev = _ssd_fwd(xc, dtr, dtrt, prow, pcol, nb, nc, di, "ssd")
    ygn, y_b = _gate_norm_out(y_ssd, proj, z_col0, small["ssm_norm"], wts["w_out_ssm"], di, "gate_norm_out")
    merged, h2, un = _merge_mix_out(y_a, y_b, proj, ga_blk, gb_blk, wts["w_mix_out"], h1, small["xattn_norm"], "merge_mix_out")
    got, _ = comm.weights(3, h2)
    wts.update(got)
    mn = _rms_fwd(mem, small["mem_norm"], "mem_norm")
    q = _mm(un, wts["w_q"], BF16, "q_proj")
    kv = _mm(mn, wts["w_kv"], BF16, "kv_proj", nt=True)
    o = _xattn_fwd(q, kv, nb, s, m_len, d, "xattn")
    h3, n2 = _mm(o, wts["w_o_x"], F32, "o_proj", res=h2, norm_out=small["ffn2_norm"])
    (dh4, dg_final, loss_vec), ffn2_saved = _ffn_fwd(h3, n2, wts["ffn2_w_gate_up"], wts["ffn2_w_down"], "ffn2",
                                                     loss_head=(small["final_norm"].reshape(1, d), target))

    grads = {"final_norm": dg_final.reshape(d)}
    big = {}
    dh3, grads["ffn2_norm"] = _ffn_bwd(
        dh4, ffn2_saved, small["ffn2_norm"], wts["ffn2_w_gate_up"], wts["ffn2_w_down"], "ffn2", None,
        lambda dw_gu_t, dw_d: comm.grads(0, {"ffn2_w_gate_up": dw_gu_t, "ffn2_w_down": dw_d}))
    big["w_o_x"] = _mm_tn(o, dh3, "dw_o")
    do = _mm(dh3, wts["w_o_x"], BF16, "d_o", nt=True)
    dq, dk, dv = _xattn_bwd(do, q, kv, nb, s, m_len, d, "xattn_bwd")
    big["w_q"] = _mm_tn(un, dq, "dw_q")
    big["w_kv"] = _mm_tn(dv, mn, "dw_v", out_rows=2 * d, row_off=d, into=_mm_tn(dk, mn, "dw_k", out_rows=2 * d))
    _, grads["mem_norm"] = _mm([dk, dv], wts["w_kv"], F32, "d_mn", norm_bwd=(mem, small["mem_norm"]))
    dh2, grads["xattn_norm"] = _mm(dq, wts["w_q"], F32, "d_un", nt=True, res=dh3, norm_bwd=(h2, small["xattn_norm"]))
    big["w_mix_out"] = _mm_tn(merged, dh2, "dw_mix")
    dya, dyb, dg = _merge_bwd(dh2, wts["w_mix_out"], y_a, y_b, proj, ga_blk, gb_blk, d, "merge_bwd")
    big["w_out_a"] = _mm_tn(yap, dya, "dw_out_a")
    big["w_out_ssm"] = _mm_tn(ygn, dyb, "dw_out_ssm")
    dyap = _mm(dya, wts["w_out_a"], BF16, "d_yap", nt=True)
    dcx, dconv_a = _conv_a_bwd(dyap, proj, conv_a_w8, nb, s, d, 3 * d + cc, "conv_a_bwd")
    dy_ssd, dz, grads["ssm_norm"] = _gate_norm_bwd(dyb, wts["w_out_ssm"], y_ssd, proj, z_col0, small["ssm_norm"], di,
                                                   "gate_norm_bwd")
    dxc, ddtr, ssd_sums = _ssd_bwd(dy_ssd, xc, dtr, dtrt, prow, pcol, e_mat, sprev, nb, nc, di, "ssd_bwd")
    dcx, dconv_s, grads["ssm_conv_b"] = _conv_s_bwd(dxc, conv_pre, proj, xbc_col0, ssm_conv_w8, nb, s, cc, dcx, "conv_s_bwd")
    dw_cx, dw_z, dw_g = _mm_tn(dcx, u, "dw_in_cx"), _mm_tn(dz, u, "dw_in_z"), _mm_tn(dg, u, "dw_in_g")
    dw_dt = _mm_tn(ddtr, u, "dw_in_dt")[:hs]
    du_main = _mm([dcx, dz, dg], w_main_t, F32, "d_u")
    dh1, grads["mix_norm"] = _mm(ddtr, w_dt_t, F32, "d_u_dt", res=dh2, norm_bwd=(h1, small["mix_norm"], du_main))
    bcv = [[dw_cx[(3 * j + i) * cb_a:(3 * j + i + 1) * cb_a] for j in range(d // cb_a)] for i in range(3)]
    big["w_in"] = jnp.concatenate(bcv[0] + bcv[1] + bcv[2] + [dw_z, dw_cx[3 * d:], dw_dt, dw_g], axis=0)
    dep = comm.grads(1, big)
    dx, grads["ffn1_norm"] = _ffn_bwd(
        dh1, ffn1_saved, small["ffn1_norm"], wts["ffn1_w_gate_up"], wts["ffn1_w_down"], "ffn1", dep,
        lambda dw_gu_t, dw_d: comm.grads(2, {"ffn1_w_gate_up": dw_gu_t, "ffn1_w_down": dw_d}))

    grads["conv_a_w"] = dconv_a[:CONV_A_K]
    grads["ssm_conv_w"] = dconv_s[:SSM_CONV_K]
    grads["ssm_dt_bias"] = ssd_sums[0:1, :hs]
    grads["ssm_a_log"] = ssd_sums[1:2, :hs]
    grads["ssm_d"] = ssd_sums[2:3, :hs]
    return loss_vec[0, 0], dx.reshape(nb, s, d), grads


def _step(inputs):
    w = {k: inputs[k] for k in WEIGHT_ORDER}
    mom = {k: inputs["m_" + k] for k in WEIGHT_ORDER}
    vel = {k: inputs["v_" + k] for k in WEIGHT_ORDER}
    me = 4 * lax.axis_index("x") + 2 * lax.axis_index("y") + lax.axis_index("c")

    send = {k: (w[k][0].T if k in COL_SHARDED else w[k][0]).astype(BF16) for k in BIG_WEIGHTS}

    def own_slot(land, mine):
        return lax.dynamic_update_slice(land, mine[None], (me, 0, 0))

    gathers, exchanges = {}, {}

    def weights(i, after):
        conv = {}
        if i == 0:
            packed_c, conv_offs = _pack_rows([w[k][0] for k in SMALL_SHARDED], F32, 8)
            lands = list(_all_gather([send[k] for k in W_GROUPS[0]] + [packed_c], "gather0"))
            conv_blocks = _unpack_rows(lands.pop(), conv_offs, [w[k].shape[1:] for k in SMALL_SHARDED])
            conv = {k: _full_from_slots(b, True)[None] for k, b in zip(SMALL_SHARDED, conv_blocks)}
        else:
            sent, lands = _push_wait(gathers[i], False, after, f"gather{i}_wait")
            lands = [own_slot(land, mine) for land, mine in zip(lands, sent)]
        full = {k: land.reshape(N_DEV * land.shape[1], land.shape[2]) for k, land in zip(W_GROUPS[i], lands)}
        full.update(conv)
        dep = jnp.zeros((8, LANES), F32)
        if i + 1 < len(W_GROUPS):
            gathers[i + 1] = _push_start([send[k] for k in W_GROUPS[i + 1]], False, lands[0], f"gather{i + 1}_start")
            dep = gathers[i + 1]["token"]
        return full, dep

    def slot_shape(k):
        rows, cols = send[k].shape
        return (rows * cols // LANES, LANES) if rows % BF16_SUBLANES else (rows, cols)

    def send_grads(i, by_name):
        slots = [by_name[k].reshape((N_DEV,) + slot_shape(k)) for k in G_GROUPS[i]]
        exchanges[i] = _push_start(slots, True, slots[0], f"exchange{i}_start")
        return exchanges[i]["token"]

    comm = types.SimpleNamespace(weights=weights, grads=send_grads)
    small = {k: w[k] for k in SMALL_REPLICATED}

    loss_local, grad_x, grads = _local_step(inputs["x"], inputs["mem"], inputs["loss_target"], small, comm)
    loss = lax.psum(loss_local, AXES)

    out = {}
    for i, names in enumerate(G_GROUPS):
        sent, lands = _push_wait(exchanges[i], True, grad_x, f"exchange{i}_wait")
        for k, land, slots in zip(names, lands, sent):
            parts = own_slot(land, lax.dynamic_index_in_dim(slots, me, 0, keepdims=False))
            if k in COL_SHARDED and w[k].shape[2] % LANES:
                flip = lambda a: a.transpose(0, 2, 1).reshape((1,) + slot_shape(k))
                unflip = lambda a: a.reshape((1,) + send[k].shape).transpose(0, 2, 1)
                out[k] = tuple(unflip(o) for o in _sum8_adamw(parts, flip(w[k]), flip(mom[k]), flip(vel[k]),
                                                             f"sum_adamw_{k}", tr=2048 if slot_shape(k)[1] == LANES else 128))
                continue
            if k in COL_SHARDED:
                parts = parts.transpose(0, 2, 1)
            out[k] = tuple(_sum8_adamw(parts, w[k], mom[k], vel[k], f"sum_adamw_{k}"))

    small_names = SMALL_REPLICATED + SMALL_SHARDED
    packed_g, small_offs = _pack_rows([grads[k] for k in small_names], F32, 8)
    total = _sum8(_all_gather([packed_g], "gather_small_grads")[0], "sum_small_grads")
    full_grads = _unpack_rows(total, small_offs, [grads[k].shape for k in small_names])
    mine = {}
    for k, g in zip(small_names, full_grads):
        if k in SMALL_SHARDED:
            c_loc = w[k].shape[2]
            g = lax.dynamic_slice_in_dim(g, me * c_loc, c_loc, axis=1)
        mine[k] = g.reshape(w[k].shape)
    sg, s_offs = _pack_rows([mine[k] for k in small_names], F32, 8)
    sw, _ = _pack_rows([w[k] for k in small_names], F32, 8)
    sm, _ = _pack_rows([mom[k] for k in small_names], F32, 8)
    sv, _ = _pack_rows([vel[k] for k in small_names], F32, 8)
    s_shapes = [w[k].shape for k in small_names]
    small_out = [_unpack_rows(a, s_offs, s_shapes) for a in _adamw(sg, sw, sm, sv, "adamw_small")]
    for i, k in enumerate(small_names):
        out[k] = (mine[k],) + tuple(o[i] for o in small_out)

    res = [loss, grad_x]
    for j in range(4):
        res += [out[k][j] for k in WEIGHT_ORDER]
    return tuple(res)


def kernel(x, mem, ffn1_norm, ffn1_w_gate_up, ffn1_w_down, mix_norm, w_in, conv_a_w, w_out_a, ssm_conv_w, ssm_conv_b, ssm_dt_bias, ssm_a_log, ssm_d, ssm_norm, w_out_ssm, w_mix_out, xattn_norm, mem_norm, w_q, w_kv, w_o_x, ffn2_norm, ffn2_w_gate_up, ffn2_w_down, final_norm, loss_target, m_ffn1_norm, m_ffn1_w_gate_up, m_ffn1_w_down, m_mix_norm, m_w_in, m_conv_a_w, m_w_out_a, m_ssm_conv_w, m_ssm_conv_b, m_ssm_dt_bias, m_ssm_a_log, m_ssm_d, m_ssm_norm, m_w_out_ssm, m_w_mix_out, m_xattn_norm, m_mem_norm, m_w_q, m_w_kv, m_w_o_x, m_ffn2_norm, m_ffn2_w_gate_up, m_ffn2_w_down, m_final_norm, v_ffn1_norm, v_ffn1_w_gate_up, v_ffn1_w_down, v_mix_norm, v_w_in, v_conv_a_w, v_w_out_a, v_ssm_conv_w, v_ssm_conv_b, v_ssm_dt_bias, v_ssm_a_log, v_ssm_d, v_ssm_norm, v_w_out_ssm, v_w_mix_out, v_xattn_norm, v_mem_norm, v_w_q, v_w_kv, v_w_o_x, v_ffn2_norm, v_ffn2_w_gate_up, v_ffn2_w_down, v_final_norm):
    return _step(dict(locals()))
```

```python
import functools
import math
import types

import jax
import jax.numpy as jnp
from jax import lax
from jax.experimental import pallas as pl
from jax.experimental.pallas import tpu as pltpu

F32, BF16 = jnp.float32, jnp.bfloat16
HI = lax.Precision.HIGHEST
MESH = pl.DeviceIdType.MESH
AXES = ("x", "y", "c")
N_DEV = 8

EPS = 1e-6
FFN_RES_WEIGHT = 0.5
SSM_HEAD_DIM = 64
SSM_GROUPS = 4
SSM_STATE = 128
SSM_CHUNK = 128
CONV_A_K = 3
SSM_CONV_K = 4
XATTN_HEADS = 4
ADAM_LR, ADAM_B1, ADAM_B2, ADAM_EPS, ADAM_WD, ADAM_STEP = 1e-3, 0.9, 0.999, 1e-8, 0.01, 10

LANES = 128
BF16_SUBLANES = 16
VMEM_LIMIT_BYTES = 56 * 2 ** 20
NEG_BIG = -1e30

BIG_WEIGHTS = ("ffn1_w_gate_up", "ffn1_w_down", "w_in", "w_out_a", "w_out_ssm", "w_mix_out",
               "w_q", "w_kv", "w_o_x", "ffn2_w_gate_up", "ffn2_w_down")
COL_SHARDED = ("ffn1_w_gate_up", "w_in", "w_kv", "ffn2_w_gate_up")
SMALL_REPLICATED = ("ffn1_norm", "mix_norm", "ssm_conv_b", "ssm_dt_bias", "ssm_a_log", "ssm_d", "ssm_norm",
                    "xattn_norm", "mem_norm", "ffn2_norm", "final_norm")
SMALL_SHARDED = ("conv_a_w", "ssm_conv_w")
WEIGHT_ORDER = ("ffn1_norm", "ffn1_w_gate_up", "ffn1_w_down", "mix_norm", "w_in", "conv_a_w", "w_out_a",
                "ssm_conv_w", "ssm_conv_b", "ssm_dt_bias", "ssm_a_log", "ssm_d", "ssm_norm", "w_out_ssm",
                "w_mix_out", "xattn_norm", "mem_norm", "w_q", "w_kv", "w_o_x", "ffn2_norm", "ffn2_w_gate_up",
                "ffn2_w_down", "final_norm")


def _tile(dim, pref, unit):
    best = None
    t = unit
    while t <= min(dim, pref):
        if dim % t == 0:
            best = t
        t += unit
    return best if best is not None else dim


def _params(*sem):
    return pltpu.CompilerParams(dimension_semantics=sem, vmem_limit_bytes=VMEM_LIMIT_BYTES)


def _sigmoid(x):
    return pl.reciprocal(1.0 + jnp.exp(-x), approx=True)


def _silu(x):
    return x * _sigmoid(x)


def _dsilu(x):
    s = _sigmoid(x)
    return s * (1.0 + x * (1.0 - s))


def _softplus(x):
    return jnp.maximum(x, 0.0) + jnp.log(1.0 + jnp.exp(-jnp.abs(x)))


def _dot(a, b, dims=(((1,), (0,)), ((), ())), precision=None):
    return lax.dot_general(a, b, dims, preferred_element_type=F32, precision=precision)


def _stack_rows(rows, width):
    r_idx = lax.broadcasted_iota(jnp.int32, (8, width), 0)
    acc = jnp.zeros((8, width), F32)
    for k, row in enumerate(rows):
        acc = jnp.where(r_idx == k, row, acc)
    return acc


NT = (((1,), (1,)), ((), ()))
TN = (((0,), (0,)), ((), ()))


def _mm(a, b, out_dtype, name, res=None, alpha=1.0, nt=False, dep=None, norm_out=None, norm_bwd=None,
        loss_head=None, tm=1024, tn=2048, tk=2816):
    pieces = list(a) if isinstance(a, (list, tuple)) else [a]
    m = pieces[0].shape[0]
    k = sum(p.shape[1] for p in pieces)
    n = b.shape[0] if nt else b.shape[1]
    assert (b.shape[1] if nt else b.shape[0]) == k
    tm, tn = _tile(m, tm, 8), _tile(n, tn, LANES)
    tk = _tile(math.gcd(*[p.shape[1] for p in pieces]), tk, LANES)
    nk = k // tk
    starts, s0 = [], 0
    for p in pieces:
        starts.append((s0, p.shape[1] // tk))
        s0 += p.shape[1] // tk
    n_p = len(pieces)
    assert (norm_out is not None) + (norm_bwd is not None) + (loss_head is not None) <= 1
    whole_rows = norm_out is not None or norm_bwd is not None or loss_head is not None
    assert not whole_rows or tn == n
    has_pre = norm_bwd is not None and len(norm_bwd) == 3
    has_x = norm_bwd is not None or loss_head is not None

    def body(*refs):
        a_refs, b_ref = refs[:n_p], refs[n_p]
        nxt = n_p + 1
        r_ref = refs[nxt] if res is not None else None
        nxt += (res is not None) + (dep is not None)
        g_ref = refs[nxt] if whole_rows else None
        x_ref = refs[nxt + 1] if has_x else None
        pre_ref = refs[nxt + 2] if has_pre else None
        nxt += whole_rows + has_x + has_pre
        o_ref = refs[nxt]
        o2_ref = refs[nxt + 1] if whole_rows else None
        o3_ref = refs[nxt + 2] if loss_head is not None else None
        scr = refs[nxt + 1 + whole_rows + (loss_head is not None):]
        first_rows = pl.program_id(0) == 0

        def finish(acc):
            acc = alpha * acc if alpha != 1.0 else acc
            if loss_head is not None:
                hv = r_ref[...] + acc if r_ref is not None else acc
                r = lax.rsqrt(jnp.mean(hv * hv, axis=-1, keepdims=True) + EPS)
                xh = hv * r
                err = xh * g_ref[...] - x_ref[...]
                dout = err * (1.0 / n)
                gy = dout * g_ref[...]
                o_ref[...] = r * (gy - xh * jnp.mean(gy * xh, axis=-1, keepdims=True))
                dg_part = jnp.sum(dout * xh, axis=0, keepdims=True)
                loss_part = jnp.full((1, LANES), 0.5 / n, F32) * jnp.sum(err * err)

                @pl.when(first_rows)
                def _():
                    o2_ref[...] = dg_part
                    o3_ref[...] = loss_part

                @pl.when(jnp.logical_not(first_rows))
                def _():
                    o2_ref[...] += dg_part
                    o3_ref[...] += loss_part
                return
            if norm_bwd is not None:
                if pre_ref is not None:
                    acc = acc + pre_ref[...]
                xv = x_ref[...]
                r = lax.rsqrt(jnp.mean(xv * xv, axis=-1, keepdims=True) + EPS)
                xh = xv * r
                gy = acc * g_ref[...]
                part = jnp.sum(acc * xh, axis=0, keepdims=True)
                acc = r * (gy - xh * jnp.mean(gy * xh, axis=-1, keepdims=True))

                @pl.when(first_rows)
                def _():
                    o2_ref[...] = part

                @pl.when(jnp.logical_not(first_rows))
                def _():
                    o2_ref[...] += part
            if r_ref is not None:
                acc = r_ref[...] + acc
            o_ref[...] = acc.astype(out_dtype)
            if norm_out is not None:
                r = lax.rsqrt(jnp.mean(acc * acc, axis=-1, keepdims=True) + EPS)
                o2_ref[...] = (acc * r * g_ref[...]).astype(BF16)

        def product(a_ref):
            return _dot(a_ref[...].astype(BF16), b_ref[...].astype(BF16), NT if nt else (((1,), (0,)), ((), ())))

        if nk == 1:
            finish(product(a_refs[0]))
            return
        acc_ref = scr[0]
        kk = pl.program_id(2)
        for (s, cnt), a_ref in zip(starts, a_refs):
            if s == 0:
                @pl.when(kk == 0)
                def _():
                    acc_ref[...] = product(a_ref)

                @pl.when(jnp.logical_and(kk > 0, kk < cnt))
                def _():
                    acc_ref[...] += product(a_ref)
            else:
                @pl.when(jnp.logical_and(kk >= s, kk < s + cnt))
                def _():
                    acc_ref[...] += product(a_ref)

        @pl.when(kk == nk - 1)
        def _():
            finish(acc_ref[...])

    def a_spec(s, cnt):
        return pl.BlockSpec((tm, tk), lambda i, j, kk: (i, jnp.clip(kk - s, 0, cnt - 1)))

    in_specs = [a_spec(s, cnt) for s, cnt in starts]
    in_specs.append(pl.BlockSpec((tn, tk), lambda i, j, kk: (j, kk)) if nt else pl.BlockSpec((tk, tn), lambda i, j, kk: (kk, j)))
    args = pieces + [b]
    if res is not None:
        in_specs.append(pl.BlockSpec((tm, tn), lambda i, j, kk: (i, j)))
        args.append(res)
    if dep is not None:
        in_specs.append(pl.BlockSpec((8, LANES), lambda i, j, kk: (0, 0)))
        args.append(dep)
    tile = pl.BlockSpec((tm, tn), lambda i, j, kk: (i, j))
    vec = pl.BlockSpec((1, n), lambda i, j, kk: (0, 0))
    out_specs, out_shape = tile, jax.ShapeDtypeStruct((m, n), out_dtype)
    if norm_out is not None:
        in_specs.append(vec)
        args.append(norm_out)
        out_specs, out_shape = [tile, tile], [out_shape, jax.ShapeDtypeStruct((m, n), BF16)]
    if norm_bwd is not None:
        in_specs += [vec, tile] + ([tile] if has_pre else [])
        args += [norm_bwd[1], norm_bwd[0]] + ([norm_bwd[2]] if has_pre else [])
        out_specs, out_shape = [tile, vec], [out_shape, jax.ShapeDtypeStruct((1, n), F32)]
    if loss_head is not None:
        in_specs += [vec, tile]
        args += [loss_head[0], loss_head[1]]
        out_specs = [tile, vec, pl.BlockSpec((1, LANES), lambda i, j, kk: (0, 0))]
        out_shape = [out_shape, jax.ShapeDtypeStruct((1, n), F32), jax.ShapeDtypeStruct((1, LANES), F32)]
    sums_over_rows = norm_bwd is not None or loss_head is not None
    return pl.pallas_call(
        body, name=name, grid=(m // tm, n // tn, nk), in_specs=in_specs, out_specs=out_specs, out_shape=out_shape,
        scratch_shapes=[pltpu.VMEM((tm, tn), F32)] if nk > 1 else [],
        compiler_params=(_params("arbitrary", "arbitrary", "arbitrary") if sums_over_rows
                         else _params("parallel", "parallel", "arbitrary")),
    )(*args)


def _mm_tn(x, dy, name, out_dtype=BF16, alpha=1.0, dep=None, out_rows=None, row_off=0, into=None,
           tko=1408, tn=1024, tt=2048):
    t, k = x.shape
    n = dy.shape[1]
    tko, tn, tt = _tile(k, tko, LANES), _tile(n, tn, LANES), _tile(t, tt, 8)
    nt_steps = t // tt

    def body(*refs):
        x_ref, dy_ref = refs[:2]
        o_ref, acc_ref = refs[-2:]
        part = _dot(x_ref[...].astype(BF16), dy_ref[...].astype(BF16), TN)
        step = pl.program_id(2)

        @pl.when(step == 0)
        def _():
            acc_ref[...] = part

        @pl.when(step > 0)
        def _():
            acc_ref[...] += part

        @pl.when(step == nt_steps - 1)
        def _():
            acc = acc_ref[...]
            o_ref[...] = (alpha * acc if alpha != 1.0 else acc).astype(out_dtype)

    in_specs = [pl.BlockSpec((tt, tko), lambda i, j, s: (s, i)), pl.BlockSpec((tt, tn), lambda i, j, s: (s, j))]
    args = [x, dy]
    if dep is not None:
        in_specs.append(pl.BlockSpec((8, LANES), lambda i, j, s: (0, 0)))
        args.append(dep)
    aliases = {}
    if into is not None:
        in_specs.append(pl.BlockSpec(memory_space=pl.ANY))
        args.append(into)
        aliases = {len(args) - 1: 0}
    band = row_off // tko
    assert row_off % tko == 0
    return pl.pallas_call(
        body, name=name, grid=(k // tko, n // tn, nt_steps), in_specs=in_specs,
        out_specs=pl.BlockSpec((tko, tn), lambda i, j, s: (i + band, j)),
        out_shape=jax.ShapeDtypeStruct((out_rows or k, n), out_dtype),
        scratch_shapes=[pltpu.VMEM((tko, tn), F32)], input_output_aliases=aliases,
        compiler_params=_params("parallel", "parallel", "arbitrary"),
    )(*args)


def _rms_fwd(x, g, name, tt=512):
    t, d = x.shape
    tt = _tile(t, tt, 8)

    def body(x_ref, g_ref, o_ref):
        xv = x_ref[...]
        r = lax.rsqrt(jnp.mean(xv * xv, axis=-1, keepdims=True) + EPS)
        o_ref[...] = (xv * r * g_ref[...]).astype(BF16)

    return pl.pallas_call(
        body, name=name, grid=(t // tt,),
        in_specs=[pl.BlockSpec((tt, d), lambda i: (i, 0)), pl.BlockSpec((1, d), lambda i: (0, 0))],
        out_specs=pl.BlockSpec((tt, d), lambda i: (i, 0)),
        out_shape=jax.ShapeDtypeStruct((t, d), BF16), compiler_params=_params("parallel"),
    )(x, g)


def _ffn_up(n, w_gu_t, name, tm=512, tf=2816):
    t, d = n.shape
    f = w_gu_t.shape[0] // 2
    tm, tf = _tile(t, tm, 8), _tile(f, tf, LANES)
    nf = f // tf

    def body(n_ref, wg_ref, wu_ref, g_ref, u_ref, a_ref):
        nv = n_ref[...]
        gate, up = _dot(nv, wg_ref[...], NT), _dot(nv, wu_ref[...], NT)
        s = _sigmoid(gate)
        sg = gate * s
        g_ref[...] = (up * (s * (1.0 + gate * (1.0 - s)))).astype(BF16)
        u_ref[...] = sg.astype(BF16)
        a_ref[...] = (sg * up).astype(BF16)

    blk = pl.BlockSpec((tm, tf), lambda i, j: (i, j))
    out = jax.ShapeDtypeStruct((t, f), BF16)
    return pl.pallas_call(
        body, name=name, grid=(t // tm, nf),
        in_specs=[pl.BlockSpec((tm, d), lambda i, j: (i, 0)), pl.BlockSpec((tf, d), lambda i, j: (j, 0)),
                  pl.BlockSpec((tf, d), lambda i, j: (j + nf, 0))],
        out_specs=[blk, blk, blk], out_shape=[out, out, out], compiler_params=_params("parallel", "parallel"),
    )(n, w_gu_t, w_gu_t)


def _ffn_da(dh, w_d, gate, up, name, alpha, dep=None, tm=512, tf=2816):
    t, d = dh.shape
    f = w_d.shape[0]
    tm, tf = _tile(t, tm, 8), _tile(f, tf, LANES)

    def body(*refs):
        dh_ref, w_ref, g_ref, u_ref = refs[:4]
        dg_ref, du_ref = refs[-2:]
        da = alpha * _dot(dh_ref[...].astype(BF16), w_ref[...], NT)
        dg_ref[...] = (da * g_ref[...].astype(F32)).astype(BF16)
        du_ref[...] = (da * u_ref[...].astype(F32)).astype(BF16)

    blk = pl.BlockSpec((tm, tf), lambda i, j: (i, j))
    in_specs = [pl.BlockSpec((tm, d), lambda i, j: (i, 0)), pl.BlockSpec((tf, d), lambda i, j: (j, 0)), blk, blk]
    args = [dh, w_d, gate, up]
    if dep is not None:
        in_specs.append(pl.BlockSpec((8, LANES), lambda i, j: (0, 0)))
        args.append(dep)
    out = jax.ShapeDtypeStruct((t, f), BF16)
    return pl.pallas_call(
        body, name=name, grid=(t // tm, f // tf), in_specs=in_specs, out_specs=[blk, blk], out_shape=[out, out],
        compiler_params=_params("parallel", "parallel"),
    )(*args)


def _merge_mix_out(ya, yb, proj, ga_blk, gb_blk, w_mix, h, gain, name, tt=512):
    t, d = ya.shape
    tt = _tile(t, tt, 8)

    def body(ya_ref, yb_ref, ga_ref, gb_ref, w_ref, h_ref, g_ref, m_ref, ho_ref, n_ref):
        merged = (_sigmoid(ga_ref[...].astype(F32)) * ya_ref[...].astype(F32)
                  + _sigmoid(gb_ref[...].astype(F32)) * yb_ref[...].astype(F32)).astype(BF16)
        m_ref[...] = merged
        hv = h_ref[...] + _dot(merged, w_ref[...])
        ho_ref[...] = hv
        r = lax.rsqrt(jnp.mean(hv * hv, axis=-1, keepdims=True) + EPS)
        n_ref[...] = (hv * r * g_ref[...]).astype(BF16)

    row = pl.BlockSpec((tt, d), lambda i: (i, 0))
    half = jax.ShapeDtypeStruct((t, d), BF16)
    return pl.pallas_call(
        body, name=name, grid=(t // tt,),
        in_specs=[row, row, pl.BlockSpec((tt, d), lambda i: (i, ga_blk)), pl.BlockSpec((tt, d), lambda i: (i, gb_blk)),
                  pl.BlockSpec((d, d), lambda i: (0, 0)), row, pl.BlockSpec((1, d), lambda i: (0, 0))],
        out_specs=[row, row, row], out_shape=[half, jax.ShapeDtypeStruct((t, d), F32), half],
        compiler_params=_params("parallel"),
    )(ya, yb, proj, proj, w_mix, h, gain)


def _merge_bwd(dh, w_mix, ya, yb, proj, ga_blk, gb_blk, d, name, tt=512):
    t = ya.shape[0]
    tt = _tile(t, tt, 8)

    def body(dh_ref, w_ref, ya_ref, yb_ref, ga_ref, gb_ref, dya_ref, dyb_ref, dg_ref):
        dmv = _dot(dh_ref[...].astype(BF16), w_ref[...], NT)
        sa, sb = _sigmoid(ga_ref[...].astype(F32)), _sigmoid(gb_ref[...].astype(F32))
        dya_ref[...] = (dmv * sa).astype(BF16)
        dyb_ref[...] = (dmv * sb).astype(BF16)
        dg_ref[:, 0:d] = (dmv * ya_ref[...].astype(F32) * sa * (1.0 - sa)).astype(BF16)
        dg_ref[:, d:2 * d] = (dmv * yb_ref[...].astype(F32) * sb * (1.0 - sb)).astype(BF16)

    row = pl.BlockSpec((tt, d), lambda i: (i, 0))
    out = jax.ShapeDtypeStruct((t, d), BF16)
    return pl.pallas_call(
        body, name=name, grid=(t // tt,),
        in_specs=[row, pl.BlockSpec((d, d), lambda i: (0, 0)), row, row,
                  pl.BlockSpec((tt, d), lambda i: (i, ga_blk)), pl.BlockSpec((tt, d), lambda i: (i, gb_blk))],
        out_specs=[row, row, pl.BlockSpec((tt, 2 * d), lambda i: (i, 0))],
        out_shape=[out, out, jax.ShapeDtypeStruct((t, 2 * d), BF16)], compiler_params=_params("parallel"),
    )(dh, w_mix, ya, yb, proj, proj)


def _shift_down(x, k, t_idx):
    if k == 0:
        return x
    return jnp.where(t_idx >= k, pltpu.roll(x, k, 0), 0.0)


def _shift_up(x, k, t_idx, s):
    if k == 0:
        return x
    return jnp.where(t_idx < s - k, pltpu.roll(x, s - k, 0), 0.0)


CONV_A_BLOCK = 256


def _conv_a_fwd(proj, w, nb, s, d, name):
    cb = _tile(d, CONV_A_BLOCK, LANES)

    def body(x_ref, w_ref, o_ref):
        t_idx = lax.broadcasted_iota(jnp.int32, (s, cb), 0)
        cv = x_ref[:, cb:2 * cb].astype(F32) * x_ref[:, 2 * cb:3 * cb].astype(F32)
        cc = sum(w_ref[k:k + 1, :] * _shift_down(cv, CONV_A_K - 1 - k, t_idx) for k in range(CONV_A_K))
        o_ref[...] = (x_ref[:, 0:cb].astype(F32) * cc).astype(BF16)

    return pl.pallas_call(
        body, name=name, grid=(nb, d // cb),
        in_specs=[pl.BlockSpec((s, 3 * cb), lambda b, j: (b, j)), pl.BlockSpec((8, cb), lambda b, j: (0, j))],
        out_specs=pl.BlockSpec((s, cb), lambda b, j: (b, j)),
        out_shape=jax.ShapeDtypeStruct((nb * s, d), BF16), compiler_params=_params("parallel", "parallel"),
    )(proj, w)


def _conv_a_bwd(dya, w_out, proj, w, nb, s, d, out_cols, name):
    cb = _tile(d, CONV_A_BLOCK, LANES)

    def body(dya_ref, wo_ref, x_ref, w_ref, o_ref, dw_ref, dy_ref):
        dy_ref[...] = _dot(dya_ref[...], wo_ref[...], NT)
        t_idx = lax.broadcasted_iota(jnp.int32, (s, cb), 0)
        cv_c, cv_v = x_ref[:, cb:2 * cb].astype(F32), x_ref[:, 2 * cb:3 * cb].astype(F32)
        cv = cv_c * cv_v
        shifted = [_shift_down(cv, CONV_A_K - 1 - k, t_idx) for k in range(CONV_A_K)]
        cc = sum(w_ref[k:k + 1, :] * shifted[k] for k in range(CONV_A_K))
        dyv = dy_ref[...].astype(F32)
        o_ref[:, 0:cb] = (dyv * cc).astype(BF16)
        dcc = dyv * x_ref[:, 0:cb].astype(F32)
        dcv = sum(w_ref[k:k + 1, :] * _shift_up(dcc, CONV_A_K - 1 - k, t_idx, s) for k in range(CONV_A_K))
        o_ref[:, cb:2 * cb] = (dcv * cv_v).astype(BF16)
        o_ref[:, 2 * cb:3 * cb] = (dcv * cv_c).astype(BF16)
        rows = [jnp.sum(dcc * shifted[k], axis=0, keepdims=True) for k in range(CONV_A_K)]
        part = _stack_rows(rows, cb)

        @pl.when(pl.program_id(1) == 0)
        def _():
            dw_ref[...] = part

        @pl.when(pl.program_id(1) > 0)
        def _():
            dw_ref[...] += part

    wspec = pl.BlockSpec((8, cb), lambda j, b: (0, j))
    wide = pl.BlockSpec((s, 3 * cb), lambda j, b: (b, j))
    return pl.pallas_call(
        body, name=name, grid=(d // cb, nb),
        in_specs=[pl.BlockSpec((s, d), lambda j, b: (b, 0)), pl.BlockSpec((cb, d), lambda j, b: (j, 0)), wide, wspec],
        out_specs=[wide, wspec],
        out_shape=[jax.ShapeDtypeStruct((nb * s, out_cols), BF16), jax.ShapeDtypeStruct((8, d), F32)],
        scratch_shapes=[pltpu.VMEM((s, cb), F32)],
        compiler_params=_params("parallel", "arbitrary"),
    )(dya, w_out, proj, w)


def _conv_s_fwd(proj, col0, w, bias, nb, s, cc_width, name, cb=256):
    cb = _tile(math.gcd(cc_width, col0) if col0 else cc_width, cb, LANES)
    nd, off = cc_width // cb, col0 // cb

    def body(x_ref, w_ref, b_ref, o_ref, pre_ref):
        t_idx = lax.broadcasted_iota(jnp.int32, (s, cb), 0)
        xv = x_ref[...].astype(F32)
        pre = b_ref[...] + sum(w_ref[k:k + 1, :] * _shift_down(xv, SSM_CONV_K - 1 - k, t_idx) for k in range(SSM_CONV_K))
        o_ref[...] = _silu(pre).astype(BF16)
        pre_ref[...] = pre.astype(BF16)

    vec = pl.BlockSpec((8, cb), lambda b, j: (0, j))
    own = pl.BlockSpec((s, cb), lambda b, j: (b, j))
    out = jax.ShapeDtypeStruct((nb * s, cc_width), BF16)
    return pl.pallas_call(
        body, name=name, grid=(nb, nd),
        in_specs=[pl.BlockSpec((s, cb), lambda b, j: (b, j + off)), vec, pl.BlockSpec((1, cb), lambda b, j: (0, j))],
        out_specs=[own, own], out_shape=[out, out], compiler_params=_params("parallel", "parallel"),
    )(proj, w, bias)


def _conv_s_bwd(dxc, pre, proj, col0, w, nb, s, cc_width, into, name, cb=256):
    cb = _tile(math.gcd(cc_width, col0) if col0 else cc_width, cb, LANES)
    nd, off = cc_width // cb, col0 // cb

    def body(d_ref, pre_ref, x_ref, w_ref, into_ref, dx_ref, dw_ref, db_ref):
        t_idx = lax.broadcasted_iota(jnp.int32, (s, cb), 0)
        xv = x_ref[...].astype(F32)
        dpre = d_ref[...].astype(F32) * _dsilu(pre_ref[...].astype(F32))
        ahead = [_shift_up(dpre, j, t_idx, s) for j in range(SSM_CONV_K)]
        dx_ref[...] = sum(w_ref[k:k + 1, :] * ahead[SSM_CONV_K - 1 - k] for k in range(SSM_CONV_K)).astype(BF16)
        rows = [jnp.sum(ahead[SSM_CONV_K - 1 - k] * xv, axis=0, keepdims=True) for k in range(SSM_CONV_K)]
        dw_part = _stack_rows(rows, cb)
        db_part = jnp.sum(dpre, axis=0, keepdims=True)

        @pl.when(pl.program_id(1) == 0)
        def _():
            dw_ref[...] = dw_part
            db_ref[...] = db_part

        @pl.when(pl.program_id(1) > 0)
        def _():
            dw_ref[...] += dw_part
            db_ref[...] += db_part

    own = pl.BlockSpec((s, cb), lambda j, b: (b, j))
    shifted = pl.BlockSpec((s, cb), lambda j, b: (b, j + off))
    wspec = pl.BlockSpec((8, cb), lambda j, b: (0, j))
    bspec = pl.BlockSpec((1, cb), lambda j, b: (0, j))
    return pl.pallas_call(
        body, name=name, grid=(nd, nb),
        in_specs=[own, own, shifted, wspec, pl.BlockSpec(memory_space=pl.ANY)],
        out_specs=[shifted, wspec, bspec],
        out_shape=[jax.ShapeDtypeStruct(into.shape, into.dtype), jax.ShapeDtypeStruct((8, cc_width), F32),
                   jax.ShapeDtypeStruct((1, cc_width), F32)],
        input_output_aliases={4: 0},
        compiler_params=_params("parallel", "arbitrary"),
    )(dxc, pre, proj, w, into)


def _split3(v):
    hi = v.astype(BF16)
    r1 = v - hi.astype(F32)
    mid = r1.astype(BF16)
    return hi, mid, (r1 - mid.astype(F32)).astype(BF16)


def _exact_left(mask_b, v):
    return sum(_dot(mask_b, t) for t in _split3(v))


def _exact_right(v, mask_b):
    return sum(_dot(t, mask_b) for t in _split3(v))


def _head_sums(v, e_b):
    return _dot(v.astype(BF16), e_b, NT)


def _spread(v, out_ref, di):
    lane = lax.broadcasted_iota(jnp.int32, (v.shape[0], LANES), 1)
    for pr in range(di // LANES):
        h0 = pr * (LANES // SSM_HEAD_DIM)
        out_ref[:, pr * LANES:(pr + 1) * LANES] = jnp.where(lane < SSM_HEAD_DIM, v[:, h0:h0 + 1], v[:, h0 + 1:h0 + 2])


def _ssd_common(xc_ref, dtr_ref, dtrt_ref, prow_ref, pcol_ref, dtx_ref, acsx_ref, dx_ref, di):
    l = SSM_CHUNK
    bias_r, a_r = prow_ref[0:1, :], -jnp.exp(prow_ref[1:2, :])
    sp_in = dtr_ref[...] + bias_r
    dt = _softplus(sp_in)
    li = lax.broadcasted_iota(jnp.int32, (l, l), 0)
    si = lax.broadcasted_iota(jnp.int32, (l, l), 1)
    lower_b = (li >= si).astype(BF16)
    upper_b = (li <= si).astype(BF16)
    acs = _exact_left(lower_b, dt * a_r)
    bias_c, a_c = pcol_ref[:, 0:1], -jnp.exp(pcol_ref[:, 1:2])
    dt_t = _softplus(dtrt_ref[...] + bias_c)
    acs_t = _exact_right(dt_t * a_c, upper_b)
    _spread(dt, dtx_ref, di)
    _spread(acs, acsx_ref, di)
    _spread(prow_ref[0:8, :], dx_ref, di)
    acs_exp = acsx_ref[...]
    acs_last = acs_exp[l - 1:l, :]
    x = xc_ref[:, 0:di].astype(F32)
    return dict(dt=dt, a_r=a_r, sp_in=sp_in, acs=acs, acs_t=acs_t, dt_exp=dtx_ref[...], e_exp=jnp.exp(acs_exp),
                el_exp=jnp.exp(acs_last), f_exp=jnp.exp(acs_last - acs_exp), x=x, mask=li >= si, upper_b=upper_b,
                d_exp=dx_ref[2:3, :])


def _decay(q, h):
    seg = q["acs"][:, h:h + 1] - q["acs_t"][h:h + 1, :]
    return jnp.exp(jnp.where(q["mask"], seg, NEG_BIG))


def _ssd_fwd(xc, dtr, dtrt, prow, pcol, nb, nc, di, name):
    l, n, g_n, p = SSM_CHUNK, SSM_STATE, SSM_GROUPS, SSM_HEAD_DIM
    cc = xc.shape[1]
    gw = di // g_n
    assert p * 2 == LANES and gw % LANES == 0

    def body(xc_ref, dtr_ref, dtrt_ref, prow_ref, pcol_ref, y_ref, sprev_ref, st_ref, dtx_ref, acsx_ref, dx_ref):
        @pl.when(pl.program_id(1) == 0)
        def _():
            st_ref[...] = jnp.zeros_like(st_ref)

        q = _ssd_common(xc_ref, dtr_ref, dtrt_ref, prow_ref, pcol_ref, dtx_ref, acsx_ref, dx_ref, di)
        x = q["x"]
        xd = x * q["dt_exp"]
        xdb = xd.astype(BF16)
        xdf = (xd * q["f_exp"]).astype(BF16)
        lane = lax.broadcasted_iota(jnp.int32, (l, LANES), 1)
        for g in range(g_n):
            lo = g * gw
            bg = xc_ref[:, di + g * n: di + (g + 1) * n]
            cg = xc_ref[:, di + g_n * n + g * n: di + g_n * n + (g + 1) * n]
            cb = _dot(cg, bg, NT)
            st_g = st_ref[:, lo:lo + gw]
            y_off = q["e_exp"][:, lo:lo + gw] * _dot(cg, st_g.astype(BF16))
            for pr in range(gw // LANES):
                c0 = lo + pr * LANES
                h0 = c0 // p
                xp = xdb[:, c0:c0 + LANES]
                m0 = (cb * _decay(q, h0)).astype(BF16)
                m1 = (cb * _decay(q, h0 + 1)).astype(BF16)
                yd = _dot(m0, jnp.where(lane < p, xp, 0)) + _dot(m1, jnp.where(lane >= p, xp, 0))
                y_ref[:, c0:c0 + LANES] = (yd + y_off[:, pr * LANES:(pr + 1) * LANES]
                                           + q["d_exp"][:, c0:c0 + LANES] * x[:, c0:c0 + LANES]).astype(BF16)
            sprev_ref[:, lo:lo + gw] = st_g
            st_ref[:, lo:lo + gw] = q["el_exp"][:, lo:lo + gw] * st_g + _dot(bg, xdf[:, lo:lo + gw], TN)

    tok = lambda w: pl.BlockSpec((l, w), lambda b, c: (b * nc + c, 0))
    const = lambda r, w: pl.BlockSpec((r, w), lambda b, c: (0, 0))
    return pl.pallas_call(
        body, name=name, grid=(nb, nc),
        in_specs=[tok(cc), tok(LANES), pl.BlockSpec((LANES, l), lambda b, c: (0, b * nc + c)),
                  const(8, LANES), const(LANES, 8)],
        out_specs=[tok(di), pl.BlockSpec((None, n, di), lambda b, c: (b * nc + c, 0, 0))],
        out_shape=[jax.ShapeDtypeStruct((nb * nc * l, di), BF16), jax.ShapeDtypeStruct((nb * nc, n, di), F32)],
        scratch_shapes=[pltpu.VMEM((n, di), F32), pltpu.VMEM((l, di), F32), pltpu.VMEM((l, di), F32),
                        pltpu.VMEM((8, di), F32)],
        compiler_params=_params("parallel", "arbitrary"),
    )(xc, dtr, dtrt, prow, pcol)


def _ssd_bwd(dy, xc, dtr, dtrt, prow, pcol, e_mat, sprev, nb, nc, di, name):
    l, n, g_n, p = SSM_CHUNK, SSM_STATE, SSM_GROUPS, SSM_HEAD_DIM
    cc = xc.shape[1]
    gw = di // g_n

    def body(dy_ref, xc_ref, dtr_ref, dtrt_ref, prow_ref, pcol_ref, e_ref, sprev_ref,
             dxc_ref, ddtr_ref, sums_ref, dst_ref, off_ref, dxd_ref, last_ref, vst_ref,
             dtx_ref, acsx_ref, dx_ref):
        first = jnp.logical_and(pl.program_id(0) == 0, pl.program_id(1) == 0)

        @pl.when(pl.program_id(1) == 0)
        def _():
            dst_ref[...] = jnp.zeros_like(dst_ref)

        head_row = lax.broadcasted_iota(jnp.int32, (LANES, l), 0)
        row_sums, col_sums = jnp.zeros((l, LANES), F32), jnp.zeros((LANES, l), F32)
        strict_lower = lax.broadcasted_iota(jnp.int32, (l, l), 0) > lax.broadcasted_iota(jnp.int32, (l, l), 1)

        q = _ssd_common(xc_ref, dtr_ref, dtrt_ref, prow_ref, pcol_ref, dtx_ref, acsx_ref, dx_ref, di)
        x = q["x"]
        xd = x * q["dt_exp"]
        xdb = xd.astype(BF16)
        xdf = (xd * q["f_exp"]).astype(BF16)
        dyv = dy_ref[...].astype(F32)
        dyb = dy_ref[...]
        dye = (dyv * q["e_exp"]).astype(BF16)
        upper_b = q["upper_b"]
        lane = lax.broadcasted_iota(jnp.int32, (l, LANES), 1)
        for g in range(g_n):
            lo = g * gw
            bg = xc_ref[:, di + g * n: di + (g + 1) * n]
            cg = xc_ref[:, di + g_n * n + g * n: di + g_n * n + (g + 1) * n]
            cb = _dot(cg, bg, NT)
            st_g = sprev_ref[:, lo:lo + gw]
            st_gb = st_g.astype(BF16)
            dst_g = dst_ref[:, lo:lo + gw]
            dst_gb = dst_g.astype(BF16)
            dye_g = dye[:, lo:lo + gw]
            xdf_g = xdf[:, lo:lo + gw]
            y_off = q["e_exp"][:, lo:lo + gw] * _dot(cg, st_gb)
            dc_g = _dot(dye_g, st_gb, NT)
            db_g = _dot(xdf_g, dst_gb, NT)
            dxd_state = _dot(bg, dst_gb) * q["f_exp"][:, lo:lo + gw]
            last_ref[:, lo:lo + gw] = jnp.sum(dst_g * st_g, axis=0, keepdims=True)
            dst_ref[:, lo:lo + gw] = q["el_exp"][:, lo:lo + gw] * dst_g + _dot(cg, dye_g, TN)
            off_ref[:, lo:lo + gw] = dyv[:, lo:lo + gw] * y_off
            vst_ref[:, lo:lo + gw] = xd[:, lo:lo + gw] * dxd_state
            dcb = jnp.zeros((l, l), F32)
            for pr in range(gw // LANES):
                c0 = lo + pr * LANES
                h0 = c0 // p
                xp = xdb[:, c0:c0 + LANES]
                dyp = dyb[:, c0:c0 + LANES]
                dxd_diag = jnp.zeros((l, LANES), F32)
                for k, keep in enumerate((lane < p, lane >= p)):
                    dec = _decay(q, h0 + k)
                    dy_h = jnp.where(keep, dyp, 0)
                    dm_dec = _dot(dy_h, xp, NT) * dec
                    dcb = dcb + dm_dec
                    dxd_diag = dxd_diag + _dot((cb * dec).astype(BF16), dy_h, TN)
                    qm = dm_dec * cb
                    row_sums = jnp.where(lane == h0 + k, jnp.sum(qm, axis=1, keepdims=True), row_sums)
                    col_sums = jnp.where(head_row == h0 + k, jnp.sum(qm, axis=0, keepdims=True), col_sums)
                dxd_ref[:, c0:c0 + LANES] = dxd_diag + dxd_state[:, pr * LANES:(pr + 1) * LANES]
            dcb_b = dcb.astype(BF16)
            dxc_ref[:, di + g * n: di + (g + 1) * n] = (db_g + _dot(dcb_b, cg, TN)).astype(BF16)
            dxc_ref[:, di + g_n * n + g * n: di + g_n * n + (g + 1) * n] = (dc_g + _dot(dcb_b, bg)).astype(BF16)
        dxd = dxd_ref[...]
        e_b = e_ref[...]
        from_y = _exact_left(upper_b, _head_sums(off_ref[...], e_b) + row_sums - col_sums.T)
        from_s = _exact_left(strict_lower.astype(BF16), _head_sums(vst_ref[...], e_b))
        carried = _head_sums(jnp.broadcast_to(last_ref[...], (8, di)), e_b)[0:1, :] * jnp.exp(q["acs"][l - 1:l, :])
        dla = from_y + from_s + carried
        ddt = dla * q["a_r"] + _head_sums(dxd * x, e_b)
        ddtr = ddt * jax.nn.sigmoid(q["sp_in"])
        ddtr_ref[...] = ddtr
        dxc_ref[:, 0:di] = (dxd * q["dt_exp"] + q["d_exp"] * dyv).astype(BF16)
        dd_exp = jnp.sum(dyv * x, axis=0, keepdims=True)
        dd = _head_sums(jnp.broadcast_to(dd_exp, (8, di)), e_b)[0:1, :]
        part = _stack_rows([jnp.sum(ddtr, axis=0, keepdims=True),
                            jnp.sum(dla * q["dt"], axis=0, keepdims=True) * q["a_r"], dd], LANES)

        @pl.when(first)
        def _():
            sums_ref[...] = part

        @pl.when(jnp.logical_not(first))
        def _():
            sums_ref[...] += part

    rev = lambda b, c: b * nc + (nc - 1 - c)
    tok = lambda w: pl.BlockSpec((l, w), lambda b, c: (rev(b, c), 0))
    const = lambda r, w: pl.BlockSpec((r, w), lambda b, c: (0, 0))
    return pl.pallas_call(
        body, name=name, grid=(nb, nc),
        in_specs=[tok(di), tok(cc), tok(LANES), pl.BlockSpec((LANES, l), lambda b, c: (0, rev(b, c))),
                  const(8, LANES), const(LANES, 8), const(LANES, di),
                  pl.BlockSpec((None, n, di), lambda b, c: (rev(b, c), 0, 0))],
        out_specs=[tok(cc), tok(LANES), const(8, LANES)],
        out_shape=[jax.ShapeDtypeStruct((nb * nc * l, cc), BF16), jax.ShapeDtypeStruct((nb * nc * l, LANES), F32),
                   jax.ShapeDtypeStruct((8, LANES), F32)],
        scratch_shapes=[pltpu.VMEM((n, di), F32), pltpu.VMEM((l, di), F32), pltpu.VMEM((l, di), F32),
                        pltpu.VMEM((1, di), F32), pltpu.VMEM((l, di), F32),
                        pltpu.VMEM((l, di), F32), pltpu.VMEM((l, di), F32), pltpu.VMEM((8, di), F32)],
        compiler_params=_params("arbitrary", "arbitrary"),
    )(dy, xc, dtr, dtrt, prow, pcol, e_mat, sprev)


def _gate_norm_out(y, proj, z_col0, norm_g, w_out, di, name, tt=512):
    t = y.shape[0]
    d_out = w_out.shape[1]
    tt = _tile(t, tt, 8)
    gw = di // SSM_GROUPS
    zw = _tile(math.gcd(di, z_col0), di, LANES)
    nz, zoff = di // zw, z_col0 // zw

    def body(*refs):
        y_ref, z_refs, g_ref, w_ref, o_ref, yb_ref = refs[0], refs[1:1 + nz], refs[1 + nz], refs[2 + nz], refs[3 + nz], refs[4 + nz]
        for g in range(SSM_GROUPS):
            lo = g * gw
            zv = z_refs[lo // zw][:, lo % zw:lo % zw + gw].astype(F32)
            yg = y_ref[:, lo:lo + gw].astype(F32) * _silu(zv)
            r = lax.rsqrt(jnp.mean(yg * yg, axis=-1, keepdims=True) + EPS)
            o_ref[:, lo:lo + gw] = (yg * r * g_ref[:, lo:lo + gw]).astype(BF16)
        yb_ref[...] = _dot(o_ref[...], w_ref[...]).astype(BF16)

    row = pl.BlockSpec((tt, di), lambda i: (i, 0))
    zspecs = [pl.BlockSpec((tt, zw), functools.partial(lambda i, k: (i, zoff + k), k=k)) for k in range(nz)]
    return pl.pallas_call(
        body, name=name, grid=(t // tt,),
        in_specs=[row] + zspecs + [pl.BlockSpec((1, di), lambda i: (0, 0)), pl.BlockSpec((di, d_out), lambda i: (0, 0))],
        out_specs=[row, pl.BlockSpec((tt, d_out), lambda i: (i, 0))],
        out_shape=[jax.ShapeDtypeStruct((t, di), BF16), jax.ShapeDtypeStruct((t, d_out), BF16)],
        compiler_params=_params("parallel"),
    )(y, *([proj] * nz), norm_g, w_out)


def _gate_norm_bwd(dyb, w_out, y, proj, z_col0, norm_g, di, name, tt=256):
    t = y.shape[0]
    d_out = w_out.shape[1]
    tt = _tile(t, tt, 8)
    gw = di // SSM_GROUPS
    zw = _tile(math.gcd(di, z_col0), di, LANES)
    nz, zoff = di // zw, z_col0 // zw

    def body(*refs):
        dyb_ref, w_ref, y_ref, z_refs, g_ref = refs[0], refs[1], refs[2], refs[3:3 + nz], refs[3 + nz]
        dy_ref, dz_ref, dg_ref, dn_ref = refs[4 + nz:]
        dn_ref[...] = _dot(dyb_ref[...], w_ref[...], NT)
        first = pl.program_id(0) == 0
        for g in range(SSM_GROUPS):
            lo = g * gw
            zv = z_refs[lo // zw][:, lo % zw:lo % zw + gw].astype(F32)
            yv = y_ref[:, lo:lo + gw].astype(F32)
            sz = _silu(zv)
            yg = yv * sz
            r = lax.rsqrt(jnp.mean(yg * yg, axis=-1, keepdims=True) + EPS)
            yh = yg * r
            dnv = dn_ref[:, lo:lo + gw].astype(F32)
            gy = dnv * g_ref[:, lo:lo + gw]
            dyg = r * (gy - yh * jnp.mean(gy * yh, axis=-1, keepdims=True))
            dy_ref[:, lo:lo + gw] = (dyg * sz).astype(BF16)
            dz_ref[:, lo:lo + gw] = (dyg * yv * _dsilu(zv)).astype(BF16)
            part = jnp.sum(dnv * yh, axis=0, keepdims=True)

            @pl.when(first)
            def _():
                dg_ref[:, lo:lo + gw] = part

            @pl.when(jnp.logical_not(first))
            def _():
                dg_ref[:, lo:lo + gw] += part

    row = pl.BlockSpec((tt, di), lambda i: (i, 0))
    vec = pl.BlockSpec((1, di), lambda i: (0, 0))
    zspecs = [pl.BlockSpec((tt, zw), functools.partial(lambda i, k: (i, zoff + k), k=k)) for k in range(nz)]
    return pl.pallas_call(
        body, name=name, grid=(t // tt,),
        in_specs=[pl.BlockSpec((tt, d_out), lambda i: (i, 0)), pl.BlockSpec((di, d_out), lambda i: (0, 0)), row] + zspecs + [vec],
        out_specs=[row, row, vec],
        out_shape=[jax.ShapeDtypeStruct((t, di), BF16), jax.ShapeDtypeStruct((t, di), BF16), jax.ShapeDtypeStruct((1, di), F32)],
        scratch_shapes=[pltpu.VMEM((tt, di), F32)],
        compiler_params=_params("arbitrary"),
    )(dyb, w_out, y, *([proj] * nz), norm_g)


def _softmax_rows(s):
    s = s - jnp.max(s, axis=-1, keepdims=True)
    e = jnp.exp(s)
    return e * (1.0 / jnp.sum(e, axis=-1, keepdims=True))


def _xattn_fwd(q, kv, nb, s, m, d, name, tq=1024):
    tq = _tile(s, tq, 8)
    nq = s // tq
    hd = d // XATTN_HEADS
    scale = 1.0 / math.sqrt(hd)

    def body(q_ref, k_ref, v_ref, o_ref):
        for h in range(XATTN_HEADS):
            sl = slice(h * hd, (h + 1) * hd)
            prob = _softmax_rows(_dot(q_ref[:, sl], k_ref[:, sl], NT) * scale)
            o_ref[:, sl] = _dot(prob.astype(BF16), v_ref[:, sl]).astype(BF16)

    return pl.pallas_call(
        body, name=name, grid=(nb, nq),
        in_specs=[pl.BlockSpec((tq, d), lambda b, i: (b * nq + i, 0)), pl.BlockSpec((m, d), lambda b, i: (b, 0)),
                  pl.BlockSpec((m, d), lambda b, i: (b, 1))],
        out_specs=pl.BlockSpec((tq, d), lambda b, i: (b * nq + i, 0)),
        out_shape=jax.ShapeDtypeStruct((nb * s, d), BF16), compiler_params=_params("parallel", "parallel"),
    )(q, kv, kv)


def _xattn_bwd(do, q, kv, nb, s, m, d, name, tq=1024):
    tq = _tile(s, tq, 8)
    nq = s // tq
    hd = d // XATTN_HEADS
    scale = 1.0 / math.sqrt(hd)

    def body(do_ref, q_ref, k_ref, v_ref, dq_ref, dk_ref, dv_ref):
        first = pl.program_id(1) == 0
        for h in range(XATTN_HEADS):
            sl = slice(h * hd, (h + 1) * hd)
            qh, kh, vh, doh = q_ref[:, sl], k_ref[:, sl], v_ref[:, sl], do_ref[:, sl]
            prob = _softmax_rows(_dot(qh, kh, NT) * scale)
            dv_h = _dot(prob.astype(BF16), doh, TN)
            dp = _dot(doh, vh, NT)
            ds = (prob * (dp - jnp.sum(dp * prob, axis=-1, keepdims=True)) * scale).astype(BF16)
            dq_ref[:, sl] = _dot(ds, kh).astype(BF16)
            dk_h = _dot(ds, qh, TN)

            @pl.when(first)
            def _():
                dk_ref[:, sl] = dk_h
                dv_ref[:, sl] = dv_h

            @pl.when(jnp.logical_not(first))
            def _():
                dk_ref[:, sl] += dk_h
                dv_ref[:, sl] += dv_h

    qspec = pl.BlockSpec((tq, d), lambda b, i: (b * nq + i, 0))
    dq, dk, dv = pl.pallas_call(
        body, name=name, grid=(nb, nq),
        in_specs=[qspec, qspec, pl.BlockSpec((m, d), lambda b, i: (b, 0)), pl.BlockSpec((m, d), lambda b, i: (b, 1))],
        out_specs=[qspec, pl.BlockSpec((m, d), lambda b, i: (b, 0)), pl.BlockSpec((m, d), lambda b, i: (b, 0))],
        out_shape=[jax.ShapeDtypeStruct((nb * s, d), BF16), jax.ShapeDtypeStruct((nb * m, d), F32),
                   jax.ShapeDtypeStruct((nb * m, d), F32)],
        compiler_params=_params("parallel", "arbitrary"),
    )(do, q, kv, kv)
    return dq, dk, dv


def _all_gather(shards, name):
    n_arr = len(shards)

    def body(*refs):
        x_refs, out_refs = refs[:n_arr], refs[n_arr:2 * n_arr]
        send_sems, recv_sems, local_sems = refs[2 * n_arr:]
        x, y, c = lax.axis_index("x"), lax.axis_index("y"), lax.axis_index("c")
        me, sibling = (x, y, c), (x, y, 1 - c)
        chips = [(1 - x, y), (x, 1 - y), (1 - x, 1 - y)]

        def copy(w, k, block, to, from_input=False):
            px, py, pc = block
            rows = out_refs[w].at[4 * px + 2 * py + pc]
            return pltpu.make_async_remote_copy(
                src_ref=x_refs[w] if from_input else rows, dst_ref=rows,
                send_sem=send_sems.at[7 * w + k], recv_sem=recv_sems.at[7 * w + k], device_id=to, device_id_type=MESH)

        started = []
        for w in range(n_arr):
            mine = pltpu.make_async_copy(x_refs[w], out_refs[w].at[4 * x + 2 * y + c], local_sems.at[w])
            mine.start()
            started.append(mine)
        sends = []
        for w in range(n_arr):
            sends.append(copy(w, 0, me, sibling, from_input=True))
            sends += [copy(w, 1 + j, me, (*chip, c), from_input=True) for j, chip in enumerate(chips)]
        for cp in sends:
            cp.start()
        for j, chip in enumerate(chips):
            for w in range(n_arr):
                copy(w, 1 + j, (*chip, c), me).wait_recv()
                passed = copy(w, 4 + j, (*chip, c), sibling)
                passed.start()
                sends.append(passed)
        for w in range(n_arr):
            copy(w, 0, sibling, me).wait_recv()
            for j, chip in enumerate(chips):
                copy(w, 4 + j, (*chip, 1 - c), me).wait_recv()
        for cp in sends:
            cp.wait_send()
        for mine in started:
            mine.wait()

    hbm = pl.BlockSpec(memory_space=pl.ANY)
    return pl.pallas_call(
        body, name=name, out_shape=[jax.ShapeDtypeStruct((N_DEV,) + s.shape, s.dtype) for s in shards],
        in_specs=[hbm] * n_arr, out_specs=[hbm] * n_arr,
        scratch_shapes=[pltpu.SemaphoreType.DMA((7 * n_arr,)), pltpu.SemaphoreType.DMA((7 * n_arr,)),
                        pltpu.SemaphoreType.DMA((n_arr,))],
    )(*shards)


_HBM = pl.BlockSpec(memory_space=pltpu.HBM)
_SEM = pl.BlockSpec(memory_space=pltpu.SEMAPHORE)
_DATAFLOW = pltpu.SideEffectType.DATAFLOW_SIDE_EFFECTING


def _peer_list(x, y, c):
    return [(1 - x if k & 4 else x, 1 - y if k & 2 else y, 1 - c if k & 1 else c) for k in range(1, N_DEV)]


def _push_copy(src_ref, land_ref, send_sems, recv_sems, w, k, peer, me, per_peer_src, receiving):
    px, py, pc = peer
    peer_slot = 4 * px + 2 * py + pc
    return pltpu.make_async_remote_copy(
        src_ref=src_ref.at[peer_slot] if per_peer_src else src_ref,
        dst_ref=land_ref.at[peer_slot if receiving else me],
        send_sem=send_sems.at[7 * w + k], recv_sem=recv_sems.at[7 * w + k], device_id=peer, device_id_type=MESH)


def _push_start(srcs, per_peer_src, after, name):
    n_arr = len(srcs)
    land_shapes = [s.shape if per_peer_src else (N_DEV,) + s.shape for s in srcs]

    def body(*refs):
        src_refs, land_refs = refs[:n_arr], refs[n_arr:2 * n_arr]
        send_sems, recv_sems = refs[2 * n_arr + 1], refs[2 * n_arr + 2]
        token = refs[-1]
        x, y, c = lax.axis_index("x"), lax.axis_index("y"), lax.axis_index("c")
        me = 4 * x + 2 * y + c
        for w in range(n_arr):
            for k, peer in enumerate(_peer_list(x, y, c)):
                _push_copy(src_refs[w], land_refs[w], send_sems, recv_sems, w, k, peer, me, per_peer_src, False).start()
        token[...] = jnp.zeros_like(token)

    lands = [pltpu.with_memory_space_constraint(lax.empty(ls, s.dtype), pltpu.HBM) for ls, s in zip(land_shapes, srcs)]
    srcs_hbm = [pltpu.with_memory_space_constraint(s, pltpu.HBM) for s in srcs]
    out = pl.pallas_call(
        body, name=name,
        out_shape=(pltpu.SemaphoreType.DMA((7 * n_arr,)), pltpu.SemaphoreType.DMA((7 * n_arr,)),
                   *[pltpu.HBM(s.shape, s.dtype) for s in srcs], *[pltpu.HBM(ls, s.dtype) for ls, s in zip(land_shapes, srcs)],
                   jax.ShapeDtypeStruct((8, LANES), F32)),
        in_specs=[_HBM] * (2 * n_arr) + [pl.BlockSpec(memory_space=pl.ANY)],
        out_specs=(_SEM, _SEM, *([_HBM] * (2 * n_arr)), pl.BlockSpec(memory_space=pltpu.VMEM)),
        input_output_aliases={i: 2 + i for i in range(2 * n_arr)},
        compiler_params=pltpu.CompilerParams(has_side_effects=_DATAFLOW),
    )(*srcs_hbm, *lands, after)
    return dict(send=out[0], recv=out[1], srcs=list(out[2:2 + n_arr]), lands=list(out[2 + n_arr:2 + 2 * n_arr]),
                token=out[-1])


def _push_wait(pending, per_peer_src, after, name):
    n_arr = len(pending["srcs"])

    def body(*refs):
        src_refs, land_refs = refs[:n_arr], refs[n_arr:2 * n_arr]
        send_sems, recv_sems = refs[2 * n_arr], refs[2 * n_arr + 1]
        x, y, c = lax.axis_index("x"), lax.axis_index("y"), lax.axis_index("c")
        me = 4 * x + 2 * y + c
        for w in range(n_arr):
            for k, peer in enumerate(_peer_list(x, y, c)):
                cp = _push_copy(src_refs[w], land_refs[w], send_sems, recv_sems, w, k, peer, me, per_peer_src, True)
                cp.wait_send()
                cp.wait_recv()

    out = pl.pallas_call(
        body, name=name,
        out_shape=tuple(pltpu.HBM(a.shape, a.dtype) for a in pending["srcs"] + pending["lands"]),
        in_specs=[_HBM] * (2 * n_arr) + [_SEM, _SEM, pl.BlockSpec(memory_space=pl.ANY)],
        out_specs=tuple([_HBM] * (2 * n_arr)),
        input_output_aliases={i: i for i in range(2 * n_arr)},
        compiler_params=pltpu.CompilerParams(has_side_effects=_DATAFLOW),
    )(*pending["srcs"], *pending["lands"], pending["send"], pending["recv"], after)
    return list(out[:n_arr]), list(out[n_arr:])


def _adamw_math(w, g, m, v):
    m = ADAM_B1 * m + (1.0 - ADAM_B1) * g
    v = ADAM_B2 * v + (1.0 - ADAM_B2) * (g * g)
    m_hat = m / (1.0 - ADAM_B1 ** ADAM_STEP)
    v_hat = v / (1.0 - ADAM_B2 ** ADAM_STEP)
    delta = -ADAM_LR * (m_hat / (jnp.sqrt(v_hat) + ADAM_EPS) + ADAM_WD * w)
    return delta, m, v


def _sum8(parts, name, tr=512):
    _, r, c_dim = parts.shape
    tr = _tile(r, tr, BF16_SUBLANES)

    def body(p_ref, o_ref):
        acc = p_ref[0].astype(F32)
        for k in range(1, N_DEV):
            acc = acc + p_ref[k].astype(F32)
        o_ref[...] = acc

    return pl.pallas_call(
        body, name=name, grid=(r // tr,), in_specs=[pl.BlockSpec((N_DEV, tr, c_dim), lambda i: (0, i, 0))],
        out_specs=pl.BlockSpec((tr, c_dim), lambda i: (i, 0)),
        out_shape=jax.ShapeDtypeStruct((r, c_dim), F32), compiler_params=_params("parallel"),
    )(parts)


def _sum8_adamw(parts, w, m, v, name, tr=128):
    _, r, c_dim = parts.shape
    tr = _tile(r, tr, BF16_SUBLANES)
    tc = c_dim if tr <= 2 * LANES else _tile(c_dim, LANES, LANES)

    def body(p_ref, w_ref, m_ref, v_ref, g_ref, d_ref, nm_ref, nv_ref):
        g = p_ref[0].astype(F32)
        for k in range(1, N_DEV):
            g = g + p_ref[k].astype(F32)
        g_ref[...] = g
        d_ref[...], nm_ref[...], nv_ref[...] = _adamw_math(w_ref[...], g, m_ref[...], v_ref[...])

    blk = pl.BlockSpec((None, tr, tc), lambda i, j: (0, i, j))
    out = jax.ShapeDtypeStruct((1, r, c_dim), F32)
    return pl.pallas_call(
        body, name=name, grid=(r // tr, c_dim // tc),
        in_specs=[pl.BlockSpec((N_DEV, tr, tc), lambda i, j: (0, i, j)), blk, blk, blk],
        out_specs=[blk] * 4, out_shape=[out] * 4, compiler_params=_params("parallel", "parallel"),
    )(parts, w, m, v)


def _adamw(g, w, m, v, name):
    r, c_dim = g.shape

    def body(g_ref, w_ref, m_ref, v_ref, d_ref, nm_ref, nv_ref):
        d_ref[...], nm_ref[...], nv_ref[...] = _adamw_math(w_ref[...], g_ref[...], m_ref[...], v_ref[...])

    out = jax.ShapeDtypeStruct((r, c_dim), F32)
    return pl.pallas_call(body, name=name, out_shape=[out] * 3)(g, w, m, v)


def _pack_rows(arrays, dtype, row_unit):
    chunks, offs, r0 = [], [], 0
    for a in arrays:
        flat = a.reshape(-1).astype(dtype)
        rows = -(-flat.shape[0] // (LANES * row_unit)) * row_unit
        flat = jnp.pad(flat, (0, rows * LANES - flat.shape[0]))
        chunks.append(flat.reshape(rows, LANES))
        offs.append((r0, rows))
        r0 += rows
    return jnp.concatenate(chunks, axis=0), offs


def _unpack_rows(packed, offs, shapes):
    out = []
    for (r0, rows), shape in zip(offs, shapes):
        n = math.prod(shape)
        blk = packed[..., r0:r0 + rows, :]
        blk = blk.reshape(packed.shape[:-2] + (rows * LANES,))[..., :n]
        out.append(blk.reshape(packed.shape[:-2] + tuple(shape)))
    return out


def _full_from_slots(blk, col_sharded):
    _, r, c = blk.shape
    if col_sharded:
        return blk.transpose(1, 0, 2).reshape(r, N_DEV * c)
    return blk.reshape(N_DEV * r, c)


def _ffn_fwd(h, n, w_gu_t, w_d, tag, **tail):
    gate, up, a = _ffn_up(n, w_gu_t, f"{tag}_up")
    return _mm(a, w_d, F32, f"{tag}_down", res=h, alpha=FFN_RES_WEIGHT, **tail), (h, n, gate, up, a)


def _ffn_bwd(dh_out, saved, g, w_gu_t, w_d, tag, dep, send_grads):
    h, n, gate, up, a = saved
    dw_d = _mm_tn(a, dh_out, f"{tag}_dw_down", alpha=FFN_RES_WEIGHT, dep=dep)
    dgate, dup = _ffn_da(dh_out, w_d, gate, up, f"{tag}_da", FFN_RES_WEIGHT)
    f = dgate.shape[1]
    dw_gu_t = _mm_tn(dgate, n, f"{tag}_dw_gate", out_rows=2 * f)
    dw_gu_t = _mm_tn(dup, n, f"{tag}_dw_up", out_rows=2 * f, row_off=f, into=dw_gu_t)
    dep = send_grads(dw_gu_t, dw_d)
    return _mm([dgate, dup], w_gu_t, F32, f"{tag}_dn", dep=dep, tk=1408, res=dh_out, norm_bwd=(h, g))


W_GROUPS = (("ffn1_w_gate_up", "ffn1_w_down"),
            ("w_in",),
            ("w_out_a", "w_out_ssm", "w_mix_out"),
            ("w_q", "w_kv", "w_o_x", "ffn2_w_gate_up", "ffn2_w_down"))
G_GROUPS = (("ffn2_w_gate_up", "ffn2_w_down"),
            ("w_o_x", "w_q", "w_kv", "w_mix_out", "w_out_a", "w_out_ssm", "w_in"),
            ("ffn1_w_gate_up", "ffn1_w_down"))


def _local_step(x3, mem3, target3, small, comm):
    nb, s, d = x3.shape
    m_len = mem3.shape[1]
    t = nb * s
    nc = s // SSM_CHUNK
    di = small["ssm_norm"].shape[1]
    hs = di // SSM_HEAD_DIM
    cc = di + 2 * SSM_GROUPS * SSM_STATE
    x, mem, target = x3.reshape(t, d), mem3.reshape(nb * m_len, d), target3.reshape(t, d)

    sizes = (d, d, d, di, cc, hs, d, d)
    offs = [0]
    for sz in sizes:
        offs.append(offs[-1] + sz)
    cb_a = _tile(d, CONV_A_BLOCK, LANES)
    xbc_col0, z_col0 = 3 * d, 3 * d + cc
    ga_blk, gb_blk = (3 * d + di + cc) // d, (4 * d + di + cc) // d

    pad_vec = lambda v: jnp.pad(v.reshape(1, -1), ((0, 0), (0, LANES - hs)))
    prow = jnp.concatenate([pad_vec(small["ssm_dt_bias"]), pad_vec(small["ssm_a_log"]), pad_vec(small["ssm_d"]),
                            jnp.zeros((5, LANES), F32)], axis=0)
    pcol = prow.T
    e_mat = (lax.broadcasted_iota(jnp.int32, (LANES, di), 0)
             == lax.broadcasted_iota(jnp.int32, (LANES, di), 1) // SSM_HEAD_DIM).astype(BF16)

    wts, dep = comm.weights(0, None)
    conv_a_w8 = jnp.pad(wts["conv_a_w"][0], ((0, 8 - CONV_A_K), (0, 0)))
    ssm_conv_w8 = jnp.pad(wts["ssm_conv_w"][0], ((0, 8 - SSM_CONV_K), (0, 0)))
    n1 = _rms_fwd(x, small["ffn1_norm"] + dep[0, 0], "ffn1_norm")
    (h1, u), ffn1_saved = _ffn_fwd(x, n1, wts["ffn1_w_gate_up"], wts["ffn1_w_down"], "ffn1", norm_out=small["mix_norm"])
    got, dep = comm.weights(1, h1)
    wts.update(got)
    w_in_t = wts["w_in"]
    conv_rows = [w_in_t[offs[i] + j * cb_a:offs[i] + (j + 1) * cb_a] for j in range(d // cb_a) for i in (0, 1, 2)]
    w_main_t = jnp.concatenate(conv_rows + [w_in_t[offs[i]:offs[i + 1]] for i in (4, 3, 6, 7)], axis=0)
    w_dt_t = jnp.pad(w_in_t[offs[5]:offs[6]], ((0, LANES - hs), (0, 0)))
    proj = _mm(u, w_main_t, BF16, "in_proj", nt=True, dep=dep)
    dtr = _mm(u, w_dt_t, F32, "in_proj_dt", nt=True)
    yap = _conv_a_fwd(proj, conv_a_w8, nb, s, d, "conv_a")
    got, dep = comm.weights(2, yap)
    wts.update(got)
    y_a = _mm(yap, wts["w_out_a"], BF16, "out_a", dep=dep)
    xc, conv_pre = _conv_s_fwd(proj, xbc_col0, ssm_conv_w8, small["ssm_conv_b"] + dep[0, 0], nb, s, cc, "conv_s")
    dtrt = dtr.T
    y_ssd, sprev = _ssd_fwd(xc, dtr, dtrt, prow, pcol, nb, nc, di, "ssd")
    ygn, y_b = _gate_norm_out(y_ssd, proj, z_col0, small["ssm_norm"], wts["w_out_ssm"], di, "gate_norm_out")
    merged, h2, un = _merge_mix_out(y_a, y_b, proj, ga_blk, gb_blk, wts["w_mix_out"], h1, small["xattn_norm"], "merge_mix_out")
    got, _ = comm.weights(3, h2)
    wts.update(got)
    mn = _rms_fwd(mem, small["mem_norm"], "mem_norm")
    q = _mm(un, wts["w_q"], BF16, "q_proj")
    kv = _mm(mn, wts["w_kv"], BF16, "kv_proj", nt=True)
    o = _xattn_fwd(q, kv, nb, s, m_len, d, "xattn")
    h3, n2 = _mm(o, wts["w_o_x"], F32, "o_proj", res=h2, norm_out=small["ffn2_norm"])
    (dh4, dg_final, loss_vec), ffn2_saved = _ffn_fwd(h3, n2, wts["ffn2_w_gate_up"], wts["ffn2_w_down"], "ffn2",
                                                     loss_head=(small["final_norm"].reshape(1, d), target))

    grads = {"final_norm": dg_final.reshape(d)}
    big = {}
    dh3, grads["ffn2_norm"] = _ffn_bwd(
        dh4, ffn2_saved, small["ffn2_norm"], wts["ffn2_w_gate_up"], wts["ffn2_w_down"], "ffn2", None,
        lambda dw_gu_t, dw_d: comm.grads(0, {"ffn2_w_gate_up": dw_gu_t, "ffn2_w_down": dw_d}))
    big["w_o_x"] = _mm_tn(o, dh3, "dw_o")
    do = _mm(dh3, wts["w_o_x"], BF16, "d_o", nt=True)
    dq, dk, dv = _xattn_bwd(do, q, kv, nb, s, m_len, d, "xattn_bwd")
    big["w_q"] = _mm_tn(un, dq, "dw_q")
    big["w_kv"] = _mm_tn(dv, mn, "dw_v", out_rows=2 * d, row_off=d, into=_mm_tn(dk, mn, "dw_k", out_rows=2 * d))
    _, grads["mem_norm"] = _mm([dk, dv], wts["w_kv"], F32, "d_mn", norm_bwd=(mem, small["mem_norm"]))
    dh2, grads["xattn_norm"] = _mm(dq, wts["w_q"], F32, "d_un", nt=True, res=dh3, norm_bwd=(h2, small["xattn_norm"]))
    big["w_mix_out"] = _mm_tn(merged, dh2, "dw_mix")
    dya, dyb, dg = _merge_bwd(dh2, wts["w_mix_out"], y_a, y_b, proj, ga_blk, gb_blk, d, "merge_bwd")
    big["w_out_a"] = _mm_tn(yap, dya, "dw_out_a")
    big["w_out_ssm"] = _mm_tn(ygn, dyb, "dw_out_ssm")
    dcx, dconv_a = _conv_a_bwd(dya, wts["w_out_a"], proj, conv_a_w8, nb, s, d, 3 * d + cc, "conv_a_bwd")
    dy_ssd, dz, grads["ssm_norm"] = _gate_norm_bwd(dyb, wts["w_out_ssm"], y_ssd, proj, z_col0, small["ssm_norm"], di,
                                                   "gate_norm_bwd")
    dxc, ddtr, ssd_sums = _ssd_bwd(dy_ssd, xc, dtr, dtrt, prow, pcol, e_mat, sprev, nb, nc, di, "ssd_bwd")
    dcx, dconv_s, grads["ssm_conv_b"] = _conv_s_bwd(dxc, conv_pre, proj, xbc_col0, ssm_conv_w8, nb, s, cc, dcx, "conv_s_bwd")
    dw_cx, dw_z, dw_g = _mm_tn(dcx, u, "dw_in_cx"), _mm_tn(dz, u, "dw_in_z"), _mm_tn(dg, u, "dw_in_g")
    dw_dt = _mm_tn(ddtr, u, "dw_in_dt")[:hs]
    du_main = _mm([dcx, dz, dg], w_main_t, F32, "d_u")
    dh1, grads["mix_norm"] = _mm(ddtr, w_dt_t, F32, "d_u_dt", res=dh2, norm_bwd=(h1, small["mix_norm"], du_main))
    bcv = [[dw_cx[(3 * j + i) * cb_a:(3 * j + i + 1) * cb_a] for j in range(d // cb_a)] for i in range(3)]
    big["w_in"] = jnp.concatenate(bcv[0] + bcv[1] + bcv[2] + [dw_z, dw_cx[3 * d:], dw_dt, dw_g], axis=0)
    dep = comm.grads(1, big)
    dx, grads["ffn1_norm"] = _ffn_bwd(
        dh1, ffn1_saved, small["ffn1_norm"], wts["ffn1_w_gate_up"], wts["ffn1_w_down"], "ffn1", dep,
        lambda dw_gu_t, dw_d: comm.grads(2, {"ffn1_w_gate_up": dw_gu_t, "ffn1_w_down": dw_d}))

    grads["conv_a_w"] = dconv_a[:CONV_A_K]
    grads["ssm_conv_w"] = dconv_s[:SSM_CONV_K]
    grads["ssm_dt_bias"] = ssd_sums[0:1, :hs]
    grads["ssm_a_log"] = ssd_sums[1:2, :hs]
    grads["ssm_d"] = ssd_sums[2:3, :hs]
    return loss_vec[0, 0], dx.reshape(nb, s, d), grads


def _step(inputs):
    w = {k: inputs[k] for k in WEIGHT_ORDER}
    mom = {k: inputs["m_" + k] for k in WEIGHT_ORDER}
    vel = {k: inputs["v_" + k] for k in WEIGHT_ORDER}
    me = 4 * lax.axis_index("x") + 2 * lax.axis_index("y") + lax.axis_index("c")

    send = {k: (w[k][0].T if k in COL_SHARDED else w[k][0]).astype(BF16) for k in BIG_WEIGHTS}

    def own_slot(land, mine):
        return lax.dynamic_update_slice(land, mine[None], (me, 0, 0))

    gathers, exchanges = {}, {}

    def weights(i, after):
        conv = {}
        if i == 0:
            packed_c, conv_offs = _pack_rows([w[k][0] for k in SMALL_SHARDED], F32, 8)
            lands = list(_all_gather([send[k] for k in W_GROUPS[0]] + [packed_c], "gather0"))
            conv_blocks = _unpack_rows(lands.pop(), conv_offs, [w[k].shape[1:] for k in SMALL_SHARDED])
            conv = {k: _full_from_slots(b, True)[None] for k, b in zip(SMALL_SHARDED, conv_blocks)}
        else:
            sent, lands = _push_wait(gathers[i], False, after, f"gather{i}_wait")
            lands = [own_slot(land, mine) for land, mine in zip(lands, sent)]
        full = {k: land.reshape(N_DEV * land.shape[1], land.shape[2]) for k, land in zip(W_GROUPS[i], lands)}
        full.update(conv)
        dep = jnp.zeros((8, LANES), F32)
        if i + 1 < len(W_GROUPS):
            gathers[i + 1] = _push_start([send[k] for k in W_GROUPS[i + 1]], False, lands[0], f"gather{i + 1}_start")
            dep = gathers[i + 1]["token"]
        return full, dep

    def slot_shape(k):
        rows, cols = send[k].shape
        return (rows * cols // LANES, LANES) if rows % BF16_SUBLANES else (rows, cols)

    def send_grads(i, by_name):
        slots = [by_name[k].reshape((N_DEV,) + slot_shape(k)) for k in G_GROUPS[i]]
        exchanges[i] = _push_start(slots, True, slots[0], f"exchange{i}_start")
        return exchanges[i]["token"]

    comm = types.SimpleNamespace(weights=weights, grads=send_grads)
    small = {k: w[k] for k in SMALL_REPLICATED}

    loss_local, grad_x, grads = _local_step(inputs["x"], inputs["mem"], inputs["loss_target"], small, comm)
    loss = lax.psum(loss_local, AXES)

    out = {}
    for i, names in enumerate(G_GROUPS):
        sent, lands = _push_wait(exchanges[i], True, grad_x, f"exchange{i}_wait")
        for k, land, slots in zip(names, lands, sent):
            parts = own_slot(land, lax.dynamic_index_in_dim(slots, me, 0, keepdims=False))
            if k in COL_SHARDED and w[k].shape[2] % LANES:
                flip = lambda a: a.transpose(0, 2, 1).reshape((1,) + slot_shape(k))
                unflip = lambda a: a.reshape((1,) + send[k].shape).transpose(0, 2, 1)
                out[k] = tuple(unflip(o) for o in _sum8_adamw(parts, flip(w[k]), flip(mom[k]), flip(vel[k]),
                                                             f"sum_adamw_{k}", tr=2048 if slot_shape(k)[1] == LANES else 128))
                continue
            if k in COL_SHARDED:
                parts = parts.transpose(0, 2, 1)
            out[k] = tuple(_sum8_adamw(parts, w[k], mom[k], vel[k], f"sum_adamw_{k}"))

    small_names = SMALL_REPLICATED + SMALL_SHARDED
    packed_g, small_offs = _pack_rows([grads[k] for k in small_names], F32, 8)
    total = _sum8(_all_gather([packed_g], "gather_small_grads")[0], "sum_small_grads")
    full_grads = _unpack_rows(total, small_offs, [grads[k].shape for k in small_names])
    mine = {}
    for k, g in zip(small_names, full_grads):
        if k in SMALL_SHARDED:
            c_loc = w[k].shape[2]
            g = lax.dynamic_slice_in_dim(g, me * c_loc, c_loc, axis=1)
        mine[k] = g.reshape(w[k].shape)
    sg, s_offs = _pack_rows([mine[k] for k in small_names], F32, 8)
    sw, _ = _pack_rows([w[k] for k in small_names], F32, 8)
    sm, _ = _pack_rows([mom[k] for k in small_names], F32, 8)
    sv, _ = _pack_rows([vel[k] for k in small_names], F32, 8)
    s_shapes = [w[k].shape for k in small_names]
    small_out = [_unpack_rows(a, s_offs, s_shapes) for a in _adamw(sg, sw, sm, sv, "adamw_small")]
    for i, k in enumerate(small_names):
        out[k] = (mine[k],) + tuple(o[i] for o in small_out)

    res = [loss, grad_x]
    for j in range(4):
        res += [out[k][j] for k in WEIGHT_ORDER]
    return tuple(res)


def kernel(x, mem, ffn1_norm, ffn1_w_gate_up, ffn1_w_down, mix_norm, w_in, conv_a_w, w_out_a, ssm_conv_w, ssm_conv_b, ssm_dt_bias, ssm_a_log, ssm_d, ssm_norm, w_out_ssm, w_mix_out, xattn_norm, mem_norm, w_q, w_kv, w_o_x, ffn2_norm, ffn2_w_gate_up, ffn2_w_down, final_norm, loss_target, m_ffn1_norm, m_ffn1_w_gate_up, m_ffn1_w_down, m_mix_norm, m_w_in, m_conv_a_w, m_w_out_a, m_ssm_conv_w, m_ssm_conv_b, m_ssm_dt_bias, m_ssm_a_log, m_ssm_d, m_ssm_norm, m_w_out_ssm, m_w_mix_out, m_xattn_norm, m_mem_norm, m_w_q, m_w_kv, m_w_o_x, m_ffn2_norm, m_ffn2_w_gate_up, m_ffn2_w_down, m_final_norm, v_ffn1_norm, v_ffn1_w_gate_up, v_ffn1_w_down, v_mix_norm, v_w_in, v_conv_a_w, v_w_out_a, v_ssm_conv_w, v_ssm_conv_b, v_ssm_dt_bias, v_ssm_a_log, v_ssm_d, v_ssm_norm, v_w_out_ssm, v_w_mix_out, v_xattn_norm, v_mem_norm, v_w_q, v_w_kv, v_w_o_x, v_ffn2_norm, v_ffn2_w_gate_up, v_ffn2_w_down, v_final_norm):
    return _step(dict(locals()))
```

```python
import functools
import math
import types

import jax
import jax.numpy as jnp
from jax import lax
from jax.experimental import pallas as pl
from jax.experimental.pallas import tpu as pltpu

F32, BF16 = jnp.float32, jnp.bfloat16
HI = lax.Precision.HIGHEST
MESH = pl.DeviceIdType.MESH
AXES = ("x", "y", "c")
N_DEV = 8

EPS = 1e-6
FFN_RES_WEIGHT = 0.5
SSM_HEAD_DIM = 64
SSM_GROUPS = 4
SSM_STATE = 128
SSM_CHUNK = 128
CONV_A_K = 3
SSM_CONV_K = 4
XATTN_HEADS = 4
ADAM_LR, ADAM_B1, ADAM_B2, ADAM_EPS, ADAM_WD, ADAM_STEP = 1e-3, 0.9, 0.999, 1e-8, 0.01, 10

LANES = 128
BF16_SUBLANES = 16
VMEM_LIMIT_BYTES = 56 * 2 ** 20
NEG_BIG = -1e30

BIG_WEIGHTS = ("ffn1_w_gate_up", "ffn1_w_down", "w_in", "w_out_a", "w_out_ssm", "w_mix_out",
               "w_q", "w_kv", "w_o_x", "ffn2_w_gate_up", "ffn2_w_down")
COL_SHARDED = ("ffn1_w_gate_up", "w_in", "w_kv", "ffn2_w_gate_up")
SMALL_REPLICATED = ("ffn1_norm", "mix_norm", "ssm_conv_b", "ssm_dt_bias", "ssm_a_log", "ssm_d", "ssm_norm",
                    "xattn_norm", "mem_norm", "ffn2_norm", "final_norm")
SMALL_SHARDED = ("conv_a_w", "ssm_conv_w")
WEIGHT_ORDER = ("ffn1_norm", "ffn1_w_gate_up", "ffn1_w_down", "mix_norm", "w_in", "conv_a_w", "w_out_a",
                "ssm_conv_w", "ssm_conv_b", "ssm_dt_bias", "ssm_a_log", "ssm_d", "ssm_norm", "w_out_ssm",
                "w_mix_out", "xattn_norm", "mem_norm", "w_q", "w_kv", "w_o_x", "ffn2_norm", "ffn2_w_gate_up",
                "ffn2_w_down", "final_norm")


def _tile(dim, pref, unit):
    best = None
    t = unit
    while t <= min(dim, pref):
        if dim % t == 0:
            best = t
        t += unit
    return best if best is not None else dim


def _params(*sem):
    return pltpu.CompilerParams(dimension_semantics=sem, vmem_limit_bytes=VMEM_LIMIT_BYTES)


def _sigmoid(x):
    return pl.reciprocal(1.0 + jnp.exp(-x), approx=True)


def _silu(x):
    return x * _sigmoid(x)


def _dsilu(x):
    s = _sigmoid(x)
    return s * (1.0 + x * (1.0 - s))


def _softplus(x):
    return jnp.maximum(x, 0.0) + jnp.log(1.0 + jnp.exp(-jnp.abs(x)))


def _dot(a, b, dims=(((1,), (0,)), ((), ())), precision=None):
    return lax.dot_general(a, b, dims, preferred_element_type=F32, precision=precision)


def _stack_rows(rows, width):
    r_idx = lax.broadcasted_iota(jnp.int32, (8, width), 0)
    acc = jnp.zeros((8, width), F32)
    for k, row in enumerate(rows):
        acc = jnp.where(r_idx == k, row, acc)
    return acc


NT = (((1,), (1,)), ((), ()))
TN = (((0,), (0,)), ((), ()))


def _mm(a, b, out_dtype, name, res=None, alpha=1.0, nt=False, dep=None, norm_out=None, norm_bwd=None,
        loss_head=None, tm=1024, tn=2048, tk=2816):
    pieces = list(a) if isinstance(a, (list, tuple)) else [a]
    m = pieces[0].shape[0]
    k = sum(p.shape[1] for p in pieces)
    n = b.shape[0] if nt else b.shape[1]
    assert (b.shape[1] if nt else b.shape[0]) == k
    tm, tn = _tile(m, tm, 8), _tile(n, tn, LANES)
    tk = _tile(math.gcd(*[p.shape[1] for p in pieces]), tk, LANES)
    nk = k // tk
    starts, s0 = [], 0
    for p in pieces:
        starts.append((s0, p.shape[1] // tk))
        s0 += p.shape[1] // tk
    n_p = len(pieces)
    assert (norm_out is not None) + (norm_bwd is not None) + (loss_head is not None) <= 1
    whole_rows = norm_out is not None or norm_bwd is not None or loss_head is not None
    assert not whole_rows or tn == n
    has_pre = norm_bwd is not None and len(norm_bwd) == 3
    has_x = norm_bwd is not None or loss_head is not None

    def body(*refs):
        a_refs, b_ref = refs[:n_p], refs[n_p]
        nxt = n_p + 1
        r_ref = refs[nxt] if res is not None else None
        nxt += (res is not None) + (dep is not None)
        g_ref = refs[nxt] if whole_rows else None
        x_ref = refs[nxt + 1] if has_x else None
        pre_ref = refs[nxt + 2] if has_pre else None
        nxt += whole_rows + has_x + has_pre
        o_ref = refs[nxt]
        o2_ref = refs[nxt + 1] if whole_rows else None
        o3_ref = refs[nxt + 2] if loss_head is not None else None
        scr = refs[nxt + 1 + whole_rows + (loss_head is not None):]
        first_rows = pl.program_id(0) == 0

        def finish(acc):
            acc = alpha * acc if alpha != 1.0 else acc
            if loss_head is not None:
                hv = r_ref[...] + acc if r_ref is not None else acc
                r = lax.rsqrt(jnp.mean(hv * hv, axis=-1, keepdims=True) + EPS)
                xh = hv * r
                err = xh * g_ref[...] - x_ref[...]
                dout = err * (1.0 / n)
                gy = dout * g_ref[...]
                o_ref[...] = r * (gy - xh * jnp.mean(gy * xh, axis=-1, keepdims=True))
                dg_part = jnp.sum(dout * xh, axis=0, keepdims=True)
                loss_part = jnp.full((1, LANES), 0.5 / n, F32) * jnp.sum(err * err)

                @pl.when(first_rows)
                def _():
                    o2_ref[...] = dg_part
                    o3_ref[...] = loss_part

                @pl.when(jnp.logical_not(first_rows))
                def _():
                    o2_ref[...] += dg_part
                    o3_ref[...] += loss_part
                return
            if norm_bwd is not None:
                if pre_ref is not None:
                    acc = acc + pre_ref[...]
                xv = x_ref[...]
                r = lax.rsqrt(jnp.mean(xv * xv, axis=-1, keepdims=True) + EPS)
                xh = xv * r
                gy = acc * g_ref[...]
                part = jnp.sum(acc * xh, axis=0, keepdims=True)
                acc = r * (gy - xh * jnp.mean(gy * xh, axis=-1, keepdims=True))

                @pl.when(first_rows)
                def _():
                    o2_ref[...] = part

                @pl.when(jnp.logical_not(first_rows))
                def _():
                    o2_ref[...] += part
            if r_ref is not None:
                acc = r_ref[...] + acc
            o_ref[...] = acc.astype(out_dtype)
            if norm_out is not None:
                r = lax.rsqrt(jnp.mean(acc * acc, axis=-1, keepdims=True) + EPS)
                o2_ref[...] = (acc * r * g_ref[...]).astype(BF16)

        def product(a_ref):
            return _dot(a_ref[...].astype(BF16), b_ref[...].astype(BF16), NT if nt else (((1,), (0,)), ((), ())))

        if nk == 1:
            finish(product(a_refs[0]))
            return
        acc_ref = scr[0]
        kk = pl.program_id(2)
        for (s, cnt), a_ref in zip(starts, a_refs):
            if s == 0:
                @pl.when(kk == 0)
                def _():
                    acc_ref[...] = product(a_ref)

                @pl.when(jnp.logical_and(kk > 0, kk < cnt))
                def _():
                    acc_ref[...] += product(a_ref)
            else:
                @pl.when(jnp.logical_and(kk >= s, kk < s + cnt))
                def _():
                    acc_ref[...] += product(a_ref)

        @pl.when(kk == nk - 1)
        def _():
            finish(acc_ref[...])

    def a_spec(s, cnt):
        return pl.BlockSpec((tm, tk), lambda i, j, kk: (i, jnp.clip(kk - s, 0, cnt - 1)))

    in_specs = [a_spec(s, cnt) for s, cnt in starts]
    in_specs.append(pl.BlockSpec((tn, tk), lambda i, j, kk: (j, kk)) if nt else pl.BlockSpec((tk, tn), lambda i, j, kk: (kk, j)))
    args = pieces + [b]
    if res is not None:
        in_specs.append(pl.BlockSpec((tm, tn), lambda i, j, kk: (i, j)))
        args.append(res)
    if dep is not None:
        in_specs.append(pl.BlockSpec((8, LANES), lambda i, j, kk: (0, 0)))
        args.append(dep)
    tile = pl.BlockSpec((tm, tn), lambda i, j, kk: (i, j))
    vec = pl.BlockSpec((1, n), lambda i, j, kk: (0, 0))
    out_specs, out_shape = tile, jax.ShapeDtypeStruct((m, n), out_dtype)
    if norm_out is not None:
        in_specs.append(vec)
        args.append(norm_out)
        out_specs, out_shape = [tile, tile], [out_shape, jax.ShapeDtypeStruct((m, n), BF16)]
    if norm_bwd is not None:
        in_specs += [vec, tile] + ([tile] if has_pre else [])
        args += [norm_bwd[1], norm_bwd[0]] + ([norm_bwd[2]] if has_pre else [])
        out_specs, out_shape = [tile, vec], [out_shape, jax.ShapeDtypeStruct((1, n), F32)]
    if loss_head is not None:
        in_specs += [vec, tile]
        args += [loss_head[0], loss_head[1]]
        out_specs = [tile, vec, pl.BlockSpec((1, LANES), lambda i, j, kk: (0, 0))]
        out_shape = [out_shape, jax.ShapeDtypeStruct((1, n), F32), jax.ShapeDtypeStruct((1, LANES), F32)]
    sums_over_rows = norm_bwd is not None or loss_head is not None
    return pl.pallas_call(
        body, name=name, grid=(m // tm, n // tn, nk), in_specs=in_specs, out_specs=out_specs, out_shape=out_shape,
        scratch_shapes=[pltpu.VMEM((tm, tn), F32)] if nk > 1 else [],
        compiler_params=(_params("arbitrary", "arbitrary", "arbitrary") if sums_over_rows
                         else _params("parallel", "parallel", "arbitrary")),
    )(*args)


def _mm_tn(x, dy, name, out_dtype=BF16, alpha=1.0, dep=None, out_rows=None, row_off=0, into=None,
           tko=1408, tn=1024, tt=2048):
    t, k = x.shape
    n = dy.shape[1]
    tko, tn, tt = _tile(k, tko, LANES), _tile(n, tn, LANES), _tile(t, tt, 8)
    nt_steps = t // tt

    def body(*refs):
        x_ref, dy_ref = refs[:2]
        o_ref, acc_ref = refs[-2:]
        part = _dot(x_ref[...].astype(BF16), dy_ref[...].astype(BF16), TN)
        step = pl.program_id(2)

        @pl.when(step == 0)
        def _():
            acc_ref[...] = part

        @pl.when(step > 0)
        def _():
            acc_ref[...] += part

        @pl.when(step == nt_steps - 1)
        def _():
            acc = acc_ref[...]
            o_ref[...] = (alpha * acc if alpha != 1.0 else acc).astype(out_dtype)

    in_specs = [pl.BlockSpec((tt, tko), lambda i, j, s: (s, i)), pl.BlockSpec((tt, tn), lambda i, j, s: (s, j))]
    args = [x, dy]
    if dep is not None:
        in_specs.append(pl.BlockSpec((8, LANES), lambda i, j, s: (0, 0)))
        args.append(dep)
    aliases = {}
    if into is not None:
        in_specs.append(pl.BlockSpec(memory_space=pl.ANY))
        args.append(into)
        aliases = {len(args) - 1: 0}
    band = row_off // tko
    assert row_off % tko == 0
    return pl.pallas_call(
        body, name=name, grid=(k // tko, n // tn, nt_steps), in_specs=in_specs,
        out_specs=pl.BlockSpec((tko, tn), lambda i, j, s: (i + band, j)),
        out_shape=jax.ShapeDtypeStruct((out_rows or k, n), out_dtype),
        scratch_shapes=[pltpu.VMEM((tko, tn), F32)], input_output_aliases=aliases,
        compiler_params=_params("parallel", "parallel", "arbitrary"),
    )(*args)


def _rms_fwd(x, g, name, tt=512):
    t, d = x.shape
    tt = _tile(t, tt, 8)

    def body(x_ref, g_ref, o_ref):
        xv = x_ref[...]
        r = lax.rsqrt(jnp.mean(xv * xv, axis=-1, keepdims=True) + EPS)
        o_ref[...] = (xv * r * g_ref[...]).astype(BF16)

    return pl.pallas_call(
        body, name=name, grid=(t // tt,),
        in_specs=[pl.BlockSpec((tt, d), lambda i: (i, 0)), pl.BlockSpec((1, d), lambda i: (0, 0))],
        out_specs=pl.BlockSpec((tt, d), lambda i: (i, 0)),
        out_shape=jax.ShapeDtypeStruct((t, d), BF16), compiler_params=_params("parallel"),
    )(x, g)


def _ffn_up(n, w_gu_t, name, tm=512, tf=2816):
    t, d = n.shape
    f = w_gu_t.shape[0] // 2
    tm, tf = _tile(t, tm, 8), _tile(f, tf, LANES)
    nf = f // tf

    def body(n_ref, wg_ref, wu_ref, g_ref, u_ref, a_ref):
        nv = n_ref[...]
        gate, up = _dot(nv, wg_ref[...], NT), _dot(nv, wu_ref[...], NT)
        s = _sigmoid(gate)
        sg = gate * s
        g_ref[...] = (up * (s * (1.0 + gate * (1.0 - s)))).astype(BF16)
        u_ref[...] = sg.astype(BF16)
        a_ref[...] = (sg * up).astype(BF16)

    blk = pl.BlockSpec((tm, tf), lambda i, j: (i, j))
    out = jax.ShapeDtypeStruct((t, f), BF16)
    return pl.pallas_call(
        body, name=name, grid=(t // tm, nf),
        in_specs=[pl.BlockSpec((tm, d), lambda i, j: (i, 0)), pl.BlockSpec((tf, d), lambda i, j: (j, 0)),
                  pl.BlockSpec((tf, d), lambda i, j: (j + nf, 0))],
        out_specs=[blk, blk, blk], out_shape=[out, out, out], compiler_params=_params("parallel", "parallel"),
    )(n, w_gu_t, w_gu_t)


def _ffn_da(dh, w_d, gate, up, name, alpha, dep=None, tm=512, tf=2816):
    t, d = dh.shape
    f = w_d.shape[0]
    tm, tf = _tile(t, tm, 8), _tile(f, tf, LANES)

    def body(*refs):
        dh_ref, w_ref, g_ref, u_ref = refs[:4]
        dg_ref, du_ref = refs[-2:]
        da = alpha * _dot(dh_ref[...].astype(BF16), w_ref[...], NT)
        dg_ref[...] = (da * g_ref[...].astype(F32)).astype(BF16)
        du_ref[...] = (da * u_ref[...].astype(F32)).astype(BF16)

    blk = pl.BlockSpec((tm, tf), lambda i, j: (i, j))
    in_specs = [pl.BlockSpec((tm, d), lambda i, j: (i, 0)), pl.BlockSpec((tf, d), lambda i, j: (j, 0)), blk, blk]
    args = [dh, w_d, gate, up]
    if dep is not None:
        in_specs.append(pl.BlockSpec((8, LANES), lambda i, j: (0, 0)))
        args.append(dep)
    out = jax.ShapeDtypeStruct((t, f), BF16)
    return pl.pallas_call(
        body, name=name, grid=(t // tm, f // tf), in_specs=in_specs, out_specs=[blk, blk], out_shape=[out, out],
        compiler_params=_params("parallel", "parallel"),
    )(*args)


def _merge_mix_out(ya, yb, proj, ga_blk, gb_blk, w_mix, h, gain, name, tt=512):
    t, d = ya.shape
    tt = _tile(t, tt, 8)

    def body(ya_ref, yb_ref, ga_ref, gb_ref, w_ref, h_ref, g_ref, m_ref, ho_ref, n_ref):
        merged = (_sigmoid(ga_ref[...].astype(F32)) * ya_ref[...].astype(F32)
                  + _sigmoid(gb_ref[...].astype(F32)) * yb_ref[...].astype(F32)).astype(BF16)
        m_ref[...] = merged
        hv = h_ref[...] + _dot(merged, w_ref[...])
        ho_ref[...] = hv
        r = lax.rsqrt(jnp.mean(hv * hv, axis=-1, keepdims=True) + EPS)
        n_ref[...] = (hv * r * g_ref[...]).astype(BF16)

    row = pl.BlockSpec((tt, d), lambda i: (i, 0))
    half = jax.ShapeDtypeStruct((t, d), BF16)
    return pl.pallas_call(
        body, name=name, grid=(t // tt,),
        in_specs=[row, row, pl.BlockSpec((tt, d), lambda i: (i, ga_blk)), pl.BlockSpec((tt, d), lambda i: (i, gb_blk)),
                  pl.BlockSpec((d, d), lambda i: (0, 0)), row, pl.BlockSpec((1, d), lambda i: (0, 0))],
        out_specs=[row, row, row], out_shape=[half, jax.ShapeDtypeStruct((t, d), F32), half],
        compiler_params=_params("parallel"),
    )(ya, yb, proj, proj, w_mix, h, gain)


def _merge_bwd(dh, w_mix, ya, yb, proj, ga_blk, gb_blk, d, name, tt=512):
    t = ya.shape[0]
    tt = _tile(t, tt, 8)

    def body(dh_ref, w_ref, ya_ref, yb_ref, ga_ref, gb_ref, dya_ref, dyb_ref, dg_ref):
        dmv = _dot(dh_ref[...].astype(BF16), w_ref[...], NT)
        sa, sb = _sigmoid(ga_ref[...].astype(F32)), _sigmoid(gb_ref[...].astype(F32))
        dya_ref[...] = (dmv * sa).astype(BF16)
        dyb_ref[...] = (dmv * sb).astype(BF16)
        dg_ref[:, 0:d] = (dmv * ya_ref[...].astype(F32) * sa * (1.0 - sa)).astype(BF16)
        dg_ref[:, d:2 * d] = (dmv * yb_ref[...].astype(F32) * sb * (1.0 - sb)).astype(BF16)

    row = pl.BlockSpec((tt, d), lambda i: (i, 0))
    out = jax.ShapeDtypeStruct((t, d), BF16)
    return pl.pallas_call(
        body, name=name, grid=(t // tt,),
        in_specs=[row, pl.BlockSpec((d, d), lambda i: (0, 0)), row, row,
                  pl.BlockSpec((tt, d), lambda i: (i, ga_blk)), pl.BlockSpec((tt, d), lambda i: (i, gb_blk))],
        out_specs=[row, row, pl.BlockSpec((tt, 2 * d), lambda i: (i, 0))],
        out_shape=[out, out, jax.ShapeDtypeStruct((t, 2 * d), BF16)], compiler_params=_params("parallel"),
    )(dh, w_mix, ya, yb, proj, proj)


def _shift_down(x, k, t_idx):
    if k == 0:
        return x
    return jnp.where(t_idx >= k, pltpu.roll(x, k, 0), 0.0)


def _shift_up(x, k, t_idx, s):
    if k == 0:
        return x
    return jnp.where(t_idx < s - k, pltpu.roll(x, s - k, 0), 0.0)


CONV_A_BLOCK = 256


def _conv_a_fwd(proj, w, nb, s, d, name):
    cb = _tile(d, CONV_A_BLOCK, LANES)

    def body(x_ref, w_ref, o_ref):
        t_idx = lax.broadcasted_iota(jnp.int32, (s, cb), 0)
        cv = x_ref[:, cb:2 * cb].astype(F32) * x_ref[:, 2 * cb:3 * cb].astype(F32)
        cc = sum(w_ref[k:k + 1, :] * _shift_down(cv, CONV_A_K - 1 - k, t_idx) for k in range(CONV_A_K))
        o_ref[...] = (x_ref[:, 0:cb].astype(F32) * cc).astype(BF16)

    return pl.pallas_call(
        body, name=name, grid=(nb, d // cb),
        in_specs=[pl.BlockSpec((s, 3 * cb), lambda b, j: (b, j)), pl.BlockSpec((8, cb), lambda b, j: (0, j))],
        out_specs=pl.BlockSpec((s, cb), lambda b, j: (b, j)),
        out_shape=jax.ShapeDtypeStruct((nb * s, d), BF16), compiler_params=_params("parallel", "parallel"),
    )(proj, w)


def _conv_a_bwd(dya, w_out, proj, w, nb, s, d, out_cols, name):
    cb = _tile(d, CONV_A_BLOCK, LANES)

    def body(dya_ref, wo_ref, x_ref, w_ref, o_ref, dw_ref, dy_ref):
        dy_ref[...] = _dot(dya_ref[...], wo_ref[...], NT)
        t_idx = lax.broadcasted_iota(jnp.int32, (s, cb), 0)
        cv_c, cv_v = x_ref[:, cb:2 * cb].astype(F32), x_ref[:, 2 * cb:3 * cb].astype(F32)
        cv = cv_c * cv_v
        shifted = [_shift_down(cv, CONV_A_K - 1 - k, t_idx) for k in range(CONV_A_K)]
        cc = sum(w_ref[k:k + 1, :] * shifted[k] for k in range(CONV_A_K))
        dyv = dy_ref[...].astype(F32)
        o_ref[:, 0:cb] = (dyv * cc).astype(BF16)
        dcc = dyv * x_ref[:, 0:cb].astype(F32)
        dcv = sum(w_ref[k:k + 1, :] * _shift_up(dcc, CONV_A_K - 1 - k, t_idx, s) for k in range(CONV_A_K))
        o_ref[:, cb:2 * cb] = (dcv * cv_v).astype(BF16)
        o_ref[:, 2 * cb:3 * cb] = (dcv * cv_c).astype(BF16)
        rows = [jnp.sum(dcc * shifted[k], axis=0, keepdims=True) for k in range(CONV_A_K)]
        part = _stack_rows(rows, cb)

        @pl.when(pl.program_id(1) == 0)
        def _():
            dw_ref[...] = part

        @pl.when(pl.program_id(1) > 0)
        def _():
            dw_ref[...] += part

    wspec = pl.BlockSpec((8, cb), lambda j, b: (0, j))
    wide = pl.BlockSpec((s, 3 * cb), lambda j, b: (b, j))
    return pl.pallas_call(
        body, name=name, grid=(d // cb, nb),
        in_specs=[pl.BlockSpec((s, d), lambda j, b: (b, 0)), pl.BlockSpec((cb, d), lambda j, b: (j, 0)), wide, wspec],
        out_specs=[wide, wspec],
        out_shape=[jax.ShapeDtypeStruct((nb * s, out_cols), BF16), jax.ShapeDtypeStruct((8, d), F32)],
        scratch_shapes=[pltpu.VMEM((s, cb), F32)],
        compiler_params=_params("parallel", "arbitrary"),
    )(dya, w_out, proj, w)


def _conv_s_fwd(proj, col0, w, bias, nb, s, cc_width, name, cb=256):
    cb = _tile(math.gcd(cc_width, col0) if col0 else cc_width, cb, LANES)
    nd, off = cc_width // cb, col0 // cb

    def body(x_ref, w_ref, b_ref, o_ref, pre_ref):
        t_idx = lax.broadcasted_iota(jnp.int32, (s, cb), 0)
        xv = x_ref[...].astype(F32)
        pre = b_ref[...] + sum(w_ref[k:k + 1, :] * _shift_down(xv, SSM_CONV_K - 1 - k, t_idx) for k in range(SSM_CONV_K))
        o_ref[...] = _silu(pre).astype(BF16)
        pre_ref[...] = pre.astype(BF16)

    vec = pl.BlockSpec((8, cb), lambda b, j: (0, j))
    own = pl.BlockSpec((s, cb), lambda b, j: (b, j))
    out = jax.ShapeDtypeStruct((nb * s, cc_width), BF16)
    return pl.pallas_call(
        body, name=name, grid=(nb, nd),
        in_specs=[pl.BlockSpec((s, cb), lambda b, j: (b, j + off)), vec, pl.BlockSpec((1, cb), lambda b, j: (0, j))],
        out_specs=[own, own], out_shape=[out, out], compiler_params=_params("parallel", "parallel"),
    )(proj, w, bias)


def _conv_s_bwd(dxc, pre, proj, col0, w, nb, s, cc_width, into, name, cb=256):
    cb = _tile(math.gcd(cc_width, col0) if col0 else cc_width, cb, LANES)
    nd, off = cc_width // cb, col0 // cb

    def body(d_ref, pre_ref, x_ref, w_ref, into_ref, dx_ref, dw_ref, db_ref):
        t_idx = lax.broadcasted_iota(jnp.int32, (s, cb), 0)
        xv = x_ref[...].astype(F32)
        dpre = d_ref[...].astype(F32) * _dsilu(pre_ref[...].astype(F32))
        ahead = [_shift_up(dpre, j, t_idx, s) for j in range(SSM_CONV_K)]
        dx_ref[...] = sum(w_ref[k:k + 1, :] * ahead[SSM_CONV_K - 1 - k] for k in range(SSM_CONV_K)).astype(BF16)
        rows = [jnp.sum(ahead[SSM_CONV_K - 1 - k] * xv, axis=0, keepdims=True) for k in range(SSM_CONV_K)]
        dw_part = _stack_rows(rows, cb)
        db_part = jnp.sum(dpre, axis=0, keepdims=True)

        @pl.when(pl.program_id(1) == 0)
        def _():
            dw_ref[...] = dw_part
            db_ref[...] = db_part

        @pl.when(pl.program_id(1) > 0)
        def _():
            dw_ref[...] += dw_part
            db_ref[...] += db_part

    own = pl.BlockSpec((s, cb), lambda j, b: (b, j))
    shifted = pl.BlockSpec((s, cb), lambda j, b: (b, j + off))
    wspec = pl.BlockSpec((8, cb), lambda j, b: (0, j))
    bspec = pl.BlockSpec((1, cb), lambda j, b: (0, j))
    return pl.pallas_call(
        body, name=name, grid=(nd, nb),
        in_specs=[own, own, shifted, wspec, pl.BlockSpec(memory_space=pl.ANY)],
        out_specs=[shifted, wspec, bspec],
        out_shape=[jax.ShapeDtypeStruct(into.shape, into.dtype), jax.ShapeDtypeStruct((8, cc_width), F32),
                   jax.ShapeDtypeStruct((1, cc_width), F32)],
        input_output_aliases={4: 0},
        compiler_params=_params("parallel", "arbitrary"),
    )(dxc, pre, proj, w, into)


def _split3(v):
    hi = v.astype(BF16)
    r1 = v - hi.astype(F32)
    mid = r1.astype(BF16)
    return hi, mid, (r1 - mid.astype(F32)).astype(BF16)


def _exact_left(mask_b, v):
    return sum(_dot(mask_b, t) for t in _split3(v))


def _exact_right(v, mask_b):
    return sum(_dot(t, mask_b) for t in _split3(v))


def _head_sums(v, e_b):
    return _dot(v.astype(BF16), e_b, NT)


def _spread(v, out_ref, di):
    lane = lax.broadcasted_iota(jnp.int32, (v.shape[0], LANES), 1)
    for pr in range(di // LANES):
        h0 = pr * (LANES // SSM_HEAD_DIM)
        out_ref[:, pr * LANES:(pr + 1) * LANES] = jnp.where(lane < SSM_HEAD_DIM, v[:, h0:h0 + 1], v[:, h0 + 1:h0 + 2])


def _ssd_common(xc_ref, dtr_ref, dtrt_ref, prow_ref, pcol_ref, dtx_ref, acsx_ref, dx_ref, di):
    l = SSM_CHUNK
    bias_r, a_r = prow_ref[0:1, :], -jnp.exp(prow_ref[1:2, :])
    sp_in = dtr_ref[...] + bias_r
    dt = _softplus(sp_in)
    li = lax.broadcasted_iota(jnp.int32, (l, l), 0)
    si = lax.broadcasted_iota(jnp.int32, (l, l), 1)
    lower_b = (li >= si).astype(BF16)
    upper_b = (li <= si).astype(BF16)
    acs = _exact_left(lower_b, dt * a_r)
    bias_c, a_c = pcol_ref[:, 0:1], -jnp.exp(pcol_ref[:, 1:2])
    dt_t = _softplus(dtrt_ref[...] + bias_c)
    acs_t = _exact_right(dt_t * a_c, upper_b)
    _spread(dt, dtx_ref, di)
    _spread(acs, acsx_ref, di)
    _spread(prow_ref[0:8, :], dx_ref, di)
    acs_exp = acsx_ref[...]
    acs_last = acs_exp[l - 1:l, :]
    x = xc_ref[:, 0:di].astype(F32)
    return dict(dt=dt, a_r=a_r, sp_in=sp_in, acs=acs, acs_t=acs_t, dt_exp=dtx_ref[...], e_exp=jnp.exp(acs_exp),
                el_exp=jnp.exp(acs_last), f_exp=jnp.exp(acs_last - acs_exp), x=x, mask=li >= si, upper_b=upper_b,
                d_exp=dx_ref[2:3, :])


def _decay(q, h):
    seg = q["acs"][:, h:h + 1] - q["acs_t"][h:h + 1, :]
    return jnp.exp(jnp.where(q["mask"], seg, NEG_BIG))


def _ssd_fwd(xc, dtr, dtrt, prow, pcol, nb, nc, di, name):
    l, n, g_n, p = SSM_CHUNK, SSM_STATE, SSM_GROUPS, SSM_HEAD_DIM
    cc = xc.shape[1]
    gw = di // g_n
    assert p * 2 == LANES and gw % LANES == 0

    def body(xc_ref, dtr_ref, dtrt_ref, prow_ref, pcol_ref, y_ref, sprev_ref, st_ref, dtx_ref, acsx_ref, dx_ref):
        @pl.when(pl.program_id(1) == 0)
        def _():
            st_ref[...] = jnp.zeros_like(st_ref)

        q = _ssd_common(xc_ref, dtr_ref, dtrt_ref, prow_ref, pcol_ref, dtx_ref, acsx_ref, dx_ref, di)
        x = q["x"]
        xd = x * q["dt_exp"]
        xdb = xd.astype(BF16)
        xdf = (xd * q["f_exp"]).astype(BF16)
        lane = lax.broadcasted_iota(jnp.int32, (l, LANES), 1)
        for g in range(g_n):
            lo = g * gw
            bg = xc_ref[:, di + g * n: di + (g + 1) * n]
            cg = xc_ref[:, di + g_n * n + g * n: di + g_n * n + (g + 1) * n]
            cb = _dot(cg, bg, NT)
            st_g = st_ref[:, lo:lo + gw]
            y_off = q["e_exp"][:, lo:lo + gw] * _dot(cg, st_g.astype(BF16))
            for pr in range(gw // LANES):
                c0 = lo + pr * LANES
                h0 = c0 // p
                xp = xdb[:, c0:c0 + LANES]
                m0 = (cb * _decay(q, h0)).astype(BF16)
                m1 = (cb * _decay(q, h0 + 1)).astype(BF16)
                yd = _dot(m0, jnp.where(lane < p, xp, 0)) + _dot(m1, jnp.where(lane >= p, xp, 0))
                y_ref[:, c0:c0 + LANES] = (yd + y_off[:, pr * LANES:(pr + 1) * LANES]
                                           + q["d_exp"][:, c0:c0 + LANES] * x[:, c0:c0 + LANES]).astype(BF16)
            sprev_ref[:, lo:lo + gw] = st_g
            st_ref[:, lo:lo + gw] = q["el_exp"][:, lo:lo + gw] * st_g + _dot(bg, xdf[:, lo:lo + gw], TN)

    tok = lambda w: pl.BlockSpec((l, w), lambda b, c: (b * nc + c, 0))
    const = lambda r, w: pl.BlockSpec((r, w), lambda b, c: (0, 0))
    return pl.pallas_call(
        body, name=name, grid=(nb, nc),
        in_specs=[tok(cc), tok(LANES), pl.BlockSpec((LANES, l), lambda b, c: (0, b * nc + c)),
                  const(8, LANES), const(LANES, 8)],
        out_specs=[tok(di), pl.BlockSpec((None, n, di), lambda b, c: (b * nc + c, 0, 0))],
        out_shape=[jax.ShapeDtypeStruct((nb * nc * l, di), BF16), jax.ShapeDtypeStruct((nb * nc, n, di), F32)],
        scratch_shapes=[pltpu.VMEM((n, di), F32), pltpu.VMEM((l, di), F32), pltpu.VMEM((l, di), F32),
                        pltpu.VMEM((8, di), F32)],
        compiler_params=_params("parallel", "arbitrary"),
    )(xc, dtr, dtrt, prow, pcol)


def _ssd_bwd(dy, xc, dtr, dtrt, prow, pcol, e_mat, sprev, nb, nc, di, name):
    l, n, g_n, p = SSM_CHUNK, SSM_STATE, SSM_GROUPS, SSM_HEAD_DIM
    cc = xc.shape[1]
    gw = di // g_n

    def body(dy_ref, xc_ref, dtr_ref, dtrt_ref, prow_ref, pcol_ref, e_ref, sprev_ref,
             dxc_ref, ddtr_ref, sums_ref, dst_ref, off_ref, dxd_ref, last_ref, vst_ref,
             dtx_ref, acsx_ref, dx_ref):
        first = jnp.logical_and(pl.program_id(0) == 0, pl.program_id(1) == 0)

        @pl.when(pl.program_id(1) == 0)
        def _():
            dst_ref[...] = jnp.zeros_like(dst_ref)

        head_row = lax.broadcasted_iota(jnp.int32, (LANES, l), 0)
        row_sums, col_sums = jnp.zeros((l, LANES), F32), jnp.zeros((LANES, l), F32)
        strict_lower = lax.broadcasted_iota(jnp.int32, (l, l), 0) > lax.broadcasted_iota(jnp.int32, (l, l), 1)

        q = _ssd_common(xc_ref, dtr_ref, dtrt_ref, prow_ref, pcol_ref, dtx_ref, acsx_ref, dx_ref, di)
        x = q["x"]
        xd = x * q["dt_exp"]
        xdb = xd.astype(BF16)
        xdf = (xd * q["f_exp"]).astype(BF16)
        dyv = dy_ref[...].astype(F32)
        dyb = dy_ref[...]
        dye = (dyv * q["e_exp"]).astype(BF16)
        upper_b = q["upper_b"]
        lane = lax.broadcasted_iota(jnp.int32, (l, LANES), 1)
        for g in range(g_n):
            lo = g * gw
            bg = xc_ref[:, di + g * n: di + (g + 1) * n]
            cg = xc_ref[:, di + g_n * n + g * n: di + g_n * n + (g + 1) * n]
            cb = _dot(cg, bg, NT)
            st_g = sprev_ref[:, lo:lo + gw]
            st_gb = st_g.astype(BF16)
            dst_g = dst_ref[:, lo:lo + gw]
            dst_gb = dst_g.astype(BF16)
            dye_g = dye[:, lo:lo + gw]
            xdf_g = xdf[:, lo:lo + gw]
            y_off = q["e_exp"][:, lo:lo + gw] * _dot(cg, st_gb)
            dc_g = _dot(dye_g, st_gb, NT)
            db_g = _dot(xdf_g, dst_gb, NT)
            dxd_state = _dot(bg, dst_gb) * q["f_exp"][:, lo:lo + gw]
            last_ref[:, lo:lo + gw] = jnp.sum(dst_g * st_g, axis=0, keepdims=True)
            dst_ref[:, lo:lo + gw] = q["el_exp"][:, lo:lo + gw] * dst_g + _dot(cg, dye_g, TN)
            off_ref[:, lo:lo + gw] = dyv[:, lo:lo + gw] * y_off
            vst_ref[:, lo:lo + gw] = xd[:, lo:lo + gw] * dxd_state
            dcb = jnp.zeros((l, l), F32)
            for pr in range(gw // LANES):
                c0 = lo + pr * LANES
                h0 = c0 // p
                xp = xdb[:, c0:c0 + LANES]
                dyp = dyb[:, c0:c0 + LANES]
                dxd_diag = jnp.zeros((l, LANES), F32)
                for k, keep in enumerate((lane < p, lane >= p)):
                    dec = _decay(q, h0 + k)
                    dy_h = jnp.where(keep, dyp, 0)
                    dm_dec = _dot(dy_h, xp, NT) * dec
                    dcb = dcb + dm_dec
                    dxd_diag = dxd_diag + _dot((cb * dec).astype(BF16), dy_h, TN)
                    qm = dm_dec * cb
                    row_sums = jnp.where(lane == h0 + k, jnp.sum(qm, axis=1, keepdims=True), row_sums)
                    col_sums = jnp.where(head_row == h0 + k, jnp.sum(qm, axis=0, keepdims=True), col_sums)
                dxd_ref[:, c0:c0 + LANES] = dxd_diag + dxd_state[:, pr * LANES:(pr + 1) * LANES]
            dcb_b = dcb.astype(BF16)
            dxc_ref[:, di + g * n: di + (g + 1) * n] = (db_g + _dot(dcb_b, cg, TN)).astype(BF16)
            dxc_ref[:, di + g_n * n + g * n: di + g_n * n + (g + 1) * n] = (dc_g + _dot(dcb_b, bg)).astype(BF16)
        dxd = dxd_ref[...]
        e_b = e_ref[...]
        from_y = _exact_left(upper_b, _head_sums(off_ref[...], e_b) + row_sums - col_sums.T)
        from_s = _exact_left(strict_lower.astype(BF16), _head_sums(vst_ref[...], e_b))
        carried = _head_sums(jnp.broadcast_to(last_ref[...], (8, di)), e_b)[0:1, :] * jnp.exp(q["acs"][l - 1:l, :])
        dla = from_y + from_s + carried
        ddt = dla * q["a_r"] + _head_sums(dxd * x, e_b)
        ddtr = ddt * jax.nn.sigmoid(q["sp_in"])
        ddtr_ref[...] = ddtr
        dxc_ref[:, 0:di] = (dxd * q["dt_exp"] + q["d_exp"] * dyv).astype(BF16)
        dd_exp = jnp.sum(dyv * x, axis=0, keepdims=True)
        dd = _head_sums(jnp.broadcast_to(dd_exp, (8, di)), e_b)[0:1, :]
        part = _stack_rows([jnp.sum(ddtr, axis=0, keepdims=True),
                            jnp.sum(dla * q["dt"], axis=0, keepdims=True) * q["a_r"], dd], LANES)

        @pl.when(first)
        def _():
            sums_ref[...] = part

        @pl.when(jnp.logical_not(first))
        def _():
            sums_ref[...] += part

    rev = lambda b, c: b * nc + (nc - 1 - c)
    tok = lambda w: pl.BlockSpec((l, w), lambda b, c: (rev(b, c), 0))
    const = lambda r, w: pl.BlockSpec((r, w), lambda b, c: (0, 0))
    return pl.pallas_call(
        body, name=name, grid=(nb, nc),
        in_specs=[tok(di), tok(cc), tok(LANES), pl.BlockSpec((LANES, l), lambda b, c: (0, rev(b, c))),
                  const(8, LANES), const(LANES, 8), const(LANES, di),
                  pl.BlockSpec((None, n, di), lambda b, c: (rev(b, c), 0, 0))],
        out_specs=[tok(cc), tok(LANES), const(8, LANES)],
        out_shape=[jax.ShapeDtypeStruct((nb * nc * l, cc), BF16), jax.ShapeDtypeStruct((nb * nc * l, LANES), F32),
                   jax.ShapeDtypeStruct((8, LANES), F32)],
        scratch_shapes=[pltpu.VMEM((n, di), F32), pltpu.VMEM((l, di), F32), pltpu.VMEM((l, di), F32),
                        pltpu.VMEM((1, di), F32), pltpu.VMEM((l, di), F32),
                        pltpu.VMEM((l, di), F32), pltpu.VMEM((l, di), F32), pltpu.VMEM((8, di), F32)],
        compiler_params=_params("arbitrary", "arbitrary"),
    )(dy, xc, dtr, dtrt, prow, pcol, e_mat, sprev)


def _gate_norm_out(y, proj, z_col0, norm_g, w_out, di, name, tt=512):
    t = y.shape[0]
    d_out = w_out.shape[1]
    tt = _tile(t, tt, 8)
    gw = di // SSM_GROUPS
    zw = _tile(math.gcd(di, z_col0), di, LANES)
    nz, zoff = di // zw, z_col0 // zw

    def body(*refs):
        y_ref, z_refs, g_ref, w_ref, o_ref, yb_ref = refs[0], refs[1:1 + nz], refs[1 + nz], refs[2 + nz], refs[3 + nz], refs[4 + nz]
        for g in range(SSM_GROUPS):
            lo = g * gw
            zv = z_refs[lo // zw][:, lo % zw:lo % zw + gw].astype(F32)
            yg = y_ref[:, lo:lo + gw].astype(F32) * _silu(zv)
            r = lax.rsqrt(jnp.mean(yg * yg, axis=-1, keepdims=True) + EPS)
            o_ref[:, lo:lo + gw] = (yg * r * g_ref[:, lo:lo + gw]).astype(BF16)
        yb_ref[...] = _dot(o_ref[...], w_ref[...]).astype(BF16)

    row = pl.BlockSpec((tt, di), lambda i: (i, 0))
    zspecs = [pl.BlockSpec((tt, zw), functools.partial(lambda i, k: (i, zoff + k), k=k)) for k in range(nz)]
    return pl.pallas_call(
        body, name=name, grid=(t // tt,),
        in_specs=[row] + zspecs + [pl.BlockSpec((1, di), lambda i: (0, 0)), pl.BlockSpec((di, d_out), lambda i: (0, 0))],
        out_specs=[row, pl.BlockSpec((tt, d_out), lambda i: (i, 0))],
        out_shape=[jax.ShapeDtypeStruct((t, di), BF16), jax.ShapeDtypeStruct((t, d_out), BF16)],
        compiler_params=_params("parallel"),
    )(y, *([proj] * nz), norm_g, w_out)


def _gate_norm_bwd(dyb, w_out, y, proj, z_col0, norm_g, di, name, tt=512):
    t = y.shape[0]
    d_out = w_out.shape[1]
    tt = _tile(t, tt, 8)
    gw = di // SSM_GROUPS
    zw = _tile(math.gcd(di, z_col0), di, LANES)
    nz, zoff = di // zw, z_col0 // zw

    def body(*refs):
        dyb_ref, w_ref, y_ref, z_refs, g_ref = refs[0], refs[1], refs[2], refs[3:3 + nz], refs[3 + nz]
        dy_ref, dz_ref, dg_ref, dn_ref = refs[4 + nz:]
        dn_ref[...] = _dot(dyb_ref[...], w_ref[...], NT)
        first = pl.program_id(0) == 0
        for g in range(SSM_GROUPS):
            lo = g * gw
            zv = z_refs[lo // zw][:, lo % zw:lo % zw + gw].astype(F32)
            yv = y_ref[:, lo:lo + gw].astype(F32)
            sz = _silu(zv)
            yg = yv * sz
            r = lax.rsqrt(jnp.mean(yg * yg, axis=-1, keepdims=True) + EPS)
            yh = yg * r
            dnv = dn_ref[:, lo:lo + gw].astype(F32)
            gy = dnv * g_ref[:, lo:lo + gw]
            dyg = r * (gy - yh * jnp.mean(gy * yh, axis=-1, keepdims=True))
            dy_ref[:, lo:lo + gw] = (dyg * sz).astype(BF16)
            dz_ref[:, lo:lo + gw] = (dyg * yv * _dsilu(zv)).astype(BF16)
            part = jnp.sum(dnv * yh, axis=0, keepdims=True)

            @pl.when(first)
            def _():
                dg_ref[:, lo:lo + gw] = part

            @pl.when(jnp.logical_not(first))
            def _():
                dg_ref[:, lo:lo + gw] += part

    row = pl.BlockSpec((tt, di), lambda i: (i, 0))
    vec = pl.BlockSpec((1, di), lambda i: (0, 0))
    zspecs = [pl.BlockSpec((tt, zw), functools.partial(lambda i, k: (i, zoff + k), k=k)) for k in range(nz)]
    return pl.pallas_call(
        body, name=name, grid=(t // tt,),
        in_specs=[pl.BlockSpec((tt, d_out), lambda i: (i, 0)), pl.BlockSpec((di, d_out), lambda i: (0, 0)), row] + zspecs + [vec],
        out_specs=[row, row, vec],
        out_shape=[jax.ShapeDtypeStruct((t, di), BF16), jax.ShapeDtypeStruct((t, di), BF16), jax.ShapeDtypeStruct((1, di), F32)],
        scratch_shapes=[pltpu.VMEM((tt, di), F32)],
        compiler_params=_params("arbitrary"),
    )(dyb, w_out, y, *([proj] * nz), norm_g)


def _softmax_rows(s):
    s = s - jnp.max(s, axis=-1, keepdims=True)
    e = jnp.exp(s)
    return e * (1.0 / jnp.sum(e, axis=-1, keepdims=True))


def _xattn_fwd(q, kv, nb, s, m, d, name, tq=1024):
    tq = _tile(s, tq, 8)
    nq = s // tq
    hd = d // XATTN_HEADS
    scale = 1.0 / math.sqrt(hd)

    def body(q_ref, k_ref, v_ref, o_ref):
        for h in range(XATTN_HEADS):
            sl = slice(h * hd, (h + 1) * hd)
            prob = _softmax_rows(_dot(q_ref[:, sl], k_ref[:, sl], NT) * scale)
            o_ref[:, sl] = _dot(prob.astype(BF16), v_ref[:, sl]).astype(BF16)

    return pl.pallas_call(
        body, name=name, grid=(nb, nq),
        in_specs=[pl.BlockSpec((tq, d), lambda b, i: (b * nq + i, 0)), pl.BlockSpec((m, d), lambda b, i: (b, 0)),
                  pl.BlockSpec((m, d), lambda b, i: (b, 1))],
        out_specs=pl.BlockSpec((tq, d), lambda b, i: (b * nq + i, 0)),
        out_shape=jax.ShapeDtypeStruct((nb * s, d), BF16), compiler_params=_params("parallel", "parallel"),
    )(q, kv, kv)


def _xattn_bwd(do, q, kv, nb, s, m, d, name, tq=1024):
    tq = _tile(s, tq, 8)
    nq = s // tq
    hd = d // XATTN_HEADS
    scale = 1.0 / math.sqrt(hd)

    def body(do_ref, q_ref, k_ref, v_ref, dq_ref, dk_ref, dv_ref):
        first = pl.program_id(1) == 0
        for h in range(XATTN_HEADS):
            sl = slice(h * hd, (h + 1) * hd)
            qh, kh, vh, doh = q_ref[:, sl], k_ref[:, sl], v_ref[:, sl], do_ref[:, sl]
            prob = _softmax_rows(_dot(qh, kh, NT) * scale)
            dv_h = _dot(prob.astype(BF16), doh, TN)
            dp = _dot(doh, vh, NT)
            ds = (prob * (dp - jnp.sum(dp * prob, axis=-1, keepdims=True)) * scale).astype(BF16)
            dq_ref[:, sl] = _dot(ds, kh).astype(BF16)
            dk_h = _dot(ds, qh, TN)

            @pl.when(first)
            def _():
                dk_ref[:, sl] = dk_h
                dv_ref[:, sl] = dv_h

            @pl.when(jnp.logical_not(first))
            def _():
                dk_ref[:, sl] += dk_h
                dv_ref[:, sl] += dv_h

    qspec = pl.BlockSpec((tq, d), lambda b, i: (b * nq + i, 0))
    dq, dk, dv = pl.pallas_call(
        body, name=name, grid=(nb, nq),
        in_specs=[qspec, qspec, pl.BlockSpec((m, d), lambda b, i: (b, 0)), pl.BlockSpec((m, d), lambda b, i: (b, 1))],
        out_specs=[qspec, pl.BlockSpec((m, d), lambda b, i: (b, 0)), pl.BlockSpec((m, d), lambda b, i: (b, 0))],
        out_shape=[jax.ShapeDtypeStruct((nb * s, d), BF16), jax.ShapeDtypeStruct((nb * m, d), F32),
                   jax.ShapeDtypeStruct((nb * m, d), F32)],
        compiler_params=_params("parallel", "arbitrary"),
    )(do, q, kv, kv)
    return dq, dk, dv


def _all_gather(shards, name):
    n_arr = len(shards)

    def body(*refs):
        x_refs, out_refs = refs[:n_arr], refs[n_arr:2 * n_arr]
        send_sems, recv_sems, local_sems = refs[2 * n_arr:]
        x, y, c = lax.axis_index("x"), lax.axis_index("y"), lax.axis_index("c")
        me, sibling = (x, y, c), (x, y, 1 - c)
        chips = [(1 - x, y), (x, 1 - y), (1 - x, 1 - y)]

        def copy(w, k, block, to, from_input=False):
            px, py, pc = block
            rows = out_refs[w].at[4 * px + 2 * py + pc]
            return pltpu.make_async_remote_copy(
                src_ref=x_refs[w] if from_input else rows, dst_ref=rows,
                send_sem=send_sems.at[7 * w + k], recv_sem=recv_sems.at[7 * w + k], device_id=to, device_id_type=MESH)

        started = []
        for w in range(n_arr):
            mine = pltpu.make_async_copy(x_refs[w], out_refs[w].at[4 * x + 2 * y + c], local_sems.at[w])
            mine.start()
            started.append(mine)
        sends = []
        for w in range(n_arr):
            sends.append(copy(w, 0, me, sibling, from_input=True))
            sends += [copy(w, 1 + j, me, (*chip, c), from_input=True) for j, chip in enumerate(chips)]
        for cp in sends:
            cp.start()
        for j, chip in enumerate(chips):
            for w in range(n_arr):
                copy(w, 1 + j, (*chip, c), me).wait_recv()
                passed = copy(w, 4 + j, (*chip, c), sibling)
                passed.start()
                sends.append(passed)
        for w in range(n_arr):
            copy(w, 0, sibling, me).wait_recv()
            for j, chip in enumerate(chips):
                copy(w, 4 + j, (*chip, 1 - c), me).wait_recv()
        for cp in sends:
            cp.wait_send()
        for mine in started:
            mine.wait()

    hbm = pl.BlockSpec(memory_space=pl.ANY)
    return pl.pallas_call(
        body, name=name, out_shape=[jax.ShapeDtypeStruct((N_DEV,) + s.shape, s.dtype) for s in shards],
        in_specs=[hbm] * n_arr, out_specs=[hbm] * n_arr,
        scratch_shapes=[pltpu.SemaphoreType.DMA((7 * n_arr,)), pltpu.SemaphoreType.DMA((7 * n_arr,)),
                        pltpu.SemaphoreType.DMA((n_arr,))],
    )(*shards)


_HBM = pl.BlockSpec(memory_space=pltpu.HBM)
_SEM = pl.BlockSpec(memory_space=pltpu.SEMAPHORE)
_DATAFLOW = pltpu.SideEffectType.DATAFLOW_SIDE_EFFECTING


def _peer_list(x, y, c):
    return [(1 - x if k & 4 else x, 1 - y if k & 2 else y, 1 - c if k & 1 else c) for k in range(1, N_DEV)]


def _push_copy(src_ref, land_ref, send_sems, recv_sems, w, k, peer, me, per_peer_src, receiving):
    px, py, pc = peer
    peer_slot = 4 * px + 2 * py + pc
    return pltpu.make_async_remote_copy(
        src_ref=src_ref.at[peer_slot] if per_peer_src else src_ref,
        dst_ref=land_ref.at[peer_slot if receiving else me],
        send_sem=send_sems.at[7 * w + k], recv_sem=recv_sems.at[7 * w + k], device_id=peer, device_id_type=MESH)


def _push_start(srcs, per_peer_src, after, name):
    n_arr = len(srcs)
    land_shapes = [s.shape if per_peer_src else (N_DEV,) + s.shape for s in srcs]

    def body(*refs):
        src_refs, land_refs = refs[:n_arr], refs[n_arr:2 * n_arr]
        send_sems, recv_sems = refs[2 * n_arr + 1], refs[2 * n_arr + 2]
        token = refs[-1]
        x, y, c = lax.axis_index("x"), lax.axis_index("y"), lax.axis_index("c")
        me = 4 * x + 2 * y + c
        for w in range(n_arr):
            for k, peer in enumerate(_peer_list(x, y, c)):
                _push_copy(src_refs[w], land_refs[w], send_sems, recv_sems, w, k, peer, me, per_peer_src, False).start()
        token[...] = jnp.zeros_like(token)

    lands = [pltpu.with_memory_space_constraint(lax.empty(ls, s.dtype), pltpu.HBM) for ls, s in zip(land_shapes, srcs)]
    srcs_hbm = [pltpu.with_memory_space_constraint(s, pltpu.HBM) for s in srcs]
    out = pl.pallas_call(
        body, name=name,
        out_shape=(pltpu.SemaphoreType.DMA((7 * n_arr,)), pltpu.SemaphoreType.DMA((7 * n_arr,)),
                   *[pltpu.HBM(s.shape, s.dtype) for s in srcs], *[pltpu.HBM(ls, s.dtype) for ls, s in zip(land_shapes, srcs)],
                   jax.ShapeDtypeStruct((8, LANES), F32)),
        in_specs=[_HBM] * (2 * n_arr) + [pl.BlockSpec(memory_space=pl.ANY)],
        out_specs=(_SEM, _SEM, *([_HBM] * (2 * n_arr)), pl.BlockSpec(memory_space=pltpu.VMEM)),
        input_output_aliases={i: 2 + i for i in range(2 * n_arr)},
        compiler_params=pltpu.CompilerParams(has_side_effects=_DATAFLOW),
    )(*srcs_hbm, *lands, after)
    return dict(send=out[0], recv=out[1], srcs=list(out[2:2 + n_arr]), lands=list(out[2 + n_arr:2 + 2 * n_arr]),
                token=out[-1])


def _push_wait(pending, per_peer_src, after, name):
    n_arr = len(pending["srcs"])

    def body(*refs):
        src_refs, land_refs = refs[:n_arr], refs[n_arr:2 * n_arr]
        send_sems, recv_sems = refs[2 * n_arr], refs[2 * n_arr + 1]
        x, y, c = lax.axis_index("x"), lax.axis_index("y"), lax.axis_index("c")
        me = 4 * x + 2 * y + c
        for w in range(n_arr):
            for k, peer in enumerate(_peer_list(x, y, c)):
                cp = _push_copy(src_refs[w], land_refs[w], send_sems, recv_sems, w, k, peer, me, per_peer_src, True)
                cp.wait_send()
                cp.wait_recv()

    out = pl.pallas_call(
        body, name=name,
        out_shape=tuple(pltpu.HBM(a.shape, a.dtype) for a in pending["srcs"] + pending["lands"]),
        in_specs=[_HBM] * (2 * n_arr) + [_SEM, _SEM, pl.BlockSpec(memory_space=pl.ANY)],
        out_specs=tuple([_HBM] * (2 * n_arr)),
        input_output_aliases={i: i for i in range(2 * n_arr)},
        compiler_params=pltpu.CompilerParams(has_side_effects=_DATAFLOW),
    )(*pending["srcs"], *pending["lands"], pending["send"], pending["recv"], after)
    return list(out[:n_arr]), list(out[n_arr:])


def _adamw_math(w, g, m, v):
    m = ADAM_B1 * m + (1.0 - ADAM_B1) * g
    v = ADAM_B2 * v + (1.0 - ADAM_B2) * (g * g)
    m_hat = m / (1.0 - ADAM_B1 ** ADAM_STEP)
    v_hat = v / (1.0 - ADAM_B2 ** ADAM_STEP)
    delta = -ADAM_LR * (m_hat / (jnp.sqrt(v_hat) + ADAM_EPS) + ADAM_WD * w)
    return delta, m, v


def _sum8(parts, name, tr=512):
    _, r, c_dim = parts.shape
    tr = _tile(r, tr, BF16_SUBLANES)

    def body(p_ref, o_ref):
        acc = p_ref[0].astype(F32)
        for k in range(1, N_DEV):
            acc = acc + p_ref[k].astype(F32)
        o_ref[...] = acc

    return pl.pallas_call(
        body, name=name, grid=(r // tr,), in_specs=[pl.BlockSpec((N_DEV, tr, c_dim), lambda i: (0, i, 0))],
        out_specs=pl.BlockSpec((tr, c_dim), lambda i: (i, 0)),
        out_shape=jax.ShapeDtypeStruct((r, c_dim), F32), compiler_params=_params("parallel"),
    )(parts)


def _sum8_adamw(parts, w, m, v, name, tr=128):
    _, r, c_dim = parts.shape
    tr = _tile(r, tr, BF16_SUBLANES)
    tc = c_dim if tr <= 2 * LANES else _tile(c_dim, LANES, LANES)

    def body(p_ref, w_ref, m_ref, v_ref, g_ref, d_ref, nm_ref, nv_ref):
        g = p_ref[0].astype(F32)
        for k in range(1, N_DEV):
            g = g + p_ref[k].astype(F32)
        g_ref[...] = g
        d_ref[...], nm_ref[...], nv_ref[...] = _adamw_math(w_ref[...], g, m_ref[...], v_ref[...])

    blk = pl.BlockSpec((None, tr, tc), lambda i, j: (0, i, j))
    out = jax.ShapeDtypeStruct((1, r, c_dim), F32)
    return pl.pallas_call(
        body, name=name, grid=(r // tr, c_dim // tc),
        in_specs=[pl.BlockSpec((N_DEV, tr, tc), lambda i, j: (0, i, j)), blk, blk, blk],
        out_specs=[blk] * 4, out_shape=[out] * 4, compiler_params=_params("parallel", "parallel"),
    )(parts, w, m, v)


def _adamw(g, w, m, v, name):
    r, c_dim = g.shape

    def body(g_ref, w_ref, m_ref, v_ref, d_ref, nm_ref, nv_ref):
        d_ref[...], nm_ref[...], nv_ref[...] = _adamw_math(w_ref[...], g_ref[...], m_ref[...], v_ref[...])

    out = jax.ShapeDtypeStruct((r, c_dim), F32)
    return pl.pallas_call(body, name=name, out_shape=[out] * 3)(g, w, m, v)


def _pack_rows(arrays, dtype, row_unit):
    chunks, offs, r0 = [], [], 0
    for a in arrays:
        flat = a.reshape(-1).astype(dtype)
        rows = -(-flat.shape[0] // (LANES * row_unit)) * row_unit
        flat = jnp.pad(flat, (0, rows * LANES - flat.shape[0]))
        chunks.append(flat.reshape(rows, LANES))
        offs.append((r0, rows))
        r0 += rows
    return jnp.concatenate(chunks, axis=0), offs


def _unpack_rows(packed, offs, shapes):
    out = []
    for (r0, rows), shape in zip(offs, shapes):
        n = math.prod(shape)
        blk = packed[..., r0:r0 + rows, :]
        blk = blk.reshape(packed.shape[:-2] + (rows * LANES,))[..., :n]
        out.append(blk.reshape(packed.shape[:-2] + tuple(shape)))
    return out


def _full_from_slots(blk, col_sharded):
    _, r, c = blk.shape
    if col_sharded:
        return blk.transpose(1, 0, 2).reshape(r, N_DEV * c)
    return blk.reshape(N_DEV * r, c)


def _ffn_fwd(h, n, w_gu_t, w_d, tag, **tail):
    gate, up, a = _ffn_up(n, w_gu_t, f"{tag}_up")
    return _mm(a, w_d, F32, f"{tag}_down", res=h, alpha=FFN_RES_WEIGHT, **tail), (h, n, gate, up, a)


def _ffn_bwd(dh_out, saved, g, w_gu_t, w_d, tag, dep, send_grads):
    h, n, gate, up, a = saved
    dw_d = _mm_tn(a, dh_out, f"{tag}_dw_down", alpha=FFN_RES_WEIGHT, dep=dep)
    dgate, dup = _ffn_da(dh_out, w_d, gate, up, f"{tag}_da", FFN_RES_WEIGHT)
    f = dgate.shape[1]
    dw_gu_t = _mm_tn(dgate, n, f"{tag}_dw_gate", out_rows=2 * f)
    dw_gu_t = _mm_tn(dup, n, f"{tag}_dw_up", out_rows=2 * f, row_off=f, into=dw_gu_t)
    dep = send_grads(dw_gu_t, dw_d)
    return _mm([dgate, dup], w_gu_t, F32, f"{tag}_dn", dep=dep, tk=1408, res=dh_out, norm_bwd=(h, g))


W_GROUPS = (("ffn1_w_gate_up", "ffn1_w_down"),
            ("w_in",),
            ("w_out_a", "w_out_ssm", "w_mix_out"),
            ("w_q", "w_kv", "w_o_x", "ffn2_w_gate_up", "ffn2_w_down"))
G_GROUPS = (("ffn2_w_gate_up", "ffn2_w_down"),
            ("w_o_x", "w_q", "w_kv", "w_mix_out", "w_out_a", "w_out_ssm", "w_in"),
            ("ffn1_w_gate_up", "ffn1_w_down"))


def _local_step(x3, mem3, target3, small, comm):
    nb, s, d = x3.shape
    m_len = mem3.shape[1]
    t = nb * s
    nc = s // SSM_CHUNK
    di = small["ssm_norm"].shape[1]
    hs = di // SSM_HEAD_DIM
    cc = di + 2 * SSM_GROUPS * SSM_STATE
    x, mem, target = x3.reshape(t, d), mem3.reshape(nb * m_len, d), target3.reshape(t, d)

    sizes = (d, d, d, di, cc, hs, d, d)
    offs = [0]
    for sz in sizes:
        offs.append(offs[-1] + sz)
    cb_a = _tile(d, CONV_A_BLOCK, LANES)
    xbc_col0, z_col0 = 3 * d, 3 * d + cc
    ga_blk, gb_blk = (3 * d + di + cc) // d, (4 * d + di + cc) // d

    pad_vec = lambda v: jnp.pad(v.reshape(1, -1), ((0, 0), (0, LANES - hs)))
    prow = jnp.concatenate([pad_vec(small["ssm_dt_bias"]), pad_vec(small["ssm_a_log"]), pad_vec(small["ssm_d"]),
                            jnp.zeros((5, LANES), F32)], axis=0)
    pcol = prow.T
    e_mat = (lax.broadcasted_iota(jnp.int32, (LANES, di), 0)
             == lax.broadcasted_iota(jnp.int32, (LANES, di), 1) // SSM_HEAD_DIM).astype(BF16)

    wts, dep = comm.weights(0, None)
    conv_a_w8 = jnp.pad(wts["conv_a_w"][0], ((0, 8 - CONV_A_K), (0, 0)))
    ssm_conv_w8 = jnp.pad(wts["ssm_conv_w"][0], ((0, 8 - SSM_CONV_K), (0, 0)))
    n1 = _rms_fwd(x, small["ffn1_norm"] + dep[0, 0], "ffn1_norm")
    (h1, u), ffn1_saved = _ffn_fwd(x, n1, wts["ffn1_w_gate_up"], wts["ffn1_w_down"], "ffn1", norm_out=small["mix_norm"])
    got, dep = comm.weights(1, h1)
    wts.update(got)
    w_in_t = wts["w_in"]
    conv_rows = [w_in_t[offs[i] + j * cb_a:offs[i] + (j + 1) * cb_a] for j in range(d // cb_a) for i in (0, 1, 2)]
    w_main_t = jnp.concatenate(conv_rows + [w_in_t[offs[i]:offs[i + 1]] for i in (4, 3, 6, 7)], axis=0)
    w_dt_t = jnp.pad(w_in_t[offs[5]:offs[6]], ((0, LANES - hs), (0, 0)))
    proj = _mm(u, w_main_t, BF16, "in_proj", nt=True, dep=dep)
    dtr = _mm(u, w_dt_t, F32, "in_proj_dt", nt=True)
    yap = _conv_a_fwd(proj, conv_a_w8, nb, s, d, "conv_a")
    got, dep = comm.weights(2, yap)
    wts.update(got)
    y_a = _mm(yap, wts["w_out_a"], BF16, "out_a", dep=dep)
    xc, conv_pre = _conv_s_fwd(proj, xbc_col0, ssm_conv_w8, small["ssm_conv_b"] + dep[0, 0], nb, s, cc, "conv_s")
    dtrt = dtr.T
    y_ssd, sprev = _ssd_fwd(xc, dtr, dtrt, prow, pcol, nb, nc, di, "ssd")
    ygn, y_b = _gate_norm_out(y_ssd, proj, z_col0, small["ssm_norm"], wts["w_out_ssm"], di, "gate_norm_out")
    merged, h2, un = _merge_mix_out(y_a, y_b, proj, ga_blk, gb_blk, wts["w_mix_out"], h1, small["xattn_norm"], "merge_mix_out")
    got, _ = comm.weights(3, h2)
    wts.update(got)
    mn = _rms_fwd(mem, small["mem_norm"], "mem_norm")
    q = _mm(un, wts["w_q"], BF16, "q_proj")
    kv = _mm(mn, wts["w_kv"], BF16, "kv_proj", nt=True)
    o = _xattn_fwd(q, kv, nb, s, m_len, d, "xattn")
    h3, n2 = _mm(o, wts["w_o_x"], F32, "o_proj", res=h2, norm_out=small["ffn2_norm"])
    (dh4, dg_final, loss_vec), ffn2_saved = _ffn_fwd(h3, n2, wts["ffn2_w_gate_up"], wts["ffn2_w_down"], "ffn2",
                                                     loss_head=(small["final_norm"].reshape(1, d), target))

    grads = {"final_norm": dg_final.reshape(d)}
    big = {}
    dh3, grads["ffn2_norm"] = _ffn_bwd(
        dh4, ffn2_saved, small["ffn2_norm"], wts["ffn2_w_gate_up"], wts["ffn2_w_down"], "ffn2", None,
        lambda dw_gu_t, dw_d: comm.grads(0, {"ffn2_w_gate_up": dw_gu_t, "ffn2_w_down": dw_d}))
    big["w_o_x"] = _mm_tn(o, dh3, "dw_o")
    do = _mm(dh3, wts["w_o_x"], BF16, "d_o", nt=True)
    dq, dk, dv = _xattn_bwd(do, q, kv, nb, s, m_len, d, "xattn_bwd")
    big["w_q"] = _mm_tn(un, dq, "dw_q")
    big["w_kv"] = _mm_tn(dv, mn, "dw_v", out_rows=2 * d, row_off=d, into=_mm_tn(dk, mn, "dw_k", out_rows=2 * d))
    _, grads["mem_norm"] = _mm([dk, dv], wts["w_kv"], F32, "d_mn", norm_bwd=(mem, small["mem_norm"]))
    dh2, grads["xattn_norm"] = _mm(dq, wts["w_q"], F32, "d_un", nt=True, res=dh3, norm_bwd=(h2, small["xattn_norm"]))
    big["w_mix_out"] = _mm_tn(merged, dh2, "dw_mix")
    dya, dyb, dg = _merge_bwd(dh2, wts["w_mix_out"], y_a, y_b, proj, ga_blk, gb_blk, d, "merge_bwd")
    big["w_out_a"] = _mm_tn(yap, dya, "dw_out_a")
    big["w_out_ssm"] = _mm_tn(ygn, dyb, "dw_out_ssm")
    dcx, dconv_a = _conv_a_bwd(dya, wts["w_out_a"], proj, conv_a_w8, nb, s, d, 3 * d + cc, "conv_a_bwd")
    dy_ssd, dz, grads["ssm_norm"] = _gate_norm_bwd(dyb, wts["w_out_ssm"], y_ssd, proj, z_col0, small["ssm_norm"], di,
                                                   "gate_norm_bwd")
    dxc, ddtr, ssd_sums = _ssd_bwd(dy_ssd, xc, dtr, dtrt, prow, pcol, e_mat, sprev, nb, nc, di, "ssd_bwd")
    dcx, dconv_s, grads["ssm_conv_b"] = _conv_s_bwd(dxc, conv_pre, proj, xbc_col0, ssm_conv_w8, nb, s, cc, dcx, "conv_s_bwd")
    dw_cx, dw_z, dw_g = _mm_tn(dcx, u, "dw_in_cx"), _mm_tn(dz, u, "dw_in_z"), _mm_tn(dg, u, "dw_in_g")
    dw_dt = _mm_tn(ddtr, u, "dw_in_dt")[:hs]
    du_main = _mm([dcx, dz, dg], w_main_t, F32, "d_u")
    dh1, grads["mix_norm"] = _mm(ddtr, w_dt_t, F32, "d_u_dt", res=dh2, norm_bwd=(h1, small["mix_norm"], du_main))
    bcv = [[dw_cx[(3 * j + i) * cb_a:(3 * j + i + 1) * cb_a] for j in range(d // cb_a)] for i in range(3)]
    big["w_in"] = jnp.concatenate(bcv[0] + bcv[1] + bcv[2] + [dw_z, dw_cx[3 * d:], dw_dt, dw_g], axis=0)
    dep = comm.grads(1, big)
    dx, grads["ffn1_norm"] = _ffn_bwd(
        dh1, ffn1_saved, small["ffn1_norm"], wts["ffn1_w_gate_up"], wts["ffn1_w_down"], "ffn1", dep,
        lambda dw_gu_t, dw_d: comm.grads(2, {"ffn1_w_gate_up": dw_gu_t, "ffn1_w_down": dw_d}))

    grads["conv_a_w"] = dconv_a[:CONV_A_K]
    grads["ssm_conv_w"] = dconv_s[:SSM_CONV_K]
    grads["ssm_dt_bias"] = ssd_sums[0:1, :hs]
    grads["ssm_a_log"] = ssd_sums[1:2, :hs]
    grads["ssm_d"] = ssd_sums[2:3, :hs]
    return loss_vec[0, 0], dx.reshape(nb, s, d), grads


def _step(inputs):
    w = {k: inputs[k] for k in WEIGHT_ORDER}
    mom = {k: inputs["m_" + k] for k in WEIGHT_ORDER}
    vel = {k: inputs["v_" + k] for k in WEIGHT_ORDER}
    me = 4 * lax.axis_index("x") + 2 * lax.axis_index("y") + lax.axis_index("c")

    send = {k: (w[k][0].T if k in COL_SHARDED else w[k][0]).astype(BF16) for k in BIG_WEIGHTS}

    def own_slot(land, mine):
        return lax.dynamic_update_slice(land, mine[None], (me, 0, 0))

    gathers, exchanges = {}, {}

    def weights(i, after):
        conv = {}
        if i == 0:
            packed_c, conv_offs = _pack_rows([w[k][0] for k in SMALL_SHARDED], F32, 8)
            lands = list(_all_gather([send[k] for k in W_GROUPS[0]] + [packed_c], "gather0"))
            conv_blocks = _unpack_rows(lands.pop(), conv_offs, [w[k].shape[1:] for k in SMALL_SHARDED])
            conv = {k: _full_from_slots(b, True)[None] for k, b in zip(SMALL_SHARDED, conv_blocks)}
        else:
            sent, lands = _push_wait(gathers[i], False, after, f"gather{i}_wait")
            lands = [own_slot(land, mine) for land, mine in zip(lands, sent)]
        full = {k: land.reshape(N_DEV * land.shape[1], land.shape[2]) for k, land in zip(W_GROUPS[i], lands)}
        full.update(conv)
        dep = jnp.zeros((8, LANES), F32)
        if i + 1 < len(W_GROUPS):
            gathers[i + 1] = _push_start([send[k] for k in W_GROUPS[i + 1]], False, lands[0], f"gather{i + 1}_start")
            dep = gathers[i + 1]["token"]
        return full, dep

    def slot_shape(k):
        rows, cols = send[k].shape
        return (rows * cols // LANES, LANES) if rows % BF16_SUBLANES else (rows, cols)

    def send_grads(i, by_name):
        slots = [by_name[k].reshape((N_DEV,) + slot_shape(k)) for k in G_GROUPS[i]]
        exchanges[i] = _push_start(slots, True, slots[0], f"exchange{i}_start")
        return exchanges[i]["token"]

    comm = types.SimpleNamespace(weights=weights, grads=send_grads)
    small = {k: w[k] for k in SMALL_REPLICATED}

    loss_local, grad_x, grads = _local_step(inputs["x"], inputs["mem"], inputs["loss_target"], small, comm)
    loss = lax.psum(loss_local, AXES)

    out = {}
    for i, names in enumerate(G_GROUPS):
        sent, lands = _push_wait(exchanges[i], True, grad_x, f"exchange{i}_wait")
        for k, land, slots in zip(names, lands, sent):
            parts = own_slot(land, lax.dynamic_index_in_dim(slots, me, 0, keepdims=False))
            if k in COL_SHARDED and w[k].shape[2] % LANES:
                flip = lambda a: a.transpose(0, 2, 1).reshape((1,) + slot_shape(k))
                unflip = lambda a: a.reshape((1,) + send[k].shape).transpose(0, 2, 1)
                out[k] = tuple(unflip(o) for o in _sum8_adamw(parts, flip(w[k]), flip(mom[k]), flip(vel[k]),
                                                             f"sum_adamw_{k}", tr=2048 if slot_shape(k)[1] == LANES else 128))
                continue
            if k in COL_SHARDED:
                parts = parts.transpose(0, 2, 1)
            out[k] = tuple(_sum8_adamw(parts, w[k], mom[k], vel[k], f"sum_adamw_{k}"))

    small_names = SMALL_REPLICATED + SMALL_SHARDED
    packed_g, small_offs = _pack_rows([grads[k] for k in small_names], F32, 8)
    total = _sum8(_all_gather([packed_g], "gather_small_grads")[0], "sum_small_grads")
    full_grads = _unpack_rows(total, small_offs, [grads[k].shape for k in small_names])
    mine = {}
    for k, g in zip(small_names, full_grads):
        if k in SMALL_SHARDED:
            c_loc = w[k].shape[2]
            g = lax.dynamic_slice_in_dim(g, me * c_loc, c_loc, axis=1)
        mine[k] = g.reshape(w[k].shape)
    sg, s_offs = _pack_rows([mine[k] for k in small_names], F32, 8)
    sw, _ = _pack_rows([w[k] for k in small_names], F32, 8)
    sm, _ = _pack_rows([mom[k] for k in small_names], F32, 8)
    sv, _ = _pack_rows([vel[k] for k in small_names], F32, 8)
    s_shapes = [w[k].shape for k in small_names]
    small_out = [_unpack_rows(a, s_offs, s_shapes) for a in _adamw(sg, sw, sm, sv, "adamw_small")]
    for i, k in enumerate(small_names):
        out[k] = (mine[k],) + tuple(o[i] for o in small_out)

    res = [loss, grad_x]
    for j in range(4):
        res += [out[k][j] for k in WEIGHT_ORDER]
    return tuple(res)


def kernel(x, mem, ffn1_norm, ffn1_w_gate_up, ffn1_w_down, mix_norm, w_in, conv_a_w, w_out_a, ssm_conv_w, ssm_conv_b, ssm_dt_bias, ssm_a_log, ssm_d, ssm_norm, w_out_ssm, w_mix_out, xattn_norm, mem_norm, w_q, w_kv, w_o_x, ffn2_norm, ffn2_w_gate_up, ffn2_w_down, final_norm, loss_target, m_ffn1_norm, m_ffn1_w_gate_up, m_ffn1_w_down, m_mix_norm, m_w_in, m_conv_a_w, m_w_out_a, m_ssm_conv_w, m_ssm_conv_b, m_ssm_dt_bias, m_ssm_a_log, m_ssm_d, m_ssm_norm, m_w_out_ssm, m_w_mix_out, m_xattn_norm, m_mem_norm, m_w_q, m_w_kv, m_w_o_x, m_ffn2_norm, m_ffn2_w_gate_up, m_ffn2_w_down, m_final_norm, v_ffn1_norm, v_ffn1_w_gate_up, v_ffn1_w_down, v_mix_norm, v_w_in, v_conv_a_w, v_w_out_a, v_ssm_conv_w, v_ssm_conv_b, v_ssm_dt_bias, v_ssm_a_log, v_ssm_d, v_ssm_norm, v_w_out_ssm, v_w_mix_out, v_xattn_norm, v_mem_norm, v_w_q, v_w_kv, v_w_o_x, v_ffn2_norm, v_ffn2_w_gate_up, v_ffn2_w_down, v_final_norm):
    return _step(dict(locals()))
```
